```python
import jax, jax.numpy as jnp
from jax import lax
import numpy as np

D_MODEL = 1024
BATCH = 8
SEQ = 2048
DEPTH = 4

ATTN_HEADS = 8
HEAD_DIM = 64
ATTN_WIDTH = ATTN_HEADS * HEAD_DIM
CONV_WIDTH = D_MODEL - ATTN_WIDTH
CONV_KERNEL = 31
D_FF = 2816
FFN_KERNEL = 3
BLOCK_Q = 128
IN_WIDTH = 3 * ATTN_WIDTH + 2 * CONV_WIDTH
EPS = 1e-6

kernel_name = "hybrid_stickbreak_conformer_convffn"


def rms_norm(x, g):
    xf = x.astype(jnp.float32)
    y = xf * lax.rsqrt(jnp.mean(xf * xf, axis=-1, keepdims=True) + EPS)
    return (y * g.astype(jnp.float32)).astype(x.dtype)


def layer_norm(x, g, b):
    xf = x.astype(jnp.float32)
    mu = jnp.mean(xf, axis=-1, keepdims=True)
    xc = xf - mu
    var = jnp.mean(xc * xc, axis=-1, keepdims=True)
    y = xc * lax.rsqrt(var + EPS) * g.astype(jnp.float32) + b.astype(jnp.float32)
    return y.astype(x.dtype)


def causal_depthwise_conv(x, w, b):
    k_width, channels = w.shape
    y = lax.conv_general_dilated(
        x, w[:, None, :].astype(x.dtype),
        window_strides=(1,), padding=[(k_width - 1, 0)],
        dimension_numbers=("NWC", "WIO", "NWC"),
        feature_group_count=channels)
    return y + b.astype(x.dtype)


def stick_breaking_attention(q, k, v):
    seq = q.shape[1]
    scale = q.shape[-1] ** -0.5
    outs = []
    for start in range(0, seq, BLOCK_Q):
        end = min(start + BLOCK_Q, seq)
        qb = q[:, start:end].astype(jnp.float32)
        kb = k[:, :end].astype(jnp.float32)
        vb = v[:, :end].astype(jnp.float32)
        z = jnp.einsum("bqhd,bkhd->bhqk", qb, kb) * scale
        t_idx = start + jnp.arange(end - start)[:, None]
        s_idx = jnp.arange(end)[None, :]
        mask = s_idx < t_idx
        log_beta = jax.nn.log_sigmoid(z)
        log_one_minus = jnp.where(mask, jax.nn.log_sigmoid(-z), 0.0)
        tail = lax.cumsum(log_one_minus, axis=3, reverse=True) - log_one_minus
        weights = jnp.where(mask, jnp.exp(log_beta + tail), 0.0)
        outs.append(jnp.einsum("bhqk,bkhd->bqhd", weights, vb))
    return jnp.concatenate(outs, axis=1).astype(v.dtype)


def _fwd_setup_inputs(seed: int = 0) -> dict:
    key = jax.random.key(seed)
    ks = jax.random.split(key, 16)
    f32 = jnp.float32
    out_scale = (2.0 * DEPTH) ** -0.5

    def nrm(k, shape, s):
        return jax.random.normal(k, shape, f32) * s

    return {
        "x": jax.random.normal(ks[0], (BATCH, SEQ, D_MODEL), f32),
        "norm1_g": 1.0 + nrm(ks[1], (DEPTH, D_MODEL), 0.02),
        "w_in": nrm(ks[2], (DEPTH, D_MODEL, IN_WIDTH), D_MODEL ** -0.5),
        "q_norm_g": 1.0 + nrm(ks[3], (DEPTH, HEAD_DIM), 0.02),
        "k_norm_g": 1.0 + nrm(ks[4], (DEPTH, HEAD_DIM), 0.02),
        "conv_dw_w": nrm(ks[5], (DEPTH, CONV_KERNEL, CONV_WIDTH), CONV_KERNEL ** -0.5),
        "conv_dw_b": nrm(ks[6], (DEPTH, CONV_WIDTH), 0.01),
        "conv_ln_g": 1.0 + nrm(ks[7], (DEPTH, CONV_WIDTH), 0.02),
        "conv_ln_b": nrm(ks[8], (DEPTH, CONV_WIDTH), 0.01),
        "w_out": nrm(ks[9], (DEPTH, D_MODEL, D_MODEL), D_MODEL ** -0.5 * out_scale),
        "norm2_g": 1.0 + nrm(ks[10], (DEPTH, D_MODEL), 0.02),
        "w_up": nrm(ks[11], (DEPTH, D_MODEL, 2 * D_FF), D_MODEL ** -0.5),
        "ffn_dw_w": nrm(ks[12], (DEPTH, FFN_KERNEL, D_FF), FFN_KERNEL ** -0.5),
        "ffn_dw_b": nrm(ks[13], (DEPTH, D_FF), 0.01),
        "w_down": nrm(ks[14], (DEPTH, D_FF, D_MODEL), D_FF ** -0.5 * out_scale),
    }


def _fwd_reference(x, norm1_g, w_in, q_norm_g, k_norm_g, conv_dw_w, conv_dw_b,
              conv_ln_g, conv_ln_b, w_out, norm2_g, w_up, ffn_dw_w, ffn_dw_b,
              w_down):
    bsz, seq, _ = x.shape
    splits = [ATTN_WIDTH, 2 * ATTN_WIDTH, 3 * ATTN_WIDTH, 3 * ATTN_WIDTH + CONV_WIDTH]
    for layer in range(DEPTH):
        h = rms_norm(x, norm1_g[layer])
        proj = h @ w_in[layer]
        q, k, v, glu_a, glu_b = jnp.split(proj, splits, axis=-1)
        q = rms_norm(q.reshape(bsz, seq, ATTN_HEADS, HEAD_DIM), q_norm_g[layer])
        k = rms_norm(k.reshape(bsz, seq, ATTN_HEADS, HEAD_DIM), k_norm_g[layer])
        v = v.reshape(bsz, seq, ATTN_HEADS, HEAD_DIM)
        attn = stick_breaking_attention(q, k, v).reshape(bsz, seq, ATTN_WIDTH)

        c = glu_a * jax.nn.sigmoid(glu_b)
        c = causal_depthwise_conv(c, conv_dw_w[layer], conv_dw_b[layer])
        c = jax.nn.silu(layer_norm(c, conv_ln_g[layer], conv_ln_b[layer]))

        mixed = jnp.concatenate([attn, c], axis=-1) @ w_out[layer]
        x = x + mixed

        h = rms_norm(x, norm2_g[layer])
        gate, val = jnp.split(h @ w_up[layer], 2, axis=-1)
        gate = jax.nn.silu(causal_depthwise_conv(gate, ffn_dw_w[layer], ffn_dw_b[layer]))
        x = x + (gate * val) @ w_down[layer]
    return x


import jax as _jax
import jax.numpy as _jnp

TWIN_FORMAT = 'train_step'
FWD_PARAMS = ['x', 'norm1_g', 'w_in', 'q_norm_g', 'k_norm_g', 'conv_dw_w', 'conv_dw_b', 'conv_ln_g', 'conv_ln_b', 'w_out', 'norm2_g', 'w_up', 'ffn_dw_w', 'ffn_dw_b', 'w_down']
TWIN_WEIGHTS = ['norm1_g', 'w_in', 'q_norm_g', 'k_norm_g', 'conv_dw_w', 'conv_dw_b', 'conv_ln_g', 'conv_ln_b', 'w_out', 'norm2_g', 'w_up', 'ffn_dw_w', 'ffn_dw_b', 'w_down']
TWIN_DIFF_INPUT = 'x'
TWIN_INPUTS = ['x', 'norm1_g', 'w_in', 'q_norm_g', 'k_norm_g', 'conv_dw_w', 'conv_dw_b', 'conv_ln_g', 'conv_ln_b', 'w_out', 'norm2_g', 'w_up', 'ffn_dw_w', 'ffn_dw_b', 'w_down', 'loss_target', 'm_norm1_g', 'm_w_in', 'm_q_norm_g', 'm_k_norm_g', 'm_conv_dw_w', 'm_conv_dw_b', 'm_conv_ln_g', 'm_conv_ln_b', 'm_w_out', 'm_norm2_g', 'm_w_up', 'm_ffn_dw_w', 'm_ffn_dw_b', 'm_w_down', 'v_norm1_g', 'v_w_in', 'v_q_norm_g', 'v_k_norm_g', 'v_conv_dw_w', 'v_conv_dw_b', 'v_conv_ln_g', 'v_conv_ln_b', 'v_w_out', 'v_norm2_g', 'v_w_up', 'v_ffn_dw_w', 'v_ffn_dw_b', 'v_w_down']
TWIN_OUTPUTS = ['loss', 'grad_x', 'grad_norm1_g', 'grad_w_in', 'grad_q_norm_g', 'grad_k_norm_g', 'grad_conv_dw_w', 'grad_conv_dw_b', 'grad_conv_ln_g', 'grad_conv_ln_b', 'grad_w_out', 'grad_norm2_g', 'grad_w_up', 'grad_ffn_dw_w', 'grad_ffn_dw_b', 'grad_w_down', 'delta_norm1_g', 'delta_w_in', 'delta_q_norm_g', 'delta_k_norm_g', 'delta_conv_dw_w', 'delta_conv_dw_b', 'delta_conv_ln_g', 'delta_conv_ln_b', 'delta_w_out', 'delta_norm2_g', 'delta_w_up', 'delta_ffn_dw_w', 'delta_ffn_dw_b', 'delta_w_down', 'new_m_norm1_g', 'new_m_w_in', 'new_m_q_norm_g', 'new_m_k_norm_g', 'new_m_conv_dw_w', 'new_m_conv_dw_b', 'new_m_conv_ln_g', 'new_m_conv_ln_b', 'new_m_w_out', 'new_m_norm2_g', 'new_m_w_up', 'new_m_ffn_dw_w', 'new_m_ffn_dw_b', 'new_m_w_down', 'new_v_norm1_g', 'new_v_w_in', 'new_v_q_norm_g', 'new_v_k_norm_g', 'new_v_conv_dw_w', 'new_v_conv_dw_b', 'new_v_conv_ln_g', 'new_v_conv_ln_b', 'new_v_w_out', 'new_v_norm2_g', 'new_v_w_up', 'new_v_ffn_dw_w', 'new_v_ffn_dw_b', 'new_v_w_down']
TWIN_LEAF_KINDS = {'loss': 'loss', 'grad_x': 'grad_x', 'grad_norm1_g': 'grad_w', 'grad_w_in': 'grad_w', 'grad_q_norm_g': 'grad_w', 'grad_k_norm_g': 'grad_w', 'grad_conv_dw_w': 'grad_w', 'grad_conv_dw_b': 'grad_w', 'grad_conv_ln_g': 'grad_w', 'grad_conv_ln_b': 'grad_w', 'grad_w_out': 'grad_w', 'grad_norm2_g': 'grad_w', 'grad_w_up': 'grad_w', 'grad_ffn_dw_w': 'grad_w', 'grad_ffn_dw_b': 'grad_w', 'grad_w_down': 'grad_w', 'delta_norm1_g': 'delta_w', 'delta_w_in': 'delta_w', 'delta_q_norm_g': 'delta_w', 'delta_k_norm_g': 'delta_w', 'delta_conv_dw_w': 'delta_w', 'delta_conv_dw_b': 'delta_w', 'delta_conv_ln_g': 'delta_w', 'delta_conv_ln_b': 'delta_w', 'delta_w_out': 'delta_w', 'delta_norm2_g': 'delta_w', 'delta_w_up': 'delta_w', 'delta_ffn_dw_w': 'delta_w', 'delta_ffn_dw_b': 'delta_w', 'delta_w_down': 'delta_w', 'new_m_norm1_g': 'new_m', 'new_m_w_in': 'new_m', 'new_m_q_norm_g': 'new_m', 'new_m_k_norm_g': 'new_m', 'new_m_conv_dw_w': 'new_m', 'new_m_conv_dw_b': 'new_m', 'new_m_conv_ln_g': 'new_m', 'new_m_conv_ln_b': 'new_m', 'new_m_w_out': 'new_m', 'new_m_norm2_g': 'new_m', 'new_m_w_up': 'new_m', 'new_m_ffn_dw_w': 'new_m', 'new_m_ffn_dw_b': 'new_m', 'new_m_w_down': 'new_m', 'new_v_norm1_g': 'new_v', 'new_v_w_in': 'new_v', 'new_v_q_norm_g': 'new_v', 'new_v_k_norm_g': 'new_v', 'new_v_conv_dw_w': 'new_v', 'new_v_conv_dw_b': 'new_v', 'new_v_conv_ln_g': 'new_v', 'new_v_conv_ln_b': 'new_v', 'new_v_w_out': 'new_v', 'new_v_norm2_g': 'new_v', 'new_v_w_up': 'new_v', 'new_v_ffn_dw_w': 'new_v', 'new_v_ffn_dw_b': 'new_v', 'new_v_w_down': 'new_v'}


def _forward(args):
    return _fwd_reference(*[args[k] for k in FWD_PARAMS])


def _output_shape():
    out = _jax.eval_shape(lambda: _forward(_fwd_setup_inputs(0)))
    return out.shape, out.dtype

N_MICROBATCH = 1
ADAM_LR = 0.001
ADAM_B1 = 0.9
ADAM_B2 = 0.999
ADAM_EPS = 1e-08
ADAM_WD = 0.01
ADAM_STEP = 10
PER_EXAMPLE_BATCH_AXIS = {'x': 0, 'loss_target': 0}
SHARED_INPUTS = []
_WEIGHT_DTYPES = {'norm1_g': _jnp.float32, 'w_in': _jnp.float32, 'q_norm_g': _jnp.float32, 'k_norm_g': _jnp.float32, 'conv_dw_w': _jnp.float32, 'conv_dw_b': _jnp.float32, 'conv_ln_g': _jnp.float32, 'conv_ln_b': _jnp.float32, 'w_out': _jnp.float32, 'norm2_g': _jnp.float32, 'w_up': _jnp.float32, 'ffn_dw_w': _jnp.float32, 'ffn_dw_b': _jnp.float32, 'w_down': _jnp.float32}
MOMENT_SCALE = {'norm1_g': 4.053447e-01, 'w_in': 4.278684e-02, 'q_norm_g': 9.752701e-01, 'k_norm_g': 9.746289e-01, 'conv_dw_w': 6.210743e-02, 'conv_dw_b': 6.078560e-01, 'conv_ln_g': 8.966312e-01, 'conv_ln_b': 6.470798e-01, 'w_out': 2.896040e-01, 'norm2_g': 1.610498e+00, 'w_up': 3.659027e-02, 'ffn_dw_w': 1.736933e-01, 'ffn_dw_b': 2.178176e-01, 'w_down': 1.685574e-01}


def _to_microbatches(a, axis):
    t = _jnp.moveaxis(a, axis, 0)
    t = t.reshape((N_MICROBATCH, t.shape[0] // N_MICROBATCH) + t.shape[1:])
    return _jnp.moveaxis(t, 1, axis + 1)


def setup_inputs(seed: int = 0) -> dict:
    inp = _fwd_setup_inputs(seed)
    key = _jax.random.fold_in(_jax.random.key(seed), 7919)
    shape, _ = _output_shape()
    out = dict(inp)
    out["loss_target"] = _jax.random.normal(_jax.random.fold_in(key, 0), shape, _jnp.float32)
    for i, name in enumerate(TWIN_WEIGHTS):
        w = inp[name].astype(_jnp.float32)
        if MOMENT_SCALE is None:
            s = _jnp.sqrt(_jnp.mean(_jnp.square(w)) + 1e-30)
        else:
            s = MOMENT_SCALE[name]
        km, kv = _jax.random.split(_jax.random.fold_in(key, i + 1))
        out[name] = w
        out["m_" + name] = s * _jax.random.normal(km, w.shape, _jnp.float32)
        out["v_" + name] = (s * s) * _jax.random.uniform(kv, w.shape, _jnp.float32, 0.5, 1.5)
    if N_MICROBATCH > 1:
        for name, axis in PER_EXAMPLE_BATCH_AXIS.items():
            out[name] = _to_microbatches(out[name], axis)
    return {'x': out['x'], 'norm1_g': out['norm1_g'], 'w_in': out['w_in'], 'q_norm_g': out['q_norm_g'], 'k_norm_g': out['k_norm_g'], 'conv_dw_w': out['conv_dw_w'], 'conv_dw_b': out['conv_dw_b'], 'conv_ln_g': out['conv_ln_g'], 'conv_ln_b': out['conv_ln_b'], 'w_out': out['w_out'], 'norm2_g': out['norm2_g'], 'w_up': out['w_up'], 'ffn_dw_w': out['ffn_dw_w'], 'ffn_dw_b': out['ffn_dw_b'], 'w_down': out['w_down'], 'loss_target': out['loss_target'], 'm_norm1_g': out['m_norm1_g'], 'm_w_in': out['m_w_in'], 'm_q_norm_g': out['m_q_norm_g'], 'm_k_norm_g': out['m_k_norm_g'], 'm_conv_dw_w': out['m_conv_dw_w'], 'm_conv_dw_b': out['m_conv_dw_b'], 'm_conv_ln_g': out['m_conv_ln_g'], 'm_conv_ln_b': out['m_conv_ln_b'], 'm_w_out': out['m_w_out'], 'm_norm2_g': out['m_norm2_g'], 'm_w_up': out['m_w_up'], 'm_ffn_dw_w': out['m_ffn_dw_w'], 'm_ffn_dw_b': out['m_ffn_dw_b'], 'm_w_down': out['m_w_down'], 'v_norm1_g': out['v_norm1_g'], 'v_w_in': out['v_w_in'], 'v_q_norm_g': out['v_q_norm_g'], 'v_k_norm_g': out['v_k_norm_g'], 'v_conv_dw_w': out['v_conv_dw_w'], 'v_conv_dw_b': out['v_conv_dw_b'], 'v_conv_ln_g': out['v_conv_ln_g'], 'v_conv_ln_b': out['v_conv_ln_b'], 'v_w_out': out['v_w_out'], 'v_norm2_g': out['v_norm2_g'], 'v_w_up': out['v_w_up'], 'v_ffn_dw_w': out['v_ffn_dw_w'], 'v_ffn_dw_b': out['v_ffn_dw_b'], 'v_w_down': out['v_w_down']}


def _loss(weights, diff, rest, loss_target):
    with _jax.named_scope("forward"):
        args = {**rest, TWIN_DIFF_INPUT: diff, **{k: w.astype(_WEIGHT_DTYPES[k]) for k, w in weights.items()}}
        y = _forward(args)
    with _jax.named_scope("loss_head"):
        err = _jnp.square(y.astype(_jnp.float32) - loss_target)
        return 0.5 * _jnp.sum(_jnp.mean(err, axis=-1)) if err.ndim else 0.5 * err


def _adamw(w, g, m, v):
    m = ADAM_B1 * m + (1.0 - ADAM_B1) * g
    v = ADAM_B2 * v + (1.0 - ADAM_B2) * _jnp.square(g)
    m_hat = m / (1.0 - ADAM_B1 ** ADAM_STEP)
    v_hat = v / (1.0 - ADAM_B2 ** ADAM_STEP)
    delta = -ADAM_LR * (m_hat / (_jnp.sqrt(v_hat) + ADAM_EPS) + ADAM_WD * w)
    return delta, m, v


def reference(x, norm1_g, w_in, q_norm_g, k_norm_g, conv_dw_w, conv_dw_b, conv_ln_g, conv_ln_b, w_out, norm2_g, w_up, ffn_dw_w, ffn_dw_b, w_down, loss_target, m_norm1_g, m_w_in, m_q_norm_g, m_k_norm_g, m_conv_dw_w, m_conv_dw_b, m_conv_ln_g, m_conv_ln_b, m_w_out, m_norm2_g, m_w_up, m_ffn_dw_w, m_ffn_dw_b, m_w_down, v_norm1_g, v_w_in, v_q_norm_g, v_k_norm_g, v_conv_dw_w, v_conv_dw_b, v_conv_ln_g, v_conv_ln_b, v_w_out, v_norm2_g, v_w_up, v_ffn_dw_w, v_ffn_dw_b, v_w_down):
    given = dict(x=x, norm1_g=norm1_g, w_in=w_in, q_norm_g=q_norm_g, k_norm_g=k_norm_g, conv_dw_w=conv_dw_w, conv_dw_b=conv_dw_b, conv_ln_g=conv_ln_g, conv_ln_b=conv_ln_b, w_out=w_out, norm2_g=norm2_g, w_up=w_up, ffn_dw_w=ffn_dw_w, ffn_dw_b=ffn_dw_b, w_down=w_down, loss_target=loss_target, m_norm1_g=m_norm1_g, m_w_in=m_w_in, m_q_norm_g=m_q_norm_g, m_k_norm_g=m_k_norm_g, m_conv_dw_w=m_conv_dw_w, m_conv_dw_b=m_conv_dw_b, m_conv_ln_g=m_conv_ln_g, m_conv_ln_b=m_conv_ln_b, m_w_out=m_w_out, m_norm2_g=m_norm2_g, m_w_up=m_w_up, m_ffn_dw_w=m_ffn_dw_w, m_ffn_dw_b=m_ffn_dw_b, m_w_down=m_w_down, v_norm1_g=v_norm1_g, v_w_in=v_w_in, v_q_norm_g=v_q_norm_g, v_k_norm_g=v_k_norm_g, v_conv_dw_w=v_conv_dw_w, v_conv_dw_b=v_conv_dw_b, v_conv_ln_g=v_conv_ln_g, v_conv_ln_b=v_conv_ln_b, v_w_out=v_w_out, v_norm2_g=v_norm2_g, v_w_up=v_w_up, v_ffn_dw_w=v_ffn_dw_w, v_ffn_dw_b=v_ffn_dw_b, v_w_down=v_w_down)
    weights = {n: given[n] for n in TWIN_WEIGHTS}
    shared = {n: given[n] for n in SHARED_INPUTS}
    per_example = {n: given[n] for n in ['x']}
    grad_fn = _jax.value_and_grad(_loss, argnums=(0, 1))

    def one_microbatch(ex, loss_target):
        ex = dict(ex)
        diff = ex.pop(TWIN_DIFF_INPUT)
        return grad_fn(weights, diff, {**shared, **ex}, loss_target)

    if N_MICROBATCH == 1:
        loss, (grad_w, grad_x) = one_microbatch(per_example, given["loss_target"])
    else:
        def body(carry, xs):
            loss_sum, grad_sum = carry
            l_k, (gw_k, gx_k) = one_microbatch(xs[0], xs[1])
            with _jax.named_scope("update"):
                return (loss_sum + l_k, _jax.tree.map(_jnp.add, grad_sum, gw_k)), gx_k

        init = (_jnp.zeros((), _jnp.float32), _jax.tree.map(_jnp.zeros_like, weights))
        (loss, grad_w), grad_x = _jax.lax.scan(body, init, (per_example, given["loss_target"]))
    with _jax.named_scope("update"):
        delta_w, new_m, new_v = {}, {}, {}
        for n in TWIN_WEIGHTS:
            delta_w[n], new_m[n], new_v[n] = _adamw(weights[n], grad_w[n], given["m_" + n], given["v_" + n])
    return (loss, grad_x, *[grad_w[n] for n in TWIN_WEIGHTS], *[delta_w[n] for n in TWIN_WEIGHTS],
            *[new_m[n] for n in TWIN_WEIGHTS], *[new_v[n] for n in TWIN_WEIGHTS])
```

```python
import functools

import jax
import jax.numpy as jnp
from jax import lax
from jax.experimental import pallas as pl
from jax.experimental.pallas import tpu as pltpu

F32 = jnp.float32
BF16 = jnp.bfloat16

N_DEV = 8
HEADS = 8
HEAD_DIM = 64
ATTN_WIDTH = HEADS * HEAD_DIM
CONV_KERNEL = 31
FFN_KERNEL = 3
EPS = 1e-6
BLK = 128
LANES = 128
CONV_HALO = 32
FFN_HALO = 8

ADAM_LR = 0.001
ADAM_B1 = 0.9
ADAM_B2 = 0.999
ADAM_EPS = 1e-08
ADAM_WD = 0.01
ADAM_STEP = 10

VMEM_LIMIT = 56 * 1024 * 1024


def _params(n_axes=0):
    kw = dict(vmem_limit_bytes=VMEM_LIMIT)
    if n_axes:
        kw["dimension_semantics"] = ("arbitrary",) * n_axes
    return pltpu.CompilerParams(**kw)


def _dot(a, b):
    return jnp.dot(a, b, preferred_element_type=F32)


def _dot_nt(a, b):
    return lax.dot_general(a, b, (((1,), (1,)), ((), ())), preferred_element_type=F32)


def _dot_tn(a, b):
    return lax.dot_general(a, b, (((0,), (0,)), ((), ())), preferred_element_type=F32)


def _sigmoid(x):
    return 1.0 / (1.0 + jnp.exp(-x))


def _split_bf16(x):
    hi = x.astype(BF16)
    lo = (x - hi.astype(F32)).astype(BF16)
    return hi, lo


def _pick(n, options):
    for t in options:
        if n % t == 0:
            return t
    return n


def _mm_rms(x, g, w, name):
    m, k = x.shape
    n = w.shape[1]
    tm = _pick(m, (512, 256, 128))
    tn = _pick(n, (512, 256, 128))

    def body(x_ref, g_ref, w_ref, o_ref, h_ref, h_s):
        @pl.when(pl.program_id(1) == 0)
        def _():
            xv = x_ref[...]
            r = lax.rsqrt(jnp.mean(xv * xv, axis=-1, keepdims=True) + EPS)
            hv = (xv * r * g_ref[...]).astype(BF16)
            h_s[...] = hv
            h_ref[...] = hv

        o_ref[...] = _dot(h_s[...], w_ref[...])

    return pl.pallas_call(
        body, name=name, grid=(m // tm, n // tn),
        in_specs=[pl.BlockSpec((tm, k), lambda i, j: (i, 0)),
                  pl.BlockSpec((1, k), lambda i, j: (0, 0)),
                  pl.BlockSpec((k, tn), lambda i, j: (0, j))],
        out_specs=[pl.BlockSpec((tm, tn), lambda i, j: (i, j)),
                   pl.BlockSpec((tm, k), lambda i, j: (i, 0))],
        out_shape=[jax.ShapeDtypeStruct((m, n), F32), jax.ShapeDtypeStruct((m, k), BF16)],
        scratch_shapes=[pltpu.VMEM((tm, k), BF16)],
        compiler_params=_params(2),
    )(x, g, w)


def _mm_res(a, w, res, name):
    m, k = a.shape
    n = w.shape[1]
    tm = _pick(m, (512, 256, 128))
    tn = _pick(n, (512, 256, 128))

    def body(a_ref, w_ref, r_ref, o_ref):
        o_ref[...] = r_ref[...] + _dot(a_ref[...], w_ref[...])

    return pl.pallas_call(
        body, name=name, grid=(m // tm, n // tn),
        in_specs=[pl.BlockSpec((tm, k), lambda i, j: (i, 0)),
                  pl.BlockSpec((k, tn), lambda i, j: (0, j)),
                  pl.BlockSpec((tm, tn), lambda i, j: (i, j))],
        out_specs=pl.BlockSpec((tm, tn), lambda i, j: (i, j)),
        out_shape=jax.ShapeDtypeStruct((m, n), F32),
        compiler_params=_params(2),
    )(a, w, res)


def _mm_nt(a, w, name):
    m, k = a.shape
    n = w.shape[0]
    tm = _pick(m, (512, 256, 128))
    tn = _pick(n, (512, 256, 128))

    def body(a_ref, w_ref, o_ref):
        o_ref[...] = _dot_nt(a_ref[...], w_ref[...])

    return pl.pallas_call(
        body, name=name, grid=(m // tm, n // tn),
        in_specs=[pl.BlockSpec((tm, k), lambda i, j: (i, 0)),
                  pl.BlockSpec((tn, k), lambda i, j: (j, 0))],
        out_specs=pl.BlockSpec((tm, tn), lambda i, j: (i, j)),
        out_shape=jax.ShapeDtypeStruct((m, n), F32),
        compiler_params=_params(2),
    )(a, w)


def _mm_tn(a, b, name):
    s, m = a.shape
    n = b.shape[1]
    tm = _pick(m, (512, 256, 128))
    tn = _pick(n, (512, 256, 128))

    def body(a_ref, b_ref, o_ref):
        o_ref[...] = _dot_tn(a_ref[...], b_ref[...]).astype(BF16)

    return pl.pallas_call(
        body, name=name, grid=(m // tm, n // tn),
        in_specs=[pl.BlockSpec((s, tm), lambda i, j: (0, i)),
                  pl.BlockSpec((s, tn), lambda i, j: (0, j))],
        out_specs=pl.BlockSpec((tm, tn), lambda i, j: (i, j)),
        out_shape=jax.ShapeDtypeStruct((m, n), BF16),
        compiler_params=_params(2),
    )(a, b)


def _mm_nt_rmsbwd(a, w, x, g, dres, name):
    m, k = a.shape
    n = w.shape[0]
    tm = _pick(m, (256, 128))
    tk = _pick(k, (512, 256, 128))
    nk = k // tk

    def body(a_ref, w_ref, x_ref, g_ref, r_ref, dx_ref, dxb_ref, dg_ref, acc):
        i, kk = pl.program_id(0), pl.program_id(1)
        part = _dot_nt(a_ref[...], w_ref[...])

        @pl.when(kk == 0)
        def _():
            acc[...] = part

        @pl.when(kk > 0)
        def _():
            acc[...] += part

        @pl.when(kk == nk - 1)
        def _():
            dh = acc[...]
            xv = x_ref[...]
            r = lax.rsqrt(jnp.mean(xv * xv, axis=-1, keepdims=True) + EPS)
            xh = xv * r
            dgp = jnp.sum(dh * xh, axis=0, keepdims=True)

            @pl.when(i == 0)
            def _():
                dg_ref[...] = dgp

            @pl.when(i > 0)
            def _():
                dg_ref[...] += dgp

            dxh = dh * g_ref[...]
            dx = r_ref[...] + r * (dxh - xh * jnp.mean(dxh * xh, axis=-1, keepdims=True))
            dx_ref[...] = dx
            dxb_ref[...] = dx.astype(BF16)

    return pl.pallas_call(
        body, name=name, grid=(m // tm, nk),
        in_specs=[pl.BlockSpec((tm, tk), lambda i, kk: (i, kk)),
                  pl.BlockSpec((n, tk), lambda i, kk: (0, kk)),
                  pl.BlockSpec((tm, n), lambda i, kk: (i, 0)),
                  pl.BlockSpec((1, n), lambda i, kk: (0, 0)),
                  pl.BlockSpec((tm, n), lambda i, kk: (i, 0))],
        out_specs=[pl.BlockSpec((tm, n), lambda i, kk: (i, 0)),
                   pl.BlockSpec((tm, n), lambda i, kk: (i, 0)),
                   pl.BlockSpec((1, n), lambda i, kk: (0, 0))],
        out_shape=[jax.ShapeDtypeStruct((m, n), F32), jax.ShapeDtypeStruct((m, n), BF16),
                   jax.ShapeDtypeStruct((1, n), F32)],
        scratch_shapes=[pltpu.VMEM((tm, n), F32)],
        compiler_params=_params(2),
    )(a, w, x, g, dres)


def _tri(kind):
    j = lax.broadcasted_iota(jnp.int32, (BLK, BLK), 0)
    s = lax.broadcasted_iota(jnp.int32, (BLK, BLK), 1)
    m = {"after": j > s, "upto": j <= s, "before": j < s}[kind]
    return jnp.concatenate([jnp.where(m, 1.0, 0.0), jnp.ones((BLK, BLK), F32)], axis=1).astype(BF16)


def _scan_rows(v, tri):
    hi, lo = _split_bf16(v)
    r = _dot(hi, tri) + _dot(lo, tri)
    return r[:, :BLK], r[:, BLK:]


def _head_norm(v, g):
    r = lax.rsqrt(jnp.mean(v * v, axis=-1, keepdims=True) + EPS)
    return v * r * g


def _attn_prep(q_ref, k_ref, v_ref, qg_ref, kg_ref, sl, qh_s, ql_s, kh_s, kl_s, vb_s, n_blk):
    scale = HEAD_DIM ** -0.5

    def prep(i, _):
        rows = pl.ds(pl.multiple_of(i * BLK, BLK), BLK)
        qn = _head_norm(q_ref[rows, sl], qg_ref[...]) * scale
        kn = _head_norm(k_ref[rows, sl], kg_ref[...])
        qh_s[rows, :], ql_s[rows, :] = _split_bf16(qn)
        kh_s[rows, :], kl_s[rows, :] = _split_bf16(kn)
        vb_s[rows, :] = v_ref[rows, sl].astype(BF16)
        return 0

    lax.fori_loop(0, n_blk, prep, 0)


def _scores(qh, ql, kh, kl):
    return _dot_nt(qh, kh) + _dot_nt(qh, kl) + _dot_nt(ql, kh)


def _strict_lower():
    row = lax.broadcasted_iota(jnp.int32, (BLK, BLK), 0)
    col = lax.broadcasted_iota(jnp.int32, (BLK, BLK), 1)
    return col < row


def _attn_fwd(proj, qg, kg, name):
    s = proj.shape[0]
    n_blk = s // BLK
    pairs = ATTN_WIDTH // LANES

    def body(q_ref, k_ref, v_ref, qg_ref, kg_ref, o_ref, t_ref, qh_s, ql_s, kh_s, kl_s, vb_s):
        tri = _tri("after")
        mask = _strict_lower()
        for hh in range(LANES // HEAD_DIM):
            sl = slice(hh * HEAD_DIM, (hh + 1) * HEAD_DIM)
            _attn_prep(q_ref, k_ref, v_ref, qg_ref, kg_ref, sl, qh_s, ql_s, kh_s, kl_s, vb_s, n_blk)

            def tile(qh, ql, kb, carry, acc, diag):
                cols = pl.ds(pl.multiple_of(kb * BLK, BLK), BLK)
                z = _scores(qh, ql, kh_s[cols, :], kl_s[cols, :])
                sp = jnp.maximum(z, 0.0) + jnp.log(1.0 + jnp.exp(-jnp.abs(z)))
                lom = -sp
                if diag:
                    lom = jnp.where(mask, lom, 0.0)
                tail, tot = _scan_rows(lom, tri)
                w = jnp.exp(z - sp + tail + carry)
                if diag:
                    w = jnp.where(mask, w, 0.0)
                acc = acc + _dot(w.astype(BF16), vb_s[cols, :])
                return carry + tot, acc

            def q_block(qi, _):
                rows = pl.ds(pl.multiple_of(qi * BLK, BLK), BLK)
                qh, ql = qh_s[rows, :], ql_s[rows, :]
                carry, acc = tile(qh, ql, qi, jnp.zeros((BLK, BLK), F32), jnp.zeros((BLK, HEAD_DIM), F32), True)
                carry, acc = lax.fori_loop(
                    0, qi, lambda t, c: tile(qh, ql, qi - 1 - t, c[0], c[1], False), (carry, acc))
                o_ref[rows, sl] = acc.astype(BF16)
                t_ref[rows, sl] = carry[:, :HEAD_DIM]
                return 0

            lax.fori_loop(0, n_blk, q_block, 0)

    col = lambda off: pl.BlockSpec((s, LANES), lambda p: (0, off + p))
    vec = pl.BlockSpec((1, HEAD_DIM), lambda p: (0, 0))
    return pl.pallas_call(
        body, name=name, grid=(pairs,),
        in_specs=[col(0), col(pairs), col(2 * pairs), vec, vec],
        out_specs=[pl.BlockSpec((s, LANES), lambda p: (0, p))] * 2,
        out_shape=[jax.ShapeDtypeStruct((s, ATTN_WIDTH), BF16), jax.ShapeDtypeStruct((s, ATTN_WIDTH), F32)],
        scratch_shapes=[pltpu.VMEM((s, HEAD_DIM), BF16)] * 5,
        compiler_params=_params(1),
    )(proj, proj, proj, qg, kg)


def _attn_bwd(proj, dcat, tsum, qg, kg, name):
    s = proj.shape[0]
    n_blk = s // BLK
    pairs = ATTN_WIDTH // LANES
    scale = HEAD_DIM ** -0.5

    def norm_bwd(raw, g, dn):
        r = lax.rsqrt(jnp.mean(raw * raw, axis=-1, keepdims=True) + EPS)
        xh = raw * r
        dg = jnp.sum(dn * xh, axis=0, keepdims=True)
        dxh = dn * g
        return r * (dxh - xh * jnp.mean(dxh * xh, axis=-1, keepdims=True)), dg

    def body(q_ref, k_ref, v_ref, do_ref, t_ref, qg_ref, kg_ref, dq_ref, dk_ref, dv_ref, dqg_ref, dkg_ref,
             qh_s, ql_s, kh_s, kl_s, vb_s, dob_s, dk_s, dv_s):
        tri_p = _tri("upto")
        tri_h = _tri("before")
        mask = _strict_lower()

        @pl.when(pl.program_id(0) == 0)
        def _():
            dqg_ref[...] = jnp.zeros_like(dqg_ref)
            dkg_ref[...] = jnp.zeros_like(dkg_ref)

        for hh in range(LANES // HEAD_DIM):
            sl = slice(hh * HEAD_DIM, (hh + 1) * HEAD_DIM)
            _attn_prep(q_ref, k_ref, v_ref, qg_ref, kg_ref, sl, qh_s, ql_s, kh_s, kl_s, vb_s, n_blk)

            def prep(i, _):
                rows = pl.ds(pl.multiple_of(i * BLK, BLK), BLK)
                dob_s[rows, :] = do_ref[rows, sl].astype(BF16)
                dk_s[rows, :] = jnp.zeros((BLK, HEAD_DIM), F32)
                dv_s[rows, :] = jnp.zeros((BLK, HEAD_DIM), F32)
                return 0

            lax.fori_loop(0, n_blk, prep, 0)

            def tile(qh, ql, dob, tq, kb, pc, hc, dq, diag):
                cols = pl.ds(pl.multiple_of(kb * BLK, BLK), BLK)
                kh = kh_s[cols, :]
                z = _scores(qh, ql, kh, kl_s[cols, :])
                sp = jnp.maximum(z, 0.0) + jnp.log(1.0 + jnp.exp(-jnp.abs(z)))
                lom = -sp
                if diag:
                    lom = jnp.where(mask, lom, 0.0)
                lb = z - sp
                p_in, p_tot = _scan_rows(lom, tri_p)
                a = jnp.exp(lb + (tq - pc - p_in))
                if diag:
                    a = jnp.where(mask, a, 0.0)
                gw = _dot_nt(dob, vb_s[cols, :]) * a
                h_in, g_tot = _scan_rows(gw, tri_h)
                dz = gw - jnp.exp(lb) * (gw + hc + h_in)
                if diag:
                    dz = jnp.where(mask, dz, 0.0)
                dzb = dz.astype(BF16)
                dq = dq + _dot(dzb, kh)
                dk_s[cols, :] += _dot_tn(dzb, qh)
                dv_s[cols, :] += _dot_tn(a.astype(BF16), dob)
                return pc + p_tot, hc + g_tot, dq

            def q_block(qi, dqg):
                rows = pl.ds(pl.multiple_of(qi * BLK, BLK), BLK)
                qh, ql, dob = qh_s[rows, :], ql_s[rows, :], dob_s[rows, :]
                th = t_ref[rows, sl]
                tq = jnp.concatenate([th, th], axis=1)
                zero = jnp.zeros((BLK, BLK), F32)
                pc, hc, dq = lax.fori_loop(
                    0, qi, lambda kb, c: tile(qh, ql, dob, tq, kb, c[0], c[1], c[2], False),
                    (zero, zero, jnp.zeros((BLK, HEAD_DIM), F32)))
                _, _, dq = tile(qh, ql, dob, tq, qi, pc, hc, dq, True)
                dq_raw, dg = norm_bwd(q_ref[rows, sl], qg_ref[...], dq * scale)
                dq_ref[rows, sl] = dq_raw.astype(BF16)
                return dqg + dg

            dqg = lax.fori_loop(0, n_blk, q_block, jnp.zeros((1, HEAD_DIM), F32))

            def finish(i, dkg):
                rows = pl.ds(pl.multiple_of(i * BLK, BLK), BLK)
                dk_raw, dg = norm_bwd(k_ref[rows, sl], kg_ref[...], dk_s[rows, :])
                dk_ref[rows, sl] = dk_raw.astype(BF16)
                dv_ref[rows, sl] = dv_s[rows, :].astype(BF16)
                return dkg + dg

            dkg = lax.fori_loop(0, n_blk, finish, jnp.zeros((1, HEAD_DIM), F32))
            dqg_ref[0:1, 0:HEAD_DIM] += dqg
            dkg_ref[0:1, 0:HEAD_DIM] += dkg

    col = lambda off: pl.BlockSpec((s, LANES), lambda p: (0, off + p))
    vec = pl.BlockSpec((1, HEAD_DIM), lambda p: (0, 0))
    small = pl.BlockSpec((8, LANES), lambda p: (0, 0))
    return pl.pallas_call(
        body, name=name, grid=(pairs,),
        in_specs=[col(0), col(pairs), col(2 * pairs), col(0), col(0), vec, vec],
        out_specs=[col(0)] * 3 + [small] * 2,
        out_shape=[jax.ShapeDtypeStruct((s, ATTN_WIDTH), BF16)] * 3 + [jax.ShapeDtypeStruct((8, LANES), F32)] * 2,
        scratch_shapes=[pltpu.VMEM((s, HEAD_DIM), BF16)] * 6 + [pltpu.VMEM((s, HEAD_DIM), F32)] * 2,
        compiler_params=_params(1),
    )(proj, proj, proj, dcat, tsum, qg, kg)


CONV_ROWS = 128


def _shifted(window, shift, halo):
    if shift == 0:
        return window[halo:, :]
    return pltpu.roll(window, shift, 0)[halo:, :]


def _conv_taps(u_s, r0, w_ref, rows):
    window = u_s[pl.ds(r0, CONV_HALO + rows), :]
    y = None
    for k in range(CONV_KERNEL):
        term = _shifted(window, CONV_KERNEL - 1 - k, CONV_HALO) * w_ref[k:k + 1, :]
        y = term if y is None else y + term
    return y


def _conv_fwd(proj, w, b, lg, lb, name):
    s = proj.shape[0]
    cw = w.shape[1]
    rows = CONV_ROWS
    blk_a = (proj.shape[1] - 2 * cw) // cw

    def body(a_ref, g_ref, w_ref, b_ref, lg_ref, lb_ref, o_ref, u_s):
        u_s[0:CONV_HALO, :] = jnp.zeros((CONV_HALO, cw), F32)

        def glu(i, _):
            r0 = pl.multiple_of(i * rows, rows)
            u_s[pl.ds(CONV_HALO + r0, rows), :] = a_ref[pl.ds(r0, rows), :] * _sigmoid(g_ref[pl.ds(r0, rows), :])
            return 0

        lax.fori_loop(0, s // rows, glu, 0)

        def chunk(i, _):
            r0 = pl.multiple_of(i * rows, rows)
            y = _conv_taps(u_s, r0, w_ref, rows) + b_ref[...]
            yc = y - jnp.mean(y, axis=-1, keepdims=True)
            n = yc * lax.rsqrt(jnp.mean(yc * yc, axis=-1, keepdims=True) + EPS)
            ln = n * lg_ref[...] + lb_ref[...]
            o_ref[pl.ds(r0, rows), :] = (ln * _sigmoid(ln)).astype(BF16)
            return 0

        lax.fori_loop(0, s // rows, chunk, 0)

    vec = pl.BlockSpec((1, cw), lambda i: (0, 0))
    return pl.pallas_call(
        body, name=name, grid=(1,),
        in_specs=[pl.BlockSpec((s, cw), lambda i: (0, blk_a)), pl.BlockSpec((s, cw), lambda i: (0, blk_a + 1)),
                  pl.BlockSpec((CONV_KERNEL, cw), lambda i: (0, 0)), vec, vec, vec],
        out_specs=pl.BlockSpec((s, cw), lambda i: (0, 0)),
        out_shape=jax.ShapeDtypeStruct((s, cw), BF16),
        scratch_shapes=[pltpu.VMEM((CONV_HALO + s, cw), F32)],
        compiler_params=_params(1),
    )(proj, proj, w, b, lg, lb)


def _conv_bwd(proj, dcat, w, b, lg, lb, name):
    s = proj.shape[0]
    cw = w.shape[1]
    rows = CONV_ROWS
    blk_a = (proj.shape[1] - 2 * cw) // cw
    n_chunk = s // rows

    def body(a_ref, g_ref, dc_ref, w_ref, b_ref, lg_ref, lb_ref, o_ref, dw_ref, db_ref, dlg_ref, dlb_ref,
             u_s, dy_s, dw_s):
        u_s[0:CONV_HALO, :] = jnp.zeros((CONV_HALO, cw), F32)
        dy_s[pl.ds(s, CONV_HALO), :] = jnp.zeros((CONV_HALO, cw), F32)
        dw_s[...] = jnp.zeros_like(dw_s)

        def glu(i, _):
            r0 = pl.multiple_of(i * rows, rows)
            u_s[pl.ds(CONV_HALO + r0, rows), :] = a_ref[pl.ds(r0, rows), :] * _sigmoid(g_ref[pl.ds(r0, rows), :])
            return 0

        lax.fori_loop(0, n_chunk, glu, 0)

        def chunk(i, carry):
            db, dlg, dlb = carry
            r0 = pl.multiple_of(i * rows, rows)
            window = u_s[pl.ds(r0, CONV_HALO + rows), :]
            y = None
            for k in range(CONV_KERNEL):
                term = _shifted(window, CONV_KERNEL - 1 - k, CONV_HALO) * w_ref[k:k + 1, :]
                y = term if y is None else y + term
            y = y + b_ref[...]
            yc = y - jnp.mean(y, axis=-1, keepdims=True)
            r = lax.rsqrt(jnp.mean(yc * yc, axis=-1, keepdims=True) + EPS)
            n = yc * r
            ln = n * lg_ref[...] + lb_ref[...]
            sg = _sigmoid(ln)
            dln = dc_ref[pl.ds(r0, rows), :] * (sg * (1.0 + ln * (1.0 - sg)))
            dn = dln * lg_ref[...]
            dy = r * (dn - jnp.mean(dn, axis=-1, keepdims=True) - n * jnp.mean(dn * n, axis=-1, keepdims=True))
            dy_s[pl.ds(r0, rows), :] = dy
            for k in range(CONV_KERNEL):
                prod = _shifted(window, CONV_KERNEL - 1 - k, CONV_HALO) * dy
                dw_s[k] += jnp.sum(prod.reshape(rows // 8, 8, cw), axis=0)
            return (db + jnp.sum(dy, axis=0, keepdims=True),
                    dlg + jnp.sum(dln * n, axis=0, keepdims=True),
                    dlb + jnp.sum(dln, axis=0, keepdims=True))

        zero = jnp.zeros((1, cw), F32)
        db, dlg, dlb = lax.fori_loop(0, n_chunk, chunk, (zero, zero, zero))
        db_ref[...] = db
        dlg_ref[...] = dlg
        dlb_ref[...] = dlb
        for k in range(CONV_KERNEL):
            dw_ref[k:k + 1, :] = jnp.sum(dw_s[k], axis=0, keepdims=True)

        def back(i, _):
            r0 = pl.multiple_of(i * rows, rows)
            window = dy_s[pl.ds(r0, rows + CONV_HALO), :]
            du = None
            for k in range(CONV_KERNEL):
                sh = CONV_KERNEL - 1 - k
                shifted = window[:rows, :] if sh == 0 else pltpu.roll(window, rows + CONV_HALO - sh, 0)[:rows, :]
                term = shifted * w_ref[k:k + 1, :]
                du = term if du is None else du + term
            av = a_ref[pl.ds(r0, rows), :]
            sg = _sigmoid(g_ref[pl.ds(r0, rows), :])
            o_ref[pl.ds(r0, rows), 0:cw] = (du * sg).astype(BF16)
            o_ref[pl.ds(r0, rows), cw:2 * cw] = (du * av * sg * (1.0 - sg)).astype(BF16)
            return 0

        lax.fori_loop(0, n_chunk, back, 0)

    vec = pl.BlockSpec((1, cw), lambda i: (0, 0))
    wspec = pl.BlockSpec((CONV_KERNEL, cw), lambda i: (0, 0))
    return pl.pallas_call(
        body, name=name, grid=(1,),
        in_specs=[pl.BlockSpec((s, cw), lambda i: (0, blk_a)), pl.BlockSpec((s, cw), lambda i: (0, blk_a + 1)),
                  pl.BlockSpec((s, cw), lambda i: (0, 1)), wspec, vec, vec, vec],
        out_specs=[pl.BlockSpec((s, 2 * cw), lambda i: (0, 0)), wspec, vec, vec, vec],
        out_shape=[jax.ShapeDtypeStruct((s, 2 * cw), BF16), jax.ShapeDtypeStruct((CONV_KERNEL, cw), F32)]
        + [jax.ShapeDtypeStruct((1, cw), F32)] * 3,
        scratch_shapes=[pltpu.VMEM((CONV_HALO + s, cw), F32), pltpu.VMEM((s + CONV_HALO, cw), F32),
                        pltpu.VMEM((CONV_KERNEL, 8, cw), F32)],
        compiler_params=_params(1),
    )(proj, proj, dcat, w, b, lg, lb)


FFN_ROWS = 256


def _ffn_gate(g_ref, r0, rows, w_ref, b_ref):
    cur = g_ref[pl.ds(r0, rows), :]
    prev = g_ref[pl.ds(pl.multiple_of(jnp.maximum(r0 - FFN_HALO, 0), FFN_HALO), FFN_HALO), :]
    prev = jnp.where(r0 > 0, prev, 0.0)
    window = jnp.concatenate([prev, cur], axis=0)
    gc = cur * w_ref[FFN_KERNEL - 1:FFN_KERNEL, :] + b_ref[...]
    for k in range(FFN_KERNEL - 1):
        gc = gc + _shifted(window, FFN_KERNEL - 1 - k, FFN_HALO) * w_ref[k:k + 1, :]
    return gc, window


def _ffn_fwd(up, w, b, name):
    s = up.shape[0]
    f = w.shape[1]
    tc = _pick(f, (256, 128))
    nc = f // tc
    rows = _pick(s, (FFN_ROWS, 128))

    def body(g_ref, v_ref, w_ref, b_ref, o_ref):
        def chunk(i, _):
            r0 = pl.multiple_of(i * rows, rows)
            gc, _w = _ffn_gate(g_ref, r0, rows, w_ref, b_ref)
            o_ref[pl.ds(r0, rows), :] = (gc * _sigmoid(gc) * v_ref[pl.ds(r0, rows), :]).astype(BF16)
            return 0

        lax.fori_loop(0, s // rows, chunk, 0)

    return pl.pallas_call(
        body, name=name, grid=(nc,),
        in_specs=[pl.BlockSpec((s, tc), lambda j: (0, j)), pl.BlockSpec((s, tc), lambda j: (0, nc + j)),
                  pl.BlockSpec((FFN_KERNEL, tc), lambda j: (0, j)), pl.BlockSpec((1, tc), lambda j: (0, j))],
        out_specs=pl.BlockSpec((s, tc), lambda j: (0, j)),
        out_shape=jax.ShapeDtypeStruct((s, f), BF16),
        compiler_params=_params(1),
    )(up, up, w, b)


def _ffn_bwd(up, dact, w, b, name):
    s = up.shape[0]
    f = w.shape[1]
    tc = _pick(f, (256, 128))
    nc = f // tc
    rows = _pick(s, (FFN_ROWS, 128))
    n_chunk = s // rows

    def body(g_ref, v_ref, da_ref, w_ref, b_ref, dg_ref, dv_ref, dw_ref, db_ref, dgc_s):
        dgc_s[pl.ds(s, FFN_HALO), :] = jnp.zeros((FFN_HALO, tc), F32)

        def chunk(i, carry):
            r0 = pl.multiple_of(i * rows, rows)
            gc, window = _ffn_gate(g_ref, r0, rows, w_ref, b_ref)
            sg = _sigmoid(gc)
            da = da_ref[pl.ds(r0, rows), :]
            dv_ref[pl.ds(r0, rows), :] = (da * gc * sg).astype(BF16)
            dgc = da * v_ref[pl.ds(r0, rows), :] * (sg * (1.0 + gc * (1.0 - sg)))
            dgc_s[pl.ds(r0, rows), :] = dgc
            out = [carry[0] + jnp.sum(dgc, axis=0, keepdims=True)]
            for k in range(FFN_KERNEL):
                out.append(carry[1 + k] + jnp.sum(_shifted(window, FFN_KERNEL - 1 - k, FFN_HALO) * dgc,
                                                  axis=0, keepdims=True))
            return tuple(out)

        zero = jnp.zeros((1, tc), F32)
        sums = lax.fori_loop(0, n_chunk, chunk, (zero,) * (1 + FFN_KERNEL))
        db_ref[...] = sums[0]
        for k in range(FFN_KERNEL):
            dw_ref[k:k + 1, :] = sums[1 + k]

        def back(i, _):
            r0 = pl.multiple_of(i * rows, rows)
            window = dgc_s[pl.ds(r0, rows + FFN_HALO), :]
            dg = window[:rows, :] * w_ref[FFN_KERNEL - 1:FFN_KERNEL, :]
            for k in range(FFN_KERNEL - 1):
                sh = FFN_KERNEL - 1 - k
                dg = dg + pltpu.roll(window, rows + FFN_HALO - sh, 0)[:rows, :] * w_ref[k:k + 1, :]
            dg_ref[pl.ds(r0, rows), :] = dg.astype(BF16)
            return 0

        lax.fori_loop(0, n_chunk, back, 0)

    blk = lambda off: pl.BlockSpec((s, tc), lambda j: (0, off + j))
    return pl.pallas_call(
        body, name=name, grid=(nc,),
        in_specs=[blk(0), blk(nc), blk(0), pl.BlockSpec((FFN_KERNEL, tc), lambda j: (0, j)),
                  pl.BlockSpec((1, tc), lambda j: (0, j))],
        out_specs=[blk(0), blk(0), pl.BlockSpec((FFN_KERNEL, tc), lambda j: (0, j)),
                   pl.BlockSpec((1, tc), lambda j: (0, j))],
        out_shape=[jax.ShapeDtypeStruct((s, f), BF16), jax.ShapeDtypeStruct((s, f), BF16),
                   jax.ShapeDtypeStruct((FFN_KERNEL, f), F32), jax.ShapeDtypeStruct((1, f), F32)],
        scratch_shapes=[pltpu.VMEM((s + FFN_HALO, tc), F32)],
        compiler_params=_params(1),
    )(up, up, dact, w, b)


def _loss_head(y, target, name):
    m, n = y.shape
    tm = _pick(m, (256, 128))

    def body(y_ref, t_ref, l_ref, d_ref, db_ref):
        e = y_ref[...] - t_ref[...]
        part = 0.5 * jnp.sum(jnp.sum(e * e, axis=-1, keepdims=True) / n, axis=0, keepdims=True)

        @pl.when(pl.program_id(0) == 0)
        def _():
            l_ref[...] = jnp.zeros_like(l_ref)

        l_ref[...] += part
        d = e / n
        d_ref[...] = d
        db_ref[...] = d.astype(BF16)

    return pl.pallas_call(
        body, name=name, grid=(m // tm,),
        in_specs=[pl.BlockSpec((tm, n), lambda i: (i, 0))] * 2,
        out_specs=[pl.BlockSpec((8, LANES), lambda i: (0, 0)), pl.BlockSpec((tm, n), lambda i: (i, 0)),
                   pl.BlockSpec((tm, n), lambda i: (i, 0))],
        out_shape=[jax.ShapeDtypeStruct((8, LANES), F32), jax.ShapeDtypeStruct((m, n), F32),
                   jax.ShapeDtypeStruct((m, n), BF16)],
        compiler_params=_params(1),
    )(y, target)


def _adamw_math(w, g, m, v):
    m = ADAM_B1 * m + (1.0 - ADAM_B1) * g
    v = ADAM_B2 * v + (1.0 - ADAM_B2) * (g * g)
    m_hat = m / (1.0 - ADAM_B1 ** ADAM_STEP)
    v_hat = v / (1.0 - ADAM_B2 ** ADAM_STEP)
    delta = -ADAM_LR * (m_hat / (jnp.sqrt(v_hat) + ADAM_EPS) + ADAM_WD * w)
    return delta, m, v


def _sum_adamw(parts, w, m, v, name):
    depth, r, c = w.shape
    tr = _pick(r, (256, 128, 32, 16))

    def body(p_ref, w_ref, m_ref, v_ref, g_out, d_out, m_out, v_out):
        g = p_ref[0, 0].astype(F32)
        for src in range(1, N_DEV):
            g = g + p_ref[src, 0].astype(F32)
        d, mn, vn = _adamw_math(w_ref[0], g, m_ref[0], v_ref[0])
        g_out[0] = g
        d_out[0] = d
        m_out[0] = mn
        v_out[0] = vn

    blk = pl.BlockSpec((1, tr, c), lambda l, i: (l, i, 0))
    return pl.pallas_call(
        body, name=name, grid=(depth, r // tr),
        in_specs=[pl.BlockSpec((N_DEV, 1, tr, c), lambda l, i: (0, l, i, 0)), blk, blk, blk],
        out_specs=[blk] * 4,
        out_shape=[jax.ShapeDtypeStruct(w.shape, F32)] * 4,
        compiler_params=_params(2),
    )(parts, w, m, v)


def _sum_rows(parts, name):
    _, r, c = parts.shape

    def body(p_ref, o_ref):
        g = p_ref[0]
        for src in range(1, N_DEV):
            g = g + p_ref[src]
        o_ref[...] = g

    return pl.pallas_call(
        body, name=name, grid=(1,),
        in_specs=[pl.BlockSpec((N_DEV, r, c), lambda i: (0, 0, 0))],
        out_specs=pl.BlockSpec((r, c), lambda i: (0, 0)),
        out_shape=jax.ShapeDtypeStruct((r, c), F32),
        compiler_params=_params(1),
    )(parts)


def _adamw_rows(w, g, m, v, name):
    r, c = w.shape

    def body(w_ref, g_ref, m_ref, v_ref, d_out, m_out, v_out):
        d, mn, vn = _adamw_math(w_ref[...], g_ref[...], m_ref[...], v_ref[...])
        d_out[...] = d
        m_out[...] = mn
        v_out[...] = vn

    blk = pl.BlockSpec((r, c), lambda i: (0, 0))
    return pl.pallas_call(
        body, name=name, grid=(1,), in_specs=[blk] * 4, out_specs=[blk] * 3,
        out_shape=[jax.ShapeDtypeStruct((r, c), F32)] * 3,
        compiler_params=_params(1),
    )(w, g, m, v)


def _exchange(parts, block_axis, out_axis, scatter, name):
    n = len(parts)

    def pick(ref, axis, idx):
        return ref.at[(slice(None),) * axis + (idx,)]

    out_shape = []
    for a, ba, oa in zip(parts, block_axis, out_axis):
        blk = list(a.shape)
        if scatter:
            del blk[ba]
        blk.insert(oa, N_DEV)
        out_shape.append(jax.ShapeDtypeStruct(tuple(blk), a.dtype))

    def body(*refs):
        ins, outs = refs[:n], refs[n:2 * n]
        send_sems, recv_sems, local_sems = refs[2 * n:]
        x, y, c = lax.axis_index("x"), lax.axis_index("y"), lax.axis_index("c")
        me = 4 * x + 2 * y + c
        copies = []
        for i in range(n):
            src = pick(ins[i], block_axis[i], me) if scatter else ins[i]
            cp = pltpu.make_async_copy(src, pick(outs[i], out_axis[i], me), local_sems.at[i])
            cp.start()
            copies.append(cp)
        for d in range(1, N_DEV):
            px = 1 - x if d & 4 else x
            py = 1 - y if d & 2 else y
            pc = 1 - c if d & 1 else c
            peer = 4 * px + 2 * py + pc
            for i in range(n):
                src = pick(ins[i], block_axis[i], peer) if scatter else ins[i]
                cp = pltpu.make_async_remote_copy(
                    src_ref=src, dst_ref=pick(outs[i], out_axis[i], me),
                    send_sem=send_sems.at[i, d - 1], recv_sem=recv_sems.at[i, d - 1],
                    device_id=(px, py, pc), device_id_type=pl.DeviceIdType.MESH)
                cp.start()
                copies.append(cp)
        for cp in copies:
            cp.wait()

    any_spec = pl.BlockSpec(memory_space=pl.ANY)
    return pl.pallas_call(
        body, name=name,
        in_specs=[any_spec] * n, out_specs=[any_spec] * n, out_shape=out_shape,
        scratch_shapes=[pltpu.SemaphoreType.DMA((n, N_DEV - 1)), pltpu.SemaphoreType.DMA((n, N_DEV - 1)),
                        pltpu.SemaphoreType.DMA((n,))],
    )(*parts)


PACK_TILE = 8 * LANES


def _pack(arrays):
    rows = []
    for a in arrays:
        flat = a.reshape(-1).astype(F32)
        pad = (-flat.shape[0]) % PACK_TILE
        rows.append(jnp.pad(flat, (0, pad)).reshape(-1, LANES))
    return jnp.concatenate(rows, axis=0)


def _unpack(packed, shapes, lead=()):
    out, r0 = [], 0
    for shp in shapes:
        size = 1
        for d in shp:
            size *= d
        nrows = -(-size // PACK_TILE) * 8
        piece = packed[..., r0:r0 + nrows, :].reshape(lead + (nrows * LANES,))[..., :size]
        out.append(piece.reshape(lead + tuple(shp)))
        r0 += nrows
    return out


def kernel(x, norm1_g, w_in, q_norm_g, k_norm_g, conv_dw_w, conv_dw_b, conv_ln_g, conv_ln_b, w_out, norm2_g, w_up, ffn_dw_w, ffn_dw_b, w_down, loss_target, m_norm1_g, m_w_in, m_q_norm_g, m_k_norm_g, m_conv_dw_w, m_conv_dw_b, m_conv_ln_g, m_conv_ln_b, m_w_out, m_norm2_g, m_w_up, m_ffn_dw_w, m_ffn_dw_b, m_w_down, v_norm1_g, v_w_in, v_q_norm_g, v_k_norm_g, v_conv_dw_w, v_conv_dw_b, v_conv_ln_g, v_conv_ln_b, v_w_out, v_norm2_g, v_w_up, v_ffn_dw_w, v_ffn_dw_b, v_w_down):
    depth, d_model, in_shard = w_in.shape
    out_shard = w_out.shape[1]
    up_shard = w_up.shape[2]
    down_shard = w_down.shape[1]
    d_ff = down_shard * N_DEV
    conv_w = conv_dw_b.shape[1]
    cw_shard = conv_dw_w.shape[2]
    fw_shard = ffn_dw_w.shape[2]
    me = 4 * lax.axis_index("x") + 2 * lax.axis_index("y") + lax.axis_index("c")

    g_in, g_out, g_up, g_down = _exchange(
        [w_in.astype(BF16), w_out.astype(BF16), w_up.astype(BF16), w_down.astype(BF16)],
        block_axis=[0, 0, 0, 0], out_axis=[0, 1, 0, 1], scatter=False, name="gather_weights")
    wf_in = g_in.transpose(1, 2, 0, 3).reshape(depth, d_model, N_DEV * in_shard)
    wf_out = g_out.reshape(depth, N_DEV * out_shard, d_model)
    wf_up = g_up.transpose(1, 2, 0, 3).reshape(depth, d_model, N_DEV * up_shard)
    wf_down = g_down.reshape(depth, d_ff, d_model)
    (g_small,) = _exchange([_pack([conv_dw_w, ffn_dw_w])], block_axis=[0], out_axis=[0], scatter=False,
                           name="gather_filters")
    g_cw, g_fw = _unpack(g_small, [conv_dw_w.shape, ffn_dw_w.shape], lead=(N_DEV,))
    cwf = g_cw.transpose(1, 2, 0, 3).reshape(depth, CONV_KERNEL, conv_w)
    fwf = g_fw.transpose(1, 2, 0, 3).reshape(depth, FFN_KERNEL, d_ff)

    row = lambda a, l: a[l].reshape(1, -1)

    xs = x[0]
    saved = []
    for l in range(depth):
        proj, h1 = _mm_rms(xs, row(norm1_g, l), wf_in[l], name="fwd_in")
        attn, tsum = _attn_fwd(proj, row(q_norm_g, l), row(k_norm_g, l), name="fwd_attn")
        conv = _conv_fwd(proj, cwf[l], row(conv_dw_b, l), row(conv_ln_g, l), row(conv_ln_b, l), name="fwd_conv")
        cat = jnp.concatenate([attn, conv], axis=1)
        x_mid = _mm_res(cat, wf_out[l], xs, name="fwd_out")
        up, h2 = _mm_rms(x_mid, row(norm2_g, l), wf_up[l], name="fwd_up")
        act = _ffn_fwd(up, fwf[l], row(ffn_dw_b, l), name="fwd_ffn")
        x_next = _mm_res(act, wf_down[l], x_mid, name="fwd_down")
        saved.append((xs, h1, proj, tsum, cat, x_mid, h2, up, act))
        xs = x_next

    loss_tile, dx, dxb = _loss_head(xs, loss_target[0], name="loss_head")
    loss = lax.psum(loss_tile[0, 0], ("x", "y", "c"))

    gw_in, gw_out, gw_up, gw_down = [None] * depth, [None] * depth, [None] * depth, [None] * depth
    small = {k: [None] * depth for k in ("norm1_g", "q_norm_g", "k_norm_g", "conv_dw_w", "conv_dw_b", "conv_ln_g",
                                         "conv_ln_b", "norm2_g", "ffn_dw_w", "ffn_dw_b")}
    for l in reversed(range(depth)):
        xs, h1, proj, tsum, cat, x_mid, h2, up, act = saved[l]
        dact = _mm_nt(dxb, wf_down[l], name="bwd_dact")
        gw_down[l] = _mm_tn(act, dxb, name="bwd_gw_down")
        dgate, dval, small["ffn_dw_w"][l], small["ffn_dw_b"][l] = _ffn_bwd(
            up, dact, fwf[l], row(ffn_dw_b, l), name="bwd_ffn")
        dup = jnp.concatenate([dgate, dval], axis=1)
        gw_up[l] = _mm_tn(h2, dup, name="bwd_gw_up")
        dx, dxb, small["norm2_g"][l] = _mm_nt_rmsbwd(dup, wf_up[l], x_mid, row(norm2_g, l), dx, name="bwd_up")
        dcat = _mm_nt(dxb, wf_out[l], name="bwd_dcat")
        gw_out[l] = _mm_tn(cat, dxb, name="bwd_gw_out")
        dglu, small["conv_dw_w"][l], small["conv_dw_b"][l], small["conv_ln_g"][l], small["conv_ln_b"][l] = _conv_bwd(
            proj, dcat, cwf[l], row(conv_dw_b, l), row(conv_ln_g, l), row(conv_ln_b, l), name="bwd_conv")
        dq, dk, dv, dqg, dkg = _attn_bwd(proj, dcat, tsum, row(q_norm_g, l), row(k_norm_g, l), name="bwd_attn")
        small["q_norm_g"][l] = dqg[0:1, 0:HEAD_DIM]
        small["k_norm_g"][l] = dkg[0:1, 0:HEAD_DIM]
        dproj = jnp.concatenate([dq, dk, dv, dglu], axis=1)
        gw_in[l] = _mm_tn(h1, dproj, name="bwd_gw_in")
        dx, dxb, small["norm1_g"][l] = _mm_nt_rmsbwd(dproj, wf_in[l], xs, row(norm1_g, l), dx, name="bwd_in")
    grad_x = dx[None]

    s_in = jnp.stack(gw_in).reshape(depth, d_model, N_DEV, in_shard).transpose(2, 0, 1, 3)
    s_out = jnp.stack(gw_out).reshape(depth, N_DEV, out_shard, d_model)
    s_up = jnp.stack(gw_up).reshape(depth, d_model, N_DEV, up_shard).transpose(2, 0, 1, 3)
    s_down = jnp.stack(gw_down).reshape(depth, N_DEV, down_shard, d_model)
    r_in, r_out, r_up, r_down = _exchange([s_in, s_out, s_up, s_down], block_axis=[0, 1, 0, 1],
                                          out_axis=[0, 0, 0, 0], scatter=True, name="exchange_grads")
    big = {
        "w_in": _sum_adamw(r_in, w_in, m_w_in, v_w_in, name="adamw_in"),
        "w_out": _sum_adamw(r_out, w_out, m_w_out, v_w_out, name="adamw_out"),
        "w_up": _sum_adamw(r_up, w_up, m_w_up, v_w_up, name="adamw_up"),
        "w_down": _sum_adamw(r_down, w_down, m_w_down, v_w_down, name="adamw_down"),
    }

    names = ["norm1_g", "q_norm_g", "k_norm_g", "conv_dw_w", "conv_dw_b", "conv_ln_g", "conv_ln_b", "norm2_g",
             "ffn_dw_w", "ffn_dw_b"]
    full_shapes = {"norm1_g": norm1_g.shape, "q_norm_g": q_norm_g.shape, "k_norm_g": k_norm_g.shape,
                   "conv_dw_w": (depth, CONV_KERNEL, conv_w), "conv_dw_b": conv_dw_b.shape,
                   "conv_ln_g": conv_ln_g.shape, "conv_ln_b": conv_ln_b.shape, "norm2_g": norm2_g.shape,
                   "ffn_dw_w": (depth, FFN_KERNEL, d_ff), "ffn_dw_b": ffn_dw_b.shape}
    partial = _pack([jnp.stack(small[k]).reshape(full_shapes[k]) for k in names])
    (all_partials,) = _exchange([partial], block_axis=[0], out_axis=[0], scatter=False, name="gather_small_grads")
    total = _unpack(_sum_rows(all_partials, name="sum_small_grads"), [full_shapes[k] for k in names])
    grads = dict(zip(names, total))
    grads["conv_dw_w"] = lax.dynamic_slice_in_dim(grads["conv_dw_w"], me * cw_shard, cw_shard, axis=2)
    grads["ffn_dw_w"] = lax.dynamic_slice_in_dim(grads["ffn_dw_w"], me * fw_shard, fw_shard, axis=2)
    weights = dict(norm1_g=norm1_g, q_norm_g=q_norm_g, k_norm_g=k_norm_g, conv_dw_w=conv_dw_w, conv_dw_b=conv_dw_b,
                   conv_ln_g=conv_ln_g, conv_ln_b=conv_ln_b, norm2_g=norm2_g, ffn_dw_w=ffn_dw_w, ffn_dw_b=ffn_dw_b)
    m_in = dict(norm1_g=m_norm1_g, q_norm_g=m_q_norm_g, k_norm_g=m_k_norm_g, conv_dw_w=m_conv_dw_w,
                conv_dw_b=m_conv_dw_b, conv_ln_g=m_conv_ln_g, conv_ln_b=m_conv_ln_b, norm2_g=m_norm2_g,
                ffn_dw_w=m_ffn_dw_w, ffn_dw_b=m_ffn_dw_b)
    v_in = dict(norm1_g=v_norm1_g, q_norm_g=v_q_norm_g, k_norm_g=v_k_norm_g, conv_dw_w=v_conv_dw_w,
                conv_dw_b=v_conv_dw_b, conv_ln_g=v_conv_ln_g, conv_ln_b=v_conv_ln_b, norm2_g=v_norm2_g,
                ffn_dw_w=v_ffn_dw_w, ffn_dw_b=v_ffn_dw_b)
    shard_shapes = [weights[k].shape for k in names]
    d_s, m_s, v_s = _adamw_rows(_pack([weights[k] for k in names]), _pack([grads[k] for k in names]),
                                _pack([m_in[k] for k in names]), _pack([v_in[k] for k in names]), name="adamw_small")
    delta = dict(zip(names, _unpack(d_s, shard_shapes)))
    new_m = dict(zip(names, _unpack(m_s, shard_shapes)))
    new_v = dict(zip(names, _unpack(v_s, shard_shapes)))
    for k, (g, d, mn, vn) in big.items():
        grads[k], delta[k], new_m[k], new_v[k] = g, d, mn, vn

    order = ["norm1_g", "w_in", "q_norm_g", "k_norm_g", "conv_dw_w", "conv_dw_b", "conv_ln_g", "conv_ln_b", "w_out",
             "norm2_g", "w_up", "ffn_dw_w", "ffn_dw_b", "w_down"]
    return (loss, grad_x, *[grads[k] for k in order], *[delta[k] for k in order], *[new_m[k] for k in order],
            *[new_v[k] for k in order])
```

```python
import functools

import jax
import jax.numpy as jnp
from jax import lax
from jax.experimental import pallas as pl
from jax.experimental.pallas import tpu as pltpu

F32 = jnp.float32
BF16 = jnp.bfloat16

N_DEV = 8
HEADS = 8
HEAD_DIM = 64
ATTN_WIDTH = HEADS * HEAD_DIM
CONV_KERNEL = 31
FFN_KERNEL = 3
EPS = 1e-6
BLK = 128
KEY_GROUP = 4
LANES = 128
CONV_HALO = 32
FFN_HALO = 8

ADAM_LR = 0.001
ADAM_B1 = 0.9
ADAM_B2 = 0.999
ADAM_EPS = 1e-08
ADAM_WD = 0.01
ADAM_STEP = 10

VMEM_LIMIT = 56 * 1024 * 1024


def _params(n_axes=0):
    kw = dict(vmem_limit_bytes=VMEM_LIMIT)
    if n_axes:
        kw["dimension_semantics"] = ("arbitrary",) * n_axes
    return pltpu.CompilerParams(**kw)


def _dot(a, b):
    return jnp.dot(a, b, preferred_element_type=F32)


def _dot_nt(a, b):
    return lax.dot_general(a, b, (((1,), (1,)), ((), ())), preferred_element_type=F32)


def _dot_tn(a, b):
    return lax.dot_general(a, b, (((0,), (0,)), ((), ())), preferred_element_type=F32)


def _sigmoid(x):
    return 1.0 / (1.0 + jnp.exp(-x))


def _split_bf16(x):
    hi = x.astype(BF16)
    lo = (x - hi.astype(F32)).astype(BF16)
    return hi, lo


def _pick(n, options):
    for t in options:
        if n % t == 0:
            return t
    return n


def _mm_rms(x, g, w, name):
    m, k = x.shape
    n = w.shape[1]
    tm = _pick(m, (512, 256, 128))
    tn = _pick(n, (512, 256, 128))

    def body(x_ref, g_ref, w_ref, o_ref, h_ref, h_s):
        @pl.when(pl.program_id(1) == 0)
        def _():
            xv = x_ref[...]
            r = lax.rsqrt(jnp.mean(xv * xv, axis=-1, keepdims=True) + EPS)
            hv = (xv * r * g_ref[...]).astype(BF16)
            h_s[...] = hv
            h_ref[...] = hv

        o_ref[...] = _dot(h_s[...], w_ref[...])

    return pl.pallas_call(
        body, name=name, grid=(m // tm, n // tn),
        in_specs=[pl.BlockSpec((tm, k), lambda i, j: (i, 0)),
                  pl.BlockSpec((1, k), lambda i, j: (0, 0)),
                  pl.BlockSpec((k, tn), lambda i, j: (0, j))],
        out_specs=[pl.BlockSpec((tm, tn), lambda i, j: (i, j)),
                   pl.BlockSpec((tm, k), lambda i, j: (i, 0))],
        out_shape=[jax.ShapeDtypeStruct((m, n), F32), jax.ShapeDtypeStruct((m, k), BF16)],
        scratch_shapes=[pltpu.VMEM((tm, k), BF16)],
        compiler_params=_params(2),
    )(x, g, w)


def _mm_res(a, w, res, name):
    m, k = a.shape
    n = w.shape[1]
    tm = _pick(m, (512, 256, 128))
    tn = _pick(n, (512, 256, 128))

    def body(a_ref, w_ref, r_ref, o_ref):
        o_ref[...] = r_ref[...] + _dot(a_ref[...], w_ref[...])

    return pl.pallas_call(
        body, name=name, grid=(m // tm, n // tn),
        in_specs=[pl.BlockSpec((tm, k), lambda i, j: (i, 0)),
                  pl.BlockSpec((k, tn), lambda i, j: (0, j)),
                  pl.BlockSpec((tm, tn), lambda i, j: (i, j))],
        out_specs=pl.BlockSpec((tm, tn), lambda i, j: (i, j)),
        out_shape=jax.ShapeDtypeStruct((m, n), F32),
        compiler_params=_params(2),
    )(a, w, res)


def _mm_nt(a, w, name):
    m, k = a.shape
    n = w.shape[0]
    tm = _pick(m, (512, 256, 128))
    tn = _pick(n, (512, 256, 128))

    def body(a_ref, w_ref, o_ref):
        o_ref[...] = _dot_nt(a_ref[...], w_ref[...])

    return pl.pallas_call(
        body, name=name, grid=(m // tm, n // tn),
        in_specs=[pl.BlockSpec((tm, k), lambda i, j: (i, 0)),
                  pl.BlockSpec((tn, k), lambda i, j: (j, 0))],
        out_specs=pl.BlockSpec((tm, tn), lambda i, j: (i, j)),
        out_shape=jax.ShapeDtypeStruct((m, n), F32),
        compiler_params=_params(2),
    )(a, w)


def _mm_tn(a, b, name):
    s, m = a.shape
    n = b.shape[1]
    tm = _pick(m, (512, 256, 128))
    tn = _pick(n, (512, 256, 128))

    def body(a_ref, b_ref, o_ref):
        o_ref[...] = _dot_tn(a_ref[...], b_ref[...]).astype(BF16)

    return pl.pallas_call(
        body, name=name, grid=(m // tm, n // tn),
        in_specs=[pl.BlockSpec((s, tm), lambda i, j: (0, i)),
                  pl.BlockSpec((s, tn), lambda i, j: (0, j))],
        out_specs=pl.BlockSpec((tm, tn), lambda i, j: (i, j)),
        out_shape=jax.ShapeDtypeStruct((m, n), BF16),
        compiler_params=_params(2),
    )(a, b)


def _mm_nt_rmsbwd(a, w, x, g, dres, name):
    m, k = a.shape
    n = w.shape[0]
    tm = _pick(m, (256, 128))
    tk = _pick(k, (512, 256, 128))
    nk = k // tk

    def body(a_ref, w_ref, x_ref, g_ref, r_ref, dx_ref, dxb_ref, dg_ref, acc):
        i, kk = pl.program_id(0), pl.program_id(1)
        part = _dot_nt(a_ref[...], w_ref[...])

        @pl.when(kk == 0)
        def _():
            acc[...] = part

        @pl.when(kk > 0)
        def _():
            acc[...] += part

        @pl.when(kk == nk - 1)
        def _():
            dh = acc[...]
            xv = x_ref[...]
            r = lax.rsqrt(jnp.mean(xv * xv, axis=-1, keepdims=True) + EPS)
            xh = xv * r
            dgp = jnp.sum(dh * xh, axis=0, keepdims=True)

            @pl.when(i == 0)
            def _():
                dg_ref[...] = dgp

            @pl.when(i > 0)
            def _():
                dg_ref[...] += dgp

            dxh = dh * g_ref[...]
            dx = r_ref[...] + r * (dxh - xh * jnp.mean(dxh * xh, axis=-1, keepdims=True))
            dx_ref[...] = dx
            dxb_ref[...] = dx.astype(BF16)

    return pl.pallas_call(
        body, name=name, grid=(m // tm, nk),
        in_specs=[pl.BlockSpec((tm, tk), lambda i, kk: (i, kk)),
                  pl.BlockSpec((n, tk), lambda i, kk: (0, kk)),
                  pl.BlockSpec((tm, n), lambda i, kk: (i, 0)),
                  pl.BlockSpec((1, n), lambda i, kk: (0, 0)),
                  pl.BlockSpec((tm, n), lambda i, kk: (i, 0))],
        out_specs=[pl.BlockSpec((tm, n), lambda i, kk: (i, 0)),
                   pl.BlockSpec((tm, n), lambda i, kk: (i, 0)),
                   pl.BlockSpec((1, n), lambda i, kk: (0, 0))],
        out_shape=[jax.ShapeDtypeStruct((m, n), F32), jax.ShapeDtypeStruct((m, n), BF16),
                   jax.ShapeDtypeStruct((1, n), F32)],
        scratch_shapes=[pltpu.VMEM((tm, n), F32)],
        compiler_params=_params(2),
    )(a, w, x, g, dres)


def _tri(kind):
    j = lax.broadcasted_iota(jnp.int32, (BLK, BLK), 0)
    s = lax.broadcasted_iota(jnp.int32, (BLK, BLK), 1)
    m = {"after": j > s, "upto": j <= s, "before": j < s}[kind]
    half = jnp.concatenate([jnp.where(m, 1.0, 0.0), jnp.ones((BLK, BLK), F32)], axis=1).astype(BF16)
    return jnp.concatenate([half, half], axis=0)


def _scan_rows(v, tri):
    hi, lo = _split_bf16(v)
    r = _dot(jnp.concatenate([hi, lo], axis=1), tri)
    return r[:, :BLK], r[:, BLK:]


def _head_norm(v, g):
    r = lax.rsqrt(jnp.mean(v * v, axis=-1, keepdims=True) + EPS)
    return v * r * g


def _attn_prep(q_ref, k_ref, v_ref, qg_ref, kg_ref, sl, qc_s, kc_s, vb_s, n_blk):
    scale = HEAD_DIM ** -0.5

    def prep(i, _):
        rows = pl.ds(pl.multiple_of(i * BLK, BLK), BLK)
        qh, ql = _split_bf16(_head_norm(q_ref[rows, sl], qg_ref[...]) * scale)
        kh, kl = _split_bf16(_head_norm(k_ref[rows, sl], kg_ref[...]))
        qc_s[rows, :] = jnp.concatenate([qh, ql, qh, ql], axis=1)
        kc_s[rows, :] = jnp.concatenate([kh, kh, kl, kl], axis=1)
        vb_s[rows, :] = v_ref[rows, sl].astype(BF16)
        return 0

    lax.fori_loop(0, n_blk, prep, 0)


def _col_minus_row():
    row = lax.broadcasted_iota(jnp.int32, (BLK, BLK), 0)
    col = lax.broadcasted_iota(jnp.int32, (BLK, BLK), 1)
    return col - row


def _softplus(z):
    return jnp.maximum(z, 0.0) + jnp.log(1.0 + jnp.exp(-jnp.abs(z)))


def _attn_fwd(proj, qg, kg, name):
    s = proj.shape[0]
    n_blk = s // BLK
    pairs = ATTN_WIDTH // LANES

    def body(q_ref, k_ref, v_ref, qg_ref, kg_ref, o_ref, t_ref, qc_s, kc_s, vb_s):
        tri = _tri("after")
        diff = _col_minus_row()
        for hh in range(LANES // HEAD_DIM):
            sl = slice(hh * HEAD_DIM, (hh + 1) * HEAD_DIM)
            _attn_prep(q_ref, k_ref, v_ref, qg_ref, kg_ref, sl, qc_s, kc_s, vb_s, n_blk)

            def group(qc, qi, grp, carry, acc, masked):
                blocks = [grp * KEY_GROUP + j for j in reversed(range(KEY_GROUP))]
                cols_of = [pl.ds(pl.multiple_of(kb * BLK, BLK), BLK) for kb in blocks]
                zs = [_dot_nt(qc, kc_s[cols, :]) for cols in cols_of]
                parts = []
                for kb, cols, z in zip(blocks, cols_of, zs):
                    sp = _softplus(z)
                    lom = -sp
                    keep = None
                    if masked:
                        keep = diff < (qi - kb) * BLK
                        lom = jnp.where(keep, lom, 0.0)
                    tail, tot = _scan_rows(lom, tri)
                    parts.append((z - sp + tail, tot, keep, cols))
                for lw, tot, keep, cols in parts:
                    w = jnp.exp(lw + carry)
                    if masked:
                        w = jnp.where(keep, w, 0.0)
                    acc = acc + _dot(w.astype(BF16), vb_s[cols, :])
                    carry = carry + tot
                return carry, acc

            def q_block(qi, _):
                rows = pl.ds(pl.multiple_of(qi * BLK, BLK), BLK)
                qc = qc_s[rows, :]
                top = qi // KEY_GROUP
                carry, acc = group(qc, qi, top, jnp.zeros((BLK, BLK), F32), jnp.zeros((BLK, HEAD_DIM), F32), True)
                carry, acc = lax.fori_loop(
                    0, top, lambda t, c: group(qc, qi, top - 1 - t, c[0], c[1], False), (carry, acc))
                o_ref[rows, sl] = acc.astype(BF16)
                t_ref[rows, sl] = carry[:, :HEAD_DIM]
                return 0

            lax.fori_loop(0, n_blk, q_block, 0)

    col = lambda off: pl.BlockSpec((s, LANES), lambda p: (0, off + p))
    vec = pl.BlockSpec((1, HEAD_DIM), lambda p: (0, 0))
    return pl.pallas_call(
        body, name=name, grid=(pairs,),
        in_specs=[col(0), col(pairs), col(2 * pairs), vec, vec],
        out_specs=[pl.BlockSpec((s, LANES), lambda p: (0, p))] * 2,
        out_shape=[jax.ShapeDtypeStruct((s, ATTN_WIDTH), BF16), jax.ShapeDtypeStruct((s, ATTN_WIDTH), F32)],
        scratch_shapes=[pltpu.VMEM((s, 4 * HEAD_DIM), BF16)] * 2 + [pltpu.VMEM((s, HEAD_DIM), BF16)],
        compiler_params=_params(1),
    )(proj, proj, proj, qg, kg)


def _attn_bwd(proj, dcat, tsum, qg, kg, name):
    s = proj.shape[0]
    n_blk = s // BLK
    pairs = ATTN_WIDTH // LANES
    scale = HEAD_DIM ** -0.5

    def norm_bwd(raw, g, dn):
        r = lax.rsqrt(jnp.mean(raw * raw, axis=-1, keepdims=True) + EPS)
        xh = raw * r
        dg = jnp.sum(dn * xh, axis=0, keepdims=True)
        dxh = dn * g
        return r * (dxh - xh * jnp.mean(dxh * xh, axis=-1, keepdims=True)), dg

    def body(q_ref, k_ref, v_ref, do_ref, t_ref, qg_ref, kg_ref, dq_ref, dk_ref, dv_ref, dqg_ref, dkg_ref,
             qc_s, kc_s, vb_s, dob_s, dkv_s):
        tri_p = _tri("upto")
        tri_h = _tri("before")
        diff = _col_minus_row()

        @pl.when(pl.program_id(0) == 0)
        def _():
            dqg_ref[...] = jnp.zeros_like(dqg_ref)
            dkg_ref[...] = jnp.zeros_like(dkg_ref)

        for hh in range(LANES // HEAD_DIM):
            sl = slice(hh * HEAD_DIM, (hh + 1) * HEAD_DIM)
            _attn_prep(q_ref, k_ref, v_ref, qg_ref, kg_ref, sl, qc_s, kc_s, vb_s, n_blk)

            def prep(i, _):
                rows = pl.ds(pl.multiple_of(i * BLK, BLK), BLK)
                dob_s[rows, :] = do_ref[rows, sl].astype(BF16)
                dkv_s[rows, :] = jnp.zeros((BLK, 2 * HEAD_DIM), F32)
                return 0

            lax.fori_loop(0, n_blk, prep, 0)

            def group(qc, qd, dob, tq, qi, grp, pc, hc, dq, masked):
                blocks = [grp * KEY_GROUP + j for j in range(KEY_GROUP)]
                cols_of = [pl.ds(pl.multiple_of(kb * BLK, BLK), BLK) for kb in blocks]
                kcs = [kc_s[cols, :] for cols in cols_of]
                zs = [_dot_nt(qc, kc) for kc in kcs]
                das = [_dot_nt(dob, vb_s[cols, :]) for cols in cols_of]
                lbs, keeps, scans = [], [], []
                for kb, z in zip(blocks, zs):
                    sp = _softplus(z)
                    lom = -sp
                    keep = None
                    if masked:
                        keep = diff < (qi - kb) * BLK
                        lom = jnp.where(keep, lom, 0.0)
                    lbs.append(z - sp)
                    keeps.append(keep)
                    scans.append(_scan_rows(lom, tri_p))
                avs, gws, hscans = [], [], []
                for lb, keep, (p_in, p_tot), da in zip(lbs, keeps, scans, das):
                    a = jnp.exp(lb + (tq - pc - p_in))
                    if masked:
                        a = jnp.where(keep, a, 0.0)
                    pc = pc + p_tot
                    gw = da * a
                    avs.append(a.astype(BF16))
                    gws.append(gw)
                    hscans.append(_scan_rows(gw, tri_h))
                dzs = []
                for lb, keep, gw, (h_in, g_tot) in zip(lbs, keeps, gws, hscans):
                    dz = gw - jnp.exp(lb) * (gw + hc + h_in)
                    if masked:
                        dz = jnp.where(keep, dz, 0.0)
                    hc = hc + g_tot
                    dzs.append(dz.astype(BF16))
                for dzb, kc in zip(dzs, kcs):
                    dq = dq + _dot(dzb, kc[:, :HEAD_DIM])
                for dzb, ab, cols in zip(dzs, avs, cols_of):
                    dkv_s[cols, :] += _dot_tn(jnp.concatenate([dzb, ab], axis=0), qd)
                return pc, hc, dq

            def q_block(qi, dqg):
                rows = pl.ds(pl.multiple_of(qi * BLK, BLK), BLK)
                qc, dob = qc_s[rows, :], dob_s[rows, :]
                none = jnp.zeros((BLK, HEAD_DIM), BF16)
                qd = jnp.concatenate([jnp.concatenate([qc[:, :HEAD_DIM], none], axis=1),
                                      jnp.concatenate([none, dob], axis=1)], axis=0)
                th = t_ref[rows, sl]
                tq = jnp.concatenate([th, th], axis=1)
                zero = jnp.zeros((BLK, BLK), F32)
                top = qi // KEY_GROUP
                pc, hc, dq = lax.fori_loop(
                    0, top, lambda grp, c: group(qc, qd, dob, tq, qi, grp, c[0], c[1], c[2], False),
                    (zero, zero, jnp.zeros((BLK, HEAD_DIM), F32)))
                _, _, dq = group(qc, qd, dob, tq, qi, top, pc, hc, dq, True)
                dq_raw, dg = norm_bwd(q_ref[rows, sl], qg_ref[...], dq * scale)
                dq_ref[rows, sl] = dq_raw.astype(BF16)
                return dqg + dg

            dqg = lax.fori_loop(0, n_blk, q_block, jnp.zeros((1, HEAD_DIM), F32))

            def finish(i, dkg):
                rows = pl.ds(pl.multiple_of(i * BLK, BLK), BLK)
                dk_raw, dg = norm_bwd(k_ref[rows, sl], kg_ref[...], dkv_s[rows, 0:HEAD_DIM])
                dk_ref[rows, sl] = dk_raw.astype(BF16)
                dv_ref[rows, sl] = dkv_s[rows, HEAD_DIM:2 * HEAD_DIM].astype(BF16)
                return dkg + dg

            dkg = lax.fori_loop(0, n_blk, finish, jnp.zeros((1, HEAD_DIM), F32))
            dqg_ref[0:1, 0:HEAD_DIM] += dqg
            dkg_ref[0:1, 0:HEAD_DIM] += dkg

    col = lambda off: pl.BlockSpec((s, LANES), lambda p: (0, off + p))
    vec = pl.BlockSpec((1, HEAD_DIM), lambda p: (0, 0))
    small = pl.BlockSpec((8, LANES), lambda p: (0, 0))
    return pl.pallas_call(
        body, name=name, grid=(pairs,),
        in_specs=[col(0), col(pairs), col(2 * pairs), col(0), col(0), vec, vec],
        out_specs=[col(0)] * 3 + [small] * 2,
        out_shape=[jax.ShapeDtypeStruct((s, ATTN_WIDTH), BF16)] * 3 + [jax.ShapeDtypeStruct((8, LANES), F32)] * 2,
        scratch_shapes=[pltpu.VMEM((s, 4 * HEAD_DIM), BF16)] * 2 + [pltpu.VMEM((s, HEAD_DIM), BF16)] * 2
        + [pltpu.VMEM((s, 2 * HEAD_DIM), F32)],
        compiler_params=_params(1),
    )(proj, proj, proj, dcat, tsum, qg, kg)


CONV_ROWS = 128


def _shifted(window, shift, halo):
    if shift == 0:
        return window[halo:, :]
    return pltpu.roll(window, shift, 0)[halo:, :]


def _conv_taps(u_s, r0, w_ref, rows):
    window = u_s[pl.ds(r0, CONV_HALO + rows), :]
    y = None
    for k in range(CONV_KERNEL):
        term = _shifted(window, CONV_KERNEL - 1 - k, CONV_HALO) * w_ref[k:k + 1, :]
        y = term if y is None else y + term
    return y


def _conv_fwd(proj, w, b, lg, lb, name):
    s = proj.shape[0]
    cw = w.shape[1]
    rows = CONV_ROWS
    blk_a = (proj.shape[1] - 2 * cw) // cw

    def body(a_ref, g_ref, w_ref, b_ref, lg_ref, lb_ref, o_ref, u_s):
        u_s[0:CONV_HALO, :] = jnp.zeros((CONV_HALO, cw), F32)

        def glu(i, _):
            r0 = pl.multiple_of(i * rows, rows)
            u_s[pl.ds(CONV_HALO + r0, rows), :] = a_ref[pl.ds(r0, rows), :] * _sigmoid(g_ref[pl.ds(r0, rows), :])
            return 0

        lax.fori_loop(0, s // rows, glu, 0)

        def chunk(i, _):
            r0 = pl.multiple_of(i * rows, rows)
            y = _conv_taps(u_s, r0, w_ref, rows) + b_ref[...]
            yc = y - jnp.mean(y, axis=-1, keepdims=True)
            n = yc * lax.rsqrt(jnp.mean(yc * yc, axis=-1, keepdims=True) + EPS)
            ln = n * lg_ref[...] + lb_ref[...]
            o_ref[pl.ds(r0, rows), :] = (ln * _sigmoid(ln)).astype(BF16)
            return 0

        lax.fori_loop(0, s // rows, chunk, 0)

    vec = pl.BlockSpec((1, cw), lambda i: (0, 0))
    return pl.pallas_call(
        body, name=name, grid=(1,),
        in_specs=[pl.BlockSpec((s, cw), lambda i: (0, blk_a)), pl.BlockSpec((s, cw), lambda i: (0, blk_a + 1)),
                  pl.BlockSpec((CONV_KERNEL, cw), lambda i: (0, 0)), vec, vec, vec],
        out_specs=pl.BlockSpec((s, cw), lambda i: (0, 0)),
        out_shape=jax.ShapeDtypeStruct((s, cw), BF16),
        scratch_shapes=[pltpu.VMEM((CONV_HALO + s, cw), F32)],
        compiler_params=_params(1),
    )(proj, proj, w, b, lg, lb)


def _conv_bwd(proj, dcat, w, b, lg, lb, name):
    s = proj.shape[0]
    cw = w.shape[1]
    rows = CONV_ROWS
    blk_a = (proj.shape[1] - 2 * cw) // cw
    n_chunk = s // rows

    def body(a_ref, g_ref, dc_ref, w_ref, b_ref, lg_ref, lb_ref, o_ref, dw_ref, db_ref, dlg_ref, dlb_ref,
             u_s, dy_s, dw_s):
        u_s[0:CONV_HALO, :] = jnp.zeros((CONV_HALO, cw), F32)
        dy_s[pl.ds(s, CONV_HALO), :] = jnp.zeros((CONV_HALO, cw), F32)
        dw_s[...] = jnp.zeros_like(dw_s)

        def glu(i, _):
            r0 = pl.multiple_of(i * rows, rows)
            u_s[pl.ds(CONV_HALO + r0, rows), :] = a_ref[pl.ds(r0, rows), :] * _sigmoid(g_ref[pl.ds(r0, rows), :])
            return 0

        lax.fori_loop(0, n_chunk, glu, 0)

        def chunk(i, carry):
            db, dlg, dlb = carry
            r0 = pl.multiple_of(i * rows, rows)
            window = u_s[pl.ds(r0, CONV_HALO + rows), :]
            y = None
            for k in range(CONV_KERNEL):
                term = _shifted(window, CONV_KERNEL - 1 - k, CONV_HALO) * w_ref[k:k + 1, :]
                y = term if y is None else y + term
            y = y + b_ref[...]
            yc = y - jnp.mean(y, axis=-1, keepdims=True)
            r = lax.rsqrt(jnp.mean(yc * yc, axis=-1, keepdims=True) + EPS)
            n = yc * r
            ln = n * lg_ref[...] + lb_ref[...]
            sg = _sigmoid(ln)
            dln = dc_ref[pl.ds(r0, rows), :] * (sg * (1.0 + ln * (1.0 - sg)))
            dn = dln * lg_ref[...]
            dy = r * (dn - jnp.mean(dn, axis=-1, keepdims=True) - n * jnp.mean(dn * n, axis=-1, keepdims=True))
            dy_s[pl.ds(r0, rows), :] = dy
            for k in range(CONV_KERNEL):
                prod = _shifted(window, CONV_KERNEL - 1 - k, CONV_HALO) * dy
                dw_s[k] += jnp.sum(prod.reshape(rows // 8, 8, cw), axis=0)
            return (db + jnp.sum(dy, axis=0, keepdims=True),
                    dlg + jnp.sum(dln * n, axis=0, keepdims=True),
                    dlb + jnp.sum(dln, axis=0, keepdims=True))

        zero = jnp.zeros((1, cw), F32)
        db, dlg, dlb = lax.fori_loop(0, n_chunk, chunk, (zero, zero, zero))
        db_ref[...] = db
        dlg_ref[...] = dlg
        dlb_ref[...] = dlb
        for k in range(CONV_KERNEL):
            dw_ref[k:k + 1, :] = jnp.sum(dw_s[k], axis=0, keepdims=True)

        def back(i, _):
            r0 = pl.multiple_of(i * rows, rows)
            window = dy_s[pl.ds(r0, rows + CONV_HALO), :]
            du = None
            for k in range(CONV_KERNEL):
                sh = CONV_KERNEL - 1 - k
                shifted = window[:rows, :] if sh == 0 else pltpu.roll(window, rows + CONV_HALO - sh, 0)[:rows, :]
                term = shifted * w_ref[k:k + 1, :]
                du = term if du is None else du + term
            av = a_ref[pl.ds(r0, rows), :]
            sg = _sigmoid(g_ref[pl.ds(r0, rows), :])
            o_ref[pl.ds(r0, rows), 0:cw] = (du * sg).astype(BF16)
            o_ref[pl.ds(r0, rows), cw:2 * cw] = (du * av * sg * (1.0 - sg)).astype(BF16)
            return 0

        lax.fori_loop(0, n_chunk, back, 0)

    vec = pl.BlockSpec((1, cw), lambda i: (0, 0))
    wspec = pl.BlockSpec((CONV_KERNEL, cw), lambda i: (0, 0))
    return pl.pallas_call(
        body, name=name, grid=(1,),
        in_specs=[pl.BlockSpec((s, cw), lambda i: (0, blk_a)), pl.BlockSpec((s, cw), lambda i: (0, blk_a + 1)),
                  pl.BlockSpec((s, cw), lambda i: (0, 1)), wspec, vec, vec, vec],
        out_specs=[pl.BlockSpec((s, 2 * cw), lambda i: (0, 0)), wspec, vec, vec, vec],
        out_shape=[jax.ShapeDtypeStruct((s, 2 * cw), BF16), jax.ShapeDtypeStruct((CONV_KERNEL, cw), F32)]
        + [jax.ShapeDtypeStruct((1, cw), F32)] * 3,
        scratch_shapes=[pltpu.VMEM((CONV_HALO + s, cw), F32), pltpu.VMEM((s + CONV_HALO, cw), F32),
                        pltpu.VMEM((CONV_KERNEL, 8, cw), F32)],
        compiler_params=_params(1),
    )(proj, proj, dcat, w, b, lg, lb)


FFN_ROWS = 256


def _ffn_gate(g_ref, r0, rows, w_ref, b_ref):
    cur = g_ref[pl.ds(r0, rows), :]
    prev = g_ref[pl.ds(pl.multiple_of(jnp.maximum(r0 - FFN_HALO, 0), FFN_HALO), FFN_HALO), :]
    prev = jnp.where(r0 > 0, prev, 0.0)
    window = jnp.concatenate([prev, cur], axis=0)
    gc = cur * w_ref[FFN_KERNEL - 1:FFN_KERNEL, :] + b_ref[...]
    for k in range(FFN_KERNEL - 1):
        gc = gc + _shifted(window, FFN_KERNEL - 1 - k, FFN_HALO) * w_ref[k:k + 1, :]
    return gc, window


def _ffn_fwd(up, w, b, name):
    s = up.shape[0]
    f = w.shape[1]
    tc = _pick(f, (256, 128))
    nc = f // tc
    rows = _pick(s, (FFN_ROWS, 128))

    def body(g_ref, v_ref, w_ref, b_ref, o_ref):
        def chunk(i, _):
            r0 = pl.multiple_of(i * rows, rows)
            gc, _w = _ffn_gate(g_ref, r0, rows, w_ref, b_ref)
            o_ref[pl.ds(r0, rows), :] = (gc * _sigmoid(gc) * v_ref[pl.ds(r0, rows), :]).astype(BF16)
            return 0

        lax.fori_loop(0, s // rows, chunk, 0)

    return pl.pallas_call(
        body, name=name, grid=(nc,),
        in_specs=[pl.BlockSpec((s, tc), lambda j: (0, j)), pl.BlockSpec((s, tc), lambda j: (0, nc + j)),
                  pl.BlockSpec((FFN_KERNEL, tc), lambda j: (0, j)), pl.BlockSpec((1, tc), lambda j: (0, j))],
        out_specs=pl.BlockSpec((s, tc), lambda j: (0, j)),
        out_shape=jax.ShapeDtypeStruct((s, f), BF16),
        compiler_params=_params(1),
    )(up, up, w, b)


def _ffn_bwd(up, dact, w, b, name):
    s = up.shape[0]
    f = w.shape[1]
    tc = _pick(f, (256, 128))
    nc = f // tc
    rows = _pick(s, (FFN_ROWS, 128))
    n_chunk = s // rows

    def body(g_ref, v_ref, da_ref, w_ref, b_ref, dg_ref, dv_ref, dw_ref, db_ref, dgc_s):
        dgc_s[pl.ds(s, FFN_HALO), :] = jnp.zeros((FFN_HALO, tc), F32)

        def chunk(i, carry):
            r0 = pl.multiple_of(i * rows, rows)
            gc, window = _ffn_gate(g_ref, r0, rows, w_ref, b_ref)
            sg = _sigmoid(gc)
            da = da_ref[pl.ds(r0, rows), :]
            dv_ref[pl.ds(r0, rows), :] = (da * gc * sg).astype(BF16)
            dgc = da * v_ref[pl.ds(r0, rows), :] * (sg * (1.0 + gc * (1.0 - sg)))
            dgc_s[pl.ds(r0, rows), :] = dgc
            out = [carry[0] + jnp.sum(dgc, axis=0, keepdims=True)]
            for k in range(FFN_KERNEL):
                out.append(carry[1 + k] + jnp.sum(_shifted(window, FFN_KERNEL - 1 - k, FFN_HALO) * dgc,
                                                  axis=0, keepdims=True))
            return tuple(out)

        zero = jnp.zeros((1, tc), F32)
        sums = lax.fori_loop(0, n_chunk, chunk, (zero,) * (1 + FFN_KERNEL))
        db_ref[...] = sums[0]
        for k in range(FFN_KERNEL):
            dw_ref[k:k + 1, :] = sums[1 + k]

        def back(i, _):
            r0 = pl.multiple_of(i * rows, rows)
            window = dgc_s[pl.ds(r0, rows + FFN_HALO), :]
            dg = window[:rows, :] * w_ref[FFN_KERNEL - 1:FFN_KERNEL, :]
            for k in range(FFN_KERNEL - 1):
                sh = FFN_KERNEL - 1 - k
                dg = dg + pltpu.roll(window, rows + FFN_HALO - sh, 0)[:rows, :] * w_ref[k:k + 1, :]
            dg_ref[pl.ds(r0, rows), :] = dg.astype(BF16)
            return 0

        lax.fori_loop(0, n_chunk, back, 0)

    blk = lambda off: pl.BlockSpec((s, tc), lambda j: (0, off + j))
    return pl.pallas_call(
        body, name=name, grid=(nc,),
        in_specs=[blk(0), blk(nc), blk(0), pl.BlockSpec((FFN_KERNEL, tc), lambda j: (0, j)),
                  pl.BlockSpec((1, tc), lambda j: (0, j))],
        out_specs=[blk(0), blk(0), pl.BlockSpec((FFN_KERNEL, tc), lambda j: (0, j)),
                   pl.BlockSpec((1, tc), lambda j: (0, j))],
        out_shape=[jax.ShapeDtypeStruct((s, f), BF16), jax.ShapeDtypeStruct((s, f), BF16),
                   jax.ShapeDtypeStruct((FFN_KERNEL, f), F32), jax.ShapeDtypeStruct((1, f), F32)],
        scratch_shapes=[pltpu.VMEM((s + FFN_HALO, tc), F32)],
        compiler_params=_params(1),
    )(up, up, dact, w, b)


def _loss_head(y, target, name):
    m, n = y.shape
    tm = _pick(m, (256, 128))

    def body(y_ref, t_ref, l_ref, d_ref, db_ref):
        e = y_ref[...] - t_ref[...]
        part = 0.5 * jnp.sum(jnp.sum(e * e, axis=-1, keepdims=True) / n, axis=0, keepdims=True)

        @pl.when(pl.program_id(0) == 0)
        def _():
            l_ref[...] = jnp.zeros_like(l_ref)

        l_ref[...] += part
        d = e / n
        d_ref[...] = d
        db_ref[...] = d.astype(BF16)

    return pl.pallas_call(
        body, name=name, grid=(m // tm,),
        in_specs=[pl.BlockSpec((tm, n), lambda i: (i, 0))] * 2,
        out_specs=[pl.BlockSpec((8, LANES), lambda i: (0, 0)), pl.BlockSpec((tm, n), lambda i: (i, 0)),
                   pl.BlockSpec((tm, n), lambda i: (i, 0))],
        out_shape=[jax.ShapeDtypeStruct((8, LANES), F32), jax.ShapeDtypeStruct((m, n), F32),
                   jax.ShapeDtypeStruct((m, n), BF16)],
        compiler_params=_params(1),
    )(y, target)


def _adamw_math(w, g, m, v):
    m = ADAM_B1 * m + (1.0 - ADAM_B1) * g
    v = ADAM_B2 * v + (1.0 - ADAM_B2) * (g * g)
    m_hat = m / (1.0 - ADAM_B1 ** ADAM_STEP)
    v_hat = v / (1.0 - ADAM_B2 ** ADAM_STEP)
    delta = -ADAM_LR * (m_hat / (jnp.sqrt(v_hat) + ADAM_EPS) + ADAM_WD * w)
    return delta, m, v


def _sum_adamw(parts, w, m, v, name):
    depth, r, c = w.shape
    tr = _pick(r, (256, 128, 32, 16))

    def body(p_ref, w_ref, m_ref, v_ref, g_out, d_out, m_out, v_out):
        g = p_ref[0, 0].astype(F32)
        for src in range(1, N_DEV):
            g = g + p_ref[src, 0].astype(F32)
        d, mn, vn = _adamw_math(w_ref[0], g, m_ref[0], v_ref[0])
        g_out[0] = g
        d_out[0] = d
        m_out[0] = mn
        v_out[0] = vn

    blk = pl.BlockSpec((1, tr, c), lambda l, i: (l, i, 0))
    return pl.pallas_call(
        body, name=name, grid=(depth, r // tr),
        in_specs=[pl.BlockSpec((N_DEV, 1, tr, c), lambda l, i: (0, l, i, 0)), blk, blk, blk],
        out_specs=[blk] * 4,
        out_shape=[jax.ShapeDtypeStruct(w.shape, F32)] * 4,
        compiler_params=_params(2),
    )(parts, w, m, v)


def _sum_rows(parts, name):
    _, r, c = parts.shape

    def body(p_ref, o_ref):
        g = p_ref[0]
        for src in range(1, N_DEV):
            g = g + p_ref[src]
        o_ref[...] = g

    return pl.pallas_call(
        body, name=name, grid=(1,),
        in_specs=[pl.BlockSpec((N_DEV, r, c), lambda i: (0, 0, 0))],
        out_specs=pl.BlockSpec((r, c), lambda i: (0, 0)),
        out_shape=jax.ShapeDtypeStruct((r, c), F32),
        compiler_params=_params(1),
    )(parts)


def _adamw_rows(w, g, m, v, name):
    r, c = w.shape

    def body(w_ref, g_ref, m_ref, v_ref, d_out, m_out, v_out):
        d, mn, vn = _adamw_math(w_ref[...], g_ref[...], m_ref[...], v_ref[...])
        d_out[...] = d
        m_out[...] = mn
        v_out[...] = vn

    blk = pl.BlockSpec((r, c), lambda i: (0, 0))
    return pl.pallas_call(
        body, name=name, grid=(1,), in_specs=[blk] * 4, out_specs=[blk] * 3,
        out_shape=[jax.ShapeDtypeStruct((r, c), F32)] * 3,
        compiler_params=_params(1),
    )(w, g, m, v)


def _exchange(parts, block_axis, out_axis, scatter, name):
    n = len(parts)

    def pick(ref, axis, idx):
        return ref.at[(slice(None),) * axis + (idx,)]

    out_shape = []
    for a, ba, oa in zip(parts, block_axis, out_axis):
        blk = list(a.shape)
        if scatter:
            del blk[ba]
        blk.insert(oa, N_DEV)
        out_shape.append(jax.ShapeDtypeStruct(tuple(blk), a.dtype))

    def body(*refs):
        ins, outs = refs[:n], refs[n:2 * n]
        send_sems, recv_sems, local_sems = refs[2 * n:]
        x, y, c = lax.axis_index("x"), lax.axis_index("y"), lax.axis_index("c")
        me = 4 * x + 2 * y + c
        copies = []
        for i in range(n):
            src = pick(ins[i], block_axis[i], me) if scatter else ins[i]
            cp = pltpu.make_async_copy(src, pick(outs[i], out_axis[i], me), local_sems.at[i])
            cp.start()
            copies.append(cp)
        for d in range(1, N_DEV):
            px = 1 - x if d & 4 else x
            py = 1 - y if d & 2 else y
            pc = 1 - c if d & 1 else c
            peer = 4 * px + 2 * py + pc
            for i in range(n):
                src = pick(ins[i], block_axis[i], peer) if scatter else ins[i]
                cp = pltpu.make_async_remote_copy(
                    src_ref=src, dst_ref=pick(outs[i], out_axis[i], me),
                    send_sem=send_sems.at[i, d - 1], recv_sem=recv_sems.at[i, d - 1],
                    device_id=(px, py, pc), device_id_type=pl.DeviceIdType.MESH)
                cp.start()
                copies.append(cp)
        for cp in copies:
            cp.wait()

    any_spec = pl.BlockSpec(memory_space=pl.ANY)
    return pl.pallas_call(
        body, name=name,
        in_specs=[any_spec] * n, out_specs=[any_spec] * n, out_shape=out_shape,
        scratch_shapes=[pltpu.SemaphoreType.DMA((n, N_DEV - 1)), pltpu.SemaphoreType.DMA((n, N_DEV - 1)),
                        pltpu.SemaphoreType.DMA((n,))],
    )(*parts)


PACK_TILE = 8 * LANES


def _pack(arrays):
    rows = []
    for a in arrays:
        flat = a.reshape(-1).astype(F32)
        pad = (-flat.shape[0]) % PACK_TILE
        rows.append(jnp.pad(flat, (0, pad)).reshape(-1, LANES))
    return jnp.concatenate(rows, axis=0)


def _unpack(packed, shapes, lead=()):
    out, r0 = [], 0
    for shp in shapes:
        size = 1
        for d in shp:
            size *= d
        nrows = -(-size // PACK_TILE) * 8
        piece = packed[..., r0:r0 + nrows, :].reshape(lead + (nrows * LANES,))[..., :size]
        out.append(piece.reshape(lead + tuple(shp)))
        r0 += nrows
    return out


def kernel(x, norm1_g, w_in, q_norm_g, k_norm_g, conv_dw_w, conv_dw_b, conv_ln_g, conv_ln_b, w_out, norm2_g, w_up, ffn_dw_w, ffn_dw_b, w_down, loss_target, m_norm1_g, m_w_in, m_q_norm_g, m_k_norm_g, m_conv_dw_w, m_conv_dw_b, m_conv_ln_g, m_conv_ln_b, m_w_out, m_norm2_g, m_w_up, m_ffn_dw_w, m_ffn_dw_b, m_w_down, v_norm1_g, v_w_in, v_q_norm_g, v_k_norm_g, v_conv_dw_w, v_conv_dw_b, v_conv_ln_g, v_conv_ln_b, v_w_out, v_norm2_g, v_w_up, v_ffn_dw_w, v_ffn_dw_b, v_w_down):
    depth, d_model, in_shard = w_in.shape
    out_shard = w_out.shape[1]
    up_shard = w_up.shape[2]
    down_shard = w_down.shape[1]
    d_ff = down_shard * N_DEV
    conv_w = conv_dw_b.shape[1]
    cw_shard = conv_dw_w.shape[2]
    fw_shard = ffn_dw_w.shape[2]
    me = 4 * lax.axis_index("x") + 2 * lax.axis_index("y") + lax.axis_index("c")

    g_in, g_out, g_up, g_down = _exchange(
        [w_in.astype(BF16), w_out.astype(BF16), w_up.astype(BF16), w_down.astype(BF16)],
        block_axis=[0, 0, 0, 0], out_axis=[0, 1, 0, 1], scatter=False, name="gather_weights")
    wf_in = g_in.transpose(1, 2, 0, 3).reshape(depth, d_model, N_DEV * in_shard)
    wf_out = g_out.reshape(depth, N_DEV * out_shard, d_model)
    wf_up = g_up.transpose(1, 2, 0, 3).reshape(depth, d_model, N_DEV * up_shard)
    wf_down = g_down.reshape(depth, d_ff, d_model)
    (g_small,) = _exchange([_pack([conv_dw_w, ffn_dw_w])], block_axis=[0], out_axis=[0], scatter=False,
                           name="gather_filters")
    g_cw, g_fw = _unpack(g_small, [conv_dw_w.shape, ffn_dw_w.shape], lead=(N_DEV,))
    cwf = g_cw.transpose(1, 2, 0, 3).reshape(depth, CONV_KERNEL, conv_w)
    fwf = g_fw.transpose(1, 2, 0, 3).reshape(depth, FFN_KERNEL, d_ff)

    row = lambda a, l: a[l].reshape(1, -1)

    xs = x[0]
    saved = []
    for l in range(depth):
        proj, h1 = _mm_rms(xs, row(norm1_g, l), wf_in[l], name="fwd_in")
        attn, tsum = _attn_fwd(proj, row(q_norm_g, l), row(k_norm_g, l), name="fwd_attn")
        conv = _conv_fwd(proj, cwf[l], row(conv_dw_b, l), row(conv_ln_g, l), row(conv_ln_b, l), name="fwd_conv")
        cat = jnp.concatenate([attn, conv], axis=1)
        x_mid = _mm_res(cat, wf_out[l], xs, name="fwd_out")
        up, h2 = _mm_rms(x_mid, row(norm2_g, l), wf_up[l], name="fwd_up")
        act = _ffn_fwd(up, fwf[l], row(ffn_dw_b, l), name="fwd_ffn")
        x_next = _mm_res(act, wf_down[l], x_mid, name="fwd_down")
        saved.append((xs, h1, proj, tsum, cat, x_mid, h2, up, act))
        xs = x_next

    loss_tile, dx, dxb = _loss_head(xs, loss_target[0], name="loss_head")
    loss = lax.psum(loss_tile[0, 0], ("x", "y", "c"))

    gw_in, gw_out, gw_up, gw_down = [None] * depth, [None] * depth, [None] * depth, [None] * depth
    small = {k: [None] * depth for k in ("norm1_g", "q_norm_g", "k_norm_g", "conv_dw_w", "conv_dw_b", "conv_ln_g",
                                         "conv_ln_b", "norm2_g", "ffn_dw_w", "ffn_dw_b")}
    for l in reversed(range(depth)):
        xs, h1, proj, tsum, cat, x_mid, h2, up, act = saved[l]
        dact = _mm_nt(dxb, wf_down[l], name="bwd_dact")
        gw_down[l] = _mm_tn(act, dxb, name="bwd_gw_down")
        dgate, dval, small["ffn_dw_w"][l], small["ffn_dw_b"][l] = _ffn_bwd(
            up, dact, fwf[l], row(ffn_dw_b, l), name="bwd_ffn")
        dup = jnp.concatenate([dgate, dval], axis=1)
        gw_up[l] = _mm_tn(h2, dup, name="bwd_gw_up")
        dx, dxb, small["norm2_g"][l] = _mm_nt_rmsbwd(dup, wf_up[l], x_mid, row(norm2_g, l), dx, name="bwd_up")
        dcat = _mm_nt(dxb, wf_out[l], name="bwd_dcat")
        gw_out[l] = _mm_tn(cat, dxb, name="bwd_gw_out")
        dglu, small["conv_dw_w"][l], small["conv_dw_b"][l], small["conv_ln_g"][l], small["conv_ln_b"][l] = _conv_bwd(
            proj, dcat, cwf[l], row(conv_dw_b, l), row(conv_ln_g, l), row(conv_ln_b, l), name="bwd_conv")
        dq, dk, dv, dqg, dkg = _attn_bwd(proj, dcat, tsum, row(q_norm_g, l), row(k_norm_g, l), name="bwd_attn")
        small["q_norm_g"][l] = dqg[0:1, 0:HEAD_DIM]
        small["k_norm_g"][l] = dkg[0:1, 0:HEAD_DIM]
        dproj = jnp.concatenate([dq, dk, dv, dglu], axis=1)
        gw_in[l] = _mm_tn(h1, dproj, name="bwd_gw_in")
        dx, dxb, small["norm1_g"][l] = _mm_nt_rmsbwd(dproj, wf_in[l], xs, row(norm1_g, l), dx, name="bwd_in")
    grad_x = dx[None]

    s_in = jnp.stack(gw_in).reshape(depth, d_model, N_DEV, in_shard).transpose(2, 0, 1, 3)
    s_out = jnp.stack(gw_out).reshape(depth, N_DEV, out_shard, d_model)
    s_up = jnp.stack(gw_up).reshape(depth, d_model, N_DEV, up_shard).transpose(2, 0, 1, 3)
    s_down = jnp.stack(gw_down).reshape(depth, N_DEV, down_shard, d_model)
    r_in, r_out, r_up, r_down = _exchange([s_in, s_out, s_up, s_down], block_axis=[0, 1, 0, 1],
                                          out_axis=[0, 0, 0, 0], scatter=True, name="exchange_grads")
    big = {
        "w_in": _sum_adamw(r_in, w_in, m_w_in, v_w_in, name="adamw_in"),
        "w_out": _sum_adamw(r_out, w_out, m_w_out, v_w_out, name="adamw_out"),
        "w_up": _sum_adamw(r_up, w_up, m_w_up, v_w_up, name="adamw_up"),
        "w_down": _sum_adamw(r_down, w_down, m_w_down, v_w_down, name="adamw_down"),
    }

    names = ["norm1_g", "q_norm_g", "k_norm_g", "conv_dw_w", "conv_dw_b", "conv_ln_g", "conv_ln_b", "norm2_g",
             "ffn_dw_w", "ffn_dw_b"]
    full_shapes = {"norm1_g": norm1_g.shape, "q_norm_g": q_norm_g.shape, "k_norm_g": k_norm_g.shape,
                   "conv_dw_w": (depth, CONV_KERNEL, conv_w), "conv_dw_b": conv_dw_b.shape,
                   "conv_ln_g": conv_ln_g.shape, "conv_ln_b": conv_ln_b.shape, "norm2_g": norm2_g.shape,
                   "ffn_dw_w": (depth, FFN_KERNEL, d_ff), "ffn_dw_b": ffn_dw_b.shape}
    partial = _pack([jnp.stack(small[k]).reshape(full_shapes[k]) for k in names])
    (all_partials,) = _exchange([partial], block_axis=[0], out_axis=[0], scatter=False, name="gather_small_grads")
    total = _unpack(_sum_rows(all_partials, name="sum_small_grads"), [full_shapes[k] for k in names])
    grads = dict(zip(names, total))
    grads["conv_dw_w"] = lax.dynamic_slice_in_dim(grads["conv_dw_w"], me * cw_shard, cw_shard, axis=2)
    grads["ffn_dw_w"] = lax.dynamic_slice_in_dim(grads["ffn_dw_w"], me * fw_shard, fw_shard, axis=2)
    weights = dict(norm1_g=norm1_g, q_norm_g=q_norm_g, k_norm_g=k_norm_g, conv_dw_w=conv_dw_w, conv_dw_b=conv_dw_b,
                   conv_ln_g=conv_ln_g, conv_ln_b=conv_ln_b, norm2_g=norm2_g, ffn_dw_w=ffn_dw_w, ffn_dw_b=ffn_dw_b)
    m_in = dict(norm1_g=m_norm1_g, q_norm_g=m_q_norm_g, k_norm_g=m_k_norm_g, conv_dw_w=m_conv_dw_w,
                conv_dw_b=m_conv_dw_b, conv_ln_g=m_conv_ln_g, conv_ln_b=m_conv_ln_b, norm2_g=m_norm2_g,
                ffn_dw_w=m_ffn_dw_w, ffn_dw_b=m_ffn_dw_b)
    v_in = dict(norm1_g=v_norm1_g, q_norm_g=v_q_norm_g, k_norm_g=v_k_norm_g, conv_dw_w=v_conv_dw_w,
                conv_dw_b=v_conv_dw_b, conv_ln_g=v_conv_ln_g, conv_ln_b=v_conv_ln_b, norm2_g=v_norm2_g,
                ffn_dw_w=v_ffn_dw_w, ffn_dw_b=v_ffn_dw_b)
    shard_shapes = [weights[k].shape for k in names]
    d_s, m_s, v_s = _adamw_rows(_pack([weights[k] for k in names]), _pack([grads[k] for k in names]),
                                _pack([m_in[k] for k in names]), _pack([v_in[k] for k in names]), name="adamw_small")
    delta = dict(zip(names, _unpack(d_s, shard_shapes)))
    new_m = dict(zip(names, _unpack(m_s, shard_shapes)))
    new_v = dict(zip(names, _unpack(v_s, shard_shapes)))
    for k, (g, d, mn, vn) in big.items():
        grads[k], delta[k], new_m[k], new_v[k] = g, d, mn, vn

    order = ["norm1_g", "w_in", "q_norm_g", "k_norm_g", "conv_dw_w", "conv_dw_b", "conv_ln_g", "conv_ln_b", "w_out",
             "norm2_g", "w_up", "ffn_dw_w", "ffn_dw_b", "w_down"]
    return (loss, grad_x, *[grads[k] for k in order], *[delta[k] for k in order], *[new_m[k] for k in order],
            *[new_v[k] for k in order])
```

```python
import functools

import jax
import jax.numpy as jnp
from jax import lax
from jax.experimental import pallas as pl
from jax.experimental.pallas import tpu as pltpu

F32 = jnp.float32
BF16 = jnp.bfloat16

N_DEV = 8
HEADS = 8
HEAD_DIM = 64
ATTN_WIDTH = HEADS * HEAD_DIM
CONV_KERNEL = 31
FFN_KERNEL = 3
EPS = 1e-6
BLK = 128
KEY_GROUP = 4
LANES = 128
NORM_ROWS = 128
CONV_HALO = 32
FFN_HALO = 8

ADAM_LR = 0.001
ADAM_B1 = 0.9
ADAM_B2 = 0.999
ADAM_EPS = 1e-08
ADAM_WD = 0.01
ADAM_STEP = 10

VMEM_LIMIT = 56 * 1024 * 1024


def _params(n_axes=0):
    kw = dict(vmem_limit_bytes=VMEM_LIMIT)
    if n_axes:
        kw["dimension_semantics"] = ("arbitrary",) * n_axes
    return pltpu.CompilerParams(**kw)


def _dot(a, b):
    return jnp.dot(a, b, preferred_element_type=F32)


def _dot_nt(a, b):
    return lax.dot_general(a, b, (((1,), (1,)), ((), ())), preferred_element_type=F32)


def _dot_tn(a, b):
    return lax.dot_general(a, b, (((0,), (0,)), ((), ())), preferred_element_type=F32)


def _sigmoid(x):
    return 1.0 / (1.0 + jnp.exp(-x))


def _split_bf16(x):
    hi = x.astype(BF16)
    lo = (x - hi.astype(F32)).astype(BF16)
    return hi, lo


def _pick(n, options):
    for t in options:
        if n % t == 0:
            return t
    return n


def _tile(n, cap):
    best = None
    for t in range(LANES, min(n, cap) + 1, LANES):
        if n % t == 0:
            best = t
    return best or n


def _mm_rms(x, g, w, name):
    m, k = x.shape
    n = w.shape[1]
    tm = _tile(m, 2048)
    tn = _tile(n, 512)

    def body(x_ref, g_ref, w_ref, o_ref, h_ref, h_s):
        @pl.when(pl.program_id(1) == 0)
        def _():
            def chunk(c, _):
                rows = pl.ds(pl.multiple_of(c * NORM_ROWS, NORM_ROWS), NORM_ROWS)
                xv = x_ref[rows, :]
                r = lax.rsqrt(jnp.mean(xv * xv, axis=-1, keepdims=True) + EPS)
                hv = (xv * r * g_ref[...]).astype(BF16)
                h_s[rows, :] = hv
                h_ref[rows, :] = hv
                return 0

            lax.fori_loop(0, tm // NORM_ROWS, chunk, 0)

        o_ref[...] = _dot(h_s[...], w_ref[...])

    return pl.pallas_call(
        body, name=name, grid=(m // tm, n // tn),
        in_specs=[pl.BlockSpec((tm, k), lambda i, j: (i, 0)),
                  pl.BlockSpec((1, k), lambda i, j: (0, 0)),
                  pl.BlockSpec((k, tn), lambda i, j: (0, j))],
        out_specs=[pl.BlockSpec((tm, tn), lambda i, j: (i, j)),
                   pl.BlockSpec((tm, k), lambda i, j: (i, 0))],
        out_shape=[jax.ShapeDtypeStruct((m, n), F32), jax.ShapeDtypeStruct((m, k), BF16)],
        scratch_shapes=[pltpu.VMEM((tm, k), BF16)],
        compiler_params=_params(2),
    )(x, g, w)


def _mm_res(a, w, res, name):
    m, k = a.shape
    n = w.shape[1]
    tm = _tile(m, 1024)
    tn = _tile(n, 512)

    def body(a_ref, w_ref, r_ref, o_ref):
        o_ref[...] = r_ref[...] + _dot(a_ref[...], w_ref[...])

    return pl.pallas_call(
        body, name=name, grid=(m // tm, n // tn),
        in_specs=[pl.BlockSpec((tm, k), lambda i, j: (i, 0)),
                  pl.BlockSpec((k, tn), lambda i, j: (0, j)),
                  pl.BlockSpec((tm, tn), lambda i, j: (i, j))],
        out_specs=pl.BlockSpec((tm, tn), lambda i, j: (i, j)),
        out_shape=jax.ShapeDtypeStruct((m, n), F32),
        compiler_params=_params(2),
    )(a, w, res)


def _mm_nt(a, w, name):
    m, k = a.shape
    n = w.shape[0]
    tm = _tile(m, 1024)
    tn = _tile(n, 1408)

    def body(a_ref, w_ref, o_ref):
        o_ref[...] = _dot_nt(a_ref[...], w_ref[...])

    return pl.pallas_call(
        body, name=name, grid=(m // tm, n // tn),
        in_specs=[pl.BlockSpec((tm, k), lambda i, j: (i, 0)),
                  pl.BlockSpec((tn, k), lambda i, j: (j, 0))],
        out_specs=pl.BlockSpec((tm, tn), lambda i, j: (i, j)),
        out_shape=jax.ShapeDtypeStruct((m, n), F32),
        compiler_params=_params(2),
    )(a, w)


def _mm_tn(a, b, name):
    s, m = a.shape
    n = b.shape[1]
    tm = _tile(m, 1408)
    tn = _tile(n, 512)

    def body(a_ref, b_ref, o_ref):
        o_ref[...] = _dot_tn(a_ref[...], b_ref[...]).astype(BF16)

    return pl.pallas_call(
        body, name=name, grid=(m // tm, n // tn),
        in_specs=[pl.BlockSpec((s, tm), lambda i, j: (0, i)),
                  pl.BlockSpec((s, tn), lambda i, j: (0, j))],
        out_specs=pl.BlockSpec((tm, tn), lambda i, j: (i, j)),
        out_shape=jax.ShapeDtypeStruct((m, n), BF16),
        compiler_params=_params(2),
    )(a, b)


def _mm_nt_rmsbwd(a, w, x, g, dres, name):
    m, k = a.shape
    n = w.shape[0]
    tm = _tile(m, 1024)
    tk = _tile(k, 1408)
    nk = k // tk

    def body(a_ref, w_ref, x_ref, g_ref, r_ref, dx_ref, dxb_ref, dg_ref, acc):
        i, kk = pl.program_id(0), pl.program_id(1)
        part = _dot_nt(a_ref[...], w_ref[...])

        @pl.when(kk == 0)
        def _():
            acc[...] = part

        @pl.when(kk > 0)
        def _():
            acc[...] += part

        @pl.when(kk == nk - 1)
        def _():
            def chunk(c, dgp):
                rows = pl.ds(pl.multiple_of(c * NORM_ROWS, NORM_ROWS), NORM_ROWS)
                dh = acc[rows, :]
                xv = x_ref[rows, :]
                r = lax.rsqrt(jnp.mean(xv * xv, axis=-1, keepdims=True) + EPS)
                xh = xv * r
                dxh = dh * g_ref[...]
                dx = r_ref[rows, :] + r * (dxh - xh * jnp.mean(dxh * xh, axis=-1, keepdims=True))
                dx_ref[rows, :] = dx
                dxb_ref[rows, :] = dx.astype(BF16)
                return dgp + jnp.sum(dh * xh, axis=0, keepdims=True)

            dgp = lax.fori_loop(0, tm // NORM_ROWS, chunk, jnp.zeros((1, n), F32))

            @pl.when(i == 0)
            def _():
                dg_ref[...] = dgp

            @pl.when(i > 0)
            def _():
                dg_ref[...] += dgp

    return pl.pallas_call(
        body, name=name, grid=(m // tm, nk),
        in_specs=[pl.BlockSpec((tm, tk), lambda i, kk: (i, kk)),
                  pl.BlockSpec((n, tk), lambda i, kk: (0, kk)),
                  pl.BlockSpec((tm, n), lambda i, kk: (i, 0)),
                  pl.BlockSpec((1, n), lambda i, kk: (0, 0)),
                  pl.BlockSpec((tm, n), lambda i, kk: (i, 0))],
        out_specs=[pl.BlockSpec((tm, n), lambda i, kk: (i, 0)),
                   pl.BlockSpec((tm, n), lambda i, kk: (i, 0)),
                   pl.BlockSpec((1, n), lambda i, kk: (0, 0))],
        out_shape=[jax.ShapeDtypeStruct((m, n), F32), jax.ShapeDtypeStruct((m, n), BF16),
                   jax.ShapeDtypeStruct((1, n), F32)],
        scratch_shapes=[pltpu.VMEM((tm, n), F32)],
        compiler_params=_params(2),
    )(a, w, x, g, dres)


ANY_SPEC = pl.BlockSpec(memory_space=pl.ANY)
SEMS_PER_OPERAND = N_DEV - 1


def _exchange_sems(n):
    return [pltpu.SemaphoreType.DMA((n, SEMS_PER_OPERAND)), pltpu.SemaphoreType.DMA((n, SEMS_PER_OPERAND)),
            pltpu.SemaphoreType.DMA((n,))]


def _exchange_shapes(parts, scatter):
    return [jax.ShapeDtypeStruct(a.shape if sc else (N_DEV,) + a.shape, a.dtype) for a, sc in zip(parts, scatter)]


def _flat(pos):
    return 4 * pos[0] + 2 * pos[1] + pos[2]


def _remote(src, dst, sems, i, k, to):
    send_sems, recv_sems, _ = sems
    return pltpu.make_async_remote_copy(src_ref=src, dst_ref=dst, send_sem=send_sems.at[i, k],
                                        recv_sem=recv_sems.at[i, k], device_id=to,
                                        device_id_type=pl.DeviceIdType.MESH)


def _direct_copies(ins, outs, scatter, sems):
    x, y, c = lax.axis_index("x"), lax.axis_index("y"), lax.axis_index("c")
    me = _flat((x, y, c))
    copies = []
    for i in range(len(ins)):
        src = ins[i].at[me] if scatter[i] else ins[i]
        copies.append(pltpu.make_async_copy(src, outs[i].at[me], sems[2].at[i]))
    for d in range(1, N_DEV):
        peer = (1 - x if d & 4 else x, 1 - y if d & 2 else y, 1 - c if d & 1 else c)
        for i in range(len(ins)):
            src = ins[i].at[_flat(peer)] if scatter[i] else ins[i]
            copies.append(_remote(src, outs[i].at[me], sems, i, d - 1, peer))
    return copies


def _two_level_gather(ins, outs, sems):
    x, y, c = lax.axis_index("x"), lax.axis_index("y"), lax.axis_index("c")
    me, sibling = (x, y, c), (x, y, 1 - c)
    chips = [(1 - x, y), (x, 1 - y), (1 - x, 1 - y)]
    n = len(ins)

    def block(i, pos):
        return outs[i].at[_flat(pos)]

    local = [pltpu.make_async_copy(ins[i], block(i, me), sems[2].at[i]) for i in range(n)]
    own = [_remote(ins[i], block(i, me), sems, i, 0, sibling) for i in range(n)]
    own += [_remote(ins[i], block(i, me), sems, i, 1 + j, (*chip, c)) for i in range(n) for j, chip in enumerate(chips)]
    passed = [[_remote(block(i, (*chip, c)), block(i, (*chip, c)), sems, i, 4 + j, sibling) for i in range(n)]
              for j, chip in enumerate(chips)]

    def first():
        for cp in local + own:
            cp.start()

    def relay():
        for j, chip in enumerate(chips):
            for i in range(n):
                _remote(ins[i], block(i, (*chip, c)), sems, i, 1 + j, me).wait_recv()
                passed[j][i].start()

    def finish():
        for i in range(n):
            _remote(ins[i], block(i, sibling), sems, i, 0, me).wait_recv()
            for j, chip in enumerate(chips):
                _remote(ins[i], block(i, (*chip, 1 - c)), sems, i, 4 + j, me).wait_recv()
        for cp in own + [cp for row in passed for cp in row]:
            cp.wait_send()
        for cp in local:
            cp.wait()

    return first, relay, finish


def _exchange(parts, scatter, name):
    n = len(parts)

    def body(*refs):
        copies = _direct_copies(refs[:n], refs[n:2 * n], scatter, refs[2 * n:])
        for cp in copies:
            cp.start()
        for cp in copies:
            cp.wait()

    return pl.pallas_call(
        body, name=name, in_specs=[ANY_SPEC] * n, out_specs=[ANY_SPEC] * n,
        out_shape=_exchange_shapes(parts, scatter), scratch_shapes=_exchange_sems(n),
    )(*parts)


def _tri(kind):
    j = lax.broadcasted_iota(jnp.int32, (BLK, BLK), 0)
    s = lax.broadcasted_iota(jnp.int32, (BLK, BLK), 1)
    m = {"after": j > s, "upto": j <= s, "before": j < s}[kind]
    half = jnp.concatenate([jnp.where(m, 1.0, 0.0), jnp.ones((BLK, BLK), F32)], axis=1).astype(BF16)
    return jnp.concatenate([half, half], axis=0)


def _scan_rows(v, tri):
    hi, lo = _split_bf16(v)
    r = _dot(jnp.concatenate([hi, lo], axis=1), tri)
    return r[:, :BLK], r[:, BLK:]


def _head_norm(v, g):
    r = lax.rsqrt(jnp.mean(v * v, axis=-1, keepdims=True) + EPS)
    return v * r * g


def _attn_prep(q_ref, k_ref, v_ref, qg_ref, kg_ref, sl, qc_s, kc_s, vb_s, n_blk):
    scale = HEAD_DIM ** -0.5

    def prep(i, _):
        rows = pl.ds(pl.multiple_of(i * BLK, BLK), BLK)
        qh, ql = _split_bf16(_head_norm(q_ref[rows, sl], qg_ref[...]) * scale)
        kh, kl = _split_bf16(_head_norm(k_ref[rows, sl], kg_ref[...]))
        qc_s[rows, :] = jnp.concatenate([qh, ql, qh, ql], axis=1)
        kc_s[rows, :] = jnp.concatenate([kh, kh, kl, kl], axis=1)
        vb_s[rows, :] = v_ref[rows, sl].astype(BF16)
        return 0

    lax.fori_loop(0, n_blk, prep, 0)


def _col_minus_row():
    row = lax.broadcasted_iota(jnp.int32, (BLK, BLK), 0)
    col = lax.broadcasted_iota(jnp.int32, (BLK, BLK), 1)
    return col - row


def _softplus(z):
    return jnp.maximum(z, 0.0) + jnp.log(1.0 + jnp.exp(-jnp.abs(z)))


def _attn_fwd(proj, qg, kg, sends, name):
    s = proj.shape[0]
    n_blk = s // BLK
    pairs = ATTN_WIDTH // LANES
    n_send = len(sends)

    def body(q_ref, k_ref, v_ref, qg_ref, kg_ref, *rest):
        send_refs, (o_ref, t_ref), rest = rest[:n_send], rest[n_send:n_send + 2], rest[n_send + 2:]
        got_refs, (qc_s, kc_s, vb_s), sems = rest[:n_send], rest[n_send:n_send + 3], rest[n_send + 3:]
        first, relay, finish = _two_level_gather(send_refs, got_refs, sems)
        step = pl.program_id(0)
        pl.when(step == 0)(first)
        pl.when(step == pairs - 1)(relay)
        tri = _tri("after")
        diff = _col_minus_row()
        for hh in range(LANES // HEAD_DIM):
            sl = slice(hh * HEAD_DIM, (hh + 1) * HEAD_DIM)
            _attn_prep(q_ref, k_ref, v_ref, qg_ref, kg_ref, sl, qc_s, kc_s, vb_s, n_blk)

            def group(qc, qi, grp, carry, acc, masked):
                blocks = [grp * KEY_GROUP + j for j in reversed(range(KEY_GROUP))]
                cols_of = [pl.ds(pl.multiple_of(kb * BLK, BLK), BLK) for kb in blocks]
                zs = [_dot_nt(qc, kc_s[cols, :]) for cols in cols_of]
                parts = []
                for kb, cols, z in zip(blocks, cols_of, zs):
                    sp = _softplus(z)
                    lom = -sp
                    keep = None
                    if masked:
                        keep = diff < (qi - kb) * BLK
                        lom = jnp.where(keep, lom, 0.0)
                    tail, tot = _scan_rows(lom, tri)
                    parts.append((z - sp + tail, tot, keep, cols))
                for lw, tot, keep, cols in parts:
                    w = jnp.exp(lw + carry)
                    if masked:
                        w = jnp.where(keep, w, 0.0)
                    acc = acc + _dot(w.astype(BF16), vb_s[cols, :])
                    carry = carry + tot
                return carry, acc

            def q_block(qi, _):
                rows = pl.ds(pl.multiple_of(qi * BLK, BLK), BLK)
                qc = qc_s[rows, :]
                top = qi // KEY_GROUP
                carry, acc = group(qc, qi, top, jnp.zeros((BLK, BLK), F32), jnp.zeros((BLK, HEAD_DIM), F32), True)
                carry, acc = lax.fori_loop(
                    0, top, lambda t, c: group(qc, qi, top - 1 - t, c[0], c[1], False), (carry, acc))
                o_ref[rows, sl] = acc.astype(BF16)
                t_ref[rows, sl] = carry[:, :HEAD_DIM]
                return 0

            lax.fori_loop(0, n_blk, q_block, 0)

        pl.when(step == pairs - 1)(finish)

    col = lambda off: pl.BlockSpec((s, LANES), lambda p: (0, off + p))
    vec = pl.BlockSpec((1, HEAD_DIM), lambda p: (0, 0))
    out = pl.pallas_call(
        body, name=name, grid=(pairs,),
        in_specs=[col(0), col(pairs), col(2 * pairs), vec, vec] + [ANY_SPEC] * n_send,
        out_specs=[pl.BlockSpec((s, LANES), lambda p: (0, p))] * 2 + [ANY_SPEC] * n_send,
        out_shape=[jax.ShapeDtypeStruct((s, ATTN_WIDTH), BF16), jax.ShapeDtypeStruct((s, ATTN_WIDTH), F32)]
        + _exchange_shapes(sends, [False] * n_send),
        scratch_shapes=[pltpu.VMEM((s, 4 * HEAD_DIM), BF16)] * 2 + [pltpu.VMEM((s, HEAD_DIM), BF16)]
        + _exchange_sems(n_send),
        compiler_params=_params(1),
    )(proj, proj, proj, qg, kg, *sends)
    return out[0], out[1], out[2:]


def _attn_bwd(proj, dcat, tsum, qg, kg, sends, name):
    s = proj.shape[0]
    n_blk = s // BLK
    pairs = ATTN_WIDTH // LANES
    scale = HEAD_DIM ** -0.5
    n_send = len(sends)

    def norm_bwd(raw, g, dn):
        r = lax.rsqrt(jnp.mean(raw * raw, axis=-1, keepdims=True) + EPS)
        xh = raw * r
        dg = jnp.sum(dn * xh, axis=0, keepdims=True)
        dxh = dn * g
        return r * (dxh - xh * jnp.mean(dxh * xh, axis=-1, keepdims=True)), dg

    def body(q_ref, k_ref, v_ref, do_ref, t_ref, qg_ref, kg_ref, *rest):
        send_refs, rest = rest[:n_send], rest[n_send:]
        (dq_ref, dk_ref, dv_ref, dqg_ref, dkg_ref), rest = rest[:5], rest[5:]
        got_refs, (qc_s, kc_s, vb_s, dob_s, dkv_s), sems = rest[:n_send], rest[n_send:n_send + 5], rest[n_send + 5:]
        copies = _direct_copies(send_refs, got_refs, [True] * n_send, sems)

        @pl.when(pl.program_id(0) == 0)
        def _():
            for cp in copies:
                cp.start()

        tri_p = _tri("upto")
        tri_h = _tri("before")
        diff = _col_minus_row()

        @pl.when(pl.program_id(0) == 0)
        def _():
            dqg_ref[...] = jnp.zeros_like(dqg_ref)
            dkg_ref[...] = jnp.zeros_like(dkg_ref)

        for hh in range(LANES // HEAD_DIM):
            sl = slice(hh * HEAD_DIM, (hh + 1) * HEAD_DIM)
            _attn_prep(q_ref, k_ref, v_ref, qg_ref, kg_ref, sl, qc_s, kc_s, vb_s, n_blk)

            def prep(i, _):
                rows = pl.ds(pl.multiple_of(i * BLK, BLK), BLK)
                dob_s[rows, :] = do_ref[rows, sl].astype(BF16)
                dkv_s[rows, :] = jnp.zeros((BLK, 2 * HEAD_DIM), F32)
                return 0

            lax.fori_loop(0, n_blk, prep, 0)

            def group(qc, qd, dob, tq, qi, grp, pc, hc, dq, masked):
                blocks = [grp * KEY_GROUP + j for j in range(KEY_GROUP)]
                cols_of = [pl.ds(pl.multiple_of(kb * BLK, BLK), BLK) for kb in blocks]
                kcs = [kc_s[cols, :] for cols in cols_of]
                zs = [_dot_nt(qc, kc) for kc in kcs]
                das = [_dot_nt(dob, vb_s[cols, :]) for cols in cols_of]
                lbs, keeps, scans = [], [], []
                for kb, z in zip(blocks, zs):
                    sp = _softplus(z)
                    lom = -sp
                    keep = None
                    if masked:
                        keep = diff < (qi - kb) * BLK
                        lom = jnp.where(keep, lom, 0.0)
                    lbs.append(z - sp)
                    keeps.append(keep)
                    scans.append(_scan_rows(lom, tri_p))
                avs, gws, hscans = [], [], []
                for lb, keep, (p_in, p_tot), da in zip(lbs, keeps, scans, das):
                    a = jnp.exp(lb + (tq - pc - p_in))
                    if masked:
                        a = jnp.where(keep, a, 0.0)
                    pc = pc + p_tot
                    gw = da * a
                    avs.append(a.astype(BF16))
                    gws.append(gw)
                    hscans.append(_scan_rows(gw, tri_h))
                dzs = []
                for lb, keep, gw, (h_in, g_tot) in zip(lbs, keeps, gws, hscans):
                    dz = gw - jnp.exp(lb) * (gw + hc + h_in)
                    if masked:
                        dz = jnp.where(keep, dz, 0.0)
                    hc = hc + g_tot
                    dzs.append(dz.astype(BF16))
                for dzb, kc in zip(dzs, kcs):
                    dq = dq + _dot(dzb, kc[:, :HEAD_DIM])
                for dzb, ab, cols in zip(dzs, avs, cols_of):
                    dkv_s[cols, :] += _dot_tn(jnp.concatenate([dzb, ab], axis=0), qd)
                return pc, hc, dq

            def q_block(qi, dqg):
                rows = pl.ds(pl.multiple_of(qi * BLK, BLK), BLK)
                qc, dob = qc_s[rows, :], dob_s[rows, :]
                none = jnp.zeros((BLK, HEAD_DIM), BF16)
                qd = jnp.concatenate([jnp.concatenate([qc[:, :HEAD_DIM], none], axis=1),
                                      jnp.concatenate([none, dob], axis=1)], axis=0)
                th = t_ref[rows, sl]
                tq = jnp.concatenate([th, th], axis=1)
                zero = jnp.zeros((BLK, BLK), F32)
                top = qi // KEY_GROUP
                pc, hc, dq = lax.fori_loop(
                    0, top, lambda grp, c: group(qc, qd, dob, tq, qi, grp, c[0], c[1], c[2], False),
                    (zero, zero, jnp.zeros((BLK, HEAD_DIM), F32)))
                _, _, dq = group(qc, qd, dob, tq, qi, top, pc, hc, dq, True)
                dq_raw, dg = norm_bwd(q_ref[rows, sl], qg_ref[...], dq * scale)
                dq_ref[rows, sl] = dq_raw.astype(BF16)
                return dqg + dg

            dqg = lax.fori_loop(0, n_blk, q_block, jnp.zeros((1, HEAD_DIM), F32))

            def finish(i, dkg):
                rows = pl.ds(pl.multiple_of(i * BLK, BLK), BLK)
                dk_raw, dg = norm_bwd(k_ref[rows, sl], kg_ref[...], dkv_s[rows, 0:HEAD_DIM])
                dk_ref[rows, sl] = dk_raw.astype(BF16)
                dv_ref[rows, sl] = dkv_s[rows, HEAD_DIM:2 * HEAD_DIM].astype(BF16)
                return dkg + dg

            dkg = lax.fori_loop(0, n_blk, finish, jnp.zeros((1, HEAD_DIM), F32))
            dqg_ref[0:1, 0:HEAD_DIM] += dqg
            dkg_ref[0:1, 0:HEAD_DIM] += dkg

        @pl.when(pl.program_id(0) == pairs - 1)
        def _():
            for cp in copies:
                cp.wait()

    col = lambda off: pl.BlockSpec((s, LANES), lambda p: (0, off + p))
    vec = pl.BlockSpec((1, HEAD_DIM), lambda p: (0, 0))
    small = pl.BlockSpec((8, LANES), lambda p: (0, 0))
    out = pl.pallas_call(
        body, name=name, grid=(pairs,),
        in_specs=[col(0), col(pairs), col(2 * pairs), col(0), col(0), vec, vec] + [ANY_SPEC] * n_send,
        out_specs=[col(0)] * 3 + [small] * 2 + [ANY_SPEC] * n_send,
        out_shape=[jax.ShapeDtypeStruct((s, ATTN_WIDTH), BF16)] * 3 + [jax.ShapeDtypeStruct((8, LANES), F32)] * 2
        + _exchange_shapes(sends, [True] * n_send),
        scratch_shapes=[pltpu.VMEM((s, 4 * HEAD_DIM), BF16)] * 2 + [pltpu.VMEM((s, HEAD_DIM), BF16)] * 2
        + [pltpu.VMEM((s, 2 * HEAD_DIM), F32)] + _exchange_sems(n_send),
        compiler_params=_params(1),
    )(proj, proj, proj, dcat, tsum, qg, kg, *sends)
    return out[:5], out[5:]


CONV_ROWS = 128


def _shifted(window, shift, halo):
    if shift == 0:
        return window[halo:, :]
    return pltpu.roll(window, shift, 0)[halo:, :]


def _conv_taps(u_s, r0, w_ref, rows):
    window = u_s[pl.ds(r0, CONV_HALO + rows), :]
    y = None
    for k in range(CONV_KERNEL):
        term = _shifted(window, CONV_KERNEL - 1 - k, CONV_HALO) * w_ref[k:k + 1, :]
        y = term if y is None else y + term
    return y


def _conv_fwd(proj, w, b, lg, lb, name):
    s = proj.shape[0]
    cw = w.shape[1]
    rows = CONV_ROWS
    blk_a = (proj.shape[1] - 2 * cw) // cw

    def body(a_ref, g_ref, w_ref, b_ref, lg_ref, lb_ref, o_ref, u_s):
        u_s[0:CONV_HALO, :] = jnp.zeros((CONV_HALO, cw), F32)

        def glu(i, _):
            r0 = pl.multiple_of(i * rows, rows)
            u_s[pl.ds(CONV_HALO + r0, rows), :] = a_ref[pl.ds(r0, rows), :] * _sigmoid(g_ref[pl.ds(r0, rows), :])
            return 0

        lax.fori_loop(0, s // rows, glu, 0)

        def chunk(i, _):
            r0 = pl.multiple_of(i * rows, rows)
            y = _conv_taps(u_s, r0, w_ref, rows) + b_ref[...]
            yc = y - jnp.mean(y, axis=-1, keepdims=True)
            n = yc * lax.rsqrt(jnp.mean(yc * yc, axis=-1, keepdims=True) + EPS)
            ln = n * lg_ref[...] + lb_ref[...]
            o_ref[pl.ds(r0, rows), :] = (ln * _sigmoid(ln)).astype(BF16)
            return 0

        lax.fori_loop(0, s // rows, chunk, 0)

    vec = pl.BlockSpec((1, cw), lambda i: (0, 0))
    return pl.pallas_call(
        body, name=name, grid=(1,),
        in_specs=[pl.BlockSpec((s, cw), lambda i: (0, blk_a)), pl.BlockSpec((s, cw), lambda i: (0, blk_a + 1)),
                  pl.BlockSpec((CONV_KERNEL, cw), lambda i: (0, 0)), vec, vec, vec],
        out_specs=pl.BlockSpec((s, cw), lambda i: (0, 0)),
        out_shape=jax.ShapeDtypeStruct((s, cw), BF16),
        scratch_shapes=[pltpu.VMEM((CONV_HALO + s, cw), F32)],
        compiler_params=_params(1),
    )(proj, proj, w, b, lg, lb)


def _conv_bwd(proj, dcat, w, b, lg, lb, name):
    s = proj.shape[0]
    cw = w.shape[1]
    rows = CONV_ROWS
    blk_a = (proj.shape[1] - 2 * cw) // cw
    n_chunk = s // rows

    def body(a_ref, g_ref, dc_ref, w_ref, b_ref, lg_ref, lb_ref, o_ref, dw_ref, db_ref, dlg_ref, dlb_ref,
             u_s, dy_s, dw_s):
        u_s[0:CONV_HALO, :] = jnp.zeros((CONV_HALO, cw), F32)
        dy_s[pl.ds(s, CONV_HALO), :] = jnp.zeros((CONV_HALO, cw), F32)
        dw_s[...] = jnp.zeros_like(dw_s)

        def glu(i, _):
            r0 = pl.multiple_of(i * rows, rows)
            u_s[pl.ds(CONV_HALO + r0, rows), :] = a_ref[pl.ds(r0, rows), :] * _sigmoid(g_ref[pl.ds(r0, rows), :])
            return 0

        lax.fori_loop(0, n_chunk, glu, 0)

        def chunk(i, carry):
            db, dlg, dlb = carry
            r0 = pl.multiple_of(i * rows, rows)
            window = u_s[pl.ds(r0, CONV_HALO + rows), :]
            y = None
            for k in range(CONV_KERNEL):
                term = _shifted(window, CONV_KERNEL - 1 - k, CONV_HALO) * w_ref[k:k + 1, :]
                y = term if y is None else y + term
            y = y + b_ref[...]
            yc = y - jnp.mean(y, axis=-1, keepdims=True)
            r = lax.rsqrt(jnp.mean(yc * yc, axis=-1, keepdims=True) + EPS)
            n = yc * r
            ln = n * lg_ref[...] + lb_ref[...]
            sg = _sigmoid(ln)
            dln = dc_ref[pl.ds(r0, rows), :] * (sg * (1.0 + ln * (1.0 - sg)))
            dn = dln * lg_ref[...]
            dy = r * (dn - jnp.mean(dn, axis=-1, keepdims=True) - n * jnp.mean(dn * n, axis=-1, keepdims=True))
            dy_s[pl.ds(r0, rows), :] = dy
            for k in range(CONV_KERNEL):
                prod = _shifted(window, CONV_KERNEL - 1 - k, CONV_HALO) * dy
                dw_s[k] += jnp.sum(prod.reshape(rows // 8, 8, cw), axis=0)
            return (db + jnp.sum(dy, axis=0, keepdims=True),
                    dlg + jnp.sum(dln * n, axis=0, keepdims=True),
                    dlb + jnp.sum(dln, axis=0, keepdims=True))

        zero = jnp.zeros((1, cw), F32)
        db, dlg, dlb = lax.fori_loop(0, n_chunk, chunk, (zero, zero, zero))
        db_ref[...] = db
        dlg_ref[...] = dlg
        dlb_ref[...] = dlb
        for k in range(CONV_KERNEL):
            dw_ref[k:k + 1, :] = jnp.sum(dw_s[k], axis=0, keepdims=True)

        def back(i, _):
            r0 = pl.multiple_of(i * rows, rows)
            window = dy_s[pl.ds(r0, rows + CONV_HALO), :]
            du = None
            for k in range(CONV_KERNEL):
                sh = CONV_KERNEL - 1 - k
                shifted = window[:rows, :] if sh == 0 else pltpu.roll(window, rows + CONV_HALO - sh, 0)[:rows, :]
                term = shifted * w_ref[k:k + 1, :]
                du = term if du is None else du + term
            av = a_ref[pl.ds(r0, rows), :]
            sg = _sigmoid(g_ref[pl.ds(r0, rows), :])
            o_ref[pl.ds(r0, rows), 0:cw] = (du * sg).astype(BF16)
            o_ref[pl.ds(r0, rows), cw:2 * cw] = (du * av * sg * (1.0 - sg)).astype(BF16)
            return 0

        lax.fori_loop(0, n_chunk, back, 0)

    vec = pl.BlockSpec((1, cw), lambda i: (0, 0))
    wspec = pl.BlockSpec((CONV_KERNEL, cw), lambda i: (0, 0))
    return pl.pallas_call(
        body, name=name, grid=(1,),
        in_specs=[pl.BlockSpec((s, cw), lambda i: (0, blk_a)), pl.BlockSpec((s, cw), lambda i: (0, blk_a + 1)),
                  pl.BlockSpec((s, cw), lambda i: (0, 1)), wspec, vec, vec, vec],
        out_specs=[pl.BlockSpec((s, 2 * cw), lambda i: (0, 0)), wspec, vec, vec, vec],
        out_shape=[jax.ShapeDtypeStruct((s, 2 * cw), BF16), jax.ShapeDtypeStruct((CONV_KERNEL, cw), F32)]
        + [jax.ShapeDtypeStruct((1, cw), F32)] * 3,
        scratch_shapes=[pltpu.VMEM((CONV_HALO + s, cw), F32), pltpu.VMEM((s + CONV_HALO, cw), F32),
                        pltpu.VMEM((CONV_KERNEL, 8, cw), F32)],
        compiler_params=_params(1),
    )(proj, proj, dcat, w, b, lg, lb)


FFN_ROWS = 256


def _ffn_gate(g_ref, r0, rows, w_ref, b_ref):
    cur = g_ref[pl.ds(r0, rows), :]
    prev = g_ref[pl.ds(pl.multiple_of(jnp.maximum(r0 - FFN_HALO, 0), FFN_HALO), FFN_HALO), :]
    prev = jnp.where(r0 > 0, prev, 0.0)
    window = jnp.concatenate([prev, cur], axis=0)
    gc = cur * w_ref[FFN_KERNEL - 1:FFN_KERNEL, :] + b_ref[...]
    for k in range(FFN_KERNEL - 1):
        gc = gc + _shifted(window, FFN_KERNEL - 1 - k, FFN_HALO) * w_ref[k:k + 1, :]
    return gc, window


def _ffn_fwd(up, w, b, name):
    s = up.shape[0]
    f = w.shape[1]
    tc = _pick(f, (256, 128))
    nc = f // tc
    rows = _pick(s, (FFN_ROWS, 128))

    def body(g_ref, v_ref, w_ref, b_ref, o_ref):
        def chunk(i, _):
            r0 = pl.multiple_of(i * rows, rows)
            gc, _w = _ffn_gate(g_ref, r0, rows, w_ref, b_ref)
            o_ref[pl.ds(r0, rows), :] = (gc * _sigmoid(gc) * v_ref[pl.ds(r0, rows), :]).astype(BF16)
            return 0

        lax.fori_loop(0, s // rows, chunk, 0)

    return pl.pallas_call(
        body, name=name, grid=(nc,),
        in_specs=[pl.BlockSpec((s, tc), lambda j: (0, j)), pl.BlockSpec((s, tc), lambda j: (0, nc + j)),
                  pl.BlockSpec((FFN_KERNEL, tc), lambda j: (0, j)), pl.BlockSpec((1, tc), lambda j: (0, j))],
        out_specs=pl.BlockSpec((s, tc), lambda j: (0, j)),
        out_shape=jax.ShapeDtypeStruct((s, f), BF16),
        compiler_params=_params(1),
    )(up, up, w, b)


def _ffn_bwd(up, dact, w, b, name):
    s = up.shape[0]
    f = w.shape[1]
    tc = _pick(f, (256, 128))
    nc = f // tc
    rows = _pick(s, (FFN_ROWS, 128))
    n_chunk = s // rows

    def body(g_ref, v_ref, da_ref, w_ref, b_ref, dg_ref, dv_ref, dw_ref, db_ref, dgc_s):
        dgc_s[pl.ds(s, FFN_HALO), :] = jnp.zeros((FFN_HALO, tc), F32)

        def chunk(i, carry):
            r0 = pl.multiple_of(i * rows, rows)
            gc, window = _ffn_gate(g_ref, r0, rows, w_ref, b_ref)
            sg = _sigmoid(gc)
            da = da_ref[pl.ds(r0, rows), :]
            dv_ref[pl.ds(r0, rows), :] = (da * gc * sg).astype(BF16)
            dgc = da * v_ref[pl.ds(r0, rows), :] * (sg * (1.0 + gc * (1.0 - sg)))
            dgc_s[pl.ds(r0, rows), :] = dgc
            out = [carry[0] + jnp.sum(dgc, axis=0, keepdims=True)]
            for k in range(FFN_KERNEL):
                out.append(carry[1 + k] + jnp.sum(_shifted(window, FFN_KERNEL - 1 - k, FFN_HALO) * dgc,
                                                  axis=0, keepdims=True))
            return tuple(out)

        zero = jnp.zeros((1, tc), F32)
        sums = lax.fori_loop(0, n_chunk, chunk, (zero,) * (1 + FFN_KERNEL))
        db_ref[...] = sums[0]
        for k in range(FFN_KERNEL):
            dw_ref[k:k + 1, :] = sums[1 + k]

        def back(i, _):
            r0 = pl.multiple_of(i * rows, rows)
            window = dgc_s[pl.ds(r0, rows + FFN_HALO), :]
            dg = window[:rows, :] * w_ref[FFN_KERNEL - 1:FFN_KERNEL, :]
            for k in range(FFN_KERNEL - 1):
                sh = FFN_KERNEL - 1 - k
                dg = dg + pltpu.roll(window, rows + FFN_HALO - sh, 0)[:rows, :] * w_ref[k:k + 1, :]
            dg_ref[pl.ds(r0, rows), :] = dg.astype(BF16)
            return 0

        lax.fori_loop(0, n_chunk, back, 0)

    blk = lambda off: pl.BlockSpec((s, tc), lambda j: (0, off + j))
    return pl.pallas_call(
        body, name=name, grid=(nc,),
        in_specs=[blk(0), blk(nc), blk(0), pl.BlockSpec((FFN_KERNEL, tc), lambda j: (0, j)),
                  pl.BlockSpec((1, tc), lambda j: (0, j))],
        out_specs=[blk(0), blk(0), pl.BlockSpec((FFN_KERNEL, tc), lambda j: (0, j)),
                   pl.BlockSpec((1, tc), lambda j: (0, j))],
        out_shape=[jax.ShapeDtypeStruct((s, f), BF16), jax.ShapeDtypeStruct((s, f), BF16),
                   jax.ShapeDtypeStruct((FFN_KERNEL, f), F32), jax.ShapeDtypeStruct((1, f), F32)],
        scratch_shapes=[pltpu.VMEM((s + FFN_HALO, tc), F32)],
        compiler_params=_params(1),
    )(up, up, dact, w, b)


def _loss_head(y, target, name):
    m, n = y.shape
    tm = _pick(m, (256, 128))

    def body(y_ref, t_ref, l_ref, d_ref, db_ref):
        e = y_ref[...] - t_ref[...]
        part = 0.5 * jnp.sum(jnp.sum(e * e, axis=-1, keepdims=True) / n, axis=0, keepdims=True)

        @pl.when(pl.program_id(0) == 0)
        def _():
            l_ref[...] = jnp.zeros_like(l_ref)

        l_ref[...] += part
        d = e / n
        d_ref[...] = d
        db_ref[...] = d.astype(BF16)

    return pl.pallas_call(
        body, name=name, grid=(m // tm,),
        in_specs=[pl.BlockSpec((tm, n), lambda i: (i, 0))] * 2,
        out_specs=[pl.BlockSpec((8, LANES), lambda i: (0, 0)), pl.BlockSpec((tm, n), lambda i: (i, 0)),
                   pl.BlockSpec((tm, n), lambda i: (i, 0))],
        out_shape=[jax.ShapeDtypeStruct((8, LANES), F32), jax.ShapeDtypeStruct((m, n), F32),
                   jax.ShapeDtypeStruct((m, n), BF16)],
        compiler_params=_params(1),
    )(y, target)


def _adamw_math(w, g, m, v):
    m = ADAM_B1 * m + (1.0 - ADAM_B1) * g
    v = ADAM_B2 * v + (1.0 - ADAM_B2) * (g * g)
    m_hat = m / (1.0 - ADAM_B1 ** ADAM_STEP)
    v_hat = v / (1.0 - ADAM_B2 ** ADAM_STEP)
    delta = -ADAM_LR * (m_hat / (jnp.sqrt(v_hat) + ADAM_EPS) + ADAM_WD * w)
    return delta, m, v


def _sum_adamw(parts, w, m, v, name):
    depth, r, c = w.shape
    tr = _pick(r, (256, 128, 32, 16))
    steps = r // tr

    def body(*refs):
        p_refs = refs[:depth]
        w_ref, m_ref, v_ref, g_out, d_out, m_out, v_out = refs[depth:]
        for layer in range(depth):
            @pl.when(pl.program_id(0) == layer)
            def _():
                g = p_refs[layer][0].astype(F32)
                for src in range(1, N_DEV):
                    g = g + p_refs[layer][src].astype(F32)
                d, mn, vn = _adamw_math(w_ref[0], g, m_ref[0], v_ref[0])
                g_out[0] = g
                d_out[0] = d
                m_out[0] = mn
                v_out[0] = vn

    def part_spec(layer):
        return pl.BlockSpec((N_DEV, tr, c), lambda l, i: (0, jnp.clip((l - layer) * steps + i, 0, steps - 1), 0))

    blk = pl.BlockSpec((1, tr, c), lambda l, i: (l, i, 0))
    return pl.pallas_call(
        body, name=name, grid=(depth, steps),
        in_specs=[part_spec(layer) for layer in range(depth)] + [blk, blk, blk],
        out_specs=[blk] * 4,
        out_shape=[jax.ShapeDtypeStruct(w.shape, F32)] * 4,
        compiler_params=_params(2),
    )(*parts, w, m, v)


def _sum_rows(parts, name):
    _, r, c = parts.shape

    def body(p_ref, o_ref):
        g = p_ref[0]
        for src in range(1, N_DEV):
            g = g + p_ref[src]
        o_ref[...] = g

    return pl.pallas_call(
        body, name=name, grid=(1,),
        in_specs=[pl.BlockSpec((N_DEV, r, c), lambda i: (0, 0, 0))],
        out_specs=pl.BlockSpec((r, c), lambda i: (0, 0)),
        out_shape=jax.ShapeDtypeStruct((r, c), F32),
        compiler_params=_params(1),
    )(parts)


def _adamw_rows(w, g, m, v, name):
    r, c = w.shape

    def body(w_ref, g_ref, m_ref, v_ref, d_out, m_out, v_out):
        d, mn, vn = _adamw_math(w_ref[...], g_ref[...], m_ref[...], v_ref[...])
        d_out[...] = d
        m_out[...] = mn
        v_out[...] = vn

    blk = pl.BlockSpec((r, c), lambda i: (0, 0))
    return pl.pallas_call(
        body, name=name, grid=(1,), in_specs=[blk] * 4, out_specs=[blk] * 3,
        out_shape=[jax.ShapeDtypeStruct((r, c), F32)] * 3,
        compiler_params=_params(1),
    )(w, g, m, v)


PACK_TILE = 8 * LANES


def _pack(arrays):
    rows = []
    for a in arrays:
        flat = a.reshape(-1).astype(F32)
        pad = (-flat.shape[0]) % PACK_TILE
        rows.append(jnp.pad(flat, (0, pad)).reshape(-1, LANES))
    return jnp.concatenate(rows, axis=0)


def _unpack(packed, shapes, lead=()):
    out, r0 = [], 0
    for shp in shapes:
        size = 1
        for d in shp:
            size *= d
        nrows = -(-size // PACK_TILE) * 8
        piece = packed[..., r0:r0 + nrows, :].reshape(lead + (nrows * LANES,))[..., :size]
        out.append(piece.reshape(lead + tuple(shp)))
        r0 += nrows
    return out


def kernel(x, norm1_g, w_in, q_norm_g, k_norm_g, conv_dw_w, conv_dw_b, conv_ln_g, conv_ln_b, w_out, norm2_g, w_up, ffn_dw_w, ffn_dw_b, w_down, loss_target, m_norm1_g, m_w_in, m_q_norm_g, m_k_norm_g, m_conv_dw_w, m_conv_dw_b, m_conv_ln_g, m_conv_ln_b, m_w_out, m_norm2_g, m_w_up, m_ffn_dw_w, m_ffn_dw_b, m_w_down, v_norm1_g, v_w_in, v_q_norm_g, v_k_norm_g, v_conv_dw_w, v_conv_dw_b, v_conv_ln_g, v_conv_ln_b, v_w_out, v_norm2_g, v_w_up, v_ffn_dw_w, v_ffn_dw_b, v_w_down):
    depth, d_model, in_shard = w_in.shape
    out_shard = w_out.shape[1]
    up_shard = w_up.shape[2]
    down_shard = w_down.shape[1]
    d_ff = down_shard * N_DEV
    conv_w = conv_dw_b.shape[1]
    cw_shard = conv_dw_w.shape[2]
    fw_shard = ffn_dw_w.shape[2]
    me = 4 * lax.axis_index("x") + 2 * lax.axis_index("y") + lax.axis_index("c")

    b_in, b_out, b_up, b_down = w_in.astype(BF16), w_out.astype(BF16), w_up.astype(BF16), w_down.astype(BF16)
    cols_major = lambda g: g.transpose(1, 0, 2).reshape(g.shape[1], N_DEV * g.shape[2])
    rows_major = lambda g: g.reshape(N_DEV * g.shape[1], g.shape[2])
    g_in0, g_small = _exchange([b_in[0], _pack([conv_dw_w, ffn_dw_w])], [False, False], name="gather_first")
    wf_in, wf_out, wf_up, wf_down = [cols_major(g_in0)] + [None] * (depth - 1), [None] * depth, [None] * depth, [None] * depth
    g_cw, g_fw = _unpack(g_small, [conv_dw_w.shape, ffn_dw_w.shape], lead=(N_DEV,))
    cwf = g_cw.transpose(1, 2, 0, 3).reshape(depth, CONV_KERNEL, conv_w)
    fwf = g_fw.transpose(1, 2, 0, 3).reshape(depth, FFN_KERNEL, d_ff)

    row = lambda a, l: a[l].reshape(1, -1)

    xs = x[0]
    saved = []
    for l in range(depth):
        proj, h1 = _mm_rms(xs, row(norm1_g, l), wf_in[l], name="fwd_in")
        sends = [b_out[l], b_up[l], b_down[l]] + ([b_in[l + 1]] if l + 1 < depth else [])
        attn, tsum, got = _attn_fwd(proj, row(q_norm_g, l), row(k_norm_g, l), sends, name="fwd_attn")
        wf_out[l], wf_up[l], wf_down[l] = rows_major(got[0]), cols_major(got[1]), rows_major(got[2])
        if l + 1 < depth:
            wf_in[l + 1] = cols_major(got[3])
        conv = _conv_fwd(proj, cwf[l], row(conv_dw_b, l), row(conv_ln_g, l), row(conv_ln_b, l), name="fwd_conv")
        cat = jnp.concatenate([attn, conv], axis=1)
        x_mid = _mm_res(cat, wf_out[l], xs, name="fwd_out")
        up, h2 = _mm_rms(x_mid, row(norm2_g, l), wf_up[l], name="fwd_up")
        act = _ffn_fwd(up, fwf[l], row(ffn_dw_b, l), name="fwd_ffn")
        x_next = _mm_res(act, wf_down[l], x_mid, name="fwd_down")
        saved.append((xs, h1, proj, tsum, cat, x_mid, h2, up, act))
        xs = x_next

    loss_tile, dx, dxb = _loss_head(xs, loss_target[0], name="loss_head")
    loss = lax.psum(loss_tile[0, 0], ("x", "y", "c"))

    r_in, r_out, r_up, r_down = [None] * depth, [None] * depth, [None] * depth, [None] * depth
    col_blocks = lambda g: g.reshape(g.shape[0], N_DEV, g.shape[1] // N_DEV).transpose(1, 0, 2)
    row_blocks = lambda g: g.reshape(N_DEV, g.shape[0] // N_DEV, g.shape[1])
    small = {k: [None] * depth for k in ("norm1_g", "q_norm_g", "k_norm_g", "conv_dw_w", "conv_dw_b", "conv_ln_g",
                                         "conv_ln_b", "norm2_g", "ffn_dw_w", "ffn_dw_b")}
    gw_in = None
    for l in reversed(range(depth)):
        xs, h1, proj, tsum, cat, x_mid, h2, up, act = saved[l]
        dact = _mm_nt(dxb, wf_down[l], name="bwd_dact")
        gw_down = _mm_tn(act, dxb, name="bwd_gw_down")
        dgate, dval, small["ffn_dw_w"][l], small["ffn_dw_b"][l] = _ffn_bwd(
            up, dact, fwf[l], row(ffn_dw_b, l), name="bwd_ffn")
        dup = jnp.concatenate([dgate, dval], axis=1)
        gw_up = _mm_tn(h2, dup, name="bwd_gw_up")
        dx, dxb, small["norm2_g"][l] = _mm_nt_rmsbwd(dup, wf_up[l], x_mid, row(norm2_g, l), dx, name="bwd_up")
        dcat = _mm_nt(dxb, wf_out[l], name="bwd_dcat")
        gw_out = _mm_tn(cat, dxb, name="bwd_gw_out")
        dglu, small["conv_dw_w"][l], small["conv_dw_b"][l], small["conv_ln_g"][l], small["conv_ln_b"][l] = _conv_bwd(
            proj, dcat, cwf[l], row(conv_dw_b, l), row(conv_ln_g, l), row(conv_ln_b, l), name="bwd_conv")
        sends = [row_blocks(gw_down), col_blocks(gw_up), row_blocks(gw_out)] + ([col_blocks(gw_in)] if l + 1 < depth else [])
        (dq, dk, dv, dqg, dkg), got = _attn_bwd(proj, dcat, tsum, row(q_norm_g, l), row(k_norm_g, l), sends,
                                                name="bwd_attn")
        r_down[l], r_up[l], r_out[l] = got[:3]
        if l + 1 < depth:
            r_in[l + 1] = got[3]
        small["q_norm_g"][l] = dqg[0:1, 0:HEAD_DIM]
        small["k_norm_g"][l] = dkg[0:1, 0:HEAD_DIM]
        dproj = jnp.concatenate([dq, dk, dv, dglu], axis=1)
        gw_in = _mm_tn(h1, dproj, name="bwd_gw_in")
        dx, dxb, small["norm1_g"][l] = _mm_nt_rmsbwd(dproj, wf_in[l], xs, row(norm1_g, l), dx, name="bwd_in")
    grad_x = dx[None]

    names = ["norm1_g", "q_norm_g", "k_norm_g", "conv_dw_w", "conv_dw_b", "conv_ln_g", "conv_ln_b", "norm2_g",
             "ffn_dw_w", "ffn_dw_b"]
    full_shapes = {"norm1_g": norm1_g.shape, "q_norm_g": q_norm_g.shape, "k_norm_g": k_norm_g.shape,
                   "conv_dw_w": (depth, CONV_KERNEL, conv_w), "conv_dw_b": conv_dw_b.shape,
                   "conv_ln_g": conv_ln_g.shape, "conv_ln_b": conv_ln_b.shape, "norm2_g": norm2_g.shape,
                   "ffn_dw_w": (depth, FFN_KERNEL, d_ff), "ffn_dw_b": ffn_dw_b.shape}
    partial = _pack([jnp.stack(small[k]).reshape(full_shapes[k]) for k in names])
    r_in[0], all_partials = _exchange([col_blocks(gw_in), partial], [True, False], name="exchange_last")
    big = {
        "w_in": _sum_adamw(r_in, w_in, m_w_in, v_w_in, name="adamw_in"),
        "w_out": _sum_adamw(r_out, w_out, m_w_out, v_w_out, name="adamw_out"),
        "w_up": _sum_adamw(r_up, w_up, m_w_up, v_w_up, name="adamw_up"),
        "w_down": _sum_adamw(r_down, w_down, m_w_down, v_w_down, name="adamw_down"),
    }

    total = _unpack(_sum_rows(all_partials, name="sum_small_grads"), [full_shapes[k] for k in names])
    grads = dict(zip(names, total))
    grads["conv_dw_w"] = lax.dynamic_slice_in_dim(grads["conv_dw_w"], me * cw_shard, cw_shard, axis=2)
    grads["ffn_dw_w"] = lax.dynamic_slice_in_dim(grads["ffn_dw_w"], me * fw_shard, fw_shard, axis=2)
    weights = dict(norm1_g=norm1_g, q_norm_g=q_norm_g, k_norm_g=k_norm_g, conv_dw_w=conv_dw_w, conv_dw_b=conv_dw_b,
                   conv_ln_g=conv_ln_g, conv_ln_b=conv_ln_b, norm2_g=norm2_g, ffn_dw_w=ffn_dw_w, ffn_dw_b=ffn_dw_b)
    m_in = dict(norm1_g=m_norm1_g, q_norm_g=m_q_norm_g, k_norm_g=m_k_norm_g, conv_dw_w=m_conv_dw_w,
                conv_dw_b=m_conv_dw_b, conv_ln_g=m_conv_ln_g, conv_ln_b=m_conv_ln_b, norm2_g=m_norm2_g,
                ffn_dw_w=m_ffn_dw_w, ffn_dw_b=m_ffn_dw_b)
    v_in = dict(norm1_g=v_norm1_g, q_norm_g=v_q_norm_g, k_norm_g=v_k_norm_g, conv_dw_w=v_conv_dw_w,
                conv_dw_b=v_conv_dw_b, conv_ln_g=v_conv_ln_g, conv_ln_b=v_conv_ln_b, norm2_g=v_norm2_g,
                ffn_dw_w=v_ffn_dw_w, ffn_dw_b=v_ffn_dw_b)
    shard_shapes = [weights[k].shape for k in names]
    d_s, m_s, v_s = _adamw_rows(_pack([weights[k] for k in names]), _pack([grads[k] for k in names]),
                                _pack([m_in[k] for k in names]), _pack([v_in[k] for k in names]), name="adamw_small")
    delta = dict(zip(names, _unpack(d_s, shard_shapes)))
    new_m = dict(zip(names, _unpack(m_s, shard_shapes)))
    new_v = dict(zip(names, _unpack(v_s, shard_shapes)))
    for k, (g, d, mn, vn) in big.items():
        grads[k], delta[k], new_m[k], new_v[k] = g, d, mn, vn

    order = ["norm1_g", "w_in", "q_norm_g", "k_norm_g", "conv_dw_w", "conv_dw_b", "conv_ln_g", "conv_ln_b", "w_out",
             "norm2_g", "w_up", "ffn_dw_w", "ffn_dw_b", "w_down"]
    return (loss, grad_x, *[grads[k] for k in order], *[delta[k] for k in order], *[new_m[k] for k in order],
            *[new_v[k] for k in order])
```

```python
import functools

import jax
import jax.numpy as jnp
from jax import lax
from jax.experimental import pallas as pl
from jax.experimental.pallas import tpu as pltpu

F32 = jnp.float32
BF16 = jnp.bfloat16

N_DEV = 8
HEADS = 8
HEAD_DIM = 64
ATTN_WIDTH = HEADS * HEAD_DIM
CONV_KERNEL = 31
FFN_KERNEL = 3
EPS = 1e-6
BLK = 128
KEY_GROUP = 4
LANES = 128
NORM_ROWS = 128
CONV_HALO = 32
FFN_HALO = 8

ADAM_LR = 0.001
ADAM_B1 = 0.9
ADAM_B2 = 0.999
ADAM_EPS = 1e-08
ADAM_WD = 0.01
ADAM_STEP = 10

VMEM_LIMIT = 56 * 1024 * 1024


def _params(n_axes=0):
    kw = dict(vmem_limit_bytes=VMEM_LIMIT)
    if n_axes:
        kw["dimension_semantics"] = ("arbitrary",) * n_axes
    return pltpu.CompilerParams(**kw)


def _dot(a, b):
    return jnp.dot(a, b, preferred_element_type=F32)


def _dot_nt(a, b):
    return lax.dot_general(a, b, (((1,), (1,)), ((), ())), preferred_element_type=F32)


def _dot_tn(a, b):
    return lax.dot_general(a, b, (((0,), (0,)), ((), ())), preferred_element_type=F32)


def _sigmoid(x):
    return 1.0 / (1.0 + jnp.exp(-x))


def _split_bf16(x):
    hi = x.astype(BF16)
    lo = (x - hi.astype(F32)).astype(BF16)
    return hi, lo


def _pick(n, options):
    for t in options:
        if n % t == 0:
            return t
    return n


def _tile(n, cap):
    best = None
    for t in range(LANES, min(n, cap) + 1, LANES):
        if n % t == 0:
            best = t
    return best or n


def _mm_rms(x, g, w, name):
    m, k = x.shape
    n = w.shape[1]
    tm = _tile(m, 2048)
    tn = _tile(n, 512)

    def body(x_ref, g_ref, w_ref, o_ref, h_ref, h_s):
        @pl.when(pl.program_id(1) == 0)
        def _():
            def chunk(c, _):
                rows = pl.ds(pl.multiple_of(c * NORM_ROWS, NORM_ROWS), NORM_ROWS)
                xv = x_ref[rows, :]
                r = lax.rsqrt(jnp.mean(xv * xv, axis=-1, keepdims=True) + EPS)
                hv = (xv * r * g_ref[...]).astype(BF16)
                h_s[rows, :] = hv
                h_ref[rows, :] = hv
                return 0

            lax.fori_loop(0, tm // NORM_ROWS, chunk, 0)

        o_ref[...] = _dot(h_s[...], w_ref[...])

    return pl.pallas_call(
        body, name=name, grid=(m // tm, n // tn),
        in_specs=[pl.BlockSpec((tm, k), lambda i, j: (i, 0)),
                  pl.BlockSpec((1, k), lambda i, j: (0, 0)),
                  pl.BlockSpec((k, tn), lambda i, j: (0, j))],
        out_specs=[pl.BlockSpec((tm, tn), lambda i, j: (i, j)),
                   pl.BlockSpec((tm, k), lambda i, j: (i, 0))],
        out_shape=[jax.ShapeDtypeStruct((m, n), F32), jax.ShapeDtypeStruct((m, k), BF16)],
        scratch_shapes=[pltpu.VMEM((tm, k), BF16)],
        compiler_params=_params(2),
    )(x, g, w)


def _mm_res(a, w, res, name):
    m, k = a.shape
    n = w.shape[1]
    tm = _tile(m, 1024)
    tn = _tile(n, 512)

    def body(a_ref, w_ref, r_ref, o_ref):
        o_ref[...] = r_ref[...] + _dot(a_ref[...], w_ref[...])

    return pl.pallas_call(
        body, name=name, grid=(m // tm, n // tn),
        in_specs=[pl.BlockSpec((tm, k), lambda i, j: (i, 0)),
                  pl.BlockSpec((k, tn), lambda i, j: (0, j)),
                  pl.BlockSpec((tm, tn), lambda i, j: (i, j))],
        out_specs=pl.BlockSpec((tm, tn), lambda i, j: (i, j)),
        out_shape=jax.ShapeDtypeStruct((m, n), F32),
        compiler_params=_params(2),
    )(a, w, res)


def _mm_nt(a, w, name):
    m, k = a.shape
    n = w.shape[0]
    tm = _tile(m, 1024)
    tn = _tile(n, 1408)

    def body(a_ref, w_ref, o_ref):
        o_ref[...] = _dot_nt(a_ref[...], w_ref[...])

    return pl.pallas_call(
        body, name=name, grid=(m // tm, n // tn),
        in_specs=[pl.BlockSpec((tm, k), lambda i, j: (i, 0)),
                  pl.BlockSpec((tn, k), lambda i, j: (j, 0))],
        out_specs=pl.BlockSpec((tm, tn), lambda i, j: (i, j)),
        out_shape=jax.ShapeDtypeStruct((m, n), F32),
        compiler_params=_params(2),
    )(a, w)


def _mm_tn(a, b, name):
    s, m = a.shape
    n = b.shape[1]
    tm = _tile(m, 1408)
    tn = _tile(n, 512)

    def body(a_ref, b_ref, o_ref):
        o_ref[...] = _dot_tn(a_ref[...], b_ref[...]).astype(BF16)

    return pl.pallas_call(
        body, name=name, grid=(m // tm, n // tn),
        in_specs=[pl.BlockSpec((s, tm), lambda i, j: (0, i)),
                  pl.BlockSpec((s, tn), lambda i, j: (0, j))],
        out_specs=pl.BlockSpec((tm, tn), lambda i, j: (i, j)),
        out_shape=jax.ShapeDtypeStruct((m, n), BF16),
        compiler_params=_params(2),
    )(a, b)


def _mm_nt_rmsbwd(a, w, x, g, dres, name):
    m, k = a.shape
    n = w.shape[0]
    tm = _tile(m, 1024)
    tk = _tile(k, 1408)
    nk = k // tk

    def body(a_ref, w_ref, x_ref, g_ref, r_ref, dx_ref, dxb_ref, dg_ref, acc):
        i, kk = pl.program_id(0), pl.program_id(1)
        part = _dot_nt(a_ref[...], w_ref[...])

        @pl.when(kk == 0)
        def _():
            acc[...] = part

        @pl.when(kk > 0)
        def _():
            acc[...] += part

        @pl.when(kk == nk - 1)
        def _():
            def chunk(c, dgp):
                rows = pl.ds(pl.multiple_of(c * NORM_ROWS, NORM_ROWS), NORM_ROWS)
                dh = acc[rows, :]
                xv = x_ref[rows, :]
                r = lax.rsqrt(jnp.mean(xv * xv, axis=-1, keepdims=True) + EPS)
                xh = xv * r
                dxh = dh * g_ref[...]
                dx = r_ref[rows, :] + r * (dxh - xh * jnp.mean(dxh * xh, axis=-1, keepdims=True))
                dx_ref[rows, :] = dx
                dxb_ref[rows, :] = dx.astype(BF16)
                return dgp + jnp.sum(dh * xh, axis=0, keepdims=True)

            dgp = lax.fori_loop(0, tm // NORM_ROWS, chunk, jnp.zeros((1, n), F32))

            @pl.when(i == 0)
            def _():
                dg_ref[...] = dgp

            @pl.when(i > 0)
            def _():
                dg_ref[...] += dgp

    return pl.pallas_call(
        body, name=name, grid=(m // tm, nk),
        in_specs=[pl.BlockSpec((tm, tk), lambda i, kk: (i, kk)),
                  pl.BlockSpec((n, tk), lambda i, kk: (0, kk)),
                  pl.BlockSpec((tm, n), lambda i, kk: (i, 0)),
                  pl.BlockSpec((1, n), lambda i, kk: (0, 0)),
                  pl.BlockSpec((tm, n), lambda i, kk: (i, 0))],
        out_specs=[pl.BlockSpec((tm, n), lambda i, kk: (i, 0)),
                   pl.BlockSpec((tm, n), lambda i, kk: (i, 0)),
                   pl.BlockSpec((1, n), lambda i, kk: (0, 0))],
        out_shape=[jax.ShapeDtypeStruct((m, n), F32), jax.ShapeDtypeStruct((m, n), BF16),
                   jax.ShapeDtypeStruct((1, n), F32)],
        scratch_shapes=[pltpu.VMEM((tm, n), F32)],
        compiler_params=_params(2),
    )(a, w, x, g, dres)


ANY_SPEC = pl.BlockSpec(memory_space=pl.ANY)
SEMS_PER_OPERAND = N_DEV - 1


def _exchange_sems(n):
    return [pltpu.SemaphoreType.DMA((n, SEMS_PER_OPERAND)), pltpu.SemaphoreType.DMA((n, SEMS_PER_OPERAND)),
            pltpu.SemaphoreType.DMA((n,))]


def _exchange_shapes(parts, scatter):
    return [jax.ShapeDtypeStruct(a.shape if sc else (N_DEV,) + a.shape, a.dtype) for a, sc in zip(parts, scatter)]


def _flat(pos):
    return 4 * pos[0] + 2 * pos[1] + pos[2]


def _remote(src, dst, sems, i, k, to):
    send_sems, recv_sems, _ = sems
    return pltpu.make_async_remote_copy(src_ref=src, dst_ref=dst, send_sem=send_sems.at[i, k],
                                        recv_sem=recv_sems.at[i, k], device_id=to,
                                        device_id_type=pl.DeviceIdType.MESH)


def _direct_copies(ins, outs, scatter, sems):
    x, y, c = lax.axis_index("x"), lax.axis_index("y"), lax.axis_index("c")
    me = _flat((x, y, c))
    copies = []
    for i in range(len(ins)):
        src = ins[i].at[me] if scatter[i] else ins[i]
        copies.append(pltpu.make_async_copy(src, outs[i].at[me], sems[2].at[i]))
    for d in range(1, N_DEV):
        peer = (1 - x if d & 4 else x, 1 - y if d & 2 else y, 1 - c if d & 1 else c)
        for i in range(len(ins)):
            src = ins[i].at[_flat(peer)] if scatter[i] else ins[i]
            copies.append(_remote(src, outs[i].at[me], sems, i, d - 1, peer))
    return copies


def _two_level_gather(ins, outs, sems):
    x, y, c = lax.axis_index("x"), lax.axis_index("y"), lax.axis_index("c")
    me, sibling = (x, y, c), (x, y, 1 - c)
    chips = [(1 - x, y), (x, 1 - y), (1 - x, 1 - y)]
    n = len(ins)

    def block(i, pos):
        return outs[i].at[_flat(pos)]

    local = [pltpu.make_async_copy(ins[i], block(i, me), sems[2].at[i]) for i in range(n)]
    own = [_remote(ins[i], block(i, me), sems, i, 0, sibling) for i in range(n)]
    own += [_remote(ins[i], block(i, me), sems, i, 1 + j, (*chip, c)) for i in range(n) for j, chip in enumerate(chips)]
    passed = [[_remote(block(i, (*chip, c)), block(i, (*chip, c)), sems, i, 4 + j, sibling) for i in range(n)]
              for j, chip in enumerate(chips)]

    def first():
        for cp in local + own:
            cp.start()

    def relay():
        for j, chip in enumerate(chips):
            for i in range(n):
                _remote(ins[i], block(i, (*chip, c)), sems, i, 1 + j, me).wait_recv()
                passed[j][i].start()

    def finish():
        for i in range(n):
            _remote(ins[i], block(i, sibling), sems, i, 0, me).wait_recv()
            for j, chip in enumerate(chips):
                _remote(ins[i], block(i, (*chip, 1 - c)), sems, i, 4 + j, me).wait_recv()
        for cp in own + [cp for row in passed for cp in row]:
            cp.wait_send()
        for cp in local:
            cp.wait()

    return first, relay, finish


def _exchange(parts, scatter, name):
    n = len(parts)

    def body(*refs):
        copies = _direct_copies(refs[:n], refs[n:2 * n], scatter, refs[2 * n:])
        for cp in copies:
            cp.start()
        for cp in copies:
            cp.wait()

    return pl.pallas_call(
        body, name=name, in_specs=[ANY_SPEC] * n, out_specs=[ANY_SPEC] * n,
        out_shape=_exchange_shapes(parts, scatter), scratch_shapes=_exchange_sems(n),
    )(*parts)


def _tri(kind):
    j = lax.broadcasted_iota(jnp.int32, (BLK, BLK), 0)
    s = lax.broadcasted_iota(jnp.int32, (BLK, BLK), 1)
    m = {"after": j > s, "upto": j <= s, "before": j < s}[kind]
    half = jnp.concatenate([jnp.where(m, 1.0, 0.0), jnp.ones((BLK, BLK), F32)], axis=1).astype(BF16)
    return jnp.concatenate([half, half], axis=0)


def _scan_rows(v, tri):
    hi, lo = _split_bf16(v)
    r = _dot(jnp.concatenate([hi, lo], axis=1), tri)
    return r[:, :BLK], r[:, BLK:]


HEADS_PER_STEP = LANES // HEAD_DIM


def _first_head_lanes():
    return lax.broadcasted_iota(jnp.int32, (1, LANES), 1) < HEAD_DIM


def _pair_mean(v, first):
    m0 = jnp.sum(jnp.where(first, v, 0.0), axis=-1, keepdims=True)
    m1 = jnp.sum(jnp.where(first, 0.0, v), axis=-1, keepdims=True)
    return jnp.where(first, m0, m1) * (1.0 / HEAD_DIM)


def _pair_norm(v, g2, first):
    return v * lax.rsqrt(_pair_mean(v * v, first) + EPS) * g2


def _pair_norm_bwd(raw, g2, dn, first):
    r = lax.rsqrt(_pair_mean(raw * raw, first) + EPS)
    xh = raw * r
    dxh = dn * g2
    return r * (dxh - xh * _pair_mean(dxh * xh, first)), jnp.sum(dn * xh, axis=0, keepdims=True)


def _block_diag(v, first):
    zero = jnp.zeros_like(v)
    return jnp.concatenate([jnp.where(first, v, zero), jnp.where(first, zero, v)], axis=0)


def _attn_prep(q_ref, k_ref, v_ref, qg_ref, kg_ref, qc_s, kc_s, vd_s, kd_s, n_blk):
    scale = HEAD_DIM ** -0.5
    first = _first_head_lanes()

    def prep(i, _):
        rows = pl.ds(pl.multiple_of(i * BLK, BLK), BLK)
        both = pl.ds(pl.multiple_of(i * 2 * BLK, 2 * BLK), 2 * BLK)
        qh, ql = _split_bf16(_pair_norm(q_ref[rows, :], qg_ref[...], first) * scale)
        kh, kl = _split_bf16(_pair_norm(k_ref[rows, :], kg_ref[...], first))
        for h in range(HEADS_PER_STEP):
            sl = slice(h * HEAD_DIM, (h + 1) * HEAD_DIM)
            qc_s[h, rows, :] = jnp.concatenate([qh[:, sl], ql[:, sl], qh[:, sl], ql[:, sl]], axis=1)
            kc_s[h, rows, :] = jnp.concatenate([kh[:, sl], kh[:, sl], kl[:, sl], kl[:, sl]], axis=1)
        vd_s[both, :] = _block_diag(v_ref[rows, :].astype(BF16), first)
        if kd_s is not None:
            kd_s[both, :] = _block_diag(kh, first)
        return 0

    lax.fori_loop(0, n_blk, prep, 0)


def _col_minus_row():
    row = lax.broadcasted_iota(jnp.int32, (BLK, BLK), 0)
    col = lax.broadcasted_iota(jnp.int32, (BLK, BLK), 1)
    return col - row


def _softplus(z):
    return jnp.maximum(z, 0.0) + jnp.log(1.0 + jnp.exp(-jnp.abs(z)))


def _attn_fwd(proj, qg, kg, sends, name):
    s = proj.shape[0]
    n_blk = s // BLK
    pairs = ATTN_WIDTH // LANES
    n_send = len(sends)

    def body(q_ref, k_ref, v_ref, qg_ref, kg_ref, *rest):
        send_refs, (o_ref, t_ref), rest = rest[:n_send], rest[n_send:n_send + 2], rest[n_send + 2:]
        got_refs, (qc_s, kc_s, vd_s), sems = rest[:n_send], rest[n_send:n_send + 3], rest[n_send + 3:]
        start, relay, finish = _two_level_gather(send_refs, got_refs, sems)
        step = pl.program_id(0)
        pl.when(step == 0)(start)
        pl.when(step == pairs - 1)(relay)
        tri = _tri("after")
        diff = _col_minus_row()
        first = _first_head_lanes()
        heads = range(HEADS_PER_STEP)
        _attn_prep(q_ref, k_ref, v_ref, qg_ref, kg_ref, qc_s, kc_s, vd_s, None, n_blk)

        def group(qc, qi, grp, carry, acc, masked):
            blocks = [grp * KEY_GROUP + j for j in reversed(range(KEY_GROUP))]
            zs = [[_dot_nt(qc[h], kc_s[h, pl.ds(pl.multiple_of(kb * BLK, BLK), BLK), :]) for kb in blocks]
                  for h in heads]
            keeps = [diff < (qi - kb) * BLK if masked else None for kb in blocks]
            parts = [[None] * KEY_GROUP for _ in heads]
            for h in heads:
                for j, z in enumerate(zs[h]):
                    sp = _softplus(z)
                    lom = -sp
                    if masked:
                        lom = jnp.where(keeps[j], lom, 0.0)
                    tail, tot = _scan_rows(lom, tri)
                    parts[h][j] = (z - sp + tail, tot)
            carry = list(carry)
            for j, kb in enumerate(blocks):
                ws = []
                for h in heads:
                    lw, tot = parts[h][j]
                    w = jnp.exp(lw + carry[h])
                    if masked:
                        w = jnp.where(keeps[j], w, 0.0)
                    ws.append(w.astype(BF16))
                    carry[h] = carry[h] + tot
                acc = acc + _dot(jnp.concatenate(ws, axis=1),
                                 vd_s[pl.ds(pl.multiple_of(kb * 2 * BLK, 2 * BLK), 2 * BLK), :])
            return tuple(carry), acc

        def q_block(qi, _):
            rows = pl.ds(pl.multiple_of(qi * BLK, BLK), BLK)
            qc = [qc_s[h, rows, :] for h in heads]
            top = qi // KEY_GROUP
            zero = jnp.zeros((BLK, BLK), F32)
            carry, acc = group(qc, qi, top, (zero,) * HEADS_PER_STEP, jnp.zeros((BLK, LANES), F32), True)
            carry, acc = lax.fori_loop(
                0, top, lambda t, c: group(qc, qi, top - 1 - t, c[0], c[1], False), (carry, acc))
            o_ref[rows, :] = acc.astype(BF16)
            t_ref[rows, :] = jnp.where(first, carry[0], carry[1])
            return 0

        lax.fori_loop(0, n_blk, q_block, 0)
        pl.when(step == pairs - 1)(finish)

    col = lambda off: pl.BlockSpec((s, LANES), lambda p: (0, off + p))
    vec = pl.BlockSpec((1, LANES), lambda p: (0, 0))
    out = pl.pallas_call(
        body, name=name, grid=(pairs,),
        in_specs=[col(0), col(pairs), col(2 * pairs), vec, vec] + [ANY_SPEC] * n_send,
        out_specs=[pl.BlockSpec((s, LANES), lambda p: (0, p))] * 2 + [ANY_SPEC] * n_send,
        out_shape=[jax.ShapeDtypeStruct((s, ATTN_WIDTH), BF16), jax.ShapeDtypeStruct((s, ATTN_WIDTH), F32)]
        + _exchange_shapes(sends, [False] * n_send),
        scratch_shapes=[pltpu.VMEM((HEADS_PER_STEP, s, 4 * HEAD_DIM), BF16)] * 2
        + [pltpu.VMEM((HEADS_PER_STEP * s, LANES), BF16)] + _exchange_sems(n_send),
        compiler_params=_params(1),
    )(proj, proj, proj, qg, kg, *sends)
    return out[0], out[1], out[2:]


def _attn_bwd(proj, dcat, tsum, qg, kg, sends, name):
    s = proj.shape[0]
    n_blk = s // BLK
    pairs = ATTN_WIDTH // LANES
    scale = HEAD_DIM ** -0.5
    n_send = len(sends)
    n_scratch = 7

    def body(q_ref, k_ref, v_ref, do_ref, t_ref, qg_ref, kg_ref, *rest):
        send_refs, rest = rest[:n_send], rest[n_send:]
        (dq_ref, dk_ref, dv_ref, dqg_ref, dkg_ref), rest = rest[:5], rest[5:]
        got_refs, scratch, sems = rest[:n_send], rest[n_send:n_send + n_scratch], rest[n_send + n_scratch:]
        qc_s, kc_s, vd_s, kd_s, qd_s, dob_s, dkv_s = scratch
        copies = _direct_copies(send_refs, got_refs, [True] * n_send, sems)

        @pl.when(pl.program_id(0) == 0)
        def _():
            for cp in copies:
                cp.start()

        tri_p = _tri("upto")
        tri_h = _tri("before")
        diff = _col_minus_row()

        @pl.when(pl.program_id(0) == 0)
        def _():
            dqg_ref[...] = jnp.zeros_like(dqg_ref)
            dkg_ref[...] = jnp.zeros_like(dkg_ref)

        first = _first_head_lanes()
        heads = range(HEADS_PER_STEP)
        _attn_prep(q_ref, k_ref, v_ref, qg_ref, kg_ref, qc_s, kc_s, vd_s, kd_s, n_blk)

        def prep(i, _):
            rows = pl.ds(pl.multiple_of(i * BLK, BLK), BLK)
            both = pl.ds(pl.multiple_of(i * 2 * BLK, 2 * BLK), 2 * BLK)
            dob = do_ref[rows, :].astype(BF16)
            dob_s[rows, :] = dob
            none = jnp.zeros((BLK, HEAD_DIM), BF16)
            for h in heads:
                qd_s[h, both, :] = jnp.concatenate(
                    [jnp.concatenate([qc_s[h, rows, 0:HEAD_DIM], none], axis=1),
                     jnp.concatenate([none, dob[:, h * HEAD_DIM:(h + 1) * HEAD_DIM]], axis=1)], axis=0)
                dkv_s[h, rows, :] = jnp.zeros((BLK, LANES), F32)
            return 0

        lax.fori_loop(0, n_blk, prep, 0)

        def group(qc, qd, dob, tq, qi, grp, pc, hc, dq, masked):
            blocks = [grp * KEY_GROUP + j for j in range(KEY_GROUP)]
            cols_of = [pl.ds(pl.multiple_of(kb * BLK, BLK), BLK) for kb in blocks]
            both_of = [pl.ds(pl.multiple_of(kb * 2 * BLK, 2 * BLK), 2 * BLK) for kb in blocks]
            zs = [[_dot_nt(qc[h], kc_s[h, cols, :]) for cols in cols_of] for h in heads]
            das = [_dot_nt(dob, vd_s[both, :]) for both in both_of]
            keeps = [diff < (qi - kb) * BLK if masked else None for kb in blocks]
            lbs = [[None] * KEY_GROUP for _ in heads]
            scans = [[None] * KEY_GROUP for _ in heads]
            for h in heads:
                for j, z in enumerate(zs[h]):
                    sp = _softplus(z)
                    lom = -sp
                    if masked:
                        lom = jnp.where(keeps[j], lom, 0.0)
                    lbs[h][j] = z - sp
                    scans[h][j] = _scan_rows(lom, tri_p)
            pc, hc = list(pc), list(hc)
            avs = [[None] * KEY_GROUP for _ in heads]
            gws = [[None] * KEY_GROUP for _ in heads]
            hscans = [[None] * KEY_GROUP for _ in heads]
            for h in heads:
                for j in range(KEY_GROUP):
                    p_in, p_tot = scans[h][j]
                    a = jnp.exp(lbs[h][j] + (tq[h] - pc[h] - p_in))
                    if masked:
                        a = jnp.where(keeps[j], a, 0.0)
                    pc[h] = pc[h] + p_tot
                    gw = das[j][:, h * BLK:(h + 1) * BLK] * a
                    avs[h][j] = a.astype(BF16)
                    gws[h][j] = gw
                    hscans[h][j] = _scan_rows(gw, tri_h)
            dzs = [[None] * KEY_GROUP for _ in heads]
            for h in heads:
                for j in range(KEY_GROUP):
                    h_in, g_tot = hscans[h][j]
                    gw = gws[h][j]
                    dz = gw - jnp.exp(lbs[h][j]) * (gw + hc[h] + h_in)
                    if masked:
                        dz = jnp.where(keeps[j], dz, 0.0)
                    hc[h] = hc[h] + g_tot
                    dzs[h][j] = dz.astype(BF16)
            for j, both in enumerate(both_of):
                dq = dq + _dot(jnp.concatenate([dzs[h][j] for h in heads], axis=1), kd_s[both, :])
            for h in heads:
                for j, cols in enumerate(cols_of):
                    dkv_s[h, cols, :] += _dot_tn(jnp.concatenate([dzs[h][j], avs[h][j]], axis=0), qd[h])
            return tuple(pc), tuple(hc), dq

        def q_block(qi, dqg):
            rows = pl.ds(pl.multiple_of(qi * BLK, BLK), BLK)
            both = pl.ds(pl.multiple_of(qi * 2 * BLK, 2 * BLK), 2 * BLK)
            qc = [qc_s[h, rows, :] for h in heads]
            qd = [qd_s[h, both, :] for h in heads]
            dob = dob_s[rows, :]
            tboth = t_ref[rows, :]
            tq = [jnp.concatenate([tboth[:, h * HEAD_DIM:(h + 1) * HEAD_DIM]] * 2, axis=1) for h in heads]
            zero = (jnp.zeros((BLK, BLK), F32),) * HEADS_PER_STEP
            top = qi // KEY_GROUP
            pc, hc, dq = lax.fori_loop(
                0, top, lambda grp, c: group(qc, qd, dob, tq, qi, grp, c[0], c[1], c[2], False),
                (zero, zero, jnp.zeros((BLK, LANES), F32)))
            _, _, dq = group(qc, qd, dob, tq, qi, top, pc, hc, dq, True)
            dq_raw, dg = _pair_norm_bwd(q_ref[rows, :], qg_ref[...], dq * scale, first)
            dq_ref[rows, :] = dq_raw.astype(BF16)
            return dqg + dg

        dqg = lax.fori_loop(0, n_blk, q_block, jnp.zeros((1, LANES), F32))

        def finish(i, dkg):
            rows = pl.ds(pl.multiple_of(i * BLK, BLK), BLK)
            dk = jnp.concatenate([dkv_s[h, rows, 0:HEAD_DIM] for h in heads], axis=1)
            dv = jnp.concatenate([dkv_s[h, rows, HEAD_DIM:2 * HEAD_DIM] for h in heads], axis=1)
            dk_raw, dg = _pair_norm_bwd(k_ref[rows, :], kg_ref[...], dk, first)
            dk_ref[rows, :] = dk_raw.astype(BF16)
            dv_ref[rows, :] = dv.astype(BF16)
            return dkg + dg

        dkg = lax.fori_loop(0, n_blk, finish, jnp.zeros((1, LANES), F32))
        dqg_ref[0:1, :] += dqg
        dkg_ref[0:1, :] += dkg

        @pl.when(pl.program_id(0) == pairs - 1)
        def _():
            for cp in copies:
                cp.wait()

    col = lambda off: pl.BlockSpec((s, LANES), lambda p: (0, off + p))
    vec = pl.BlockSpec((1, LANES), lambda p: (0, 0))
    small = pl.BlockSpec((8, LANES), lambda p: (0, 0))
    out = pl.pallas_call(
        body, name=name, grid=(pairs,),
        in_specs=[col(0), col(pairs), col(2 * pairs), col(0), col(0), vec, vec] + [ANY_SPEC] * n_send,
        out_specs=[col(0)] * 3 + [small] * 2 + [ANY_SPEC] * n_send,
        out_shape=[jax.ShapeDtypeStruct((s, ATTN_WIDTH), BF16)] * 3 + [jax.ShapeDtypeStruct((8, LANES), F32)] * 2
        + _exchange_shapes(sends, [True] * n_send),
        scratch_shapes=[pltpu.VMEM((HEADS_PER_STEP, s, 4 * HEAD_DIM), BF16)] * 2
        + [pltpu.VMEM((HEADS_PER_STEP * s, LANES), BF16)] * 2
        + [pltpu.VMEM((HEADS_PER_STEP, HEADS_PER_STEP * s, LANES), BF16), pltpu.VMEM((s, LANES), BF16),
           pltpu.VMEM((HEADS_PER_STEP, s, LANES), F32)] + _exchange_sems(n_send),
        compiler_params=_params(1),
    )(proj, proj, proj, dcat, tsum, qg, kg, *sends)
    return out[:5], out[5:]


CONV_ROWS = 128


def _shifted(window, shift, halo):
    if shift == 0:
        return window[halo:, :]
    return pltpu.roll(window, shift, 0)[halo:, :]


def _conv_taps(u_s, r0, w_ref, rows):
    window = u_s[pl.ds(r0, CONV_HALO + rows), :]
    y = None
    for k in range(CONV_KERNEL):
        term = _shifted(window, CONV_KERNEL - 1 - k, CONV_HALO) * w_ref[k:k + 1, :]
        y = term if y is None else y + term
    return y


def _conv_fwd(proj, w, b, lg, lb, name):
    s = proj.shape[0]
    cw = w.shape[1]
    rows = CONV_ROWS
    blk_a = (proj.shape[1] - 2 * cw) // cw

    def body(a_ref, g_ref, w_ref, b_ref, lg_ref, lb_ref, o_ref, u_s):
        u_s[0:CONV_HALO, :] = jnp.zeros((CONV_HALO, cw), F32)

        def glu(i, _):
            r0 = pl.multiple_of(i * rows, rows)
            u_s[pl.ds(CONV_HALO + r0, rows), :] = a_ref[pl.ds(r0, rows), :] * _sigmoid(g_ref[pl.ds(r0, rows), :])
            return 0

        lax.fori_loop(0, s // rows, glu, 0)

        def chunk(i, _):
            r0 = pl.multiple_of(i * rows, rows)
            y = _conv_taps(u_s, r0, w_ref, rows) + b_ref[...]
            yc = y - jnp.mean(y, axis=-1, keepdims=True)
            n = yc * lax.rsqrt(jnp.mean(yc * yc, axis=-1, keepdims=True) + EPS)
            ln = n * lg_ref[...] + lb_ref[...]
            o_ref[pl.ds(r0, rows), :] = (ln * _sigmoid(ln)).astype(BF16)
            return 0

        lax.fori_loop(0, s // rows, chunk, 0)

    vec = pl.BlockSpec((1, cw), lambda i: (0, 0))
    return pl.pallas_call(
        body, name=name, grid=(1,),
        in_specs=[pl.BlockSpec((s, cw), lambda i: (0, blk_a)), pl.BlockSpec((s, cw), lambda i: (0, blk_a + 1)),
                  pl.BlockSpec((CONV_KERNEL, cw), lambda i: (0, 0)), vec, vec, vec],
        out_specs=pl.BlockSpec((s, cw), lambda i: (0, 0)),
        out_shape=jax.ShapeDtypeStruct((s, cw), BF16),
        scratch_shapes=[pltpu.VMEM((CONV_HALO + s, cw), F32)],
        compiler_params=_params(1),
    )(proj, proj, w, b, lg, lb)


def _conv_bwd(proj, dcat, w, b, lg, lb, name):
    s = proj.shape[0]
    cw = w.shape[1]
    rows = CONV_ROWS
    blk_a = (proj.shape[1] - 2 * cw) // cw
    n_chunk = s // rows

    def body(a_ref, g_ref, dc_ref, w_ref, b_ref, lg_ref, lb_ref, o_ref, dw_ref, db_ref, dlg_ref, dlb_ref,
             u_s, dy_s, dw_s):
        u_s[0:CONV_HALO, :] = jnp.zeros((CONV_HALO, cw), F32)
        dy_s[pl.ds(s, CONV_HALO), :] = jnp.zeros((CONV_HALO, cw), F32)
        dw_s[...] = jnp.zeros_like(dw_s)

        def glu(i, _):
            r0 = pl.multiple_of(i * rows, rows)
            u_s[pl.ds(CONV_HALO + r0, rows), :] = a_ref[pl.ds(r0, rows), :] * _sigmoid(g_ref[pl.ds(r0, rows), :])
            return 0

        lax.fori_loop(0, n_chunk, glu, 0)

        def chunk(i, carry):
            db, dlg, dlb = carry
            r0 = pl.multiple_of(i * rows, rows)
            window = u_s[pl.ds(r0, CONV_HALO + rows), :]
            y = None
            for k in range(CONV_KERNEL):
                term = _shifted(window, CONV_KERNEL - 1 - k, CONV_HALO) * w_ref[k:k + 1, :]
                y = term if y is None else y + term
            y = y + b_ref[...]
            yc = y - jnp.mean(y, axis=-1, keepdims=True)
            r = lax.rsqrt(jnp.mean(yc * yc, axis=-1, keepdims=True) + EPS)
            n = yc * r
            ln = n * lg_ref[...] + lb_ref[...]
            sg = _sigmoid(ln)
            dln = dc_ref[pl.ds(r0, rows), :] * (sg * (1.0 + ln * (1.0 - sg)))
            dn = dln * lg_ref[...]
            dy = r * (dn - jnp.mean(dn, axis=-1, keepdims=True) - n * jnp.mean(dn * n, axis=-1, keepdims=True))
            dy_s[pl.ds(r0, rows), :] = dy
            for k in range(CONV_KERNEL):
                prod = _shifted(window, CONV_KERNEL - 1 - k, CONV_HALO) * dy
                dw_s[k] += jnp.sum(prod.reshape(rows // 8, 8, cw), axis=0)
            return (db + jnp.sum(dy, axis=0, keepdims=True),
                    dlg + jnp.sum(dln * n, axis=0, keepdims=True),
                    dlb + jnp.sum(dln, axis=0, keepdims=True))

        zero = jnp.zeros((1, cw), F32)
        db, dlg, dlb = lax.fori_loop(0, n_chunk, chunk, (zero, zero, zero))
        db_ref[...] = db
        dlg_ref[...] = dlg
        dlb_ref[...] = dlb
        for k in range(CONV_KERNEL):
            dw_ref[k:k + 1, :] = jnp.sum(dw_s[k], axis=0, keepdims=True)

        def back(i, _):
            r0 = pl.multiple_of(i * rows, rows)
            window = dy_s[pl.ds(r0, rows + CONV_HALO), :]
            du = None
            for k in range(CONV_KERNEL):
                sh = CONV_KERNEL - 1 - k
                shifted = window[:rows, :] if sh == 0 else pltpu.roll(window, rows + CONV_HALO - sh, 0)[:rows, :]
                term = shifted * w_ref[k:k + 1, :]
                du = term if du is None else du + term
            av = a_ref[pl.ds(r0, rows), :]
            sg = _sigmoid(g_ref[pl.ds(r0, rows), :])
            o_ref[pl.ds(r0, rows), 0:cw] = (du * sg).astype(BF16)
            o_ref[pl.ds(r0, rows), cw:2 * cw] = (du * av * sg * (1.0 - sg)).astype(BF16)
            return 0

        lax.fori_loop(0, n_chunk, back, 0)

    vec = pl.BlockSpec((1, cw), lambda i: (0, 0))
    wspec = pl.BlockSpec((CONV_KERNEL, cw), lambda i: (0, 0))
    return pl.pallas_call(
        body, name=name, grid=(1,),
        in_specs=[pl.BlockSpec((s, cw), lambda i: (0, blk_a)), pl.BlockSpec((s, cw), lambda i: (0, blk_a + 1)),
                  pl.BlockSpec((s, cw), lambda i: (0, 1)), wspec, vec, vec, vec],
        out_specs=[pl.BlockSpec((s, 2 * cw), lambda i: (0, 0)), wspec, vec, vec, vec],
        out_shape=[jax.ShapeDtypeStruct((s, 2 * cw), BF16), jax.ShapeDtypeStruct((CONV_KERNEL, cw), F32)]
        + [jax.ShapeDtypeStruct((1, cw), F32)] * 3,
        scratch_shapes=[pltpu.VMEM((CONV_HALO + s, cw), F32), pltpu.VMEM((s + CONV_HALO, cw), F32),
                        pltpu.VMEM((CONV_KERNEL, 8, cw), F32)],
        compiler_params=_params(1),
    )(proj, proj, dcat, w, b, lg, lb)


FFN_ROWS = 256


def _ffn_gate(g_ref, r0, rows, w_ref, b_ref):
    cur = g_ref[pl.ds(r0, rows), :]
    prev = g_ref[pl.ds(pl.multiple_of(jnp.maximum(r0 - FFN_HALO, 0), FFN_HALO), FFN_HALO), :]
    prev = jnp.where(r0 > 0, prev, 0.0)
    window = jnp.concatenate([prev, cur], axis=0)
    gc = cur * w_ref[FFN_KERNEL - 1:FFN_KERNEL, :] + b_ref[...]
    for k in range(FFN_KERNEL - 1):
        gc = gc + _shifted(window, FFN_KERNEL - 1 - k, FFN_HALO) * w_ref[k:k + 1, :]
    return gc, window


def _ffn_fwd(up, w, b, name):
    s = up.shape[0]
    f = w.shape[1]
    tc = _pick(f, (256, 128))
    nc = f // tc
    rows = _pick(s, (FFN_ROWS, 128))

    def body(g_ref, v_ref, w_ref, b_ref, o_ref):
        def chunk(i, _):
            r0 = pl.multiple_of(i * rows, rows)
            gc, _w = _ffn_gate(g_ref, r0, rows, w_ref, b_ref)
            o_ref[pl.ds(r0, rows), :] = (gc * _sigmoid(gc) * v_ref[pl.ds(r0, rows), :]).astype(BF16)
            return 0

        lax.fori_loop(0, s // rows, chunk, 0)

    return pl.pallas_call(
        body, name=name, grid=(nc,),
        in_specs=[pl.BlockSpec((s, tc), lambda j: (0, j)), pl.BlockSpec((s, tc), lambda j: (0, nc + j)),
                  pl.BlockSpec((FFN_KERNEL, tc), lambda j: (0, j)), pl.BlockSpec((1, tc), lambda j: (0, j))],
        out_specs=pl.BlockSpec((s, tc), lambda j: (0, j)),
        out_shape=jax.ShapeDtypeStruct((s, f), BF16),
        compiler_params=_params(1),
    )(up, up, w, b)


def _ffn_bwd(up, dact, w, b, name):
    s = up.shape[0]
    f = w.shape[1]
    tc = _pick(f, (256, 128))
    nc = f // tc
    rows = _pick(s, (FFN_ROWS, 128))
    n_chunk = s // rows

    def body(g_ref, v_ref, da_ref, w_ref, b_ref, dg_ref, dv_ref, dw_ref, db_ref, dgc_s):
        dgc_s[pl.ds(s, FFN_HALO), :] = jnp.zeros((FFN_HALO, tc), F32)

        def chunk(i, carry):
            r0 = pl.multiple_of(i * rows, rows)
            gc, window = _ffn_gate(g_ref, r0, rows, w_ref, b_ref)
            sg = _sigmoid(gc)
            da = da_ref[pl.ds(r0, rows), :]
            dv_ref[pl.ds(r0, rows), :] = (da * gc * sg).astype(BF16)
            dgc = da * v_ref[pl.ds(r0, rows), :] * (sg * (1.0 + gc * (1.0 - sg)))
            dgc_s[pl.ds(r0, rows), :] = dgc
            out = [carry[0] + jnp.sum(dgc, axis=0, keepdims=True)]
            for k in range(FFN_KERNEL):
                out.append(carry[1 + k] + jnp.sum(_shifted(window, FFN_KERNEL - 1 - k, FFN_HALO) * dgc,
                                                  axis=0, keepdims=True))
            return tuple(out)

        zero = jnp.zeros((1, tc), F32)
        sums = lax.fori_loop(0, n_chunk, chunk, (zero,) * (1 + FFN_KERNEL))
        db_ref[...] = sums[0]
        for k in range(FFN_KERNEL):
            dw_ref[k:k + 1, :] = sums[1 + k]

        def back(i, _):
            r0 = pl.multiple_of(i * rows, rows)
            window = dgc_s[pl.ds(r0, rows + FFN_HALO), :]
            dg = window[:rows, :] * w_ref[FFN_KERNEL - 1:FFN_KERNEL, :]
            for k in range(FFN_KERNEL - 1):
                sh = FFN_KERNEL - 1 - k
                dg = dg + pltpu.roll(window, rows + FFN_HALO - sh, 0)[:rows, :] * w_ref[k:k + 1, :]
            dg_ref[pl.ds(r0, rows), :] = dg.astype(BF16)
            return 0

        lax.fori_loop(0, n_chunk, back, 0)

    blk = lambda off: pl.BlockSpec((s, tc), lambda j: (0, off + j))
    return pl.pallas_call(
        body, name=name, grid=(nc,),
        in_specs=[blk(0), blk(nc), blk(0), pl.BlockSpec((FFN_KERNEL, tc), lambda j: (0, j)),
                  pl.BlockSpec((1, tc), lambda j: (0, j))],
        out_specs=[blk(0), blk(0), pl.BlockSpec((FFN_KERNEL, tc), lambda j: (0, j)),
                   pl.BlockSpec((1, tc), lambda j: (0, j))],
        out_shape=[jax.ShapeDtypeStruct((s, f), BF16), jax.ShapeDtypeStruct((s, f), BF16),
                   jax.ShapeDtypeStruct((FFN_KERNEL, f), F32), jax.ShapeDtypeStruct((1, f), F32)],
        scratch_shapes=[pltpu.VMEM((s + FFN_HALO, tc), F32)],
        compiler_params=_params(1),
    )(up, up, dact, w, b)


def _loss_head(y, target, name):
    m, n = y.shape
    tm = _pick(m, (256, 128))

    def body(y_ref, t_ref, l_ref, d_ref, db_ref):
        e = y_ref[...] - t_ref[...]
        part = 0.5 * jnp.sum(jnp.sum(e * e, axis=-1, keepdims=True) / n, axis=0, keepdims=True)

        @pl.when(pl.program_id(0) == 0)
        def _():
            l_ref[...] = jnp.zeros_like(l_ref)

        l_ref[...] += part
        d = e / n
        d_ref[...] = d
        db_ref[...] = d.astype(BF16)

    return pl.pallas_call(
        body, name=name, grid=(m // tm,),
        in_specs=[pl.BlockSpec((tm, n), lambda i: (i, 0))] * 2,
        out_specs=[pl.BlockSpec((8, LANES), lambda i: (0, 0)), pl.BlockSpec((tm, n), lambda i: (i, 0)),
                   pl.BlockSpec((tm, n), lambda i: (i, 0))],
        out_shape=[jax.ShapeDtypeStruct((8, LANES), F32), jax.ShapeDtypeStruct((m, n), F32),
                   jax.ShapeDtypeStruct((m, n), BF16)],
        compiler_params=_params(1),
    )(y, target)


def _adamw_math(w, g, m, v):
    m = ADAM_B1 * m + (1.0 - ADAM_B1) * g
    v = ADAM_B2 * v + (1.0 - ADAM_B2) * (g * g)
    m_hat = m / (1.0 - ADAM_B1 ** ADAM_STEP)
    v_hat = v / (1.0 - ADAM_B2 ** ADAM_STEP)
    delta = -ADAM_LR * (m_hat / (jnp.sqrt(v_hat) + ADAM_EPS) + ADAM_WD * w)
    return delta, m, v


def _sum_adamw(parts, w, m, v, name):
    depth, r, c = w.shape
    tr = _pick(r, (256, 128, 32, 16))
    steps = r // tr

    def body(*refs):
        p_refs = refs[:depth]
        w_ref, m_ref, v_ref, g_out, d_out, m_out, v_out = refs[depth:]
        for layer in range(depth):
            @pl.when(pl.program_id(0) == layer)
            def _():
                g = p_refs[layer][0].astype(F32)
                for src in range(1, N_DEV):
                    g = g + p_refs[layer][src].astype(F32)
                d, mn, vn = _adamw_math(w_ref[0], g, m_ref[0], v_ref[0])
                g_out[0] = g
                d_out[0] = d
                m_out[0] = mn
                v_out[0] = vn

    def part_spec(layer):
        return pl.BlockSpec((N_DEV, tr, c), lambda l, i: (0, jnp.clip((l - layer) * steps + i, 0, steps - 1), 0))

    blk = pl.BlockSpec((1, tr, c), lambda l, i: (l, i, 0))
    return pl.pallas_call(
        body, name=name, grid=(depth, steps),
        in_specs=[part_spec(layer) for layer in range(depth)] + [blk, blk, blk],
        out_specs=[blk] * 4,
        out_shape=[jax.ShapeDtypeStruct(w.shape, F32)] * 4,
        compiler_params=_params(2),
    )(*parts, w, m, v)


def _sum_rows(parts, name):
    _, r, c = parts.shape

    def body(p_ref, o_ref):
        g = p_ref[0]
        for src in range(1, N_DEV):
            g = g + p_ref[src]
        o_ref[...] = g

    return pl.pallas_call(
        body, name=name, grid=(1,),
        in_specs=[pl.BlockSpec((N_DEV, r, c), lambda i: (0, 0, 0))],
        out_specs=pl.BlockSpec((r, c), lambda i: (0, 0)),
        out_shape=jax.ShapeDtypeStruct((r, c), F32),
        compiler_params=_params(1),
    )(parts)


def _adamw_rows(w, g, m, v, name):
    r, c = w.shape

    def body(w_ref, g_ref, m_ref, v_ref, d_out, m_out, v_out):
        d, mn, vn = _adamw_math(w_ref[...], g_ref[...], m_ref[...], v_ref[...])
        d_out[...] = d
        m_out[...] = mn
        v_out[...] = vn

    blk = pl.BlockSpec((r, c), lambda i: (0, 0))
    return pl.pallas_call(
        body, name=name, grid=(1,), in_specs=[blk] * 4, out_specs=[blk] * 3,
        out_shape=[jax.ShapeDtypeStruct((r, c), F32)] * 3,
        compiler_params=_params(1),
    )(w, g, m, v)


PACK_TILE = 8 * LANES


def _pack(arrays):
    rows = []
    for a in arrays:
        flat = a.reshape(-1).astype(F32)
        pad = (-flat.shape[0]) % PACK_TILE
        rows.append(jnp.pad(flat, (0, pad)).reshape(-1, LANES))
    return jnp.concatenate(rows, axis=0)


def _unpack(packed, shapes, lead=()):
    out, r0 = [], 0
    for shp in shapes:
        size = 1
        for d in shp:
            size *= d
        nrows = -(-size // PACK_TILE) * 8
        piece = packed[..., r0:r0 + nrows, :].reshape(lead + (nrows * LANES,))[..., :size]
        out.append(piece.reshape(lead + tuple(shp)))
        r0 += nrows
    return out


def kernel(x, norm1_g, w_in, q_norm_g, k_norm_g, conv_dw_w, conv_dw_b, conv_ln_g, conv_ln_b, w_out, norm2_g, w_up, ffn_dw_w, ffn_dw_b, w_down, loss_target, m_norm1_g, m_w_in, m_q_norm_g, m_k_norm_g, m_conv_dw_w, m_conv_dw_b, m_conv_ln_g, m_conv_ln_b, m_w_out, m_norm2_g, m_w_up, m_ffn_dw_w, m_ffn_dw_b, m_w_down, v_norm1_g, v_w_in, v_q_norm_g, v_k_norm_g, v_conv_dw_w, v_conv_dw_b, v_conv_ln_g, v_conv_ln_b, v_w_out, v_norm2_g, v_w_up, v_ffn_dw_w, v_ffn_dw_b, v_w_down):
    depth, d_model, in_shard = w_in.shape
    out_shard = w_out.shape[1]
    up_shard = w_up.shape[2]
    down_shard = w_down.shape[1]
    d_ff = down_shard * N_DEV
    conv_w = conv_dw_b.shape[1]
    cw_shard = conv_dw_w.shape[2]
    fw_shard = ffn_dw_w.shape[2]
    me = 4 * lax.axis_index("x") + 2 * lax.axis_index("y") + lax.axis_index("c")

    b_in, b_out, b_up, b_down = w_in.astype(BF16), w_out.astype(BF16), w_up.astype(BF16), w_down.astype(BF16)
    cols_major = lambda g: g.transpose(1, 0, 2).reshape(g.shape[1], N_DEV * g.shape[2])
    rows_major = lambda g: g.reshape(N_DEV * g.shape[1], g.shape[2])
    g_in0, g_small = _exchange([b_in[0], _pack([conv_dw_w, ffn_dw_w])], [False, False], name="gather_first")
    wf_in, wf_out, wf_up, wf_down = [cols_major(g_in0)] + [None] * (depth - 1), [None] * depth, [None] * depth, [None] * depth
    g_cw, g_fw = _unpack(g_small, [conv_dw_w.shape, ffn_dw_w.shape], lead=(N_DEV,))
    cwf = g_cw.transpose(1, 2, 0, 3).reshape(depth, CONV_KERNEL, conv_w)
    fwf = g_fw.transpose(1, 2, 0, 3).reshape(depth, FFN_KERNEL, d_ff)

    row = lambda a, l: a[l].reshape(1, -1)
    both_heads = lambda a, l: jnp.tile(row(a, l), (1, HEADS_PER_STEP))

    xs = x[0]
    saved = []
    for l in range(depth):
        proj, h1 = _mm_rms(xs, row(norm1_g, l), wf_in[l], name="fwd_in")
        sends = [b_out[l], b_up[l], b_down[l]] + ([b_in[l + 1]] if l + 1 < depth else [])
        attn, tsum, got = _attn_fwd(proj, both_heads(q_norm_g, l), both_heads(k_norm_g, l), sends, name="fwd_attn")
        wf_out[l], wf_up[l], wf_down[l] = rows_major(got[0]), cols_major(got[1]), rows_major(got[2])
        if l + 1 < depth:
            wf_in[l + 1] = cols_major(got[3])
        conv = _conv_fwd(proj, cwf[l], row(conv_dw_b, l), row(conv_ln_g, l), row(conv_ln_b, l), name="fwd_conv")
        cat = jnp.concatenate([attn, conv], axis=1)
        x_mid = _mm_res(cat, wf_out[l], xs, name="fwd_out")
        up, h2 = _mm_rms(x_mid, row(norm2_g, l), wf_up[l], name="fwd_up")
        act = _ffn_fwd(up, fwf[l], row(ffn_dw_b, l), name="fwd_ffn")
        x_next = _mm_res(act, wf_down[l], x_mid, name="fwd_down")
        saved.append((xs, h1, proj, tsum, cat, x_mid, h2, up, act))
        xs = x_next

    loss_tile, dx, dxb = _loss_head(xs, loss_target[0], name="loss_head")
    loss = lax.psum(loss_tile[0, 0], ("x", "y", "c"))

    r_in, r_out, r_up, r_down = [None] * depth, [None] * depth, [None] * depth, [None] * depth
    col_blocks = lambda g: g.reshape(g.shape[0], N_DEV, g.shape[1] // N_DEV).transpose(1, 0, 2)
    row_blocks = lambda g: g.reshape(N_DEV, g.shape[0] // N_DEV, g.shape[1])
    small = {k: [None] * depth for k in ("norm1_g", "q_norm_g", "k_norm_g", "conv_dw_w", "conv_dw_b", "conv_ln_g",
                                         "conv_ln_b", "norm2_g", "ffn_dw_w", "ffn_dw_b")}
    gw_in = None
    for l in reversed(range(depth)):
        xs, h1, proj, tsum, cat, x_mid, h2, up, act = saved[l]
        dact = _mm_nt(dxb, wf_down[l], name="bwd_dact")
        gw_down = _mm_tn(act, dxb, name="bwd_gw_down")
        dgate, dval, small["ffn_dw_w"][l], small["ffn_dw_b"][l] = _ffn_bwd(
            up, dact, fwf[l], row(ffn_dw_b, l), name="bwd_ffn")
        dup = jnp.concatenate([dgate, dval], axis=1)
        gw_up = _mm_tn(h2, dup, name="bwd_gw_up")
        dx, dxb, small["norm2_g"][l] = _mm_nt_rmsbwd(dup, wf_up[l], x_mid, row(norm2_g, l), dx, name="bwd_up")
        dcat = _mm_nt(dxb, wf_out[l], name="bwd_dcat")
        gw_out = _mm_tn(cat, dxb, name="bwd_gw_out")
        dglu, small["conv_dw_w"][l], small["conv_dw_b"][l], small["conv_ln_g"][l], small["conv_ln_b"][l] = _conv_bwd(
            proj, dcat, cwf[l], row(conv_dw_b, l), row(conv_ln_g, l), row(conv_ln_b, l), name="bwd_conv")
        sends = [row_blocks(gw_down), col_blocks(gw_up), row_blocks(gw_out)] + ([col_blocks(gw_in)] if l + 1 < depth else [])
        (dq, dk, dv, dqg, dkg), got = _attn_bwd(proj, dcat, tsum, both_heads(q_norm_g, l), both_heads(k_norm_g, l),
                                                sends, name="bwd_attn")
        r_down[l], r_up[l], r_out[l] = got[:3]
        if l + 1 < depth:
            r_in[l + 1] = got[3]
        small["q_norm_g"][l] = dqg[0:1, :HEAD_DIM] + dqg[0:1, HEAD_DIM:]
        small["k_norm_g"][l] = dkg[0:1, :HEAD_DIM] + dkg[0:1, HEAD_DIM:]
        dproj = jnp.concatenate([dq, dk, dv, dglu], axis=1)
        gw_in = _mm_tn(h1, dproj, name="bwd_gw_in")
        dx, dxb, small["norm1_g"][l] = _mm_nt_rmsbwd(dproj, wf_in[l], xs, row(norm1_g, l), dx, name="bwd_in")
    grad_x = dx[None]

    names = ["norm1_g", "q_norm_g", "k_norm_g", "conv_dw_w", "conv_dw_b", "conv_ln_g", "conv_ln_b", "norm2_g",
             "ffn_dw_w", "ffn_dw_b"]
    full_shapes = {"norm1_g": norm1_g.shape, "q_norm_g": q_norm_g.shape, "k_norm_g": k_norm_g.shape,
                   "conv_dw_w": (depth, CONV_KERNEL, conv_w), "conv_dw_b": conv_dw_b.shape,
                   "conv_ln_g": conv_ln_g.shape, "conv_ln_b": conv_ln_b.shape, "norm2_g": norm2_g.shape,
                   "ffn_dw_w": (depth, FFN_KERNEL, d_ff), "ffn_dw_b": ffn_dw_b.shape}
    partial = _pack([jnp.stack(small[k]).reshape(full_shapes[k]) for k in names])
    r_in[0], all_partials = _exchange([col_blocks(gw_in), partial], [True, False], name="exchange_last")
    big = {
        "w_in": _sum_adamw(r_in, w_in, m_w_in, v_w_in, name="adamw_in"),
        "w_out": _sum_adamw(r_out, w_out, m_w_out, v_w_out, name="adamw_out"),
        "w_up": _sum_adamw(r_up, w_up, m_w_up, v_w_up, name="adamw_up"),
        "w_down": _sum_adamw(r_down, w_down, m_w_down, v_w_down, name="adamw_down"),
    }

    total = _unpack(_sum_rows(all_partials, name="sum_small_grads"), [full_shapes[k] for k in names])
    grads = dict(zip(names, total))
    grads["conv_dw_w"] = lax.dynamic_slice_in_dim(grads["conv_dw_w"], me * cw_shard, cw_shard, axis=2)
    grads["ffn_dw_w"] = lax.dynamic_slice_in_dim(grads["ffn_dw_w"], me * fw_shard, fw_shard, axis=2)
    weights = dict(norm1_g=norm1_g, q_norm_g=q_norm_g, k_norm_g=k_norm_g, conv_dw_w=conv_dw_w, conv_dw_b=conv_dw_b,
                   conv_ln_g=conv_ln_g, conv_ln_b=conv_ln_b, norm2_g=norm2_g, ffn_dw_w=ffn_dw_w, ffn_dw_b=ffn_dw_b)
    m_in = dict(norm1_g=m_norm1_g, q_norm_g=m_q_norm_g, k_norm_g=m_k_norm_g, conv_dw_w=m_conv_dw_w,
                conv_dw_b=m_conv_dw_b, conv_ln_g=m_conv_ln_g, conv_ln_b=m_conv_ln_b, norm2_g=m_norm2_g,
                ffn_dw_w=m_ffn_dw_w, ffn_dw_b=m_ffn_dw_b)
    v_in = dict(norm1_g=v_norm1_g, q_norm_g=v_q_norm_g, k_norm_g=v_k_norm_g, conv_dw_w=v_conv_dw_w,
                conv_dw_b=v_conv_dw_b, conv_ln_g=v_conv_ln_g, conv_ln_b=v_conv_ln_b, norm2_g=v_norm2_g,
                ffn_dw_w=v_ffn_dw_w, ffn_dw_b=v_ffn_dw_b)
    shard_shapes = [weights[k].shape for k in names]
    d_s, m_s, v_s = _adamw_rows(_pack([weights[k] for k in names]), _pack([grads[k] for k in names]),
                                _pack([m_in[k] for k in names]), _pack([v_in[k] for k in names]), name="adamw_small")
    delta = dict(zip(names, _unpack(d_s, shard_shapes)))
    new_m = dict(zip(names, _unpack(m_s, shard_shapes)))
    new_v = dict(zip(names, _unpack(v_s, shard_shapes)))
    for k, (g, d, mn, vn) in big.items():
        grads[k], delta[k], new_m[k], new_v[k] = g, d, mn, vn

    order = ["norm1_g", "w_in", "q_norm_g", "k_norm_g", "conv_dw_w", "conv_dw_b", "conv_ln_g", "conv_ln_b", "w_out",
             "norm2_g", "w_up", "ffn_dw_w", "ffn_dw_b", "w_down"]
    return (loss, grad_x, *[grads[k] for k in order], *[delta[k] for k in order], *[new_m[k] for k in order],
            *[new_v[k] for k in order])
```

```python
import functools

import jax
import jax.numpy as jnp
from jax import lax
from jax.experimental import pallas as pl
from jax.experimental.pallas import tpu as pltpu

F32 = jnp.float32
BF16 = jnp.bfloat16

N_DEV = 8
HEADS = 8
HEAD_DIM = 64
ATTN_WIDTH = HEADS * HEAD_DIM
CONV_KERNEL = 31
FFN_KERNEL = 3
EPS = 1e-6
BLK = 128
KEY_GROUP = 4
LANES = 128
NORM_ROWS = 128
CONV_HALO = 32
FFN_HALO = 8

ADAM_LR = 0.001
ADAM_B1 = 0.9
ADAM_B2 = 0.999
ADAM_EPS = 1e-08
ADAM_WD = 0.01
ADAM_STEP = 10

VMEM_LIMIT = 56 * 1024 * 1024


def _params(n_axes=0):
    kw = dict(vmem_limit_bytes=VMEM_LIMIT)
    if n_axes:
        kw["dimension_semantics"] = ("arbitrary",) * n_axes
    return pltpu.CompilerParams(**kw)


def _dot(a, b):
    return jnp.dot(a, b, preferred_element_type=F32)


def _dot_nt(a, b):
    return lax.dot_general(a, b, (((1,), (1,)), ((), ())), preferred_element_type=F32)


def _dot_tn(a, b):
    return lax.dot_general(a, b, (((0,), (0,)), ((), ())), preferred_element_type=F32)


def _sigmoid(x):
    return 1.0 / (1.0 + jnp.exp(-x))


def _split_bf16(x):
    hi = x.astype(BF16)
    lo = (x - hi.astype(F32)).astype(BF16)
    return hi, lo


def _pick(n, options):
    for t in options:
        if n % t == 0:
            return t
    return n


def _tile(n, cap):
    best = None
    for t in range(LANES, min(n, cap) + 1, LANES):
        if n % t == 0:
            best = t
    return best or n


def _mm_rms(x, g, wt, name):
    m, k = x.shape
    n = wt.shape[0]
    tm = _tile(m, 2048)
    tn = _tile(n, 512)

    def body(x_ref, g_ref, w_ref, o_ref, h_ref, h_s):
        @pl.when(pl.program_id(1) == 0)
        def _():
            def chunk(c, _):
                rows = pl.ds(pl.multiple_of(c * NORM_ROWS, NORM_ROWS), NORM_ROWS)
                xv = x_ref[rows, :]
                r = lax.rsqrt(jnp.mean(xv * xv, axis=-1, keepdims=True) + EPS)
                hv = (xv * r * g_ref[...]).astype(BF16)
                h_s[rows, :] = hv
                h_ref[rows, :] = hv
                return 0

            lax.fori_loop(0, tm // NORM_ROWS, chunk, 0)

        o_ref[...] = _dot_nt(h_s[...], w_ref[...])

    return pl.pallas_call(
        body, name=name, grid=(m // tm, n // tn),
        in_specs=[pl.BlockSpec((tm, k), lambda i, j: (i, 0)),
                  pl.BlockSpec((1, k), lambda i, j: (0, 0)),
                  pl.BlockSpec((tn, k), lambda i, j: (j, 0))],
        out_specs=[pl.BlockSpec((tm, tn), lambda i, j: (i, j)),
                   pl.BlockSpec((tm, k), lambda i, j: (i, 0))],
        out_shape=[jax.ShapeDtypeStruct((m, n), F32), jax.ShapeDtypeStruct((m, k), BF16)],
        scratch_shapes=[pltpu.VMEM((tm, k), BF16)],
        compiler_params=_params(2),
    )(x, g, wt)


def _mm_res(a, w, res, name):
    m, k = a.shape
    n = w.shape[1]
    tm = _tile(m, 1024)
    tn = _tile(n, 512)

    def body(a_ref, w_ref, r_ref, o_ref):
        o_ref[...] = r_ref[...] + _dot(a_ref[...], w_ref[...])

    return pl.pallas_call(
        body, name=name, grid=(m // tm, n // tn),
        in_specs=[pl.BlockSpec((tm, k), lambda i, j: (i, 0)),
                  pl.BlockSpec((k, tn), lambda i, j: (0, j)),
                  pl.BlockSpec((tm, tn), lambda i, j: (i, j))],
        out_specs=pl.BlockSpec((tm, tn), lambda i, j: (i, j)),
        out_shape=jax.ShapeDtypeStruct((m, n), F32),
        compiler_params=_params(2),
    )(a, w, res)


def _mm_nt(a, w, name):
    m, k = a.shape
    n = w.shape[0]
    tm = _tile(m, 1024)
    tn = _tile(n, 1408)

    def body(a_ref, w_ref, o_ref):
        o_ref[...] = _dot_nt(a_ref[...], w_ref[...])

    return pl.pallas_call(
        body, name=name, grid=(m // tm, n // tn),
        in_specs=[pl.BlockSpec((tm, k), lambda i, j: (i, 0)),
                  pl.BlockSpec((tn, k), lambda i, j: (j, 0))],
        out_specs=pl.BlockSpec((tm, tn), lambda i, j: (i, j)),
        out_shape=jax.ShapeDtypeStruct((m, n), F32),
        compiler_params=_params(2),
    )(a, w)


def _column_tiles(a, cap):
    if a.ndim == 2:
        s, c = a.shape
        tc = _tile(c, cap)
        return s, c, tc, lambda rows, index: pl.BlockSpec((rows, tc), lambda *g: (index(*g)[0], index(*g)[1]))
    slabs, s, width = a.shape
    tc = _tile(width, cap)
    per = width // tc
    return s, slabs * width, tc, lambda rows, index: pl.BlockSpec(
        (None, rows, tc), lambda *g: (index(*g)[1] // per, index(*g)[0], index(*g)[1] % per))


def _mm_tn(a, b, name):
    s, m, tm, a_spec = _column_tiles(a, 1408)
    n = b.shape[1]
    tn = _tile(n, 1024)

    def body(a_ref, b_ref, o_ref):
        o_ref[...] = _dot_tn(a_ref[...], b_ref[...]).astype(BF16)

    return pl.pallas_call(
        body, name=name, grid=(m // tm, n // tn),
        in_specs=[a_spec(s, lambda i, j: (0, i)),
                  pl.BlockSpec((s, tn), lambda i, j: (0, j))],
        out_specs=pl.BlockSpec((tm, tn), lambda i, j: (i, j)),
        out_shape=jax.ShapeDtypeStruct((m, n), BF16),
        compiler_params=_params(2),
    )(a, b)


def _mm_rmsbwd(a, w, x, g, dres, name):
    m, k, tk, a_spec = _column_tiles(a, 1408)
    n = w.shape[1]
    tm = _tile(m, 1024)
    nk = k // tk

    def body(a_ref, w_ref, x_ref, g_ref, r_ref, dx_ref, dxb_ref, dg_ref, acc):
        i, kk = pl.program_id(0), pl.program_id(1)
        part = _dot(a_ref[...], w_ref[...])

        @pl.when(kk == 0)
        def _():
            acc[...] = part

        @pl.when(kk > 0)
        def _():
            acc[...] += part

        @pl.when(kk == nk - 1)
        def _():
            def chunk(c, dgp):
                rows = pl.ds(pl.multiple_of(c * NORM_ROWS, NORM_ROWS), NORM_ROWS)
                dh = acc[rows, :]
                xv = x_ref[rows, :]
                r = lax.rsqrt(jnp.mean(xv * xv, axis=-1, keepdims=True) + EPS)
                xh = xv * r
                dxh = dh * g_ref[...]
                dx = r_ref[rows, :] + r * (dxh - xh * jnp.mean(dxh * xh, axis=-1, keepdims=True))
                dx_ref[rows, :] = dx
                dxb_ref[rows, :] = dx.astype(BF16)
                return dgp + jnp.sum(dh * xh, axis=0, keepdims=True)

            dgp = lax.fori_loop(0, tm // NORM_ROWS, chunk, jnp.zeros((1, n), F32))

            @pl.when(i == 0)
            def _():
                dg_ref[...] = dgp

            @pl.when(i > 0)
            def _():
                dg_ref[...] += dgp

    return pl.pallas_call(
        body, name=name, grid=(m // tm, nk),
        in_specs=[a_spec(tm, lambda i, kk: (i, kk)),
                  pl.BlockSpec((tk, n), lambda i, kk: (kk, 0)),
                  pl.BlockSpec((tm, n), lambda i, kk: (i, 0)),
                  pl.BlockSpec((1, n), lambda i, kk: (0, 0)),
                  pl.BlockSpec((tm, n), lambda i, kk: (i, 0))],
        out_specs=[pl.BlockSpec((tm, n), lambda i, kk: (i, 0)),
                   pl.BlockSpec((tm, n), lambda i, kk: (i, 0)),
                   pl.BlockSpec((1, n), lambda i, kk: (0, 0))],
        out_shape=[jax.ShapeDtypeStruct((m, n), F32), jax.ShapeDtypeStruct((m, n), BF16),
                   jax.ShapeDtypeStruct((1, n), F32)],
        scratch_shapes=[pltpu.VMEM((tm, n), F32)],
        compiler_params=_params(2),
    )(a, w, x, g, dres)


ANY_SPEC = pl.BlockSpec(memory_space=pl.ANY)
SEMS_PER_OPERAND = N_DEV - 1


def _exchange_sems(n):
    return [pltpu.SemaphoreType.DMA((n, SEMS_PER_OPERAND)), pltpu.SemaphoreType.DMA((n, SEMS_PER_OPERAND)),
            pltpu.SemaphoreType.DMA((n,))]


def _exchange_shapes(parts, scatter):
    return [jax.ShapeDtypeStruct(a.shape if sc else (N_DEV,) + a.shape, a.dtype) for a, sc in zip(parts, scatter)]


def _flat(pos):
    return 4 * pos[0] + 2 * pos[1] + pos[2]


def _remote(src, dst, sems, i, k, to):
    send_sems, recv_sems, _ = sems
    return pltpu.make_async_remote_copy(src_ref=src, dst_ref=dst, send_sem=send_sems.at[i, k],
                                        recv_sem=recv_sems.at[i, k], device_id=to,
                                        device_id_type=pl.DeviceIdType.MESH)


def _direct_copies(ins, outs, scatter, sems):
    x, y, c = lax.axis_index("x"), lax.axis_index("y"), lax.axis_index("c")
    me = _flat((x, y, c))
    copies = []
    for i in range(len(ins)):
        src = ins[i].at[me] if scatter[i] else ins[i]
        copies.append(pltpu.make_async_copy(src, outs[i].at[me], sems[2].at[i]))
    for d in range(1, N_DEV):
        peer = (1 - x if d & 4 else x, 1 - y if d & 2 else y, 1 - c if d & 1 else c)
        for i in range(len(ins)):
            src = ins[i].at[_flat(peer)] if scatter[i] else ins[i]
            copies.append(_remote(src, outs[i].at[me], sems, i, d - 1, peer))
    return copies


def _two_level_gather(ins, outs, sems):
    x, y, c = lax.axis_index("x"), lax.axis_index("y"), lax.axis_index("c")
    me, sibling = (x, y, c), (x, y, 1 - c)
    chips = [(1 - x, y), (x, 1 - y), (1 - x, 1 - y)]
    n = len(ins)

    def block(i, pos):
        return outs[i].at[_flat(pos)]

    local = [pltpu.make_async_copy(ins[i], block(i, me), sems[2].at[i]) for i in range(n)]
    own = [_remote(ins[i], block(i, me), sems, i, 0, sibling) for i in range(n)]
    own += [_remote(ins[i], block(i, me), sems, i, 1 + j, (*chip, c)) for i in range(n) for j, chip in enumerate(chips)]
    passed = [[_remote(block(i, (*chip, c)), block(i, (*chip, c)), sems, i, 4 + j, sibling) for i in range(n)]
              for j, chip in enumerate(chips)]

    def first():
        for cp in local + own:
            cp.start()

    def relay():
        for j, chip in enumerate(chips):
            for i in range(n):
                _remote(ins[i], block(i, (*chip, c)), sems, i, 1 + j, me).wait_recv()
                passed[j][i].start()

    def finish():
        for i in range(n):
            _remote(ins[i], block(i, sibling), sems, i, 0, me).wait_recv()
            for j, chip in enumerate(chips):
                _remote(ins[i], block(i, (*chip, 1 - c)), sems, i, 4 + j, me).wait_recv()
        for cp in own + [cp for row in passed for cp in row]:
            cp.wait_send()
        for cp in local:
            cp.wait()

    return first, relay, finish


def _exchange(parts, scatter, name):
    n = len(parts)

    def body(*refs):
        copies = _direct_copies(refs[:n], refs[n:2 * n], scatter, refs[2 * n:])
        for cp in copies:
            cp.start()
        for cp in copies:
            cp.wait()

    return pl.pallas_call(
        body, name=name, in_specs=[ANY_SPEC] * n, out_specs=[ANY_SPEC] * n,
        out_shape=_exchange_shapes(parts, scatter), scratch_shapes=_exchange_sems(n),
    )(*parts)


def _tri(kind):
    j = lax.broadcasted_iota(jnp.int32, (BLK, BLK), 0)
    s = lax.broadcasted_iota(jnp.int32, (BLK, BLK), 1)
    m = {"after": j > s, "upto": j <= s, "before": j < s}[kind]
    half = jnp.concatenate([jnp.where(m, 1.0, 0.0), jnp.ones((BLK, BLK), F32)], axis=1).astype(BF16)
    return jnp.concatenate([half, half], axis=0)


def _scan_rows(v, tri):
    hi, lo = _split_bf16(v)
    r = _dot(jnp.concatenate([hi, lo], axis=1), tri)
    return r[:, :BLK], r[:, BLK:]


HEADS_PER_STEP = LANES // HEAD_DIM


def _first_head_lanes():
    return lax.broadcasted_iota(jnp.int32, (1, LANES), 1) < HEAD_DIM


def _pair_mean(v, first):
    m0 = jnp.sum(jnp.where(first, v, 0.0), axis=-1, keepdims=True)
    m1 = jnp.sum(jnp.where(first, 0.0, v), axis=-1, keepdims=True)
    return jnp.where(first, m0, m1) * (1.0 / HEAD_DIM)


def _pair_norm(v, g2, first):
    return v * lax.rsqrt(_pair_mean(v * v, first) + EPS) * g2


def _pair_norm_bwd(raw, g2, dn, first):
    r = lax.rsqrt(_pair_mean(raw * raw, first) + EPS)
    xh = raw * r
    dxh = dn * g2
    return r * (dxh - xh * _pair_mean(dxh * xh, first)), jnp.sum(dn * xh, axis=0, keepdims=True)


def _block_diag(v, first):
    zero = jnp.zeros_like(v)
    return jnp.concatenate([jnp.where(first, v, zero), jnp.where(first, zero, v)], axis=0)


def _attn_prep(q_ref, k_ref, v_ref, qg_ref, kg_ref, qc_s, kc_s, vd_s, kd_s, n_blk):
    scale = HEAD_DIM ** -0.5
    first = _first_head_lanes()

    def prep(i, _):
        rows = pl.ds(pl.multiple_of(i * BLK, BLK), BLK)
        both = pl.ds(pl.multiple_of(i * 2 * BLK, 2 * BLK), 2 * BLK)
        qh, ql = _split_bf16(_pair_norm(q_ref[rows, :], qg_ref[...], first) * scale)
        kh, kl = _split_bf16(_pair_norm(k_ref[rows, :], kg_ref[...], first))
        for h in range(HEADS_PER_STEP):
            sl = slice(h * HEAD_DIM, (h + 1) * HEAD_DIM)
            qc_s[h, rows, :] = jnp.concatenate([qh[:, sl], ql[:, sl], qh[:, sl], ql[:, sl]], axis=1)
            kc_s[h, rows, :] = jnp.concatenate([kh[:, sl], kh[:, sl], kl[:, sl], kl[:, sl]], axis=1)
        vd_s[both, :] = _block_diag(v_ref[rows, :].astype(BF16), first)
        if kd_s is not None:
            kd_s[both, :] = _block_diag(kh, first)
        return 0

    lax.fori_loop(0, n_blk, prep, 0)


def _col_minus_row():
    row = lax.broadcasted_iota(jnp.int32, (BLK, BLK), 0)
    col = lax.broadcasted_iota(jnp.int32, (BLK, BLK), 1)
    return col - row


def _softplus(z):
    return jnp.maximum(z, 0.0) + jnp.log(1.0 + jnp.exp(-jnp.abs(z)))


def _attn_fwd(proj, qg, kg, sends, name):
    s = proj.shape[0]
    n_blk = s // BLK
    pairs = ATTN_WIDTH // LANES
    n_send = len(sends)

    def body(q_ref, k_ref, v_ref, qg_ref, kg_ref, *rest):
        send_refs, (o_ref, t_ref), rest = rest[:n_send], rest[n_send:n_send + 2], rest[n_send + 2:]
        got_refs, (qc_s, kc_s, vd_s), sems = rest[:n_send], rest[n_send:n_send + 3], rest[n_send + 3:]
        start, relay, finish = _two_level_gather(send_refs, got_refs, sems)
        step = pl.program_id(0)
        pl.when(step == 0)(start)
        pl.when(step == pairs - 1)(relay)
        tri = _tri("after")
        diff = _col_minus_row()
        first = _first_head_lanes()
        heads = range(HEADS_PER_STEP)
        _attn_prep(q_ref, k_ref, v_ref, qg_ref, kg_ref, qc_s, kc_s, vd_s, None, n_blk)

        def group(qc, qi, grp, carry, acc, masked):
            blocks = [grp * KEY_GROUP + j for j in reversed(range(KEY_GROUP))]
            zs = [[_dot_nt(qc[h], kc_s[h, pl.ds(pl.multiple_of(kb * BLK, BLK), BLK), :]) for kb in blocks]
                  for h in heads]
            keeps = [diff < (qi - kb) * BLK if masked else None for kb in blocks]
            parts = [[None] * KEY_GROUP for _ in heads]
            for h in heads:
                for j, z in enumerate(zs[h]):
                    sp = _softplus(z)
                    lom = -sp
                    if masked:
                        lom = jnp.where(keeps[j], lom, 0.0)
                    tail, tot = _scan_rows(lom, tri)
                    parts[h][j] = (z - sp + tail, tot)
            carry = list(carry)
            for j, kb in enumerate(blocks):
                ws = []
                for h in heads:
                    lw, tot = parts[h][j]
                    w = jnp.exp(lw + carry[h])
                    if masked:
                        w = jnp.where(keeps[j], w, 0.0)
                    ws.append(w.astype(BF16))
                    carry[h] = carry[h] + tot
                acc = acc + _dot(jnp.concatenate(ws, axis=1),
                                 vd_s[pl.ds(pl.multiple_of(kb * 2 * BLK, 2 * BLK), 2 * BLK), :])
            return tuple(carry), acc

        def q_block(qi, _):
            rows = pl.ds(pl.multiple_of(qi * BLK, BLK), BLK)
            qc = [qc_s[h, rows, :] for h in heads]
            top = qi // KEY_GROUP
            zero = jnp.zeros((BLK, BLK), F32)
            carry, acc = group(qc, qi, top, (zero,) * HEADS_PER_STEP, jnp.zeros((BLK, LANES), F32), True)
            carry, acc = lax.fori_loop(
                0, top, lambda t, c: group(qc, qi, top - 1 - t, c[0], c[1], False), (carry, acc))
            o_ref[rows, :] = acc.astype(BF16)
            t_ref[rows, :] = jnp.where(first, carry[0], carry[1])
            return 0

        lax.fori_loop(0, n_blk, q_block, 0)
        pl.when(step == pairs - 1)(finish)

    col = lambda off: pl.BlockSpec((s, LANES), lambda p: (0, off + p))
    vec = pl.BlockSpec((1, LANES), lambda p: (0, 0))
    out = pl.pallas_call(
        body, name=name, grid=(pairs,),
        in_specs=[col(0), col(pairs), col(2 * pairs), vec, vec] + [ANY_SPEC] * n_send,
        out_specs=[pl.BlockSpec((s, LANES), lambda p: (0, p))] * 2 + [ANY_SPEC] * n_send,
        out_shape=[jax.ShapeDtypeStruct((s, ATTN_WIDTH), BF16), jax.ShapeDtypeStruct((s, ATTN_WIDTH), F32)]
        + _exchange_shapes(sends, [False] * n_send),
        scratch_shapes=[pltpu.VMEM((HEADS_PER_STEP, s, 4 * HEAD_DIM), BF16)] * 2
        + [pltpu.VMEM((HEADS_PER_STEP * s, LANES), BF16)] + _exchange_sems(n_send),
        compiler_params=_params(1),
    )(proj, proj, proj, qg, kg, *sends)
    return out[0], out[1], out[2:]


def _attn_bwd(proj, dcat, tsum, qg, kg, sends, name):
    s = proj.shape[0]
    n_blk = s // BLK
    pairs = ATTN_WIDTH // LANES
    scale = HEAD_DIM ** -0.5
    n_send = len(sends)
    n_scratch = 7

    def body(q_ref, k_ref, v_ref, do_ref, t_ref, qg_ref, kg_ref, *rest):
        send_refs, rest = rest[:n_send], rest[n_send:]
        (dq_ref, dk_ref, dv_ref, dqg_ref, dkg_ref), rest = rest[:5], rest[5:]
        got_refs, scratch, sems = rest[:n_send], rest[n_send:n_send + n_scratch], rest[n_send + n_scratch:]
        qc_s, kc_s, vd_s, kd_s, qd_s, dob_s, dkv_s = scratch
        copies = _direct_copies(send_refs, got_refs, [True] * n_send, sems)

        @pl.when(pl.program_id(0) == 0)
        def _():
            for cp in copies:
                cp.start()

        tri_p = _tri("upto")
        tri_h = _tri("before")
        diff = _col_minus_row()

        @pl.when(pl.program_id(0) == 0)
        def _():
            dqg_ref[...] = jnp.zeros_like(dqg_ref)
            dkg_ref[...] = jnp.zeros_like(dkg_ref)

        first = _first_head_lanes()
        heads = range(HEADS_PER_STEP)
        _attn_prep(q_ref, k_ref, v_ref, qg_ref, kg_ref, qc_s, kc_s, vd_s, kd_s, n_blk)

        def prep(i, _):
            rows = pl.ds(pl.multiple_of(i * BLK, BLK), BLK)
            both = pl.ds(pl.multiple_of(i * 2 * BLK, 2 * BLK), 2 * BLK)
            dob = do_ref[rows, :].astype(BF16)
            dob_s[rows, :] = dob
            none = jnp.zeros((BLK, HEAD_DIM), BF16)
            for h in heads:
                qd_s[h, both, :] = jnp.concatenate(
                    [jnp.concatenate([qc_s[h, rows, 0:HEAD_DIM], none], axis=1),
                     jnp.concatenate([none, dob[:, h * HEAD_DIM:(h + 1) * HEAD_DIM]], axis=1)], axis=0)
                dkv_s[h, rows, :] = jnp.zeros((BLK, LANES), F32)
            return 0

        lax.fori_loop(0, n_blk, prep, 0)

        def group(qc, qd, dob, tq, qi, grp, pc, hc, dq, masked):
            blocks = [grp * KEY_GROUP + j for j in range(KEY_GROUP)]
            cols_of = [pl.ds(pl.multiple_of(kb * BLK, BLK), BLK) for kb in blocks]
            both_of = [pl.ds(pl.multiple_of(kb * 2 * BLK, 2 * BLK), 2 * BLK) for kb in blocks]
            zs = [[_dot_nt(qc[h], kc_s[h, cols, :]) for cols in cols_of] for h in heads]
            das = [_dot_nt(dob, vd_s[both, :]) for both in both_of]
            keeps = [diff < (qi - kb) * BLK if masked else None for kb in blocks]
            lbs = [[None] * KEY_GROUP for _ in heads]
            scans = [[None] * KEY_GROUP for _ in heads]
            for h in heads:
                for j, z in enumerate(zs[h]):
                    sp = _softplus(z)
                    lom = -sp
                    if masked:
                        lom = jnp.where(keeps[j], lom, 0.0)
                    lbs[h][j] = z - sp
                    scans[h][j] = _scan_rows(lom, tri_p)
            pc, hc = list(pc), list(hc)
            avs = [[None] * KEY_GROUP for _ in heads]
            gws = [[None] * KEY_GROUP for _ in heads]
            hscans = [[None] * KEY_GROUP for _ in heads]
            for h in heads:
                for j in range(KEY_GROUP):
                    p_in, p_tot = scans[h][j]
                    a = jnp.exp(lbs[h][j] + (tq[h] - pc[h] - p_in))
                    if masked:
                        a = jnp.where(keeps[j], a, 0.0)
                    pc[h] = pc[h] + p_tot
                    gw = das[j][:, h * BLK:(h + 1) * BLK] * a
                    avs[h][j] = a.astype(BF16)
                    gws[h][j] = gw
                    hscans[h][j] = _scan_rows(gw, tri_h)
            dzs = [[None] * KEY_GROUP for _ in heads]
            for h in heads:
                for j in range(KEY_GROUP):
                    h_in, g_tot = hscans[h][j]
                    gw = gws[h][j]
                    dz = gw - jnp.exp(lbs[h][j]) * (gw + hc[h] + h_in)
                    if masked:
                        dz = jnp.where(keeps[j], dz, 0.0)
                    hc[h] = hc[h] + g_tot
                    dzs[h][j] = dz.astype(BF16)
            for j, both in enumerate(both_of):
                dq = dq + _dot(jnp.concatenate([dzs[h][j] for h in heads], axis=1), kd_s[both, :])
            for h in heads:
                for j, cols in enumerate(cols_of):
                    dkv_s[h, cols, :] += _dot_tn(jnp.concatenate([dzs[h][j], avs[h][j]], axis=0), qd[h])
            return tuple(pc), tuple(hc), dq

        def q_block(qi, dqg):
            rows = pl.ds(pl.multiple_of(qi * BLK, BLK), BLK)
            both = pl.ds(pl.multiple_of(qi * 2 * BLK, 2 * BLK), 2 * BLK)
            qc = [qc_s[h, rows, :] for h in heads]
            qd = [qd_s[h, both, :] for h in heads]
            dob = dob_s[rows, :]
            tboth = t_ref[rows, :]
            tq = [jnp.concatenate([tboth[:, h * HEAD_DIM:(h + 1) * HEAD_DIM]] * 2, axis=1) for h in heads]
            zero = (jnp.zeros((BLK, BLK), F32),) * HEADS_PER_STEP
            top = qi // KEY_GROUP
            pc, hc, dq = lax.fori_loop(
                0, top, lambda grp, c: group(qc, qd, dob, tq, qi, grp, c[0], c[1], c[2], False),
                (zero, zero, jnp.zeros((BLK, LANES), F32)))
            _, _, dq = group(qc, qd, dob, tq, qi, top, pc, hc, dq, True)
            dq_raw, dg = _pair_norm_bwd(q_ref[rows, :], qg_ref[...], dq * scale, first)
            dq_ref[rows, :] = dq_raw.astype(BF16)
            return dqg + dg

        dqg = lax.fori_loop(0, n_blk, q_block, jnp.zeros((1, LANES), F32))

        def finish(i, dkg):
            rows = pl.ds(pl.multiple_of(i * BLK, BLK), BLK)
            dk = jnp.concatenate([dkv_s[h, rows, 0:HEAD_DIM] for h in heads], axis=1)
            dv = jnp.concatenate([dkv_s[h, rows, HEAD_DIM:2 * HEAD_DIM] for h in heads], axis=1)
            dk_raw, dg = _pair_norm_bwd(k_ref[rows, :], kg_ref[...], dk, first)
            dk_ref[rows, :] = dk_raw.astype(BF16)
            dv_ref[rows, :] = dv.astype(BF16)
            return dkg + dg

        dkg = lax.fori_loop(0, n_blk, finish, jnp.zeros((1, LANES), F32))
        dqg_ref[0:1, :] += dqg
        dkg_ref[0:1, :] += dkg

        @pl.when(pl.program_id(0) == pairs - 1)
        def _():
            for cp in copies:
                cp.wait()

    col = lambda off: pl.BlockSpec((s, LANES), lambda p: (0, off + p))
    vec = pl.BlockSpec((1, LANES), lambda p: (0, 0))
    small = pl.BlockSpec((8, LANES), lambda p: (0, 0))
    out = pl.pallas_call(
        body, name=name, grid=(pairs,),
        in_specs=[col(0), col(pairs), col(2 * pairs), col(0), col(0), vec, vec] + [ANY_SPEC] * n_send,
        out_specs=[col(0)] * 3 + [small] * 2 + [ANY_SPEC] * n_send,
        out_shape=[jax.ShapeDtypeStruct((s, ATTN_WIDTH), BF16)] * 3 + [jax.ShapeDtypeStruct((8, LANES), F32)] * 2
        + _exchange_shapes(sends, [True] * n_send),
        scratch_shapes=[pltpu.VMEM((HEADS_PER_STEP, s, 4 * HEAD_DIM), BF16)] * 2
        + [pltpu.VMEM((HEADS_PER_STEP * s, LANES), BF16)] * 2
        + [pltpu.VMEM((HEADS_PER_STEP, HEADS_PER_STEP * s, LANES), BF16), pltpu.VMEM((s, LANES), BF16),
           pltpu.VMEM((HEADS_PER_STEP, s, LANES), F32)] + _exchange_sems(n_send),
        compiler_params=_params(1),
    )(proj, proj, proj, dcat, tsum, qg, kg, *sends)
    return out[:5], out[5:]


CONV_ROWS = 128


def _shifted(window, shift, halo):
    if shift == 0:
        return window[halo:, :]
    return pltpu.roll(window, shift, 0)[halo:, :]


def _conv_taps(u_s, r0, w_ref, rows):
    window = u_s[pl.ds(r0, CONV_HALO + rows), :]
    y = None
    for k in range(CONV_KERNEL):
        term = _shifted(window, CONV_KERNEL - 1 - k, CONV_HALO) * w_ref[k:k + 1, :]
        y = term if y is None else y + term
    return y


def _conv_fwd(proj, w, b, lg, lb, name):
    s = proj.shape[0]
    cw = w.shape[1]
    rows = CONV_ROWS
    blk_a = (proj.shape[1] - 2 * cw) // cw

    def body(a_ref, g_ref, w_ref, b_ref, lg_ref, lb_ref, o_ref, u_s):
        u_s[0:CONV_HALO, :] = jnp.zeros((CONV_HALO, cw), F32)

        def glu(i, _):
            r0 = pl.multiple_of(i * rows, rows)
            u_s[pl.ds(CONV_HALO + r0, rows), :] = a_ref[pl.ds(r0, rows), :] * _sigmoid(g_ref[pl.ds(r0, rows), :])
            return 0

        lax.fori_loop(0, s // rows, glu, 0)

        def chunk(i, _):
            r0 = pl.multiple_of(i * rows, rows)
            y = _conv_taps(u_s, r0, w_ref, rows) + b_ref[...]
            yc = y - jnp.mean(y, axis=-1, keepdims=True)
            n = yc * lax.rsqrt(jnp.mean(yc * yc, axis=-1, keepdims=True) + EPS)
            ln = n * lg_ref[...] + lb_ref[...]
            o_ref[pl.ds(r0, rows), :] = (ln * _sigmoid(ln)).astype(BF16)
            return 0

        lax.fori_loop(0, s // rows, chunk, 0)

    vec = pl.BlockSpec((1, cw), lambda i: (0, 0))
    return pl.pallas_call(
        body, name=name, grid=(1,),
        in_specs=[pl.BlockSpec((s, cw), lambda i: (0, blk_a)), pl.BlockSpec((s, cw), lambda i: (0, blk_a + 1)),
                  pl.BlockSpec((CONV_KERNEL, cw), lambda i: (0, 0)), vec, vec, vec],
        out_specs=pl.BlockSpec((s, cw), lambda i: (0, 0)),
        out_shape=jax.ShapeDtypeStruct((s, cw), BF16),
        scratch_shapes=[pltpu.VMEM((CONV_HALO + s, cw), F32)],
        compiler_params=_params(1),
    )(proj, proj, w, b, lg, lb)


def _conv_bwd(proj, dcat, w, b, lg, lb, name):
    s = proj.shape[0]
    cw = w.shape[1]
    rows = CONV_ROWS
    blk_a = (proj.shape[1] - 2 * cw) // cw
    n_chunk = s // rows

    def body(a_ref, g_ref, dc_ref, w_ref, b_ref, lg_ref, lb_ref, o_ref, dw_ref, db_ref, dlg_ref, dlb_ref,
             u_s, dy_s, dw_s):
        u_s[0:CONV_HALO, :] = jnp.zeros((CONV_HALO, cw), F32)
        dy_s[pl.ds(s, CONV_HALO), :] = jnp.zeros((CONV_HALO, cw), F32)
        dw_s[...] = jnp.zeros_like(dw_s)

        def glu(i, _):
            r0 = pl.multiple_of(i * rows, rows)
            u_s[pl.ds(CONV_HALO + r0, rows), :] = a_ref[pl.ds(r0, rows), :] * _sigmoid(g_ref[pl.ds(r0, rows), :])
            return 0

        lax.fori_loop(0, n_chunk, glu, 0)

        def chunk(i, carry):
            db, dlg, dlb = carry
            r0 = pl.multiple_of(i * rows, rows)
            window = u_s[pl.ds(r0, CONV_HALO + rows), :]
            y = None
            for k in range(CONV_KERNEL):
                term = _shifted(window, CONV_KERNEL - 1 - k, CONV_HALO) * w_ref[k:k + 1, :]
                y = term if y is None else y + term
            y = y + b_ref[...]
            yc = y - jnp.mean(y, axis=-1, keepdims=True)
            r = lax.rsqrt(jnp.mean(yc * yc, axis=-1, keepdims=True) + EPS)
            n = yc * r
            ln = n * lg_ref[...] + lb_ref[...]
            sg = _sigmoid(ln)
            dln = dc_ref[pl.ds(r0, rows), :] * (sg * (1.0 + ln * (1.0 - sg)))
            dn = dln * lg_ref[...]
            dy = r * (dn - jnp.mean(dn, axis=-1, keepdims=True) - n * jnp.mean(dn * n, axis=-1, keepdims=True))
            dy_s[pl.ds(r0, rows), :] = dy
            for k in range(CONV_KERNEL):
                prod = _shifted(window, CONV_KERNEL - 1 - k, CONV_HALO) * dy
                dw_s[k] += jnp.sum(prod.reshape(rows // 8, 8, cw), axis=0)
            return (db + jnp.sum(dy, axis=0, keepdims=True),
                    dlg + jnp.sum(dln * n, axis=0, keepdims=True),
                    dlb + jnp.sum(dln, axis=0, keepdims=True))

        zero = jnp.zeros((1, cw), F32)
        db, dlg, dlb = lax.fori_loop(0, n_chunk, chunk, (zero, zero, zero))
        db_ref[...] = db
        dlg_ref[...] = dlg
        dlb_ref[...] = dlb
        for k in range(CONV_KERNEL):
            dw_ref[k:k + 1, :] = jnp.sum(dw_s[k], axis=0, keepdims=True)

        def back(i, _):
            r0 = pl.multiple_of(i * rows, rows)
            window = dy_s[pl.ds(r0, rows + CONV_HALO), :]
            du = None
            for k in range(CONV_KERNEL):
                sh = CONV_KERNEL - 1 - k
                shifted = window[:rows, :] if sh == 0 else pltpu.roll(window, rows + CONV_HALO - sh, 0)[:rows, :]
                term = shifted * w_ref[k:k + 1, :]
                du = term if du is None else du + term
            av = a_ref[pl.ds(r0, rows), :]
            sg = _sigmoid(g_ref[pl.ds(r0, rows), :])
            o_ref[pl.ds(r0, rows), 0:cw] = (du * sg).astype(BF16)
            o_ref[pl.ds(r0, rows), cw:2 * cw] = (du * av * sg * (1.0 - sg)).astype(BF16)
            return 0

        lax.fori_loop(0, n_chunk, back, 0)

    vec = pl.BlockSpec((1, cw), lambda i: (0, 0))
    wspec = pl.BlockSpec((CONV_KERNEL, cw), lambda i: (0, 0))
    return pl.pallas_call(
        body, name=name, grid=(1,),
        in_specs=[pl.BlockSpec((s, cw), lambda i: (0, blk_a)), pl.BlockSpec((s, cw), lambda i: (0, blk_a + 1)),
                  pl.BlockSpec((s, cw), lambda i: (0, 1)), wspec, vec, vec, vec],
        out_specs=[pl.BlockSpec((s, 2 * cw), lambda i: (0, 0)), wspec, vec, vec, vec],
        out_shape=[jax.ShapeDtypeStruct((s, 2 * cw), BF16), jax.ShapeDtypeStruct((CONV_KERNEL, cw), F32)]
        + [jax.ShapeDtypeStruct((1, cw), F32)] * 3,
        scratch_shapes=[pltpu.VMEM((CONV_HALO + s, cw), F32), pltpu.VMEM((s + CONV_HALO, cw), F32),
                        pltpu.VMEM((CONV_KERNEL, 8, cw), F32)],
        compiler_params=_params(1),
    )(proj, proj, dcat, w, b, lg, lb)


FFN_ROWS = 256


def _ffn_gate(g_ref, r0, rows, w_ref, b_ref):
    cur = g_ref[pl.ds(r0, rows), :]
    prev = g_ref[pl.ds(pl.multiple_of(jnp.maximum(r0 - FFN_HALO, 0), FFN_HALO), FFN_HALO), :]
    prev = jnp.where(r0 > 0, prev, 0.0)
    window = jnp.concatenate([prev, cur], axis=0)
    gc = cur * w_ref[FFN_KERNEL - 1:FFN_KERNEL, :] + b_ref[...]
    for k in range(FFN_KERNEL - 1):
        gc = gc + _shifted(window, FFN_KERNEL - 1 - k, FFN_HALO) * w_ref[k:k + 1, :]
    return gc, window


def _ffn_fwd(up, w, b, name):
    s = up.shape[0]
    f = w.shape[1]
    tc = _pick(f, (256, 128))
    nc = f // tc
    rows = _pick(s, (FFN_ROWS, 128))

    def body(g_ref, v_ref, w_ref, b_ref, o_ref):
        def chunk(i, _):
            r0 = pl.multiple_of(i * rows, rows)
            gc, _w = _ffn_gate(g_ref, r0, rows, w_ref, b_ref)
            o_ref[pl.ds(r0, rows), :] = (gc * _sigmoid(gc) * v_ref[pl.ds(r0, rows), :]).astype(BF16)
            return 0

        lax.fori_loop(0, s // rows, chunk, 0)

    return pl.pallas_call(
        body, name=name, grid=(nc,),
        in_specs=[pl.BlockSpec((s, tc), lambda j: (0, j)), pl.BlockSpec((s, tc), lambda j: (0, nc + j)),
                  pl.BlockSpec((FFN_KERNEL, tc), lambda j: (0, j)), pl.BlockSpec((1, tc), lambda j: (0, j))],
        out_specs=pl.BlockSpec((s, tc), lambda j: (0, j)),
        out_shape=jax.ShapeDtypeStruct((s, f), BF16),
        compiler_params=_params(1),
    )(up, up, w, b)


def _ffn_bwd(up, dact, w, b, name):
    s = up.shape[0]
    f = w.shape[1]
    tc = _pick(f, (256, 128))
    nc = f // tc
    rows = _pick(s, (FFN_ROWS, 128))
    n_chunk = s // rows

    def body(g_ref, v_ref, da_ref, w_ref, b_ref, d_ref, dw_ref, db_ref, dgc_s):
        dg_ref, dv_ref = d_ref.at[0], d_ref.at[1]
        dgc_s[pl.ds(s, FFN_HALO), :] = jnp.zeros((FFN_HALO, tc), F32)

        def chunk(i, carry):
            r0 = pl.multiple_of(i * rows, rows)
            gc, window = _ffn_gate(g_ref, r0, rows, w_ref, b_ref)
            sg = _sigmoid(gc)
            da = da_ref[pl.ds(r0, rows), :]
            dv_ref[pl.ds(r0, rows), :] = (da * gc * sg).astype(BF16)
            dgc = da * v_ref[pl.ds(r0, rows), :] * (sg * (1.0 + gc * (1.0 - sg)))
            dgc_s[pl.ds(r0, rows), :] = dgc
            out = [carry[0] + jnp.sum(dgc, axis=0, keepdims=True)]
            for k in range(FFN_KERNEL):
                out.append(carry[1 + k] + jnp.sum(_shifted(window, FFN_KERNEL - 1 - k, FFN_HALO) * dgc,
                                                  axis=0, keepdims=True))
            return tuple(out)

        zero = jnp.zeros((1, tc), F32)
        sums = lax.fori_loop(0, n_chunk, chunk, (zero,) * (1 + FFN_KERNEL))
        db_ref[...] = sums[0]
        for k in range(FFN_KERNEL):
            dw_ref[k:k + 1, :] = sums[1 + k]

        def back(i, _):
            r0 = pl.multiple_of(i * rows, rows)
            window = dgc_s[pl.ds(r0, rows + FFN_HALO), :]
            dg = window[:rows, :] * w_ref[FFN_KERNEL - 1:FFN_KERNEL, :]
            for k in range(FFN_KERNEL - 1):
                sh = FFN_KERNEL - 1 - k
                dg = dg + pltpu.roll(window, rows + FFN_HALO - sh, 0)[:rows, :] * w_ref[k:k + 1, :]
            dg_ref[pl.ds(r0, rows), :] = dg.astype(BF16)
            return 0

        lax.fori_loop(0, n_chunk, back, 0)

    blk = lambda off: pl.BlockSpec((s, tc), lambda j: (0, off + j))
    return pl.pallas_call(
        body, name=name, grid=(nc,),
        in_specs=[blk(0), blk(nc), blk(0), pl.BlockSpec((FFN_KERNEL, tc), lambda j: (0, j)),
                  pl.BlockSpec((1, tc), lambda j: (0, j))],
        out_specs=[pl.BlockSpec((2, s, tc), lambda j: (0, 0, j)), pl.BlockSpec((FFN_KERNEL, tc), lambda j: (0, j)),
                   pl.BlockSpec((1, tc), lambda j: (0, j))],
        out_shape=[jax.ShapeDtypeStruct((2, s, f), BF16),
                   jax.ShapeDtypeStruct((FFN_KERNEL, f), F32), jax.ShapeDtypeStruct((1, f), F32)],
        scratch_shapes=[pltpu.VMEM((s + FFN_HALO, tc), F32)],
        compiler_params=_params(1),
    )(up, up, dact, w, b)


def _loss_head(y, target, name):
    m, n = y.shape
    tm = _pick(m, (256, 128))

    def body(y_ref, t_ref, l_ref, d_ref, db_ref):
        e = y_ref[...] - t_ref[...]
        part = 0.5 * jnp.sum(jnp.sum(e * e, axis=-1, keepdims=True) / n, axis=0, keepdims=True)

        @pl.when(pl.program_id(0) == 0)
        def _():
            l_ref[...] = jnp.zeros_like(l_ref)

        l_ref[...] += part
        d = e / n
        d_ref[...] = d
        db_ref[...] = d.astype(BF16)

    return pl.pallas_call(
        body, name=name, grid=(m // tm,),
        in_specs=[pl.BlockSpec((tm, n), lambda i: (i, 0))] * 2,
        out_specs=[pl.BlockSpec((8, LANES), lambda i: (0, 0)), pl.BlockSpec((tm, n), lambda i: (i, 0)),
                   pl.BlockSpec((tm, n), lambda i: (i, 0))],
        out_shape=[jax.ShapeDtypeStruct((8, LANES), F32), jax.ShapeDtypeStruct((m, n), F32),
                   jax.ShapeDtypeStruct((m, n), BF16)],
        compiler_params=_params(1),
    )(y, target)


def _adamw_math(w, g, m, v):
    m = ADAM_B1 * m + (1.0 - ADAM_B1) * g
    v = ADAM_B2 * v + (1.0 - ADAM_B2) * (g * g)
    m_hat = m / (1.0 - ADAM_B1 ** ADAM_STEP)
    v_hat = v / (1.0 - ADAM_B2 ** ADAM_STEP)
    delta = -ADAM_LR * (m_hat / (jnp.sqrt(v_hat) + ADAM_EPS) + ADAM_WD * w)
    return delta, m, v


def _sum_adamw(parts, w, m, v, name):
    depth, r, c = w.shape
    tr = max(t for t in range(16, min(r, 192) + 1, 16) if r % t == 0)
    steps = r // tr

    def body(*refs):
        p_refs = refs[:depth]
        w_ref, m_ref, v_ref, g_out, d_out, m_out, v_out = refs[depth:]
        for layer in range(depth):
            @pl.when(pl.program_id(0) == layer)
            def _():
                g = p_refs[layer][0].astype(F32)
                for src in range(1, N_DEV):
                    g = g + p_refs[layer][src].astype(F32)
                d, mn, vn = _adamw_math(w_ref[0], g, m_ref[0], v_ref[0])
                g_out[0] = g
                d_out[0] = d
                m_out[0] = mn
                v_out[0] = vn

    def part_spec(layer):
        return pl.BlockSpec((N_DEV, tr, c), lambda l, i: (0, jnp.clip((l - layer) * steps + i, 0, steps - 1), 0))

    blk = pl.BlockSpec((1, tr, c), lambda l, i: (l, i, 0))
    return pl.pallas_call(
        body, name=name, grid=(depth, steps),
        in_specs=[part_spec(layer) for layer in range(depth)] + [blk, blk, blk],
        out_specs=[blk] * 4,
        out_shape=[jax.ShapeDtypeStruct(w.shape, F32)] * 4,
        compiler_params=_params(2),
    )(*parts, w, m, v)


def _sum_rows(parts, name):
    _, r, c = parts.shape

    def body(p_ref, o_ref):
        g = p_ref[0]
        for src in range(1, N_DEV):
            g = g + p_ref[src]
        o_ref[...] = g

    return pl.pallas_call(
        body, name=name, grid=(1,),
        in_specs=[pl.BlockSpec((N_DEV, r, c), lambda i: (0, 0, 0))],
        out_specs=pl.BlockSpec((r, c), lambda i: (0, 0)),
        out_shape=jax.ShapeDtypeStruct((r, c), F32),
        compiler_params=_params(1),
    )(parts)


def _adamw_rows(w, g, m, v, name):
    r, c = w.shape

    def body(w_ref, g_ref, m_ref, v_ref, d_out, m_out, v_out):
        d, mn, vn = _adamw_math(w_ref[...], g_ref[...], m_ref[...], v_ref[...])
        d_out[...] = d
        m_out[...] = mn
        v_out[...] = vn

    blk = pl.BlockSpec((r, c), lambda i: (0, 0))
    return pl.pallas_call(
        body, name=name, grid=(1,), in_specs=[blk] * 4, out_specs=[blk] * 3,
        out_shape=[jax.ShapeDtypeStruct((r, c), F32)] * 3,
        compiler_params=_params(1),
    )(w, g, m, v)


PACK_TILE = 8 * LANES


def _pack(arrays):
    rows = []
    for a in arrays:
        flat = a.reshape(-1).astype(F32)
        pad = (-flat.shape[0]) % PACK_TILE
        rows.append(jnp.pad(flat, (0, pad)).reshape(-1, LANES))
    return jnp.concatenate(rows, axis=0)


def _unpack(packed, shapes, lead=()):
    out, r0 = [], 0
    for shp in shapes:
        size = 1
        for d in shp:
            size *= d
        nrows = -(-size // PACK_TILE) * 8
        piece = packed[..., r0:r0 + nrows, :].reshape(lead + (nrows * LANES,))[..., :size]
        out.append(piece.reshape(lead + tuple(shp)))
        r0 += nrows
    return out


def kernel(x, norm1_g, w_in, q_norm_g, k_norm_g, conv_dw_w, conv_dw_b, conv_ln_g, conv_ln_b, w_out, norm2_g, w_up, ffn_dw_w, ffn_dw_b, w_down, loss_target, m_norm1_g, m_w_in, m_q_norm_g, m_k_norm_g, m_conv_dw_w, m_conv_dw_b, m_conv_ln_g, m_conv_ln_b, m_w_out, m_norm2_g, m_w_up, m_ffn_dw_w, m_ffn_dw_b, m_w_down, v_norm1_g, v_w_in, v_q_norm_g, v_k_norm_g, v_conv_dw_w, v_conv_dw_b, v_conv_ln_g, v_conv_ln_b, v_w_out, v_norm2_g, v_w_up, v_ffn_dw_w, v_ffn_dw_b, v_w_down):
    depth, d_model, in_shard = w_in.shape
    out_shard = w_out.shape[1]
    up_shard = w_up.shape[2]
    down_shard = w_down.shape[1]
    d_ff = down_shard * N_DEV
    conv_w = conv_dw_b.shape[1]
    cw_shard = conv_dw_w.shape[2]
    fw_shard = ffn_dw_w.shape[2]
    me = 4 * lax.axis_index("x") + 2 * lax.axis_index("y") + lax.axis_index("c")

    transposed = lambda a: a.transpose(0, 2, 1)
    b_in, b_out, b_up, b_down = (transposed(w_in).astype(BF16), w_out.astype(BF16), transposed(w_up).astype(BF16),
                                 w_down.astype(BF16))
    rows_major = lambda g: g.reshape(N_DEV * g.shape[1], g.shape[2])
    g_in0, g_small = _exchange([b_in[0], _pack([conv_dw_w, ffn_dw_w])], [False, False], name="gather_first")
    wf_in, wf_out, wf_up, wf_down = [rows_major(g_in0)] + [None] * (depth - 1), [None] * depth, [None] * depth, [None] * depth
    g_cw, g_fw = _unpack(g_small, [conv_dw_w.shape, ffn_dw_w.shape], lead=(N_DEV,))
    cwf = g_cw.transpose(1, 2, 0, 3).reshape(depth, CONV_KERNEL, conv_w)
    fwf = g_fw.transpose(1, 2, 0, 3).reshape(depth, FFN_KERNEL, d_ff)

    row = lambda a, l: a[l].reshape(1, -1)
    both_heads = lambda a, l: jnp.tile(row(a, l), (1, HEADS_PER_STEP))

    xs = x[0]
    saved = []
    for l in range(depth):
        proj, h1 = _mm_rms(xs, row(norm1_g, l), wf_in[l], name="fwd_in")
        sends = [b_out[l], b_up[l], b_down[l]] + ([b_in[l + 1]] if l + 1 < depth else [])
        attn, tsum, got = _attn_fwd(proj, both_heads(q_norm_g, l), both_heads(k_norm_g, l), sends, name="fwd_attn")
        wf_out[l], wf_up[l], wf_down[l] = rows_major(got[0]), rows_major(got[1]), rows_major(got[2])
        if l + 1 < depth:
            wf_in[l + 1] = rows_major(got[3])
        conv = _conv_fwd(proj, cwf[l], row(conv_dw_b, l), row(conv_ln_g, l), row(conv_ln_b, l), name="fwd_conv")
        cat = jnp.concatenate([attn, conv], axis=1)
        x_mid = _mm_res(cat, wf_out[l], xs, name="fwd_out")
        up, h2 = _mm_rms(x_mid, row(norm2_g, l), wf_up[l], name="fwd_up")
        act = _ffn_fwd(up, fwf[l], row(ffn_dw_b, l), name="fwd_ffn")
        x_next = _mm_res(act, wf_down[l], x_mid, name="fwd_down")
        saved.append((xs, h1, proj, tsum, cat, x_mid, h2, up, act))
        xs = x_next

    loss_tile, dx, dxb = _loss_head(xs, loss_target[0], name="loss_head")
    loss = lax.psum(loss_tile[0, 0], ("x", "y", "c"))

    r_in, r_out, r_up, r_down = [None] * depth, [None] * depth, [None] * depth, [None] * depth
    row_blocks = lambda g: g.reshape(N_DEV, g.shape[0] // N_DEV, g.shape[1])
    small = {k: [None] * depth for k in ("norm1_g", "q_norm_g", "k_norm_g", "conv_dw_w", "conv_dw_b", "conv_ln_g",
                                         "conv_ln_b", "norm2_g", "ffn_dw_w", "ffn_dw_b")}
    gw_in = None
    for l in reversed(range(depth)):
        xs, h1, proj, tsum, cat, x_mid, h2, up, act = saved[l]
        dact = _mm_nt(dxb, wf_down[l], name="bwd_dact")
        gw_down = _mm_tn(act, dxb, name="bwd_gw_down")
        dup, small["ffn_dw_w"][l], small["ffn_dw_b"][l] = _ffn_bwd(up, dact, fwf[l], row(ffn_dw_b, l), name="bwd_ffn")
        gw_up = _mm_tn(dup, h2, name="bwd_gw_up")
        dx, dxb, small["norm2_g"][l] = _mm_rmsbwd(dup, wf_up[l], x_mid, row(norm2_g, l), dx, name="bwd_up")
        dcat = _mm_nt(dxb, wf_out[l], name="bwd_dcat")
        gw_out = _mm_tn(cat, dxb, name="bwd_gw_out")
        dglu, small["conv_dw_w"][l], small["conv_dw_b"][l], small["conv_ln_g"][l], small["conv_ln_b"][l] = _conv_bwd(
            proj, dcat, cwf[l], row(conv_dw_b, l), row(conv_ln_g, l), row(conv_ln_b, l), name="bwd_conv")
        sends = [row_blocks(gw_down), row_blocks(gw_up), row_blocks(gw_out)] + ([row_blocks(gw_in)] if l + 1 < depth else [])
        (dq, dk, dv, dqg, dkg), got = _attn_bwd(proj, dcat, tsum, both_heads(q_norm_g, l), both_heads(k_norm_g, l),
                                                sends, name="bwd_attn")
        r_down[l], r_up[l], r_out[l] = got[:3]
        if l + 1 < depth:
            r_in[l + 1] = got[3]
        small["q_norm_g"][l] = dqg[0:1, :HEAD_DIM] + dqg[0:1, HEAD_DIM:]
        small["k_norm_g"][l] = dkg[0:1, :HEAD_DIM] + dkg[0:1, HEAD_DIM:]
        dproj = jnp.concatenate([dq, dk, dv, dglu], axis=1)
        gw_in = _mm_tn(dproj, h1, name="bwd_gw_in")
        dx, dxb, small["norm1_g"][l] = _mm_rmsbwd(dproj, wf_in[l], xs, row(norm1_g, l), dx, name="bwd_in")
    grad_x = dx[None]

    names = ["norm1_g", "q_norm_g", "k_norm_g", "conv_dw_w", "conv_dw_b", "conv_ln_g", "conv_ln_b", "norm2_g",
             "ffn_dw_w", "ffn_dw_b"]
    full_shapes = {"norm1_g": norm1_g.shape, "q_norm_g": q_norm_g.shape, "k_norm_g": k_norm_g.shape,
                   "conv_dw_w": (depth, CONV_KERNEL, conv_w), "conv_dw_b": conv_dw_b.shape,
                   "conv_ln_g": conv_ln_g.shape, "conv_ln_b": conv_ln_b.shape, "norm2_g": norm2_g.shape,
                   "ffn_dw_w": (depth, FFN_KERNEL, d_ff), "ffn_dw_b": ffn_dw_b.shape}
    partial = _pack([jnp.stack(small[k]).reshape(full_shapes[k]) for k in names])
    r_in[0], all_partials = _exchange([row_blocks(gw_in), partial], [True, False], name="exchange_last")
    big = {
        "w_in": [transposed(a) for a in _sum_adamw(r_in, transposed(w_in), transposed(m_w_in), transposed(v_w_in),
                                                   name="adamw_in")],
        "w_out": _sum_adamw(r_out, w_out, m_w_out, v_w_out, name="adamw_out"),
        "w_up": [transposed(a) for a in _sum_adamw(r_up, transposed(w_up), transposed(m_w_up), transposed(v_w_up),
                                                   name="adamw_up")],
        "w_down": _sum_adamw(r_down, w_down, m_w_down, v_w_down, name="adamw_down"),
    }

    total = _unpack(_sum_rows(all_partials, name="sum_small_grads"), [full_shapes[k] for k in names])
    grads = dict(zip(names, total))
    grads["conv_dw_w"] = lax.dynamic_slice_in_dim(grads["conv_dw_w"], me * cw_shard, cw_shard, axis=2)
    grads["ffn_dw_w"] = lax.dynamic_slice_in_dim(grads["ffn_dw_w"], me * fw_shard, fw_shard, axis=2)
    weights = dict(norm1_g=norm1_g, q_norm_g=q_norm_g, k_norm_g=k_norm_g, conv_dw_w=conv_dw_w, conv_dw_b=conv_dw_b,
                   conv_ln_g=conv_ln_g, conv_ln_b=conv_ln_b, norm2_g=norm2_g, ffn_dw_w=ffn_dw_w, ffn_dw_b=ffn_dw_b)
    m_in = dict(norm1_g=m_norm1_g, q_norm_g=m_q_norm_g, k_norm_g=m_k_norm_g, conv_dw_w=m_conv_dw_w,
                conv_dw_b=m_conv_dw_b, conv_ln_g=m_conv_ln_g, conv_ln_b=m_conv_ln_b, norm2_g=m_norm2_g,
                ffn_dw_w=m_ffn_dw_w, ffn_dw_b=m_ffn_dw_b)
    v_in = dict(norm1_g=v_norm1_g, q_norm_g=v_q_norm_g, k_norm_g=v_k_norm_g, conv_dw_w=v_conv_dw_w,
                conv_dw_b=v_conv_dw_b, conv_ln_g=v_conv_ln_g, conv_ln_b=v_conv_ln_b, norm2_g=v_norm2_g,
                ffn_dw_w=v_ffn_dw_w, ffn_dw_b=v_ffn_dw_b)
    shard_shapes = [weights[k].shape for k in names]
    d_s, m_s, v_s = _adamw_rows(_pack([weights[k] for k in names]), _pack([grads[k] for k in names]),
                                _pack([m_in[k] for k in names]), _pack([v_in[k] for k in names]), name="adamw_small")
    delta = dict(zip(names, _unpack(d_s, shard_shapes)))
    new_m = dict(zip(names, _unpack(m_s, shard_shapes)))
    new_v = dict(zip(names, _unpack(v_s, shard_shapes)))
    for k, (g, d, mn, vn) in big.items():
        grads[k], delta[k], new_m[k], new_v[k] = g, d, mn, vn

    order = ["norm1_g", "w_in", "q_norm_g", "k_norm_g", "conv_dw_w", "conv_dw_b", "conv_ln_g", "conv_ln_b", "w_out",
             "norm2_g", "w_up", "ffn_dw_w", "ffn_dw_b", "w_down"]
    return (loss, grad_x, *[grads[k] for k in order], *[delta[k] for k in order], *[new_m[k] for k in order],
            *[new_v[k] for k in order])
```

```python
import functools

import jax
import jax.numpy as jnp
from jax import lax
from jax.experimental import pallas as pl
from jax.experimental.pallas import tpu as pltpu

F32 = jnp.float32
BF16 = jnp.bfloat16

N_DEV = 8
HEADS = 8
HEAD_DIM = 64
ATTN_WIDTH = HEADS * HEAD_DIM
CONV_KERNEL = 31
FFN_KERNEL = 3
EPS = 1e-6
BLK = 128
KEY_GROUP = 4
LANES = 128
NORM_ROWS = 128
CONV_HALO = 32
FFN_HALO = 16

ADAM_LR = 0.001
ADAM_B1 = 0.9
ADAM_B2 = 0.999
ADAM_EPS = 1e-08
ADAM_WD = 0.01
ADAM_STEP = 10

VMEM_LIMIT = 56 * 1024 * 1024


def _params(n_axes=0):
    kw = dict(vmem_limit_bytes=VMEM_LIMIT)
    if n_axes:
        kw["dimension_semantics"] = ("arbitrary",) * n_axes
    return pltpu.CompilerParams(**kw)


def _dot(a, b):
    return jnp.dot(a, b, preferred_element_type=F32)


def _dot_nt(a, b):
    return lax.dot_general(a, b, (((1,), (1,)), ((), ())), preferred_element_type=F32)


def _dot_tn(a, b):
    return lax.dot_general(a, b, (((0,), (0,)), ((), ())), preferred_element_type=F32)


def _sigmoid(x):
    return 1.0 / (1.0 + jnp.exp(-x))


def _split_bf16(x):
    hi = x.astype(BF16)
    lo = (x - hi.astype(F32)).astype(BF16)
    return hi, lo


def _pick(n, options):
    for t in options:
        if n % t == 0:
            return t
    return n


def _tile(n, cap):
    best = None
    for t in range(LANES, min(n, cap) + 1, LANES):
        if n % t == 0:
            best = t
    return best or n


def _mm_rms(x, g, wt, out_dtype, sends, name):
    m, k = x.shape
    n = wt.shape[0]
    tm = _tile(m, 2048)
    tn = _tile(n, 512)
    n_send = len(sends)
    grid = (m // tm, n // tn)

    def body(x_ref, g_ref, w_ref, *rest):
        send_refs, (o_ref, h_ref), rest = rest[:n_send], rest[n_send:n_send + 2], rest[n_send + 2:]
        got_refs, h_s, sems = rest[:n_send], rest[n_send], rest[n_send + 1:]
        if n_send:
            start, relay, finish = _two_level_gather(send_refs, got_refs, sems)
            is_first = (pl.program_id(0) == 0) & (pl.program_id(1) == 0)
            is_last = (pl.program_id(0) == grid[0] - 1) & (pl.program_id(1) == grid[1] - 1)
            pl.when(is_first)(start)
            pl.when(is_last)(relay)

        @pl.when(pl.program_id(1) == 0)
        def _():
            def chunk(c, _):
                rows = pl.ds(pl.multiple_of(c * NORM_ROWS, NORM_ROWS), NORM_ROWS)
                xv = x_ref[rows, :]
                r = lax.rsqrt(jnp.mean(xv * xv, axis=-1, keepdims=True) + EPS)
                hv = (xv * r * g_ref[...]).astype(BF16)
                h_s[rows, :] = hv
                h_ref[rows, :] = hv
                return 0

            lax.fori_loop(0, tm // NORM_ROWS, chunk, 0)

        o_ref[...] = _dot_nt(h_s[...], w_ref[...]).astype(out_dtype)
        if n_send:
            pl.when(is_last)(finish)

    out = pl.pallas_call(
        body, name=name, grid=grid,
        in_specs=[pl.BlockSpec((tm, k), lambda i, j: (i, 0)),
                  pl.BlockSpec((1, k), lambda i, j: (0, 0)),
                  pl.BlockSpec((tn, k), lambda i, j: (j, 0))] + [ANY_SPEC] * n_send,
        out_specs=[pl.BlockSpec((tm, tn), lambda i, j: (i, j)),
                   pl.BlockSpec((tm, k), lambda i, j: (i, 0))] + [ANY_SPEC] * n_send,
        out_shape=[jax.ShapeDtypeStruct((m, n), out_dtype), jax.ShapeDtypeStruct((m, k), BF16)]
        + _exchange_shapes(sends, [False] * n_send),
        scratch_shapes=[pltpu.VMEM((tm, k), BF16)] + (_exchange_sems(n_send) if n_send else []),
        compiler_params=_params(2),
    )(x, g, wt, *sends)
    return out[0], out[1], out[2:]


def _mm_res(a, w, res, name):
    m, k = a.shape
    n = w.shape[1]
    tm = _tile(m, 1024)
    tn = _tile(n, 512)

    def body(a_ref, w_ref, r_ref, o_ref):
        o_ref[...] = r_ref[...] + _dot(a_ref[...], w_ref[...])

    return pl.pallas_call(
        body, name=name, grid=(m // tm, n // tn),
        in_specs=[pl.BlockSpec((tm, k), lambda i, j: (i, 0)),
                  pl.BlockSpec((k, tn), lambda i, j: (0, j)),
                  pl.BlockSpec((tm, tn), lambda i, j: (i, j))],
        out_specs=pl.BlockSpec((tm, tn), lambda i, j: (i, j)),
        out_shape=jax.ShapeDtypeStruct((m, n), F32),
        compiler_params=_params(2),
    )(a, w, res)


def _mm_nt(a, w, out_dtype, name):
    m, k = a.shape
    n = w.shape[0]
    tm = _tile(m, 1024)
    tn = _tile(n, 1408)

    def body(a_ref, w_ref, o_ref):
        o_ref[...] = _dot_nt(a_ref[...], w_ref[...]).astype(out_dtype)

    return pl.pallas_call(
        body, name=name, grid=(m // tm, n // tn),
        in_specs=[pl.BlockSpec((tm, k), lambda i, j: (i, 0)),
                  pl.BlockSpec((tn, k), lambda i, j: (j, 0))],
        out_specs=pl.BlockSpec((tm, tn), lambda i, j: (i, j)),
        out_shape=jax.ShapeDtypeStruct((m, n), out_dtype),
        compiler_params=_params(2),
    )(a, w)


def _column_tiles(a, cap):
    if a.ndim == 2:
        s, c = a.shape
        tc = _tile(c, cap)
        return s, c, tc, lambda rows, index: pl.BlockSpec((rows, tc), lambda *g: (index(*g)[0], index(*g)[1]))
    slabs, s, width = a.shape
    tc = _tile(width, cap)
    per = width // tc
    return s, slabs * width, tc, lambda rows, index: pl.BlockSpec(
        (None, rows, tc), lambda *g: (index(*g)[1] // per, index(*g)[0], index(*g)[1] % per))


def _mm_tn(a, b, name):
    s, m, tm, a_spec = _column_tiles(a, 1408)
    n = b.shape[1]
    tn = _tile(n, 1024)

    def body(a_ref, b_ref, o_ref):
        o_ref[...] = _dot_tn(a_ref[...], b_ref[...]).astype(BF16)

    return pl.pallas_call(
        body, name=name, grid=(m // tm, n // tn),
        in_specs=[a_spec(s, lambda i, j: (0, i)),
                  pl.BlockSpec((s, tn), lambda i, j: (0, j))],
        out_specs=pl.BlockSpec((tm, tn), lambda i, j: (i, j)),
        out_shape=jax.ShapeDtypeStruct((m, n), BF16),
        compiler_params=_params(2),
    )(a, b)


def _mm_rmsbwd(a, w, x, g, dres, sends, name):
    m, k, tk, a_spec = _column_tiles(a, 1408)
    n = w.shape[1]
    tm = _tile(m, 1024)
    nk = k // tk
    n_send = len(sends)

    def body(a_ref, w_ref, x_ref, g_ref, r_ref, *rest):
        send_refs, (dx_ref, dxb_ref, dg_ref), rest = rest[:n_send], rest[n_send:n_send + 3], rest[n_send + 3:]
        got_refs, acc, sems = rest[:n_send], rest[n_send], rest[n_send + 1:]
        i, kk = pl.program_id(0), pl.program_id(1)
        if n_send:
            copies = _direct_copies(send_refs, got_refs, [True] * n_send, sems)

            @pl.when((i == 0) & (kk == 0))
            def _():
                for cp in copies:
                    cp.start()

        part = _dot(a_ref[...], w_ref[...])

        @pl.when(kk == 0)
        def _():
            acc[...] = part

        @pl.when(kk > 0)
        def _():
            acc[...] += part

        @pl.when(kk == nk - 1)
        def _():
            def chunk(c, dgp):
                rows = pl.ds(pl.multiple_of(c * NORM_ROWS, NORM_ROWS), NORM_ROWS)
                dh = acc[rows, :]
                xv = x_ref[rows, :]
                r = lax.rsqrt(jnp.mean(xv * xv, axis=-1, keepdims=True) + EPS)
                xh = xv * r
                dxh = dh * g_ref[...]
                dx = r_ref[rows, :] + r * (dxh - xh * jnp.mean(dxh * xh, axis=-1, keepdims=True))
                dx_ref[rows, :] = dx
                dxb_ref[rows, :] = dx.astype(BF16)
                return dgp + jnp.sum(dh * xh, axis=0, keepdims=True)

            dgp = lax.fori_loop(0, tm // NORM_ROWS, chunk, jnp.zeros((1, n), F32))

            @pl.when(i == 0)
            def _():
                dg_ref[...] = dgp

            @pl.when(i > 0)
            def _():
                dg_ref[...] += dgp

        if n_send:
            @pl.when((i == m // tm - 1) & (kk == nk - 1))
            def _():
                for cp in copies:
                    cp.wait()

    out = pl.pallas_call(
        body, name=name, grid=(m // tm, nk),
        in_specs=[a_spec(tm, lambda i, kk: (i, kk)),
                  pl.BlockSpec((tk, n), lambda i, kk: (kk, 0)),
                  pl.BlockSpec((tm, n), lambda i, kk: (i, 0)),
                  pl.BlockSpec((1, n), lambda i, kk: (0, 0)),
                  pl.BlockSpec((tm, n), lambda i, kk: (i, 0))] + [ANY_SPEC] * n_send,
        out_specs=[pl.BlockSpec((tm, n), lambda i, kk: (i, 0)),
                   pl.BlockSpec((tm, n), lambda i, kk: (i, 0)),
                   pl.BlockSpec((1, n), lambda i, kk: (0, 0))] + [ANY_SPEC] * n_send,
        out_shape=[jax.ShapeDtypeStruct((m, n), F32), jax.ShapeDtypeStruct((m, n), BF16),
                   jax.ShapeDtypeStruct((1, n), F32)] + _exchange_shapes(sends, [True] * n_send),
        scratch_shapes=[pltpu.VMEM((tm, n), F32)] + (_exchange_sems(n_send) if n_send else []),
        compiler_params=_params(2),
    )(a, w, x, g, dres, *sends)
    return out[0], out[1], out[2], out[3:]


ANY_SPEC = pl.BlockSpec(memory_space=pl.ANY)
SEMS_PER_OPERAND = N_DEV - 1


def _exchange_sems(n):
    return [pltpu.SemaphoreType.DMA((n, SEMS_PER_OPERAND)), pltpu.SemaphoreType.DMA((n, SEMS_PER_OPERAND)),
            pltpu.SemaphoreType.DMA((n,))]


def _exchange_shapes(parts, scatter):
    return [jax.ShapeDtypeStruct(a.shape if sc else (N_DEV,) + a.shape, a.dtype) for a, sc in zip(parts, scatter)]


def _flat(pos):
    return 4 * pos[0] + 2 * pos[1] + pos[2]


def _remote(src, dst, sems, i, k, to):
    send_sems, recv_sems, _ = sems
    return pltpu.make_async_remote_copy(src_ref=src, dst_ref=dst, send_sem=send_sems.at[i, k],
                                        recv_sem=recv_sems.at[i, k], device_id=to,
                                        device_id_type=pl.DeviceIdType.MESH)


def _direct_copies(ins, outs, scatter, sems):
    x, y, c = lax.axis_index("x"), lax.axis_index("y"), lax.axis_index("c")
    me = _flat((x, y, c))
    copies = []
    for i in range(len(ins)):
        src = ins[i].at[me] if scatter[i] else ins[i]
        copies.append(pltpu.make_async_copy(src, outs[i].at[me], sems[2].at[i]))
    for d in range(1, N_DEV):
        peer = (1 - x if d & 4 else x, 1 - y if d & 2 else y, 1 - c if d & 1 else c)
        for i in range(len(ins)):
            src = ins[i].at[_flat(peer)] if scatter[i] else ins[i]
            copies.append(_remote(src, outs[i].at[me], sems, i, d - 1, peer))
    return copies


def _two_level_gather(ins, outs, sems):
    x, y, c = lax.axis_index("x"), lax.axis_index("y"), lax.axis_index("c")
    me, sibling = (x, y, c), (x, y, 1 - c)
    chips = [(1 - x, y), (x, 1 - y), (1 - x, 1 - y)]
    n = len(ins)

    def block(i, pos):
        return outs[i].at[_flat(pos)]

    local = [pltpu.make_async_copy(ins[i], block(i, me), sems[2].at[i]) for i in range(n)]
    own = [_remote(ins[i], block(i, me), sems, i, 0, sibling) for i in range(n)]
    own += [_remote(ins[i], block(i, me), sems, i, 1 + j, (*chip, c)) for i in range(n) for j, chip in enumerate(chips)]
    passed = [[_remote(block(i, (*chip, c)), block(i, (*chip, c)), sems, i, 4 + j, sibling) for i in range(n)]
              for j, chip in enumerate(chips)]

    def first():
        for cp in local + own:
            cp.start()

    def relay():
        for j, chip in enumerate(chips):
            for i in range(n):
                _remote(ins[i], block(i, (*chip, c)), sems, i, 1 + j, me).wait_recv()
                passed[j][i].start()

    def finish():
        for i in range(n):
            _remote(ins[i], block(i, sibling), sems, i, 0, me).wait_recv()
            for j, chip in enumerate(chips):
                _remote(ins[i], block(i, (*chip, 1 - c)), sems, i, 4 + j, me).wait_recv()
        for cp in own + [cp for row in passed for cp in row]:
            cp.wait_send()
        for cp in local:
            cp.wait()

    return first, relay, finish


def _exchange(parts, scatter, name):
    n = len(parts)

    def body(*refs):
        copies = _direct_copies(refs[:n], refs[n:2 * n], scatter, refs[2 * n:])
        for cp in copies:
            cp.start()
        for cp in copies:
            cp.wait()

    return pl.pallas_call(
        body, name=name, in_specs=[ANY_SPEC] * n, out_specs=[ANY_SPEC] * n,
        out_shape=_exchange_shapes(parts, scatter), scratch_shapes=_exchange_sems(n),
    )(*parts)


def _tri(kind):
    j = lax.broadcasted_iota(jnp.int32, (BLK, BLK), 0)
    s = lax.broadcasted_iota(jnp.int32, (BLK, BLK), 1)
    m = {"after": j > s, "upto": j <= s, "before": j < s}[kind]
    half = jnp.concatenate([jnp.where(m, 1.0, 0.0), jnp.ones((BLK, BLK), F32)], axis=1).astype(BF16)
    return jnp.concatenate([half, half], axis=0)


def _scan_rows(v, tri):
    hi, lo = _split_bf16(v)
    r = _dot(jnp.concatenate([hi, lo], axis=1), tri)
    return r[:, :BLK], r[:, BLK:]


HEADS_PER_STEP = LANES // HEAD_DIM


def _first_head_lanes():
    return lax.broadcasted_iota(jnp.int32, (1, LANES), 1) < HEAD_DIM


def _pair_mean(v, first):
    m0 = jnp.sum(jnp.where(first, v, 0.0), axis=-1, keepdims=True)
    m1 = jnp.sum(jnp.where(first, 0.0, v), axis=-1, keepdims=True)
    return jnp.where(first, m0, m1) * (1.0 / HEAD_DIM)


def _pair_norm(v, g2, first):
    return v * lax.rsqrt(_pair_mean(v * v, first) + EPS) * g2


def _pair_norm_bwd(raw, g2, dn, first):
    r = lax.rsqrt(_pair_mean(raw * raw, first) + EPS)
    xh = raw * r
    dxh = dn * g2
    return r * (dxh - xh * _pair_mean(dxh * xh, first)), jnp.sum(dn * xh, axis=0, keepdims=True)


def _block_diag(v, first):
    zero = jnp.zeros_like(v)
    return jnp.concatenate([jnp.where(first, v, zero), jnp.where(first, zero, v)], axis=0)


def _attn_prep(q_ref, k_ref, v_ref, qg_ref, kg_ref, qc_s, kc_s, vd_s, kd_s, n_blk):
    scale = HEAD_DIM ** -0.5
    first = _first_head_lanes()

    def prep(i, _):
        rows = pl.ds(pl.multiple_of(i * BLK, BLK), BLK)
        both = pl.ds(pl.multiple_of(i * 2 * BLK, 2 * BLK), 2 * BLK)
        qh, ql = _split_bf16(_pair_norm(q_ref[rows, :], qg_ref[...], first) * scale)
        kh, kl = _split_bf16(_pair_norm(k_ref[rows, :], kg_ref[...], first))
        for h in range(HEADS_PER_STEP):
            sl = slice(h * HEAD_DIM, (h + 1) * HEAD_DIM)
            qc_s[h, rows, :] = jnp.concatenate([qh[:, sl], ql[:, sl], qh[:, sl], ql[:, sl]], axis=1)
            kc_s[h, rows, :] = jnp.concatenate([kh[:, sl], kh[:, sl], kl[:, sl], kl[:, sl]], axis=1)
        vd_s[both, :] = _block_diag(v_ref[rows, :].astype(BF16), first)
        if kd_s is not None:
            kd_s[both, :] = _block_diag(kh, first)
        return 0

    lax.fori_loop(0, n_blk, prep, 0)


def _pair_scores(qc, kc_ref, grp):
    zs = []
    for j in range(0, KEY_GROUP, 2):
        two = pl.ds(pl.multiple_of((grp * KEY_GROUP + j) * BLK, 2 * BLK), 2 * BLK)
        z = _dot_nt(qc, kc_ref[two, :])
        zs += [z[:, :BLK], z[:, BLK:]]
    return zs


def _col_minus_row():
    row = lax.broadcasted_iota(jnp.int32, (BLK, BLK), 0)
    col = lax.broadcasted_iota(jnp.int32, (BLK, BLK), 1)
    return col - row


def _softplus(z):
    return jnp.maximum(z, 0.0) + jnp.log(1.0 + jnp.exp(-jnp.abs(z)))


def _attn_fwd(proj, qg, kg, sends, name):
    s = proj.shape[0]
    n_blk = s // BLK
    pairs = ATTN_WIDTH // LANES
    n_send = len(sends)

    def body(q_ref, k_ref, v_ref, qg_ref, kg_ref, *rest):
        send_refs, (o_ref, t_ref), rest = rest[:n_send], rest[n_send:n_send + 2], rest[n_send + 2:]
        got_refs, (qc_s, kc_s, vd_s), sems = rest[:n_send], rest[n_send:n_send + 3], rest[n_send + 3:]
        start, relay, finish = _two_level_gather(send_refs, got_refs, sems)
        step = pl.program_id(0)
        pl.when(step == 0)(start)
        pl.when(step == pairs - 1)(relay)
        tri = _tri("after")
        diff = _col_minus_row()
        first = _first_head_lanes()
        heads = range(HEADS_PER_STEP)
        _attn_prep(q_ref, k_ref, v_ref, qg_ref, kg_ref, qc_s, kc_s, vd_s, None, n_blk)

        def group(qc, qi, grp, carry, acc, masked):
            blocks = [grp * KEY_GROUP + j for j in reversed(range(KEY_GROUP))]
            zs = [_pair_scores(qc[h], kc_s.at[h], grp)[::-1] for h in heads]
            keeps = [diff < (qi - kb) * BLK if masked else None for kb in blocks]
            parts = [[None] * KEY_GROUP for _ in heads]
            for h in heads:
                for j, z in enumerate(zs[h]):
                    sp = _softplus(z)
                    lom = -sp
                    if masked:
                        lom = jnp.where(keeps[j], lom, 0.0)
                    tail, tot = _scan_rows(lom, tri)
                    parts[h][j] = (z - sp + tail, tot)
            carry = list(carry)
            for j, kb in enumerate(blocks):
                ws = []
                for h in heads:
                    lw, tot = parts[h][j]
                    w = jnp.exp(lw + carry[h])
                    if masked:
                        w = jnp.where(keeps[j], w, 0.0)
                    ws.append(w.astype(BF16))
                    carry[h] = carry[h] + tot
                acc = acc + _dot(jnp.concatenate(ws, axis=1),
                                 vd_s[pl.ds(pl.multiple_of(kb * 2 * BLK, 2 * BLK), 2 * BLK), :])
            return tuple(carry), acc

        def q_block(qi, _):
            rows = pl.ds(pl.multiple_of(qi * BLK, BLK), BLK)
            qc = [qc_s[h, rows, :] for h in heads]
            top = qi // KEY_GROUP
            zero = jnp.zeros((BLK, BLK), F32)
            carry, acc = group(qc, qi, top, (zero,) * HEADS_PER_STEP, jnp.zeros((BLK, LANES), F32), True)
            carry, acc = lax.fori_loop(
                0, top, lambda t, c: group(qc, qi, top - 1 - t, c[0], c[1], False), (carry, acc))
            o_ref[rows, :] = acc.astype(BF16)
            t_ref[rows, :] = jnp.where(first, carry[0], carry[1])
            return 0

        lax.fori_loop(0, n_blk, q_block, 0)
        pl.when(step == pairs - 1)(finish)

    col = lambda off: pl.BlockSpec((s, LANES), lambda p: (0, off + p))
    vec = pl.BlockSpec((1, LANES), lambda p: (0, 0))
    out = pl.pallas_call(
        body, name=name, grid=(pairs,),
        in_specs=[col(0), col(pairs), col(2 * pairs), vec, vec] + [ANY_SPEC] * n_send,
        out_specs=[pl.BlockSpec((s, LANES), lambda p: (0, p))] * 2 + [ANY_SPEC] * n_send,
        out_shape=[jax.ShapeDtypeStruct((s, ATTN_WIDTH), BF16), jax.ShapeDtypeStruct((s, ATTN_WIDTH), F32)]
        + _exchange_shapes(sends, [False] * n_send),
        scratch_shapes=[pltpu.VMEM((HEADS_PER_STEP, s, 4 * HEAD_DIM), BF16)] * 2
        + [pltpu.VMEM((HEADS_PER_STEP * s, LANES), BF16)] + _exchange_sems(n_send),
        compiler_params=_params(1),
    )(proj, proj, proj, qg, kg, *sends)
    return out[0], out[1], out[2:]


def _attn_bwd(proj, dcat, tsum, qg, kg, sends, name):
    s = proj.shape[0]
    n_blk = s // BLK
    pairs = ATTN_WIDTH // LANES
    scale = HEAD_DIM ** -0.5
    n_send = len(sends)
    n_scratch = 7

    def body(q_ref, k_ref, v_ref, do_ref, t_ref, qg_ref, kg_ref, *rest):
        send_refs, rest = rest[:n_send], rest[n_send:]
        (dq_ref, dk_ref, dv_ref, dqg_ref, dkg_ref), rest = rest[:5], rest[5:]
        got_refs, scratch, sems = rest[:n_send], rest[n_send:n_send + n_scratch], rest[n_send + n_scratch:]
        qc_s, kc_s, vd_s, kd_s, qd_s, dob_s, dkv_s = scratch
        copies = _direct_copies(send_refs, got_refs, [True] * n_send, sems)

        @pl.when(pl.program_id(0) == 0)
        def _():
            for cp in copies:
                cp.start()

        tri_p = _tri("upto")
        tri_h = _tri("before")
        diff = _col_minus_row()

        @pl.when(pl.program_id(0) == 0)
        def _():
            dqg_ref[...] = jnp.zeros_like(dqg_ref)
            dkg_ref[...] = jnp.zeros_like(dkg_ref)

        first = _first_head_lanes()
        heads = range(HEADS_PER_STEP)
        _attn_prep(q_ref, k_ref, v_ref, qg_ref, kg_ref, qc_s, kc_s, vd_s, kd_s, n_blk)

        def prep(i, _):
            rows = pl.ds(pl.multiple_of(i * BLK, BLK), BLK)
            both = pl.ds(pl.multiple_of(i * 2 * BLK, 2 * BLK), 2 * BLK)
            dob = do_ref[rows, :].astype(BF16)
            dob_s[rows, :] = dob
            none = jnp.zeros((BLK, HEAD_DIM), BF16)
            for h in heads:
                qd_s[h, both, :] = jnp.concatenate(
                    [jnp.concatenate([qc_s[h, rows, 0:HEAD_DIM], none], axis=1),
                     jnp.concatenate([none, dob[:, h * HEAD_DIM:(h + 1) * HEAD_DIM]], axis=1)], axis=0)
                dkv_s[h, rows, :] = jnp.zeros((BLK, LANES), F32)
            return 0

        lax.fori_loop(0, n_blk, prep, 0)

        def group(qc, qd, dob, tq, qi, grp, pc, hc, dq, masked):
            blocks = [grp * KEY_GROUP + j for j in range(KEY_GROUP)]
            cols_of = [pl.ds(pl.multiple_of(kb * BLK, BLK), BLK) for kb in blocks]
            both_of = [pl.ds(pl.multiple_of(kb * 2 * BLK, 2 * BLK), 2 * BLK) for kb in blocks]
            zs = [_pair_scores(qc[h], kc_s.at[h], grp) for h in heads]
            das =[_dot_nt(dob, vd_s[both, :]) for both in both_of]
            keeps = [diff < (qi - kb) * BLK if masked else None for kb in blocks]
            lbs = [[None] * KEY_GROUP for _ in heads]
            scans = [[None] * KEY_GROUP for _ in heads]
            for h in heads:
                for j, z in enumerate(zs[h]):
                    sp = _softplus(z)
                    lom = -sp
                    if masked:
                        lom = jnp.where(keeps[j], lom, 0.0)
                    lbs[h][j] = z - sp
                    scans[h][j] = _scan_rows(lom, tri_p)
            pc, hc = list(pc), list(hc)
            avs = [[None] * KEY_GROUP for _ in heads]
            gws = [[None] * KEY_GROUP for _ in heads]
            hscans = [[None] * KEY_GROUP for _ in heads]
            for h in heads:
                for j in range(KEY_GROUP):
                    p_in, p_tot = scans[h][j]
                    a = jnp.exp(lbs[h][j] + (tq[h] - pc[h] - p_in))
                    if masked:
                        a = jnp.where(keeps[j], a, 0.0)
                    pc[h] = pc[h] + p_tot
                    gw = das[j][:, h * BLK:(h + 1) * BLK] * a
                    avs[h][j] = a.astype(BF16)
                    gws[h][j] = gw
                    hscans[h][j] = _scan_rows(gw, tri_h)
            dzs = [[None] * KEY_GROUP for _ in heads]
            for h in heads:
                for j in range(KEY_GROUP):
                    h_in, g_tot = hscans[h][j]
                    gw = gws[h][j]
                    dz = gw - jnp.exp(lbs[h][j]) * (gw + hc[h] + h_in)
                    if masked:
                        dz = jnp.where(keeps[j], dz, 0.0)
                    hc[h] = hc[h] + g_tot
                    dzs[h][j] = dz.astype(BF16)
            for j, both in enumerate(both_of):
                dq = dq + _dot(jnp.concatenate([dzs[h][j] for h in heads], axis=1), kd_s[both, :])
            for h in heads:
                for j, cols in enumerate(cols_of):
                    dkv_s[h, cols, :] += _dot_tn(jnp.concatenate([dzs[h][j], avs[h][j]], axis=0), qd[h])
            return tuple(pc), tuple(hc), dq

        def q_block(qi, dqg):
            rows = pl.ds(pl.multiple_of(qi * BLK, BLK), BLK)
            both = pl.ds(pl.multiple_of(qi * 2 * BLK, 2 * BLK), 2 * BLK)
            qc = [qc_s[h, rows, :] for h in heads]
            qd = [qd_s[h, both, :] for h in heads]
            dob = dob_s[rows, :]
            tboth = t_ref[rows, :]
            tq = [jnp.concatenate([tboth[:, h * HEAD_DIM:(h + 1) * HEAD_DIM]] * 2, axis=1) for h in heads]
            zero = (jnp.zeros((BLK, BLK), F32),) * HEADS_PER_STEP
            top = qi // KEY_GROUP
            pc, hc, dq = lax.fori_loop(
                0, top, lambda grp, c: group(qc, qd, dob, tq, qi, grp, c[0], c[1], c[2], False),
                (zero, zero, jnp.zeros((BLK, LANES), F32)))
            _, _, dq = group(qc, qd, dob, tq, qi, top, pc, hc, dq, True)
            dq_raw, dg = _pair_norm_bwd(q_ref[rows, :], qg_ref[...], dq * scale, first)
            dq_ref[rows, :] = dq_raw.astype(BF16)
            return dqg + dg

        dqg = lax.fori_loop(0, n_blk, q_block, jnp.zeros((1, LANES), F32))

        def finish(i, dkg):
            rows = pl.ds(pl.multiple_of(i * BLK, BLK), BLK)
            dk = jnp.concatenate([dkv_s[h, rows, 0:HEAD_DIM] for h in heads], axis=1)
            dv = jnp.concatenate([dkv_s[h, rows, HEAD_DIM:2 * HEAD_DIM] for h in heads], axis=1)
            dk_raw, dg = _pair_norm_bwd(k_ref[rows, :], kg_ref[...], dk, first)
            dk_ref[rows, :] = dk_raw.astype(BF16)
            dv_ref[rows, :] = dv.astype(BF16)
            return dkg + dg

        dkg = lax.fori_loop(0, n_blk, finish, jnp.zeros((1, LANES), F32))
        dqg_ref[0:1, :] += dqg
        dkg_ref[0:1, :] += dkg

        @pl.when(pl.program_id(0) == pairs - 1)
        def _():
            for cp in copies:
                cp.wait()

    col = lambda off: pl.BlockSpec((s, LANES), lambda p: (0, off + p))
    vec = pl.BlockSpec((1, LANES), lambda p: (0, 0))
    small = pl.BlockSpec((8, LANES), lambda p: (0, 0))
    out = pl.pallas_call(
        body, name=name, grid=(pairs,),
        in_specs=[col(0), col(pairs), col(2 * pairs), col(0), col(0), vec, vec] + [ANY_SPEC] * n_send,
        out_specs=[col(0)] * 3 + [small] * 2 + [ANY_SPEC] * n_send,
        out_shape=[jax.ShapeDtypeStruct((s, ATTN_WIDTH), BF16)] * 3 + [jax.ShapeDtypeStruct((8, LANES), F32)] * 2
        + _exchange_shapes(sends, [True] * n_send),
        scratch_shapes=[pltpu.VMEM((HEADS_PER_STEP, s, 4 * HEAD_DIM), BF16)] * 2
        + [pltpu.VMEM((HEADS_PER_STEP * s, LANES), BF16)] * 2
        + [pltpu.VMEM((HEADS_PER_STEP, HEADS_PER_STEP * s, LANES), BF16), pltpu.VMEM((s, LANES), BF16),
           pltpu.VMEM((HEADS_PER_STEP, s, LANES), F32)] + _exchange_sems(n_send),
        compiler_params=_params(1),
    )(proj, proj, proj, dcat, tsum, qg, kg, *sends)
    return out[:5], out[5:]


CONV_ROWS = 128


def _shifted(window, shift, halo):
    if shift == 0:
        return window[halo:, :]
    return pltpu.roll(window, shift, 0)[halo:, :]


def _conv_taps(u_s, r0, w_ref, rows):
    window = u_s[pl.ds(r0, CONV_HALO + rows), :]
    y = None
    for k in range(CONV_KERNEL):
        term = _shifted(window, CONV_KERNEL - 1 - k, CONV_HALO) * w_ref[k:k + 1, :]
        y = term if y is None else y + term
    return y


def _conv_fwd(proj, w, b, lg, lb, name):
    s = proj.shape[0]
    cw = w.shape[1]
    rows = CONV_ROWS
    blk_a = (proj.shape[1] - 2 * cw) // cw

    def body(a_ref, g_ref, w_ref, b_ref, lg_ref, lb_ref, o_ref, u_s):
        u_s[0:CONV_HALO, :] = jnp.zeros((CONV_HALO, cw), F32)

        def glu(i, _):
            r0 = pl.multiple_of(i * rows, rows)
            u_s[pl.ds(CONV_HALO + r0, rows), :] = a_ref[pl.ds(r0, rows), :] * _sigmoid(g_ref[pl.ds(r0, rows), :])
            return 0

        lax.fori_loop(0, s // rows, glu, 0)

        def chunk(i, _):
            r0 = pl.multiple_of(i * rows, rows)
            y = _conv_taps(u_s, r0, w_ref, rows) + b_ref[...]
            yc = y - jnp.mean(y, axis=-1, keepdims=True)
            n = yc * lax.rsqrt(jnp.mean(yc * yc, axis=-1, keepdims=True) + EPS)
            ln = n * lg_ref[...] + lb_ref[...]
            o_ref[pl.ds(r0, rows), :] = (ln * _sigmoid(ln)).astype(BF16)
            return 0

        lax.fori_loop(0, s // rows, chunk, 0)

    vec = pl.BlockSpec((1, cw), lambda i: (0, 0))
    return pl.pallas_call(
        body, name=name, grid=(1,),
        in_specs=[pl.BlockSpec((s, cw), lambda i: (0, blk_a)), pl.BlockSpec((s, cw), lambda i: (0, blk_a + 1)),
                  pl.BlockSpec((CONV_KERNEL, cw), lambda i: (0, 0)), vec, vec, vec],
        out_specs=pl.BlockSpec((s, cw), lambda i: (0, 0)),
        out_shape=jax.ShapeDtypeStruct((s, cw), BF16),
        scratch_shapes=[pltpu.VMEM((CONV_HALO + s, cw), F32)],
        compiler_params=_params(1),
    )(proj, proj, w, b, lg, lb)


def _conv_bwd(proj, dcat, w, b, lg, lb, name):
    s = proj.shape[0]
    cw = w.shape[1]
    rows = CONV_ROWS
    blk_a = (proj.shape[1] - 2 * cw) // cw
    n_chunk = s // rows

    def body(a_ref, g_ref, dc_ref, w_ref, b_ref, lg_ref, lb_ref, o_ref, dw_ref, db_ref, dlg_ref, dlb_ref,
             u_s, dy_s, dw_s):
        u_s[0:CONV_HALO, :] = jnp.zeros((CONV_HALO, cw), F32)
        dy_s[pl.ds(s, CONV_HALO), :] = jnp.zeros((CONV_HALO, cw), F32)
        dw_s[...] = jnp.zeros_like(dw_s)

        def glu(i, _):
            r0 = pl.multiple_of(i * rows, rows)
            u_s[pl.ds(CONV_HALO + r0, rows), :] = a_ref[pl.ds(r0, rows), :] * _sigmoid(g_ref[pl.ds(r0, rows), :])
            return 0

        lax.fori_loop(0, n_chunk, glu, 0)

        def chunk(i, carry):
            db, dlg, dlb = carry
            r0 = pl.multiple_of(i * rows, rows)
            window = u_s[pl.ds(r0, CONV_HALO + rows), :]
            y = None
            for k in range(CONV_KERNEL):
                term = _shifted(window, CONV_KERNEL - 1 - k, CONV_HALO) * w_ref[k:k + 1, :]
                y = term if y is None else y + term
            y = y + b_ref[...]
            yc = y - jnp.mean(y, axis=-1, keepdims=True)
            r = lax.rsqrt(jnp.mean(yc * yc, axis=-1, keepdims=True) + EPS)
            n = yc * r
            ln = n * lg_ref[...] + lb_ref[...]
            sg = _sigmoid(ln)
            dln = dc_ref[pl.ds(r0, rows), :] * (sg * (1.0 + ln * (1.0 - sg)))
            dn = dln * lg_ref[...]
            dy = r * (dn - jnp.mean(dn, axis=-1, keepdims=True) - n * jnp.mean(dn * n, axis=-1, keepdims=True))
            dy_s[pl.ds(r0, rows), :] = dy
            for k in range(CONV_KERNEL):
                prod = _shifted(window, CONV_KERNEL - 1 - k, CONV_HALO) * dy
                dw_s[k] += jnp.sum(prod.reshape(rows // 8, 8, cw), axis=0)
            return (db + jnp.sum(dy, axis=0, keepdims=True),
                    dlg + jnp.sum(dln * n, axis=0, keepdims=True),
                    dlb + jnp.sum(dln, axis=0, keepdims=True))

        zero = jnp.zeros((1, cw), F32)
        db, dlg, dlb = lax.fori_loop(0, n_chunk, chunk, (zero, zero, zero))
        db_ref[...] = db
        dlg_ref[...] = dlg
        dlb_ref[...] = dlb
        for k in range(CONV_KERNEL):
            dw_ref[k:k + 1, :] = jnp.sum(dw_s[k], axis=0, keepdims=True)

        def back(i, _):
            r0 = pl.multiple_of(i * rows, rows)
            window = dy_s[pl.ds(r0, rows + CONV_HALO), :]
            du = None
            for k in range(CONV_KERNEL):
                sh = CONV_KERNEL - 1 - k
                shifted = window[:rows, :] if sh == 0 else pltpu.roll(window, rows + CONV_HALO - sh, 0)[:rows, :]
                term = shifted * w_ref[k:k + 1, :]
                du = term if du is None else du + term
            av = a_ref[pl.ds(r0, rows), :]
            sg = _sigmoid(g_ref[pl.ds(r0, rows), :])
            o_ref[pl.ds(r0, rows), 0:cw] = (du * sg).astype(BF16)
            o_ref[pl.ds(r0, rows), cw:2 * cw] = (du * av * sg * (1.0 - sg)).astype(BF16)
            return 0

        lax.fori_loop(0, n_chunk, back, 0)

    vec = pl.BlockSpec((1, cw), lambda i: (0, 0))
    wspec = pl.BlockSpec((CONV_KERNEL, cw), lambda i: (0, 0))
    return pl.pallas_call(
        body, name=name, grid=(1,),
        in_specs=[pl.BlockSpec((s, cw), lambda i: (0, blk_a)), pl.BlockSpec((s, cw), lambda i: (0, blk_a + 1)),
                  pl.BlockSpec((s, cw), lambda i: (0, 1)), wspec, vec, vec, vec],
        out_specs=[pl.BlockSpec((s, 2 * cw), lambda i: (0, 0)), wspec, vec, vec, vec],
        out_shape=[jax.ShapeDtypeStruct((s, 2 * cw), BF16), jax.ShapeDtypeStruct((CONV_KERNEL, cw), F32)]
        + [jax.ShapeDtypeStruct((1, cw), F32)] * 3,
        scratch_shapes=[pltpu.VMEM((CONV_HALO + s, cw), F32), pltpu.VMEM((s + CONV_HALO, cw), F32),
                        pltpu.VMEM((CONV_KERNEL, 8, cw), F32)],
        compiler_params=_params(1),
    )(proj, proj, dcat, w, b, lg, lb)


FFN_ROWS = 256


def _ffn_gate(g_ref, r0, rows, w_ref, b_ref):
    cur = g_ref[pl.ds(r0, rows), :].astype(F32)
    prev = g_ref[pl.ds(pl.multiple_of(jnp.maximum(r0 - FFN_HALO, 0), FFN_HALO), FFN_HALO), :].astype(F32)
    prev = jnp.where(r0 > 0, prev, 0.0)
    window = jnp.concatenate([prev, cur], axis=0)
    gc = cur * w_ref[FFN_KERNEL - 1:FFN_KERNEL, :] + b_ref[...]
    for k in range(FFN_KERNEL - 1):
        gc = gc + _shifted(window, FFN_KERNEL - 1 - k, FFN_HALO) * w_ref[k:k + 1, :]
    return gc, window


def _ffn_fwd(up, w, b, name):
    s = up.shape[0]
    f = w.shape[1]
    tc = _pick(f, (256, 128))
    nc = f // tc
    rows = _pick(s, (FFN_ROWS, 128))

    def body(g_ref, v_ref, w_ref, b_ref, o_ref):
        def chunk(i, _):
            r0 = pl.multiple_of(i * rows, rows)
            gc, _w = _ffn_gate(g_ref, r0, rows, w_ref, b_ref)
            o_ref[pl.ds(r0, rows), :] = (gc * _sigmoid(gc) * v_ref[pl.ds(r0, rows), :].astype(F32)).astype(BF16)
            return 0

        lax.fori_loop(0, s // rows, chunk, 0)

    return pl.pallas_call(
        body, name=name, grid=(nc,),
        in_specs=[pl.BlockSpec((s, tc), lambda j: (0, j)), pl.BlockSpec((s, tc), lambda j: (0, nc + j)),
                  pl.BlockSpec((FFN_KERNEL, tc), lambda j: (0, j)), pl.BlockSpec((1, tc), lambda j: (0, j))],
        out_specs=pl.BlockSpec((s, tc), lambda j: (0, j)),
        out_shape=jax.ShapeDtypeStruct((s, f), BF16),
        compiler_params=_params(1),
    )(up, up, w, b)


def _ffn_bwd(up, dact, w, b, name):
    s = up.shape[0]
    f = w.shape[1]
    tc = _pick(f, (256, 128))
    nc = f // tc
    rows = _pick(s, (FFN_ROWS, 128))
    n_chunk = s // rows

    def body(g_ref, v_ref, da_ref, w_ref, b_ref, d_ref, dw_ref, db_ref, dgc_s):
        dg_ref, dv_ref = d_ref.at[0], d_ref.at[1]
        dgc_s[pl.ds(s, FFN_HALO), :] = jnp.zeros((FFN_HALO, tc), F32)

        def chunk(i, carry):
            r0 = pl.multiple_of(i * rows, rows)
            gc, window = _ffn_gate(g_ref, r0, rows, w_ref, b_ref)
            sg = _sigmoid(gc)
            da = da_ref[pl.ds(r0, rows), :].astype(F32)
            dv_ref[pl.ds(r0, rows), :] = (da * gc * sg).astype(BF16)
            dgc = da * v_ref[pl.ds(r0, rows), :].astype(F32) * (sg * (1.0 + gc * (1.0 - sg)))
            dgc_s[pl.ds(r0, rows), :] = dgc
            out = [carry[0] + jnp.sum(dgc, axis=0, keepdims=True)]
            for k in range(FFN_KERNEL):
                out.append(carry[1 + k] + jnp.sum(_shifted(window, FFN_KERNEL - 1 - k, FFN_HALO) * dgc,
                                                  axis=0, keepdims=True))
            return tuple(out)

        zero = jnp.zeros((1, tc), F32)
        sums = lax.fori_loop(0, n_chunk, chunk, (zero,) * (1 + FFN_KERNEL))
        db_ref[...] = sums[0]
        for k in range(FFN_KERNEL):
            dw_ref[k:k + 1, :] = sums[1 + k]

        def back(i, _):
            r0 = pl.multiple_of(i * rows, rows)
            window = dgc_s[pl.ds(r0, rows + FFN_HALO), :]
            dg = window[:rows, :] * w_ref[FFN_KERNEL - 1:FFN_KERNEL, :]
            for k in range(FFN_KERNEL - 1):
                sh = FFN_KERNEL - 1 - k
                dg = dg + pltpu.roll(window, rows + FFN_HALO - sh, 0)[:rows, :] * w_ref[k:k + 1, :]
            dg_ref[pl.ds(r0, rows), :] = dg.astype(BF16)
            return 0

        lax.fori_loop(0, n_chunk, back, 0)

    blk = lambda off: pl.BlockSpec((s, tc), lambda j: (0, off + j))
    return pl.pallas_call(
        body, name=name, grid=(nc,),
        in_specs=[blk(0), blk(nc), blk(0), pl.BlockSpec((FFN_KERNEL, tc), lambda j: (0, j)),
                  pl.BlockSpec((1, tc), lambda j: (0, j))],
        out_specs=[pl.BlockSpec((2, s, tc), lambda j: (0, 0, j)), pl.BlockSpec((FFN_KERNEL, tc), lambda j: (0, j)),
                   pl.BlockSpec((1, tc), lambda j: (0, j))],
        out_shape=[jax.ShapeDtypeStruct((2, s, f), BF16),
                   jax.ShapeDtypeStruct((FFN_KERNEL, f), F32), jax.ShapeDtypeStruct((1, f), F32)],
        scratch_shapes=[pltpu.VMEM((s + FFN_HALO, tc), F32)],
        compiler_params=_params(1),
    )(up, up, dact, w, b)


def _loss_head(y, target, name):
    m, n = y.shape
    tm = _pick(m, (256, 128))

    def body(y_ref, t_ref, l_ref, d_ref, db_ref):
        e = y_ref[...] - t_ref[...]
        part = 0.5 * jnp.sum(jnp.sum(e * e, axis=-1, keepdims=True) / n, axis=0, keepdims=True)

        @pl.when(pl.program_id(0) == 0)
        def _():
            l_ref[...] = jnp.zeros_like(l_ref)

        l_ref[...] += part
        d = e / n
        d_ref[...] = d
        db_ref[...] = d.astype(BF16)

    return pl.pallas_call(
        body, name=name, grid=(m // tm,),
        in_specs=[pl.BlockSpec((tm, n), lambda i: (i, 0))] * 2,
        out_specs=[pl.BlockSpec((8, LANES), lambda i: (0, 0)), pl.BlockSpec((tm, n), lambda i: (i, 0)),
                   pl.BlockSpec((tm, n), lambda i: (i, 0))],
        out_shape=[jax.ShapeDtypeStruct((8, LANES), F32), jax.ShapeDtypeStruct((m, n), F32),
                   jax.ShapeDtypeStruct((m, n), BF16)],
        compiler_params=_params(1),
    )(y, target)


def _adamw_math(w, g, m, v):
    m = ADAM_B1 * m + (1.0 - ADAM_B1) * g
    v = ADAM_B2 * v + (1.0 - ADAM_B2) * (g * g)
    m_hat = m / (1.0 - ADAM_B1 ** ADAM_STEP)
    v_hat = v / (1.0 - ADAM_B2 ** ADAM_STEP)
    delta = -ADAM_LR * (m_hat / (jnp.sqrt(v_hat) + ADAM_EPS) + ADAM_WD * w)
    return delta, m, v


def _sum_adamw(parts, w, m, v, name):
    depth, r, c = w.shape
    tr = max(t for t in range(16, min(r, 192) + 1, 16) if r % t == 0)
    steps = r // tr

    def body(*refs):
        p_refs = refs[:depth]
        w_ref, m_ref, v_ref, g_out, d_out, m_out, v_out = refs[depth:]
        for layer in range(depth):
            @pl.when(pl.program_id(0) == layer)
            def _():
                g = p_refs[layer][0].astype(F32)
                for src in range(1, N_DEV):
                    g = g + p_refs[layer][src].astype(F32)
                d, mn, vn = _adamw_math(w_ref[0], g, m_ref[0], v_ref[0])
                g_out[0] = g
                d_out[0] = d
                m_out[0] = mn
                v_out[0] = vn

    def part_spec(layer):
        return pl.BlockSpec((N_DEV, tr, c), lambda l, i: (0, jnp.clip((l - layer) * steps + i, 0, steps - 1), 0))

    blk = pl.BlockSpec((1, tr, c), lambda l, i: (l, i, 0))
    return pl.pallas_call(
        body, name=name, grid=(depth, steps),
        in_specs=[part_spec(layer) for layer in range(depth)] + [blk, blk, blk],
        out_specs=[blk] * 4,
        out_shape=[jax.ShapeDtypeStruct(w.shape, F32)] * 4,
        compiler_params=_params(2),
    )(*parts, w, m, v)


def _sum_rows(parts, name):
    _, r, c = parts.shape

    def body(p_ref, o_ref):
        g = p_ref[0]
        for src in range(1, N_DEV):
            g = g + p_ref[src]
        o_ref[...] = g

    return pl.pallas_call(
        body, name=name, grid=(1,),
        in_specs=[pl.BlockSpec((N_DEV, r, c), lambda i: (0, 0, 0))],
        out_specs=pl.BlockSpec((r, c), lambda i: (0, 0)),
        out_shape=jax.ShapeDtypeStruct((r, c), F32),
        compiler_params=_params(1),
    )(parts)


def _adamw_rows(w, g, m, v, name):
    r, c = w.shape

    def body(w_ref, g_ref, m_ref, v_ref, d_out, m_out, v_out):
        d, mn, vn = _adamw_math(w_ref[...], g_ref[...], m_ref[...], v_ref[...])
        d_out[...] = d
        m_out[...] = mn
        v_out[...] = vn

    blk = pl.BlockSpec((r, c), lambda i: (0, 0))
    return pl.pallas_call(
        body, name=name, grid=(1,), in_specs=[blk] * 4, out_specs=[blk] * 3,
        out_shape=[jax.ShapeDtypeStruct((r, c), F32)] * 3,
        compiler_params=_params(1),
    )(w, g, m, v)


PACK_TILE = 8 * LANES


def _pack(arrays):
    rows = []
    for a in arrays:
        flat = a.reshape(-1).astype(F32)
        pad = (-flat.shape[0]) % PACK_TILE
        rows.append(jnp.pad(flat, (0, pad)).reshape(-1, LANES))
    return jnp.concatenate(rows, axis=0)


def _unpack(packed, shapes, lead=()):
    out, r0 = [], 0
    for shp in shapes:
        size = 1
        for d in shp:
            size *= d
        nrows = -(-size // PACK_TILE) * 8
        piece = packed[..., r0:r0 + nrows, :].reshape(lead + (nrows * LANES,))[..., :size]
        out.append(piece.reshape(lead + tuple(shp)))
        r0 += nrows
    return out


def kernel(x, norm1_g, w_in, q_norm_g, k_norm_g, conv_dw_w, conv_dw_b, conv_ln_g, conv_ln_b, w_out, norm2_g, w_up, ffn_dw_w, ffn_dw_b, w_down, loss_target, m_norm1_g, m_w_in, m_q_norm_g, m_k_norm_g, m_conv_dw_w, m_conv_dw_b, m_conv_ln_g, m_conv_ln_b, m_w_out, m_norm2_g, m_w_up, m_ffn_dw_w, m_ffn_dw_b, m_w_down, v_norm1_g, v_w_in, v_q_norm_g, v_k_norm_g, v_conv_dw_w, v_conv_dw_b, v_conv_ln_g, v_conv_ln_b, v_w_out, v_norm2_g, v_w_up, v_ffn_dw_w, v_ffn_dw_b, v_w_down):
    depth, d_model, in_shard = w_in.shape
    out_shard = w_out.shape[1]
    up_shard = w_up.shape[2]
    down_shard = w_down.shape[1]
    d_ff = down_shard * N_DEV
    conv_w = conv_dw_b.shape[1]
    cw_shard = conv_dw_w.shape[2]
    fw_shard = ffn_dw_w.shape[2]
    me = 4 * lax.axis_index("x") + 2 * lax.axis_index("y") + lax.axis_index("c")

    transposed = lambda a: a.transpose(0, 2, 1)
    b_in, b_out, b_up, b_down = (transposed(w_in).astype(BF16), w_out.astype(BF16), transposed(w_up).astype(BF16),
                                 w_down.astype(BF16))
    rows_major = lambda g: g.reshape(N_DEV * g.shape[1], g.shape[2])
    g_in0, g_small = _exchange([b_in[0], _pack([conv_dw_w, ffn_dw_w])], [False, False], name="gather_first")
    wf_in, wf_out, wf_up, wf_down = [rows_major(g_in0)] + [None] * (depth - 1), [None] * depth, [None] * depth, [None] * depth
    g_cw, g_fw = _unpack(g_small, [conv_dw_w.shape, ffn_dw_w.shape], lead=(N_DEV,))
    cwf = g_cw.transpose(1, 2, 0, 3).reshape(depth, CONV_KERNEL, conv_w)
    fwf = g_fw.transpose(1, 2, 0, 3).reshape(depth, FFN_KERNEL, d_ff)

    row = lambda a, l: a[l].reshape(1, -1)
    both_heads = lambda a, l: jnp.tile(row(a, l), (1, HEADS_PER_STEP))

    xs = x[0]
    saved = []
    for l in range(depth):
        proj, h1, _ = _mm_rms(xs, row(norm1_g, l), wf_in[l], F32, [], name="fwd_in")
        sends = [b_out[l], b_up[l]] + ([b_in[l + 1]] if l + 1 < depth else [])
        attn, tsum, got = _attn_fwd(proj, both_heads(q_norm_g, l), both_heads(k_norm_g, l), sends, name="fwd_attn")
        wf_out[l], wf_up[l] = rows_major(got[0]), rows_major(got[1])
        if l + 1 < depth:
            wf_in[l + 1] = rows_major(got[2])
        conv = _conv_fwd(proj, cwf[l], row(conv_dw_b, l), row(conv_ln_g, l), row(conv_ln_b, l), name="fwd_conv")
        cat = jnp.concatenate([attn, conv], axis=1)
        x_mid = _mm_res(cat, wf_out[l], xs, name="fwd_out")
        up, h2, got = _mm_rms(x_mid, row(norm2_g, l), wf_up[l], BF16, [b_down[l]], name="fwd_up")
        wf_down[l] = rows_major(got[0])
        act = _ffn_fwd(up, fwf[l], row(ffn_dw_b, l), name="fwd_ffn")
        x_next = _mm_res(act, wf_down[l], x_mid, name="fwd_down")
        saved.append((xs, h1, proj, tsum, cat, x_mid, h2, up, act))
        xs = x_next

    loss_tile, dx, dxb = _loss_head(xs, loss_target[0], name="loss_head")
    loss = lax.psum(loss_tile[0, 0], ("x", "y", "c"))

    r_in, r_out, r_up, r_down = [None] * depth, [None] * depth, [None] * depth, [None] * depth
    row_blocks = lambda g: g.reshape(N_DEV, g.shape[0] // N_DEV, g.shape[1])
    small = {k: [None] * depth for k in ("norm1_g", "q_norm_g", "k_norm_g", "conv_dw_w", "conv_dw_b", "conv_ln_g",
                                         "conv_ln_b", "norm2_g", "ffn_dw_w", "ffn_dw_b")}
    gw_in = None
    for l in reversed(range(depth)):
        xs, h1, proj, tsum, cat, x_mid, h2, up, act = saved[l]
        dact = _mm_nt(dxb, wf_down[l], BF16, name="bwd_dact")
        gw_down = _mm_tn(act, dxb, name="bwd_gw_down")
        dup, small["ffn_dw_w"][l], small["ffn_dw_b"][l] = _ffn_bwd(up, dact, fwf[l], row(ffn_dw_b, l), name="bwd_ffn")
        gw_up = _mm_tn(dup, h2, name="bwd_gw_up")
        dx, dxb, small["norm2_g"][l], got = _mm_rmsbwd(dup, wf_up[l], x_mid, row(norm2_g, l), dx,
                                                       [row_blocks(gw_down)], name="bwd_up")
        r_down[l] = got[0]
        dcat = _mm_nt(dxb, wf_out[l], F32, name="bwd_dcat")
        gw_out = _mm_tn(cat, dxb, name="bwd_gw_out")
        dglu, small["conv_dw_w"][l], small["conv_dw_b"][l], small["conv_ln_g"][l], small["conv_ln_b"][l] = _conv_bwd(
            proj, dcat, cwf[l], row(conv_dw_b, l), row(conv_ln_g, l), row(conv_ln_b, l), name="bwd_conv")
        sends = [row_blocks(gw_up), row_blocks(gw_out)] + ([row_blocks(gw_in)] if l + 1 < depth else [])
        (dq, dk, dv, dqg, dkg), got = _attn_bwd(proj, dcat, tsum, both_heads(q_norm_g, l), both_heads(k_norm_g, l),
                                                sends, name="bwd_attn")
        r_up[l], r_out[l] = got[:2]
        if l + 1 < depth:
            r_in[l + 1] = got[2]
        small["q_norm_g"][l] = dqg[0:1, :HEAD_DIM] + dqg[0:1, HEAD_DIM:]
        small["k_norm_g"][l] = dkg[0:1, :HEAD_DIM] + dkg[0:1, HEAD_DIM:]
        dproj = jnp.concatenate([dq, dk, dv, dglu], axis=1)
        gw_in = _mm_tn(dproj, h1, name="bwd_gw_in")
        dx, dxb, small["norm1_g"][l], _ = _mm_rmsbwd(dproj, wf_in[l], xs, row(norm1_g, l), dx, [], name="bwd_in")
    grad_x = dx[None]

    names = ["norm1_g", "q_norm_g", "k_norm_g", "conv_dw_w", "conv_dw_b", "conv_ln_g", "conv_ln_b", "norm2_g",
             "ffn_dw_w", "ffn_dw_b"]
    full_shapes = {"norm1_g": norm1_g.shape, "q_norm_g": q_norm_g.shape, "k_norm_g": k_norm_g.shape,
                   "conv_dw_w": (depth, CONV_KERNEL, conv_w), "conv_dw_b": conv_dw_b.shape,
                   "conv_ln_g": conv_ln_g.shape, "conv_ln_b": conv_ln_b.shape, "norm2_g": norm2_g.shape,
                   "ffn_dw_w": (depth, FFN_KERNEL, d_ff), "ffn_dw_b": ffn_dw_b.shape}
    partial = _pack([jnp.stack(small[k]).reshape(full_shapes[k]) for k in names])
    r_in[0], all_partials = _exchange([row_blocks(gw_in), partial], [True, False], name="exchange_last")
    big = {
        "w_in": [transposed(a) for a in _sum_adamw(r_in, transposed(w_in), transposed(m_w_in), transposed(v_w_in),
                                                   name="adamw_in")],
        "w_out": _sum_adamw(r_out, w_out, m_w_out, v_w_out, name="adamw_out"),
        "w_up": [transposed(a) for a in _sum_adamw(r_up, transposed(w_up), transposed(m_w_up), transposed(v_w_up),
                                                   name="adamw_up")],
        "w_down": _sum_adamw(r_down, w_down, m_w_down, v_w_down, name="adamw_down"),
    }

    total = _unpack(_sum_rows(all_partials, name="sum_small_grads"), [full_shapes[k] for k in names])
    grads = dict(zip(names, total))
    grads["conv_dw_w"] = lax.dynamic_slice_in_dim(grads["conv_dw_w"], me * cw_shard, cw_shard, axis=2)
    grads["ffn_dw_w"] = lax.dynamic_slice_in_dim(grads["ffn_dw_w"], me * fw_shard, fw_shard, axis=2)
    weights = dict(norm1_g=norm1_g, q_norm_g=q_norm_g, k_norm_g=k_norm_g, conv_dw_w=conv_dw_w, conv_dw_b=conv_dw_b,
                   conv_ln_g=conv_ln_g, conv_ln_b=conv_ln_b, norm2_g=norm2_g, ffn_dw_w=ffn_dw_w, ffn_dw_b=ffn_dw_b)
    m_in = dict(norm1_g=m_norm1_g, q_norm_g=m_q_norm_g, k_norm_g=m_k_norm_g, conv_dw_w=m_conv_dw_w,
                conv_dw_b=m_conv_dw_b, conv_ln_g=m_conv_ln_g, conv_ln_b=m_conv_ln_b, norm2_g=m_norm2_g,
                ffn_dw_w=m_ffn_dw_w, ffn_dw_b=m_ffn_dw_b)
    v_in = dict(norm1_g=v_norm1_g, q_norm_g=v_q_norm_g, k_norm_g=v_k_norm_g, conv_dw_w=v_conv_dw_w,
                conv_dw_b=v_conv_dw_b, conv_ln_g=v_conv_ln_g, conv_ln_b=v_conv_ln_b, norm2_g=v_norm2_g,
                ffn_dw_w=v_ffn_dw_w, ffn_dw_b=v_ffn_dw_b)
    shard_shapes = [weights[k].shape for k in names]
    d_s, m_s, v_s = _adamw_rows(_pack([weights[k] for k in names]), _pack([grads[k] for k in names]),
                                _pack([m_in[k] for k in names]), _pack([v_in[k] for k in names]), name="adamw_small")
    delta = dict(zip(names, _unpack(d_s, shard_shapes)))
    new_m = dict(zip(names, _unpack(m_s, shard_shapes)))
    new_v = dict(zip(names, _unpack(v_s, shard_shapes)))
    for k, (g, d, mn, vn) in big.items():
        grads[k], delta[k], new_m[k], new_v[k] = g, d, mn, vn

    order = ["norm1_g", "w_in", "q_norm_g", "k_norm_g", "conv_dw_w", "conv_dw_b", "conv_ln_g", "conv_ln_b", "w_out",
             "norm2_g", "w_up", "ffn_dw_w", "ffn_dw_b", "w_down"]
    return (loss, grad_x, *[grads[k] for k in order], *[delta[k] for k in order], *[new_m[k] for k in order],
            *[new_v[k] for k in order])
```

```python
import functools

import jax
import jax.numpy as jnp
from jax import lax
from jax.experimental import pallas as pl
from jax.experimental.pallas import tpu as pltpu

F32 = jnp.float32
BF16 = jnp.bfloat16

N_DEV = 8
HEADS = 8
HEAD_DIM = 64
ATTN_WIDTH = HEADS * HEAD_DIM
CONV_KERNEL = 31
FFN_KERNEL = 3
EPS = 1e-6
BLK = 128
KEY_GROUP = 4
LANES = 128
NORM_ROWS = 128
CONV_HALO = 32
FFN_HALO = 16

ADAM_LR = 0.001
ADAM_B1 = 0.9
ADAM_B2 = 0.999
ADAM_EPS = 1e-08
ADAM_WD = 0.01
ADAM_STEP = 10

VMEM_LIMIT = 56 * 1024 * 1024


def _params(n_axes=0):
    kw = dict(vmem_limit_bytes=VMEM_LIMIT)
    if n_axes:
        kw["dimension_semantics"] = ("arbitrary",) * n_axes
    return pltpu.CompilerParams(**kw)


def _dot(a, b):
    return jnp.dot(a, b, preferred_element_type=F32)


def _dot_nt(a, b):
    return lax.dot_general(a, b, (((1,), (1,)), ((), ())), preferred_element_type=F32)


def _dot_tn(a, b):
    return lax.dot_general(a, b, (((0,), (0,)), ((), ())), preferred_element_type=F32)


def _sigmoid(x):
    return 1.0 / (1.0 + jnp.exp(-x))


def _split_bf16(x):
    hi = x.astype(BF16)
    lo = (x - hi.astype(F32)).astype(BF16)
    return hi, lo


def _pick(n, options):
    for t in options:
        if n % t == 0:
            return t
    return n


def _tile(n, cap):
    best = None
    for t in range(LANES, min(n, cap) + 1, LANES):
        if n % t == 0:
            best = t
    return best or n


def _mm_rms(x, g, wt, out_dtype, name):
    m, k = x.shape
    n = wt.shape[0]
    tm = _tile(m, 2048)
    tn = _tile(n, 512)

    def body(x_ref, g_ref, w_ref, o_ref, h_ref, h_s):
        @pl.when(pl.program_id(1) == 0)
        def _():
            def chunk(c, _):
                rows = pl.ds(pl.multiple_of(c * NORM_ROWS, NORM_ROWS), NORM_ROWS)
                xv = x_ref[rows, :]
                r = lax.rsqrt(jnp.mean(xv * xv, axis=-1, keepdims=True) + EPS)
                hv = (xv * r * g_ref[...]).astype(BF16)
                h_s[rows, :] = hv
                h_ref[rows, :] = hv
                return 0

            lax.fori_loop(0, tm // NORM_ROWS, chunk, 0)

        o_ref[...] = _dot_nt(h_s[...], w_ref[...]).astype(out_dtype)

    return pl.pallas_call(
        body, name=name, grid=(m // tm, n // tn),
        in_specs=[pl.BlockSpec((tm, k), lambda i, j: (i, 0)),
                  pl.BlockSpec((1, k), lambda i, j: (0, 0)),
                  pl.BlockSpec((tn, k), lambda i, j: (j, 0))],
        out_specs=[pl.BlockSpec((tm, tn), lambda i, j: (i, j)),
                   pl.BlockSpec((tm, k), lambda i, j: (i, 0))],
        out_shape=[jax.ShapeDtypeStruct((m, n), out_dtype), jax.ShapeDtypeStruct((m, k), BF16)],
        scratch_shapes=[pltpu.VMEM((tm, k), BF16)],
        compiler_params=_params(2),
    )(x, g, wt)


def _mm_res(a, w, res, name):
    m, k = a.shape
    n = w.shape[1]
    tm = _tile(m, 1024)
    tn = _tile(n, 512)

    def body(a_ref, w_ref, r_ref, o_ref):
        o_ref[...] = r_ref[...] + _dot(a_ref[...], w_ref[...])

    return pl.pallas_call(
        body, name=name, grid=(m // tm, n // tn),
        in_specs=[pl.BlockSpec((tm, k), lambda i, j: (i, 0)),
                  pl.BlockSpec((k, tn), lambda i, j: (0, j)),
                  pl.BlockSpec((tm, tn), lambda i, j: (i, j))],
        out_specs=pl.BlockSpec((tm, tn), lambda i, j: (i, j)),
        out_shape=jax.ShapeDtypeStruct((m, n), F32),
        compiler_params=_params(2),
    )(a, w, res)


def _mm_nt(a, w, out_dtype, name):
    m, k = a.shape
    n = w.shape[0]
    tm = _tile(m, 1024)
    tn = _tile(n, 1408)

    def body(a_ref, w_ref, o_ref):
        o_ref[...] = _dot_nt(a_ref[...], w_ref[...]).astype(out_dtype)

    return pl.pallas_call(
        body, name=name, grid=(m // tm, n // tn),
        in_specs=[pl.BlockSpec((tm, k), lambda i, j: (i, 0)),
                  pl.BlockSpec((tn, k), lambda i, j: (j, 0))],
        out_specs=pl.BlockSpec((tm, tn), lambda i, j: (i, j)),
        out_shape=jax.ShapeDtypeStruct((m, n), out_dtype),
        compiler_params=_params(2),
    )(a, w)


def _column_tiles(a, cap):
    if a.ndim == 2:
        s, c = a.shape
        tc = _tile(c, cap)
        return s, c, tc, lambda rows, index: pl.BlockSpec((rows, tc), lambda *g: (index(*g)[0], index(*g)[1]))
    slabs, s, width = a.shape
    tc = _tile(width, cap)
    per = width // tc
    return s, slabs * width, tc, lambda rows, index: pl.BlockSpec(
        (None, rows, tc), lambda *g: (index(*g)[1] // per, index(*g)[0], index(*g)[1] % per))


def _mm_tn(a, b, name):
    s, m, tm, a_spec = _column_tiles(a, 1408)
    n = b.shape[1]
    tn = _tile(n, 1024)

    def body(a_ref, b_ref, o_ref):
        o_ref[...] = _dot_tn(a_ref[...], b_ref[...]).astype(BF16)

    return pl.pallas_call(
        body, name=name, grid=(m // tm, n // tn),
        in_specs=[a_spec(s, lambda i, j: (0, i)),
                  pl.BlockSpec((s, tn), lambda i, j: (0, j))],
        out_specs=pl.BlockSpec((tm, tn), lambda i, j: (i, j)),
        out_shape=jax.ShapeDtypeStruct((m, n), BF16),
        compiler_params=_params(2),
    )(a, b)


def _mm_rmsbwd(a, w, x, g, dres, sends, name):
    m, k, tk, a_spec = _column_tiles(a, 1408)
    n = w.shape[1]
    tm = _tile(m, 1024)
    nk = k // tk
    n_send = len(sends)

    def body(a_ref, w_ref, x_ref, g_ref, r_ref, *rest):
        send_refs, (dx_ref, dxb_ref, dg_ref), rest = rest[:n_send], rest[n_send:n_send + 3], rest[n_send + 3:]
        got_refs, acc, sems = rest[:n_send], rest[n_send], rest[n_send + 1:]
        i, kk = pl.program_id(0), pl.program_id(1)
        if n_send:
            copies = _direct_copies(send_refs, got_refs, [True] * n_send, sems)

            @pl.when((i == 0) & (kk == 0))
            def _():
                for cp in copies:
                    cp.start()

        part = _dot(a_ref[...], w_ref[...])

        @pl.when(kk == 0)
        def _():
            acc[...] = part

        @pl.when(kk > 0)
        def _():
            acc[...] += part

        @pl.when(kk == nk - 1)
        def _():
            def chunk(c, dgp):
                rows = pl.ds(pl.multiple_of(c * NORM_ROWS, NORM_ROWS), NORM_ROWS)
                dh = acc[rows, :]
                xv = x_ref[rows, :]
                r = lax.rsqrt(jnp.mean(xv * xv, axis=-1, keepdims=True) + EPS)
                xh = xv * r
                dxh = dh * g_ref[...]
                dx = r_ref[rows, :] + r * (dxh - xh * jnp.mean(dxh * xh, axis=-1, keepdims=True))
                dx_ref[rows, :] = dx
                dxb_ref[rows, :] = dx.astype(BF16)
                return dgp + jnp.sum(dh * xh, axis=0, keepdims=True)

            dgp = lax.fori_loop(0, tm // NORM_ROWS, chunk, jnp.zeros((1, n), F32))

            @pl.when(i == 0)
            def _():
                dg_ref[...] = dgp

            @pl.when(i > 0)
            def _():
                dg_ref[...] += dgp

        if n_send:
            @pl.when((i == m // tm - 1) & (kk == nk - 1))
            def _():
                for cp in copies:
                    cp.wait()

    out = pl.pallas_call(
        body, name=name, grid=(m // tm, nk),
        in_specs=[a_spec(tm, lambda i, kk: (i, kk)),
                  pl.BlockSpec((tk, n), lambda i, kk: (kk, 0)),
                  pl.BlockSpec((tm, n), lambda i, kk: (i, 0)),
                  pl.BlockSpec((1, n), lambda i, kk: (0, 0)),
                  pl.BlockSpec((tm, n), lambda i, kk: (i, 0))] + [ANY_SPEC] * n_send,
        out_specs=[pl.BlockSpec((tm, n), lambda i, kk: (i, 0)),
                   pl.BlockSpec((tm, n), lambda i, kk: (i, 0)),
                   pl.BlockSpec((1, n), lambda i, kk: (0, 0))] + [ANY_SPEC] * n_send,
        out_shape=[jax.ShapeDtypeStruct((m, n), F32), jax.ShapeDtypeStruct((m, n), BF16),
                   jax.ShapeDtypeStruct((1, n), F32)] + _exchange_shapes(sends, [True] * n_send),
        scratch_shapes=[pltpu.VMEM((tm, n), F32)] + (_exchange_sems(n_send) if n_send else []),
        compiler_params=_params(2),
    )(a, w, x, g, dres, *sends)
    return out[0], out[1], out[2], out[3:]


ANY_SPEC = pl.BlockSpec(memory_space=pl.ANY)
SEMS_PER_OPERAND = N_DEV - 1


def _exchange_sems(n):
    return [pltpu.SemaphoreType.DMA((n, SEMS_PER_OPERAND)), pltpu.SemaphoreType.DMA((n, SEMS_PER_OPERAND)),
            pltpu.SemaphoreType.DMA((n,))]


def _exchange_shapes(parts, scatter):
    return [jax.ShapeDtypeStruct(a.shape if sc else (N_DEV,) + a.shape, a.dtype) for a, sc in zip(parts, scatter)]


def _flat(pos):
    return 4 * pos[0] + 2 * pos[1] + pos[2]


def _remote(src, dst, sems, i, k, to):
    send_sems, recv_sems, _ = sems
    return pltpu.make_async_remote_copy(src_ref=src, dst_ref=dst, send_sem=send_sems.at[i, k],
                                        recv_sem=recv_sems.at[i, k], device_id=to,
                                        device_id_type=pl.DeviceIdType.MESH)


def _direct_copies(ins, outs, scatter, sems):
    x, y, c = lax.axis_index("x"), lax.axis_index("y"), lax.axis_index("c")
    me = _flat((x, y, c))
    copies = []
    for i in range(len(ins)):
        src = ins[i].at[me] if scatter[i] else ins[i]
        copies.append(pltpu.make_async_copy(src, outs[i].at[me], sems[2].at[i]))
    for d in range(1, N_DEV):
        peer = (1 - x if d & 4 else x, 1 - y if d & 2 else y, 1 - c if d & 1 else c)
        for i in range(len(ins)):
            src = ins[i].at[_flat(peer)] if scatter[i] else ins[i]
            copies.append(_remote(src, outs[i].at[me], sems, i, d - 1, peer))
    return copies


def _two_level_gather(ins, outs, sems):
    x, y, c = lax.axis_index("x"), lax.axis_index("y"), lax.axis_index("c")
    me, sibling = (x, y, c), (x, y, 1 - c)
    chips = [(1 - x, y), (x, 1 - y), (1 - x, 1 - y)]
    n = len(ins)

    def block(i, pos):
        return outs[i].at[_flat(pos)]

    local = [pltpu.make_async_copy(ins[i], block(i, me), sems[2].at[i]) for i in range(n)]
    own = [_remote(ins[i], block(i, me), sems, i, 0, sibling) for i in range(n)]
    own += [_remote(ins[i], block(i, me), sems, i, 1 + j, (*chip, c)) for i in range(n) for j, chip in enumerate(chips)]
    passed = [[_remote(block(i, (*chip, c)), block(i, (*chip, c)), sems, i, 4 + j, sibling) for i in range(n)]
              for j, chip in enumerate(chips)]

    def first():
        for cp in local + own:
            cp.start()

    def relay():
        for j, chip in enumerate(chips):
            for i in range(n):
                _remote(ins[i], block(i, (*chip, c)), sems, i, 1 + j, me).wait_recv()
                passed[j][i].start()

    def finish():
        for i in range(n):
            _remote(ins[i], block(i, sibling), sems, i, 0, me).wait_recv()
            for j, chip in enumerate(chips):
                _remote(ins[i], block(i, (*chip, 1 - c)), sems, i, 4 + j, me).wait_recv()
        for cp in own + [cp for row in passed for cp in row]:
            cp.wait_send()
        for cp in local:
            cp.wait()

    return first, relay, finish


def _exchange(parts, scatter, name):
    n = len(parts)

    def body(*refs):
        copies = _direct_copies(refs[:n], refs[n:2 * n], scatter, refs[2 * n:])
        for cp in copies:
            cp.start()
        for cp in copies:
            cp.wait()

    return pl.pallas_call(
        body, name=name, in_specs=[ANY_SPEC] * n, out_specs=[ANY_SPEC] * n,
        out_shape=_exchange_shapes(parts, scatter), scratch_shapes=_exchange_sems(n),
    )(*parts)


def _tri(kind):
    j = lax.broadcasted_iota(jnp.int32, (BLK, BLK), 0)
    s = lax.broadcasted_iota(jnp.int32, (BLK, BLK), 1)
    m = {"after": j > s, "upto": j <= s, "before": j < s}[kind]
    half = jnp.concatenate([jnp.where(m, 1.0, 0.0), jnp.ones((BLK, BLK), F32)], axis=1).astype(BF16)
    return jnp.concatenate([half, half], axis=0)


def _scan_rows(v, tri):
    hi, lo = _split_bf16(v)
    r = _dot(jnp.concatenate([hi, lo], axis=1), tri)
    return r[:, :BLK], r[:, BLK:]


HEADS_PER_STEP = LANES // HEAD_DIM


def _first_head_lanes():
    return lax.broadcasted_iota(jnp.int32, (1, LANES), 1) < HEAD_DIM


def _pair_mean(v, first):
    m0 = jnp.sum(jnp.where(first, v, 0.0), axis=-1, keepdims=True)
    m1 = jnp.sum(jnp.where(first, 0.0, v), axis=-1, keepdims=True)
    return jnp.where(first, m0, m1) * (1.0 / HEAD_DIM)


def _pair_norm(v, g2, first):
    return v * lax.rsqrt(_pair_mean(v * v, first) + EPS) * g2


def _pair_norm_bwd(raw, g2, dn, first):
    r = lax.rsqrt(_pair_mean(raw * raw, first) + EPS)
    xh = raw * r
    dxh = dn * g2
    return r * (dxh - xh * _pair_mean(dxh * xh, first)), jnp.sum(dn * xh, axis=0, keepdims=True)


def _block_diag(v, first):
    zero = jnp.zeros_like(v)
    return jnp.concatenate([jnp.where(first, v, zero), jnp.where(first, zero, v)], axis=0)


def _attn_prep(q_ref, k_ref, v_ref, qg_ref, kg_ref, qc_s, kc_s, vd_s, kd_s, n_blk):
    scale = HEAD_DIM ** -0.5
    first = _first_head_lanes()

    def prep(i, _):
        rows = pl.ds(pl.multiple_of(i * BLK, BLK), BLK)
        both = pl.ds(pl.multiple_of(i * 2 * BLK, 2 * BLK), 2 * BLK)
        qh, ql = _split_bf16(_pair_norm(q_ref[rows, :], qg_ref[...], first) * scale)
        kh, kl = _split_bf16(_pair_norm(k_ref[rows, :], kg_ref[...], first))
        for h in range(HEADS_PER_STEP):
            sl = slice(h * HEAD_DIM, (h + 1) * HEAD_DIM)
            qc_s[h, rows, :] = jnp.concatenate([qh[:, sl], ql[:, sl], qh[:, sl], ql[:, sl]], axis=1)
            kc_s[h, rows, :] = jnp.concatenate([kh[:, sl], kh[:, sl], kl[:, sl], kl[:, sl]], axis=1)
        vd_s[both, :] = _block_diag(v_ref[rows, :].astype(BF16), first)
        if kd_s is not None:
            kd_s[both, :] = _block_diag(kh, first)
        return 0

    lax.fori_loop(0, n_blk, prep, 0)


def _pair_scores(qc, kc_ref, grp):
    zs = []
    for j in range(0, KEY_GROUP, 2):
        two = pl.ds(pl.multiple_of((grp * KEY_GROUP + j) * BLK, 2 * BLK), 2 * BLK)
        z = _dot_nt(qc, kc_ref[two, :])
        zs += [z[:, :BLK], z[:, BLK:]]
    return zs


def _col_minus_row():
    row = lax.broadcasted_iota(jnp.int32, (BLK, BLK), 0)
    col = lax.broadcasted_iota(jnp.int32, (BLK, BLK), 1)
    return col - row


def _softplus(z):
    return jnp.maximum(z, 0.0) + jnp.log(1.0 + jnp.exp(-jnp.abs(z)))


def _attn_fwd(proj, qg, kg, sends, name):
    s = proj.shape[0]
    n_blk = s // BLK
    pairs = ATTN_WIDTH // LANES
    n_send = len(sends)

    def body(q_ref, k_ref, v_ref, qg_ref, kg_ref, *rest):
        send_refs, (o_ref, t_ref), rest = rest[:n_send], rest[n_send:n_send + 2], rest[n_send + 2:]
        got_refs, (qc_s, kc_s, vd_s), sems = rest[:n_send], rest[n_send:n_send + 3], rest[n_send + 3:]
        start, relay, finish = _two_level_gather(send_refs, got_refs, sems)
        step = pl.program_id(0)
        pl.when(step == 0)(start)
        pl.when(step == pairs - 1)(relay)
        tri = _tri("after")
        diff = _col_minus_row()
        first = _first_head_lanes()
        heads = range(HEADS_PER_STEP)
        _attn_prep(q_ref, k_ref, v_ref, qg_ref, kg_ref, qc_s, kc_s, vd_s, None, n_blk)

        def group(qc, qi, grp, carry, acc, masked):
            blocks = [grp * KEY_GROUP + j for j in reversed(range(KEY_GROUP))]
            zs = [_pair_scores(qc[h], kc_s.at[h], grp)[::-1] for h in heads]
            keeps = [diff < (qi - kb) * BLK if masked else None for kb in blocks]
            parts = [[None] * KEY_GROUP for _ in heads]
            for h in heads:
                for j, z in enumerate(zs[h]):
                    sp = _softplus(z)
                    lom = -sp
                    if masked:
                        lom = jnp.where(keeps[j], lom, 0.0)
                    tail, tot = _scan_rows(lom, tri)
                    parts[h][j] = (z - sp + tail, tot)
            carry = list(carry)
            for j, kb in enumerate(blocks):
                ws = []
                for h in heads:
                    lw, tot = parts[h][j]
                    w = jnp.exp(lw + carry[h])
                    if masked:
                        w = jnp.where(keeps[j], w, 0.0)
                    ws.append(w.astype(BF16))
                    carry[h] = carry[h] + tot
                acc = acc + _dot(jnp.concatenate(ws, axis=1),
                                 vd_s[pl.ds(pl.multiple_of(kb * 2 * BLK, 2 * BLK), 2 * BLK), :])
            return tuple(carry), acc

        def q_block(qi, _):
            rows = pl.ds(pl.multiple_of(qi * BLK, BLK), BLK)
            qc = [qc_s[h, rows, :] for h in heads]
            top = qi // KEY_GROUP
            zero = jnp.zeros((BLK, BLK), F32)
            carry, acc = group(qc, qi, top, (zero,) * HEADS_PER_STEP, jnp.zeros((BLK, LANES), F32), True)
            carry, acc = lax.fori_loop(
                0, top, lambda t, c: group(qc, qi, top - 1 - t, c[0], c[1], False), (carry, acc))
            o_ref[rows, :] = acc.astype(BF16)
            t_ref[rows, :] = jnp.where(first, carry[0], carry[1])
            return 0

        lax.fori_loop(0, n_blk, q_block, 0)
        pl.when(step == pairs - 1)(finish)

    col = lambda off: pl.BlockSpec((s, LANES), lambda p: (0, off + p))
    vec = pl.BlockSpec((1, LANES), lambda p: (0, 0))
    out = pl.pallas_call(
        body, name=name, grid=(pairs,),
        in_specs=[col(0), col(pairs), col(2 * pairs), vec, vec] + [ANY_SPEC] * n_send,
        out_specs=[pl.BlockSpec((s, LANES), lambda p: (0, p))] * 2 + [ANY_SPEC] * n_send,
        out_shape=[jax.ShapeDtypeStruct((s, ATTN_WIDTH), BF16), jax.ShapeDtypeStruct((s, ATTN_WIDTH), F32)]
        + _exchange_shapes(sends, [False] * n_send),
        scratch_shapes=[pltpu.VMEM((HEADS_PER_STEP, s, 4 * HEAD_DIM), BF16)] * 2
        + [pltpu.VMEM((HEADS_PER_STEP * s, LANES), BF16)] + _exchange_sems(n_send),
        compiler_params=_params(1),
    )(proj, proj, proj, qg, kg, *sends)
    return out[0], out[1], out[2:]


def _attn_bwd(proj, dcat, tsum, qg, kg, sends, name):
    s = proj.shape[0]
    n_blk = s // BLK
    pairs = ATTN_WIDTH // LANES
    scale = HEAD_DIM ** -0.5
    n_send = len(sends)
    n_scratch = 7

    def body(q_ref, k_ref, v_ref, do_ref, t_ref, qg_ref, kg_ref, *rest):
        send_refs, rest = rest[:n_send], rest[n_send:]
        (dq_ref, dk_ref, dv_ref, dqg_ref, dkg_ref), rest = rest[:5], rest[5:]
        got_refs, scratch, sems = rest[:n_send], rest[n_send:n_send + n_scratch], rest[n_send + n_scratch:]
        qc_s, kc_s, vd_s, kd_s, qd_s, dob_s, dkv_s = scratch
        copies = _direct_copies(send_refs, got_refs, [True] * n_send, sems)

        @pl.when(pl.program_id(0) == 0)
        def _():
            for cp in copies:
                cp.start()

        tri_p = _tri("upto")
        tri_h = _tri("before")
        diff = _col_minus_row()

        @pl.when(pl.program_id(0) == 0)
        def _():
            dqg_ref[...] = jnp.zeros_like(dqg_ref)
            dkg_ref[...] = jnp.zeros_like(dkg_ref)

        first = _first_head_lanes()
        heads = range(HEADS_PER_STEP)
        _attn_prep(q_ref, k_ref, v_ref, qg_ref, kg_ref, qc_s, kc_s, vd_s, kd_s, n_blk)

        def prep(i, _):
            rows = pl.ds(pl.multiple_of(i * BLK, BLK), BLK)
            both = pl.ds(pl.multiple_of(i * 2 * BLK, 2 * BLK), 2 * BLK)
            dob = do_ref[rows, :].astype(BF16)
            dob_s[rows, :] = dob
            none = jnp.zeros((BLK, HEAD_DIM), BF16)
            for h in heads:
                qd_s[h, both, :] = jnp.concatenate(
                    [jnp.concatenate([qc_s[h, rows, 0:HEAD_DIM], none], axis=1),
                     jnp.concatenate([none, dob[:, h * HEAD_DIM:(h + 1) * HEAD_DIM]], axis=1)], axis=0)
                dkv_s[h, rows, :] = jnp.zeros((BLK, LANES), F32)
            return 0

        lax.fori_loop(0, n_blk, prep, 0)

        def group(qc, qd, dob, tq, qi, grp, pc, hc, dq, masked):
            blocks = [grp * KEY_GROUP + j for j in range(KEY_GROUP)]
            cols_of = [pl.ds(pl.multiple_of(kb * BLK, BLK), BLK) for kb in blocks]
            both_of = [pl.ds(pl.multiple_of(kb * 2 * BLK, 2 * BLK), 2 * BLK) for kb in blocks]
            zs = [_pair_scores(qc[h], kc_s.at[h], grp) for h in heads]
            das =[_dot_nt(dob, vd_s[both, :]) for both in both_of]
            keeps = [diff < (qi - kb) * BLK if masked else None for kb in blocks]
            lbs = [[None] * KEY_GROUP for _ in heads]
            scans = [[None] * KEY_GROUP for _ in heads]
            for h in heads:
                for j, z in enumerate(zs[h]):
                    sp = _softplus(z)
                    lom = -sp
                    if masked:
                        lom = jnp.where(keeps[j], lom, 0.0)
                    lbs[h][j] = z - sp
                    scans[h][j] = _scan_rows(lom, tri_p)
            pc, hc = list(pc), list(hc)
            avs = [[None] * KEY_GROUP for _ in heads]
            gws = [[None] * KEY_GROUP for _ in heads]
            hscans = [[None] * KEY_GROUP for _ in heads]
            for h in heads:
                for j in range(KEY_GROUP):
                    p_in, p_tot = scans[h][j]
                    a = jnp.exp(lbs[h][j] + (tq[h] - pc[h] - p_in))
                    if masked:
                        a = jnp.where(keeps[j], a, 0.0)
                    pc[h] = pc[h] + p_tot
                    gw = das[j][:, h * BLK:(h + 1) * BLK] * a
                    avs[h][j] = a.astype(BF16)
                    gws[h][j] = gw
                    hscans[h][j] = _scan_rows(gw, tri_h)
            dzs = [[None] * KEY_GROUP for _ in heads]
            for h in heads:
                for j in range(KEY_GROUP):
                    h_in, g_tot = hscans[h][j]
                    gw = gws[h][j]
                    dz = gw - jnp.exp(lbs[h][j]) * (gw + hc[h] + h_in)
                    if masked:
                        dz = jnp.where(keeps[j], dz, 0.0)
                    hc[h] = hc[h] + g_tot
                    dzs[h][j] = dz.astype(BF16)
            for j, both in enumerate(both_of):
                dq = dq + _dot(jnp.concatenate([dzs[h][j] for h in heads], axis=1), kd_s[both, :])
            for h in heads:
                for j, cols in enumerate(cols_of):
                    dkv_s[h, cols, :] += _dot_tn(jnp.concatenate([dzs[h][j], avs[h][j]], axis=0), qd[h])
            return tuple(pc), tuple(hc), dq

        def q_block(qi, dqg):
            rows = pl.ds(pl.multiple_of(qi * BLK, BLK), BLK)
            both = pl.ds(pl.multiple_of(qi * 2 * BLK, 2 * BLK), 2 * BLK)
            qc = [qc_s[h, rows, :] for h in heads]
            qd = [qd_s[h, both, :] for h in heads]
            dob = dob_s[rows, :]
            tboth = t_ref[rows, :]
            tq = [jnp.concatenate([tboth[:, h * HEAD_DIM:(h + 1) * HEAD_DIM]] * 2, axis=1) for h in heads]
            zero = (jnp.zeros((BLK, BLK), F32),) * HEADS_PER_STEP
            top = qi // KEY_GROUP
            pc, hc, dq = lax.fori_loop(
                0, top, lambda grp, c: group(qc, qd, dob, tq, qi, grp, c[0], c[1], c[2], False),
                (zero, zero, jnp.zeros((BLK, LANES), F32)))
            _, _, dq = group(qc, qd, dob, tq, qi, top, pc, hc, dq, True)
            dq_raw, dg = _pair_norm_bwd(q_ref[rows, :], qg_ref[...], dq * scale, first)
            dq_ref[rows, :] = dq_raw.astype(BF16)
            return dqg + dg

        dqg = lax.fori_loop(0, n_blk, q_block, jnp.zeros((1, LANES), F32))

        def finish(i, dkg):
            rows = pl.ds(pl.multiple_of(i * BLK, BLK), BLK)
            dk = jnp.concatenate([dkv_s[h, rows, 0:HEAD_DIM] for h in heads], axis=1)
            dv = jnp.concatenate([dkv_s[h, rows, HEAD_DIM:2 * HEAD_DIM] for h in heads], axis=1)
            dk_raw, dg = _pair_norm_bwd(k_ref[rows, :], kg_ref[...], dk, first)
            dk_ref[rows, :] = dk_raw.astype(BF16)
            dv_ref[rows, :] = dv.astype(BF16)
            return dkg + dg

        dkg = lax.fori_loop(0, n_blk, finish, jnp.zeros((1, LANES), F32))
        dqg_ref[0:1, :] += dqg
        dkg_ref[0:1, :] += dkg

        @pl.when(pl.program_id(0) == pairs - 1)
        def _():
            for cp in copies:
                cp.wait()

    col = lambda off: pl.BlockSpec((s, LANES), lambda p: (0, off + p))
    vec = pl.BlockSpec((1, LANES), lambda p: (0, 0))
    small = pl.BlockSpec((8, LANES), lambda p: (0, 0))
    out = pl.pallas_call(
        body, name=name, grid=(pairs,),
        in_specs=[col(0), col(pairs), col(2 * pairs), col(0), col(0), vec, vec] + [ANY_SPEC] * n_send,
        out_specs=[col(0)] * 3 + [small] * 2 + [ANY_SPEC] * n_send,
        out_shape=[jax.ShapeDtypeStruct((s, ATTN_WIDTH), BF16)] * 3 + [jax.ShapeDtypeStruct((8, LANES), F32)] * 2
        + _exchange_shapes(sends, [True] * n_send),
        scratch_shapes=[pltpu.VMEM((HEADS_PER_STEP, s, 4 * HEAD_DIM), BF16)] * 2
        + [pltpu.VMEM((HEADS_PER_STEP * s, LANES), BF16)] * 2
        + [pltpu.VMEM((HEADS_PER_STEP, HEADS_PER_STEP * s, LANES), BF16), pltpu.VMEM((s, LANES), BF16),
           pltpu.VMEM((HEADS_PER_STEP, s, LANES), F32)] + _exchange_sems(n_send),
        compiler_params=_params(1),
    )(proj, proj, proj, dcat, tsum, qg, kg, *sends)
    return out[:5], out[5:]


CONV_ROWS = 128


def _shifted(window, shift, halo):
    if shift == 0:
        return window[halo:, :]
    return pltpu.roll(window, shift, 0)[halo:, :]


SUBLANES = 8


def _row_shifts(window, up):
    n = window.shape[0]
    return [window] + [pltpu.roll(window, n - b if up else b, 0) for b in range(1, SUBLANES)]


def _earlier(shifts, back, rows):
    a, b = divmod(back, SUBLANES)
    return shifts[b][CONV_HALO - SUBLANES * a:CONV_HALO - SUBLANES * a + rows, :]


def _later(shifts, ahead, rows):
    a, b = divmod(ahead, SUBLANES)
    return shifts[b][SUBLANES * a:SUBLANES * a + rows, :]


def _conv_taps(shifts, w_ref, rows):
    y = None
    for k in range(CONV_KERNEL):
        term = _earlier(shifts, CONV_KERNEL - 1 - k, rows) * w_ref[k:k + 1, :]
        y = term if y is None else y + term
    return y


def _conv_fwd(proj, w, b, lg, lb, name):
    s = proj.shape[0]
    cw = w.shape[1]
    rows = CONV_ROWS
    blk_a = (proj.shape[1] - 2 * cw) // cw

    def body(a_ref, g_ref, w_ref, b_ref, lg_ref, lb_ref, o_ref, y_ref, u_s):
        u_s[0:CONV_HALO, :] = jnp.zeros((CONV_HALO, cw), F32)

        def glu(i, _):
            r0 = pl.multiple_of(i * rows, rows)
            u_s[pl.ds(CONV_HALO + r0, rows), :] = a_ref[pl.ds(r0, rows), :] * _sigmoid(g_ref[pl.ds(r0, rows), :])
            return 0

        lax.fori_loop(0, s // rows, glu, 0)

        def chunk(i, _):
            r0 = pl.multiple_of(i * rows, rows)
            y = _conv_taps(_row_shifts(u_s[pl.ds(r0, CONV_HALO + rows), :], False), w_ref, rows) + b_ref[...]
            y_ref[pl.ds(r0, rows), :] = y
            yc = y - jnp.mean(y, axis=-1, keepdims=True)
            n = yc * lax.rsqrt(jnp.mean(yc * yc, axis=-1, keepdims=True) + EPS)
            ln = n * lg_ref[...] + lb_ref[...]
            o_ref[pl.ds(r0, rows), :] = (ln * _sigmoid(ln)).astype(BF16)
            return 0

        lax.fori_loop(0, s // rows, chunk, 0)

    vec = pl.BlockSpec((1, cw), lambda i: (0, 0))
    return pl.pallas_call(
        body, name=name, grid=(1,),
        in_specs=[pl.BlockSpec((s, cw), lambda i: (0, blk_a)), pl.BlockSpec((s, cw), lambda i: (0, blk_a + 1)),
                  pl.BlockSpec((CONV_KERNEL, cw), lambda i: (0, 0)), vec, vec, vec],
        out_specs=[pl.BlockSpec((s, cw), lambda i: (0, 0))] * 2,
        out_shape=[jax.ShapeDtypeStruct((s, cw), BF16), jax.ShapeDtypeStruct((s, cw), F32)],
        scratch_shapes=[pltpu.VMEM((CONV_HALO + s, cw), F32)],
        compiler_params=_params(1),
    )(proj, proj, w, b, lg, lb)


def _conv_bwd(proj, y, dcat, w, lg, lb, name):
    s = proj.shape[0]
    cw = w.shape[1]
    rows = CONV_ROWS
    blk_a = (proj.shape[1] - 2 * cw) // cw
    n_chunk = s // rows

    def body(a_ref, g_ref, y_ref, dc_ref, w_ref, lg_ref, lb_ref, o_ref, dw_ref, db_ref, dlg_ref, dlb_ref,
             u_s, dy_s, dw_s):
        u_s[0:CONV_HALO, :] = jnp.zeros((CONV_HALO, cw), F32)
        dy_s[pl.ds(s, CONV_HALO), :] = jnp.zeros((CONV_HALO, cw), F32)
        dw_s[...] = jnp.zeros_like(dw_s)

        def glu(i, _):
            r0 = pl.multiple_of(i * rows, rows)
            u_s[pl.ds(CONV_HALO + r0, rows), :] = a_ref[pl.ds(r0, rows), :] * _sigmoid(g_ref[pl.ds(r0, rows), :])
            return 0

        lax.fori_loop(0, n_chunk, glu, 0)

        def chunk(i, carry):
            db, dlg, dlb = carry
            r0 = pl.multiple_of(i * rows, rows)
            shifts = _row_shifts(u_s[pl.ds(r0, CONV_HALO + rows), :], False)
            y = y_ref[pl.ds(r0, rows), :]
            yc = y - jnp.mean(y, axis=-1, keepdims=True)
            r = lax.rsqrt(jnp.mean(yc * yc, axis=-1, keepdims=True) + EPS)
            n = yc * r
            ln = n * lg_ref[...] + lb_ref[...]
            sg = _sigmoid(ln)
            dln = dc_ref[pl.ds(r0, rows), :] * (sg * (1.0 + ln * (1.0 - sg)))
            dn = dln * lg_ref[...]
            dy = r * (dn - jnp.mean(dn, axis=-1, keepdims=True) - n * jnp.mean(dn * n, axis=-1, keepdims=True))
            dy_s[pl.ds(r0, rows), :] = dy
            for k in range(CONV_KERNEL):
                prod = _earlier(shifts, CONV_KERNEL - 1 - k, rows) * dy
                dw_s[k] += jnp.sum(prod.reshape(rows // 8, 8, cw), axis=0)
            return (db + jnp.sum(dy, axis=0, keepdims=True),
                    dlg + jnp.sum(dln * n, axis=0, keepdims=True),
                    dlb + jnp.sum(dln, axis=0, keepdims=True))

        zero = jnp.zeros((1, cw), F32)
        db, dlg, dlb = lax.fori_loop(0, n_chunk, chunk, (zero, zero, zero))
        db_ref[...] = db
        dlg_ref[...] = dlg
        dlb_ref[...] = dlb
        for k in range(CONV_KERNEL):
            dw_ref[k:k + 1, :] = jnp.sum(dw_s[k], axis=0, keepdims=True)

        def back(i, _):
            r0 = pl.multiple_of(i * rows, rows)
            shifts = _row_shifts(dy_s[pl.ds(r0, rows + CONV_HALO), :], True)
            du = None
            for k in range(CONV_KERNEL):
                term = _later(shifts, CONV_KERNEL - 1 - k, rows) * w_ref[k:k + 1, :]
                du = term if du is None else du + term
            av = a_ref[pl.ds(r0, rows), :]
            sg = _sigmoid(g_ref[pl.ds(r0, rows), :])
            o_ref[pl.ds(r0, rows), 0:cw] = (du * sg).astype(BF16)
            o_ref[pl.ds(r0, rows), cw:2 * cw] = (du * av * sg * (1.0 - sg)).astype(BF16)
            return 0

        lax.fori_loop(0, n_chunk, back, 0)

    vec = pl.BlockSpec((1, cw), lambda i: (0, 0))
    wspec = pl.BlockSpec((CONV_KERNEL, cw), lambda i: (0, 0))
    return pl.pallas_call(
        body, name=name, grid=(1,),
        in_specs=[pl.BlockSpec((s, cw), lambda i: (0, blk_a)), pl.BlockSpec((s, cw), lambda i: (0, blk_a + 1)),
                  pl.BlockSpec((s, cw), lambda i: (0, 0)), pl.BlockSpec((s, cw), lambda i: (0, 1)), wspec, vec, vec],
        out_specs=[pl.BlockSpec((s, 2 * cw), lambda i: (0, 0)), wspec, vec, vec, vec],
        out_shape=[jax.ShapeDtypeStruct((s, 2 * cw), BF16), jax.ShapeDtypeStruct((CONV_KERNEL, cw), F32)]
        + [jax.ShapeDtypeStruct((1, cw), F32)] * 3,
        scratch_shapes=[pltpu.VMEM((CONV_HALO + s, cw), F32), pltpu.VMEM((s + CONV_HALO, cw), F32),
                        pltpu.VMEM((CONV_KERNEL, 8, cw), F32)],
        compiler_params=_params(1),
    )(proj, proj, y, dcat, w, lg, lb)


FFN_ROWS = 256


def _ffn_gate(g_ref, r0, rows, w_ref, b_ref):
    cur = g_ref[pl.ds(r0, rows), :].astype(F32)
    prev = g_ref[pl.ds(pl.multiple_of(jnp.maximum(r0 - FFN_HALO, 0), FFN_HALO), FFN_HALO), :].astype(F32)
    prev = jnp.where(r0 > 0, prev, 0.0)
    window = jnp.concatenate([prev, cur], axis=0)
    gc = cur * w_ref[FFN_KERNEL - 1:FFN_KERNEL, :] + b_ref[...]
    for k in range(FFN_KERNEL - 1):
        gc = gc + _shifted(window, FFN_KERNEL - 1 - k, FFN_HALO) * w_ref[k:k + 1, :]
    return gc, window


def _ffn_fwd(up, w, b, name):
    s = up.shape[0]
    f = w.shape[1]
    tc = _pick(f, (256, 128))
    nc = f // tc
    rows = _pick(s, (FFN_ROWS, 128))

    def body(g_ref, v_ref, w_ref, b_ref, o_ref):
        def chunk(i, _):
            r0 = pl.multiple_of(i * rows, rows)
            gc, _w = _ffn_gate(g_ref, r0, rows, w_ref, b_ref)
            o_ref[pl.ds(r0, rows), :] = (gc * _sigmoid(gc) * v_ref[pl.ds(r0, rows), :].astype(F32)).astype(BF16)
            return 0

        lax.fori_loop(0, s // rows, chunk, 0)

    return pl.pallas_call(
        body, name=name, grid=(nc,),
        in_specs=[pl.BlockSpec((s, tc), lambda j: (0, j)), pl.BlockSpec((s, tc), lambda j: (0, nc + j)),
                  pl.BlockSpec((FFN_KERNEL, tc), lambda j: (0, j)), pl.BlockSpec((1, tc), lambda j: (0, j))],
        out_specs=pl.BlockSpec((s, tc), lambda j: (0, j)),
        out_shape=jax.ShapeDtypeStruct((s, f), BF16),
        compiler_params=_params(1),
    )(up, up, w, b)


def _ffn_bwd(up, dact, w, b, name):
    s = up.shape[0]
    f = w.shape[1]
    tc = _pick(f, (256, 128))
    nc = f // tc
    rows = _pick(s, (FFN_ROWS, 128))
    n_chunk = s // rows

    def body(g_ref, v_ref, da_ref, w_ref, b_ref, d_ref, dw_ref, db_ref, dgc_s):
        dg_ref, dv_ref = d_ref.at[0], d_ref.at[1]
        dgc_s[pl.ds(s, FFN_HALO), :] = jnp.zeros((FFN_HALO, tc), F32)

        def chunk(i, carry):
            r0 = pl.multiple_of(i * rows, rows)
            gc, window = _ffn_gate(g_ref, r0, rows, w_ref, b_ref)
            sg = _sigmoid(gc)
            da = da_ref[pl.ds(r0, rows), :].astype(F32)
            dv_ref[pl.ds(r0, rows), :] = (da * gc * sg).astype(BF16)
            dgc = da * v_ref[pl.ds(r0, rows), :].astype(F32) * (sg * (1.0 + gc * (1.0 - sg)))
            dgc_s[pl.ds(r0, rows), :] = dgc
            out = [carry[0] + jnp.sum(dgc, axis=0, keepdims=True)]
            for k in range(FFN_KERNEL):
                out.append(carry[1 + k] + jnp.sum(_shifted(window, FFN_KERNEL - 1 - k, FFN_HALO) * dgc,
                                                  axis=0, keepdims=True))
            return tuple(out)

        zero = jnp.zeros((1, tc), F32)
        sums = lax.fori_loop(0, n_chunk, chunk, (zero,) * (1 + FFN_KERNEL))
        db_ref[...] = sums[0]
        for k in range(FFN_KERNEL):
            dw_ref[k:k + 1, :] = sums[1 + k]

        def back(i, _):
            r0 = pl.multiple_of(i * rows, rows)
            window = dgc_s[pl.ds(r0, rows + FFN_HALO), :]
            dg = window[:rows, :] * w_ref[FFN_KERNEL - 1:FFN_KERNEL, :]
            for k in range(FFN_KERNEL - 1):
                sh = FFN_KERNEL - 1 - k
                dg = dg + pltpu.roll(window, rows + FFN_HALO - sh, 0)[:rows, :] * w_ref[k:k + 1, :]
            dg_ref[pl.ds(r0, rows), :] = dg.astype(BF16)
            return 0

        lax.fori_loop(0, n_chunk, back, 0)

    blk = lambda off: pl.BlockSpec((s, tc), lambda j: (0, off + j))
    return pl.pallas_call(
        body, name=name, grid=(nc,),
        in_specs=[blk(0), blk(nc), blk(0), pl.BlockSpec((FFN_KERNEL, tc), lambda j: (0, j)),
                  pl.BlockSpec((1, tc), lambda j: (0, j))],
        out_specs=[pl.BlockSpec((2, s, tc), lambda j: (0, 0, j)), pl.BlockSpec((FFN_KERNEL, tc), lambda j: (0, j)),
                   pl.BlockSpec((1, tc), lambda j: (0, j))],
        out_shape=[jax.ShapeDtypeStruct((2, s, f), BF16),
                   jax.ShapeDtypeStruct((FFN_KERNEL, f), F32), jax.ShapeDtypeStruct((1, f), F32)],
        scratch_shapes=[pltpu.VMEM((s + FFN_HALO, tc), F32)],
        compiler_params=_params(1),
    )(up, up, dact, w, b)


def _loss_head(y, target, name):
    m, n = y.shape
    tm = _pick(m, (256, 128))

    def body(y_ref, t_ref, l_ref, d_ref, db_ref):
        e = y_ref[...] - t_ref[...]
        part = 0.5 * jnp.sum(jnp.sum(e * e, axis=-1, keepdims=True) / n, axis=0, keepdims=True)

        @pl.when(pl.program_id(0) == 0)
        def _():
            l_ref[...] = jnp.zeros_like(l_ref)

        l_ref[...] += part
        d = e / n
        d_ref[...] = d
        db_ref[...] = d.astype(BF16)

    return pl.pallas_call(
        body, name=name, grid=(m // tm,),
        in_specs=[pl.BlockSpec((tm, n), lambda i: (i, 0))] * 2,
        out_specs=[pl.BlockSpec((8, LANES), lambda i: (0, 0)), pl.BlockSpec((tm, n), lambda i: (i, 0)),
                   pl.BlockSpec((tm, n), lambda i: (i, 0))],
        out_shape=[jax.ShapeDtypeStruct((8, LANES), F32), jax.ShapeDtypeStruct((m, n), F32),
                   jax.ShapeDtypeStruct((m, n), BF16)],
        compiler_params=_params(1),
    )(y, target)


def _adamw_math(w, g, m, v):
    m = ADAM_B1 * m + (1.0 - ADAM_B1) * g
    v = ADAM_B2 * v + (1.0 - ADAM_B2) * (g * g)
    m_hat = m / (1.0 - ADAM_B1 ** ADAM_STEP)
    v_hat = v / (1.0 - ADAM_B2 ** ADAM_STEP)
    delta = -ADAM_LR * (m_hat / (jnp.sqrt(v_hat) + ADAM_EPS) + ADAM_WD * w)
    return delta, m, v


def _sum_adamw(parts, w, m, v, name):
    depth, r, c = w.shape
    tr = max(t for t in range(16, min(r, 192) + 1, 16) if r % t == 0)
    steps = r // tr

    def body(*refs):
        p_refs = refs[:depth]
        w_ref, m_ref, v_ref, g_out, d_out, m_out, v_out = refs[depth:]
        for layer in range(depth):
            @pl.when(pl.program_id(0) == layer)
            def _():
                g = p_refs[layer][0].astype(F32)
                for src in range(1, N_DEV):
                    g = g + p_refs[layer][src].astype(F32)
                d, mn, vn = _adamw_math(w_ref[0], g, m_ref[0], v_ref[0])
                g_out[0] = g
                d_out[0] = d
                m_out[0] = mn
                v_out[0] = vn

    def part_spec(layer):
        return pl.BlockSpec((N_DEV, tr, c), lambda l, i: (0, jnp.clip((l - layer) * steps + i, 0, steps - 1), 0))

    blk = pl.BlockSpec((1, tr, c), lambda l, i: (l, i, 0))
    return pl.pallas_call(
        body, name=name, grid=(depth, steps),
        in_specs=[part_spec(layer) for layer in range(depth)] + [blk, blk, blk],
        out_specs=[blk] * 4,
        out_shape=[jax.ShapeDtypeStruct(w.shape, F32)] * 4,
        compiler_params=_params(2),
    )(*parts, w, m, v)


def _sum_rows(parts, name):
    _, r, c = parts.shape

    def body(p_ref, o_ref):
        g = p_ref[0]
        for src in range(1, N_DEV):
            g = g + p_ref[src]
        o_ref[...] = g

    return pl.pallas_call(
        body, name=name, grid=(1,),
        in_specs=[pl.BlockSpec((N_DEV, r, c), lambda i: (0, 0, 0))],
        out_specs=pl.BlockSpec((r, c), lambda i: (0, 0)),
        out_shape=jax.ShapeDtypeStruct((r, c), F32),
        compiler_params=_params(1),
    )(parts)


def _adamw_rows(w, g, m, v, name):
    r, c = w.shape

    def body(w_ref, g_ref, m_ref, v_ref, d_out, m_out, v_out):
        d, mn, vn = _adamw_math(w_ref[...], g_ref[...], m_ref[...], v_ref[...])
        d_out[...] = d
        m_out[...] = mn
        v_out[...] = vn

    blk = pl.BlockSpec((r, c), lambda i: (0, 0))
    return pl.pallas_call(
        body, name=name, grid=(1,), in_specs=[blk] * 4, out_specs=[blk] * 3,
        out_shape=[jax.ShapeDtypeStruct((r, c), F32)] * 3,
        compiler_params=_params(1),
    )(w, g, m, v)


PACK_TILE = 8 * LANES


def _pack(arrays):
    rows = []
    for a in arrays:
        flat = a.reshape(-1).astype(F32)
        pad = (-flat.shape[0]) % PACK_TILE
        rows.append(jnp.pad(flat, (0, pad)).reshape(-1, LANES))
    return jnp.concatenate(rows, axis=0)


def _unpack(packed, shapes, lead=()):
    out, r0 = [], 0
    for shp in shapes:
        size = 1
        for d in shp:
            size *= d
        nrows = -(-size // PACK_TILE) * 8
        piece = packed[..., r0:r0 + nrows, :].reshape(lead + (nrows * LANES,))[..., :size]
        out.append(piece.reshape(lead + tuple(shp)))
        r0 += nrows
    return out


def kernel(x, norm1_g, w_in, q_norm_g, k_norm_g, conv_dw_w, conv_dw_b, conv_ln_g, conv_ln_b, w_out, norm2_g, w_up, ffn_dw_w, ffn_dw_b, w_down, loss_target, m_norm1_g, m_w_in, m_q_norm_g, m_k_norm_g, m_conv_dw_w, m_conv_dw_b, m_conv_ln_g, m_conv_ln_b, m_w_out, m_norm2_g, m_w_up, m_ffn_dw_w, m_ffn_dw_b, m_w_down, v_norm1_g, v_w_in, v_q_norm_g, v_k_norm_g, v_conv_dw_w, v_conv_dw_b, v_conv_ln_g, v_conv_ln_b, v_w_out, v_norm2_g, v_w_up, v_ffn_dw_w, v_ffn_dw_b, v_w_down):
    depth, d_model, in_shard = w_in.shape
    out_shard = w_out.shape[1]
    up_shard = w_up.shape[2]
    down_shard = w_down.shape[1]
    d_ff = down_shard * N_DEV
    conv_w = conv_dw_b.shape[1]
    cw_shard = conv_dw_w.shape[2]
    fw_shard = ffn_dw_w.shape[2]
    me = 4 * lax.axis_index("x") + 2 * lax.axis_index("y") + lax.axis_index("c")

    transposed = lambda a: a.transpose(0, 2, 1)
    b_in, b_out, b_up, b_down = (transposed(w_in).astype(BF16), w_out.astype(BF16), transposed(w_up).astype(BF16),
                                 w_down.astype(BF16))
    rows_major = lambda g: g.reshape(N_DEV * g.shape[1], g.shape[2])
    g_in0, g_small = _exchange([b_in[0], _pack([conv_dw_w, ffn_dw_w])], [False, False], name="gather_first")
    wf_in, wf_out, wf_up, wf_down = [rows_major(g_in0)] + [None] * (depth - 1), [None] * depth, [None] * depth, [None] * depth
    g_cw, g_fw = _unpack(g_small, [conv_dw_w.shape, ffn_dw_w.shape], lead=(N_DEV,))
    cwf = g_cw.transpose(1, 2, 0, 3).reshape(depth, CONV_KERNEL, conv_w)
    fwf = g_fw.transpose(1, 2, 0, 3).reshape(depth, FFN_KERNEL, d_ff)

    row = lambda a, l: a[l].reshape(1, -1)
    both_heads = lambda a, l: jnp.tile(row(a, l), (1, HEADS_PER_STEP))

    xs = x[0]
    saved = []
    for l in range(depth):
        proj, h1 = _mm_rms(xs, row(norm1_g, l), wf_in[l], F32, name="fwd_in")
        sends = [b_out[l], b_up[l], b_down[l]] + ([b_in[l + 1]] if l + 1 < depth else [])
        attn, tsum, got = _attn_fwd(proj, both_heads(q_norm_g, l), both_heads(k_norm_g, l), sends, name="fwd_attn")
        wf_out[l], wf_up[l], wf_down[l] = rows_major(got[0]), rows_major(got[1]), rows_major(got[2])
        if l + 1 < depth:
            wf_in[l + 1] = rows_major(got[3])
        conv, conv_y = _conv_fwd(proj, cwf[l], row(conv_dw_b, l), row(conv_ln_g, l), row(conv_ln_b, l),
                                 name="fwd_conv")
        cat = jnp.concatenate([attn, conv], axis=1)
        x_mid = _mm_res(cat, wf_out[l], xs, name="fwd_out")
        up, h2 = _mm_rms(x_mid, row(norm2_g, l), wf_up[l], BF16, name="fwd_up")
        act = _ffn_fwd(up, fwf[l], row(ffn_dw_b, l), name="fwd_ffn")
        x_next = _mm_res(act, wf_down[l], x_mid, name="fwd_down")
        saved.append((xs, h1, proj, tsum, cat, x_mid, h2, up, act, conv_y))
        xs = x_next

    loss_tile, dx, dxb = _loss_head(xs, loss_target[0], name="loss_head")
    loss = lax.psum(loss_tile[0, 0], ("x", "y", "c"))

    r_in, r_out, r_up, r_down = [None] * depth, [None] * depth, [None] * depth, [None] * depth
    row_blocks = lambda g: g.reshape(N_DEV, g.shape[0] // N_DEV, g.shape[1])
    small = {k: [None] * depth for k in ("norm1_g", "q_norm_g", "k_norm_g", "conv_dw_w", "conv_dw_b", "conv_ln_g",
                                         "conv_ln_b", "norm2_g", "ffn_dw_w", "ffn_dw_b")}
    gw_in = None
    for l in reversed(range(depth)):
        xs, h1, proj, tsum, cat, x_mid, h2, up, act, conv_y = saved[l]
        dact = _mm_nt(dxb, wf_down[l], BF16, name="bwd_dact")
        gw_down = _mm_tn(act, dxb, name="bwd_gw_down")
        dup, small["ffn_dw_w"][l], small["ffn_dw_b"][l] = _ffn_bwd(up, dact, fwf[l], row(ffn_dw_b, l), name="bwd_ffn")
        gw_up = _mm_tn(dup, h2, name="bwd_gw_up")
        dx, dxb, small["norm2_g"][l], _ = _mm_rmsbwd(dup, wf_up[l], x_mid, row(norm2_g, l), dx, [], name="bwd_up")
        dcat = _mm_nt(dxb, wf_out[l], F32, name="bwd_dcat")
        gw_out = _mm_tn(cat, dxb, name="bwd_gw_out")
        dglu, small["conv_dw_w"][l], small["conv_dw_b"][l], small["conv_ln_g"][l], small["conv_ln_b"][l] = _conv_bwd(
            proj, conv_y, dcat, cwf[l], row(conv_ln_g, l), row(conv_ln_b, l), name="bwd_conv")
        sends = [row_blocks(gw_down), row_blocks(gw_up), row_blocks(gw_out)] + ([row_blocks(gw_in)] if l + 1 < depth else [])
        (dq, dk, dv, dqg, dkg), got = _attn_bwd(proj, dcat, tsum, both_heads(q_norm_g, l), both_heads(k_norm_g, l),
                                                sends, name="bwd_attn")
        r_down[l], r_up[l], r_out[l] = got[:3]
        if l + 1 < depth:
            r_in[l + 1] = got[3]
        small["q_norm_g"][l] = dqg[0:1, :HEAD_DIM] + dqg[0:1, HEAD_DIM:]
        small["k_norm_g"][l] = dkg[0:1, :HEAD_DIM] + dkg[0:1, HEAD_DIM:]
        dproj = jnp.concatenate([dq, dk, dv, dglu], axis=1)
        gw_in = _mm_tn(dproj, h1, name="bwd_gw_in")
        sends = [row_blocks(gw_in)] if l == 0 else []
        dx, dxb, small["norm1_g"][l], got = _mm_rmsbwd(dproj, wf_in[l], xs, row(norm1_g, l), dx, sends, name="bwd_in")
        if l == 0:
            r_in[0] = got[0]
    grad_x = dx[None]

    names = ["norm1_g", "q_norm_g", "k_norm_g", "conv_dw_w", "conv_dw_b", "conv_ln_g", "conv_ln_b", "norm2_g",
             "ffn_dw_w", "ffn_dw_b"]
    full_shapes = {"norm1_g": norm1_g.shape, "q_norm_g": q_norm_g.shape, "k_norm_g": k_norm_g.shape,
                   "conv_dw_w": (depth, CONV_KERNEL, conv_w), "conv_dw_b": conv_dw_b.shape,
                   "conv_ln_g": conv_ln_g.shape, "conv_ln_b": conv_ln_b.shape, "norm2_g": norm2_g.shape,
                   "ffn_dw_w": (depth, FFN_KERNEL, d_ff), "ffn_dw_b": ffn_dw_b.shape}
    partial = _pack([jnp.stack(small[k]).reshape(full_shapes[k]) for k in names])
    (all_partials,) = _exchange([partial], [False], name="gather_small_grads")
    big = {
        "w_in": [transposed(a) for a in _sum_adamw(r_in, transposed(w_in), transposed(m_w_in), transposed(v_w_in),
                                                   name="adamw_in")],
        "w_out": _sum_adamw(r_out, w_out, m_w_out, v_w_out, name="adamw_out"),
        "w_up": [transposed(a) for a in _sum_adamw(r_up, transposed(w_up), transposed(m_w_up), transposed(v_w_up),
                                                   name="adamw_up")],
        "w_down": _sum_adamw(r_down, w_down, m_w_down, v_w_down, name="adamw_down"),
    }

    total = _unpack(_sum_rows(all_partials, name="sum_small_grads"), [full_shapes[k] for k in names])
    grads = dict(zip(names, total))
    grads["conv_dw_w"] = lax.dynamic_slice_in_dim(grads["conv_dw_w"], me * cw_shard, cw_shard, axis=2)
    grads["ffn_dw_w"] = lax.dynamic_slice_in_dim(grads["ffn_dw_w"], me * fw_shard, fw_shard, axis=2)
    weights = dict(norm1_g=norm1_g, q_norm_g=q_norm_g, k_norm_g=k_norm_g, conv_dw_w=conv_dw_w, conv_dw_b=conv_dw_b,
                   conv_ln_g=conv_ln_g, conv_ln_b=conv_ln_b, norm2_g=norm2_g, ffn_dw_w=ffn_dw_w, ffn_dw_b=ffn_dw_b)
    m_in = dict(norm1_g=m_norm1_g, q_norm_g=m_q_norm_g, k_norm_g=m_k_norm_g, conv_dw_w=m_conv_dw_w,
                conv_dw_b=m_conv_dw_b, conv_ln_g=m_conv_ln_g, conv_ln_b=m_conv_ln_b, norm2_g=m_norm2_g,
                ffn_dw_w=m_ffn_dw_w, ffn_dw_b=m_ffn_dw_b)
    v_in = dict(norm1_g=v_norm1_g, q_norm_g=v_q_norm_g, k_norm_g=v_k_norm_g, conv_dw_w=v_conv_dw_w,
                conv_dw_b=v_conv_dw_b, conv_ln_g=v_conv_ln_g, conv_ln_b=v_conv_ln_b, norm2_g=v_norm2_g,
                ffn_dw_w=v_ffn_dw_w, ffn_dw_b=v_ffn_dw_b)
    shard_shapes = [weights[k].shape for k in names]
    d_s, m_s, v_s = _adamw_rows(_pack([weights[k] for k in names]), _pack([grads[k] for k in names]),
                                _pack([m_in[k] for k in names]), _pack([v_in[k] for k in names]), name="adamw_small")
    delta = dict(zip(names, _unpack(d_s, shard_shapes)))
    new_m = dict(zip(names, _unpack(m_s, shard_shapes)))
    new_v = dict(zip(names, _unpack(v_s, shard_shapes)))
    for k, (g, d, mn, vn) in big.items():
        grads[k], delta[k], new_m[k], new_v[k] = g, d, mn, vn

    order = ["norm1_g", "w_in", "q_norm_g", "k_norm_g", "conv_dw_w", "conv_dw_b", "conv_ln_g", "conv_ln_b", "w_out",
             "norm2_g", "w_up", "ffn_dw_w", "ffn_dw_b", "w_down"]
    return (loss, grad_x, *[grads[k] for k in order], *[delta[k] for k in order], *[new_m[k] for k in order],
            *[new_v[k] for k in order])
```

```python
import functools

import jax
import jax.numpy as jnp
from jax import lax
from jax.experimental import pallas as pl
from jax.experimental.pallas import tpu as pltpu

F32 = jnp.float32
BF16 = jnp.bfloat16

N_DEV = 8
HEADS = 8
HEAD_DIM = 64
ATTN_WIDTH = HEADS * HEAD_DIM
CONV_KERNEL = 31
FFN_KERNEL = 3
EPS = 1e-6
BLK = 128
KEY_GROUP = 4
LANES = 128
NORM_ROWS = 128
CONV_HALO = 32
FFN_HALO = 16

ADAM_LR = 0.001
ADAM_B1 = 0.9
ADAM_B2 = 0.999
ADAM_EPS = 1e-08
ADAM_WD = 0.01
ADAM_STEP = 10

VMEM_LIMIT = 56 * 1024 * 1024


def _params(n_axes=0):
    kw = dict(vmem_limit_bytes=VMEM_LIMIT)
    if n_axes:
        kw["dimension_semantics"] = ("arbitrary",) * n_axes
    return pltpu.CompilerParams(**kw)


def _dot(a, b):
    return jnp.dot(a, b, preferred_element_type=F32)


def _dot_nt(a, b):
    return lax.dot_general(a, b, (((1,), (1,)), ((), ())), preferred_element_type=F32)


def _dot_tn(a, b):
    return lax.dot_general(a, b, (((0,), (0,)), ((), ())), preferred_element_type=F32)


def _sigmoid(x):
    return 1.0 / (1.0 + jnp.exp(-x))


def _split_bf16(x):
    hi = x.astype(BF16)
    lo = (x - hi.astype(F32)).astype(BF16)
    return hi, lo


def _pick(n, options):
    for t in options:
        if n % t == 0:
            return t
    return n


def _tile(n, cap):
    best = None
    for t in range(LANES, min(n, cap) + 1, LANES):
        if n % t == 0:
            best = t
    return best or n


def _mm_rms(x, g, wt, out_dtype, sends, name):
    m, k = x.shape
    n = wt.shape[0]
    tm = _tile(m, 2048)
    tn = _tile(n, 512)
    n_send = len(sends)
    grid = (m // tm, n // tn)

    def body(x_ref, g_ref, w_ref, *rest):
        send_refs, (o_ref, h_ref), rest = rest[:n_send], rest[n_send:n_send + 2], rest[n_send + 2:]
        got_refs, h_s, sems = rest[:n_send], rest[n_send], rest[n_send + 1:]
        if n_send:
            start, relay, finish = _two_level_gather(send_refs, got_refs, sems)
            is_first = (pl.program_id(0) == 0) & (pl.program_id(1) == 0)
            is_last = (pl.program_id(0) == grid[0] - 1) & (pl.program_id(1) == grid[1] - 1)
            pl.when(is_first)(start)
            pl.when(is_last)(relay)

        @pl.when(pl.program_id(1) == 0)
        def _():
            def chunk(c, _):
                rows = pl.ds(pl.multiple_of(c * NORM_ROWS, NORM_ROWS), NORM_ROWS)
                xv = x_ref[rows, :]
                r = lax.rsqrt(jnp.mean(xv * xv, axis=-1, keepdims=True) + EPS)
                hv = (xv * r * g_ref[...]).astype(BF16)
                h_s[rows, :] = hv
                h_ref[rows, :] = hv
                return 0

            lax.fori_loop(0, tm // NORM_ROWS, chunk, 0)

        o_ref[...] = _dot_nt(h_s[...], w_ref[...]).astype(out_dtype)
        if n_send:
            pl.when(is_last)(finish)

    out = pl.pallas_call(
        body, name=name, grid=grid,
        in_specs=[pl.BlockSpec((tm, k), lambda i, j: (i, 0)),
                  pl.BlockSpec((1, k), lambda i, j: (0, 0)),
                  pl.BlockSpec((tn, k), lambda i, j: (j, 0))] + [ANY_SPEC] * n_send,
        out_specs=[pl.BlockSpec((tm, tn), lambda i, j: (i, j)),
                   pl.BlockSpec((tm, k), lambda i, j: (i, 0))] + [ANY_SPEC] * n_send,
        out_shape=[jax.ShapeDtypeStruct((m, n), out_dtype), jax.ShapeDtypeStruct((m, k), BF16)]
        + _gathered_shapes(sends),
        scratch_shapes=[pltpu.VMEM((tm, k), BF16)] + (_exchange_sems(n_send) if n_send else []),
        compiler_params=_params(2),
    )(x, g, wt, *sends)
    return out[0], out[1], out[2:]


def _mm_res(a, w, res, name):
    m, k = a.shape
    n = w.shape[1]
    tm = _tile(m, 1024)
    tn = _tile(n, 512)

    def body(a_ref, w_ref, r_ref, o_ref):
        o_ref[...] = r_ref[...] + _dot(a_ref[...], w_ref[...])

    return pl.pallas_call(
        body, name=name, grid=(m // tm, n // tn),
        in_specs=[pl.BlockSpec((tm, k), lambda i, j: (i, 0)),
                  pl.BlockSpec((k, tn), lambda i, j: (0, j)),
                  pl.BlockSpec((tm, tn), lambda i, j: (i, j))],
        out_specs=pl.BlockSpec((tm, tn), lambda i, j: (i, j)),
        out_shape=jax.ShapeDtypeStruct((m, n), F32),
        compiler_params=_params(2),
    )(a, w, res)


def _mm_nt(a, w, out_dtype, name):
    m, k = a.shape
    n = w.shape[0]
    tm = _tile(m, 1024)
    tn = _tile(n, 1408)

    def body(a_ref, w_ref, o_ref):
        o_ref[...] = _dot_nt(a_ref[...], w_ref[...]).astype(out_dtype)

    return pl.pallas_call(
        body, name=name, grid=(m // tm, n // tn),
        in_specs=[pl.BlockSpec((tm, k), lambda i, j: (i, 0)),
                  pl.BlockSpec((tn, k), lambda i, j: (j, 0))],
        out_specs=pl.BlockSpec((tm, tn), lambda i, j: (i, j)),
        out_shape=jax.ShapeDtypeStruct((m, n), out_dtype),
        compiler_params=_params(2),
    )(a, w)


def _column_tiles(a, cap):
    if a.ndim == 2:
        s, c = a.shape
        tc = _tile(c, cap)
        return s, c, tc, lambda rows, index: pl.BlockSpec((rows, tc), lambda *g: (index(*g)[0], index(*g)[1]))
    slabs, s, width = a.shape
    tc = _tile(width, cap)
    per = width // tc
    return s, slabs * width, tc, lambda rows, index: pl.BlockSpec(
        (None, rows, tc), lambda *g: (index(*g)[1] // per, index(*g)[0], index(*g)[1] % per))


def _mm_tn(a, b, name):
    s, m, tm, a_spec = _column_tiles(a, 1408)
    n = b.shape[1]
    tn = _tile(n, 1024)

    def body(a_ref, b_ref, o_ref):
        o_ref[...] = _dot_tn(a_ref[...], b_ref[...]).astype(BF16)

    return pl.pallas_call(
        body, name=name, grid=(m // tm, n // tn),
        in_specs=[a_spec(s, lambda i, j: (0, i)),
                  pl.BlockSpec((s, tn), lambda i, j: (0, j))],
        out_specs=pl.BlockSpec((tm, tn), lambda i, j: (i, j)),
        out_shape=jax.ShapeDtypeStruct((m, n), BF16),
        compiler_params=_params(2),
    )(a, b)


def _mm_rmsbwd(a, w, x, g, dres, sends, name):
    m, k, tk, a_spec = _column_tiles(a, 1408)
    n = w.shape[1]
    tm = _tile(m, 1024)
    nk = k // tk
    n_send = len(sends)

    def body(a_ref, w_ref, x_ref, g_ref, r_ref, *rest):
        send_refs, (dx_ref, dxb_ref, dg_ref), rest = rest[:n_send], rest[n_send:n_send + 3], rest[n_send + 3:]
        got_refs, acc, sems = rest[:n_send], rest[n_send], rest[n_send + 1:]
        i, kk = pl.program_id(0), pl.program_id(1)
        if n_send:
            copies = _scatter_copies(send_refs, got_refs, sems)

            @pl.when((i == 0) & (kk == 0))
            def _():
                for cp in copies:
                    cp.start()

        part = _dot(a_ref[...], w_ref[...])

        @pl.when(kk == 0)
        def _():
            acc[...] = part

        @pl.when(kk > 0)
        def _():
            acc[...] += part

        @pl.when(kk == nk - 1)
        def _():
            def chunk(c, dgp):
                rows = pl.ds(pl.multiple_of(c * NORM_ROWS, NORM_ROWS), NORM_ROWS)
                dh = acc[rows, :]
                xv = x_ref[rows, :]
                r = lax.rsqrt(jnp.mean(xv * xv, axis=-1, keepdims=True) + EPS)
                xh = xv * r
                dxh = dh * g_ref[...]
                dx = r_ref[rows, :] + r * (dxh - xh * jnp.mean(dxh * xh, axis=-1, keepdims=True))
                dx_ref[rows, :] = dx
                dxb_ref[rows, :] = dx.astype(BF16)
                return dgp + jnp.sum(dh * xh, axis=0, keepdims=True)

            dgp = lax.fori_loop(0, tm // NORM_ROWS, chunk, jnp.zeros((1, n), F32))

            @pl.when(i == 0)
            def _():
                dg_ref[...] = dgp

            @pl.when(i > 0)
            def _():
                dg_ref[...] += dgp

        if n_send:
            @pl.when((i == m // tm - 1) & (kk == nk - 1))
            def _():
                for cp in copies:
                    cp.wait()

    out = pl.pallas_call(
        body, name=name, grid=(m // tm, nk),
        in_specs=[a_spec(tm, lambda i, kk: (i, kk)),
                  pl.BlockSpec((tk, n), lambda i, kk: (kk, 0)),
                  pl.BlockSpec((tm, n), lambda i, kk: (i, 0)),
                  pl.BlockSpec((1, n), lambda i, kk: (0, 0)),
                  pl.BlockSpec((tm, n), lambda i, kk: (i, 0))] + [ANY_SPEC] * n_send,
        out_specs=[pl.BlockSpec((tm, n), lambda i, kk: (i, 0)),
                   pl.BlockSpec((tm, n), lambda i, kk: (i, 0)),
                   pl.BlockSpec((1, n), lambda i, kk: (0, 0))] + [ANY_SPEC] * n_send,
        out_shape=[jax.ShapeDtypeStruct((m, n), F32), jax.ShapeDtypeStruct((m, n), BF16),
                   jax.ShapeDtypeStruct((1, n), F32)] + _scattered_shapes(sends),
        scratch_shapes=[pltpu.VMEM((tm, n), F32)] + (_exchange_sems(n_send) if n_send else []),
        compiler_params=_params(2),
    )(a, w, x, g, dres, *sends)
    return out[0], out[1], out[2], out[3:]


ANY_SPEC = pl.BlockSpec(memory_space=pl.ANY)
SEMS_PER_OPERAND = N_DEV - 1


def _exchange_sems(n):
    return [pltpu.SemaphoreType.DMA((n, SEMS_PER_OPERAND)), pltpu.SemaphoreType.DMA((n, SEMS_PER_OPERAND)),
            pltpu.SemaphoreType.DMA((n,))]


def _gathered_shapes(parts):
    return [jax.ShapeDtypeStruct((N_DEV,) + a.shape, a.dtype) for a in parts]


def _scattered_shapes(parts):
    return [jax.ShapeDtypeStruct(a.shape, a.dtype) for a in parts]


def _flat(pos):
    return 4 * pos[0] + 2 * pos[1] + pos[2]


def _remote(src, dst, sems, i, k, to):
    send_sems, recv_sems, _ = sems
    return pltpu.make_async_remote_copy(src_ref=src, dst_ref=dst, send_sem=send_sems.at[i, k],
                                        recv_sem=recv_sems.at[i, k], device_id=to,
                                        device_id_type=pl.DeviceIdType.MESH)


def _scatter_copies(ins, outs, sems):
    x, y, c = lax.axis_index("x"), lax.axis_index("y"), lax.axis_index("c")
    me = _flat((x, y, c))
    copies = [pltpu.make_async_copy(ins[i].at[me], outs[i].at[me], sems[2].at[i]) for i in range(len(ins))]
    for d in range(1, N_DEV):
        peer = (1 - x if d & 4 else x, 1 - y if d & 2 else y, 1 - c if d & 1 else c)
        for i in range(len(ins)):
            copies.append(_remote(ins[i].at[_flat(peer)], outs[i].at[me], sems, i, d - 1, peer))
    return copies


def _two_level_gather(ins, outs, sems):
    x, y, c = lax.axis_index("x"), lax.axis_index("y"), lax.axis_index("c")
    me, sibling = (x, y, c), (x, y, 1 - c)
    chips = [(1 - x, y), (x, 1 - y), (1 - x, 1 - y)]
    n = len(ins)

    def block(i, pos):
        return outs[i].at[_flat(pos)]

    local = [pltpu.make_async_copy(ins[i], block(i, me), sems[2].at[i]) for i in range(n)]
    own = [_remote(ins[i], block(i, me), sems, i, 0, sibling) for i in range(n)]
    own += [_remote(ins[i], block(i, me), sems, i, 1 + j, (*chip, c)) for i in range(n) for j, chip in enumerate(chips)]
    passed = [[_remote(block(i, (*chip, c)), block(i, (*chip, c)), sems, i, 4 + j, sibling) for i in range(n)]
              for j, chip in enumerate(chips)]

    def first():
        for cp in local + own:
            cp.start()

    def relay():
        for j, chip in enumerate(chips):
            for i in range(n):
                _remote(ins[i], block(i, (*chip, c)), sems, i, 1 + j, me).wait_recv()
                passed[j][i].start()

    def finish():
        for i in range(n):
            _remote(ins[i], block(i, sibling), sems, i, 0, me).wait_recv()
            for j, chip in enumerate(chips):
                _remote(ins[i], block(i, (*chip, 1 - c)), sems, i, 4 + j, me).wait_recv()
        for cp in own + [cp for row in passed for cp in row]:
            cp.wait_send()
        for cp in local:
            cp.wait()

    return first, relay, finish


def _gather(parts, name):
    n = len(parts)

    def body(*refs):
        start, relay, finish = _two_level_gather(refs[:n], refs[n:2 * n], refs[2 * n:])
        start()
        relay()
        finish()

    return pl.pallas_call(
        body, name=name, in_specs=[ANY_SPEC] * n, out_specs=[ANY_SPEC] * n,
        out_shape=_gathered_shapes(parts), scratch_shapes=_exchange_sems(n),
    )(*parts)


def _tri(kind):
    j = lax.broadcasted_iota(jnp.int32, (BLK, BLK), 0)
    s = lax.broadcasted_iota(jnp.int32, (BLK, BLK), 1)
    m = {"after": j > s, "upto": j <= s, "before": j < s}[kind]
    return jnp.concatenate([jnp.where(m, 1.0, 0.0), jnp.ones((BLK, BLK), F32)], axis=1).astype(BF16)


def _scan_rows(v, tri):
    r = _dot(v.astype(BF16), tri)
    return r[:, :BLK], r[:, BLK:]


HEADS_PER_STEP = LANES // HEAD_DIM


def _first_head_lanes():
    return lax.broadcasted_iota(jnp.int32, (1, LANES), 1) < HEAD_DIM


def _pair_mean(v, first):
    m0 = jnp.sum(jnp.where(first, v, 0.0), axis=-1, keepdims=True)
    m1 = jnp.sum(jnp.where(first, 0.0, v), axis=-1, keepdims=True)
    return jnp.where(first, m0, m1) * (1.0 / HEAD_DIM)


def _pair_norm(v, g2, first):
    return v * lax.rsqrt(_pair_mean(v * v, first) + EPS) * g2


def _pair_norm_bwd(raw, g2, dn, first):
    r = lax.rsqrt(_pair_mean(raw * raw, first) + EPS)
    xh = raw * r
    dxh = dn * g2
    return r * (dxh - xh * _pair_mean(dxh * xh, first)), jnp.sum(dn * xh, axis=0, keepdims=True)


def _block_diag(v, first):
    zero = jnp.zeros_like(v)
    return jnp.concatenate([jnp.where(first, v, zero), jnp.where(first, zero, v)], axis=0)


def _attn_prep(q_ref, k_ref, v_ref, qg_ref, kg_ref, qc_s, kc_s, vd_s, kd_s, n_blk):
    scale = HEAD_DIM ** -0.5
    first = _first_head_lanes()

    def prep(i, _):
        rows = pl.ds(pl.multiple_of(i * BLK, BLK), BLK)
        both = pl.ds(pl.multiple_of(i * 2 * BLK, 2 * BLK), 2 * BLK)
        qh, ql = _split_bf16(_pair_norm(q_ref[rows, :], qg_ref[...], first) * scale)
        kh, kl = _split_bf16(_pair_norm(k_ref[rows, :], kg_ref[...], first))
        for h in range(HEADS_PER_STEP):
            sl = slice(h * HEAD_DIM, (h + 1) * HEAD_DIM)
            qc_s[h, rows, :] = jnp.concatenate([qh[:, sl], ql[:, sl], qh[:, sl], ql[:, sl]], axis=1)
            kc_s[h, rows, :] = jnp.concatenate([kh[:, sl], kh[:, sl], kl[:, sl], kl[:, sl]], axis=1)
        vd_s[both, :] = _block_diag(v_ref[rows, :].astype(BF16), first)
        if kd_s is not None:
            kd_s[both, :] = _block_diag(kh, first)
        return 0

    lax.fori_loop(0, n_blk, prep, 0)


def _pair_scores(qc, kc_ref, grp):
    zs = []
    for j in range(0, KEY_GROUP, 2):
        two = pl.ds(pl.multiple_of((grp * KEY_GROUP + j) * BLK, 2 * BLK), 2 * BLK)
        z = _dot_nt(qc, kc_ref[two, :])
        zs += [z[:, :BLK], z[:, BLK:]]
    return zs


def _col_minus_row():
    row = lax.broadcasted_iota(jnp.int32, (BLK, BLK), 0)
    col = lax.broadcasted_iota(jnp.int32, (BLK, BLK), 1)
    return col - row


def _softplus(z):
    return jnp.maximum(z, 0.0) + jnp.log(1.0 + jnp.exp(-jnp.abs(z)))


def _attn_fwd(proj, qg, kg, sends, name):
    s = proj.shape[0]
    n_blk = s // BLK
    pairs = ATTN_WIDTH // LANES
    n_send = len(sends)

    def body(q_ref, k_ref, v_ref, qg_ref, kg_ref, *rest):
        send_refs, (o_ref, t_ref), rest = rest[:n_send], rest[n_send:n_send + 2], rest[n_send + 2:]
        got_refs, (qc_s, kc_s, vd_s), sems = rest[:n_send], rest[n_send:n_send + 3], rest[n_send + 3:]
        start, relay, finish = _two_level_gather(send_refs, got_refs, sems)
        step = pl.program_id(0)
        pl.when(step == 0)(start)
        pl.when(step == pairs - 1)(relay)
        tri = _tri("after")
        diff = _col_minus_row()
        first = _first_head_lanes()
        heads = range(HEADS_PER_STEP)
        _attn_prep(q_ref, k_ref, v_ref, qg_ref, kg_ref, qc_s, kc_s, vd_s, None, n_blk)

        def group(qc, qi, grp, carry, acc, masked):
            blocks = [grp * KEY_GROUP + j for j in reversed(range(KEY_GROUP))]
            zs = [_pair_scores(qc[h], kc_s.at[h], grp)[::-1] for h in heads]
            keeps = [diff < (qi - kb) * BLK if masked else None for kb in blocks]
            parts = [[None] * KEY_GROUP for _ in heads]
            for h in heads:
                for j, z in enumerate(zs[h]):
                    sp = _softplus(z)
                    lom = -sp
                    if masked:
                        lom = jnp.where(keeps[j], lom, 0.0)
                    tail, tot = _scan_rows(lom, tri)
                    parts[h][j] = (z - sp + tail, tot)
            carry = list(carry)
            for j, kb in enumerate(blocks):
                ws = []
                for h in heads:
                    lw, tot = parts[h][j]
                    w = jnp.exp(lw + carry[h])
                    if masked:
                        w = jnp.where(keeps[j], w, 0.0)
                    ws.append(w.astype(BF16))
                    carry[h] = carry[h] + tot
                acc = acc + _dot(jnp.concatenate(ws, axis=1),
                                 vd_s[pl.ds(pl.multiple_of(kb * 2 * BLK, 2 * BLK), 2 * BLK), :])
            return tuple(carry), acc

        def q_block(qi, _):
            rows = pl.ds(pl.multiple_of(qi * BLK, BLK), BLK)
            qc = [qc_s[h, rows, :] for h in heads]
            top = qi // KEY_GROUP
            zero = jnp.zeros((BLK, BLK), F32)
            carry, acc = group(qc, qi, top, (zero,) * HEADS_PER_STEP, jnp.zeros((BLK, LANES), F32), True)
            carry, acc = lax.fori_loop(
                0, top, lambda t, c: group(qc, qi, top - 1 - t, c[0], c[1], False), (carry, acc))
            o_ref[rows, :] = acc.astype(BF16)
            t_ref[rows, :] = jnp.where(first, carry[0], carry[1])
            return 0

        lax.fori_loop(0, n_blk, q_block, 0)
        pl.when(step == pairs - 1)(finish)

    col = lambda off: pl.BlockSpec((s, LANES), lambda p: (0, off + p))
    vec = pl.BlockSpec((1, LANES), lambda p: (0, 0))
    out = pl.pallas_call(
        body, name=name, grid=(pairs,),
        in_specs=[col(0), col(pairs), col(2 * pairs), vec, vec] + [ANY_SPEC] * n_send,
        out_specs=[pl.BlockSpec((s, LANES), lambda p: (0, p))] * 2 + [ANY_SPEC] * n_send,
        out_shape=[jax.ShapeDtypeStruct((s, ATTN_WIDTH), BF16), jax.ShapeDtypeStruct((s, ATTN_WIDTH), F32)]
        + _gathered_shapes(sends),
        scratch_shapes=[pltpu.VMEM((HEADS_PER_STEP, s, 4 * HEAD_DIM), BF16)] * 2
        + [pltpu.VMEM((HEADS_PER_STEP * s, LANES), BF16)] + _exchange_sems(n_send),
        compiler_params=_params(1),
    )(proj, proj, proj, qg, kg, *sends)
    return out[0], out[1], out[2:]


def _attn_bwd(proj, dcat, tsum, qg, kg, sends, name):
    s = proj.shape[0]
    n_blk = s // BLK
    pairs = ATTN_WIDTH // LANES
    scale = HEAD_DIM ** -0.5
    n_send = len(sends)
    n_scratch = 7

    def body(q_ref, k_ref, v_ref, do_ref, t_ref, qg_ref, kg_ref, *rest):
        send_refs, rest = rest[:n_send], rest[n_send:]
        (dq_ref, dk_ref, dv_ref, dqg_ref, dkg_ref), rest = rest[:5], rest[5:]
        got_refs, scratch, sems = rest[:n_send], rest[n_send:n_send + n_scratch], rest[n_send + n_scratch:]
        qc_s, kc_s, vd_s, kd_s, qd_s, dob_s, dkv_s = scratch
        copies = _scatter_copies(send_refs, got_refs, sems)

        @pl.when(pl.program_id(0) == 0)
        def _():
            for cp in copies:
                cp.start()

        tri_p = _tri("upto")
        tri_h = _tri("before")
        diff = _col_minus_row()

        @pl.when(pl.program_id(0) == 0)
        def _():
            dqg_ref[...] = jnp.zeros_like(dqg_ref)
            dkg_ref[...] = jnp.zeros_like(dkg_ref)

        first = _first_head_lanes()
        heads = range(HEADS_PER_STEP)
        _attn_prep(q_ref, k_ref, v_ref, qg_ref, kg_ref, qc_s, kc_s, vd_s, kd_s, n_blk)

        def prep(i, _):
            rows = pl.ds(pl.multiple_of(i * BLK, BLK), BLK)
            both = pl.ds(pl.multiple_of(i * 2 * BLK, 2 * BLK), 2 * BLK)
            dob = do_ref[rows, :].astype(BF16)
            dob_s[rows, :] = dob
            none = jnp.zeros((BLK, HEAD_DIM), BF16)
            for h in heads:
                qd_s[h, both, :] = jnp.concatenate(
                    [jnp.concatenate([qc_s[h, rows, 0:HEAD_DIM], none], axis=1),
                     jnp.concatenate([none, dob[:, h * HEAD_DIM:(h + 1) * HEAD_DIM]], axis=1)], axis=0)
                dkv_s[h, rows, :] = jnp.zeros((BLK, LANES), F32)
            return 0

        lax.fori_loop(0, n_blk, prep, 0)

        def group(qc, qd, dob, tq, qi, grp, pc, hc, dq, masked):
            blocks = [grp * KEY_GROUP + j for j in range(KEY_GROUP)]
            cols_of = [pl.ds(pl.multiple_of(kb * BLK, BLK), BLK) for kb in blocks]
            both_of = [pl.ds(pl.multiple_of(kb * 2 * BLK, 2 * BLK), 2 * BLK) for kb in blocks]
            zs = [_pair_scores(qc[h], kc_s.at[h], grp) for h in heads]
            das =[_dot_nt(dob, vd_s[both, :]) for both in both_of]
            keeps = [diff < (qi - kb) * BLK if masked else None for kb in blocks]
            lbs = [[None] * KEY_GROUP for _ in heads]
            scans = [[None] * KEY_GROUP for _ in heads]
            for h in heads:
                for j, z in enumerate(zs[h]):
                    sp = _softplus(z)
                    lom = -sp
                    if masked:
                        lom = jnp.where(keeps[j], lom, 0.0)
                    lbs[h][j] = z - sp
                    scans[h][j] = _scan_rows(lom, tri_p)
            pc, hc = list(pc), list(hc)
            avs = [[None] * KEY_GROUP for _ in heads]
            gws = [[None] * KEY_GROUP for _ in heads]
            hscans = [[None] * KEY_GROUP for _ in heads]
            for h in heads:
                for j in range(KEY_GROUP):
                    p_in, p_tot = scans[h][j]
                    a = jnp.exp(lbs[h][j] + (tq[h] - pc[h] - p_in))
                    if masked:
                        a = jnp.where(keeps[j], a, 0.0)
                    pc[h] = pc[h] + p_tot
                    gw = das[j][:, h * BLK:(h + 1) * BLK] * a
                    avs[h][j] = a.astype(BF16)
                    gws[h][j] = gw
                    hscans[h][j] = _scan_rows(gw, tri_h)
            dzs = [[None] * KEY_GROUP for _ in heads]
            for h in heads:
                for j in range(KEY_GROUP):
                    h_in, g_tot = hscans[h][j]
                    gw = gws[h][j]
                    dz = gw - jnp.exp(lbs[h][j]) * (gw + hc[h] + h_in)
                    if masked:
                        dz = jnp.where(keeps[j], dz, 0.0)
                    hc[h] = hc[h] + g_tot
                    dzs[h][j] = dz.astype(BF16)
            for j, both in enumerate(both_of):
                dq = dq + _dot(jnp.concatenate([dzs[h][j] for h in heads], axis=1), kd_s[both, :])
            for h in heads:
                for j, cols in enumerate(cols_of):
                    dkv_s[h, cols, :] += _dot_tn(jnp.concatenate([dzs[h][j], avs[h][j]], axis=0), qd[h])
            return tuple(pc), tuple(hc), dq

        def q_block(qi, dqg):
            rows = pl.ds(pl.multiple_of(qi * BLK, BLK), BLK)
            both = pl.ds(pl.multiple_of(qi * 2 * BLK, 2 * BLK), 2 * BLK)
            qc = [qc_s[h, rows, :] for h in heads]
            qd = [qd_s[h, both, :] for h in heads]
            dob = dob_s[rows, :]
            tboth = t_ref[rows, :]
            tq = [jnp.concatenate([tboth[:, h * HEAD_DIM:(h + 1) * HEAD_DIM]] * 2, axis=1) for h in heads]
            zero = (jnp.zeros((BLK, BLK), F32),) * HEADS_PER_STEP
            top = qi // KEY_GROUP
            pc, hc, dq = lax.fori_loop(
                0, top, lambda grp, c: group(qc, qd, dob, tq, qi, grp, c[0], c[1], c[2], False),
                (zero, zero, jnp.zeros((BLK, LANES), F32)))
            _, _, dq = group(qc, qd, dob, tq, qi, top, pc, hc, dq, True)
            dq_raw, dg = _pair_norm_bwd(q_ref[rows, :], qg_ref[...], dq * scale, first)
            dq_ref[rows, :] = dq_raw.astype(BF16)
            return dqg + dg

        dqg = lax.fori_loop(0, n_blk, q_block, jnp.zeros((1, LANES), F32))

        def finish(i, dkg):
            rows = pl.ds(pl.multiple_of(i * BLK, BLK), BLK)
            dk = jnp.concatenate([dkv_s[h, rows, 0:HEAD_DIM] for h in heads], axis=1)
            dv = jnp.concatenate([dkv_s[h, rows, HEAD_DIM:2 * HEAD_DIM] for h in heads], axis=1)
            dk_raw, dg = _pair_norm_bwd(k_ref[rows, :], kg_ref[...], dk, first)
            dk_ref[rows, :] = dk_raw.astype(BF16)
            dv_ref[rows, :] = dv.astype(BF16)
            return dkg + dg

        dkg = lax.fori_loop(0, n_blk, finish, jnp.zeros((1, LANES), F32))
        dqg_ref[0:1, :] += dqg
        dkg_ref[0:1, :] += dkg

        @pl.when(pl.program_id(0) == pairs - 1)
        def _():
            for cp in copies:
                cp.wait()

    col = lambda off: pl.BlockSpec((s, LANES), lambda p: (0, off + p))
    vec = pl.BlockSpec((1, LANES), lambda p: (0, 0))
    small = pl.BlockSpec((8, LANES), lambda p: (0, 0))
    out = pl.pallas_call(
        body, name=name, grid=(pairs,),
        in_specs=[col(0), col(pairs), col(2 * pairs), col(0), col(0), vec, vec] + [ANY_SPEC] * n_send,
        out_specs=[col(0)] * 3 + [small] * 2 + [ANY_SPEC] * n_send,
        out_shape=[jax.ShapeDtypeStruct((s, ATTN_WIDTH), BF16)] * 3 + [jax.ShapeDtypeStruct((8, LANES), F32)] * 2
        + _scattered_shapes(sends),
        scratch_shapes=[pltpu.VMEM((HEADS_PER_STEP, s, 4 * HEAD_DIM), BF16)] * 2
        + [pltpu.VMEM((HEADS_PER_STEP * s, LANES), BF16)] * 2
        + [pltpu.VMEM((HEADS_PER_STEP, HEADS_PER_STEP * s, LANES), BF16), pltpu.VMEM((s, LANES), BF16),
           pltpu.VMEM((HEADS_PER_STEP, s, LANES), F32)] + _exchange_sems(n_send),
        compiler_params=_params(1),
    )(proj, proj, proj, dcat, tsum, qg, kg, *sends)
    return out[:5], out[5:]


CONV_ROWS = 128


def _shifted(window, shift, halo):
    if shift == 0:
        return window[halo:, :]
    return pltpu.roll(window, shift, 0)[halo:, :]


SUBLANES = 8


def _row_shifts(window, up):
    n = window.shape[0]
    return [window] + [pltpu.roll(window, n - b if up else b, 0) for b in range(1, SUBLANES)]


def _earlier(shifts, back, rows):
    a, b = divmod(back, SUBLANES)
    return shifts[b][CONV_HALO - SUBLANES * a:CONV_HALO - SUBLANES * a + rows, :]


def _later(shifts, ahead, rows):
    a, b = divmod(ahead, SUBLANES)
    return shifts[b][SUBLANES * a:SUBLANES * a + rows, :]


def _conv_taps(shifts, w_ref, rows):
    y = None
    for k in range(CONV_KERNEL):
        term = _earlier(shifts, CONV_KERNEL - 1 - k, rows) * w_ref[k:k + 1, :]
        y = term if y is None else y + term
    return y


def _conv_fwd(proj, w, b, lg, lb, name):
    s = proj.shape[0]
    cw = w.shape[1]
    rows = CONV_ROWS
    blk_a = (proj.shape[1] - 2 * cw) // cw

    def body(a_ref, g_ref, w_ref, b_ref, lg_ref, lb_ref, o_ref, y_ref, u_s):
        u_s[0:CONV_HALO, :] = jnp.zeros((CONV_HALO, cw), F32)

        def glu(i, _):
            r0 = pl.multiple_of(i * rows, rows)
            u_s[pl.ds(CONV_HALO + r0, rows), :] = a_ref[pl.ds(r0, rows), :] * _sigmoid(g_ref[pl.ds(r0, rows), :])
            return 0

        lax.fori_loop(0, s // rows, glu, 0)

        def chunk(i, _):
            r0 = pl.multiple_of(i * rows, rows)
            y = _conv_taps(_row_shifts(u_s[pl.ds(r0, CONV_HALO + rows), :], False), w_ref, rows) + b_ref[...]
            y_ref[pl.ds(r0, rows), :] = y
            yc = y - jnp.mean(y, axis=-1, keepdims=True)
            n = yc * lax.rsqrt(jnp.mean(yc * yc, axis=-1, keepdims=True) + EPS)
            ln = n * lg_ref[...] + lb_ref[...]
            o_ref[pl.ds(r0, rows), :] = (ln * _sigmoid(ln)).astype(BF16)
            return 0

        lax.fori_loop(0, s // rows, chunk, 0)

    vec = pl.BlockSpec((1, cw), lambda i: (0, 0))
    return pl.pallas_call(
        body, name=name, grid=(1,),
        in_specs=[pl.BlockSpec((s, cw), lambda i: (0, blk_a)), pl.BlockSpec((s, cw), lambda i: (0, blk_a + 1)),
                  pl.BlockSpec((CONV_KERNEL, cw), lambda i: (0, 0)), vec, vec, vec],
        out_specs=[pl.BlockSpec((s, cw), lambda i: (0, 0))] * 2,
        out_shape=[jax.ShapeDtypeStruct((s, cw), BF16), jax.ShapeDtypeStruct((s, cw), F32)],
        scratch_shapes=[pltpu.VMEM((CONV_HALO + s, cw), F32)],
        compiler_params=_params(1),
    )(proj, proj, w, b, lg, lb)


def _conv_bwd(proj, y, dcat, w, lg, lb, name):
    s = proj.shape[0]
    cw = w.shape[1]
    rows = CONV_ROWS
    blk_a = (proj.shape[1] - 2 * cw) // cw
    n_chunk = s // rows

    def body(a_ref, g_ref, y_ref, dc_ref, w_ref, lg_ref, lb_ref, o_ref, dw_ref, db_ref, dlg_ref, dlb_ref,
             u_s, dy_s, dw_s):
        u_s[0:CONV_HALO, :] = jnp.zeros((CONV_HALO, cw), F32)
        dy_s[pl.ds(s, CONV_HALO), :] = jnp.zeros((CONV_HALO, cw), F32)
        dw_s[...] = jnp.zeros_like(dw_s)

        def glu(i, _):
            r0 = pl.multiple_of(i * rows, rows)
            u_s[pl.ds(CONV_HALO + r0, rows), :] = a_ref[pl.ds(r0, rows), :] * _sigmoid(g_ref[pl.ds(r0, rows), :])
            return 0

        lax.fori_loop(0, n_chunk, glu, 0)

        def chunk(i, carry):
            db, dlg, dlb = carry
            r0 = pl.multiple_of(i * rows, rows)
            shifts = _row_shifts(u_s[pl.ds(r0, CONV_HALO + rows), :], False)
            y = y_ref[pl.ds(r0, rows), :]
            yc = y - jnp.mean(y, axis=-1, keepdims=True)
            r = lax.rsqrt(jnp.mean(yc * yc, axis=-1, keepdims=True) + EPS)
            n = yc * r
            ln = n * lg_ref[...] + lb_ref[...]
            sg = _sigmoid(ln)
            dln = dc_ref[pl.ds(r0, rows), :] * (sg * (1.0 + ln * (1.0 - sg)))
            dn = dln * lg_ref[...]
            dy = r * (dn - jnp.mean(dn, axis=-1, keepdims=True) - n * jnp.mean(dn * n, axis=-1, keepdims=True))
            dy_s[pl.ds(r0, rows), :] = dy
            for k in range(CONV_KERNEL):
                prod = _earlier(shifts, CONV_KERNEL - 1 - k, rows) * dy
                dw_s[k] += jnp.sum(prod.reshape(rows // 8, 8, cw), axis=0)
            return (db + jnp.sum(dy, axis=0, keepdims=True),
                    dlg + jnp.sum(dln * n, axis=0, keepdims=True),
                    dlb + jnp.sum(dln, axis=0, keepdims=True))

        zero = jnp.zeros((1, cw), F32)
        db, dlg, dlb = lax.fori_loop(0, n_chunk, chunk, (zero, zero, zero))
        db_ref[...] = db
        dlg_ref[...] = dlg
        dlb_ref[...] = dlb
        for k in range(CONV_KERNEL):
            dw_ref[k:k + 1, :] = jnp.sum(dw_s[k], axis=0, keepdims=True)

        def back(i, _):
            r0 = pl.multiple_of(i * rows, rows)
            shifts = _row_shifts(dy_s[pl.ds(r0, rows + CONV_HALO), :], True)
            du = None
            for k in range(CONV_KERNEL):
                term = _later(shifts, CONV_KERNEL - 1 - k, rows) * w_ref[k:k + 1, :]
                du = term if du is None else du + term
            av = a_ref[pl.ds(r0, rows), :]
            sg = _sigmoid(g_ref[pl.ds(r0, rows), :])
            o_ref[pl.ds(r0, rows), 0:cw] = (du * sg).astype(BF16)
            o_ref[pl.ds(r0, rows), cw:2 * cw] = (du * av * sg * (1.0 - sg)).astype(BF16)
            return 0

        lax.fori_loop(0, n_chunk, back, 0)

    vec = pl.BlockSpec((1, cw), lambda i: (0, 0))
    wspec = pl.BlockSpec((CONV_KERNEL, cw), lambda i: (0, 0))
    return pl.pallas_call(
        body, name=name, grid=(1,),
        in_specs=[pl.BlockSpec((s, cw), lambda i: (0, blk_a)), pl.BlockSpec((s, cw), lambda i: (0, blk_a + 1)),
                  pl.BlockSpec((s, cw), lambda i: (0, 0)), pl.BlockSpec((s, cw), lambda i: (0, 1)), wspec, vec, vec],
        out_specs=[pl.BlockSpec((s, 2 * cw), lambda i: (0, 0)), wspec, vec, vec, vec],
        out_shape=[jax.ShapeDtypeStruct((s, 2 * cw), BF16), jax.ShapeDtypeStruct((CONV_KERNEL, cw), F32)]
        + [jax.ShapeDtypeStruct((1, cw), F32)] * 3,
        scratch_shapes=[pltpu.VMEM((CONV_HALO + s, cw), F32), pltpu.VMEM((s + CONV_HALO, cw), F32),
                        pltpu.VMEM((CONV_KERNEL, 8, cw), F32)],
        compiler_params=_params(1),
    )(proj, proj, y, dcat, w, lg, lb)


FFN_ROWS = 256


def _ffn_gate(g_ref, r0, rows, w_ref, b_ref):
    cur = g_ref[pl.ds(r0, rows), :].astype(F32)
    prev = g_ref[pl.ds(pl.multiple_of(jnp.maximum(r0 - FFN_HALO, 0), FFN_HALO), FFN_HALO), :].astype(F32)
    prev = jnp.where(r0 > 0, prev, 0.0)
    window = jnp.concatenate([prev, cur], axis=0)
    gc = cur * w_ref[FFN_KERNEL - 1:FFN_KERNEL, :] + b_ref[...]
    for k in range(FFN_KERNEL - 1):
        gc = gc + _shifted(window, FFN_KERNEL - 1 - k, FFN_HALO) * w_ref[k:k + 1, :]
    return gc, window


def _ffn_fwd(up, w, b, name):
    s = up.shape[0]
    f = w.shape[1]
    tc = _pick(f, (256, 128))
    nc = f // tc
    rows = _pick(s, (FFN_ROWS, 128))

    def body(g_ref, v_ref, w_ref, b_ref, o_ref):
        def chunk(i, _):
            r0 = pl.multiple_of(i * rows, rows)
            gc, _w = _ffn_gate(g_ref, r0, rows, w_ref, b_ref)
            o_ref[pl.ds(r0, rows), :] = (gc * _sigmoid(gc) * v_ref[pl.ds(r0, rows), :].astype(F32)).astype(BF16)
            return 0

        lax.fori_loop(0, s // rows, chunk, 0)

    return pl.pallas_call(
        body, name=name, grid=(nc,),
        in_specs=[pl.BlockSpec((s, tc), lambda j: (0, j)), pl.BlockSpec((s, tc), lambda j: (0, nc + j)),
                  pl.BlockSpec((FFN_KERNEL, tc), lambda j: (0, j)), pl.BlockSpec((1, tc), lambda j: (0, j))],
        out_specs=pl.BlockSpec((s, tc), lambda j: (0, j)),
        out_shape=jax.ShapeDtypeStruct((s, f), BF16),
        compiler_params=_params(1),
    )(up, up, w, b)


def _ffn_bwd(up, dact, w, b, name):
    s = up.shape[0]
    f = w.shape[1]
    tc = _pick(f, (256, 128))
    nc = f // tc
    rows = _pick(s, (FFN_ROWS, 128))
    n_chunk = s // rows

    def body(g_ref, v_ref, da_ref, w_ref, b_ref, d_ref, dw_ref, db_ref, dgc_s):
        dg_ref, dv_ref = d_ref.at[0], d_ref.at[1]
        dgc_s[pl.ds(s, FFN_HALO), :] = jnp.zeros((FFN_HALO, tc), F32)

        def chunk(i, carry):
            r0 = pl.multiple_of(i * rows, rows)
            gc, window = _ffn_gate(g_ref, r0, rows, w_ref, b_ref)
            sg = _sigmoid(gc)
            da = da_ref[pl.ds(r0, rows), :].astype(F32)
            dv_ref[pl.ds(r0, rows), :] = (da * gc * sg).astype(BF16)
            dgc = da * v_ref[pl.ds(r0, rows), :].astype(F32) * (sg * (1.0 + gc * (1.0 - sg)))
            dgc_s[pl.ds(r0, rows), :] = dgc
            out = [carry[0] + jnp.sum(dgc, axis=0, keepdims=True)]
            for k in range(FFN_KERNEL):
                out.append(carry[1 + k] + jnp.sum(_shifted(window, FFN_KERNEL - 1 - k, FFN_HALO) * dgc,
                                                  axis=0, keepdims=True))
            return tuple(out)

        zero = jnp.zeros((1, tc), F32)
        sums = lax.fori_loop(0, n_chunk, chunk, (zero,) * (1 + FFN_KERNEL))
        db_ref[...] = sums[0]
        for k in range(FFN_KERNEL):
            dw_ref[k:k + 1, :] = sums[1 + k]

        def back(i, _):
            r0 = pl.multiple_of(i * rows, rows)
            window = dgc_s[pl.ds(r0, rows + FFN_HALO), :]
            dg = window[:rows, :] * w_ref[FFN_KERNEL - 1:FFN_KERNEL, :]
            for k in range(FFN_KERNEL - 1):
                sh = FFN_KERNEL - 1 - k
                dg = dg + pltpu.roll(window, rows + FFN_HALO - sh, 0)[:rows, :] * w_ref[k:k + 1, :]
            dg_ref[pl.ds(r0, rows), :] = dg.astype(BF16)
            return 0

        lax.fori_loop(0, n_chunk, back, 0)

    blk = lambda off: pl.BlockSpec((s, tc), lambda j: (0, off + j))
    return pl.pallas_call(
        body, name=name, grid=(nc,),
        in_specs=[blk(0), blk(nc), blk(0), pl.BlockSpec((FFN_KERNEL, tc), lambda j: (0, j)),
                  pl.BlockSpec((1, tc), lambda j: (0, j))],
        out_specs=[pl.BlockSpec((2, s, tc), lambda j: (0, 0, j)), pl.BlockSpec((FFN_KERNEL, tc), lambda j: (0, j)),
                   pl.BlockSpec((1, tc), lambda j: (0, j))],
        out_shape=[jax.ShapeDtypeStruct((2, s, f), BF16),
                   jax.ShapeDtypeStruct((FFN_KERNEL, f), F32), jax.ShapeDtypeStruct((1, f), F32)],
        scratch_shapes=[pltpu.VMEM((s + FFN_HALO, tc), F32)],
        compiler_params=_params(1),
    )(up, up, dact, w, b)


def _loss_head(y, target, name):
    m, n = y.shape
    tm = _pick(m, (256, 128))

    def body(y_ref, t_ref, l_ref, d_ref, db_ref):
        e = y_ref[...] - t_ref[...]
        part = 0.5 * jnp.sum(jnp.sum(e * e, axis=-1, keepdims=True) / n, axis=0, keepdims=True)

        @pl.when(pl.program_id(0) == 0)
        def _():
            l_ref[...] = jnp.zeros_like(l_ref)

        l_ref[...] += part
        d = e / n
        d_ref[...] = d
        db_ref[...] = d.astype(BF16)

    return pl.pallas_call(
        body, name=name, grid=(m // tm,),
        in_specs=[pl.BlockSpec((tm, n), lambda i: (i, 0))] * 2,
        out_specs=[pl.BlockSpec((8, LANES), lambda i: (0, 0)), pl.BlockSpec((tm, n), lambda i: (i, 0)),
                   pl.BlockSpec((tm, n), lambda i: (i, 0))],
        out_shape=[jax.ShapeDtypeStruct((8, LANES), F32), jax.ShapeDtypeStruct((m, n), F32),
                   jax.ShapeDtypeStruct((m, n), BF16)],
        compiler_params=_params(1),
    )(y, target)


def _adamw_math(w, g, m, v):
    m = ADAM_B1 * m + (1.0 - ADAM_B1) * g
    v = ADAM_B2 * v + (1.0 - ADAM_B2) * (g * g)
    m_hat = m / (1.0 - ADAM_B1 ** ADAM_STEP)
    v_hat = v / (1.0 - ADAM_B2 ** ADAM_STEP)
    delta = -ADAM_LR * (m_hat / (jnp.sqrt(v_hat) + ADAM_EPS) + ADAM_WD * w)
    return delta, m, v


def _sum_adamw(parts, w, m, v, name):
    depth, r, c = w.shape
    tr = max(t for t in range(16, min(r, 192) + 1, 16) if r % t == 0)
    steps = r // tr

    def body(*refs):
        p_refs = refs[:depth]
        w_ref, m_ref, v_ref, g_out, d_out, m_out, v_out = refs[depth:]
        for layer in range(depth):
            @pl.when(pl.program_id(0) == layer)
            def _():
                g = p_refs[layer][0].astype(F32)
                for src in range(1, N_DEV):
                    g = g + p_refs[layer][src].astype(F32)
                d, mn, vn = _adamw_math(w_ref[0], g, m_ref[0], v_ref[0])
                g_out[0] = g
                d_out[0] = d
                m_out[0] = mn
                v_out[0] = vn

    def part_spec(layer):
        return pl.BlockSpec((N_DEV, tr, c), lambda l, i: (0, jnp.clip((l - layer) * steps + i, 0, steps - 1), 0))

    blk = pl.BlockSpec((1, tr, c), lambda l, i: (l, i, 0))
    return pl.pallas_call(
        body, name=name, grid=(depth, steps),
        in_specs=[part_spec(layer) for layer in range(depth)] + [blk, blk, blk],
        out_specs=[blk] * 4,
        out_shape=[jax.ShapeDtypeStruct(w.shape, F32)] * 4,
        compiler_params=_params(2),
    )(*parts, w, m, v)


def _sum_rows(parts, name):
    _, r, c = parts.shape

    def body(p_ref, o_ref):
        g = p_ref[0]
        for src in range(1, N_DEV):
            g = g + p_ref[src]
        o_ref[...] = g

    return pl.pallas_call(
        body, name=name, grid=(1,),
        in_specs=[pl.BlockSpec((N_DEV, r, c), lambda i: (0, 0, 0))],
        out_specs=pl.BlockSpec((r, c), lambda i: (0, 0)),
        out_shape=jax.ShapeDtypeStruct((r, c), F32),
        compiler_params=_params(1),
    )(parts)


def _adamw_rows(w, g, m, v, name):
    r, c = w.shape

    def body(w_ref, g_ref, m_ref, v_ref, d_out, m_out, v_out):
        d, mn, vn = _adamw_math(w_ref[...], g_ref[...], m_ref[...], v_ref[...])
        d_out[...] = d
        m_out[...] = mn
        v_out[...] = vn

    blk = pl.BlockSpec((r, c), lambda i: (0, 0))
    return pl.pallas_call(
        body, name=name, grid=(1,), in_specs=[blk] * 4, out_specs=[blk] * 3,
        out_shape=[jax.ShapeDtypeStruct((r, c), F32)] * 3,
        compiler_params=_params(1),
    )(w, g, m, v)


PACK_TILE = 8 * LANES


def _pack(arrays):
    rows = []
    for a in arrays:
        flat = a.reshape(-1).astype(F32)
        pad = (-flat.shape[0]) % PACK_TILE
        rows.append(jnp.pad(flat, (0, pad)).reshape(-1, LANES))
    return jnp.concatenate(rows, axis=0)


def _unpack(packed, shapes, lead=()):
    out, r0 = [], 0
    for shp in shapes:
        size = 1
        for d in shp:
            size *= d
        nrows = -(-size // PACK_TILE) * 8
        piece = packed[..., r0:r0 + nrows, :].reshape(lead + (nrows * LANES,))[..., :size]
        out.append(piece.reshape(lead + tuple(shp)))
        r0 += nrows
    return out


def kernel(x, norm1_g, w_in, q_norm_g, k_norm_g, conv_dw_w, conv_dw_b, conv_ln_g, conv_ln_b, w_out, norm2_g, w_up, ffn_dw_w, ffn_dw_b, w_down, loss_target, m_norm1_g, m_w_in, m_q_norm_g, m_k_norm_g, m_conv_dw_w, m_conv_dw_b, m_conv_ln_g, m_conv_ln_b, m_w_out, m_norm2_g, m_w_up, m_ffn_dw_w, m_ffn_dw_b, m_w_down, v_norm1_g, v_w_in, v_q_norm_g, v_k_norm_g, v_conv_dw_w, v_conv_dw_b, v_conv_ln_g, v_conv_ln_b, v_w_out, v_norm2_g, v_w_up, v_ffn_dw_w, v_ffn_dw_b, v_w_down):
    depth, d_model, in_shard = w_in.shape
    out_shard = w_out.shape[1]
    up_shard = w_up.shape[2]
    down_shard = w_down.shape[1]
    d_ff = down_shard * N_DEV
    conv_w = conv_dw_b.shape[1]
    cw_shard = conv_dw_w.shape[2]
    fw_shard = ffn_dw_w.shape[2]
    me = 4 * lax.axis_index("x") + 2 * lax.axis_index("y") + lax.axis_index("c")

    transposed = lambda a: a.transpose(0, 2, 1)
    b_in, b_out, b_up, b_down = (transposed(w_in).astype(BF16), w_out.astype(BF16), transposed(w_up).astype(BF16),
                                 w_down.astype(BF16))
    rows_major = lambda g: g.reshape(N_DEV * g.shape[1], g.shape[2])
    g_in0, g_small = _gather([b_in[0], _pack([conv_dw_w, ffn_dw_w])], name="gather_first")
    wf_in, wf_out, wf_up, wf_down = [rows_major(g_in0)] + [None] * (depth - 1), [None] * depth, [None] * depth, [None] * depth
    g_cw, g_fw = _unpack(g_small, [conv_dw_w.shape, ffn_dw_w.shape], lead=(N_DEV,))
    cwf = g_cw.transpose(1, 2, 0, 3).reshape(depth, CONV_KERNEL, conv_w)
    fwf = g_fw.transpose(1, 2, 0, 3).reshape(depth, FFN_KERNEL, d_ff)

    row = lambda a, l: a[l].reshape(1, -1)
    both_heads = lambda a, l: jnp.tile(row(a, l), (1, HEADS_PER_STEP))

    xs = x[0]
    saved = []
    for l in range(depth):
        proj, h1, _ = _mm_rms(xs, row(norm1_g, l), wf_in[l], F32, [], name="fwd_in")
        sends = [b_out[l], b_up[l]] + ([b_in[l + 1]] if l + 1 < depth else [])
        attn, tsum, got = _attn_fwd(proj, both_heads(q_norm_g, l), both_heads(k_norm_g, l), sends, name="fwd_attn")
        wf_out[l], wf_up[l] = rows_major(got[0]), rows_major(got[1])
        if l + 1 < depth:
            wf_in[l + 1] = rows_major(got[2])
        conv, conv_y = _conv_fwd(proj, cwf[l], row(conv_dw_b, l), row(conv_ln_g, l), row(conv_ln_b, l),
                                 name="fwd_conv")
        cat = jnp.concatenate([attn, conv], axis=1)
        x_mid = _mm_res(cat, wf_out[l], xs, name="fwd_out")
        up, h2, got = _mm_rms(x_mid, row(norm2_g, l), wf_up[l], BF16, [b_down[l]], name="fwd_up")
        wf_down[l] = rows_major(got[0])
        act = _ffn_fwd(up, fwf[l], row(ffn_dw_b, l), name="fwd_ffn")
        x_next = _mm_res(act, wf_down[l], x_mid, name="fwd_down")
        saved.append((xs, h1, proj, tsum, cat, x_mid, h2, up, act, conv_y))
        xs = x_next

    loss_tile, dx, dxb = _loss_head(xs, loss_target[0], name="loss_head")
    loss = lax.psum(loss_tile[0, 0], ("x", "y", "c"))

    r_in, r_out, r_up, r_down = [None] * depth, [None] * depth, [None] * depth, [None] * depth
    row_blocks = lambda g: g.reshape(N_DEV, g.shape[0] // N_DEV, g.shape[1])
    small = {k: [None] * depth for k in ("norm1_g", "q_norm_g", "k_norm_g", "conv_dw_w", "conv_dw_b", "conv_ln_g",
                                         "conv_ln_b", "norm2_g", "ffn_dw_w", "ffn_dw_b")}
    gw_in = None
    for l in reversed(range(depth)):
        xs, h1, proj, tsum, cat, x_mid, h2, up, act, conv_y = saved[l]
        dact = _mm_nt(dxb, wf_down[l], BF16, name="bwd_dact")
        gw_down = _mm_tn(act, dxb, name="bwd_gw_down")
        dup, small["ffn_dw_w"][l], small["ffn_dw_b"][l] = _ffn_bwd(up, dact, fwf[l], row(ffn_dw_b, l), name="bwd_ffn")
        gw_up = _mm_tn(dup, h2, name="bwd_gw_up")
        dx, dxb, small["norm2_g"][l], _ = _mm_rmsbwd(dup, wf_up[l], x_mid, row(norm2_g, l), dx, [], name="bwd_up")
        dcat = _mm_nt(dxb, wf_out[l], F32, name="bwd_dcat")
        gw_out = _mm_tn(cat, dxb, name="bwd_gw_out")
        dglu, small["conv_dw_w"][l], small["conv_dw_b"][l], small["conv_ln_g"][l], small["conv_ln_b"][l] = _conv_bwd(
            proj, conv_y, dcat, cwf[l], row(conv_ln_g, l), row(conv_ln_b, l), name="bwd_conv")
        sends = [row_blocks(gw_down), row_blocks(gw_up), row_blocks(gw_out)] + ([row_blocks(gw_in)] if l + 1 < depth else [])
        (dq, dk, dv, dqg, dkg), got = _attn_bwd(proj, dcat, tsum, both_heads(q_norm_g, l), both_heads(k_norm_g, l),
                                                sends, name="bwd_attn")
        r_down[l], r_up[l], r_out[l] = got[:3]
        if l + 1 < depth:
            r_in[l + 1] = got[3]
        small["q_norm_g"][l] = dqg[0:1, :HEAD_DIM] + dqg[0:1, HEAD_DIM:]
        small["k_norm_g"][l] = dkg[0:1, :HEAD_DIM] + dkg[0:1, HEAD_DIM:]
        dproj = jnp.concatenate([dq, dk, dv, dglu], axis=1)
        gw_in = _mm_tn(dproj, h1, name="bwd_gw_in")
        sends = [row_blocks(gw_in)] if l == 0 else []
        dx, dxb, small["norm1_g"][l], got = _mm_rmsbwd(dproj, wf_in[l], xs, row(norm1_g, l), dx, sends, name="bwd_in")
        if l == 0:
            r_in[0] = got[0]
    grad_x = dx[None]

    names = ["norm1_g", "q_norm_g", "k_norm_g", "conv_dw_w", "conv_dw_b", "conv_ln_g", "conv_ln_b", "norm2_g",
             "ffn_dw_w", "ffn_dw_b"]
    full_shapes = {"norm1_g": norm1_g.shape, "q_norm_g": q_norm_g.shape, "k_norm_g": k_norm_g.shape,
                   "conv_dw_w": (depth, CONV_KERNEL, conv_w), "conv_dw_b": conv_dw_b.shape,
                   "conv_ln_g": conv_ln_g.shape, "conv_ln_b": conv_ln_b.shape, "norm2_g": norm2_g.shape,
                   "ffn_dw_w": (depth, FFN_KERNEL, d_ff), "ffn_dw_b": ffn_dw_b.shape}
    partial = _pack([jnp.stack(small[k]).reshape(full_shapes[k]) for k in names])
    (all_partials,) = _gather([partial], name="gather_small_grads")
    big = {
        "w_in": [transposed(a) for a in _sum_adamw(r_in, transposed(w_in), transposed(m_w_in), transposed(v_w_in),
                                                   name="adamw_in")],
        "w_out": _sum_adamw(r_out, w_out, m_w_out, v_w_out, name="adamw_out"),
        "w_up": [transposed(a) for a in _sum_adamw(r_up, transposed(w_up), transposed(m_w_up), transposed(v_w_up),
                                                   name="adamw_up")],
        "w_down": _sum_adamw(r_down, w_down, m_w_down, v_w_down, name="adamw_down"),
    }

    total = _unpack(_sum_rows(all_partials, name="sum_small_grads"), [full_shapes[k] for k in names])
    grads = dict(zip(names, total))
    grads["conv_dw_w"] = lax.dynamic_slice_in_dim(grads["conv_dw_w"], me * cw_shard, cw_shard, axis=2)
    grads["ffn_dw_w"] = lax.dynamic_slice_in_dim(grads["ffn_dw_w"], me * fw_shard, fw_shard, axis=2)
    weights = dict(norm1_g=norm1_g, q_norm_g=q_norm_g, k_norm_g=k_norm_g, conv_dw_w=conv_dw_w, conv_dw_b=conv_dw_b,
                   conv_ln_g=conv_ln_g, conv_ln_b=conv_ln_b, norm2_g=norm2_g, ffn_dw_w=ffn_dw_w, ffn_dw_b=ffn_dw_b)
    m_in = dict(norm1_g=m_norm1_g, q_norm_g=m_q_norm_g, k_norm_g=m_k_norm_g, conv_dw_w=m_conv_dw_w,
                conv_dw_b=m_conv_dw_b, conv_ln_g=m_conv_ln_g, conv_ln_b=m_conv_ln_b, norm2_g=m_norm2_g,
                ffn_dw_w=m_ffn_dw_w, ffn_dw_b=m_ffn_dw_b)
    v_in = dict(norm1_g=v_norm1_g, q_norm_g=v_q_norm_g, k_norm_g=v_k_norm_g, conv_dw_w=v_conv_dw_w,
                conv_dw_b=v_conv_dw_b, conv_ln_g=v_conv_ln_g, conv_ln_b=v_conv_ln_b, norm2_g=v_norm2_g,
                ffn_dw_w=v_ffn_dw_w, ffn_dw_b=v_ffn_dw_b)
    shard_shapes = [weights[k].shape for k in names]
    d_s, m_s, v_s = _adamw_rows(_pack([weights[k] for k in names]), _pack([grads[k] for k in names]),
                                _pack([m_in[k] for k in names]), _pack([v_in[k] for k in names]), name="adamw_small")
    delta = dict(zip(names, _unpack(d_s, shard_shapes)))
    new_m = dict(zip(names, _unpack(m_s, shard_shapes)))
    new_v = dict(zip(names, _unpack(v_s, shard_shapes)))
    for k, (g, d, mn, vn) in big.items():
        grads[k], delta[k], new_m[k], new_v[k] = g, d, mn, vn

    order = ["norm1_g", "w_in", "q_norm_g", "k_norm_g", "conv_dw_w", "conv_dw_b", "conv_ln_g", "conv_ln_b", "w_out",
             "norm2_g", "w_up", "ffn_dw_w", "ffn_dw_b", "w_down"]
    return (loss, grad_x, *[grads[k] for k in order], *[delta[k] for k in order], *[new_m[k] for k in order],
            *[new_v[k] for k in order])
```

```python
import functools

import jax
import jax.numpy as jnp
from jax import lax
from jax.experimental import pallas as pl
from jax.experimental.pallas import tpu as pltpu

F32 = jnp.float32
BF16 = jnp.bfloat16

N_DEV = 8
HEADS = 8
HEAD_DIM = 64
ATTN_WIDTH = HEADS * HEAD_DIM
CONV_KERNEL = 31
FFN_KERNEL = 3
EPS = 1e-6
BLK = 128
KEY_GROUP = 4
LANES = 128
NORM_ROWS = 128
CONV_HALO = 32
FFN_HALO = 16

ADAM_LR = 0.001
ADAM_B1 = 0.9
ADAM_B2 = 0.999
ADAM_EPS = 1e-08
ADAM_WD = 0.01
ADAM_STEP = 10

VMEM_LIMIT = 56 * 1024 * 1024


def _params(n_axes=0):
    kw = dict(vmem_limit_bytes=VMEM_LIMIT)
    if n_axes:
        kw["dimension_semantics"] = ("arbitrary",) * n_axes
    return pltpu.CompilerParams(**kw)


def _dot(a, b):
    return jnp.dot(a, b, preferred_element_type=F32)


def _dot_nt(a, b):
    return lax.dot_general(a, b, (((1,), (1,)), ((), ())), preferred_element_type=F32)


def _dot_tn(a, b):
    return lax.dot_general(a, b, (((0,), (0,)), ((), ())), preferred_element_type=F32)


def _sigmoid(x):
    return 1.0 / (1.0 + jnp.exp(-x))


def _split_bf16(x):
    hi = x.astype(BF16)
    lo = (x - hi.astype(F32)).astype(BF16)
    return hi, lo


def _pick(n, options):
    for t in options:
        if n % t == 0:
            return t
    return n


def _tile(n, cap):
    best = None
    for t in range(LANES, min(n, cap) + 1, LANES):
        if n % t == 0:
            best = t
    return best or n


def _mm_rms(x, g, wt, out_dtype, sends, name):
    m, k = x.shape
    n = wt.shape[0]
    tm = _tile(m, 2048)
    tn = _tile(n, 512)
    n_send = len(sends)
    grid = (m // tm, n // tn)

    def body(x_ref, g_ref, w_ref, *rest):
        send_refs, (o_ref, h_ref), rest = rest[:n_send], rest[n_send:n_send + 2], rest[n_send + 2:]
        got_refs, h_s, sems = rest[:n_send], rest[n_send], rest[n_send + 1:]
        if n_send:
            start, relay, finish = _two_level_gather(send_refs, got_refs, sems)
            is_first = (pl.program_id(0) == 0) & (pl.program_id(1) == 0)
            is_last = (pl.program_id(0) == grid[0] - 1) & (pl.program_id(1) == grid[1] - 1)
            pl.when(is_first)(start)
            pl.when(is_last)(relay)

        @pl.when(pl.program_id(1) == 0)
        def _():
            def chunk(c, _):
                rows = pl.ds(pl.multiple_of(c * NORM_ROWS, NORM_ROWS), NORM_ROWS)
                xv = x_ref[rows, :]
                r = lax.rsqrt(jnp.mean(xv * xv, axis=-1, keepdims=True) + EPS)
                hv = (xv * r * g_ref[...]).astype(BF16)
                h_s[rows, :] = hv
                h_ref[rows, :] = hv
                return 0

            lax.fori_loop(0, tm // NORM_ROWS, chunk, 0)

        o_ref[...] = _dot_nt(h_s[...], w_ref[...]).astype(out_dtype)
        if n_send:
            pl.when(is_last)(finish)

    out = pl.pallas_call(
        body, name=name, grid=grid,
        in_specs=[pl.BlockSpec((tm, k), lambda i, j: (i, 0)),
                  pl.BlockSpec((1, k), lambda i, j: (0, 0)),
                  pl.BlockSpec((tn, k), lambda i, j: (j, 0))] + [ANY_SPEC] * n_send,
        out_specs=[pl.BlockSpec((tm, tn), lambda i, j: (i, j)),
                   pl.BlockSpec((tm, k), lambda i, j: (i, 0))] + [ANY_SPEC] * n_send,
        out_shape=[jax.ShapeDtypeStruct((m, n), out_dtype), jax.ShapeDtypeStruct((m, k), BF16)]
        + _gathered_shapes(sends),
        scratch_shapes=[pltpu.VMEM((tm, k), BF16)] + (_exchange_sems(n_send) if n_send else []),
        compiler_params=_params(2),
    )(x, g, wt, *sends)
    return out[0], out[1], out[2:]


def _mm_res(a, w, res, name):
    m, k = a.shape
    n = w.shape[1]
    tm = _tile(m, 1024)
    tn = _tile(n, 512)

    def body(a_ref, w_ref, r_ref, o_ref):
        o_ref[...] = r_ref[...] + _dot(a_ref[...], w_ref[...])

    return pl.pallas_call(
        body, name=name, grid=(m // tm, n // tn),
        in_specs=[pl.BlockSpec((tm, k), lambda i, j: (i, 0)),
                  pl.BlockSpec((k, tn), lambda i, j: (0, j)),
                  pl.BlockSpec((tm, tn), lambda i, j: (i, j))],
        out_specs=pl.BlockSpec((tm, tn), lambda i, j: (i, j)),
        out_shape=jax.ShapeDtypeStruct((m, n), F32),
        compiler_params=_params(2),
    )(a, w, res)


def _mm_nt(a, w, out_dtype, name):
    m, k = a.shape
    n = w.shape[0]
    tm = _tile(m, 1024)
    tn = _tile(n, 1408)

    def body(a_ref, w_ref, o_ref):
        o_ref[...] = _dot_nt(a_ref[...], w_ref[...]).astype(out_dtype)

    return pl.pallas_call(
        body, name=name, grid=(m // tm, n // tn),
        in_specs=[pl.BlockSpec((tm, k), lambda i, j: (i, 0)),
                  pl.BlockSpec((tn, k), lambda i, j: (j, 0))],
        out_specs=pl.BlockSpec((tm, tn), lambda i, j: (i, j)),
        out_shape=jax.ShapeDtypeStruct((m, n), out_dtype),
        compiler_params=_params(2),
    )(a, w)


def _column_tiles(a, cap):
    if a.ndim == 2:
        s, c = a.shape
        tc = _tile(c, cap)
        return s, c, tc, lambda rows, index: pl.BlockSpec((rows, tc), lambda *g: (index(*g)[0], index(*g)[1]))
    slabs, s, width = a.shape
    tc = _tile(width, cap)
    per = width // tc
    return s, slabs * width, tc, lambda rows, index: pl.BlockSpec(
        (None, rows, tc), lambda *g: (index(*g)[1] // per, index(*g)[0], index(*g)[1] % per))


def _mm_tn(a, b, name):
    s, m, tm, a_spec = _column_tiles(a, 1408)
    n = b.shape[1]
    tn = _tile(n, 1024)

    def body(a_ref, b_ref, o_ref):
        o_ref[...] = _dot_tn(a_ref[...], b_ref[...]).astype(BF16)

    return pl.pallas_call(
        body, name=name, grid=(m // tm, n // tn),
        in_specs=[a_spec(s, lambda i, j: (0, i)),
                  pl.BlockSpec((s, tn), lambda i, j: (0, j))],
        out_specs=pl.BlockSpec((tm, tn), lambda i, j: (i, j)),
        out_shape=jax.ShapeDtypeStruct((m, n), BF16),
        compiler_params=_params(2),
    )(a, b)


def _mm_rmsbwd(a, w, x, g, dres, sends, name):
    m, k, tk, a_spec = _column_tiles(a, 1408)
    n = w.shape[1]
    tm = _tile(m, 1024)
    nk = k // tk
    n_send = len(sends)

    def body(a_ref, w_ref, x_ref, g_ref, r_ref, *rest):
        send_refs, (dx_ref, dxb_ref, dg_ref), rest = rest[:n_send], rest[n_send:n_send + 3], rest[n_send + 3:]
        got_refs, acc, sems = rest[:n_send], rest[n_send], rest[n_send + 1:]
        i, kk = pl.program_id(0), pl.program_id(1)
        if n_send:
            copies = _scatter_copies(send_refs, got_refs, sems)

            @pl.when((i == 0) & (kk == 0))
            def _():
                for cp in copies:
                    cp.start()

        part = _dot(a_ref[...], w_ref[...])

        @pl.when(kk == 0)
        def _():
            acc[...] = part

        @pl.when(kk > 0)
        def _():
            acc[...] += part

        @pl.when(kk == nk - 1)
        def _():
            def chunk(c, dgp):
                rows = pl.ds(pl.multiple_of(c * NORM_ROWS, NORM_ROWS), NORM_ROWS)
                dh = acc[rows, :]
                xv = x_ref[rows, :]
                r = lax.rsqrt(jnp.mean(xv * xv, axis=-1, keepdims=True) + EPS)
                xh = xv * r
                dxh = dh * g_ref[...]
                dx = r_ref[rows, :] + r * (dxh - xh * jnp.mean(dxh * xh, axis=-1, keepdims=True))
                dx_ref[rows, :] = dx
                dxb_ref[rows, :] = dx.astype(BF16)
                return dgp + jnp.sum(dh * xh, axis=0, keepdims=True)

            dgp = lax.fori_loop(0, tm // NORM_ROWS, chunk, jnp.zeros((1, n), F32))

            @pl.when(i == 0)
            def _():
                dg_ref[...] = dgp

            @pl.when(i > 0)
            def _():
                dg_ref[...] += dgp

        if n_send:
            @pl.when((i == m // tm - 1) & (kk == nk - 1))
            def _():
                for cp in copies:
                    cp.wait()

    out = pl.pallas_call(
        body, name=name, grid=(m // tm, nk),
        in_specs=[a_spec(tm, lambda i, kk: (i, kk)),
                  pl.BlockSpec((tk, n), lambda i, kk: (kk, 0)),
                  pl.BlockSpec((tm, n), lambda i, kk: (i, 0)),
                  pl.BlockSpec((1, n), lambda i, kk: (0, 0)),
                  pl.BlockSpec((tm, n), lambda i, kk: (i, 0))] + [ANY_SPEC] * n_send,
        out_specs=[pl.BlockSpec((tm, n), lambda i, kk: (i, 0)),
                   pl.BlockSpec((tm, n), lambda i, kk: (i, 0)),
                   pl.BlockSpec((1, n), lambda i, kk: (0, 0))] + [ANY_SPEC] * n_send,
        out_shape=[jax.ShapeDtypeStruct((m, n), F32), jax.ShapeDtypeStruct((m, n), BF16),
                   jax.ShapeDtypeStruct((1, n), F32)] + _scattered_shapes(sends),
        scratch_shapes=[pltpu.VMEM((tm, n), F32)] + (_exchange_sems(n_send) if n_send else []),
        compiler_params=_params(2),
    )(a, w, x, g, dres, *sends)
    return out[0], out[1], out[2], out[3:]


ANY_SPEC = pl.BlockSpec(memory_space=pl.ANY)
SEMS_PER_OPERAND = N_DEV - 1


def _exchange_sems(n):
    return [pltpu.SemaphoreType.DMA((n, SEMS_PER_OPERAND)), pltpu.SemaphoreType.DMA((n, SEMS_PER_OPERAND)),
            pltpu.SemaphoreType.DMA((n,))]


def _gathered_shapes(parts):
    return [jax.ShapeDtypeStruct((N_DEV,) + a.shape, a.dtype) for a in parts]


def _scattered_shapes(parts):
    return [jax.ShapeDtypeStruct(a.shape, a.dtype) for a in parts]


def _flat(pos):
    return 4 * pos[0] + 2 * pos[1] + pos[2]


def _remote(src, dst, sems, i, k, to):
    send_sems, recv_sems, _ = sems
    return pltpu.make_async_remote_copy(src_ref=src, dst_ref=dst, send_sem=send_sems.at[i, k],
                                        recv_sem=recv_sems.at[i, k], device_id=to,
                                        device_id_type=pl.DeviceIdType.MESH)


def _scatter_copies(ins, outs, sems):
    x, y, c = lax.axis_index("x"), lax.axis_index("y"), lax.axis_index("c")
    me = _flat((x, y, c))
    copies = [pltpu.make_async_copy(ins[i].at[me], outs[i].at[me], sems[2].at[i]) for i in range(len(ins))]
    for d in range(1, N_DEV):
        peer = (1 - x if d & 4 else x, 1 - y if d & 2 else y, 1 - c if d & 1 else c)
        for i in range(len(ins)):
            copies.append(_remote(ins[i].at[_flat(peer)], outs[i].at[me], sems, i, d - 1, peer))
    return copies


def _two_level_gather(ins, outs, sems):
    x, y, c = lax.axis_index("x"), lax.axis_index("y"), lax.axis_index("c")
    me, sibling = (x, y, c), (x, y, 1 - c)
    chips = [(1 - x, y), (x, 1 - y), (1 - x, 1 - y)]
    n = len(ins)

    def block(i, pos):
        return outs[i].at[_flat(pos)]

    local = [pltpu.make_async_copy(ins[i], block(i, me), sems[2].at[i]) for i in range(n)]
    own = [_remote(ins[i], block(i, me), sems, i, 0, sibling) for i in range(n)]
    own += [_remote(ins[i], block(i, me), sems, i, 1 + j, (*chip, c)) for i in range(n) for j, chip in enumerate(chips)]
    passed = [[_remote(block(i, (*chip, c)), block(i, (*chip, c)), sems, i, 4 + j, sibling) for i in range(n)]
              for j, chip in enumerate(chips)]

    def first():
        for cp in local + own:
            cp.start()

    def relay():
        for j, chip in enumerate(chips):
            for i in range(n):
                _remote(ins[i], block(i, (*chip, c)), sems, i, 1 + j, me).wait_recv()
                passed[j][i].start()

    def finish():
        for i in range(n):
            _remote(ins[i], block(i, sibling), sems, i, 0, me).wait_recv()
            for j, chip in enumerate(chips):
                _remote(ins[i], block(i, (*chip, 1 - c)), sems, i, 4 + j, me).wait_recv()
        for cp in own + [cp for row in passed for cp in row]:
            cp.wait_send()
        for cp in local:
            cp.wait()

    return first, relay, finish


def _gather(parts, name):
    n = len(parts)

    def body(*refs):
        start, relay, finish = _two_level_gather(refs[:n], refs[n:2 * n], refs[2 * n:])
        start()
        relay()
        finish()

    return pl.pallas_call(
        body, name=name, in_specs=[ANY_SPEC] * n, out_specs=[ANY_SPEC] * n,
        out_shape=_gathered_shapes(parts), scratch_shapes=_exchange_sems(n),
    )(*parts)


def _tri(kind):
    j = lax.broadcasted_iota(jnp.int32, (BLK, BLK), 0)
    s = lax.broadcasted_iota(jnp.int32, (BLK, BLK), 1)
    m = {"after": j > s, "upto": j <= s, "before": j < s}[kind]
    return jnp.concatenate([jnp.where(m, 1.0, 0.0), jnp.ones((BLK, BLK), F32)], axis=1).astype(BF16)


def _scan_rows(v, tri):
    r = _dot(v.astype(BF16), tri)
    return r[:, :BLK], r[:, BLK:]


HEADS_PER_STEP = LANES // HEAD_DIM


def _first_head_lanes():
    return lax.broadcasted_iota(jnp.int32, (1, LANES), 1) < HEAD_DIM


def _pair_mean(v, first):
    m0 = jnp.sum(jnp.where(first, v, 0.0), axis=-1, keepdims=True)
    m1 = jnp.sum(jnp.where(first, 0.0, v), axis=-1, keepdims=True)
    return jnp.where(first, m0, m1) * (1.0 / HEAD_DIM)


def _pair_norm(v, g2, first):
    return v * lax.rsqrt(_pair_mean(v * v, first) + EPS) * g2


def _pair_norm_bwd(raw, g2, dn, first):
    r = lax.rsqrt(_pair_mean(raw * raw, first) + EPS)
    xh = raw * r
    dxh = dn * g2
    return r * (dxh - xh * _pair_mean(dxh * xh, first)), jnp.sum(dn * xh, axis=0, keepdims=True)


def _block_diag(v, first):
    zero = jnp.zeros_like(v)
    return jnp.concatenate([jnp.where(first, v, zero), jnp.where(first, zero, v)], axis=0)


def _attn_prep(q_ref, k_ref, v_ref, qg_ref, kg_ref, qc_s, kc_s, vd_s, kd_s, n_blk):
    scale = HEAD_DIM ** -0.5
    first = _first_head_lanes()

    def prep(i, _):
        rows = pl.ds(pl.multiple_of(i * BLK, BLK), BLK)
        both = pl.ds(pl.multiple_of(i * 2 * BLK, 2 * BLK), 2 * BLK)
        qh, ql = _split_bf16(_pair_norm(q_ref[rows, :], qg_ref[...], first) * scale)
        kh, kl = _split_bf16(_pair_norm(k_ref[rows, :], kg_ref[...], first))
        for h in range(HEADS_PER_STEP):
            sl = slice(h * HEAD_DIM, (h + 1) * HEAD_DIM)
            qc_s[h, rows, :] = jnp.concatenate([qh[:, sl], ql[:, sl], qh[:, sl], ql[:, sl]], axis=1)
            kc_s[h, rows, :] = jnp.concatenate([kh[:, sl], kh[:, sl], kl[:, sl], kl[:, sl]], axis=1)
        vd_s[both, :] = _block_diag(v_ref[rows, :].astype(BF16), first)
        if kd_s is not None:
            kd_s[both, :] = _block_diag(kh, first)
        return 0

    lax.fori_loop(0, n_blk, prep, 0)


def _pair_scores(qc, kc_ref, grp):
    zs = []
    for j in range(0, KEY_GROUP, 2):
        two = pl.ds(pl.multiple_of((grp * KEY_GROUP + j) * BLK, 2 * BLK), 2 * BLK)
        z = _dot_nt(qc, kc_ref[two, :])
        zs += [z[:, :BLK], z[:, BLK:]]
    return zs


def _col_minus_row():
    row = lax.broadcasted_iota(jnp.int32, (BLK, BLK), 0)
    col = lax.broadcasted_iota(jnp.int32, (BLK, BLK), 1)
    return col - row


def _softplus(z):
    return jnp.maximum(z, 0.0) + jnp.log(1.0 + jnp.exp(-jnp.abs(z)))


def _attn_fwd(proj, qg, kg, sends, name):
    s = proj.shape[0]
    n_blk = s // BLK
    pairs = ATTN_WIDTH // LANES
    n_send = len(sends)

    def body(q_ref, k_ref, v_ref, qg_ref, kg_ref, *rest):
        send_refs, (o_ref, t_ref), rest = rest[:n_send], rest[n_send:n_send + 2], rest[n_send + 2:]
        got_refs, (qc_s, kc_s, vd_s), sems = rest[:n_send], rest[n_send:n_send + 3], rest[n_send + 3:]
        start, relay, finish = _two_level_gather(send_refs, got_refs, sems)
        step = pl.program_id(0)
        pl.when(step == 0)(start)
        pl.when(step == pairs - 1)(relay)
        tri = _tri("after")
        diff = _col_minus_row()
        first = _first_head_lanes()
        heads = range(HEADS_PER_STEP)
        _attn_prep(q_ref, k_ref, v_ref, qg_ref, kg_ref, qc_s, kc_s, vd_s, None, n_blk)

        def group(qc, qi, grp, carry, acc, masked):
            blocks = [grp * KEY_GROUP + j for j in reversed(range(KEY_GROUP))]
            zs = [_pair_scores(qc[h], kc_s.at[h], grp)[::-1] for h in heads]
            keeps = [diff < (qi - kb) * BLK if masked else None for kb in blocks]
            parts = [[None] * KEY_GROUP for _ in heads]
            for h in heads:
                for j, z in enumerate(zs[h]):
                    sp = _softplus(z)
                    lom = -sp
                    if masked:
                        lom = jnp.where(keeps[j], lom, 0.0)
                    tail, tot = _scan_rows(lom, tri)
                    parts[h][j] = (z - sp + tail, tot)
            carry = list(carry)
            for j, kb in enumerate(blocks):
                ws = []
                for h in heads:
                    lw, tot = parts[h][j]
                    w = jnp.exp(lw + carry[h])
                    if masked:
                        w = jnp.where(keeps[j], w, 0.0)
                    ws.append(w.astype(BF16))
                    carry[h] = carry[h] + tot
                acc = acc + _dot(jnp.concatenate(ws, axis=1),
                                 vd_s[pl.ds(pl.multiple_of(kb * 2 * BLK, 2 * BLK), 2 * BLK), :])
            return tuple(carry), acc

        def q_block(qi, _):
            rows = pl.ds(pl.multiple_of(qi * BLK, BLK), BLK)
            qc = [qc_s[h, rows, :] for h in heads]
            top = qi // KEY_GROUP
            zero = jnp.zeros((BLK, BLK), F32)
            carry, acc = group(qc, qi, top, (zero,) * HEADS_PER_STEP, jnp.zeros((BLK, LANES), F32), True)
            carry, acc = lax.fori_loop(
                0, top, lambda t, c: group(qc, qi, top - 1 - t, c[0], c[1], False), (carry, acc))
            o_ref[rows, :] = acc.astype(BF16)
            t_ref[rows, :] = jnp.where(first, carry[0], carry[1])
            return 0

        lax.fori_loop(0, n_blk, q_block, 0)
        pl.when(step == pairs - 1)(finish)

    col = lambda off: pl.BlockSpec((s, LANES), lambda p: (0, off + p))
    vec = pl.BlockSpec((1, LANES), lambda p: (0, 0))
    out = pl.pallas_call(
        body, name=name, grid=(pairs,),
        in_specs=[col(0), col(pairs), col(2 * pairs), vec, vec] + [ANY_SPEC] * n_send,
        out_specs=[pl.BlockSpec((s, LANES), lambda p: (0, p))] * 2 + [ANY_SPEC] * n_send,
        out_shape=[jax.ShapeDtypeStruct((s, ATTN_WIDTH), BF16), jax.ShapeDtypeStruct((s, ATTN_WIDTH), F32)]
        + _gathered_shapes(sends),
        scratch_shapes=[pltpu.VMEM((HEADS_PER_STEP, s, 4 * HEAD_DIM), BF16)] * 2
        + [pltpu.VMEM((HEADS_PER_STEP * s, LANES), BF16)] + _exchange_sems(n_send),
        compiler_params=_params(1),
    )(proj, proj, proj, qg, kg, *sends)
    return out[0], out[1], out[2:]


def _attn_bwd(proj, dcat, tsum, qg, kg, sends, name):
    s = proj.shape[0]
    n_blk = s // BLK
    pairs = ATTN_WIDTH // LANES
    scale = HEAD_DIM ** -0.5
    n_send = len(sends)
    n_scratch = 7

    def body(q_ref, k_ref, v_ref, do_ref, t_ref, qg_ref, kg_ref, *rest):
        send_refs, rest = rest[:n_send], rest[n_send:]
        (dq_ref, dk_ref, dv_ref, dqg_ref, dkg_ref), rest = rest[:5], rest[5:]
        got_refs, scratch, sems = rest[:n_send], rest[n_send:n_send + n_scratch], rest[n_send + n_scratch:]
        qc_s, kc_s, vd_s, kd_s, qd_s, dob_s, dkv_s = scratch
        copies = _scatter_copies(send_refs, got_refs, sems)

        @pl.when(pl.program_id(0) == 0)
        def _():
            for cp in copies:
                cp.start()

        tri_p = _tri("upto")
        tri_h = _tri("before")
        diff = _col_minus_row()

        @pl.when(pl.program_id(0) == 0)
        def _():
            dqg_ref[...] = jnp.zeros_like(dqg_ref)
            dkg_ref[...] = jnp.zeros_like(dkg_ref)

        first = _first_head_lanes()
        heads = range(HEADS_PER_STEP)
        _attn_prep(q_ref, k_ref, v_ref, qg_ref, kg_ref, qc_s, kc_s, vd_s, kd_s, n_blk)

        def prep(i, _):
            rows = pl.ds(pl.multiple_of(i * BLK, BLK), BLK)
            both = pl.ds(pl.multiple_of(i * 2 * BLK, 2 * BLK), 2 * BLK)
            dob = do_ref[rows, :].astype(BF16)
            dob_s[rows, :] = dob
            none = jnp.zeros((BLK, HEAD_DIM), BF16)
            for h in heads:
                qd_s[h, both, :] = jnp.concatenate(
                    [jnp.concatenate([qc_s[h, rows, 0:HEAD_DIM], none], axis=1),
                     jnp.concatenate([none, dob[:, h * HEAD_DIM:(h + 1) * HEAD_DIM]], axis=1)], axis=0)
                dkv_s[h, rows, :] = jnp.zeros((BLK, LANES), F32)
            return 0

        lax.fori_loop(0, n_blk, prep, 0)

        def group(qc, qd, dob, tq, qi, grp, pc, hc, dq, masked):
            blocks = [grp * KEY_GROUP + j for j in range(KEY_GROUP)]
            cols_of = [pl.ds(pl.multiple_of(kb * BLK, BLK), BLK) for kb in blocks]
            both_of = [pl.ds(pl.multiple_of(kb * 2 * BLK, 2 * BLK), 2 * BLK) for kb in blocks]
            zs = [_pair_scores(qc[h], kc_s.at[h], grp) for h in heads]
            das =[_dot_nt(dob, vd_s[both, :]) for both in both_of]
            keeps = [diff < (qi - kb) * BLK if masked else None for kb in blocks]
            lbs = [[None] * KEY_GROUP for _ in heads]
            scans = [[None] * KEY_GROUP for _ in heads]
            for h in heads:
                for j, z in enumerate(zs[h]):
                    sp = _softplus(z)
                    lom = -sp
                    if masked:
                        lom = jnp.where(keeps[j], lom, 0.0)
                    lbs[h][j] = z - sp
                    scans[h][j] = _scan_rows(lom, tri_p)
            pc, hc = list(pc), list(hc)
            avs = [[None] * KEY_GROUP for _ in heads]
            gws = [[None] * KEY_GROUP for _ in heads]
            hscans = [[None] * KEY_GROUP for _ in heads]
            for h in heads:
                for j in range(KEY_GROUP):
                    p_in, p_tot = scans[h][j]
                    a = jnp.exp(lbs[h][j] + (tq[h] - pc[h] - p_in))
                    if masked:
                        a = jnp.where(keeps[j], a, 0.0)
                    pc[h] = pc[h] + p_tot
                    gw = das[j][:, h * BLK:(h + 1) * BLK] * a
                    avs[h][j] = a.astype(BF16)
                    gws[h][j] = gw
                    hscans[h][j] = _scan_rows(gw, tri_h)
            dzs = [[None] * KEY_GROUP for _ in heads]
            for h in heads:
                for j in range(KEY_GROUP):
                    h_in, g_tot = hscans[h][j]
                    gw = gws[h][j]
                    dz = gw - jnp.exp(lbs[h][j]) * (gw + hc[h] + h_in)
                    if masked:
                        dz = jnp.where(keeps[j], dz, 0.0)
                    hc[h] = hc[h] + g_tot
                    dzs[h][j] = dz.astype(BF16)
            for j, both in enumerate(both_of):
                dq = dq + _dot(jnp.concatenate([dzs[h][j] for h in heads], axis=1), kd_s[both, :])
            for h in heads:
                for j, cols in enumerate(cols_of):
                    dkv_s[h, cols, :] += _dot_tn(jnp.concatenate([dzs[h][j], avs[h][j]], axis=0), qd[h])
            return tuple(pc), tuple(hc), dq

        def q_block(qi, dqg):
            rows = pl.ds(pl.multiple_of(qi * BLK, BLK), BLK)
            both = pl.ds(pl.multiple_of(qi * 2 * BLK, 2 * BLK), 2 * BLK)
            qc = [qc_s[h, rows, :] for h in heads]
            qd = [qd_s[h, both, :] for h in heads]
            dob = dob_s[rows, :]
            tboth = t_ref[rows, :]
            tq = [jnp.concatenate([tboth[:, h * HEAD_DIM:(h + 1) * HEAD_DIM]] * 2, axis=1) for h in heads]
            zero = (jnp.zeros((BLK, BLK), F32),) * HEADS_PER_STEP
            top = qi // KEY_GROUP
            pc, hc, dq = lax.fori_loop(
                0, top, lambda grp, c: group(qc, qd, dob, tq, qi, grp, c[0], c[1], c[2], False),
                (zero, zero, jnp.zeros((BLK, LANES), F32)))
            _, _, dq = group(qc, qd, dob, tq, qi, top, pc, hc, dq, True)
            dq_raw, dg = _pair_norm_bwd(q_ref[rows, :], qg_ref[...], dq * scale, first)
            dq_ref[rows, :] = dq_raw.astype(BF16)
            return dqg + dg

        dqg = lax.fori_loop(0, n_blk, q_block, jnp.zeros((1, LANES), F32))

        def finish(i, dkg):
            rows = pl.ds(pl.multiple_of(i * BLK, BLK), BLK)
            dk = jnp.concatenate([dkv_s[h, rows, 0:HEAD_DIM] for h in heads], axis=1)
            dv = jnp.concatenate([dkv_s[h, rows, HEAD_DIM:2 * HEAD_DIM] for h in heads], axis=1)
            dk_raw, dg = _pair_norm_bwd(k_ref[rows, :], kg_ref[...], dk, first)
            dk_ref[rows, :] = dk_raw.astype(BF16)
            dv_ref[rows, :] = dv.astype(BF16)
            return dkg + dg

        dkg = lax.fori_loop(0, n_blk, finish, jnp.zeros((1, LANES), F32))
        dqg_ref[0:1, :] += dqg
        dkg_ref[0:1, :] += dkg

        @pl.when(pl.program_id(0) == pairs - 1)
        def _():
            for cp in copies:
                cp.wait()

    col = lambda off: pl.BlockSpec((s, LANES), lambda p: (0, off + p))
    vec = pl.BlockSpec((1, LANES), lambda p: (0, 0))
    small = pl.BlockSpec((8, LANES), lambda p: (0, 0))
    out = pl.pallas_call(
        body, name=name, grid=(pairs,),
        in_specs=[col(0), col(pairs), col(2 * pairs), col(0), col(0), vec, vec] + [ANY_SPEC] * n_send,
        out_specs=[col(0)] * 3 + [small] * 2 + [ANY_SPEC] * n_send,
        out_shape=[jax.ShapeDtypeStruct((s, ATTN_WIDTH), BF16)] * 3 + [jax.ShapeDtypeStruct((8, LANES), F32)] * 2
        + _scattered_shapes(sends),
        scratch_shapes=[pltpu.VMEM((HEADS_PER_STEP, s, 4 * HEAD_DIM), BF16)] * 2
        + [pltpu.VMEM((HEADS_PER_STEP * s, LANES), BF16)] * 2
        + [pltpu.VMEM((HEADS_PER_STEP, HEADS_PER_STEP * s, LANES), BF16), pltpu.VMEM((s, LANES), BF16),
           pltpu.VMEM((HEADS_PER_STEP, s, LANES), F32)] + _exchange_sems(n_send),
        compiler_params=_params(1),
    )(proj, proj, proj, dcat, tsum, qg, kg, *sends)
    return out[:5], out[5:]


CONV_ROWS = 128


def _shifted(window, shift, halo):
    if shift == 0:
        return window[halo:, :]
    return pltpu.roll(window, shift, 0)[halo:, :]


SUBLANES = 8


def _row_shifts(window, up):
    n = window.shape[0]
    return [window] + [pltpu.roll(window, n - b if up else b, 0) for b in range(1, SUBLANES)]


def _earlier(shifts, back, rows):
    a, b = divmod(back, SUBLANES)
    return shifts[b][CONV_HALO - SUBLANES * a:CONV_HALO - SUBLANES * a + rows, :]


def _later(shifts, ahead, rows):
    a, b = divmod(ahead, SUBLANES)
    return shifts[b][SUBLANES * a:SUBLANES * a + rows, :]


def _conv_taps(shifts, w_ref, rows):
    y = None
    for k in range(CONV_KERNEL):
        term = _earlier(shifts, CONV_KERNEL - 1 - k, rows) * w_ref[k:k + 1, :]
        y = term if y is None else y + term
    return y


def _conv_fwd(proj, w, b, lg, lb, name):
    s = proj.shape[0]
    cw = w.shape[1]
    rows = CONV_ROWS
    blk_a = (proj.shape[1] - 2 * cw) // cw

    def body(a_ref, g_ref, w_ref, b_ref, lg_ref, lb_ref, o_ref, y_ref, u_s):
        u_s[0:CONV_HALO, :] = jnp.zeros((CONV_HALO, cw), F32)

        def glu(i, _):
            r0 = pl.multiple_of(i * rows, rows)
            u_s[pl.ds(CONV_HALO + r0, rows), :] = a_ref[pl.ds(r0, rows), :] * _sigmoid(g_ref[pl.ds(r0, rows), :])
            return 0

        lax.fori_loop(0, s // rows, glu, 0)

        def chunk(i, _):
            r0 = pl.multiple_of(i * rows, rows)
            y = _conv_taps(_row_shifts(u_s[pl.ds(r0, CONV_HALO + rows), :], False), w_ref, rows) + b_ref[...]
            y_ref[pl.ds(r0, rows), :] = y
            yc = y - jnp.mean(y, axis=-1, keepdims=True)
            n = yc * lax.rsqrt(jnp.mean(yc * yc, axis=-1, keepdims=True) + EPS)
            ln = n * lg_ref[...] + lb_ref[...]
            o_ref[pl.ds(r0, rows), :] = (ln * _sigmoid(ln)).astype(BF16)
            return 0

        lax.fori_loop(0, s // rows, chunk, 0)

    vec = pl.BlockSpec((1, cw), lambda i: (0, 0))
    return pl.pallas_call(
        body, name=name, grid=(1,),
        in_specs=[pl.BlockSpec((s, cw), lambda i: (0, blk_a)), pl.BlockSpec((s, cw), lambda i: (0, blk_a + 1)),
                  pl.BlockSpec((CONV_KERNEL, cw), lambda i: (0, 0)), vec, vec, vec],
        out_specs=[pl.BlockSpec((s, cw), lambda i: (0, 0))] * 2,
        out_shape=[jax.ShapeDtypeStruct((s, cw), BF16), jax.ShapeDtypeStruct((s, cw), F32)],
        scratch_shapes=[pltpu.VMEM((CONV_HALO + s, cw), F32)],
        compiler_params=_params(1),
    )(proj, proj, w, b, lg, lb)


def _conv_bwd(proj, y, dcat, w, lg, lb, sends, name):
    s = proj.shape[0]
    cw = w.shape[1]
    rows = CONV_ROWS
    blk_a = (proj.shape[1] - 2 * cw) // cw
    n_chunk = s // rows
    n_send = len(sends)

    def body(a_ref, g_ref, y_ref, dc_ref, w_ref, lg_ref, lb_ref, *rest):
        send_refs, (o_ref, dw_ref, db_ref, dlg_ref, dlb_ref), rest = rest[:n_send], rest[n_send:n_send + 5], rest[n_send + 5:]
        got_refs, (u_s, dy_s, dw_s), sems = rest[:n_send], rest[n_send:n_send + 3], rest[n_send + 3:]
        copies = _scatter_copies(send_refs, got_refs, sems)
        for cp in copies:
            cp.start()
        u_s[0:CONV_HALO, :] = jnp.zeros((CONV_HALO, cw), F32)
        dy_s[pl.ds(s, CONV_HALO), :] = jnp.zeros((CONV_HALO, cw), F32)
        dw_s[...] = jnp.zeros_like(dw_s)

        def glu(i, _):
            r0 = pl.multiple_of(i * rows, rows)
            u_s[pl.ds(CONV_HALO + r0, rows), :] = a_ref[pl.ds(r0, rows), :] * _sigmoid(g_ref[pl.ds(r0, rows), :])
            return 0

        lax.fori_loop(0, n_chunk, glu, 0)

        def chunk(i, carry):
            db, dlg, dlb = carry
            r0 = pl.multiple_of(i * rows, rows)
            shifts = _row_shifts(u_s[pl.ds(r0, CONV_HALO + rows), :], False)
            y = y_ref[pl.ds(r0, rows), :]
            yc = y - jnp.mean(y, axis=-1, keepdims=True)
            r = lax.rsqrt(jnp.mean(yc * yc, axis=-1, keepdims=True) + EPS)
            n = yc * r
            ln = n * lg_ref[...] + lb_ref[...]
            sg = _sigmoid(ln)
            dln = dc_ref[pl.ds(r0, rows), :] * (sg * (1.0 + ln * (1.0 - sg)))
            dn = dln * lg_ref[...]
            dy = r * (dn - jnp.mean(dn, axis=-1, keepdims=True) - n * jnp.mean(dn * n, axis=-1, keepdims=True))
            dy_s[pl.ds(r0, rows), :] = dy
            for k in range(CONV_KERNEL):
                prod = _earlier(shifts, CONV_KERNEL - 1 - k, rows) * dy
                dw_s[k] += jnp.sum(prod.reshape(rows // 8, 8, cw), axis=0)
            return (db + jnp.sum(dy, axis=0, keepdims=True),
                    dlg + jnp.sum(dln * n, axis=0, keepdims=True),
                    dlb + jnp.sum(dln, axis=0, keepdims=True))

        zero = jnp.zeros((1, cw), F32)
        db, dlg, dlb = lax.fori_loop(0, n_chunk, chunk, (zero, zero, zero))
        db_ref[...] = db
        dlg_ref[...] = dlg
        dlb_ref[...] = dlb
        for k in range(CONV_KERNEL):
            dw_ref[k:k + 1, :] = jnp.sum(dw_s[k], axis=0, keepdims=True)

        def back(i, _):
            r0 = pl.multiple_of(i * rows, rows)
            shifts = _row_shifts(dy_s[pl.ds(r0, rows + CONV_HALO), :], True)
            du = None
            for k in range(CONV_KERNEL):
                term = _later(shifts, CONV_KERNEL - 1 - k, rows) * w_ref[k:k + 1, :]
                du = term if du is None else du + term
            av = a_ref[pl.ds(r0, rows), :]
            sg = _sigmoid(g_ref[pl.ds(r0, rows), :])
            o_ref[pl.ds(r0, rows), 0:cw] = (du * sg).astype(BF16)
            o_ref[pl.ds(r0, rows), cw:2 * cw] = (du * av * sg * (1.0 - sg)).astype(BF16)
            return 0

        lax.fori_loop(0, n_chunk, back, 0)
        for cp in copies:
            cp.wait()

    vec = pl.BlockSpec((1, cw), lambda i: (0, 0))
    wspec = pl.BlockSpec((CONV_KERNEL, cw), lambda i: (0, 0))
    out = pl.pallas_call(
        body, name=name, grid=(1,),
        in_specs=[pl.BlockSpec((s, cw), lambda i: (0, blk_a)), pl.BlockSpec((s, cw), lambda i: (0, blk_a + 1)),
                  pl.BlockSpec((s, cw), lambda i: (0, 0)), pl.BlockSpec((s, cw), lambda i: (0, 1)), wspec, vec, vec]
        + [ANY_SPEC] * n_send,
        out_specs=[pl.BlockSpec((s, 2 * cw), lambda i: (0, 0)), wspec, vec, vec, vec] + [ANY_SPEC] * n_send,
        out_shape=[jax.ShapeDtypeStruct((s, 2 * cw), BF16), jax.ShapeDtypeStruct((CONV_KERNEL, cw), F32)]
        + [jax.ShapeDtypeStruct((1, cw), F32)] * 3 + _scattered_shapes(sends),
        scratch_shapes=[pltpu.VMEM((CONV_HALO + s, cw), F32), pltpu.VMEM((s + CONV_HALO, cw), F32),
                        pltpu.VMEM((CONV_KERNEL, 8, cw), F32)] + _exchange_sems(n_send),
        compiler_params=_params(1),
    )(proj, proj, y, dcat, w, lg, lb, *sends)
    return out[:5], out[5:]


FFN_ROWS = 256


def _ffn_gate(g_ref, r0, rows, w_ref, b_ref):
    cur = g_ref[pl.ds(r0, rows), :].astype(F32)
    prev = g_ref[pl.ds(pl.multiple_of(jnp.maximum(r0 - FFN_HALO, 0), FFN_HALO), FFN_HALO), :].astype(F32)
    prev = jnp.where(r0 > 0, prev, 0.0)
    window = jnp.concatenate([prev, cur], axis=0)
    gc = cur * w_ref[FFN_KERNEL - 1:FFN_KERNEL, :] + b_ref[...]
    for k in range(FFN_KERNEL - 1):
        gc = gc + _shifted(window, FFN_KERNEL - 1 - k, FFN_HALO) * w_ref[k:k + 1, :]
    return gc, window


def _ffn_fwd(up, w, b, name):
    s = up.shape[0]
    f = w.shape[1]
    tc = _pick(f, (256, 128))
    nc = f // tc
    rows = _pick(s, (FFN_ROWS, 128))

    def body(g_ref, v_ref, w_ref, b_ref, o_ref):
        def chunk(i, _):
            r0 = pl.multiple_of(i * rows, rows)
            gc, _w = _ffn_gate(g_ref, r0, rows, w_ref, b_ref)
            o_ref[pl.ds(r0, rows), :] = (gc * _sigmoid(gc) * v_ref[pl.ds(r0, rows), :].astype(F32)).astype(BF16)
            return 0

        lax.fori_loop(0, s // rows, chunk, 0)

    return pl.pallas_call(
        body, name=name, grid=(nc,),
        in_specs=[pl.BlockSpec((s, tc), lambda j: (0, j)), pl.BlockSpec((s, tc), lambda j: (0, nc + j)),
                  pl.BlockSpec((FFN_KERNEL, tc), lambda j: (0, j)), pl.BlockSpec((1, tc), lambda j: (0, j))],
        out_specs=pl.BlockSpec((s, tc), lambda j: (0, j)),
        out_shape=jax.ShapeDtypeStruct((s, f), BF16),
        compiler_params=_params(1),
    )(up, up, w, b)


def _ffn_bwd(up, dact, w, b, name):
    s = up.shape[0]
    f = w.shape[1]
    tc = _pick(f, (256, 128))
    nc = f // tc
    rows = _pick(s, (FFN_ROWS, 128))
    n_chunk = s // rows

    def body(g_ref, v_ref, da_ref, w_ref, b_ref, d_ref, dw_ref, db_ref, dgc_s):
        dg_ref, dv_ref = d_ref.at[0], d_ref.at[1]
        dgc_s[pl.ds(s, FFN_HALO), :] = jnp.zeros((FFN_HALO, tc), F32)

        def chunk(i, carry):
            r0 = pl.multiple_of(i * rows, rows)
            gc, window = _ffn_gate(g_ref, r0, rows, w_ref, b_ref)
            sg = _sigmoid(gc)
            da = da_ref[pl.ds(r0, rows), :].astype(F32)
            dv_ref[pl.ds(r0, rows), :] = (da * gc * sg).astype(BF16)
            dgc = da * v_ref[pl.ds(r0, rows), :].astype(F32) * (sg * (1.0 + gc * (1.0 - sg)))
            dgc_s[pl.ds(r0, rows), :] = dgc
            out = [carry[0] + jnp.sum(dgc, axis=0, keepdims=True)]
            for k in range(FFN_KERNEL):
                out.append(carry[1 + k] + jnp.sum(_shifted(window, FFN_KERNEL - 1 - k, FFN_HALO) * dgc,
                                                  axis=0, keepdims=True))
            return tuple(out)

        zero = jnp.zeros((1, tc), F32)
        sums = lax.fori_loop(0, n_chunk, chunk, (zero,) * (1 + FFN_KERNEL))
        db_ref[...] = sums[0]
        for k in range(FFN_KERNEL):
            dw_ref[k:k + 1, :] = sums[1 + k]

        def back(i, _):
            r0 = pl.multiple_of(i * rows, rows)
            window = dgc_s[pl.ds(r0, rows + FFN_HALO), :]
            dg = window[:rows, :] * w_ref[FFN_KERNEL - 1:FFN_KERNEL, :]
            for k in range(FFN_KERNEL - 1):
                sh = FFN_KERNEL - 1 - k
                dg = dg + pltpu.roll(window, rows + FFN_HALO - sh, 0)[:rows, :] * w_ref[k:k + 1, :]
            dg_ref[pl.ds(r0, rows), :] = dg.astype(BF16)
            return 0

        lax.fori_loop(0, n_chunk, back, 0)

    blk = lambda off: pl.BlockSpec((s, tc), lambda j: (0, off + j))
    return pl.pallas_call(
        body, name=name, grid=(nc,),
        in_specs=[blk(0), blk(nc), blk(0), pl.BlockSpec((FFN_KERNEL, tc), lambda j: (0, j)),
                  pl.BlockSpec((1, tc), lambda j: (0, j))],
        out_specs=[pl.BlockSpec((2, s, tc), lambda j: (0, 0, j)), pl.BlockSpec((FFN_KERNEL, tc), lambda j: (0, j)),
                   pl.BlockSpec((1, tc), lambda j: (0, j))],
        out_shape=[jax.ShapeDtypeStruct((2, s, f), BF16),
                   jax.ShapeDtypeStruct((FFN_KERNEL, f), F32), jax.ShapeDtypeStruct((1, f), F32)],
        scratch_shapes=[pltpu.VMEM((s + FFN_HALO, tc), F32)],
        compiler_params=_params(1),
    )(up, up, dact, w, b)


def _loss_head(y, target, name):
    m, n = y.shape
    tm = _pick(m, (256, 128))

    def body(y_ref, t_ref, l_ref, d_ref, db_ref):
        e = y_ref[...] - t_ref[...]
        part = 0.5 * jnp.sum(jnp.sum(e * e, axis=-1, keepdims=True) / n, axis=0, keepdims=True)

        @pl.when(pl.program_id(0) == 0)
        def _():
            l_ref[...] = jnp.zeros_like(l_ref)

        l_ref[...] += part
        d = e / n
        d_ref[...] = d
        db_ref[...] = d.astype(BF16)

    return pl.pallas_call(
        body, name=name, grid=(m // tm,),
        in_specs=[pl.BlockSpec((tm, n), lambda i: (i, 0))] * 2,
        out_specs=[pl.BlockSpec((8, LANES), lambda i: (0, 0)), pl.BlockSpec((tm, n), lambda i: (i, 0)),
                   pl.BlockSpec((tm, n), lambda i: (i, 0))],
        out_shape=[jax.ShapeDtypeStruct((8, LANES), F32), jax.ShapeDtypeStruct((m, n), F32),
                   jax.ShapeDtypeStruct((m, n), BF16)],
        compiler_params=_params(1),
    )(y, target)


def _adamw_math(w, g, m, v):
    m = ADAM_B1 * m + (1.0 - ADAM_B1) * g
    v = ADAM_B2 * v + (1.0 - ADAM_B2) * (g * g)
    m_hat = m / (1.0 - ADAM_B1 ** ADAM_STEP)
    v_hat = v / (1.0 - ADAM_B2 ** ADAM_STEP)
    delta = -ADAM_LR * (m_hat / (jnp.sqrt(v_hat) + ADAM_EPS) + ADAM_WD * w)
    return delta, m, v


def _sum_adamw(parts, w, m, v, scatters, gathers, name):
    depth, r, c = w.shape
    tr = max(t for t in range(16, min(r, 192) + 1, 16) if r % t == 0)
    steps = r // tr
    n_sc, n_ga = len(scatters), len(gathers)
    n_send = n_sc + n_ga

    def body(*refs):
        p_refs, refs = refs[:depth], refs[depth:]
        (w_ref, m_ref, v_ref), send_refs, refs = refs[:3], refs[3:3 + n_send], refs[3 + n_send:]
        (g_out, d_out, m_out, v_out), got_refs, sems = refs[:4], refs[4:4 + n_send], refs[4 + n_send:]
        is_first = (pl.program_id(0) == 0) & (pl.program_id(1) == 0)
        is_last = (pl.program_id(0) == depth - 1) & (pl.program_id(1) == steps - 1)
        if n_sc:
            copies = _scatter_copies(send_refs[:n_sc], got_refs[:n_sc], sems[:3])

            @pl.when(is_first)
            def _():
                for cp in copies:
                    cp.start()
        if n_ga:
            start, relay, finish = _two_level_gather(send_refs[n_sc:], got_refs[n_sc:], sems[3 * (n_sc > 0):])
            pl.when(is_first)(start)
            pl.when(is_last)(relay)
        for layer in range(depth):
            @pl.when(pl.program_id(0) == layer)
            def _():
                g = p_refs[layer][0].astype(F32)
                for src in range(1, N_DEV):
                    g = g + p_refs[layer][src].astype(F32)
                d, mn, vn = _adamw_math(w_ref[0], g, m_ref[0], v_ref[0])
                g_out[0] = g
                d_out[0] = d
                m_out[0] = mn
                v_out[0] = vn

        if n_sc:
            @pl.when(is_last)
            def _():
                for cp in copies:
                    cp.wait()
        if n_ga:
            pl.when(is_last)(finish)

    def part_spec(layer):
        return pl.BlockSpec((N_DEV, tr, c), lambda l, i: (0, jnp.clip((l - layer) * steps + i, 0, steps - 1), 0))

    blk = pl.BlockSpec((1, tr, c), lambda l, i: (l, i, 0))
    out = pl.pallas_call(
        body, name=name, grid=(depth, steps),
        in_specs=[part_spec(layer) for layer in range(depth)] + [blk, blk, blk] + [ANY_SPEC] * n_send,
        out_specs=[blk] * 4 + [ANY_SPEC] * n_send,
        out_shape=[jax.ShapeDtypeStruct(w.shape, F32)] * 4 + _scattered_shapes(scatters) + _gathered_shapes(gathers),
        scratch_shapes=(_exchange_sems(n_sc) if n_sc else []) + (_exchange_sems(n_ga) if n_ga else []),
        compiler_params=_params(2),
    )(*parts, w, m, v, *scatters, *gathers)
    return out[:4], out[4:]


def _sum_rows(parts, name):
    _, r, c = parts.shape

    def body(p_ref, o_ref):
        g = p_ref[0]
        for src in range(1, N_DEV):
            g = g + p_ref[src]
        o_ref[...] = g

    return pl.pallas_call(
        body, name=name, grid=(1,),
        in_specs=[pl.BlockSpec((N_DEV, r, c), lambda i: (0, 0, 0))],
        out_specs=pl.BlockSpec((r, c), lambda i: (0, 0)),
        out_shape=jax.ShapeDtypeStruct((r, c), F32),
        compiler_params=_params(1),
    )(parts)


def _adamw_rows(w, g, m, v, name):
    r, c = w.shape

    def body(w_ref, g_ref, m_ref, v_ref, d_out, m_out, v_out):
        d, mn, vn = _adamw_math(w_ref[...], g_ref[...], m_ref[...], v_ref[...])
        d_out[...] = d
        m_out[...] = mn
        v_out[...] = vn

    blk = pl.BlockSpec((r, c), lambda i: (0, 0))
    return pl.pallas_call(
        body, name=name, grid=(1,), in_specs=[blk] * 4, out_specs=[blk] * 3,
        out_shape=[jax.ShapeDtypeStruct((r, c), F32)] * 3,
        compiler_params=_params(1),
    )(w, g, m, v)


PACK_TILE = 8 * LANES


def _pack(arrays):
    rows = []
    for a in arrays:
        flat = a.reshape(-1).astype(F32)
        pad = (-flat.shape[0]) % PACK_TILE
        rows.append(jnp.pad(flat, (0, pad)).reshape(-1, LANES))
    return jnp.concatenate(rows, axis=0)


def _unpack(packed, shapes, lead=()):
    out, r0 = [], 0
    for shp in shapes:
        size = 1
        for d in shp:
            size *= d
        nrows = -(-size // PACK_TILE) * 8
        piece = packed[..., r0:r0 + nrows, :].reshape(lead + (nrows * LANES,))[..., :size]
        out.append(piece.reshape(lead + tuple(shp)))
        r0 += nrows
    return out


def kernel(x, norm1_g, w_in, q_norm_g, k_norm_g, conv_dw_w, conv_dw_b, conv_ln_g, conv_ln_b, w_out, norm2_g, w_up, ffn_dw_w, ffn_dw_b, w_down, loss_target, m_norm1_g, m_w_in, m_q_norm_g, m_k_norm_g, m_conv_dw_w, m_conv_dw_b, m_conv_ln_g, m_conv_ln_b, m_w_out, m_norm2_g, m_w_up, m_ffn_dw_w, m_ffn_dw_b, m_w_down, v_norm1_g, v_w_in, v_q_norm_g, v_k_norm_g, v_conv_dw_w, v_conv_dw_b, v_conv_ln_g, v_conv_ln_b, v_w_out, v_norm2_g, v_w_up, v_ffn_dw_w, v_ffn_dw_b, v_w_down):
    depth, d_model, in_shard = w_in.shape
    out_shard = w_out.shape[1]
    up_shard = w_up.shape[2]
    down_shard = w_down.shape[1]
    d_ff = down_shard * N_DEV
    conv_w = conv_dw_b.shape[1]
    cw_shard = conv_dw_w.shape[2]
    fw_shard = ffn_dw_w.shape[2]
    me = 4 * lax.axis_index("x") + 2 * lax.axis_index("y") + lax.axis_index("c")

    transposed = lambda a: a.transpose(0, 2, 1)
    b_in, b_out, b_up, b_down = (transposed(w_in).astype(BF16), w_out.astype(BF16), transposed(w_up).astype(BF16),
                                 w_down.astype(BF16))
    rows_major = lambda g: g.reshape(N_DEV * g.shape[1], g.shape[2])
    g_in0, g_small = _gather([b_in[0], _pack([conv_dw_w, ffn_dw_w])], name="gather_first")
    wf_in, wf_out, wf_up, wf_down = [rows_major(g_in0)] + [None] * (depth - 1), [None] * depth, [None] * depth, [None] * depth
    g_cw, g_fw = _unpack(g_small, [conv_dw_w.shape, ffn_dw_w.shape], lead=(N_DEV,))
    cwf = g_cw.transpose(1, 2, 0, 3).reshape(depth, CONV_KERNEL, conv_w)
    fwf = g_fw.transpose(1, 2, 0, 3).reshape(depth, FFN_KERNEL, d_ff)

    row = lambda a, l: a[l].reshape(1, -1)
    both_heads = lambda a, l: jnp.tile(row(a, l), (1, HEADS_PER_STEP))

    xs = x[0]
    saved = []
    for l in range(depth):
        proj, h1, _ = _mm_rms(xs, row(norm1_g, l), wf_in[l], F32, [], name="fwd_in")
        sends = [b_out[l], b_up[l]] + ([b_in[l + 1]] if l + 1 < depth else [])
        attn, tsum, got = _attn_fwd(proj, both_heads(q_norm_g, l), both_heads(k_norm_g, l), sends, name="fwd_attn")
        wf_out[l], wf_up[l] = rows_major(got[0]), rows_major(got[1])
        if l + 1 < depth:
            wf_in[l + 1] = rows_major(got[2])
        conv, conv_y = _conv_fwd(proj, cwf[l], row(conv_dw_b, l), row(conv_ln_g, l), row(conv_ln_b, l),
                                 name="fwd_conv")
        cat = jnp.concatenate([attn, conv], axis=1)
        x_mid = _mm_res(cat, wf_out[l], xs, name="fwd_out")
        up, h2, got = _mm_rms(x_mid, row(norm2_g, l), wf_up[l], BF16, [b_down[l]], name="fwd_up")
        wf_down[l] = rows_major(got[0])
        act = _ffn_fwd(up, fwf[l], row(ffn_dw_b, l), name="fwd_ffn")
        x_next = _mm_res(act, wf_down[l], x_mid, name="fwd_down")
        saved.append((xs, h1, proj, tsum, cat, x_mid, h2, up, act, conv_y))
        xs = x_next

    loss_tile, dx, dxb = _loss_head(xs, loss_target[0], name="loss_head")
    loss = lax.psum(loss_tile[0, 0], ("x", "y", "c"))

    r_in, r_out, r_up, r_down = [None] * depth, [None] * depth, [None] * depth, [None] * depth
    row_blocks = lambda g: g.reshape(N_DEV, g.shape[0] // N_DEV, g.shape[1])
    small = {k: [None] * depth for k in ("norm1_g", "q_norm_g", "k_norm_g", "conv_dw_w", "conv_dw_b", "conv_ln_g",
                                         "conv_ln_b", "norm2_g", "ffn_dw_w", "ffn_dw_b")}
    gw_in = None
    for l in reversed(range(depth)):
        xs, h1, proj, tsum, cat, x_mid, h2, up, act, conv_y = saved[l]
        dact = _mm_nt(dxb, wf_down[l], BF16, name="bwd_dact")
        gw_down = _mm_tn(act, dxb, name="bwd_gw_down")
        dup, small["ffn_dw_w"][l], small["ffn_dw_b"][l] = _ffn_bwd(up, dact, fwf[l], row(ffn_dw_b, l), name="bwd_ffn")
        gw_up = _mm_tn(dup, h2, name="bwd_gw_up")
        dx, dxb, small["norm2_g"][l], _ = _mm_rmsbwd(dup, wf_up[l], x_mid, row(norm2_g, l), dx, [], name="bwd_up")
        dcat = _mm_nt(dxb, wf_out[l], F32, name="bwd_dcat")
        gw_out = _mm_tn(cat, dxb, name="bwd_gw_out")
        (dglu, small["conv_dw_w"][l], small["conv_dw_b"][l], small["conv_ln_g"][l], small["conv_ln_b"][l]), got = (
            _conv_bwd(proj, conv_y, dcat, cwf[l], row(conv_ln_g, l), row(conv_ln_b, l), [row_blocks(gw_out)],
                      name="bwd_conv"))
        r_out[l] = got[0]
        sends = [row_blocks(gw_down), row_blocks(gw_up)] + ([row_blocks(gw_in)] if l + 1 < depth else [])
        (dq, dk, dv, dqg, dkg), got = _attn_bwd(proj, dcat, tsum, both_heads(q_norm_g, l), both_heads(k_norm_g, l),
                                                sends, name="bwd_attn")
        r_down[l], r_up[l] = got[:2]
        if l + 1 < depth:
            r_in[l + 1] = got[2]
        small["q_norm_g"][l] = dqg[0:1, :HEAD_DIM] + dqg[0:1, HEAD_DIM:]
        small["k_norm_g"][l] = dkg[0:1, :HEAD_DIM] + dkg[0:1, HEAD_DIM:]
        dproj = jnp.concatenate([dq, dk, dv, dglu], axis=1)
        gw_in = _mm_tn(dproj, h1, name="bwd_gw_in")
        dx, dxb, small["norm1_g"][l], _ = _mm_rmsbwd(dproj, wf_in[l], xs, row(norm1_g, l), dx, [], name="bwd_in")
    grad_x = dx[None]

    names = ["norm1_g", "q_norm_g", "k_norm_g", "conv_dw_w", "conv_dw_b", "conv_ln_g", "conv_ln_b", "norm2_g",
             "ffn_dw_w", "ffn_dw_b"]
    full_shapes = {"norm1_g": norm1_g.shape, "q_norm_g": q_norm_g.shape, "k_norm_g": k_norm_g.shape,
                   "conv_dw_w": (depth, CONV_KERNEL, conv_w), "conv_dw_b": conv_dw_b.shape,
                   "conv_ln_g": conv_ln_g.shape, "conv_ln_b": conv_ln_b.shape, "norm2_g": norm2_g.shape,
                   "ffn_dw_w": (depth, FFN_KERNEL, d_ff), "ffn_dw_b": ffn_dw_b.shape}
    partial = _pack([jnp.stack(small[k]).reshape(full_shapes[k]) for k in names])
    big = {}
    big["w_up"], (r_in[0],) = _sum_adamw(r_up, transposed(w_up), transposed(m_w_up), transposed(v_w_up),
                                         [row_blocks(gw_in)], [], name="adamw_up")
    big["w_down"], (all_partials,) = _sum_adamw(r_down, w_down, m_w_down, v_w_down, [], [partial], name="adamw_down")
    big["w_out"], _ = _sum_adamw(r_out, w_out, m_w_out, v_w_out, [], [], name="adamw_out")
    big["w_in"], _ = _sum_adamw(r_in, transposed(w_in), transposed(m_w_in), transposed(v_w_in), [], [],
                                name="adamw_in")
    for k in ("w_in", "w_up"):
        big[k] = [transposed(a) for a in big[k]]

    total = _unpack(_sum_rows(all_partials, name="sum_small_grads"), [full_shapes[k] for k in names])
    grads = dict(zip(names, total))
    grads["conv_dw_w"] = lax.dynamic_slice_in_dim(grads["conv_dw_w"], me * cw_shard, cw_shard, axis=2)
    grads["ffn_dw_w"] = lax.dynamic_slice_in_dim(grads["ffn_dw_w"], me * fw_shard, fw_shard, axis=2)
    weights = dict(norm1_g=norm1_g, q_norm_g=q_norm_g, k_norm_g=k_norm_g, conv_dw_w=conv_dw_w, conv_dw_b=conv_dw_b,
                   conv_ln_g=conv_ln_g, conv_ln_b=conv_ln_b, norm2_g=norm2_g, ffn_dw_w=ffn_dw_w, ffn_dw_b=ffn_dw_b)
    m_in = dict(norm1_g=m_norm1_g, q_norm_g=m_q_norm_g, k_norm_g=m_k_norm_g, conv_dw_w=m_conv_dw_w,
                conv_dw_b=m_conv_dw_b, conv_ln_g=m_conv_ln_g, conv_ln_b=m_conv_ln_b, norm2_g=m_norm2_g,
                ffn_dw_w=m_ffn_dw_w, ffn_dw_b=m_ffn_dw_b)
    v_in = dict(norm1_g=v_norm1_g, q_norm_g=v_q_norm_g, k_norm_g=v_k_norm_g, conv_dw_w=v_conv_dw_w,
                conv_dw_b=v_conv_dw_b, conv_ln_g=v_conv_ln_g, conv_ln_b=v_conv_ln_b, norm2_g=v_norm2_g,
                ffn_dw_w=v_ffn_dw_w, ffn_dw_b=v_ffn_dw_b)
    shard_shapes = [weights[k].shape for k in names]
    d_s, m_s, v_s = _adamw_rows(_pack([weights[k] for k in names]), _pack([grads[k] for k in names]),
                                _pack([m_in[k] for k in names]), _pack([v_in[k] for k in names]), name="adamw_small")
    delta = dict(zip(names, _unpack(d_s, shard_shapes)))
    new_m = dict(zip(names, _unpack(m_s, shard_shapes)))
    new_v = dict(zip(names, _unpack(v_s, shard_shapes)))
    for k, (g, d, mn, vn) in big.items():
        grads[k], delta[k], new_m[k], new_v[k] = g, d, mn, vn

    order = ["norm1_g", "w_in", "q_norm_g", "k_norm_g", "conv_dw_w", "conv_dw_b", "conv_ln_g", "conv_ln_b", "w_out",
             "norm2_g", "w_up", "ffn_dw_w", "ffn_dw_b", "w_down"]
    return (loss, grad_x, *[grads[k] for k in order], *[delta[k] for k in order], *[new_m[k] for k in order],
            *[new_v[k] for k in order])
```

```python
import functools

import jax
import jax.numpy as jnp
from jax import lax
from jax.experimental import pallas as pl
from jax.experimental.pallas import tpu as pltpu

F32 = jnp.float32
BF16 = jnp.bfloat16

N_DEV = 8
HEADS = 8
HEAD_DIM = 64
ATTN_WIDTH = HEADS * HEAD_DIM
CONV_KERNEL = 31
FFN_KERNEL = 3
EPS = 1e-6
BLK = 128
KEY_GROUP = 4
LANES = 128
NORM_ROWS = 128
CONV_HALO = 32
FFN_HALO = 16

ADAM_LR = 0.001
ADAM_B1 = 0.9
ADAM_B2 = 0.999
ADAM_EPS = 1e-08
ADAM_WD = 0.01
ADAM_STEP = 10

VMEM_LIMIT = 56 * 1024 * 1024


def _params(n_axes=0):
    kw = dict(vmem_limit_bytes=VMEM_LIMIT)
    if n_axes:
        kw["dimension_semantics"] = ("arbitrary",) * n_axes
    return pltpu.CompilerParams(**kw)


def _dot(a, b):
    return jnp.dot(a, b, preferred_element_type=F32)


def _dot_nt(a, b):
    return lax.dot_general(a, b, (((1,), (1,)), ((), ())), preferred_element_type=F32)


def _dot_tn(a, b):
    return lax.dot_general(a, b, (((0,), (0,)), ((), ())), preferred_element_type=F32)


def _sigmoid(x):
    return 1.0 / (1.0 + jnp.exp(-x))


def _split_bf16(x):
    hi = x.astype(BF16)
    lo = (x - hi.astype(F32)).astype(BF16)
    return hi, lo


def _pick(n, options):
    for t in options:
        if n % t == 0:
            return t
    return n


def _tile(n, cap):
    best = None
    for t in range(LANES, min(n, cap) + 1, LANES):
        if n % t == 0:
            best = t
    return best or n


def _mm_rms(x, g, wt, out_dtype, sends, name):
    m, k = x.shape
    n = wt.shape[0]
    tm = _tile(m, 2048)
    tn = _tile(n, 512)
    n_send = len(sends)
    grid = (m // tm, n // tn)

    def body(x_ref, g_ref, w_ref, *rest):
        send_refs, (o_ref, h_ref), rest = rest[:n_send], rest[n_send:n_send + 2], rest[n_send + 2:]
        got_refs, h_s, sems = rest[:n_send], rest[n_send], rest[n_send + 1:]
        if n_send:
            start, relay, finish = _two_level_gather(send_refs, got_refs, sems)
            is_first = (pl.program_id(0) == 0) & (pl.program_id(1) == 0)
            is_last = (pl.program_id(0) == grid[0] - 1) & (pl.program_id(1) == grid[1] - 1)
            pl.when(is_first)(start)
            pl.when(is_last)(relay)

        @pl.when(pl.program_id(1) == 0)
        def _():
            def chunk(c, _):
                rows = pl.ds(pl.multiple_of(c * NORM_ROWS, NORM_ROWS), NORM_ROWS)
                xv = x_ref[rows, :]
                r = lax.rsqrt(jnp.mean(xv * xv, axis=-1, keepdims=True) + EPS)
                hv = (xv * r * g_ref[...]).astype(BF16)
                h_s[rows, :] = hv
                h_ref[rows, :] = hv
                return 0

            lax.fori_loop(0, tm // NORM_ROWS, chunk, 0)

        o_ref[...] = _dot_nt(h_s[...], w_ref[...]).astype(out_dtype)
        if n_send:
            pl.when(is_last)(finish)

    out = pl.pallas_call(
        body, name=name, grid=grid,
        in_specs=[pl.BlockSpec((tm, k), lambda i, j: (i, 0)),
                  pl.BlockSpec((1, k), lambda i, j: (0, 0)),
                  pl.BlockSpec((tn, k), lambda i, j: (j, 0))] + [ANY_SPEC] * n_send,
        out_specs=[pl.BlockSpec((tm, tn), lambda i, j: (i, j)),
                   pl.BlockSpec((tm, k), lambda i, j: (i, 0))] + [ANY_SPEC] * n_send,
        out_shape=[jax.ShapeDtypeStruct((m, n), out_dtype), jax.ShapeDtypeStruct((m, k), BF16)]
        + _gathered_shapes(sends),
        scratch_shapes=[pltpu.VMEM((tm, k), BF16)] + (_exchange_sems(n_send) if n_send else []),
        compiler_params=_params(2),
    )(x, g, wt, *sends)
    return out[0], out[1], out[2:]


def _mm_res(a, w, res, name):
    m, k = a.shape
    n = w.shape[1]
    tm = _tile(m, 1024)
    tn = _tile(n, 512)

    def body(a_ref, w_ref, r_ref, o_ref):
        o_ref[...] = r_ref[...] + _dot(a_ref[...], w_ref[...])

    return pl.pallas_call(
        body, name=name, grid=(m // tm, n // tn),
        in_specs=[pl.BlockSpec((tm, k), lambda i, j: (i, 0)),
                  pl.BlockSpec((k, tn), lambda i, j: (0, j)),
                  pl.BlockSpec((tm, tn), lambda i, j: (i, j))],
        out_specs=pl.BlockSpec((tm, tn), lambda i, j: (i, j)),
        out_shape=jax.ShapeDtypeStruct((m, n), F32),
        compiler_params=_params(2),
    )(a, w, res)


def _mm_nt(a, w, out_dtype, name):
    m, k = a.shape
    n = w.shape[0]
    tm = _tile(m, 1024)
    tn = _tile(n, 1408)

    def body(a_ref, w_ref, o_ref):
        o_ref[...] = _dot_nt(a_ref[...], w_ref[...]).astype(out_dtype)

    return pl.pallas_call(
        body, name=name, grid=(m // tm, n // tn),
        in_specs=[pl.BlockSpec((tm, k), lambda i, j: (i, 0)),
                  pl.BlockSpec((tn, k), lambda i, j: (j, 0))],
        out_specs=pl.BlockSpec((tm, tn), lambda i, j: (i, j)),
        out_shape=jax.ShapeDtypeStruct((m, n), out_dtype),
        compiler_params=_params(2),
    )(a, w)


def _column_tiles(a, cap):
    if a.ndim == 2:
        s, c = a.shape
        tc = _tile(c, cap)
        return s, c, tc, lambda rows, index: pl.BlockSpec((rows, tc), lambda *g: (index(*g)[0], index(*g)[1]))
    slabs, s, width = a.shape
    tc = _tile(width, cap)
    per = width // tc
    return s, slabs * width, tc, lambda rows, index: pl.BlockSpec(
        (None, rows, tc), lambda *g: (index(*g)[1] // per, index(*g)[0], index(*g)[1] % per))


def _mm_tn(a, b, name):
    s, m, tm, a_spec = _column_tiles(a, 1408)
    n = b.shape[1]
    tn = _tile(n, 1024)

    def body(a_ref, b_ref, o_ref):
        o_ref[...] = _dot_tn(a_ref[...], b_ref[...]).astype(BF16)

    return pl.pallas_call(
        body, name=name, grid=(m // tm, n // tn),
        in_specs=[a_spec(s, lambda i, j: (0, i)),
                  pl.BlockSpec((s, tn), lambda i, j: (0, j))],
        out_specs=pl.BlockSpec((tm, tn), lambda i, j: (i, j)),
        out_shape=jax.ShapeDtypeStruct((m, n), BF16),
        compiler_params=_params(2),
    )(a, b)


def _mm_rmsbwd(a, w, x, g, dres, sends, name):
    m, k, tk, a_spec = _column_tiles(a, 1408)
    n = w.shape[1]
    tm = _tile(m, 1024)
    nk = k // tk
    n_send = len(sends)

    def body(a_ref, w_ref, x_ref, g_ref, r_ref, *rest):
        send_refs, (dx_ref, dxb_ref, dg_ref), rest = rest[:n_send], rest[n_send:n_send + 3], rest[n_send + 3:]
        got_refs, acc, sems = rest[:n_send], rest[n_send], rest[n_send + 1:]
        i, kk = pl.program_id(0), pl.program_id(1)
        if n_send:
            copies = _scatter_copies(send_refs, got_refs, sems)

            @pl.when((i == 0) & (kk == 0))
            def _():
                for cp in copies:
                    cp.start()

        part = _dot(a_ref[...], w_ref[...])

        @pl.when(kk == 0)
        def _():
            acc[...] = part

        @pl.when(kk > 0)
        def _():
            acc[...] += part

        @pl.when(kk == nk - 1)
        def _():
            def chunk(c, dgp):
                rows = pl.ds(pl.multiple_of(c * NORM_ROWS, NORM_ROWS), NORM_ROWS)
                dh = acc[rows, :]
                xv = x_ref[rows, :]
                r = lax.rsqrt(jnp.mean(xv * xv, axis=-1, keepdims=True) + EPS)
                xh = xv * r
                dxh = dh * g_ref[...]
                dx = r_ref[rows, :] + r * (dxh - xh * jnp.mean(dxh * xh, axis=-1, keepdims=True))
                dx_ref[rows, :] = dx
                dxb_ref[rows, :] = dx.astype(BF16)
                return dgp + jnp.sum(dh * xh, axis=0, keepdims=True)

            dgp = lax.fori_loop(0, tm // NORM_ROWS, chunk, jnp.zeros((1, n), F32))

            @pl.when(i == 0)
            def _():
                dg_ref[...] = dgp

            @pl.when(i > 0)
            def _():
                dg_ref[...] += dgp

        if n_send:
            @pl.when((i == m // tm - 1) & (kk == nk - 1))
            def _():
                for cp in copies:
                    cp.wait()

    out = pl.pallas_call(
        body, name=name, grid=(m // tm, nk),
        in_specs=[a_spec(tm, lambda i, kk: (i, kk)),
                  pl.BlockSpec((tk, n), lambda i, kk: (kk, 0)),
                  pl.BlockSpec((tm, n), lambda i, kk: (i, 0)),
                  pl.BlockSpec((1, n), lambda i, kk: (0, 0)),
                  pl.BlockSpec((tm, n), lambda i, kk: (i, 0))] + [ANY_SPEC] * n_send,
        out_specs=[pl.BlockSpec((tm, n), lambda i, kk: (i, 0)),
                   pl.BlockSpec((tm, n), lambda i, kk: (i, 0)),
                   pl.BlockSpec((1, n), lambda i, kk: (0, 0))] + [ANY_SPEC] * n_send,
        out_shape=[jax.ShapeDtypeStruct((m, n), F32), jax.ShapeDtypeStruct((m, n), BF16),
                   jax.ShapeDtypeStruct((1, n), F32)] + _scattered_shapes(sends),
        scratch_shapes=[pltpu.VMEM((tm, n), F32)] + (_exchange_sems(n_send) if n_send else []),
        compiler_params=_params(2),
    )(a, w, x, g, dres, *sends)
    return out[0], out[1], out[2], out[3:]


ANY_SPEC = pl.BlockSpec(memory_space=pl.ANY)
SEMS_PER_OPERAND = N_DEV - 1


def _exchange_sems(n):
    return [pltpu.SemaphoreType.DMA((n, SEMS_PER_OPERAND)), pltpu.SemaphoreType.DMA((n, SEMS_PER_OPERAND)),
            pltpu.SemaphoreType.DMA((n,))]


def _gathered_shapes(parts):
    return [jax.ShapeDtypeStruct((N_DEV,) + a.shape, a.dtype) for a in parts]


def _scattered_shapes(parts):
    return [jax.ShapeDtypeStruct(a.shape, a.dtype) for a in parts]


def _flat(pos):
    return 4 * pos[0] + 2 * pos[1] + pos[2]


def _remote(src, dst, sems, i, k, to):
    send_sems, recv_sems, _ = sems
    return pltpu.make_async_remote_copy(src_ref=src, dst_ref=dst, send_sem=send_sems.at[i, k],
                                        recv_sem=recv_sems.at[i, k], device_id=to,
                                        device_id_type=pl.DeviceIdType.MESH)


def _scatter_copies(ins, outs, sems):
    x, y, c = lax.axis_index("x"), lax.axis_index("y"), lax.axis_index("c")
    me = _flat((x, y, c))
    copies = [pltpu.make_async_copy(ins[i].at[me], outs[i].at[me], sems[2].at[i]) for i in range(len(ins))]
    for d in range(1, N_DEV):
        peer = (1 - x if d & 4 else x, 1 - y if d & 2 else y, 1 - c if d & 1 else c)
        for i in range(len(ins)):
            copies.append(_remote(ins[i].at[_flat(peer)], outs[i].at[me], sems, i, d - 1, peer))
    return copies


def _two_level_gather(ins, outs, sems):
    x, y, c = lax.axis_index("x"), lax.axis_index("y"), lax.axis_index("c")
    me, sibling = (x, y, c), (x, y, 1 - c)
    chips = [(1 - x, y), (x, 1 - y), (1 - x, 1 - y)]
    n = len(ins)

    def block(i, pos):
        return outs[i].at[_flat(pos)]

    local = [pltpu.make_async_copy(ins[i], block(i, me), sems[2].at[i]) for i in range(n)]
    own = [_remote(ins[i], block(i, me), sems, i, 0, sibling) for i in range(n)]
    own += [_remote(ins[i], block(i, me), sems, i, 1 + j, (*chip, c)) for i in range(n) for j, chip in enumerate(chips)]
    passed = [[_remote(block(i, (*chip, c)), block(i, (*chip, c)), sems, i, 4 + j, sibling) for i in range(n)]
              for j, chip in enumerate(chips)]

    def first():
        for cp in local + own:
            cp.start()

    def relay():
        for j, chip in enumerate(chips):
            for i in range(n):
                _remote(ins[i], block(i, (*chip, c)), sems, i, 1 + j, me).wait_recv()
                passed[j][i].start()

    def finish():
        for i in range(n):
            _remote(ins[i], block(i, sibling), sems, i, 0, me).wait_recv()
            for j, chip in enumerate(chips):
                _remote(ins[i], block(i, (*chip, 1 - c)), sems, i, 4 + j, me).wait_recv()
        for cp in own + [cp for row in passed for cp in row]:
            cp.wait_send()
        for cp in local:
            cp.wait()

    return first, relay, finish


def _gather(parts, name):
    n = len(parts)

    def body(*refs):
        start, relay, finish = _two_level_gather(refs[:n], refs[n:2 * n], refs[2 * n:])
        start()
        relay()
        finish()

    return pl.pallas_call(
        body, name=name, in_specs=[ANY_SPEC] * n, out_specs=[ANY_SPEC] * n,
        out_shape=_gathered_shapes(parts), scratch_shapes=_exchange_sems(n),
    )(*parts)


def _tri(kind):
    j = lax.broadcasted_iota(jnp.int32, (BLK, BLK), 0)
    s = lax.broadcasted_iota(jnp.int32, (BLK, BLK), 1)
    m = {"after": j > s, "upto": j <= s, "before": j < s}[kind]
    return jnp.concatenate([jnp.where(m, 1.0, 0.0), jnp.ones((BLK, BLK), F32)], axis=1).astype(BF16)


def _scan_rows(v, tri):
    r = _dot(v.astype(BF16), tri)
    return r[:, :BLK], r[:, BLK:]


HEADS_PER_STEP = LANES // HEAD_DIM


def _first_head_lanes():
    return lax.broadcasted_iota(jnp.int32, (1, LANES), 1) < HEAD_DIM


def _pair_mean(v, first):
    m0 = jnp.sum(jnp.where(first, v, 0.0), axis=-1, keepdims=True)
    m1 = jnp.sum(jnp.where(first, 0.0, v), axis=-1, keepdims=True)
    return jnp.where(first, m0, m1) * (1.0 / HEAD_DIM)


def _pair_norm(v, g2, first):
    return v * lax.rsqrt(_pair_mean(v * v, first) + EPS) * g2


def _pair_norm_bwd(raw, g2, dn, first):
    r = lax.rsqrt(_pair_mean(raw * raw, first) + EPS)
    xh = raw * r
    dxh = dn * g2
    return r * (dxh - xh * _pair_mean(dxh * xh, first)), jnp.sum(dn * xh, axis=0, keepdims=True)


def _block_diag(v, first):
    zero = jnp.zeros_like(v)
    return jnp.concatenate([jnp.where(first, v, zero), jnp.where(first, zero, v)], axis=0)


def _attn_prep(q_ref, k_ref, v_ref, qg_ref, kg_ref, qc_s, kc_s, vd_s, kd_s, n_blk):
    scale = HEAD_DIM ** -0.5
    first = _first_head_lanes()

    def prep(i, _):
        rows = pl.ds(pl.multiple_of(i * BLK, BLK), BLK)
        both = pl.ds(pl.multiple_of(i * 2 * BLK, 2 * BLK), 2 * BLK)
        qh, ql = _split_bf16(_pair_norm(q_ref[rows, :], qg_ref[...], first) * scale)
        kh, kl = _split_bf16(_pair_norm(k_ref[rows, :], kg_ref[...], first))
        for h in range(HEADS_PER_STEP):
            sl = slice(h * HEAD_DIM, (h + 1) * HEAD_DIM)
            qc_s[h, rows, :] = jnp.concatenate([qh[:, sl], ql[:, sl], qh[:, sl], ql[:, sl]], axis=1)
            kc_s[h, rows, :] = jnp.concatenate([kh[:, sl], kh[:, sl], kl[:, sl], kl[:, sl]], axis=1)
        vd_s[both, :] = _block_diag(v_ref[rows, :].astype(BF16), first)
        if kd_s is not None:
            kd_s[both, :] = _block_diag(kh, first)
        return 0

    lax.fori_loop(0, n_blk, prep, 0)


def _pair_scores(qc, kc_ref, grp):
    zs = []
    for j in range(0, KEY_GROUP, 2):
        two = pl.ds(pl.multiple_of((grp * KEY_GROUP + j) * BLK, 2 * BLK), 2 * BLK)
        z = _dot_nt(qc, kc_ref[two, :])
        zs += [z[:, :BLK], z[:, BLK:]]
    return zs


def _col_minus_row():
    row = lax.broadcasted_iota(jnp.int32, (BLK, BLK), 0)
    col = lax.broadcasted_iota(jnp.int32, (BLK, BLK), 1)
    return col - row


def _softplus(z):
    return jnp.maximum(z, 0.0) + jnp.log(1.0 + jnp.exp(-jnp.abs(z)))


def _attn_fwd(proj, qg, kg, sends, name):
    s = proj.shape[0]
    n_blk = s // BLK
    pairs = ATTN_WIDTH // LANES
    n_send = len(sends)

    def body(q_ref, k_ref, v_ref, qg_ref, kg_ref, *rest):
        send_refs, (o_ref, t_ref), rest = rest[:n_send], rest[n_send:n_send + 2], rest[n_send + 2:]
        got_refs, (qc_s, kc_s, vd_s), sems = rest[:n_send], rest[n_send:n_send + 3], rest[n_send + 3:]
        start, relay, finish = _two_level_gather(send_refs, got_refs, sems)
        step = pl.program_id(0)
        pl.when(step == 0)(start)
        pl.when(step == pairs - 1)(relay)
        tri = _tri("after")
        diff = _col_minus_row()
        first = _first_head_lanes()
        heads = range(HEADS_PER_STEP)
        _attn_prep(q_ref, k_ref, v_ref, qg_ref, kg_ref, qc_s, kc_s, vd_s, None, n_blk)

        def group(qc, qi, grp, carry, acc, masked):
            blocks = [grp * KEY_GROUP + j for j in reversed(range(KEY_GROUP))]
            zs = [_pair_scores(qc[h], kc_s.at[h], grp)[::-1] for h in heads]
            keeps = [diff < (qi - kb) * BLK if masked else None for kb in blocks]
            parts = [[None] * KEY_GROUP for _ in heads]
            for h in heads:
                for j, z in enumerate(zs[h]):
                    sp = _softplus(z)
                    lom = -sp
                    if masked:
                        lom = jnp.where(keeps[j], lom, 0.0)
                    tail, tot = _scan_rows(lom, tri)
                    parts[h][j] = (z - sp + tail, tot)
            carry = list(carry)
            for j, kb in enumerate(blocks):
                ws = []
                for h in heads:
                    lw, tot = parts[h][j]
                    w = jnp.exp(lw + carry[h])
                    if masked:
                        w = jnp.where(keeps[j], w, 0.0)
                    ws.append(w.astype(BF16))
                    carry[h] = carry[h] + tot
                acc = acc + _dot(jnp.concatenate(ws, axis=1),
                                 vd_s[pl.ds(pl.multiple_of(kb * 2 * BLK, 2 * BLK), 2 * BLK), :])
            return tuple(carry), acc

        def q_block(qi, _):
            rows = pl.ds(pl.multiple_of(qi * BLK, BLK), BLK)
            qc = [qc_s[h, rows, :] for h in heads]
            top = qi // KEY_GROUP
            zero = jnp.zeros((BLK, BLK), F32)
            carry, acc = group(qc, qi, top, (zero,) * HEADS_PER_STEP, jnp.zeros((BLK, LANES), F32), True)
            carry, acc = lax.fori_loop(
                0, top, lambda t, c: group(qc, qi, top - 1 - t, c[0], c[1], False), (carry, acc))
            o_ref[rows, :] = acc.astype(BF16)
            t_ref[rows, :] = jnp.where(first, carry[0], carry[1])
            return 0

        lax.fori_loop(0, n_blk, q_block, 0)
        pl.when(step == pairs - 1)(finish)

    col = lambda off: pl.BlockSpec((s, LANES), lambda p: (0, off + p))
    vec = pl.BlockSpec((1, LANES), lambda p: (0, 0))
    out = pl.pallas_call(
        body, name=name, grid=(pairs,),
        in_specs=[col(0), col(pairs), col(2 * pairs), vec, vec] + [ANY_SPEC] * n_send,
        out_specs=[pl.BlockSpec((s, LANES), lambda p: (0, p))] * 2 + [ANY_SPEC] * n_send,
        out_shape=[jax.ShapeDtypeStruct((s, ATTN_WIDTH), BF16), jax.ShapeDtypeStruct((s, ATTN_WIDTH), F32)]
        + _gathered_shapes(sends),
        scratch_shapes=[pltpu.VMEM((HEADS_PER_STEP, s, 4 * HEAD_DIM), BF16)] * 2
        + [pltpu.VMEM((HEADS_PER_STEP * s, LANES), BF16)] + _exchange_sems(n_send),
        compiler_params=_params(1),
    )(proj, proj, proj, qg, kg, *sends)
    return out[0], out[1], out[2:]


def _attn_bwd(proj, dcat, tsum, qg, kg, sends, name):
    s = proj.shape[0]
    n_blk = s // BLK
    pairs = ATTN_WIDTH // LANES
    scale = HEAD_DIM ** -0.5
    n_send = len(sends)
    n_scratch = 7

    def body(q_ref, k_ref, v_ref, do_ref, t_ref, qg_ref, kg_ref, *rest):
        send_refs, rest = rest[:n_send], rest[n_send:]
        (dq_ref, dk_ref, dv_ref, dqg_ref, dkg_ref), rest = rest[:5], rest[5:]
        got_refs, scratch, sems = rest[:n_send], rest[n_send:n_send + n_scratch], rest[n_send + n_scratch:]
        qc_s, kc_s, vd_s, kd_s, qd_s, dob_s, dkv_s = scratch
        copies = _scatter_copies(send_refs, got_refs, sems)

        @pl.when(pl.program_id(0) == 0)
        def _():
            for cp in copies:
                cp.start()

        tri_p = _tri("upto")
        tri_h = _tri("before")
        diff = _col_minus_row()

        @pl.when(pl.program_id(0) == 0)
        def _():
            dqg_ref[...] = jnp.zeros_like(dqg_ref)
            dkg_ref[...] = jnp.zeros_like(dkg_ref)

        first = _first_head_lanes()
        heads = range(HEADS_PER_STEP)
        _attn_prep(q_ref, k_ref, v_ref, qg_ref, kg_ref, qc_s, kc_s, vd_s, kd_s, n_blk)

        def prep(i, _):
            rows = pl.ds(pl.multiple_of(i * BLK, BLK), BLK)
            both = pl.ds(pl.multiple_of(i * 2 * BLK, 2 * BLK), 2 * BLK)
            dob = do_ref[rows, :].astype(BF16)
            dob_s[rows, :] = dob
            none = jnp.zeros((BLK, HEAD_DIM), BF16)
            for h in heads:
                qd_s[h, both, :] = jnp.concatenate(
                    [jnp.concatenate([qc_s[h, rows, 0:HEAD_DIM], none], axis=1),
                     jnp.concatenate([none, dob[:, h * HEAD_DIM:(h + 1) * HEAD_DIM]], axis=1)], axis=0)
                dkv_s[h, rows, :] = jnp.zeros((BLK, LANES), F32)
            return 0

        lax.fori_loop(0, n_blk, prep, 0)

        def group(qc, qd, dob, tq, qi, grp, pc, hc, dq, masked):
            blocks = [grp * KEY_GROUP + j for j in range(KEY_GROUP)]
            cols_of = [pl.ds(pl.multiple_of(kb * BLK, BLK), BLK) for kb in blocks]
            both_of = [pl.ds(pl.multiple_of(kb * 2 * BLK, 2 * BLK), 2 * BLK) for kb in blocks]
            zs = [_pair_scores(qc[h], kc_s.at[h], grp) for h in heads]
            das =[_dot_nt(dob, vd_s[both, :]) for both in both_of]
            keeps = [diff < (qi - kb) * BLK if masked else None for kb in blocks]
            lbs = [[None] * KEY_GROUP for _ in heads]
            scans = [[None] * KEY_GROUP for _ in heads]
            for h in heads:
                for j, z in enumerate(zs[h]):
                    sp = _softplus(z)
                    lom = -sp
                    if masked:
                        lom = jnp.where(keeps[j], lom, 0.0)
                    lbs[h][j] = z - sp
                    scans[h][j] = _scan_rows(lom, tri_p)
            pc, hc = list(pc), list(hc)
            avs = [[None] * KEY_GROUP for _ in heads]
            gws = [[None] * KEY_GROUP for _ in heads]
            hscans = [[None] * KEY_GROUP for _ in heads]
            for h in heads:
                for j in range(KEY_GROUP):
                    p_in, p_tot = scans[h][j]
                    a = jnp.exp(lbs[h][j] + (tq[h] - pc[h] - p_in))
                    if masked:
                        a = jnp.where(keeps[j], a, 0.0)
                    pc[h] = pc[h] + p_tot
                    gw = das[j][:, h * BLK:(h + 1) * BLK] * a
                    avs[h][j] = a.astype(BF16)
                    gws[h][j] = gw
                    hscans[h][j] = _scan_rows(gw, tri_h)
            dzs = [[None] * KEY_GROUP for _ in heads]
            for h in heads:
                for j in range(KEY_GROUP):
                    h_in, g_tot = hscans[h][j]
                    gw = gws[h][j]
                    dz = gw - jnp.exp(lbs[h][j]) * (gw + hc[h] + h_in)
                    if masked:
                        dz = jnp.where(keeps[j], dz, 0.0)
                    hc[h] = hc[h] + g_tot
                    dzs[h][j] = dz.astype(BF16)
            for j, both in enumerate(both_of):
                dq = dq + _dot(jnp.concatenate([dzs[h][j] for h in heads], axis=1), kd_s[both, :])
            for h in heads:
                for j, cols in enumerate(cols_of):
                    dkv_s[h, cols, :] += _dot_tn(jnp.concatenate([dzs[h][j], avs[h][j]], axis=0), qd[h])
            return tuple(pc), tuple(hc), dq

        def q_block(qi, dqg):
            rows = pl.ds(pl.multiple_of(qi * BLK, BLK), BLK)
            both = pl.ds(pl.multiple_of(qi * 2 * BLK, 2 * BLK), 2 * BLK)
            qc = [qc_s[h, rows, :] for h in heads]
            qd = [qd_s[h, both, :] for h in heads]
            dob = dob_s[rows, :]
            tboth = t_ref[rows, :]
            tq = [jnp.concatenate([tboth[:, h * HEAD_DIM:(h + 1) * HEAD_DIM]] * 2, axis=1) for h in heads]
            zero = (jnp.zeros((BLK, BLK), F32),) * HEADS_PER_STEP
            top = qi // KEY_GROUP
            pc, hc, dq = lax.fori_loop(
                0, top, lambda grp, c: group(qc, qd, dob, tq, qi, grp, c[0], c[1], c[2], False),
                (zero, zero, jnp.zeros((BLK, LANES), F32)))
            _, _, dq = group(qc, qd, dob, tq, qi, top, pc, hc, dq, True)
            dq_raw, dg = _pair_norm_bwd(q_ref[rows, :], qg_ref[...], dq * scale, first)
            dq_ref[rows, :] = dq_raw.astype(BF16)
            return dqg + dg

        dqg = lax.fori_loop(0, n_blk, q_block, jnp.zeros((1, LANES), F32))

        def finish(i, dkg):
            rows = pl.ds(pl.multiple_of(i * BLK, BLK), BLK)
            dk = jnp.concatenate([dkv_s[h, rows, 0:HEAD_DIM] for h in heads], axis=1)
            dv = jnp.concatenate([dkv_s[h, rows, HEAD_DIM:2 * HEAD_DIM] for h in heads], axis=1)
            dk_raw, dg = _pair_norm_bwd(k_ref[rows, :], kg_ref[...], dk, first)
            dk_ref[rows, :] = dk_raw.astype(BF16)
            dv_ref[rows, :] = dv.astype(BF16)
            return dkg + dg

        dkg = lax.fori_loop(0, n_blk, finish, jnp.zeros((1, LANES), F32))
        dqg_ref[0:1, :] += dqg
        dkg_ref[0:1, :] += dkg

        @pl.when(pl.program_id(0) == pairs - 1)
        def _():
            for cp in copies:
                cp.wait()

    col = lambda off: pl.BlockSpec((s, LANES), lambda p: (0, off + p))
    vec = pl.BlockSpec((1, LANES), lambda p: (0, 0))
    small = pl.BlockSpec((8, LANES), lambda p: (0, 0))
    out = pl.pallas_call(
        body, name=name, grid=(pairs,),
        in_specs=[col(0), col(pairs), col(2 * pairs), col(0), col(0), vec, vec] + [ANY_SPEC] * n_send,
        out_specs=[col(0)] * 3 + [small] * 2 + [ANY_SPEC] * n_send,
        out_shape=[jax.ShapeDtypeStruct((s, ATTN_WIDTH), BF16)] * 3 + [jax.ShapeDtypeStruct((8, LANES), F32)] * 2
        + _scattered_shapes(sends),
        scratch_shapes=[pltpu.VMEM((HEADS_PER_STEP, s, 4 * HEAD_DIM), BF16)] * 2
        + [pltpu.VMEM((HEADS_PER_STEP * s, LANES), BF16)] * 2
        + [pltpu.VMEM((HEADS_PER_STEP, HEADS_PER_STEP * s, LANES), BF16), pltpu.VMEM((s, LANES), BF16),
           pltpu.VMEM((HEADS_PER_STEP, s, LANES), F32)] + _exchange_sems(n_send),
        compiler_params=_params(1),
    )(proj, proj, proj, dcat, tsum, qg, kg, *sends)
    return out[:5], out[5:]


CONV_ROWS = 128


def _shifted(window, shift, halo):
    if shift == 0:
        return window[halo:, :]
    return pltpu.roll(window, shift, 0)[halo:, :]


SUBLANES = 8


def _row_shifts(window, up):
    n = window.shape[0]
    return [window] + [pltpu.roll(window, n - b if up else b, 0) for b in range(1, SUBLANES)]


def _earlier(shifts, back, rows):
    a, b = divmod(back, SUBLANES)
    return shifts[b][CONV_HALO - SUBLANES * a:CONV_HALO - SUBLANES * a + rows, :]


def _later(shifts, ahead, rows):
    a, b = divmod(ahead, SUBLANES)
    return shifts[b][SUBLANES * a:SUBLANES * a + rows, :]


def _conv_taps(shifts, w_ref, rows):
    y = None
    for k in range(CONV_KERNEL):
        term = _earlier(shifts, CONV_KERNEL - 1 - k, rows) * w_ref[k:k + 1, :]
        y = term if y is None else y + term
    return y


def _conv_fwd(proj, w, b, lg, lb, name):
    s = proj.shape[0]
    cw = w.shape[1]
    rows = CONV_ROWS
    blk_a = (proj.shape[1] - 2 * cw) // cw

    def body(a_ref, g_ref, w_ref, b_ref, lg_ref, lb_ref, o_ref, y_ref, u_s):
        u_s[0:CONV_HALO, :] = jnp.zeros((CONV_HALO, cw), F32)

        def glu(i, _):
            r0 = pl.multiple_of(i * rows, rows)
            u_s[pl.ds(CONV_HALO + r0, rows), :] = a_ref[pl.ds(r0, rows), :] * _sigmoid(g_ref[pl.ds(r0, rows), :])
            return 0

        lax.fori_loop(0, s // rows, glu, 0)

        def chunk(i, _):
            r0 = pl.multiple_of(i * rows, rows)
            y = _conv_taps(_row_shifts(u_s[pl.ds(r0, CONV_HALO + rows), :], False), w_ref, rows) + b_ref[...]
            y_ref[pl.ds(r0, rows), :] = y
            yc = y - jnp.mean(y, axis=-1, keepdims=True)
            n = yc * lax.rsqrt(jnp.mean(yc * yc, axis=-1, keepdims=True) + EPS)
            ln = n * lg_ref[...] + lb_ref[...]
            o_ref[pl.ds(r0, rows), :] = (ln * _sigmoid(ln)).astype(BF16)
            return 0

        lax.fori_loop(0, s // rows, chunk, 0)

    vec = pl.BlockSpec((1, cw), lambda i: (0, 0))
    return pl.pallas_call(
        body, name=name, grid=(1,),
        in_specs=[pl.BlockSpec((s, cw), lambda i: (0, blk_a)), pl.BlockSpec((s, cw), lambda i: (0, blk_a + 1)),
                  pl.BlockSpec((CONV_KERNEL, cw), lambda i: (0, 0)), vec, vec, vec],
        out_specs=[pl.BlockSpec((s, cw), lambda i: (0, 0))] * 2,
        out_shape=[jax.ShapeDtypeStruct((s, cw), BF16), jax.ShapeDtypeStruct((s, cw), F32)],
        scratch_shapes=[pltpu.VMEM((CONV_HALO + s, cw), F32)],
        compiler_params=_params(1),
    )(proj, proj, w, b, lg, lb)


def _conv_bwd(proj, y, dcat, w, lg, lb, sends, name):
    s = proj.shape[0]
    cw = w.shape[1]
    rows = CONV_ROWS
    blk_a = (proj.shape[1] - 2 * cw) // cw
    n_chunk = s // rows
    n_send = len(sends)

    def body(a_ref, g_ref, y_ref, dc_ref, w_ref, lg_ref, lb_ref, *rest):
        send_refs, (o_ref, dw_ref, db_ref, dlg_ref, dlb_ref), rest = rest[:n_send], rest[n_send:n_send + 5], rest[n_send + 5:]
        got_refs, (u_s, dy_s, dw_s), sems = rest[:n_send], rest[n_send:n_send + 3], rest[n_send + 3:]
        copies = _scatter_copies(send_refs, got_refs, sems)
        for cp in copies:
            cp.start()
        u_s[0:CONV_HALO, :] = jnp.zeros((CONV_HALO, cw), F32)
        dy_s[pl.ds(s, CONV_HALO), :] = jnp.zeros((CONV_HALO, cw), F32)
        dw_s[...] = jnp.zeros_like(dw_s)

        def glu(i, _):
            r0 = pl.multiple_of(i * rows, rows)
            u_s[pl.ds(CONV_HALO + r0, rows), :] = a_ref[pl.ds(r0, rows), :] * _sigmoid(g_ref[pl.ds(r0, rows), :])
            return 0

        lax.fori_loop(0, n_chunk, glu, 0)

        def chunk(i, carry):
            db, dlg, dlb = carry
            r0 = pl.multiple_of(i * rows, rows)
            shifts = _row_shifts(u_s[pl.ds(r0, CONV_HALO + rows), :], False)
            y = y_ref[pl.ds(r0, rows), :]
            yc = y - jnp.mean(y, axis=-1, keepdims=True)
            r = lax.rsqrt(jnp.mean(yc * yc, axis=-1, keepdims=True) + EPS)
            n = yc * r
            ln = n * lg_ref[...] + lb_ref[...]
            sg = _sigmoid(ln)
            dln = dc_ref[pl.ds(r0, rows), :] * (sg * (1.0 + ln * (1.0 - sg)))
            dn = dln * lg_ref[...]
            dy = r * (dn - jnp.mean(dn, axis=-1, keepdims=True) - n * jnp.mean(dn * n, axis=-1, keepdims=True))
            dy_s[pl.ds(r0, rows), :] = dy
            for k in range(CONV_KERNEL):
                prod = _earlier(shifts, CONV_KERNEL - 1 - k, rows) * dy
                dw_s[k] += jnp.sum(prod.reshape(rows // 8, 8, cw), axis=0)
            return (db + jnp.sum(dy, axis=0, keepdims=True),
                    dlg + jnp.sum(dln * n, axis=0, keepdims=True),
                    dlb + jnp.sum(dln, axis=0, keepdims=True))

        zero = jnp.zeros((1, cw), F32)
        db, dlg, dlb = lax.fori_loop(0, n_chunk, chunk, (zero, zero, zero))
        db_ref[...] = db
        dlg_ref[...] = dlg
        dlb_ref[...] = dlb
        for k in range(CONV_KERNEL):
            dw_ref[k:k + 1, :] = jnp.sum(dw_s[k], axis=0, keepdims=True)

        def back(i, _):
            r0 = pl.multiple_of(i * rows, rows)
            shifts = _row_shifts(dy_s[pl.ds(r0, rows + CONV_HALO), :], True)
            du = None
            for k in range(CONV_KERNEL):
                term = _later(shifts, CONV_KERNEL - 1 - k, rows) * w_ref[k:k + 1, :]
                du = term if du is None else du + term
            av = a_ref[pl.ds(r0, rows), :]
            sg = _sigmoid(g_ref[pl.ds(r0, rows), :])
            o_ref[pl.ds(r0, rows), 0:cw] = (du * sg).astype(BF16)
            o_ref[pl.ds(r0, rows), cw:2 * cw] = (du * av * sg * (1.0 - sg)).astype(BF16)
            return 0

        lax.fori_loop(0, n_chunk, back, 0)
        for cp in copies:
            cp.wait()

    vec = pl.BlockSpec((1, cw), lambda i: (0, 0))
    wspec = pl.BlockSpec((CONV_KERNEL, cw), lambda i: (0, 0))
    out = pl.pallas_call(
        body, name=name, grid=(1,),
        in_specs=[pl.BlockSpec((s, cw), lambda i: (0, blk_a)), pl.BlockSpec((s, cw), lambda i: (0, blk_a + 1)),
                  pl.BlockSpec((s, cw), lambda i: (0, 0)), pl.BlockSpec((s, cw), lambda i: (0, 1)), wspec, vec, vec]
        + [ANY_SPEC] * n_send,
        out_specs=[pl.BlockSpec((s, 2 * cw), lambda i: (0, 0)), wspec, vec, vec, vec] + [ANY_SPEC] * n_send,
        out_shape=[jax.ShapeDtypeStruct((s, 2 * cw), BF16), jax.ShapeDtypeStruct((CONV_KERNEL, cw), F32)]
        + [jax.ShapeDtypeStruct((1, cw), F32)] * 3 + _scattered_shapes(sends),
        scratch_shapes=[pltpu.VMEM((CONV_HALO + s, cw), F32), pltpu.VMEM((s + CONV_HALO, cw), F32),
                        pltpu.VMEM((CONV_KERNEL, 8, cw), F32)] + _exchange_sems(n_send),
        compiler_params=_params(1),
    )(proj, proj, y, dcat, w, lg, lb, *sends)
    return out[:5], out[5:]


FFN_ROWS = 256


def _ffn_gate(g_ref, r0, rows, w_ref, b_ref):
    cur = g_ref[pl.ds(r0, rows), :].astype(F32)
    prev = g_ref[pl.ds(pl.multiple_of(jnp.maximum(r0 - FFN_HALO, 0), FFN_HALO), FFN_HALO), :].astype(F32)
    prev = jnp.where(r0 > 0, prev, 0.0)
    window = jnp.concatenate([prev, cur], axis=0)
    gc = cur * w_ref[FFN_KERNEL - 1:FFN_KERNEL, :] + b_ref[...]
    for k in range(FFN_KERNEL - 1):
        gc = gc + _shifted(window, FFN_KERNEL - 1 - k, FFN_HALO) * w_ref[k:k + 1, :]
    return gc, window


def _ffn_fwd(up, w, b, name):
    s = up.shape[0]
    f = w.shape[1]
    tc = _pick(f, (256, 128))
    nc = f // tc
    rows = _pick(s, (FFN_ROWS, 128))

    def body(g_ref, v_ref, w_ref, b_ref, o_ref):
        def chunk(i, _):
            r0 = pl.multiple_of(i * rows, rows)
            gc, _w = _ffn_gate(g_ref, r0, rows, w_ref, b_ref)
            o_ref[pl.ds(r0, rows), :] = (gc * _sigmoid(gc) * v_ref[pl.ds(r0, rows), :].astype(F32)).astype(BF16)
            return 0

        lax.fori_loop(0, s // rows, chunk, 0)

    return pl.pallas_call(
        body, name=name, grid=(nc,),
        in_specs=[pl.BlockSpec((s, tc), lambda j: (0, j)), pl.BlockSpec((s, tc), lambda j: (0, nc + j)),
                  pl.BlockSpec((FFN_KERNEL, tc), lambda j: (0, j)), pl.BlockSpec((1, tc), lambda j: (0, j))],
        out_specs=pl.BlockSpec((s, tc), lambda j: (0, j)),
        out_shape=jax.ShapeDtypeStruct((s, f), BF16),
        compiler_params=_params(1),
    )(up, up, w, b)


def _ffn_bwd(up, dact, w, b, name):
    s = up.shape[0]
    f = w.shape[1]
    tc = _pick(f, (256, 128))
    nc = f // tc
    rows = _pick(s, (FFN_ROWS, 128))
    n_chunk = s // rows

    def body(g_ref, v_ref, da_ref, w_ref, b_ref, d_ref, dw_ref, db_ref, dgc_s):
        dg_ref, dv_ref = d_ref.at[0], d_ref.at[1]
        dgc_s[pl.ds(s, FFN_HALO), :] = jnp.zeros((FFN_HALO, tc), F32)

        def chunk(i, carry):
            r0 = pl.multiple_of(i * rows, rows)
            gc, window = _ffn_gate(g_ref, r0, rows, w_ref, b_ref)
            sg = _sigmoid(gc)
            da = da_ref[pl.ds(r0, rows), :].astype(F32)
            dv_ref[pl.ds(r0, rows), :] = (da * gc * sg).astype(BF16)
            dgc = da * v_ref[pl.ds(r0, rows), :].astype(F32) * (sg * (1.0 + gc * (1.0 - sg)))
            dgc_s[pl.ds(r0, rows), :] = dgc
            out = [carry[0] + jnp.sum(dgc, axis=0, keepdims=True)]
            for k in range(FFN_KERNEL):
                out.append(carry[1 + k] + jnp.sum(_shifted(window, FFN_KERNEL - 1 - k, FFN_HALO) * dgc,
                                                  axis=0, keepdims=True))
            return tuple(out)

        zero = jnp.zeros((1, tc), F32)
        sums = lax.fori_loop(0, n_chunk, chunk, (zero,) * (1 + FFN_KERNEL))
        db_ref[...] = sums[0]
        for k in range(FFN_KERNEL):
            dw_ref[k:k + 1, :] = sums[1 + k]

        def back(i, _):
            r0 = pl.multiple_of(i * rows, rows)
            window = dgc_s[pl.ds(r0, rows + FFN_HALO), :]
            dg = window[:rows, :] * w_ref[FFN_KERNEL - 1:FFN_KERNEL, :]
            for k in range(FFN_KERNEL - 1):
                sh = FFN_KERNEL - 1 - k
                dg = dg + pltpu.roll(window, rows + FFN_HALO - sh, 0)[:rows, :] * w_ref[k:k + 1, :]
            dg_ref[pl.ds(r0, rows), :] = dg.astype(BF16)
            return 0

        lax.fori_loop(0, n_chunk, back, 0)

    blk = lambda off: pl.BlockSpec((s, tc), lambda j: (0, off + j))
    return pl.pallas_call(
        body, name=name, grid=(nc,),
        in_specs=[blk(0), blk(nc), blk(0), pl.BlockSpec((FFN_KERNEL, tc), lambda j: (0, j)),
                  pl.BlockSpec((1, tc), lambda j: (0, j))],
        out_specs=[pl.BlockSpec((2, s, tc), lambda j: (0, 0, j)), pl.BlockSpec((FFN_KERNEL, tc), lambda j: (0, j)),
                   pl.BlockSpec((1, tc), lambda j: (0, j))],
        out_shape=[jax.ShapeDtypeStruct((2, s, f), BF16),
                   jax.ShapeDtypeStruct((FFN_KERNEL, f), F32), jax.ShapeDtypeStruct((1, f), F32)],
        scratch_shapes=[pltpu.VMEM((s + FFN_HALO, tc), F32)],
        compiler_params=_params(1),
    )(up, up, dact, w, b)


def _loss_head(y, target, name):
    m, n = y.shape
    tm = _pick(m, (256, 128))

    def body(y_ref, t_ref, l_ref, d_ref, db_ref):
        e = y_ref[...] - t_ref[...]
        part = 0.5 * jnp.sum(jnp.sum(e * e, axis=-1, keepdims=True) / n, axis=0, keepdims=True)

        @pl.when(pl.program_id(0) == 0)
        def _():
            l_ref[...] = jnp.zeros_like(l_ref)

        l_ref[...] += part
        d = e / n
        d_ref[...] = d
        db_ref[...] = d.astype(BF16)

    return pl.pallas_call(
        body, name=name, grid=(m // tm,),
        in_specs=[pl.BlockSpec((tm, n), lambda i: (i, 0))] * 2,
        out_specs=[pl.BlockSpec((8, LANES), lambda i: (0, 0)), pl.BlockSpec((tm, n), lambda i: (i, 0)),
                   pl.BlockSpec((tm, n), lambda i: (i, 0))],
        out_shape=[jax.ShapeDtypeStruct((8, LANES), F32), jax.ShapeDtypeStruct((m, n), F32),
                   jax.ShapeDtypeStruct((m, n), BF16)],
        compiler_params=_params(1),
    )(y, target)


def _adamw_math(w, g, m, v):
    m = ADAM_B1 * m + (1.0 - ADAM_B1) * g
    v = ADAM_B2 * v + (1.0 - ADAM_B2) * (g * g)
    m_hat = m / (1.0 - ADAM_B1 ** ADAM_STEP)
    v_hat = v / (1.0 - ADAM_B2 ** ADAM_STEP)
    delta = -ADAM_LR * (m_hat / (jnp.sqrt(v_hat) + ADAM_EPS) + ADAM_WD * w)
    return delta, m, v


def _sum_adamw(parts, w, m, v, sends, name):
    depth, r, c = w.shape
    tr = max(t for t in range(16, min(r, 192) + 1, 16) if r % t == 0)
    steps = r // tr
    n_send = len(sends)

    def body(*refs):
        p_refs, refs = refs[:depth], refs[depth:]
        (w_ref, m_ref, v_ref), send_refs, refs = refs[:3], refs[3:3 + n_send], refs[3 + n_send:]
        (g_out, d_out, m_out, v_out), got_refs, sems = refs[:4], refs[4:4 + n_send], refs[4 + n_send:]
        is_first = (pl.program_id(0) == 0) & (pl.program_id(1) == 0)
        is_last = (pl.program_id(0) == depth - 1) & (pl.program_id(1) == steps - 1)
        if n_send:
            start, relay, finish = _two_level_gather(send_refs, got_refs, sems)
            pl.when(is_first)(start)
            pl.when(is_last)(relay)
        for layer in range(depth):
            @pl.when(pl.program_id(0) == layer)
            def _():
                g = p_refs[layer][0].astype(F32)
                for src in range(1, N_DEV):
                    g = g + p_refs[layer][src].astype(F32)
                d, mn, vn = _adamw_math(w_ref[0], g, m_ref[0], v_ref[0])
                g_out[0] = g
                d_out[0] = d
                m_out[0] = mn
                v_out[0] = vn

        if n_send:
            pl.when(is_last)(finish)

    def part_spec(layer):
        return pl.BlockSpec((N_DEV, tr, c), lambda l, i: (0, jnp.clip((l - layer) * steps + i, 0, steps - 1), 0))

    blk = pl.BlockSpec((1, tr, c), lambda l, i: (l, i, 0))
    out = pl.pallas_call(
        body, name=name, grid=(depth, steps),
        in_specs=[part_spec(layer) for layer in range(depth)] + [blk, blk, blk] + [ANY_SPEC] * n_send,
        out_specs=[blk] * 4 + [ANY_SPEC] * n_send,
        out_shape=[jax.ShapeDtypeStruct(w.shape, F32)] * 4 + _gathered_shapes(sends),
        scratch_shapes=_exchange_sems(n_send) if n_send else [],
        compiler_params=_params(2),
    )(*parts, w, m, v, *sends)
    return out[:4], out[4:]


VMEM_SPEC = pl.BlockSpec(memory_space=pltpu.VMEM)


def _sum_small(parts, name):
    n = len(parts)

    def body(*refs):
        for p_ref, o_ref in zip(refs[:n], refs[n:]):
            g = p_ref[0]
            for src in range(1, N_DEV):
                g = g + p_ref[src]
            o_ref[...] = g

    return pl.pallas_call(
        body, name=name, in_specs=[VMEM_SPEC] * n, out_specs=[VMEM_SPEC] * n,
        out_shape=[jax.ShapeDtypeStruct(p.shape[1:], F32) for p in parts],
        compiler_params=_params(),
    )(*parts)


def _adamw_small(ws, gs, ms, vs, name):
    n = len(ws)

    def body(*refs):
        ins, outs = refs[:4 * n], refs[4 * n:]
        for i in range(n):
            d, mn, vn = _adamw_math(ins[i][...], ins[n + i][...], ins[2 * n + i][...], ins[3 * n + i][...])
            outs[i][...] = d
            outs[n + i][...] = mn
            outs[2 * n + i][...] = vn

    out = pl.pallas_call(
        body, name=name, in_specs=[VMEM_SPEC] * (4 * n), out_specs=[VMEM_SPEC] * (3 * n),
        out_shape=[jax.ShapeDtypeStruct(w.shape, F32) for w in ws] * 3,
        compiler_params=_params(),
    )(*ws, *gs, *ms, *vs)
    return out[:n], out[n:2 * n], out[2 * n:]


def kernel(x, norm1_g, w_in, q_norm_g, k_norm_g, conv_dw_w, conv_dw_b, conv_ln_g, conv_ln_b, w_out, norm2_g, w_up, ffn_dw_w, ffn_dw_b, w_down, loss_target, m_norm1_g, m_w_in, m_q_norm_g, m_k_norm_g, m_conv_dw_w, m_conv_dw_b, m_conv_ln_g, m_conv_ln_b, m_w_out, m_norm2_g, m_w_up, m_ffn_dw_w, m_ffn_dw_b, m_w_down, v_norm1_g, v_w_in, v_q_norm_g, v_k_norm_g, v_conv_dw_w, v_conv_dw_b, v_conv_ln_g, v_conv_ln_b, v_w_out, v_norm2_g, v_w_up, v_ffn_dw_w, v_ffn_dw_b, v_w_down):
    depth, d_model, in_shard = w_in.shape
    out_shard = w_out.shape[1]
    up_shard = w_up.shape[2]
    down_shard = w_down.shape[1]
    d_ff = down_shard * N_DEV
    conv_w = conv_dw_b.shape[1]
    cw_shard = conv_dw_w.shape[2]
    fw_shard = ffn_dw_w.shape[2]
    me = 4 * lax.axis_index("x") + 2 * lax.axis_index("y") + lax.axis_index("c")

    transposed = lambda a: a.transpose(0, 2, 1)
    b_in, b_out, b_up, b_down = (transposed(w_in).astype(BF16), w_out.astype(BF16), transposed(w_up).astype(BF16),
                                 w_down.astype(BF16))
    rows_major = lambda g: g.reshape(N_DEV * g.shape[1], g.shape[2])
    g_in0, g_cw, g_fw = _gather([b_in[0], conv_dw_w, ffn_dw_w], name="gather_first")
    wf_in, wf_out, wf_up, wf_down = [rows_major(g_in0)] + [None] * (depth - 1), [None] * depth, [None] * depth, [None] * depth
    cwf = g_cw.transpose(1, 2, 0, 3).reshape(depth, CONV_KERNEL, conv_w)
    fwf = g_fw.transpose(1, 2, 0, 3).reshape(depth, FFN_KERNEL, d_ff)

    row = lambda a, l: a[l].reshape(1, -1)
    both_heads = lambda a, l: jnp.tile(row(a, l), (1, HEADS_PER_STEP))

    xs = x[0]
    saved = []
    for l in range(depth):
        proj, h1, _ = _mm_rms(xs, row(norm1_g, l), wf_in[l], F32, [], name="fwd_in")
        sends = [b_out[l], b_up[l]] + ([b_in[l + 1]] if l + 1 < depth else [])
        attn, tsum, got = _attn_fwd(proj, both_heads(q_norm_g, l), both_heads(k_norm_g, l), sends, name="fwd_attn")
        wf_out[l], wf_up[l] = rows_major(got[0]), rows_major(got[1])
        if l + 1 < depth:
            wf_in[l + 1] = rows_major(got[2])
        conv, conv_y = _conv_fwd(proj, cwf[l], row(conv_dw_b, l), row(conv_ln_g, l), row(conv_ln_b, l),
                                 name="fwd_conv")
        cat = jnp.concatenate([attn, conv], axis=1)
        x_mid = _mm_res(cat, wf_out[l], xs, name="fwd_out")
        up, h2, got = _mm_rms(x_mid, row(norm2_g, l), wf_up[l], BF16, [b_down[l]], name="fwd_up")
        wf_down[l] = rows_major(got[0])
        act = _ffn_fwd(up, fwf[l], row(ffn_dw_b, l), name="fwd_ffn")
        x_next = _mm_res(act, wf_down[l], x_mid, name="fwd_down")
        saved.append((xs, h1, proj, tsum, cat, x_mid, h2, up, act, conv_y))
        xs = x_next

    loss_tile, dx, dxb = _loss_head(xs, loss_target[0], name="loss_head")
    loss = lax.psum(loss_tile[0, 0], ("x", "y", "c"))

    r_in, r_out, r_up, r_down = [None] * depth, [None] * depth, [None] * depth, [None] * depth
    row_blocks = lambda g: g.reshape(N_DEV, g.shape[0] // N_DEV, g.shape[1])
    small = {k: [None] * depth for k in ("norm1_g", "q_norm_g", "k_norm_g", "conv_dw_w", "conv_dw_b", "conv_ln_g",
                                         "conv_ln_b", "norm2_g", "ffn_dw_w", "ffn_dw_b")}
    gw_in = None
    for l in reversed(range(depth)):
        xs, h1, proj, tsum, cat, x_mid, h2, up, act, conv_y = saved[l]
        dact = _mm_nt(dxb, wf_down[l], BF16, name="bwd_dact")
        gw_down = _mm_tn(act, dxb, name="bwd_gw_down")
        dup, small["ffn_dw_w"][l], small["ffn_dw_b"][l] = _ffn_bwd(up, dact, fwf[l], row(ffn_dw_b, l), name="bwd_ffn")
        gw_up = _mm_tn(dup, h2, name="bwd_gw_up")
        dx, dxb, small["norm2_g"][l], _ = _mm_rmsbwd(dup, wf_up[l], x_mid, row(norm2_g, l), dx, [], name="bwd_up")
        dcat = _mm_nt(dxb, wf_out[l], F32, name="bwd_dcat")
        gw_out = _mm_tn(cat, dxb, name="bwd_gw_out")
        (dglu, small["conv_dw_w"][l], small["conv_dw_b"][l], small["conv_ln_g"][l], small["conv_ln_b"][l]), got = (
            _conv_bwd(proj, conv_y, dcat, cwf[l], row(conv_ln_g, l), row(conv_ln_b, l), [row_blocks(gw_out)],
                      name="bwd_conv"))
        r_out[l] = got[0]
        sends = [row_blocks(gw_down), row_blocks(gw_up)] + ([row_blocks(gw_in)] if l + 1 < depth else [])
        (dq, dk, dv, dqg, dkg), got = _attn_bwd(proj, dcat, tsum, both_heads(q_norm_g, l), both_heads(k_norm_g, l),
                                                sends, name="bwd_attn")
        r_down[l], r_up[l] = got[:2]
        if l + 1 < depth:
            r_in[l + 1] = got[2]
        small["q_norm_g"][l] = dqg[0:1, :HEAD_DIM] + dqg[0:1, HEAD_DIM:]
        small["k_norm_g"][l] = dkg[0:1, :HEAD_DIM] + dkg[0:1, HEAD_DIM:]
        dproj = jnp.concatenate([dq, dk, dv, dglu], axis=1)
        gw_in = _mm_tn(dproj, h1, name="bwd_gw_in")
        sends = [row_blocks(gw_in)] if l == 0 else []
        dx, dxb, small["norm1_g"][l], got = _mm_rmsbwd(dproj, wf_in[l], xs, row(norm1_g, l), dx, sends, name="bwd_in")
        if l == 0:
            r_in[0] = got[0]
    grad_x = dx[None]

    names = ["norm1_g", "q_norm_g", "k_norm_g", "conv_dw_w", "conv_dw_b", "conv_ln_g", "conv_ln_b", "norm2_g",
             "ffn_dw_w", "ffn_dw_b"]
    full_shapes = {"norm1_g": norm1_g.shape, "q_norm_g": q_norm_g.shape, "k_norm_g": k_norm_g.shape,
                   "conv_dw_w": (depth, CONV_KERNEL, conv_w), "conv_dw_b": conv_dw_b.shape,
                   "conv_ln_g": conv_ln_g.shape, "conv_ln_b": conv_ln_b.shape, "norm2_g": norm2_g.shape,
                   "ffn_dw_w": (depth, FFN_KERNEL, d_ff), "ffn_dw_b": ffn_dw_b.shape}
    partial = [jnp.stack(small[k]).reshape(full_shapes[k]) for k in names]
    big = {}
    big["w_out"], all_partials = _sum_adamw(r_out, w_out, m_w_out, v_w_out, partial, name="adamw_out")
    big["w_up"], _ = _sum_adamw(r_up, transposed(w_up), transposed(m_w_up), transposed(v_w_up), [], name="adamw_up")
    big["w_down"], _ = _sum_adamw(r_down, w_down, m_w_down, v_w_down, [], name="adamw_down")
    big["w_in"], _ = _sum_adamw(r_in, transposed(w_in), transposed(m_w_in), transposed(v_w_in), [], name="adamw_in")
    for k in ("w_in", "w_up"):
        big[k] = [transposed(a) for a in big[k]]

    grads = dict(zip(names, _sum_small(all_partials, name="sum_small_grads")))
    grads["conv_dw_w"] = lax.dynamic_slice_in_dim(grads["conv_dw_w"], me * cw_shard, cw_shard, axis=2)
    grads["ffn_dw_w"] = lax.dynamic_slice_in_dim(grads["ffn_dw_w"], me * fw_shard, fw_shard, axis=2)
    weights = dict(norm1_g=norm1_g, q_norm_g=q_norm_g, k_norm_g=k_norm_g, conv_dw_w=conv_dw_w, conv_dw_b=conv_dw_b,
                   conv_ln_g=conv_ln_g, conv_ln_b=conv_ln_b, norm2_g=norm2_g, ffn_dw_w=ffn_dw_w, ffn_dw_b=ffn_dw_b)
    m_in = dict(norm1_g=m_norm1_g, q_norm_g=m_q_norm_g, k_norm_g=m_k_norm_g, conv_dw_w=m_conv_dw_w,
                conv_dw_b=m_conv_dw_b, conv_ln_g=m_conv_ln_g, conv_ln_b=m_conv_ln_b, norm2_g=m_norm2_g,
                ffn_dw_w=m_ffn_dw_w, ffn_dw_b=m_ffn_dw_b)
    v_in = dict(norm1_g=v_norm1_g, q_norm_g=v_q_norm_g, k_norm_g=v_k_norm_g, conv_dw_w=v_conv_dw_w,
                conv_dw_b=v_conv_dw_b, conv_ln_g=v_conv_ln_g, conv_ln_b=v_conv_ln_b, norm2_g=v_norm2_g,
                ffn_dw_w=v_ffn_dw_w, ffn_dw_b=v_ffn_dw_b)
    d_s, m_s, v_s = _adamw_small([weights[k] for k in names], [grads[k] for k in names], [m_in[k] for k in names],
                                 [v_in[k] for k in names], name="adamw_small")
    delta, new_m, new_v = dict(zip(names, d_s)), dict(zip(names, m_s)), dict(zip(names, v_s))
    for k, (g, d, mn, vn) in big.items():
        grads[k], delta[k], new_m[k], new_v[k] = g, d, mn, vn

    order = ["norm1_g", "w_in", "q_norm_g", "k_norm_g", "conv_dw_w", "conv_dw_b", "conv_ln_g", "conv_ln_b", "w_out",
             "norm2_g", "w_up", "ffn_dw_w", "ffn_dw_b", "w_down"]
    return (loss, grad_x, *[grads[k] for k in order], *[delta[k] for k in order], *[new_m[k] for k in order],
            *[new_v[k] for k in order])
```

```python
import functools

import jax
import jax.numpy as jnp
from jax import lax
from jax.experimental import pallas as pl
from jax.experimental.pallas import tpu as pltpu

F32 = jnp.float32
BF16 = jnp.bfloat16

N_DEV = 8
HEADS = 8
HEAD_DIM = 64
ATTN_WIDTH = HEADS * HEAD_DIM
CONV_KERNEL = 31
FFN_KERNEL = 3
EPS = 1e-6
BLK = 128
KEY_GROUP = 4
LANES = 128
NORM_ROWS = 128
CONV_HALO = 32
FFN_HALO = 16

ADAM_LR = 0.001
ADAM_B1 = 0.9
ADAM_B2 = 0.999
ADAM_EPS = 1e-08
ADAM_WD = 0.01
ADAM_STEP = 10

VMEM_LIMIT = 56 * 1024 * 1024


def _params(n_axes=0):
    kw = dict(vmem_limit_bytes=VMEM_LIMIT)
    if n_axes:
        kw["dimension_semantics"] = ("arbitrary",) * n_axes
    return pltpu.CompilerParams(**kw)


def _dot(a, b):
    return jnp.dot(a, b, preferred_element_type=F32)


def _dot_nt(a, b):
    return lax.dot_general(a, b, (((1,), (1,)), ((), ())), preferred_element_type=F32)


def _dot_tn(a, b):
    return lax.dot_general(a, b, (((0,), (0,)), ((), ())), preferred_element_type=F32)


def _sigmoid(x):
    return 1.0 / (1.0 + jnp.exp(-x))


def _split_bf16(x):
    hi = x.astype(BF16)
    lo = (x - hi.astype(F32)).astype(BF16)
    return hi, lo


def _pick(n, options):
    for t in options:
        if n % t == 0:
            return t
    return n


def _tile(n, cap):
    best = None
    for t in range(LANES, min(n, cap) + 1, LANES):
        if n % t == 0:
            best = t
    return best or n


def _mm_rms(x, g, wt, out_dtype, sends, name):
    m, k = x.shape
    n = wt.shape[0]
    tm = _tile(m, 2048)
    tn = _tile(n, 512)
    n_send = len(sends)
    grid = (m // tm, n // tn)

    def body(x_ref, g_ref, w_ref, *rest):
        send_refs, (o_ref, h_ref), rest = rest[:n_send], rest[n_send:n_send + 2], rest[n_send + 2:]
        got_refs, h_s, sems = rest[:n_send], rest[n_send], rest[n_send + 1:]
        if n_send:
            start, relay, finish = _two_level_gather(send_refs, got_refs, sems)
            is_first = (pl.program_id(0) == 0) & (pl.program_id(1) == 0)
            is_last = (pl.program_id(0) == grid[0] - 1) & (pl.program_id(1) == grid[1] - 1)
            pl.when(is_first)(start)
            pl.when(is_last)(relay)

        @pl.when(pl.program_id(1) == 0)
        def _():
            def chunk(c, _):
                rows = pl.ds(pl.multiple_of(c * NORM_ROWS, NORM_ROWS), NORM_ROWS)
                xv = x_ref[rows, :]
                r = lax.rsqrt(jnp.mean(xv * xv, axis=-1, keepdims=True) + EPS)
                hv = (xv * r * g_ref[...]).astype(BF16)
                h_s[rows, :] = hv
                h_ref[rows, :] = hv
                return 0

            lax.fori_loop(0, tm // NORM_ROWS, chunk, 0)

        o_ref[...] = _dot_nt(h_s[...], w_ref[...]).astype(out_dtype)
        if n_send:
            pl.when(is_last)(finish)

    out = pl.pallas_call(
        body, name=name, grid=grid,
        in_specs=[pl.BlockSpec((tm, k), lambda i, j: (i, 0)),
                  pl.BlockSpec((1, k), lambda i, j: (0, 0)),
                  pl.BlockSpec((tn, k), lambda i, j: (j, 0))] + [ANY_SPEC] * n_send,
        out_specs=[pl.BlockSpec((tm, tn), lambda i, j: (i, j)),
                   pl.BlockSpec((tm, k), lambda i, j: (i, 0))] + [ANY_SPEC] * n_send,
        out_shape=[jax.ShapeDtypeStruct((m, n), out_dtype), jax.ShapeDtypeStruct((m, k), BF16)]
        + _gathered_shapes(sends),
        scratch_shapes=[pltpu.VMEM((tm, k), BF16)] + (_exchange_sems(n_send) if n_send else []),
        compiler_params=_params(2),
    )(x, g, wt, *sends)
    return out[0], out[1], out[2:]


def _mm_res(a, w, res, name):
    m, k = a.shape
    n = w.shape[1]
    tm = _tile(m, 1024)
    tn = _tile(n, 512)

    def body(a_ref, w_ref, r_ref, o_ref):
        o_ref[...] = r_ref[...] + _dot(a_ref[...], w_ref[...])

    return pl.pallas_call(
        body, name=name, grid=(m // tm, n // tn),
        in_specs=[pl.BlockSpec((tm, k), lambda i, j: (i, 0)),
                  pl.BlockSpec((k, tn), lambda i, j: (0, j)),
                  pl.BlockSpec((tm, tn), lambda i, j: (i, j))],
        out_specs=pl.BlockSpec((tm, tn), lambda i, j: (i, j)),
        out_shape=jax.ShapeDtypeStruct((m, n), F32),
        compiler_params=_params(2),
    )(a, w, res)


def _mm_nt(a, w, out_dtype, name):
    m, k = a.shape
    n = w.shape[0]
    tm = _tile(m, 1024)
    tn = _tile(n, 1408)

    def body(a_ref, w_ref, o_ref):
        o_ref[...] = _dot_nt(a_ref[...], w_ref[...]).astype(out_dtype)

    return pl.pallas_call(
        body, name=name, grid=(m // tm, n // tn),
        in_specs=[pl.BlockSpec((tm, k), lambda i, j: (i, 0)),
                  pl.BlockSpec((tn, k), lambda i, j: (j, 0))],
        out_specs=pl.BlockSpec((tm, tn), lambda i, j: (i, j)),
        out_shape=jax.ShapeDtypeStruct((m, n), out_dtype),
        compiler_params=_params(2),
    )(a, w)


def _column_tiles(a, cap):
    if a.ndim == 2:
        s, c = a.shape
        tc = _tile(c, cap)
        return s, c, tc, lambda rows, index: pl.BlockSpec((rows, tc), lambda *g: (index(*g)[0], index(*g)[1]))
    slabs, s, width = a.shape
    tc = _tile(width, cap)
    per = width // tc
    return s, slabs * width, tc, lambda rows, index: pl.BlockSpec(
        (None, rows, tc), lambda *g: (index(*g)[1] // per, index(*g)[0], index(*g)[1] % per))


def _mm_tn(a, b, name):
    s, m, tm, a_spec = _column_tiles(a, 1408)
    n = b.shape[1]
    tn = _tile(n, 1024)

    def body(a_ref, b_ref, o_ref):
        o_ref[...] = _dot_tn(a_ref[...], b_ref[...]).astype(BF16)

    return pl.pallas_call(
        body, name=name, grid=(m // tm, n // tn),
        in_specs=[a_spec(s, lambda i, j: (0, i)),
                  pl.BlockSpec((s, tn), lambda i, j: (0, j))],
        out_specs=pl.BlockSpec((tm, tn), lambda i, j: (i, j)),
        out_shape=jax.ShapeDtypeStruct((m, n), BF16),
        compiler_params=_params(2),
    )(a, b)


def _mm_rmsbwd(a, w, x, g, dres, sends, name):
    m, k, tk, a_spec = _column_tiles(a, 1408)
    n = w.shape[1]
    tm = _tile(m, 1024)
    nk = k // tk
    n_send = len(sends)

    def body(a_ref, w_ref, x_ref, g_ref, r_ref, *rest):
        send_refs, (dx_ref, dxb_ref, dg_ref), rest = rest[:n_send], rest[n_send:n_send + 3], rest[n_send + 3:]
        got_refs, acc, sems = rest[:n_send], rest[n_send], rest[n_send + 1:]
        i, kk = pl.program_id(0), pl.program_id(1)
        if n_send:
            copies = _scatter_copies(send_refs, got_refs, sems)

            @pl.when((i == 0) & (kk == 0))
            def _():
                for cp in copies:
                    cp.start()

        part = _dot(a_ref[...], w_ref[...])

        @pl.when(kk == 0)
        def _():
            acc[...] = part

        @pl.when(kk > 0)
        def _():
            acc[...] += part

        @pl.when(kk == nk - 1)
        def _():
            def chunk(c, dgp):
                rows = pl.ds(pl.multiple_of(c * NORM_ROWS, NORM_ROWS), NORM_ROWS)
                dh = acc[rows, :]
                xv = x_ref[rows, :]
                r = lax.rsqrt(jnp.mean(xv * xv, axis=-1, keepdims=True) + EPS)
                xh = xv * r
                dxh = dh * g_ref[...]
                dx = r_ref[rows, :] + r * (dxh - xh * jnp.mean(dxh * xh, axis=-1, keepdims=True))
                dx_ref[rows, :] = dx
                dxb_ref[rows, :] = dx.astype(BF16)
                return dgp + jnp.sum(dh * xh, axis=0, keepdims=True)

            dgp = lax.fori_loop(0, tm // NORM_ROWS, chunk, jnp.zeros((1, n), F32))

            @pl.when(i == 0)
            def _():
                dg_ref[...] = dgp

            @pl.when(i > 0)
            def _():
                dg_ref[...] += dgp

        if n_send:
            @pl.when((i == m // tm - 1) & (kk == nk - 1))
            def _():
                for cp in copies:
                    cp.wait()

    out = pl.pallas_call(
        body, name=name, grid=(m // tm, nk),
        in_specs=[a_spec(tm, lambda i, kk: (i, kk)),
                  pl.BlockSpec((tk, n), lambda i, kk: (kk, 0)),
                  pl.BlockSpec((tm, n), lambda i, kk: (i, 0)),
                  pl.BlockSpec((1, n), lambda i, kk: (0, 0)),
                  pl.BlockSpec((tm, n), lambda i, kk: (i, 0))] + [ANY_SPEC] * n_send,
        out_specs=[pl.BlockSpec((tm, n), lambda i, kk: (i, 0)),
                   pl.BlockSpec((tm, n), lambda i, kk: (i, 0)),
                   pl.BlockSpec((1, n), lambda i, kk: (0, 0))] + [ANY_SPEC] * n_send,
        out_shape=[jax.ShapeDtypeStruct((m, n), F32), jax.ShapeDtypeStruct((m, n), BF16),
                   jax.ShapeDtypeStruct((1, n), F32)] + _scattered_shapes(sends),
        scratch_shapes=[pltpu.VMEM((tm, n), F32)] + (_exchange_sems(n_send) if n_send else []),
        compiler_params=_params(2),
    )(a, w, x, g, dres, *sends)
    return out[0], out[1], out[2], out[3:]


ANY_SPEC = pl.BlockSpec(memory_space=pl.ANY)
SEMS_PER_OPERAND = N_DEV - 1


def _exchange_sems(n):
    return [pltpu.SemaphoreType.DMA((n, SEMS_PER_OPERAND)), pltpu.SemaphoreType.DMA((n, SEMS_PER_OPERAND)),
            pltpu.SemaphoreType.DMA((n,))]


def _gathered_shapes(parts):
    return [jax.ShapeDtypeStruct((N_DEV,) + a.shape, a.dtype) for a in parts]


def _scattered_shapes(parts):
    return [jax.ShapeDtypeStruct(a.shape, a.dtype) for a in parts]


def _flat(pos):
    return 4 * pos[0] + 2 * pos[1] + pos[2]


def _remote(src, dst, sems, i, k, to):
    send_sems, recv_sems, _ = sems
    return pltpu.make_async_remote_copy(src_ref=src, dst_ref=dst, send_sem=send_sems.at[i, k],
                                        recv_sem=recv_sems.at[i, k], device_id=to,
                                        device_id_type=pl.DeviceIdType.MESH)


def _scatter_copies(ins, outs, sems):
    x, y, c = lax.axis_index("x"), lax.axis_index("y"), lax.axis_index("c")
    me = _flat((x, y, c))
    copies = [pltpu.make_async_copy(ins[i].at[me], outs[i].at[me], sems[2].at[i]) for i in range(len(ins))]
    for d in range(1, N_DEV):
        peer = (1 - x if d & 4 else x, 1 - y if d & 2 else y, 1 - c if d & 1 else c)
        for i in range(len(ins)):
            copies.append(_remote(ins[i].at[_flat(peer)], outs[i].at[me], sems, i, d - 1, peer))
    return copies


def _two_level_gather(ins, outs, sems):
    x, y, c = lax.axis_index("x"), lax.axis_index("y"), lax.axis_index("c")
    me, sibling = (x, y, c), (x, y, 1 - c)
    chips = [(1 - x, y), (x, 1 - y), (1 - x, 1 - y)]
    n = len(ins)

    def block(i, pos):
        return outs[i].at[_flat(pos)]

    local = [pltpu.make_async_copy(ins[i], block(i, me), sems[2].at[i]) for i in range(n)]
    own = [_remote(ins[i], block(i, me), sems, i, 0, sibling) for i in range(n)]
    own += [_remote(ins[i], block(i, me), sems, i, 1 + j, (*chip, c)) for i in range(n) for j, chip in enumerate(chips)]
    passed = [[_remote(block(i, (*chip, c)), block(i, (*chip, c)), sems, i, 4 + j, sibling) for i in range(n)]
              for j, chip in enumerate(chips)]

    def first():
        for cp in local + own:
            cp.start()

    def relay():
        for j, chip in enumerate(chips):
            for i in range(n):
                _remote(ins[i], block(i, (*chip, c)), sems, i, 1 + j, me).wait_recv()
                passed[j][i].start()

    def finish():
        for i in range(n):
            _remote(ins[i], block(i, sibling), sems, i, 0, me).wait_recv()
            for j, chip in enumerate(chips):
                _remote(ins[i], block(i, (*chip, 1 - c)), sems, i, 4 + j, me).wait_recv()
        for cp in own + [cp for row in passed for cp in row]:
            cp.wait_send()
        for cp in local:
            cp.wait()

    return first, relay, finish


def _gather(parts, name):
    n = len(parts)

    def body(*refs):
        start, relay, finish = _two_level_gather(refs[:n], refs[n:2 * n], refs[2 * n:])
        start()
        relay()
        finish()

    return pl.pallas_call(
        body, name=name, in_specs=[ANY_SPEC] * n, out_specs=[ANY_SPEC] * n,
        out_shape=_gathered_shapes(parts), scratch_shapes=_exchange_sems(n),
    )(*parts)


def _tri(kind):
    j = lax.broadcasted_iota(jnp.int32, (BLK, BLK), 0)
    s = lax.broadcasted_iota(jnp.int32, (BLK, BLK), 1)
    m = {"after": j > s, "upto": j <= s, "before": j < s}[kind]
    return jnp.concatenate([jnp.where(m, 1.0, 0.0), jnp.ones((BLK, BLK), F32)], axis=1).astype(BF16)


def _scan_rows(v, tri):
    r = _dot(v.astype(BF16), tri)
    return r[:, :BLK], r[:, BLK:]


HEADS_PER_STEP = LANES // HEAD_DIM


def _first_head_lanes():
    return lax.broadcasted_iota(jnp.int32, (1, LANES), 1) < HEAD_DIM


def _pair_mean(v, first):
    m0 = jnp.sum(jnp.where(first, v, 0.0), axis=-1, keepdims=True)
    m1 = jnp.sum(jnp.where(first, 0.0, v), axis=-1, keepdims=True)
    return jnp.where(first, m0, m1) * (1.0 / HEAD_DIM)


def _pair_norm(v, g2, first):
    return v * lax.rsqrt(_pair_mean(v * v, first) + EPS) * g2


def _pair_norm_bwd(raw, g2, dn, first):
    r = lax.rsqrt(_pair_mean(raw * raw, first) + EPS)
    xh = raw * r
    dxh = dn * g2
    return r * (dxh - xh * _pair_mean(dxh * xh, first)), jnp.sum(dn * xh, axis=0, keepdims=True)


def _block_diag(v, first):
    zero = jnp.zeros_like(v)
    return jnp.concatenate([jnp.where(first, v, zero), jnp.where(first, zero, v)], axis=0)


def _attn_prep(q_ref, k_ref, v_ref, qg_ref, kg_ref, qc_s, kc_s, vd_s, kd_s, n_blk):
    scale = HEAD_DIM ** -0.5
    first = _first_head_lanes()

    def prep(i, _):
        rows = pl.ds(pl.multiple_of(i * BLK, BLK), BLK)
        both = pl.ds(pl.multiple_of(i * 2 * BLK, 2 * BLK), 2 * BLK)
        qh, ql = _split_bf16(_pair_norm(q_ref[rows, :], qg_ref[...], first) * scale)
        kh, kl = _split_bf16(_pair_norm(k_ref[rows, :], kg_ref[...], first))
        for h in range(HEADS_PER_STEP):
            sl = slice(h * HEAD_DIM, (h + 1) * HEAD_DIM)
            qc_s[h, rows, :] = jnp.concatenate([qh[:, sl], ql[:, sl], qh[:, sl], ql[:, sl]], axis=1)
            kc_s[h, rows, :] = jnp.concatenate([kh[:, sl], kh[:, sl], kl[:, sl], kl[:, sl]], axis=1)
        vd_s[both, :] = _block_diag(v_ref[rows, :].astype(BF16), first)
        if kd_s is not None:
            kd_s[both, :] = _block_diag(kh, first)
        return 0

    lax.fori_loop(0, n_blk, prep, 0)


def _pair_scores(qc, kc_ref, grp):
    zs = []
    for j in range(0, KEY_GROUP, 2):
        two = pl.ds(pl.multiple_of((grp * KEY_GROUP + j) * BLK, 2 * BLK), 2 * BLK)
        z = _dot_nt(qc, kc_ref[two, :])
        zs += [z[:, :BLK], z[:, BLK:]]
    return zs


def _col_minus_row():
    row = lax.broadcasted_iota(jnp.int32, (BLK, BLK), 0)
    col = lax.broadcasted_iota(jnp.int32, (BLK, BLK), 1)
    return col - row


def _softplus(z):
    return jnp.maximum(z, 0.0) + jnp.log(1.0 + jnp.exp(-jnp.abs(z)))


def _attn_fwd(proj, qg, kg, sends, name):
    s = proj.shape[0]
    n_blk = s // BLK
    pairs = ATTN_WIDTH // LANES
    n_send = len(sends)

    def body(q_ref, k_ref, v_ref, qg_ref, kg_ref, *rest):
        send_refs, (o_ref, t_ref), rest = rest[:n_send], rest[n_send:n_send + 2], rest[n_send + 2:]
        got_refs, (qc_s, kc_s, vd_s), sems = rest[:n_send], rest[n_send:n_send + 3], rest[n_send + 3:]
        start, relay, finish = _two_level_gather(send_refs, got_refs, sems)
        step = pl.program_id(0)
        pl.when(step == 0)(start)
        pl.when(step == pairs - 1)(relay)
        tri = _tri("after")
        diff = _col_minus_row()
        first = _first_head_lanes()
        heads = range(HEADS_PER_STEP)
        _attn_prep(q_ref, k_ref, v_ref, qg_ref, kg_ref, qc_s, kc_s, vd_s, None, n_blk)

        def group(qc, qi, grp, carry, acc, masked):
            blocks = [grp * KEY_GROUP + j for j in reversed(range(KEY_GROUP))]
            zs = [_pair_scores(qc[h], kc_s.at[h], grp)[::-1] for h in heads]
            keeps = [diff < (qi - kb) * BLK if masked else None for kb in blocks]
            parts = [[None] * KEY_GROUP for _ in heads]
            for h in heads:
                for j, z in enumerate(zs[h]):
                    sp = _softplus(z)
                    lom = -sp
                    if masked:
                        lom = jnp.where(keeps[j], lom, 0.0)
                    tail, tot = _scan_rows(lom, tri)
                    parts[h][j] = (z - sp + tail, tot)
            carry = list(carry)
            for j, kb in enumerate(blocks):
                ws = []
                for h in heads:
                    lw, tot = parts[h][j]
                    w = jnp.exp(lw + carry[h])
                    if masked:
                        w = jnp.where(keeps[j], w, 0.0)
                    ws.append(w.astype(BF16))
                    carry[h] = carry[h] + tot
                acc = acc + _dot(jnp.concatenate(ws, axis=1),
                                 vd_s[pl.ds(pl.multiple_of(kb * 2 * BLK, 2 * BLK), 2 * BLK), :])
            return tuple(carry), acc

        def q_block(qi, _):
            rows = pl.ds(pl.multiple_of(qi * BLK, BLK), BLK)
            qc = [qc_s[h, rows, :] for h in heads]
            top = qi // KEY_GROUP
            zero = jnp.zeros((BLK, BLK), F32)
            carry, acc = group(qc, qi, top, (zero,) * HEADS_PER_STEP, jnp.zeros((BLK, LANES), F32), True)
            carry, acc = lax.fori_loop(
                0, top, lambda t, c: group(qc, qi, top - 1 - t, c[0], c[1], False), (carry, acc))
            o_ref[rows, :] = acc.astype(BF16)
            t_ref[rows, :] = jnp.where(first, carry[0], carry[1])
            return 0

        lax.fori_loop(0, n_blk, q_block, 0)
        pl.when(step == pairs - 1)(finish)

    col = lambda off: pl.BlockSpec((s, LANES), lambda p: (0, off + p))
    vec = pl.BlockSpec((1, LANES), lambda p: (0, 0))
    out = pl.pallas_call(
        body, name=name, grid=(pairs,),
        in_specs=[col(0), col(pairs), col(2 * pairs), vec, vec] + [ANY_SPEC] * n_send,
        out_specs=[pl.BlockSpec((s, LANES), lambda p: (0, p))] * 2 + [ANY_SPEC] * n_send,
        out_shape=[jax.ShapeDtypeStruct((s, ATTN_WIDTH), BF16), jax.ShapeDtypeStruct((s, ATTN_WIDTH), F32)]
        + _gathered_shapes(sends),
        scratch_shapes=[pltpu.VMEM((HEADS_PER_STEP, s, 4 * HEAD_DIM), BF16)] * 2
        + [pltpu.VMEM((HEADS_PER_STEP * s, LANES), BF16)] + _exchange_sems(n_send),
        compiler_params=_params(1),
    )(proj, proj, proj, qg, kg, *sends)
    return out[0], out[1], out[2:]


def _attn_bwd(proj, dcat, tsum, qg, kg, sends, name):
    s = proj.shape[0]
    n_blk = s // BLK
    pairs = ATTN_WIDTH // LANES
    scale = HEAD_DIM ** -0.5
    n_send = len(sends)
    n_scratch = 7

    def body(q_ref, k_ref, v_ref, do_ref, t_ref, qg_ref, kg_ref, *rest):
        send_refs, rest = rest[:n_send], rest[n_send:]
        (dq_ref, dk_ref, dv_ref, dqg_ref, dkg_ref), rest = rest[:5], rest[5:]
        got_refs, scratch, sems = rest[:n_send], rest[n_send:n_send + n_scratch], rest[n_send + n_scratch:]
        qc_s, kc_s, vd_s, kd_s, qd_s, dob_s, dkv_s = scratch
        copies = _scatter_copies(send_refs, got_refs, sems)

        @pl.when(pl.program_id(0) == 0)
        def _():
            for cp in copies:
                cp.start()

        tri_p = _tri("upto")
        tri_h = _tri("before")
        diff = _col_minus_row()

        @pl.when(pl.program_id(0) == 0)
        def _():
            dqg_ref[...] = jnp.zeros_like(dqg_ref)
            dkg_ref[...] = jnp.zeros_like(dkg_ref)

        first = _first_head_lanes()
        heads = range(HEADS_PER_STEP)
        _attn_prep(q_ref, k_ref, v_ref, qg_ref, kg_ref, qc_s, kc_s, vd_s, kd_s, n_blk)

        def prep(i, _):
            rows = pl.ds(pl.multiple_of(i * BLK, BLK), BLK)
            both = pl.ds(pl.multiple_of(i * 2 * BLK, 2 * BLK), 2 * BLK)
            dob = do_ref[rows, :].astype(BF16)
            dob_s[rows, :] = dob
            none = jnp.zeros((BLK, HEAD_DIM), BF16)
            for h in heads:
                qd_s[h, both, :] = jnp.concatenate(
                    [jnp.concatenate([qc_s[h, rows, 0:HEAD_DIM], none], axis=1),
                     jnp.concatenate([none, dob[:, h * HEAD_DIM:(h + 1) * HEAD_DIM]], axis=1)], axis=0)
                dkv_s[h, rows, :] = jnp.zeros((BLK, LANES), F32)
            return 0

        lax.fori_loop(0, n_blk, prep, 0)

        def group(qc, qd, dob, tq, qi, grp, pc, hc, dq, masked):
            blocks = [grp * KEY_GROUP + j for j in range(KEY_GROUP)]
            cols_of = [pl.ds(pl.multiple_of(kb * BLK, BLK), BLK) for kb in blocks]
            both_of = [pl.ds(pl.multiple_of(kb * 2 * BLK, 2 * BLK), 2 * BLK) for kb in blocks]
            zs = [_pair_scores(qc[h], kc_s.at[h], grp) for h in heads]
            das =[_dot_nt(dob, vd_s[both, :]) for both in both_of]
            keeps = [diff < (qi - kb) * BLK if masked else None for kb in blocks]
            lbs = [[None] * KEY_GROUP for _ in heads]
            scans = [[None] * KEY_GROUP for _ in heads]
            for h in heads:
                for j, z in enumerate(zs[h]):
                    sp = _softplus(z)
                    lom = -sp
                    if masked:
                        lom = jnp.where(keeps[j], lom, 0.0)
                    lbs[h][j] = z - sp
                    scans[h][j] = _scan_rows(lom, tri_p)
            pc, hc = list(pc), list(hc)
            avs = [[None] * KEY_GROUP for _ in heads]
            gws = [[None] * KEY_GROUP for _ in heads]
            hscans = [[None] * KEY_GROUP for _ in heads]
            for h in heads:
                for j in range(KEY_GROUP):
                    p_in, p_tot = scans[h][j]
                    a = jnp.exp(lbs[h][j] + (tq[h] - pc[h] - p_in))
                    if masked:
                        a = jnp.where(keeps[j], a, 0.0)
                    pc[h] = pc[h] + p_tot
                    gw = das[j][:, h * BLK:(h + 1) * BLK] * a
                    avs[h][j] = a.astype(BF16)
                    gws[h][j] = gw
                    hscans[h][j] = _scan_rows(gw, tri_h)
            dzs = [[None] * KEY_GROUP for _ in heads]
            for h in heads:
                for j in range(KEY_GROUP):
                    h_in, g_tot = hscans[h][j]
                    gw = gws[h][j]
                    dz = gw - jnp.exp(lbs[h][j]) * (gw + hc[h] + h_in)
                    if masked:
                        dz = jnp.where(keeps[j], dz, 0.0)
                    hc[h] = hc[h] + g_tot
                    dzs[h][j] = dz.astype(BF16)
            for j, both in enumerate(both_of):
                dq = dq + _dot(jnp.concatenate([dzs[h][j] for h in heads], axis=1), kd_s[both, :])
            for h in heads:
                for j, cols in enumerate(cols_of):
                    dkv_s[h, cols, :] += _dot_tn(jnp.concatenate([dzs[h][j], avs[h][j]], axis=0), qd[h])
            return tuple(pc), tuple(hc), dq

        def q_block(qi, dqg):
            rows = pl.ds(pl.multiple_of(qi * BLK, BLK), BLK)
            both = pl.ds(pl.multiple_of(qi * 2 * BLK, 2 * BLK), 2 * BLK)
            qc = [qc_s[h, rows, :] for h in heads]
            qd = [qd_s[h, both, :] for h in heads]
            dob = dob_s[rows, :]
            tboth = t_ref[rows, :]
            tq = [jnp.concatenate([tboth[:, h * HEAD_DIM:(h + 1) * HEAD_DIM]] * 2, axis=1) for h in heads]
            zero = (jnp.zeros((BLK, BLK), F32),) * HEADS_PER_STEP
            top = qi // KEY_GROUP
            pc, hc, dq = lax.fori_loop(
                0, top, lambda grp, c: group(qc, qd, dob, tq, qi, grp, c[0], c[1], c[2], False),
                (zero, zero, jnp.zeros((BLK, LANES), F32)))
            _, _, dq = group(qc, qd, dob, tq, qi, top, pc, hc, dq, True)
            dq_raw, dg = _pair_norm_bwd(q_ref[rows, :], qg_ref[...], dq * scale, first)
            dq_ref[rows, :] = dq_raw.astype(BF16)
            return dqg + dg

        dqg = lax.fori_loop(0, n_blk, q_block, jnp.zeros((1, LANES), F32))

        def finish(i, dkg):
            rows = pl.ds(pl.multiple_of(i * BLK, BLK), BLK)
            dk = jnp.concatenate([dkv_s[h, rows, 0:HEAD_DIM] for h in heads], axis=1)
            dv = jnp.concatenate([dkv_s[h, rows, HEAD_DIM:2 * HEAD_DIM] for h in heads], axis=1)
            dk_raw, dg = _pair_norm_bwd(k_ref[rows, :], kg_ref[...], dk, first)
            dk_ref[rows, :] = dk_raw.astype(BF16)
            dv_ref[rows, :] = dv.astype(BF16)
            return dkg + dg

        dkg = lax.fori_loop(0, n_blk, finish, jnp.zeros((1, LANES), F32))
        dqg_ref[0:1, :] += dqg
        dkg_ref[0:1, :] += dkg

        @pl.when(pl.program_id(0) == pairs - 1)
        def _():
            for cp in copies:
                cp.wait()

    col = lambda off: pl.BlockSpec((s, LANES), lambda p: (0, off + p))
    vec = pl.BlockSpec((1, LANES), lambda p: (0, 0))
    small = pl.BlockSpec((8, LANES), lambda p: (0, 0))
    out = pl.pallas_call(
        body, name=name, grid=(pairs,),
        in_specs=[col(0), col(pairs), col(2 * pairs), col(0), col(0), vec, vec] + [ANY_SPEC] * n_send,
        out_specs=[col(0)] * 3 + [small] * 2 + [ANY_SPEC] * n_send,
        out_shape=[jax.ShapeDtypeStruct((s, ATTN_WIDTH), BF16)] * 3 + [jax.ShapeDtypeStruct((8, LANES), F32)] * 2
        + _scattered_shapes(sends),
        scratch_shapes=[pltpu.VMEM((HEADS_PER_STEP, s, 4 * HEAD_DIM), BF16)] * 2
        + [pltpu.VMEM((HEADS_PER_STEP * s, LANES), BF16)] * 2
        + [pltpu.VMEM((HEADS_PER_STEP, HEADS_PER_STEP * s, LANES), BF16), pltpu.VMEM((s, LANES), BF16),
           pltpu.VMEM((HEADS_PER_STEP, s, LANES), F32)] + _exchange_sems(n_send),
        compiler_params=_params(1),
    )(proj, proj, proj, dcat, tsum, qg, kg, *sends)
    return out[:5], out[5:]


CONV_ROWS = 128


def _shifted(window, shift, halo):
    if shift == 0:
        return window[halo:, :]
    return pltpu.roll(window, shift, 0)[halo:, :]


SUBLANES = 8


def _row_shifts(window, up):
    n = window.shape[0]
    return [window] + [pltpu.roll(window, n - b if up else b, 0) for b in range(1, SUBLANES)]


def _earlier(shifts, back, rows):
    a, b = divmod(back, SUBLANES)
    return shifts[b][CONV_HALO - SUBLANES * a:CONV_HALO - SUBLANES * a + rows, :]


def _later(shifts, ahead, rows):
    a, b = divmod(ahead, SUBLANES)
    return shifts[b][SUBLANES * a:SUBLANES * a + rows, :]


def _lane_blocks(width):
    return [slice(c, c + LANES) for c in range(0, width, LANES)]


def _conv_taps(shifts, w_ref, lanes, rows):
    y = None
    for k in range(CONV_KERNEL):
        term = _earlier(shifts, CONV_KERNEL - 1 - k, rows) * w_ref[k:k + 1, lanes]
        y = term if y is None else y + term
    return y


def _conv_fwd(proj, w, b, lg, lb, name):
    s = proj.shape[0]
    cw = w.shape[1]
    rows = CONV_ROWS
    blk_a = (proj.shape[1] - 2 * cw) // cw

    def body(a_ref, g_ref, w_ref, b_ref, lg_ref, lb_ref, o_ref, y_ref, u_s):
        u_s[0:CONV_HALO, :] = jnp.zeros((CONV_HALO, cw), F32)

        def glu(i, _):
            r0 = pl.multiple_of(i * rows, rows)
            u_s[pl.ds(CONV_HALO + r0, rows), :] = a_ref[pl.ds(r0, rows), :] * _sigmoid(g_ref[pl.ds(r0, rows), :])
            return 0

        lax.fori_loop(0, s // rows, glu, 0)

        def chunk(i, _):
            r0 = pl.multiple_of(i * rows, rows)
            for lanes in _lane_blocks(cw):
                shifts = _row_shifts(u_s[pl.ds(r0, CONV_HALO + rows), lanes], False)
                y_ref[pl.ds(r0, rows), lanes] = _conv_taps(shifts, w_ref, lanes, rows) + b_ref[:, lanes]
            y = y_ref[pl.ds(r0, rows), :]
            yc = y - jnp.mean(y, axis=-1, keepdims=True)
            n = yc * lax.rsqrt(jnp.mean(yc * yc, axis=-1, keepdims=True) + EPS)
            ln = n * lg_ref[...] + lb_ref[...]
            o_ref[pl.ds(r0, rows), :] = (ln * _sigmoid(ln)).astype(BF16)
            return 0

        lax.fori_loop(0, s // rows, chunk, 0)

    vec = pl.BlockSpec((1, cw), lambda i: (0, 0))
    return pl.pallas_call(
        body, name=name, grid=(1,),
        in_specs=[pl.BlockSpec((s, cw), lambda i: (0, blk_a)), pl.BlockSpec((s, cw), lambda i: (0, blk_a + 1)),
                  pl.BlockSpec((CONV_KERNEL, cw), lambda i: (0, 0)), vec, vec, vec],
        out_specs=[pl.BlockSpec((s, cw), lambda i: (0, 0))] * 2,
        out_shape=[jax.ShapeDtypeStruct((s, cw), BF16), jax.ShapeDtypeStruct((s, cw), F32)],
        scratch_shapes=[pltpu.VMEM((CONV_HALO + s, cw), F32)],
        compiler_params=_params(1),
    )(proj, proj, w, b, lg, lb)


def _conv_bwd(proj, y, dcat, w, lg, lb, sends, name):
    s = proj.shape[0]
    cw = w.shape[1]
    rows = CONV_ROWS
    blk_a = (proj.shape[1] - 2 * cw) // cw
    n_chunk = s // rows
    n_send = len(sends)

    def body(a_ref, g_ref, y_ref, dc_ref, w_ref, lg_ref, lb_ref, *rest):
        send_refs, (o_ref, dw_ref, db_ref, dlg_ref, dlb_ref), rest = rest[:n_send], rest[n_send:n_send + 5], rest[n_send + 5:]
        got_refs, (u_s, dy_s, dw_s), sems = rest[:n_send], rest[n_send:n_send + 3], rest[n_send + 3:]
        copies = _scatter_copies(send_refs, got_refs, sems)
        for cp in copies:
            cp.start()
        u_s[0:CONV_HALO, :] = jnp.zeros((CONV_HALO, cw), F32)
        dy_s[pl.ds(s, CONV_HALO), :] = jnp.zeros((CONV_HALO, cw), F32)
        dw_s[...] = jnp.zeros_like(dw_s)

        def glu(i, _):
            r0 = pl.multiple_of(i * rows, rows)
            u_s[pl.ds(CONV_HALO + r0, rows), :] = a_ref[pl.ds(r0, rows), :] * _sigmoid(g_ref[pl.ds(r0, rows), :])
            return 0

        lax.fori_loop(0, n_chunk, glu, 0)

        def chunk(i, carry):
            db, dlg, dlb = carry
            r0 = pl.multiple_of(i * rows, rows)
            y = y_ref[pl.ds(r0, rows), :]
            yc = y - jnp.mean(y, axis=-1, keepdims=True)
            r = lax.rsqrt(jnp.mean(yc * yc, axis=-1, keepdims=True) + EPS)
            n = yc * r
            ln = n * lg_ref[...] + lb_ref[...]
            sg = _sigmoid(ln)
            dln = dc_ref[pl.ds(r0, rows), :] * (sg * (1.0 + ln * (1.0 - sg)))
            dn = dln * lg_ref[...]
            dy = r * (dn - jnp.mean(dn, axis=-1, keepdims=True) - n * jnp.mean(dn * n, axis=-1, keepdims=True))
            dy_s[pl.ds(r0, rows), :] = dy
            for lanes in _lane_blocks(cw):
                shifts = _row_shifts(u_s[pl.ds(r0, CONV_HALO + rows), lanes], False)
                dy_part = dy[:, lanes]
                for k in range(CONV_KERNEL):
                    prod = _earlier(shifts, CONV_KERNEL - 1 - k, rows) * dy_part
                    dw_s[k, :, lanes] += jnp.sum(prod.reshape(rows // SUBLANES, SUBLANES, LANES), axis=0)
            return (db + jnp.sum(dy, axis=0, keepdims=True),
                    dlg + jnp.sum(dln * n, axis=0, keepdims=True),
                    dlb + jnp.sum(dln, axis=0, keepdims=True))

        zero = jnp.zeros((1, cw), F32)
        db, dlg, dlb = lax.fori_loop(0, n_chunk, chunk, (zero, zero, zero))
        db_ref[...] = db
        dlg_ref[...] = dlg
        dlb_ref[...] = dlb
        for k in range(CONV_KERNEL):
            dw_ref[k:k + 1, :] = jnp.sum(dw_s[k], axis=0, keepdims=True)

        def back(i, _):
            r0 = pl.multiple_of(i * rows, rows)
            for lanes in _lane_blocks(cw):
                shifts = _row_shifts(dy_s[pl.ds(r0, rows + CONV_HALO), lanes], True)
                du = None
                for k in range(CONV_KERNEL):
                    term = _later(shifts, CONV_KERNEL - 1 - k, rows) * w_ref[k:k + 1, lanes]
                    du = term if du is None else du + term
                av = a_ref[pl.ds(r0, rows), lanes]
                sg = _sigmoid(g_ref[pl.ds(r0, rows), lanes])
                o_ref[pl.ds(r0, rows), lanes] = (du * sg).astype(BF16)
                o_ref[pl.ds(r0, rows), slice(cw + lanes.start, cw + lanes.stop)] = (du * av * sg * (1.0 - sg)).astype(BF16)
            return 0

        lax.fori_loop(0, n_chunk, back, 0)
        for cp in copies:
            cp.wait()

    vec = pl.BlockSpec((1, cw), lambda i: (0, 0))
    wspec = pl.BlockSpec((CONV_KERNEL, cw), lambda i: (0, 0))
    out = pl.pallas_call(
        body, name=name, grid=(1,),
        in_specs=[pl.BlockSpec((s, cw), lambda i: (0, blk_a)), pl.BlockSpec((s, cw), lambda i: (0, blk_a + 1)),
                  pl.BlockSpec((s, cw), lambda i: (0, 0)), pl.BlockSpec((s, cw), lambda i: (0, 1)), wspec, vec, vec]
        + [ANY_SPEC] * n_send,
        out_specs=[pl.BlockSpec((s, 2 * cw), lambda i: (0, 0)), wspec, vec, vec, vec] + [ANY_SPEC] * n_send,
        out_shape=[jax.ShapeDtypeStruct((s, 2 * cw), BF16), jax.ShapeDtypeStruct((CONV_KERNEL, cw), F32)]
        + [jax.ShapeDtypeStruct((1, cw), F32)] * 3 + _scattered_shapes(sends),
        scratch_shapes=[pltpu.VMEM((CONV_HALO + s, cw), F32), pltpu.VMEM((s + CONV_HALO, cw), F32),
                        pltpu.VMEM((CONV_KERNEL, 8, cw), F32)] + _exchange_sems(n_send),
        compiler_params=_params(1),
    )(proj, proj, y, dcat, w, lg, lb, *sends)
    return out[:5], out[5:]


FFN_ROWS = 256


def _ffn_gate(g_ref, r0, rows, w_ref, b_ref):
    cur = g_ref[pl.ds(r0, rows), :].astype(F32)
    prev = g_ref[pl.ds(pl.multiple_of(jnp.maximum(r0 - FFN_HALO, 0), FFN_HALO), FFN_HALO), :].astype(F32)
    prev = jnp.where(r0 > 0, prev, 0.0)
    window = jnp.concatenate([prev, cur], axis=0)
    gc = cur * w_ref[FFN_KERNEL - 1:FFN_KERNEL, :] + b_ref[...]
    for k in range(FFN_KERNEL - 1):
        gc = gc + _shifted(window, FFN_KERNEL - 1 - k, FFN_HALO) * w_ref[k:k + 1, :]
    return gc, window


def _ffn_fwd(up, w, b, name):
    s = up.shape[0]
    f = w.shape[1]
    tc = _pick(f, (256, 128))
    nc = f // tc
    rows = _pick(s, (FFN_ROWS, 128))

    def body(g_ref, v_ref, w_ref, b_ref, o_ref):
        def chunk(i, _):
            r0 = pl.multiple_of(i * rows, rows)
            gc, _w = _ffn_gate(g_ref, r0, rows, w_ref, b_ref)
            o_ref[pl.ds(r0, rows), :] = (gc * _sigmoid(gc) * v_ref[pl.ds(r0, rows), :].astype(F32)).astype(BF16)
            return 0

        lax.fori_loop(0, s // rows, chunk, 0)

    return pl.pallas_call(
        body, name=name, grid=(nc,),
        in_specs=[pl.BlockSpec((s, tc), lambda j: (0, j)), pl.BlockSpec((s, tc), lambda j: (0, nc + j)),
                  pl.BlockSpec((FFN_KERNEL, tc), lambda j: (0, j)), pl.BlockSpec((1, tc), lambda j: (0, j))],
        out_specs=pl.BlockSpec((s, tc), lambda j: (0, j)),
        out_shape=jax.ShapeDtypeStruct((s, f), BF16),
        compiler_params=_params(1),
    )(up, up, w, b)


def _ffn_bwd(up, dact, w, b, name):
    s = up.shape[0]
    f = w.shape[1]
    tc = _pick(f, (256, 128))
    nc = f // tc
    rows = _pick(s, (FFN_ROWS, 128))
    n_chunk = s // rows

    def body(g_ref, v_ref, da_ref, w_ref, b_ref, d_ref, dw_ref, db_ref, dgc_s):
        dg_ref, dv_ref = d_ref.at[0], d_ref.at[1]
        dgc_s[pl.ds(s, FFN_HALO), :] = jnp.zeros((FFN_HALO, tc), F32)

        def chunk(i, carry):
            r0 = pl.multiple_of(i * rows, rows)
            gc, window = _ffn_gate(g_ref, r0, rows, w_ref, b_ref)
            sg = _sigmoid(gc)
            da = da_ref[pl.ds(r0, rows), :].astype(F32)
            dv_ref[pl.ds(r0, rows), :] = (da * gc * sg).astype(BF16)
            dgc = da * v_ref[pl.ds(r0, rows), :].astype(F32) * (sg * (1.0 + gc * (1.0 - sg)))
            dgc_s[pl.ds(r0, rows), :] = dgc
            out = [carry[0] + jnp.sum(dgc, axis=0, keepdims=True)]
            for k in range(FFN_KERNEL):
                out.append(carry[1 + k] + jnp.sum(_shifted(window, FFN_KERNEL - 1 - k, FFN_HALO) * dgc,
                                                  axis=0, keepdims=True))
            return tuple(out)

        zero = jnp.zeros((1, tc), F32)
        sums = lax.fori_loop(0, n_chunk, chunk, (zero,) * (1 + FFN_KERNEL))
        db_ref[...] = sums[0]
        for k in range(FFN_KERNEL):
            dw_ref[k:k + 1, :] = sums[1 + k]

        def back(i, _):
            r0 = pl.multiple_of(i * rows, rows)
            window = dgc_s[pl.ds(r0, rows + FFN_HALO), :]
            dg = window[:rows, :] * w_ref[FFN_KERNEL - 1:FFN_KERNEL, :]
            for k in range(FFN_KERNEL - 1):
                sh = FFN_KERNEL - 1 - k
                dg = dg + pltpu.roll(window, rows + FFN_HALO - sh, 0)[:rows, :] * w_ref[k:k + 1, :]
            dg_ref[pl.ds(r0, rows), :] = dg.astype(BF16)
            return 0

        lax.fori_loop(0, n_chunk, back, 0)

    blk = lambda off: pl.BlockSpec((s, tc), lambda j: (0, off + j))
    return pl.pallas_call(
        body, name=name, grid=(nc,),
        in_specs=[blk(0), blk(nc), blk(0), pl.BlockSpec((FFN_KERNEL, tc), lambda j: (0, j)),
                  pl.BlockSpec((1, tc), lambda j: (0, j))],
        out_specs=[pl.BlockSpec((2, s, tc), lambda j: (0, 0, j)), pl.BlockSpec((FFN_KERNEL, tc), lambda j: (0, j)),
                   pl.BlockSpec((1, tc), lambda j: (0, j))],
        out_shape=[jax.ShapeDtypeStruct((2, s, f), BF16),
                   jax.ShapeDtypeStruct((FFN_KERNEL, f), F32), jax.ShapeDtypeStruct((1, f), F32)],
        scratch_shapes=[pltpu.VMEM((s + FFN_HALO, tc), F32)],
        compiler_params=_params(1),
    )(up, up, dact, w, b)


def _loss_head(y, target, name):
    m, n = y.shape
    tm = _pick(m, (256, 128))

    def body(y_ref, t_ref, l_ref, d_ref, db_ref):
        e = y_ref[...] - t_ref[...]
        part = 0.5 * jnp.sum(jnp.sum(e * e, axis=-1, keepdims=True) / n, axis=0, keepdims=True)

        @pl.when(pl.program_id(0) == 0)
        def _():
            l_ref[...] = jnp.zeros_like(l_ref)

        l_ref[...] += part
        d = e / n
        d_ref[...] = d
        db_ref[...] = d.astype(BF16)

    return pl.pallas_call(
        body, name=name, grid=(m // tm,),
        in_specs=[pl.BlockSpec((tm, n), lambda i: (i, 0))] * 2,
        out_specs=[pl.BlockSpec((8, LANES), lambda i: (0, 0)), pl.BlockSpec((tm, n), lambda i: (i, 0)),
                   pl.BlockSpec((tm, n), lambda i: (i, 0))],
        out_shape=[jax.ShapeDtypeStruct((8, LANES), F32), jax.ShapeDtypeStruct((m, n), F32),
                   jax.ShapeDtypeStruct((m, n), BF16)],
        compiler_params=_params(1),
    )(y, target)


def _adamw_math(w, g, m, v):
    m = ADAM_B1 * m + (1.0 - ADAM_B1) * g
    v = ADAM_B2 * v + (1.0 - ADAM_B2) * (g * g)
    m_hat = m / (1.0 - ADAM_B1 ** ADAM_STEP)
    v_hat = v / (1.0 - ADAM_B2 ** ADAM_STEP)
    delta = -ADAM_LR * (m_hat / (jnp.sqrt(v_hat) + ADAM_EPS) + ADAM_WD * w)
    return delta, m, v


def _sum_adamw(parts, w, m, v, sends, name):
    depth, r, c = w.shape
    tr = max(t for t in range(16, min(r, 192) + 1, 16) if r % t == 0)
    steps = r // tr
    n_send = len(sends)

    def body(*refs):
        p_refs, refs = refs[:depth], refs[depth:]
        (w_ref, m_ref, v_ref), send_refs, refs = refs[:3], refs[3:3 + n_send], refs[3 + n_send:]
        (g_out, d_out, m_out, v_out), got_refs, sems = refs[:4], refs[4:4 + n_send], refs[4 + n_send:]
        is_first = (pl.program_id(0) == 0) & (pl.program_id(1) == 0)
        is_last = (pl.program_id(0) == depth - 1) & (pl.program_id(1) == steps - 1)
        if n_send:
            start, relay, finish = _two_level_gather(send_refs, got_refs, sems)
            pl.when(is_first)(start)
            pl.when(is_last)(relay)
        for layer in range(depth):
            @pl.when(pl.program_id(0) == layer)
            def _():
                g = p_refs[layer][0].astype(F32)
                for src in range(1, N_DEV):
                    g = g + p_refs[layer][src].astype(F32)
                d, mn, vn = _adamw_math(w_ref[0], g, m_ref[0], v_ref[0])
                g_out[0] = g
                d_out[0] = d
                m_out[0] = mn
                v_out[0] = vn

        if n_send:
            pl.when(is_last)(finish)

    def part_spec(layer):
        return pl.BlockSpec((N_DEV, tr, c), lambda l, i: (0, jnp.clip((l - layer) * steps + i, 0, steps - 1), 0))

    blk = pl.BlockSpec((1, tr, c), lambda l, i: (l, i, 0))
    out = pl.pallas_call(
        body, name=name, grid=(depth, steps),
        in_specs=[part_spec(layer) for layer in range(depth)] + [blk, blk, blk] + [ANY_SPEC] * n_send,
        out_specs=[blk] * 4 + [ANY_SPEC] * n_send,
        out_shape=[jax.ShapeDtypeStruct(w.shape, F32)] * 4 + _gathered_shapes(sends),
        scratch_shapes=_exchange_sems(n_send) if n_send else [],
        compiler_params=_params(2),
    )(*parts, w, m, v, *sends)
    return out[:4], out[4:]


VMEM_SPEC = pl.BlockSpec(memory_space=pltpu.VMEM)


def _sum_small(parts, name):
    n = len(parts)

    def body(*refs):
        for p_ref, o_ref in zip(refs[:n], refs[n:]):
            g = p_ref[0]
            for src in range(1, N_DEV):
                g = g + p_ref[src]
            o_ref[...] = g

    return pl.pallas_call(
        body, name=name, in_specs=[VMEM_SPEC] * n, out_specs=[VMEM_SPEC] * n,
        out_shape=[jax.ShapeDtypeStruct(p.shape[1:], F32) for p in parts],
        compiler_params=_params(),
    )(*parts)


def _adamw_small(ws, gs, ms, vs, name):
    n = len(ws)

    def body(*refs):
        ins, outs = refs[:4 * n], refs[4 * n:]
        for i in range(n):
            d, mn, vn = _adamw_math(ins[i][...], ins[n + i][...], ins[2 * n + i][...], ins[3 * n + i][...])
            outs[i][...] = d
            outs[n + i][...] = mn
            outs[2 * n + i][...] = vn

    out = pl.pallas_call(
        body, name=name, in_specs=[VMEM_SPEC] * (4 * n), out_specs=[VMEM_SPEC] * (3 * n),
        out_shape=[jax.ShapeDtypeStruct(w.shape, F32) for w in ws] * 3,
        compiler_params=_params(),
    )(*ws, *gs, *ms, *vs)
    return out[:n], out[n:2 * n], out[2 * n:]


def kernel(x, norm1_g, w_in, q_norm_g, k_norm_g, conv_dw_w, conv_dw_b, conv_ln_g, conv_ln_b, w_out, norm2_g, w_up, ffn_dw_w, ffn_dw_b, w_down, loss_target, m_norm1_g, m_w_in, m_q_norm_g, m_k_norm_g, m_conv_dw_w, m_conv_dw_b, m_conv_ln_g, m_conv_ln_b, m_w_out, m_norm2_g, m_w_up, m_ffn_dw_w, m_ffn_dw_b, m_w_down, v_norm1_g, v_w_in, v_q_norm_g, v_k_norm_g, v_conv_dw_w, v_conv_dw_b, v_conv_ln_g, v_conv_ln_b, v_w_out, v_norm2_g, v_w_up, v_ffn_dw_w, v_ffn_dw_b, v_w_down):
    depth, d_model, in_shard = w_in.shape
    out_shard = w_out.shape[1]
    up_shard = w_up.shape[2]
    down_shard = w_down.shape[1]
    d_ff = down_shard * N_DEV
    conv_w = conv_dw_b.shape[1]
    cw_shard = conv_dw_w.shape[2]
    fw_shard = ffn_dw_w.shape[2]
    me = 4 * lax.axis_index("x") + 2 * lax.axis_index("y") + lax.axis_index("c")

    transposed = lambda a: a.transpose(0, 2, 1)
    b_in, b_out, b_up, b_down = (transposed(w_in).astype(BF16), w_out.astype(BF16), transposed(w_up).astype(BF16),
                                 w_down.astype(BF16))
    rows_major = lambda g: g.reshape(N_DEV * g.shape[1], g.shape[2])
    g_in0, g_cw, g_fw = _gather([b_in[0], conv_dw_w, ffn_dw_w], name="gather_first")
    wf_in, wf_out, wf_up, wf_down = [rows_major(g_in0)] + [None] * (depth - 1), [None] * depth, [None] * depth, [None] * depth
    cwf = g_cw.transpose(1, 2, 0, 3).reshape(depth, CONV_KERNEL, conv_w)
    fwf = g_fw.transpose(1, 2, 0, 3).reshape(depth, FFN_KERNEL, d_ff)

    row = lambda a, l: a[l].reshape(1, -1)
    both_heads = lambda a, l: jnp.tile(row(a, l), (1, HEADS_PER_STEP))

    xs = x[0]
    saved = []
    for l in range(depth):
        proj, h1, _ = _mm_rms(xs, row(norm1_g, l), wf_in[l], F32, [], name="fwd_in")
        sends = [b_out[l], b_up[l]] + ([b_in[l + 1]] if l + 1 < depth else [])
        attn, tsum, got = _attn_fwd(proj, both_heads(q_norm_g, l), both_heads(k_norm_g, l), sends, name="fwd_attn")
        wf_out[l], wf_up[l] = rows_major(got[0]), rows_major(got[1])
        if l + 1 < depth:
            wf_in[l + 1] = rows_major(got[2])
        conv, conv_y = _conv_fwd(proj, cwf[l], row(conv_dw_b, l), row(conv_ln_g, l), row(conv_ln_b, l),
                                 name="fwd_conv")
        cat = jnp.concatenate([attn, conv], axis=1)
        x_mid = _mm_res(cat, wf_out[l], xs, name="fwd_out")
        up, h2, got = _mm_rms(x_mid, row(norm2_g, l), wf_up[l], BF16, [b_down[l]], name="fwd_up")
        wf_down[l] = rows_major(got[0])
        act = _ffn_fwd(up, fwf[l], row(ffn_dw_b, l), name="fwd_ffn")
        x_next = _mm_res(act, wf_down[l], x_mid, name="fwd_down")
        saved.append((xs, h1, proj, tsum, cat, x_mid, h2, up, act, conv_y))
        xs = x_next

    loss_tile, dx, dxb = _loss_head(xs, loss_target[0], name="loss_head")
    loss = lax.psum(loss_tile[0, 0], ("x", "y", "c"))

    r_in, r_out, r_up, r_down = [None] * depth, [None] * depth, [None] * depth, [None] * depth
    row_blocks = lambda g: g.reshape(N_DEV, g.shape[0] // N_DEV, g.shape[1])
    small = {k: [None] * depth for k in ("norm1_g", "q_norm_g", "k_norm_g", "conv_dw_w", "conv_dw_b", "conv_ln_g",
                                         "conv_ln_b", "norm2_g", "ffn_dw_w", "ffn_dw_b")}
    gw_in = None
    for l in reversed(range(depth)):
        xs, h1, proj, tsum, cat, x_mid, h2, up, act, conv_y = saved[l]
        dact = _mm_nt(dxb, wf_down[l], BF16, name="bwd_dact")
        gw_down = _mm_tn(act, dxb, name="bwd_gw_down")
        dup, small["ffn_dw_w"][l], small["ffn_dw_b"][l] = _ffn_bwd(up, dact, fwf[l], row(ffn_dw_b, l), name="bwd_ffn")
        gw_up = _mm_tn(dup, h2, name="bwd_gw_up")
        dx, dxb, small["norm2_g"][l], _ = _mm_rmsbwd(dup, wf_up[l], x_mid, row(norm2_g, l), dx, [], name="bwd_up")
        dcat = _mm_nt(dxb, wf_out[l], F32, name="bwd_dcat")
        gw_out = _mm_tn(cat, dxb, name="bwd_gw_out")
        (dglu, small["conv_dw_w"][l], small["conv_dw_b"][l], small["conv_ln_g"][l], small["conv_ln_b"][l]), got = (
            _conv_bwd(proj, conv_y, dcat, cwf[l], row(conv_ln_g, l), row(conv_ln_b, l), [row_blocks(gw_out)],
                      name="bwd_conv"))
        r_out[l] = got[0]
        sends = [row_blocks(gw_down), row_blocks(gw_up)] + ([row_blocks(gw_in)] if l + 1 < depth else [])
        (dq, dk, dv, dqg, dkg), got = _attn_bwd(proj, dcat, tsum, both_heads(q_norm_g, l), both_heads(k_norm_g, l),
                                                sends, name="bwd_attn")
        r_down[l], r_up[l] = got[:2]
        if l + 1 < depth:
            r_in[l + 1] = got[2]
        small["q_norm_g"][l] = dqg[0:1, :HEAD_DIM] + dqg[0:1, HEAD_DIM:]
        small["k_norm_g"][l] = dkg[0:1, :HEAD_DIM] + dkg[0:1, HEAD_DIM:]
        dproj = jnp.concatenate([dq, dk, dv, dglu], axis=1)
        gw_in = _mm_tn(dproj, h1, name="bwd_gw_in")
        sends = [row_blocks(gw_in)] if l == 0 else []
        dx, dxb, small["norm1_g"][l], got = _mm_rmsbwd(dproj, wf_in[l], xs, row(norm1_g, l), dx, sends, name="bwd_in")
        if l == 0:
            r_in[0] = got[0]
    grad_x = dx[None]

    names = ["norm1_g", "q_norm_g", "k_norm_g", "conv_dw_w", "conv_dw_b", "conv_ln_g", "conv_ln_b", "norm2_g",
             "ffn_dw_w", "ffn_dw_b"]
    full_shapes = {"norm1_g": norm1_g.shape, "q_norm_g": q_norm_g.shape, "k_norm_g": k_norm_g.shape,
                   "conv_dw_w": (depth, CONV_KERNEL, conv_w), "conv_dw_b": conv_dw_b.shape,
                   "conv_ln_g": conv_ln_g.shape, "conv_ln_b": conv_ln_b.shape, "norm2_g": norm2_g.shape,
                   "ffn_dw_w": (depth, FFN_KERNEL, d_ff), "ffn_dw_b": ffn_dw_b.shape}
    partial = [jnp.stack(small[k]).reshape(full_shapes[k]) for k in names]
    big = {}
    big["w_out"], all_partials = _sum_adamw(r_out, w_out, m_w_out, v_w_out, partial, name="adamw_out")
    big["w_up"], _ = _sum_adamw(r_up, transposed(w_up), transposed(m_w_up), transposed(v_w_up), [], name="adamw_up")
    big["w_down"], _ = _sum_adamw(r_down, w_down, m_w_down, v_w_down, [], name="adamw_down")
    big["w_in"], _ = _sum_adamw(r_in, transposed(w_in), transposed(m_w_in), transposed(v_w_in), [], name="adamw_in")
    for k in ("w_in", "w_up"):
        big[k] = [transposed(a) for a in big[k]]

    grads = dict(zip(names, _sum_small(all_partials, name="sum_small_grads")))
    grads["conv_dw_w"] = lax.dynamic_slice_in_dim(grads["conv_dw_w"], me * cw_shard, cw_shard, axis=2)
    grads["ffn_dw_w"] = lax.dynamic_slice_in_dim(grads["ffn_dw_w"], me * fw_shard, fw_shard, axis=2)
    weights = dict(norm1_g=norm1_g, q_norm_g=q_norm_g, k_norm_g=k_norm_g, conv_dw_w=conv_dw_w, conv_dw_b=conv_dw_b,
                   conv_ln_g=conv_ln_g, conv_ln_b=conv_ln_b, norm2_g=norm2_g, ffn_dw_w=ffn_dw_w, ffn_dw_b=ffn_dw_b)
    m_in = dict(norm1_g=m_norm1_g, q_norm_g=m_q_norm_g, k_norm_g=m_k_norm_g, conv_dw_w=m_conv_dw_w,
                conv_dw_b=m_conv_dw_b, conv_ln_g=m_conv_ln_g, conv_ln_b=m_conv_ln_b, norm2_g=m_norm2_g,
                ffn_dw_w=m_ffn_dw_w, ffn_dw_b=m_ffn_dw_b)
    v_in = dict(norm1_g=v_norm1_g, q_norm_g=v_q_norm_g, k_norm_g=v_k_norm_g, conv_dw_w=v_conv_dw_w,
                conv_dw_b=v_conv_dw_b, conv_ln_g=v_conv_ln_g, conv_ln_b=v_conv_ln_b, norm2_g=v_norm2_g,
                ffn_dw_w=v_ffn_dw_w, ffn_dw_b=v_ffn_dw_b)
    d_s, m_s, v_s = _adamw_small([weights[k] for k in names], [grads[k] for k in names], [m_in[k] for k in names],
                                 [v_in[k] for k in names], name="adamw_small")
    delta, new_m, new_v = dict(zip(names, d_s)), dict(zip(names, m_s)), dict(zip(names, v_s))
    for k, (g, d, mn, vn) in big.items():
        grads[k], delta[k], new_m[k], new_v[k] = g, d, mn, vn

    order = ["norm1_g", "w_in", "q_norm_g", "k_norm_g", "conv_dw_w", "conv_dw_b", "conv_ln_g", "conv_ln_b", "w_out",
             "norm2_g", "w_up", "ffn_dw_w", "ffn_dw_b", "w_down"]
    return (loss, grad_x, *[grads[k] for k in order], *[delta[k] for k in order], *[new_m[k] for k in order],
            *[new_v[k] for k in order])
```

```python
import jax
import jax.numpy as jnp
from jax import lax
from jax.experimental import pallas as pl
from jax.experimental.pallas import tpu as pltpu

F32 = jnp.float32
BF16 = jnp.bfloat16

N_DEV = 8
HEADS = 8
HEAD_DIM = 64
ATTN_WIDTH = HEADS * HEAD_DIM
CONV_KERNEL = 31
FFN_KERNEL = 3
EPS = 1e-6
BLK = 128
KEY_GROUP = 4
LANES = 128
NORM_ROWS = 128
CONV_HALO = 32
FFN_HALO = 16

ADAM_LR = 0.001
ADAM_B1 = 0.9
ADAM_B2 = 0.999
ADAM_EPS = 1e-08
ADAM_WD = 0.01
ADAM_STEP = 10

VMEM_LIMIT = 56 * 1024 * 1024


def _params(n_axes=0):
    kw = dict(vmem_limit_bytes=VMEM_LIMIT)
    if n_axes:
        kw["dimension_semantics"] = ("arbitrary",) * n_axes
    return pltpu.CompilerParams(**kw)


def _dot(a, b):
    return jnp.dot(a, b, preferred_element_type=F32)


def _dot_nt(a, b):
    return lax.dot_general(a, b, (((1,), (1,)), ((), ())), preferred_element_type=F32)


def _dot_tn(a, b):
    return lax.dot_general(a, b, (((0,), (0,)), ((), ())), preferred_element_type=F32)


def _sigmoid(x):
    return 1.0 / (1.0 + jnp.exp(-x))


def _split_bf16(x):
    hi = x.astype(BF16)
    lo = (x - hi.astype(F32)).astype(BF16)
    return hi, lo


def _pick(n, options):
    for t in options:
        if n % t == 0:
            return t
    return n


def _tile(n, cap):
    best = None
    for t in range(LANES, min(n, cap) + 1, LANES):
        if n % t == 0:
            best = t
    return best or n


def _mm_rms(x, g, wt, out_dtype, sends, name):
    m, k = x.shape
    n = wt.shape[0]
    tm = _tile(m, 2048)
    tn = _tile(n, 512)
    n_send = len(sends)
    grid = (m // tm, n // tn)

    def body(x_ref, g_ref, w_ref, *rest):
        send_refs, (o_ref, h_ref), rest = rest[:n_send], rest[n_send:n_send + 2], rest[n_send + 2:]
        got_refs, h_s, sems = rest[:n_send], rest[n_send], rest[n_send + 1:]
        if n_send:
            start, relay, finish = _two_level_gather(send_refs, got_refs, sems)
            is_first = (pl.program_id(0) == 0) & (pl.program_id(1) == 0)
            is_last = (pl.program_id(0) == grid[0] - 1) & (pl.program_id(1) == grid[1] - 1)
            pl.when(is_first)(start)
            pl.when(is_last)(relay)

        @pl.when(pl.program_id(1) == 0)
        def _():
            def chunk(c, _):
                rows = pl.ds(pl.multiple_of(c * NORM_ROWS, NORM_ROWS), NORM_ROWS)
                xv = x_ref[rows, :]
                r = lax.rsqrt(jnp.mean(xv * xv, axis=-1, keepdims=True) + EPS)
                hv = (xv * r * g_ref[...]).astype(BF16)
                h_s[rows, :] = hv
                h_ref[rows, :] = hv
                return 0

            lax.fori_loop(0, tm // NORM_ROWS, chunk, 0)

        o_ref[...] = _dot_nt(h_s[...], w_ref[...]).astype(out_dtype)
        if n_send:
            pl.when(is_last)(finish)

    out = pl.pallas_call(
        body, name=name, grid=grid,
        in_specs=[pl.BlockSpec((tm, k), lambda i, j: (i, 0)),
                  pl.BlockSpec((1, k), lambda i, j: (0, 0)),
                  pl.BlockSpec((tn, k), lambda i, j: (j, 0))] + [ANY_SPEC] * n_send,
        out_specs=[pl.BlockSpec((tm, tn), lambda i, j: (i, j)),
                   pl.BlockSpec((tm, k), lambda i, j: (i, 0))] + [ANY_SPEC] * n_send,
        out_shape=[jax.ShapeDtypeStruct((m, n), out_dtype), jax.ShapeDtypeStruct((m, k), BF16)]
        + _gathered_shapes(sends),
        scratch_shapes=[pltpu.VMEM((tm, k), BF16)] + (_exchange_sems(n_send) if n_send else []),
        compiler_params=_params(2),
    )(x, g, wt, *sends)
    return out[0], out[1], out[2:]


def _mm_res(a, w, res, name):
    m, k = a.shape
    n = w.shape[1]
    tm = _tile(m, 1024)
    tn = _tile(n, 512)

    def body(a_ref, w_ref, r_ref, o_ref):
        o_ref[...] = r_ref[...] + _dot(a_ref[...], w_ref[...])

    return pl.pallas_call(
        body, name=name, grid=(m // tm, n // tn),
        in_specs=[pl.BlockSpec((tm, k), lambda i, j: (i, 0)),
                  pl.BlockSpec((k, tn), lambda i, j: (0, j)),
                  pl.BlockSpec((tm, tn), lambda i, j: (i, j))],
        out_specs=pl.BlockSpec((tm, tn), lambda i, j: (i, j)),
        out_shape=jax.ShapeDtypeStruct((m, n), F32),
        compiler_params=_params(2),
    )(a, w, res)


def _mm_nt(a, w, out_dtype, name):
    m, k = a.shape
    n = w.shape[0]
    tm = _tile(m, 1024)
    tn = _tile(n, 1408)

    def body(a_ref, w_ref, o_ref):
        o_ref[...] = _dot_nt(a_ref[...], w_ref[...]).astype(out_dtype)

    return pl.pallas_call(
        body, name=name, grid=(m // tm, n // tn),
        in_specs=[pl.BlockSpec((tm, k), lambda i, j: (i, 0)),
                  pl.BlockSpec((tn, k), lambda i, j: (j, 0))],
        out_specs=pl.BlockSpec((tm, tn), lambda i, j: (i, j)),
        out_shape=jax.ShapeDtypeStruct((m, n), out_dtype),
        compiler_params=_params(2),
    )(a, w)


def _column_tiles(a, cap):
    if a.ndim == 2:
        s, c = a.shape
        tc = _tile(c, cap)
        return s, c, tc, lambda rows, index: pl.BlockSpec((rows, tc), lambda *g: (index(*g)[0], index(*g)[1]))
    slabs, s, width = a.shape
    tc = _tile(width, cap)
    per = width // tc
    return s, slabs * width, tc, lambda rows, index: pl.BlockSpec(
        (None, rows, tc), lambda *g: (index(*g)[1] // per, index(*g)[0], index(*g)[1] % per))


def _mm_tn(a, b, name):
    s, m, tm, a_spec = _column_tiles(a, 1408)
    n = b.shape[1]
    tn = _tile(n, 1024)

    def body(a_ref, b_ref, o_ref):
        o_ref[...] = _dot_tn(a_ref[...], b_ref[...]).astype(BF16)

    return pl.pallas_call(
        body, name=name, grid=(m // tm, n // tn),
        in_specs=[a_spec(s, lambda i, j: (0, i)),
                  pl.BlockSpec((s, tn), lambda i, j: (0, j))],
        out_specs=pl.BlockSpec((tm, tn), lambda i, j: (i, j)),
        out_shape=jax.ShapeDtypeStruct((m, n), BF16),
        compiler_params=_params(2),
    )(a, b)


def _mm_rmsbwd(a, w, x, g, dres, sends, name):
    m, k, tk, a_spec = _column_tiles(a, 1408)
    n = w.shape[1]
    tm = _tile(m, 1024)
    nk = k // tk
    n_send = len(sends)

    def body(a_ref, w_ref, x_ref, g_ref, r_ref, *rest):
        send_refs, (dx_ref, dxb_ref, dg_ref), rest = rest[:n_send], rest[n_send:n_send + 3], rest[n_send + 3:]
        got_refs, acc, sems = rest[:n_send], rest[n_send], rest[n_send + 1:]
        i, kk = pl.program_id(0), pl.program_id(1)
        if n_send:
            copies = _scatter_copies(send_refs, got_refs, sems)

            @pl.when((i == 0) & (kk == 0))
            def _():
                for cp in copies:
                    cp.start()

        part = _dot(a_ref[...], w_ref[...])

        @pl.when(kk == 0)
        def _():
            acc[...] = part

        @pl.when(kk > 0)
        def _():
            acc[...] += part

        @pl.when(kk == nk - 1)
        def _():
            def chunk(c, dgp):
                rows = pl.ds(pl.multiple_of(c * NORM_ROWS, NORM_ROWS), NORM_ROWS)
                dh = acc[rows, :]
                xv = x_ref[rows, :]
                r = lax.rsqrt(jnp.mean(xv * xv, axis=-1, keepdims=True) + EPS)
                xh = xv * r
                dxh = dh * g_ref[...]
                dx = r_ref[rows, :] + r * (dxh - xh * jnp.mean(dxh * xh, axis=-1, keepdims=True))
                dx_ref[rows, :] = dx
                dxb_ref[rows, :] = dx.astype(BF16)
                return dgp + jnp.sum(dh * xh, axis=0, keepdims=True)

            dgp = lax.fori_loop(0, tm // NORM_ROWS, chunk, jnp.zeros((1, n), F32))

            @pl.when(i == 0)
            def _():
                dg_ref[...] = dgp

            @pl.when(i > 0)
            def _():
                dg_ref[...] += dgp

        if n_send:
            @pl.when((i == m // tm - 1) & (kk == nk - 1))
            def _():
                for cp in copies:
                    cp.wait()

    out = pl.pallas_call(
        body, name=name, grid=(m // tm, nk),
        in_specs=[a_spec(tm, lambda i, kk: (i, kk)),
                  pl.BlockSpec((tk, n), lambda i, kk: (kk, 0)),
                  pl.BlockSpec((tm, n), lambda i, kk: (i, 0)),
                  pl.BlockSpec((1, n), lambda i, kk: (0, 0)),
                  pl.BlockSpec((tm, n), lambda i, kk: (i, 0))] + [ANY_SPEC] * n_send,
        out_specs=[pl.BlockSpec((tm, n), lambda i, kk: (i, 0)),
                   pl.BlockSpec((tm, n), lambda i, kk: (i, 0)),
                   pl.BlockSpec((1, n), lambda i, kk: (0, 0))] + [ANY_SPEC] * n_send,
        out_shape=[jax.ShapeDtypeStruct((m, n), F32), jax.ShapeDtypeStruct((m, n), BF16),
                   jax.ShapeDtypeStruct((1, n), F32)] + _scattered_shapes(sends),
        scratch_shapes=[pltpu.VMEM((tm, n), F32)] + (_exchange_sems(n_send) if n_send else []),
        compiler_params=_params(2),
    )(a, w, x, g, dres, *sends)
    return out[0], out[1], out[2], out[3:]


ANY_SPEC = pl.BlockSpec(memory_space=pl.ANY)
SEMS_PER_OPERAND = N_DEV - 1


def _exchange_sems(n):
    return [pltpu.SemaphoreType.DMA((n, SEMS_PER_OPERAND)), pltpu.SemaphoreType.DMA((n, SEMS_PER_OPERAND)),
            pltpu.SemaphoreType.DMA((n,))]


def _gathered_shapes(parts):
    return [jax.ShapeDtypeStruct((N_DEV,) + a.shape, a.dtype) for a in parts]


def _scattered_shapes(parts):
    return [jax.ShapeDtypeStruct(a.shape, a.dtype) for a in parts]


def _flat(pos):
    return 4 * pos[0] + 2 * pos[1] + pos[2]


def _remote(src, dst, sems, i, k, to):
    send_sems, recv_sems, _ = sems
    return pltpu.make_async_remote_copy(src_ref=src, dst_ref=dst, send_sem=send_sems.at[i, k],
                                        recv_sem=recv_sems.at[i, k], device_id=to,
                                        device_id_type=pl.DeviceIdType.MESH)


def _scatter_copies(ins, outs, sems):
    x, y, c = lax.axis_index("x"), lax.axis_index("y"), lax.axis_index("c")
    me = _flat((x, y, c))
    copies = [pltpu.make_async_copy(ins[i].at[me], outs[i].at[me], sems[2].at[i]) for i in range(len(ins))]
    for d in range(1, N_DEV):
        peer = (1 - x if d & 4 else x, 1 - y if d & 2 else y, 1 - c if d & 1 else c)
        for i in range(len(ins)):
            copies.append(_remote(ins[i].at[_flat(peer)], outs[i].at[me], sems, i, d - 1, peer))
    return copies


def _two_level_gather(ins, outs, sems):
    x, y, c = lax.axis_index("x"), lax.axis_index("y"), lax.axis_index("c")
    me, sibling = (x, y, c), (x, y, 1 - c)
    chips = [(1 - x, y), (x, 1 - y), (1 - x, 1 - y)]
    n = len(ins)

    def block(i, pos):
        return outs[i].at[_flat(pos)]

    local = [pltpu.make_async_copy(ins[i], block(i, me), sems[2].at[i]) for i in range(n)]
    own = [_remote(ins[i], block(i, me), sems, i, 0, sibling) for i in range(n)]
    own += [_remote(ins[i], block(i, me), sems, i, 1 + j, (*chip, c)) for i in range(n) for j, chip in enumerate(chips)]
    passed = [[_remote(block(i, (*chip, c)), block(i, (*chip, c)), sems, i, 4 + j, sibling) for i in range(n)]
              for j, chip in enumerate(chips)]

    def first():
        for cp in local + own:
            cp.start()

    def relay():
        for j, chip in enumerate(chips):
            for i in range(n):
                _remote(ins[i], block(i, (*chip, c)), sems, i, 1 + j, me).wait_recv()
                passed[j][i].start()

    def finish():
        for i in range(n):
            _remote(ins[i], block(i, sibling), sems, i, 0, me).wait_recv()
            for j, chip in enumerate(chips):
                _remote(ins[i], block(i, (*chip, 1 - c)), sems, i, 4 + j, me).wait_recv()
        for cp in own + [cp for row in passed for cp in row]:
            cp.wait_send()
        for cp in local:
            cp.wait()

    return first, relay, finish


def _gather(parts, name):
    n = len(parts)

    def body(*refs):
        start, relay, finish = _two_level_gather(refs[:n], refs[n:2 * n], refs[2 * n:])
        start()
        relay()
        finish()

    return pl.pallas_call(
        body, name=name, in_specs=[ANY_SPEC] * n, out_specs=[ANY_SPEC] * n,
        out_shape=_gathered_shapes(parts), scratch_shapes=_exchange_sems(n),
    )(*parts)


def _tri(kind):
    j = lax.broadcasted_iota(jnp.int32, (BLK, BLK), 0)
    s = lax.broadcasted_iota(jnp.int32, (BLK, BLK), 1)
    m = {"after": j > s, "upto": j <= s, "before": j < s}[kind]
    return jnp.concatenate([jnp.where(m, 1.0, 0.0), jnp.ones((BLK, BLK), F32)], axis=1).astype(BF16)


def _scan_rows(v, tri):
    r = _dot(v.astype(BF16), tri)
    return r[:, :BLK], r[:, BLK:]


HEADS_PER_STEP = LANES // HEAD_DIM


def _first_head_lanes():
    return lax.broadcasted_iota(jnp.int32, (1, LANES), 1) < HEAD_DIM


def _pair_mean(v, first):
    m0 = jnp.sum(jnp.where(first, v, 0.0), axis=-1, keepdims=True)
    m1 = jnp.sum(jnp.where(first, 0.0, v), axis=-1, keepdims=True)
    return jnp.where(first, m0, m1) * (1.0 / HEAD_DIM)


def _pair_norm(v, g2, first):
    return v * lax.rsqrt(_pair_mean(v * v, first) + EPS) * g2


def _pair_norm_bwd(raw, g2, dn, first):
    r = lax.rsqrt(_pair_mean(raw * raw, first) + EPS)
    xh = raw * r
    dxh = dn * g2
    return r * (dxh - xh * _pair_mean(dxh * xh, first)), jnp.sum(dn * xh, axis=0, keepdims=True)


def _block_diag(v, first):
    zero = jnp.zeros_like(v)
    return jnp.concatenate([jnp.where(first, v, zero), jnp.where(first, zero, v)], axis=0)


def _attn_prep(q_ref, k_ref, v_ref, qg_ref, kg_ref, qc_s, kc_s, vd_s, kd_s, n_blk):
    scale = HEAD_DIM ** -0.5
    first = _first_head_lanes()

    def prep(i, _):
        rows = pl.ds(pl.multiple_of(i * BLK, BLK), BLK)
        both = pl.ds(pl.multiple_of(i * 2 * BLK, 2 * BLK), 2 * BLK)
        qh, ql = _split_bf16(_pair_norm(q_ref[rows, :], qg_ref[...], first) * scale)
        kh, kl = _split_bf16(_pair_norm(k_ref[rows, :], kg_ref[...], first))
        for h in range(HEADS_PER_STEP):
            sl = slice(h * HEAD_DIM, (h + 1) * HEAD_DIM)
            qc_s[h, rows, :] = jnp.concatenate([qh[:, sl], ql[:, sl], qh[:, sl], ql[:, sl]], axis=1)
            kc_s[h, rows, :] = jnp.concatenate([kh[:, sl], kh[:, sl], kl[:, sl], kl[:, sl]], axis=1)
        vd_s[both, :] = _block_diag(v_ref[rows, :].astype(BF16), first)
        if kd_s is not None:
            kd_s[both, :] = _block_diag(kh, first)
        return 0

    lax.fori_loop(0, n_blk, prep, 0)


def _pair_scores(qc, kc_ref, grp, n_tiles=KEY_GROUP):
    zs = []
    for j in range(0, n_tiles, 2):
        two = pl.ds(pl.multiple_of((grp * KEY_GROUP + j) * BLK, 2 * BLK), 2 * BLK)
        z = _dot_nt(qc, kc_ref[two, :])
        zs += [z[:, :BLK], z[:, BLK:]]
    return zs[:n_tiles]


def _col_minus_row():
    row = lax.broadcasted_iota(jnp.int32, (BLK, BLK), 0)
    col = lax.broadcasted_iota(jnp.int32, (BLK, BLK), 1)
    return col - row


def _softplus(z):
    return jnp.maximum(z, 0.0) + jnp.log(1.0 + jnp.exp(-jnp.abs(z)))


def _attn_fwd(proj, qg, kg, sends, name):
    s = proj.shape[0]
    n_blk = s // BLK
    pairs = ATTN_WIDTH // LANES
    n_send = len(sends)

    def body(q_ref, k_ref, v_ref, qg_ref, kg_ref, *rest):
        send_refs, (o_ref, t_ref), rest = rest[:n_send], rest[n_send:n_send + 2], rest[n_send + 2:]
        got_refs, (qc_s, kc_s, vd_s), sems = rest[:n_send], rest[n_send:n_send + 3], rest[n_send + 3:]
        start, relay, finish = _two_level_gather(send_refs, got_refs, sems)
        step = pl.program_id(0)
        pl.when(step == 0)(start)
        pl.when(step == pairs - 1)(relay)
        tri = _tri("after")
        diff = _col_minus_row()
        first = _first_head_lanes()
        heads = range(HEADS_PER_STEP)
        _attn_prep(q_ref, k_ref, v_ref, qg_ref, kg_ref, qc_s, kc_s, vd_s, None, n_blk)

        below = diff < 0

        def group(qc, grp, carry, acc, n_tiles, diagonal):
            blocks = [grp * KEY_GROUP + j for j in reversed(range(n_tiles))]
            zs = [_pair_scores(qc[h], kc_s.at[h], grp, n_tiles)[::-1] for h in heads]
            parts = [[None] * n_tiles for _ in heads]
            for h in heads:
                for j, z in enumerate(zs[h]):
                    sp = _softplus(z)
                    lom = -sp
                    if diagonal and j == 0:
                        lom = jnp.where(below, lom, 0.0)
                    tail, tot = _scan_rows(lom, tri)
                    parts[h][j] = (z - sp + tail, tot)
            carry = list(carry)
            for j, kb in enumerate(blocks):
                ws = []
                for h in heads:
                    lw, tot = parts[h][j]
                    w = jnp.exp(lw + carry[h])
                    if diagonal and j == 0:
                        w = jnp.where(below, w, 0.0)
                    ws.append(w.astype(BF16))
                    carry[h] = carry[h] + tot
                acc = acc + _dot(jnp.concatenate(ws, axis=1),
                                 vd_s[pl.ds(pl.multiple_of(kb * 2 * BLK, 2 * BLK), 2 * BLK), :])
            return tuple(carry), acc

        def q_blocks(top, _):
            for r in range(KEY_GROUP):
                rows = pl.ds(pl.multiple_of((top * KEY_GROUP + r) * BLK, BLK), BLK)
                qc = [qc_s[h, rows, :] for h in heads]
                zero = jnp.zeros((BLK, BLK), F32)
                carry, acc = group(qc, top, (zero,) * HEADS_PER_STEP, jnp.zeros((BLK, LANES), F32), r + 1, True)
                carry, acc = lax.fori_loop(
                    0, top, lambda t, c: group(qc, top - 1 - t, c[0], c[1], KEY_GROUP, False), (carry, acc))
                o_ref[rows, :] = acc.astype(BF16)
                t_ref[rows, :] = jnp.where(first, carry[0], carry[1])
            return 0

        lax.fori_loop(0, n_blk // KEY_GROUP, q_blocks, 0)
        pl.when(step == pairs - 1)(finish)

    col = lambda off: pl.BlockSpec((s, LANES), lambda p: (0, off + p))
    vec = pl.BlockSpec((1, LANES), lambda p: (0, 0))
    out = pl.pallas_call(
        body, name=name, grid=(pairs,),
        in_specs=[col(0), col(pairs), col(2 * pairs), vec, vec] + [ANY_SPEC] * n_send,
        out_specs=[pl.BlockSpec((s, LANES), lambda p: (0, p))] * 2 + [ANY_SPEC] * n_send,
        out_shape=[jax.ShapeDtypeStruct((s, ATTN_WIDTH), BF16), jax.ShapeDtypeStruct((s, ATTN_WIDTH), F32)]
        + _gathered_shapes(sends),
        scratch_shapes=[pltpu.VMEM((HEADS_PER_STEP, s, 4 * HEAD_DIM), BF16)] * 2
        + [pltpu.VMEM((HEADS_PER_STEP * s, LANES), BF16)] + _exchange_sems(n_send),
        compiler_params=_params(1),
    )(proj, proj, proj, qg, kg, *sends)
    return out[0], out[1], out[2:]


def _attn_bwd(proj, dcat, tsum, qg, kg, sends, name):
    s = proj.shape[0]
    n_blk = s // BLK
    pairs = ATTN_WIDTH // LANES
    scale = HEAD_DIM ** -0.5
    n_send = len(sends)
    n_scratch = 7

    def body(q_ref, k_ref, v_ref, do_ref, t_ref, qg_ref, kg_ref, *rest):
        send_refs, rest = rest[:n_send], rest[n_send:]
        (dq_ref, dk_ref, dv_ref, dqg_ref, dkg_ref), rest = rest[:5], rest[5:]
        got_refs, scratch, sems = rest[:n_send], rest[n_send:n_send + n_scratch], rest[n_send + n_scratch:]
        qc_s, kc_s, vd_s, kd_s, qd_s, dob_s, dkv_s = scratch
        copies = _scatter_copies(send_refs, got_refs, sems)

        @pl.when(pl.program_id(0) == 0)
        def _():
            for cp in copies:
                cp.start()

        tri_p = _tri("upto")
        tri_h = _tri("before")
        diff = _col_minus_row()

        @pl.when(pl.program_id(0) == 0)
        def _():
            dqg_ref[...] = jnp.zeros_like(dqg_ref)
            dkg_ref[...] = jnp.zeros_like(dkg_ref)

        first = _first_head_lanes()
        heads = range(HEADS_PER_STEP)
        _attn_prep(q_ref, k_ref, v_ref, qg_ref, kg_ref, qc_s, kc_s, vd_s, kd_s, n_blk)

        def prep(i, _):
            rows = pl.ds(pl.multiple_of(i * BLK, BLK), BLK)
            both = pl.ds(pl.multiple_of(i * 2 * BLK, 2 * BLK), 2 * BLK)
            dob = do_ref[rows, :].astype(BF16)
            dob_s[rows, :] = dob
            none = jnp.zeros((BLK, HEAD_DIM), BF16)
            for h in heads:
                qd_s[h, both, :] = jnp.concatenate(
                    [jnp.concatenate([qc_s[h, rows, 0:HEAD_DIM], none], axis=1),
                     jnp.concatenate([none, dob[:, h * HEAD_DIM:(h + 1) * HEAD_DIM]], axis=1)], axis=0)
                dkv_s[h, rows, :] = jnp.zeros((BLK, LANES), F32)
            return 0

        lax.fori_loop(0, n_blk, prep, 0)

        def group(qc, qd, dob, tq, qi, grp, pc, hc, dq, masked):
            blocks = [grp * KEY_GROUP + j for j in range(KEY_GROUP)]
            cols_of = [pl.ds(pl.multiple_of(kb * BLK, BLK), BLK) for kb in blocks]
            both_of = [pl.ds(pl.multiple_of(kb * 2 * BLK, 2 * BLK), 2 * BLK) for kb in blocks]
            zs = [_pair_scores(qc[h], kc_s.at[h], grp) for h in heads]
            das =[_dot_nt(dob, vd_s[both, :]) for both in both_of]
            keeps = [diff < (qi - kb) * BLK if masked else None for kb in blocks]
            lbs = [[None] * KEY_GROUP for _ in heads]
            scans = [[None] * KEY_GROUP for _ in heads]
            for h in heads:
                for j, z in enumerate(zs[h]):
                    sp = _softplus(z)
                    lom = -sp
                    if masked:
                        lom = jnp.where(keeps[j], lom, 0.0)
                    lbs[h][j] = z - sp
                    scans[h][j] = _scan_rows(lom, tri_p)
            pc, hc = list(pc), list(hc)
            avs = [[None] * KEY_GROUP for _ in heads]
            gws = [[None] * KEY_GROUP for _ in heads]
            hscans = [[None] * KEY_GROUP for _ in heads]
            for h in heads:
                for j in range(KEY_GROUP):
                    p_in, p_tot = scans[h][j]
                    a = jnp.exp(lbs[h][j] + (tq[h] - pc[h] - p_in))
                    if masked:
                        a = jnp.where(keeps[j], a, 0.0)
                    pc[h] = pc[h] + p_tot
                    gw = das[j][:, h * BLK:(h + 1) * BLK] * a
                    avs[h][j] = a.astype(BF16)
                    gws[h][j] = gw
                    hscans[h][j] = _scan_rows(gw, tri_h)
            dzs = [[None] * KEY_GROUP for _ in heads]
            for h in heads:
                for j in range(KEY_GROUP):
                    h_in, g_tot = hscans[h][j]
                    gw = gws[h][j]
                    dz = gw - jnp.exp(lbs[h][j]) * (gw + hc[h] + h_in)
                    if masked:
                        dz = jnp.where(keeps[j], dz, 0.0)
                    hc[h] = hc[h] + g_tot
                    dzs[h][j] = dz.astype(BF16)
            for j, both in enumerate(both_of):
                dq = dq + _dot(jnp.concatenate([dzs[h][j] for h in heads], axis=1), kd_s[both, :])
            for h in heads:
                for j, cols in enumerate(cols_of):
                    dkv_s[h, cols, :] += _dot_tn(jnp.concatenate([dzs[h][j], avs[h][j]], axis=0), qd[h])
            return tuple(pc), tuple(hc), dq

        def q_block(qi, dqg):
            rows = pl.ds(pl.multiple_of(qi * BLK, BLK), BLK)
            both = pl.ds(pl.multiple_of(qi * 2 * BLK, 2 * BLK), 2 * BLK)
            qc = [qc_s[h, rows, :] for h in heads]
            qd = [qd_s[h, both, :] for h in heads]
            dob = dob_s[rows, :]
            tboth = t_ref[rows, :]
            tq = [jnp.concatenate([tboth[:, h * HEAD_DIM:(h + 1) * HEAD_DIM]] * 2, axis=1) for h in heads]
            zero = (jnp.zeros((BLK, BLK), F32),) * HEADS_PER_STEP
            top = qi // KEY_GROUP
            pc, hc, dq = lax.fori_loop(
                0, top, lambda grp, c: group(qc, qd, dob, tq, qi, grp, c[0], c[1], c[2], False),
                (zero, zero, jnp.zeros((BLK, LANES), F32)))
            _, _, dq = group(qc, qd, dob, tq, qi, top, pc, hc, dq, True)
            dq_raw, dg = _pair_norm_bwd(q_ref[rows, :], qg_ref[...], dq * scale, first)
            dq_ref[rows, :] = dq_raw.astype(BF16)
            return dqg + dg

        dqg = lax.fori_loop(0, n_blk, q_block, jnp.zeros((1, LANES), F32))

        def finish(i, dkg):
            rows = pl.ds(pl.multiple_of(i * BLK, BLK), BLK)
            dk = jnp.concatenate([dkv_s[h, rows, 0:HEAD_DIM] for h in heads], axis=1)
            dv = jnp.concatenate([dkv_s[h, rows, HEAD_DIM:2 * HEAD_DIM] for h in heads], axis=1)
            dk_raw, dg = _pair_norm_bwd(k_ref[rows, :], kg_ref[...], dk, first)
            dk_ref[rows, :] = dk_raw.astype(BF16)
            dv_ref[rows, :] = dv.astype(BF16)
            return dkg + dg

        dkg = lax.fori_loop(0, n_blk, finish, jnp.zeros((1, LANES), F32))
        dqg_ref[0:1, :] += dqg
        dkg_ref[0:1, :] += dkg

        @pl.when(pl.program_id(0) == pairs - 1)
        def _():
            for cp in copies:
                cp.wait()

    col = lambda off: pl.BlockSpec((s, LANES), lambda p: (0, off + p))
    vec = pl.BlockSpec((1, LANES), lambda p: (0, 0))
    small = pl.BlockSpec((8, LANES), lambda p: (0, 0))
    out = pl.pallas_call(
        body, name=name, grid=(pairs,),
        in_specs=[col(0), col(pairs), col(2 * pairs), col(0), col(0), vec, vec] + [ANY_SPEC] * n_send,
        out_specs=[col(0)] * 3 + [small] * 2 + [ANY_SPEC] * n_send,
        out_shape=[jax.ShapeDtypeStruct((s, ATTN_WIDTH), BF16)] * 3 + [jax.ShapeDtypeStruct((8, LANES), F32)] * 2
        + _scattered_shapes(sends),
        scratch_shapes=[pltpu.VMEM((HEADS_PER_STEP, s, 4 * HEAD_DIM), BF16)] * 2
        + [pltpu.VMEM((HEADS_PER_STEP * s, LANES), BF16)] * 2
        + [pltpu.VMEM((HEADS_PER_STEP, HEADS_PER_STEP * s, LANES), BF16), pltpu.VMEM((s, LANES), BF16),
           pltpu.VMEM((HEADS_PER_STEP, s, LANES), F32)] + _exchange_sems(n_send),
        compiler_params=_params(1),
    )(proj, proj, proj, dcat, tsum, qg, kg, *sends)
    return out[:5], out[5:]


CONV_ROWS = 128


def _shifted(window, shift, halo):
    if shift == 0:
        return window[halo:, :]
    return pltpu.roll(window, shift, 0)[halo:, :]


SUBLANES = 8


def _row_shifts(window, up):
    n = window.shape[0]
    return [window] + [pltpu.roll(window, n - b if up else b, 0) for b in range(1, SUBLANES)]


def _earlier(shifts, back, rows):
    a, b = divmod(back, SUBLANES)
    return shifts[b][CONV_HALO - SUBLANES * a:CONV_HALO - SUBLANES * a + rows, :]


def _later(shifts, ahead, rows):
    a, b = divmod(ahead, SUBLANES)
    return shifts[b][SUBLANES * a:SUBLANES * a + rows, :]


def _lane_blocks(width):
    return [slice(c, c + LANES) for c in range(0, width, LANES)]


def _conv_taps(shifts, w_ref, lanes, rows):
    y = None
    for k in range(CONV_KERNEL):
        term = _earlier(shifts, CONV_KERNEL - 1 - k, rows) * w_ref[k:k + 1, lanes]
        y = term if y is None else y + term
    return y


def _conv_fwd(proj, w, b, lg, lb, name):
    s = proj.shape[0]
    cw = w.shape[1]
    rows = CONV_ROWS
    blk_a = (proj.shape[1] - 2 * cw) // cw

    def body(a_ref, g_ref, w_ref, b_ref, lg_ref, lb_ref, o_ref, y_ref, u_s):
        u_s[0:CONV_HALO, :] = jnp.zeros((CONV_HALO, cw), F32)

        def glu(i, _):
            r0 = pl.multiple_of(i * rows, rows)
            u_s[pl.ds(CONV_HALO + r0, rows), :] = a_ref[pl.ds(r0, rows), :] * _sigmoid(g_ref[pl.ds(r0, rows), :])
            return 0

        lax.fori_loop(0, s // rows, glu, 0)

        def chunk(i, _):
            r0 = pl.multiple_of(i * rows, rows)
            for lanes in _lane_blocks(cw):
                shifts = _row_shifts(u_s[pl.ds(r0, CONV_HALO + rows), lanes], False)
                y_ref[pl.ds(r0, rows), lanes] = _conv_taps(shifts, w_ref, lanes, rows) + b_ref[:, lanes]
            y = y_ref[pl.ds(r0, rows), :]
            yc = y - jnp.mean(y, axis=-1, keepdims=True)
            n = yc * lax.rsqrt(jnp.mean(yc * yc, axis=-1, keepdims=True) + EPS)
            ln = n * lg_ref[...] + lb_ref[...]
            o_ref[pl.ds(r0, rows), :] = (ln * _sigmoid(ln)).astype(BF16)
            return 0

        lax.fori_loop(0, s // rows, chunk, 0)

    vec = pl.BlockSpec((1, cw), lambda i: (0, 0))
    return pl.pallas_call(
        body, name=name, grid=(1,),
        in_specs=[pl.BlockSpec((s, cw), lambda i: (0, blk_a)), pl.BlockSpec((s, cw), lambda i: (0, blk_a + 1)),
                  pl.BlockSpec((CONV_KERNEL, cw), lambda i: (0, 0)), vec, vec, vec],
        out_specs=[pl.BlockSpec((s, cw), lambda i: (0, 0))] * 2,
        out_shape=[jax.ShapeDtypeStruct((s, cw), BF16), jax.ShapeDtypeStruct((s, cw), F32)],
        scratch_shapes=[pltpu.VMEM((CONV_HALO + s, cw), F32)],
        compiler_params=_params(1),
    )(proj, proj, w, b, lg, lb)


def _conv_bwd(proj, y, dcat, w, lg, lb, sends, name):
    s = proj.shape[0]
    cw = w.shape[1]
    rows = CONV_ROWS
    blk_a = (proj.shape[1] - 2 * cw) // cw
    n_chunk = s // rows
    n_send = len(sends)

    def body(a_ref, g_ref, y_ref, dc_ref, w_ref, lg_ref, lb_ref, *rest):
        send_refs, (o_ref, dw_ref, db_ref, dlg_ref, dlb_ref), rest = rest[:n_send], rest[n_send:n_send + 5], rest[n_send + 5:]
        got_refs, (u_s, dy_s, dw_s), sems = rest[:n_send], rest[n_send:n_send + 3], rest[n_send + 3:]
        copies = _scatter_copies(send_refs, got_refs, sems)
        for cp in copies:
            cp.start()
        u_s[0:CONV_HALO, :] = jnp.zeros((CONV_HALO, cw), F32)
        dy_s[pl.ds(s, CONV_HALO), :] = jnp.zeros((CONV_HALO, cw), F32)
        dw_s[...] = jnp.zeros_like(dw_s)

        def glu(i, _):
            r0 = pl.multiple_of(i * rows, rows)
            u_s[pl.ds(CONV_HALO + r0, rows), :] = a_ref[pl.ds(r0, rows), :] * _sigmoid(g_ref[pl.ds(r0, rows), :])
            return 0

        lax.fori_loop(0, n_chunk, glu, 0)

        def chunk(i, carry):
            db, dlg, dlb = carry
            r0 = pl.multiple_of(i * rows, rows)
            y = y_ref[pl.ds(r0, rows), :]
            yc = y - jnp.mean(y, axis=-1, keepdims=True)
            r = lax.rsqrt(jnp.mean(yc * yc, axis=-1, keepdims=True) + EPS)
            n = yc * r
            ln = n * lg_ref[...] + lb_ref[...]
            sg = _sigmoid(ln)
            dln = dc_ref[pl.ds(r0, rows), :] * (sg * (1.0 + ln * (1.0 - sg)))
            dn = dln * lg_ref[...]
            dy = r * (dn - jnp.mean(dn, axis=-1, keepdims=True) - n * jnp.mean(dn * n, axis=-1, keepdims=True))
            dy_s[pl.ds(r0, rows), :] = dy
            for lanes in _lane_blocks(cw):
                shifts = _row_shifts(u_s[pl.ds(r0, CONV_HALO + rows), lanes], False)
                dy_part = dy[:, lanes]
                for k in range(CONV_KERNEL):
                    prod = _earlier(shifts, CONV_KERNEL - 1 - k, rows) * dy_part
                    dw_s[k, :, lanes] += jnp.sum(prod.reshape(rows // SUBLANES, SUBLANES, LANES), axis=0)
            return (db + jnp.sum(dy, axis=0, keepdims=True),
                    dlg + jnp.sum(dln * n, axis=0, keepdims=True),
                    dlb + jnp.sum(dln, axis=0, keepdims=True))

        zero = jnp.zeros((1, cw), F32)
        db, dlg, dlb = lax.fori_loop(0, n_chunk, chunk, (zero, zero, zero))
        db_ref[...] = db
        dlg_ref[...] = dlg
        dlb_ref[...] = dlb
        for k in range(CONV_KERNEL):
            dw_ref[k:k + 1, :] = jnp.sum(dw_s[k], axis=0, keepdims=True)

        def back(i, _):
            r0 = pl.multiple_of(i * rows, rows)
            for lanes in _lane_blocks(cw):
                shifts = _row_shifts(dy_s[pl.ds(r0, rows + CONV_HALO), lanes], True)
                du = None
                for k in range(CONV_KERNEL):
                    term = _later(shifts, CONV_KERNEL - 1 - k, rows) * w_ref[k:k + 1, lanes]
                    du = term if du is None else du + term
                av = a_ref[pl.ds(r0, rows), lanes]
                sg = _sigmoid(g_ref[pl.ds(r0, rows), lanes])
                o_ref[pl.ds(r0, rows), lanes] = (du * sg).astype(BF16)
                o_ref[pl.ds(r0, rows), slice(cw + lanes.start, cw + lanes.stop)] = (du * av * sg * (1.0 - sg)).astype(BF16)
            return 0

        lax.fori_loop(0, n_chunk, back, 0)
        for cp in copies:
            cp.wait()

    vec = pl.BlockSpec((1, cw), lambda i: (0, 0))
    wspec = pl.BlockSpec((CONV_KERNEL, cw), lambda i: (0, 0))
    out = pl.pallas_call(
        body, name=name, grid=(1,),
        in_specs=[pl.BlockSpec((s, cw), lambda i: (0, blk_a)), pl.BlockSpec((s, cw), lambda i: (0, blk_a + 1)),
                  pl.BlockSpec((s, cw), lambda i: (0, 0)), pl.BlockSpec((s, cw), lambda i: (0, 1)), wspec, vec, vec]
        + [ANY_SPEC] * n_send,
        out_specs=[pl.BlockSpec((s, 2 * cw), lambda i: (0, 0)), wspec, vec, vec, vec] + [ANY_SPEC] * n_send,
        out_shape=[jax.ShapeDtypeStruct((s, 2 * cw), BF16), jax.ShapeDtypeStruct((CONV_KERNEL, cw), F32)]
        + [jax.ShapeDtypeStruct((1, cw), F32)] * 3 + _scattered_shapes(sends),
        scratch_shapes=[pltpu.VMEM((CONV_HALO + s, cw), F32), pltpu.VMEM((s + CONV_HALO, cw), F32),
                        pltpu.VMEM((CONV_KERNEL, 8, cw), F32)] + _exchange_sems(n_send),
        compiler_params=_params(1),
    )(proj, proj, y, dcat, w, lg, lb, *sends)
    return out[:5], out[5:]


FFN_ROWS = 256


def _ffn_gate(g_ref, r0, rows, w_ref, b_ref):
    cur = g_ref[pl.ds(r0, rows), :].astype(F32)
    prev = g_ref[pl.ds(pl.multiple_of(jnp.maximum(r0 - FFN_HALO, 0), FFN_HALO), FFN_HALO), :].astype(F32)
    prev = jnp.where(r0 > 0, prev, 0.0)
    window = jnp.concatenate([prev, cur], axis=0)
    gc = cur * w_ref[FFN_KERNEL - 1:FFN_KERNEL, :] + b_ref[...]
    for k in range(FFN_KERNEL - 1):
        gc = gc + _shifted(window, FFN_KERNEL - 1 - k, FFN_HALO) * w_ref[k:k + 1, :]
    return gc, window


def _ffn_fwd(up, w, b, name):
    s = up.shape[0]
    f = w.shape[1]
    tc = _pick(f, (256, 128))
    nc = f // tc
    rows = _pick(s, (FFN_ROWS, 128))

    def body(g_ref, v_ref, w_ref, b_ref, o_ref):
        def chunk(i, _):
            r0 = pl.multiple_of(i * rows, rows)
            gc, _w = _ffn_gate(g_ref, r0, rows, w_ref, b_ref)
            o_ref[pl.ds(r0, rows), :] = (gc * _sigmoid(gc) * v_ref[pl.ds(r0, rows), :].astype(F32)).astype(BF16)
            return 0

        lax.fori_loop(0, s // rows, chunk, 0)

    return pl.pallas_call(
        body, name=name, grid=(nc,),
        in_specs=[pl.BlockSpec((s, tc), lambda j: (0, j)), pl.BlockSpec((s, tc), lambda j: (0, nc + j)),
                  pl.BlockSpec((FFN_KERNEL, tc), lambda j: (0, j)), pl.BlockSpec((1, tc), lambda j: (0, j))],
        out_specs=pl.BlockSpec((s, tc), lambda j: (0, j)),
        out_shape=jax.ShapeDtypeStruct((s, f), BF16),
        compiler_params=_params(1),
    )(up, up, w, b)


def _ffn_bwd(up, dact, w, b, name):
    s = up.shape[0]
    f = w.shape[1]
    tc = _pick(f, (256, 128))
    nc = f // tc
    rows = _pick(s, (FFN_ROWS, 128))
    n_chunk = s // rows

    def body(g_ref, v_ref, da_ref, w_ref, b_ref, d_ref, dw_ref, db_ref, dgc_s):
        dg_ref, dv_ref = d_ref.at[0], d_ref.at[1]
        dgc_s[pl.ds(s, FFN_HALO), :] = jnp.zeros((FFN_HALO, tc), F32)

        def chunk(i, carry):
            r0 = pl.multiple_of(i * rows, rows)
            gc, window = _ffn_gate(g_ref, r0, rows, w_ref, b_ref)
            sg = _sigmoid(gc)
            da = da_ref[pl.ds(r0, rows), :].astype(F32)
            dv_ref[pl.ds(r0, rows), :] = (da * gc * sg).astype(BF16)
            dgc = da * v_ref[pl.ds(r0, rows), :].astype(F32) * (sg * (1.0 + gc * (1.0 - sg)))
            dgc_s[pl.ds(r0, rows), :] = dgc
            out = [carry[0] + jnp.sum(dgc, axis=0, keepdims=True)]
            for k in range(FFN_KERNEL):
                out.append(carry[1 + k] + jnp.sum(_shifted(window, FFN_KERNEL - 1 - k, FFN_HALO) * dgc,
                                                  axis=0, keepdims=True))
            return tuple(out)

        zero = jnp.zeros((1, tc), F32)
        sums = lax.fori_loop(0, n_chunk, chunk, (zero,) * (1 + FFN_KERNEL))
        db_ref[...] = sums[0]
        for k in range(FFN_KERNEL):
            dw_ref[k:k + 1, :] = sums[1 + k]

        def back(i, _):
            r0 = pl.multiple_of(i * rows, rows)
            window = dgc_s[pl.ds(r0, rows + FFN_HALO), :]
            dg = window[:rows, :] * w_ref[FFN_KERNEL - 1:FFN_KERNEL, :]
            for k in range(FFN_KERNEL - 1):
                sh = FFN_KERNEL - 1 - k
                dg = dg + pltpu.roll(window, rows + FFN_HALO - sh, 0)[:rows, :] * w_ref[k:k + 1, :]
            dg_ref[pl.ds(r0, rows), :] = dg.astype(BF16)
            return 0

        lax.fori_loop(0, n_chunk, back, 0)

    blk = lambda off: pl.BlockSpec((s, tc), lambda j: (0, off + j))
    return pl.pallas_call(
        body, name=name, grid=(nc,),
        in_specs=[blk(0), blk(nc), blk(0), pl.BlockSpec((FFN_KERNEL, tc), lambda j: (0, j)),
                  pl.BlockSpec((1, tc), lambda j: (0, j))],
        out_specs=[pl.BlockSpec((2, s, tc), lambda j: (0, 0, j)), pl.BlockSpec((FFN_KERNEL, tc), lambda j: (0, j)),
                   pl.BlockSpec((1, tc), lambda j: (0, j))],
        out_shape=[jax.ShapeDtypeStruct((2, s, f), BF16),
                   jax.ShapeDtypeStruct((FFN_KERNEL, f), F32), jax.ShapeDtypeStruct((1, f), F32)],
        scratch_shapes=[pltpu.VMEM((s + FFN_HALO, tc), F32)],
        compiler_params=_params(1),
    )(up, up, dact, w, b)


def _loss_head(y, target, name):
    m, n = y.shape
    tm = _pick(m, (256, 128))

    def body(y_ref, t_ref, l_ref, d_ref, db_ref):
        e = y_ref[...] - t_ref[...]
        part = 0.5 * jnp.sum(jnp.sum(e * e, axis=-1, keepdims=True) / n, axis=0, keepdims=True)

        @pl.when(pl.program_id(0) == 0)
        def _():
            l_ref[...] = jnp.zeros_like(l_ref)

        l_ref[...] += part
        d = e / n
        d_ref[...] = d
        db_ref[...] = d.astype(BF16)

    return pl.pallas_call(
        body, name=name, grid=(m // tm,),
        in_specs=[pl.BlockSpec((tm, n), lambda i: (i, 0))] * 2,
        out_specs=[pl.BlockSpec((8, LANES), lambda i: (0, 0)), pl.BlockSpec((tm, n), lambda i: (i, 0)),
                   pl.BlockSpec((tm, n), lambda i: (i, 0))],
        out_shape=[jax.ShapeDtypeStruct((8, LANES), F32), jax.ShapeDtypeStruct((m, n), F32),
                   jax.ShapeDtypeStruct((m, n), BF16)],
        compiler_params=_params(1),
    )(y, target)


def _adamw_math(w, g, m, v):
    m = ADAM_B1 * m + (1.0 - ADAM_B1) * g
    v = ADAM_B2 * v + (1.0 - ADAM_B2) * (g * g)
    m_hat = m / (1.0 - ADAM_B1 ** ADAM_STEP)
    v_hat = v / (1.0 - ADAM_B2 ** ADAM_STEP)
    delta = -ADAM_LR * (m_hat / (jnp.sqrt(v_hat) + ADAM_EPS) + ADAM_WD * w)
    return delta, m, v


def _sum_adamw(parts, w, m, v, sends, name):
    depth, r, c = w.shape
    tr = max(t for t in range(16, min(r, 192) + 1, 16) if r % t == 0)
    steps = r // tr
    n_send = len(sends)

    def body(*refs):
        p_refs, refs = refs[:depth], refs[depth:]
        (w_ref, m_ref, v_ref), send_refs, refs = refs[:3], refs[3:3 + n_send], refs[3 + n_send:]
        (g_out, d_out, m_out, v_out), got_refs, sems = refs[:4], refs[4:4 + n_send], refs[4 + n_send:]
        is_first = (pl.program_id(0) == 0) & (pl.program_id(1) == 0)
        is_last = (pl.program_id(0) == depth - 1) & (pl.program_id(1) == steps - 1)
        if n_send:
            start, relay, finish = _two_level_gather(send_refs, got_refs, sems)
            pl.when(is_first)(start)
            pl.when(is_last)(relay)
        for layer in range(depth):
            @pl.when(pl.program_id(0) == layer)
            def _():
                g = p_refs[layer][0].astype(F32)
                for src in range(1, N_DEV):
                    g = g + p_refs[layer][src].astype(F32)
                d, mn, vn = _adamw_math(w_ref[0], g, m_ref[0], v_ref[0])
                g_out[0] = g
                d_out[0] = d
                m_out[0] = mn
                v_out[0] = vn

        if n_send:
            pl.when(is_last)(finish)

    def part_spec(layer):
        return pl.BlockSpec((N_DEV, tr, c), lambda l, i: (0, jnp.clip((l - layer) * steps + i, 0, steps - 1), 0))

    blk = pl.BlockSpec((1, tr, c), lambda l, i: (l, i, 0))
    out = pl.pallas_call(
        body, name=name, grid=(depth, steps),
        in_specs=[part_spec(layer) for layer in range(depth)] + [blk, blk, blk] + [ANY_SPEC] * n_send,
        out_specs=[blk] * 4 + [ANY_SPEC] * n_send,
        out_shape=[jax.ShapeDtypeStruct(w.shape, F32)] * 4 + _gathered_shapes(sends),
        scratch_shapes=_exchange_sems(n_send) if n_send else [],
        compiler_params=_params(2),
    )(*parts, w, m, v, *sends)
    return out[:4], out[4:]


VMEM_SPEC = pl.BlockSpec(memory_space=pltpu.VMEM)


def _sum_small(parts, name):
    n = len(parts)

    def body(*refs):
        for p_ref, o_ref in zip(refs[:n], refs[n:]):
            g = p_ref[0]
            for src in range(1, N_DEV):
                g = g + p_ref[src]
            o_ref[...] = g

    return pl.pallas_call(
        body, name=name, in_specs=[VMEM_SPEC] * n, out_specs=[VMEM_SPEC] * n,
        out_shape=[jax.ShapeDtypeStruct(p.shape[1:], F32) for p in parts],
        compiler_params=_params(),
    )(*parts)


def _adamw_small(ws, gs, ms, vs, name):
    n = len(ws)

    def body(*refs):
        ins, outs = refs[:4 * n], refs[4 * n:]
        for i in range(n):
            d, mn, vn = _adamw_math(ins[i][...], ins[n + i][...], ins[2 * n + i][...], ins[3 * n + i][...])
            outs[i][...] = d
            outs[n + i][...] = mn
            outs[2 * n + i][...] = vn

    out = pl.pallas_call(
        body, name=name, in_specs=[VMEM_SPEC] * (4 * n), out_specs=[VMEM_SPEC] * (3 * n),
        out_shape=[jax.ShapeDtypeStruct(w.shape, F32) for w in ws] * 3,
        compiler_params=_params(),
    )(*ws, *gs, *ms, *vs)
    return out[:n], out[n:2 * n], out[2 * n:]


def kernel(x, norm1_g, w_in, q_norm_g, k_norm_g, conv_dw_w, conv_dw_b, conv_ln_g, conv_ln_b, w_out, norm2_g, w_up, ffn_dw_w, ffn_dw_b, w_down, loss_target, m_norm1_g, m_w_in, m_q_norm_g, m_k_norm_g, m_conv_dw_w, m_conv_dw_b, m_conv_ln_g, m_conv_ln_b, m_w_out, m_norm2_g, m_w_up, m_ffn_dw_w, m_ffn_dw_b, m_w_down, v_norm1_g, v_w_in, v_q_norm_g, v_k_norm_g, v_conv_dw_w, v_conv_dw_b, v_conv_ln_g, v_conv_ln_b, v_w_out, v_norm2_g, v_w_up, v_ffn_dw_w, v_ffn_dw_b, v_w_down):
    depth = w_in.shape[0]
    d_ff = w_down.shape[1] * N_DEV
    conv_w = conv_dw_b.shape[1]
    cw_shard = conv_dw_w.shape[2]
    fw_shard = ffn_dw_w.shape[2]
    me = 4 * lax.axis_index("x") + 2 * lax.axis_index("y") + lax.axis_index("c")

    transposed = lambda a: a.transpose(0, 2, 1)
    b_in, b_out, b_up, b_down = (transposed(w_in).astype(BF16), w_out.astype(BF16), transposed(w_up).astype(BF16),
                                 w_down.astype(BF16))
    rows_major = lambda g: g.reshape(N_DEV * g.shape[1], g.shape[2])
    g_in0, g_cw, g_fw = _gather([b_in[0], conv_dw_w, ffn_dw_w], name="gather_first")
    wf_in, wf_out, wf_up, wf_down = [rows_major(g_in0)] + [None] * (depth - 1), [None] * depth, [None] * depth, [None] * depth
    cwf = g_cw.transpose(1, 2, 0, 3).reshape(depth, CONV_KERNEL, conv_w)
    fwf = g_fw.transpose(1, 2, 0, 3).reshape(depth, FFN_KERNEL, d_ff)

    row = lambda a, l: a[l].reshape(1, -1)
    both_heads = lambda a, l: jnp.tile(row(a, l), (1, HEADS_PER_STEP))

    xs = x[0]
    saved = []
    for l in range(depth):
        proj, h1, _ = _mm_rms(xs, row(norm1_g, l), wf_in[l], F32, [], name="fwd_in")
        sends = [b_out[l], b_up[l]] + ([b_in[l + 1]] if l + 1 < depth else [])
        attn, tsum, got = _attn_fwd(proj, both_heads(q_norm_g, l), both_heads(k_norm_g, l), sends, name="fwd_attn")
        wf_out[l], wf_up[l] = rows_major(got[0]), rows_major(got[1])
        if l + 1 < depth:
            wf_in[l + 1] = rows_major(got[2])
        conv, conv_y = _conv_fwd(proj, cwf[l], row(conv_dw_b, l), row(conv_ln_g, l), row(conv_ln_b, l),
                                 name="fwd_conv")
        cat = jnp.concatenate([attn, conv], axis=1)
        x_mid = _mm_res(cat, wf_out[l], xs, name="fwd_out")
        up, h2, got = _mm_rms(x_mid, row(norm2_g, l), wf_up[l], BF16, [b_down[l]], name="fwd_up")
        wf_down[l] = rows_major(got[0])
        act = _ffn_fwd(up, fwf[l], row(ffn_dw_b, l), name="fwd_ffn")
        x_next = _mm_res(act, wf_down[l], x_mid, name="fwd_down")
        saved.append((xs, h1, proj, tsum, cat, x_mid, h2, up, act, conv_y))
        xs = x_next

    loss_tile, dx, dxb = _loss_head(xs, loss_target[0], name="loss_head")
    loss = lax.psum(loss_tile[0, 0], ("x", "y", "c"))

    r_in, r_out, r_up, r_down = [None] * depth, [None] * depth, [None] * depth, [None] * depth
    row_blocks = lambda g: g.reshape(N_DEV, g.shape[0] // N_DEV, g.shape[1])
    small = {k: [None] * depth for k in ("norm1_g", "q_norm_g", "k_norm_g", "conv_dw_w", "conv_dw_b", "conv_ln_g",
                                         "conv_ln_b", "norm2_g", "ffn_dw_w", "ffn_dw_b")}
    gw_in = None
    for l in reversed(range(depth)):
        xs, h1, proj, tsum, cat, x_mid, h2, up, act, conv_y = saved[l]
        dact = _mm_nt(dxb, wf_down[l], BF16, name="bwd_dact")
        gw_down = _mm_tn(act, dxb, name="bwd_gw_down")
        dup, small["ffn_dw_w"][l], small["ffn_dw_b"][l] = _ffn_bwd(up, dact, fwf[l], row(ffn_dw_b, l), name="bwd_ffn")
        gw_up = _mm_tn(dup, h2, name="bwd_gw_up")
        dx, dxb, small["norm2_g"][l], _ = _mm_rmsbwd(dup, wf_up[l], x_mid, row(norm2_g, l), dx, [], name="bwd_up")
        dcat = _mm_nt(dxb, wf_out[l], F32, name="bwd_dcat")
        gw_out = _mm_tn(cat, dxb, name="bwd_gw_out")
        (dglu, small["conv_dw_w"][l], small["conv_dw_b"][l], small["conv_ln_g"][l], small["conv_ln_b"][l]), got = (
            _conv_bwd(proj, conv_y, dcat, cwf[l], row(conv_ln_g, l), row(conv_ln_b, l), [row_blocks(gw_out)],
                      name="bwd_conv"))
        r_out[l] = got[0]
        sends = [row_blocks(gw_down), row_blocks(gw_up)] + ([row_blocks(gw_in)] if l + 1 < depth else [])
        (dq, dk, dv, dqg, dkg), got = _attn_bwd(proj, dcat, tsum, both_heads(q_norm_g, l), both_heads(k_norm_g, l),
                                                sends, name="bwd_attn")
        r_down[l], r_up[l] = got[:2]
        if l + 1 < depth:
            r_in[l + 1] = got[2]
        small["q_norm_g"][l] = dqg[0:1, :HEAD_DIM] + dqg[0:1, HEAD_DIM:]
        small["k_norm_g"][l] = dkg[0:1, :HEAD_DIM] + dkg[0:1, HEAD_DIM:]
        dproj = jnp.concatenate([dq, dk, dv, dglu], axis=1)
        gw_in = _mm_tn(dproj, h1, name="bwd_gw_in")
        sends = [row_blocks(gw_in)] if l == 0 else []
        dx, dxb, small["norm1_g"][l], got = _mm_rmsbwd(dproj, wf_in[l], xs, row(norm1_g, l), dx, sends, name="bwd_in")
        if l == 0:
            r_in[0] = got[0]
    grad_x = dx[None]

    names = ["norm1_g", "q_norm_g", "k_norm_g", "conv_dw_w", "conv_dw_b", "conv_ln_g", "conv_ln_b", "norm2_g",
             "ffn_dw_w", "ffn_dw_b"]
    full_shapes = {"norm1_g": norm1_g.shape, "q_norm_g": q_norm_g.shape, "k_norm_g": k_norm_g.shape,
                   "conv_dw_w": (depth, CONV_KERNEL, conv_w), "conv_dw_b": conv_dw_b.shape,
                   "conv_ln_g": conv_ln_g.shape, "conv_ln_b": conv_ln_b.shape, "norm2_g": norm2_g.shape,
                   "ffn_dw_w": (depth, FFN_KERNEL, d_ff), "ffn_dw_b": ffn_dw_b.shape}
    partial = [jnp.stack(small[k]).reshape(full_shapes[k]) for k in names]
    big = {}
    big["w_out"], all_partials = _sum_adamw(r_out, w_out, m_w_out, v_w_out, partial, name="adamw_out")
    big["w_up"], _ = _sum_adamw(r_up, transposed(w_up), transposed(m_w_up), transposed(v_w_up), [], name="adamw_up")
    big["w_down"], _ = _sum_adamw(r_down, w_down, m_w_down, v_w_down, [], name="adamw_down")
    big["w_in"], _ = _sum_adamw(r_in, transposed(w_in), transposed(m_w_in), transposed(v_w_in), [], name="adamw_in")
    for k in ("w_in", "w_up"):
        big[k] = [transposed(a) for a in big[k]]

    grads = dict(zip(names, _sum_small(all_partials, name="sum_small_grads")))
    grads["conv_dw_w"] = lax.dynamic_slice_in_dim(grads["conv_dw_w"], me * cw_shard, cw_shard, axis=2)
    grads["ffn_dw_w"] = lax.dynamic_slice_in_dim(grads["ffn_dw_w"], me * fw_shard, fw_shard, axis=2)
    weights = dict(norm1_g=norm1_g, q_norm_g=q_norm_g, k_norm_g=k_norm_g, conv_dw_w=conv_dw_w, conv_dw_b=conv_dw_b,
                   conv_ln_g=conv_ln_g, conv_ln_b=conv_ln_b, norm2_g=norm2_g, ffn_dw_w=ffn_dw_w, ffn_dw_b=ffn_dw_b)
    m_in = dict(norm1_g=m_norm1_g, q_norm_g=m_q_norm_g, k_norm_g=m_k_norm_g, conv_dw_w=m_conv_dw_w,
                conv_dw_b=m_conv_dw_b, conv_ln_g=m_conv_ln_g, conv_ln_b=m_conv_ln_b, norm2_g=m_norm2_g,
                ffn_dw_w=m_ffn_dw_w, ffn_dw_b=m_ffn_dw_b)
    v_in = dict(norm1_g=v_norm1_g, q_norm_g=v_q_norm_g, k_norm_g=v_k_norm_g, conv_dw_w=v_conv_dw_w,
                conv_dw_b=v_conv_dw_b, conv_ln_g=v_conv_ln_g, conv_ln_b=v_conv_ln_b, norm2_g=v_norm2_g,
                ffn_dw_w=v_ffn_dw_w, ffn_dw_b=v_ffn_dw_b)
    d_s, m_s, v_s = _adamw_small([weights[k] for k in names], [grads[k] for k in names], [m_in[k] for k in names],
                                 [v_in[k] for k in names], name="adamw_small")
    delta, new_m, new_v = dict(zip(names, d_s)), dict(zip(names, m_s)), dict(zip(names, v_s))
    for k, (g, d, mn, vn) in big.items():
        grads[k], delta[k], new_m[k], new_v[k] = g, d, mn, vn

    order = ["norm1_g", "w_in", "q_norm_g", "k_norm_g", "conv_dw_w", "conv_dw_b", "conv_ln_g", "conv_ln_b", "w_out",
             "norm2_g", "w_up", "ffn_dw_w", "ffn_dw_b", "w_down"]
    return (loss, grad_x, *[grads[k] for k in order], *[delta[k] for k in order], *[new_m[k] for k in order],
            *[new_v[k] for k in order])
```

```python
import jax
import jax.numpy as jnp
from jax import lax
from jax.experimental import pallas as pl
from jax.experimental.pallas import tpu as pltpu

F32 = jnp.float32
BF16 = jnp.bfloat16

N_DEV = 8
HEADS = 8
HEAD_DIM = 64
ATTN_WIDTH = HEADS * HEAD_DIM
CONV_KERNEL = 31
FFN_KERNEL = 3
EPS = 1e-6
BLK = 128
KEY_GROUP = 4
LANES = 128
NORM_ROWS = 128
CONV_HALO = 32
FFN_HALO = 16

ADAM_LR = 0.001
ADAM_B1 = 0.9
ADAM_B2 = 0.999
ADAM_EPS = 1e-08
ADAM_WD = 0.01
ADAM_STEP = 10

VMEM_LIMIT = 56 * 1024 * 1024


def _params(n_axes=0):
    kw = dict(vmem_limit_bytes=VMEM_LIMIT)
    if n_axes:
        kw["dimension_semantics"] = ("arbitrary",) * n_axes
    return pltpu.CompilerParams(**kw)


def _dot(a, b):
    return jnp.dot(a, b, preferred_element_type=F32)


def _dot_nt(a, b):
    return lax.dot_general(a, b, (((1,), (1,)), ((), ())), preferred_element_type=F32)


def _dot_tn(a, b):
    return lax.dot_general(a, b, (((0,), (0,)), ((), ())), preferred_element_type=F32)


def _sigmoid(x):
    return 1.0 / (1.0 + jnp.exp(-x))


def _split_bf16(x):
    hi = x.astype(BF16)
    lo = (x - hi.astype(F32)).astype(BF16)
    return hi, lo


def _pick(n, options):
    for t in options:
        if n % t == 0:
            return t
    return n


def _tile(n, cap):
    best = None
    for t in range(LANES, min(n, cap) + 1, LANES):
        if n % t == 0:
            best = t
    return best or n


def _mm_rms(x, g, wt, out_dtype, sends, name):
    m, k = x.shape
    n = wt.shape[0]
    tm = _tile(m, 2048)
    tn = _tile(n, 512)
    n_send = len(sends)
    grid = (m // tm, n // tn)

    def body(x_ref, g_ref, w_ref, *rest):
        send_refs, (o_ref, h_ref), rest = rest[:n_send], rest[n_send:n_send + 2], rest[n_send + 2:]
        got_refs, h_s, sems = rest[:n_send], rest[n_send], rest[n_send + 1:]
        if n_send:
            start, relay, finish = _two_level_gather(send_refs, got_refs, sems)
            is_first = (pl.program_id(0) == 0) & (pl.program_id(1) == 0)
            is_last = (pl.program_id(0) == grid[0] - 1) & (pl.program_id(1) == grid[1] - 1)
            pl.when(is_first)(start)
            pl.when(is_last)(relay)

        @pl.when(pl.program_id(1) == 0)
        def _():
            def chunk(c, _):
                rows = pl.ds(pl.multiple_of(c * NORM_ROWS, NORM_ROWS), NORM_ROWS)
                xv = x_ref[rows, :]
                r = lax.rsqrt(jnp.mean(xv * xv, axis=-1, keepdims=True) + EPS)
                hv = (xv * r * g_ref[...]).astype(BF16)
                h_s[rows, :] = hv
                h_ref[rows, :] = hv
                return 0

            lax.fori_loop(0, tm // NORM_ROWS, chunk, 0)

        o_ref[...] = _dot_nt(h_s[...], w_ref[...]).astype(out_dtype)
        if n_send:
            pl.when(is_last)(finish)

    out = pl.pallas_call(
        body, name=name, grid=grid,
        in_specs=[pl.BlockSpec((tm, k), lambda i, j: (i, 0)),
                  pl.BlockSpec((1, k), lambda i, j: (0, 0)),
                  pl.BlockSpec((tn, k), lambda i, j: (j, 0))] + [ANY_SPEC] * n_send,
        out_specs=[pl.BlockSpec((tm, tn), lambda i, j: (i, j)),
                   pl.BlockSpec((tm, k), lambda i, j: (i, 0))] + [ANY_SPEC] * n_send,
        out_shape=[jax.ShapeDtypeStruct((m, n), out_dtype), jax.ShapeDtypeStruct((m, k), BF16)]
        + _gathered_shapes(sends),
        scratch_shapes=[pltpu.VMEM((tm, k), BF16)] + (_exchange_sems(n_send) if n_send else []),
        compiler_params=_params(2),
    )(x, g, wt, *sends)
    return out[0], out[1], out[2:]


def _mm_res(a, w, res, name):
    m, k = a.shape
    n = w.shape[1]
    tm = _tile(m, 1024)
    tn = _tile(n, 512)

    def body(a_ref, w_ref, r_ref, o_ref):
        o_ref[...] = r_ref[...] + _dot(a_ref[...], w_ref[...])

    return pl.pallas_call(
        body, name=name, grid=(m // tm, n // tn),
        in_specs=[pl.BlockSpec((tm, k), lambda i, j: (i, 0)),
                  pl.BlockSpec((k, tn), lambda i, j: (0, j)),
                  pl.BlockSpec((tm, tn), lambda i, j: (i, j))],
        out_specs=pl.BlockSpec((tm, tn), lambda i, j: (i, j)),
        out_shape=jax.ShapeDtypeStruct((m, n), F32),
        compiler_params=_params(2),
    )(a, w, res)


def _mm_nt(a, w, out_dtype, name):
    m, k = a.shape
    n = w.shape[0]
    tm = _tile(m, 1024)
    tn = _tile(n, 1408)

    def body(a_ref, w_ref, o_ref):
        o_ref[...] = _dot_nt(a_ref[...], w_ref[...]).astype(out_dtype)

    return pl.pallas_call(
        body, name=name, grid=(m // tm, n // tn),
        in_specs=[pl.BlockSpec((tm, k), lambda i, j: (i, 0)),
                  pl.BlockSpec((tn, k), lambda i, j: (j, 0))],
        out_specs=pl.BlockSpec((tm, tn), lambda i, j: (i, j)),
        out_shape=jax.ShapeDtypeStruct((m, n), out_dtype),
        compiler_params=_params(2),
    )(a, w)


def _column_tiles(a, cap):
    if a.ndim == 2:
        s, c = a.shape
        tc = _tile(c, cap)
        return s, c, tc, lambda rows, index: pl.BlockSpec((rows, tc), lambda *g: (index(*g)[0], index(*g)[1]))
    slabs, s, width = a.shape
    tc = _tile(width, cap)
    per = width // tc
    return s, slabs * width, tc, lambda rows, index: pl.BlockSpec(
        (None, rows, tc), lambda *g: (index(*g)[1] // per, index(*g)[0], index(*g)[1] % per))


def _mm_tn(a, b, name):
    s, m, tm, a_spec = _column_tiles(a, 1408)
    n = b.shape[1]
    tn = _tile(n, 1024)

    def body(a_ref, b_ref, o_ref):
        o_ref[...] = _dot_tn(a_ref[...], b_ref[...]).astype(BF16)

    return pl.pallas_call(
        body, name=name, grid=(m // tm, n // tn),
        in_specs=[a_spec(s, lambda i, j: (0, i)),
                  pl.BlockSpec((s, tn), lambda i, j: (0, j))],
        out_specs=pl.BlockSpec((tm, tn), lambda i, j: (i, j)),
        out_shape=jax.ShapeDtypeStruct((m, n), BF16),
        compiler_params=_params(2),
    )(a, b)


def _mm_rmsbwd(a, w, x, g, dres, sends, name):
    m, k, tk, a_spec = _column_tiles(a, 1408)
    n = w.shape[1]
    tm = _tile(m, 1024)
    nk = k // tk
    n_send = len(sends)

    def body(a_ref, w_ref, x_ref, g_ref, r_ref, *rest):
        send_refs, (dx_ref, dxb_ref, dg_ref), rest = rest[:n_send], rest[n_send:n_send + 3], rest[n_send + 3:]
        got_refs, acc, sems = rest[:n_send], rest[n_send], rest[n_send + 1:]
        i, kk = pl.program_id(0), pl.program_id(1)
        if n_send:
            copies = _scatter_copies(send_refs, got_refs, sems)

            @pl.when((i == 0) & (kk == 0))
            def _():
                for cp in copies:
                    cp.start()

        part = _dot(a_ref[...], w_ref[...])

        @pl.when(kk == 0)
        def _():
            acc[...] = part

        @pl.when(kk > 0)
        def _():
            acc[...] += part

        @pl.when(kk == nk - 1)
        def _():
            def chunk(c, dgp):
                rows = pl.ds(pl.multiple_of(c * NORM_ROWS, NORM_ROWS), NORM_ROWS)
                dh = acc[rows, :]
                xv = x_ref[rows, :]
                r = lax.rsqrt(jnp.mean(xv * xv, axis=-1, keepdims=True) + EPS)
                xh = xv * r
                dxh = dh * g_ref[...]
                dx = r_ref[rows, :] + r * (dxh - xh * jnp.mean(dxh * xh, axis=-1, keepdims=True))
                dx_ref[rows, :] = dx
                dxb_ref[rows, :] = dx.astype(BF16)
                return dgp + jnp.sum(dh * xh, axis=0, keepdims=True)

            dgp = lax.fori_loop(0, tm // NORM_ROWS, chunk, jnp.zeros((1, n), F32))

            @pl.when(i == 0)
            def _():
                dg_ref[...] = dgp

            @pl.when(i > 0)
            def _():
                dg_ref[...] += dgp

        if n_send:
            @pl.when((i == m // tm - 1) & (kk == nk - 1))
            def _():
                for cp in copies:
                    cp.wait()

    out = pl.pallas_call(
        body, name=name, grid=(m // tm, nk),
        in_specs=[a_spec(tm, lambda i, kk: (i, kk)),
                  pl.BlockSpec((tk, n), lambda i, kk: (kk, 0)),
                  pl.BlockSpec((tm, n), lambda i, kk: (i, 0)),
                  pl.BlockSpec((1, n), lambda i, kk: (0, 0)),
                  pl.BlockSpec((tm, n), lambda i, kk: (i, 0))] + [ANY_SPEC] * n_send,
        out_specs=[pl.BlockSpec((tm, n), lambda i, kk: (i, 0)),
                   pl.BlockSpec((tm, n), lambda i, kk: (i, 0)),
                   pl.BlockSpec((1, n), lambda i, kk: (0, 0))] + [ANY_SPEC] * n_send,
        out_shape=[jax.ShapeDtypeStruct((m, n), F32), jax.ShapeDtypeStruct((m, n), BF16),
                   jax.ShapeDtypeStruct((1, n), F32)] + _scattered_shapes(sends),
        scratch_shapes=[pltpu.VMEM((tm, n), F32)] + (_exchange_sems(n_send) if n_send else []),
        compiler_params=_params(2),
    )(a, w, x, g, dres, *sends)
    return out[0], out[1], out[2], out[3:]


ANY_SPEC = pl.BlockSpec(memory_space=pl.ANY)
SEMS_PER_OPERAND = N_DEV - 1


def _exchange_sems(n):
    return [pltpu.SemaphoreType.DMA((n, SEMS_PER_OPERAND)), pltpu.SemaphoreType.DMA((n, SEMS_PER_OPERAND)),
            pltpu.SemaphoreType.DMA((n,))]


def _gathered_shapes(parts):
    return [jax.ShapeDtypeStruct((N_DEV,) + a.shape, a.dtype) for a in parts]


def _scattered_shapes(parts):
    return [jax.ShapeDtypeStruct(a.shape, a.dtype) for a in parts]


def _flat(pos):
    return 4 * pos[0] + 2 * pos[1] + pos[2]


def _remote(src, dst, sems, i, k, to):
    send_sems, recv_sems, _ = sems
    return pltpu.make_async_remote_copy(src_ref=src, dst_ref=dst, send_sem=send_sems.at[i, k],
                                        recv_sem=recv_sems.at[i, k], device_id=to,
                                        device_id_type=pl.DeviceIdType.MESH)


def _scatter_copies(ins, outs, sems):
    x, y, c = lax.axis_index("x"), lax.axis_index("y"), lax.axis_index("c")
    me = _flat((x, y, c))
    copies = [pltpu.make_async_copy(ins[i].at[me], outs[i].at[me], sems[2].at[i]) for i in range(len(ins))]
    for d in range(1, N_DEV):
        peer = (1 - x if d & 4 else x, 1 - y if d & 2 else y, 1 - c if d & 1 else c)
        for i in range(len(ins)):
            copies.append(_remote(ins[i].at[_flat(peer)], outs[i].at[me], sems, i, d - 1, peer))
    return copies


def _two_level_gather(ins, outs, sems):
    x, y, c = lax.axis_index("x"), lax.axis_index("y"), lax.axis_index("c")
    me, sibling = (x, y, c), (x, y, 1 - c)
    chips = [(1 - x, y), (x, 1 - y), (1 - x, 1 - y)]
    n = len(ins)

    def block(i, pos):
        return outs[i].at[_flat(pos)]

    local = [pltpu.make_async_copy(ins[i], block(i, me), sems[2].at[i]) for i in range(n)]
    own = [_remote(ins[i], block(i, me), sems, i, 0, sibling) for i in range(n)]
    own += [_remote(ins[i], block(i, me), sems, i, 1 + j, (*chip, c)) for i in range(n) for j, chip in enumerate(chips)]
    passed = [[_remote(block(i, (*chip, c)), block(i, (*chip, c)), sems, i, 4 + j, sibling) for i in range(n)]
              for j, chip in enumerate(chips)]

    def first():
        for cp in local + own:
            cp.start()

    def relay():
        for j, chip in enumerate(chips):
            for i in range(n):
                _remote(ins[i], block(i, (*chip, c)), sems, i, 1 + j, me).wait_recv()
                passed[j][i].start()

    def finish():
        for i in range(n):
            _remote(ins[i], block(i, sibling), sems, i, 0, me).wait_recv()
            for j, chip in enumerate(chips):
                _remote(ins[i], block(i, (*chip, 1 - c)), sems, i, 4 + j, me).wait_recv()
        for cp in own + [cp for row in passed for cp in row]:
            cp.wait_send()
        for cp in local:
            cp.wait()

    return first, relay, finish


def _gather(parts, name):
    n = len(parts)

    def body(*refs):
        start, relay, finish = _two_level_gather(refs[:n], refs[n:2 * n], refs[2 * n:])
        start()
        relay()
        finish()

    return pl.pallas_call(
        body, name=name, in_specs=[ANY_SPEC] * n, out_specs=[ANY_SPEC] * n,
        out_shape=_gathered_shapes(parts), scratch_shapes=_exchange_sems(n),
    )(*parts)


def _tri(kind):
    j = lax.broadcasted_iota(jnp.int32, (BLK, BLK), 0)
    s = lax.broadcasted_iota(jnp.int32, (BLK, BLK), 1)
    m = {"after": j > s, "upto": j <= s, "before": j < s}[kind]
    return jnp.concatenate([jnp.where(m, 1.0, 0.0), jnp.ones((BLK, BLK), F32)], axis=1).astype(BF16)


def _scan_rows(v, tri):
    r = _dot(v.astype(BF16), tri)
    return r[:, :BLK], r[:, BLK:]


HEADS_PER_STEP = LANES // HEAD_DIM


def _first_head_lanes():
    return lax.broadcasted_iota(jnp.int32, (1, LANES), 1) < HEAD_DIM


def _pair_mean(v, first):
    m0 = jnp.sum(jnp.where(first, v, 0.0), axis=-1, keepdims=True)
    m1 = jnp.sum(jnp.where(first, 0.0, v), axis=-1, keepdims=True)
    return jnp.where(first, m0, m1) * (1.0 / HEAD_DIM)


def _pair_norm(v, g2, first):
    return v * lax.rsqrt(_pair_mean(v * v, first) + EPS) * g2


def _pair_norm_bwd(raw, g2, dn, first):
    r = lax.rsqrt(_pair_mean(raw * raw, first) + EPS)
    xh = raw * r
    dxh = dn * g2
    return r * (dxh - xh * _pair_mean(dxh * xh, first)), jnp.sum(dn * xh, axis=0, keepdims=True)


def _block_diag(v, first):
    zero = jnp.zeros_like(v)
    return jnp.concatenate([jnp.where(first, v, zero), jnp.where(first, zero, v)], axis=0)


def _attn_prep(q_ref, k_ref, v_ref, qg_ref, kg_ref, qc_s, kc_s, vd_s, kd_s, n_blk):
    scale = HEAD_DIM ** -0.5
    first = _first_head_lanes()

    def prep(i, _):
        rows = pl.ds(pl.multiple_of(i * BLK, BLK), BLK)
        both = pl.ds(pl.multiple_of(i * 2 * BLK, 2 * BLK), 2 * BLK)
        qh, ql = _split_bf16(_pair_norm(q_ref[rows, :], qg_ref[...], first) * scale)
        kh, kl = _split_bf16(_pair_norm(k_ref[rows, :], kg_ref[...], first))
        for h in range(HEADS_PER_STEP):
            sl = slice(h * HEAD_DIM, (h + 1) * HEAD_DIM)
            qc_s[h, rows, :] = jnp.concatenate([qh[:, sl], ql[:, sl], qh[:, sl], ql[:, sl]], axis=1)
            kc_s[h, rows, :] = jnp.concatenate([kh[:, sl], kh[:, sl], kl[:, sl], kl[:, sl]], axis=1)
        vd_s[both, :] = _block_diag(v_ref[rows, :].astype(BF16), first)
        if kd_s is not None:
            kd_s[both, :] = _block_diag(kh, first)
        return 0

    lax.fori_loop(0, n_blk, prep, 0)


def _pair_scores(qc, kc_ref, grp, n_tiles=KEY_GROUP):
    zs = []
    for j in range(0, n_tiles, 2):
        two = pl.ds(pl.multiple_of((grp * KEY_GROUP + j) * BLK, 2 * BLK), 2 * BLK)
        z = _dot_nt(qc, kc_ref[two, :])
        zs += [z[:, :BLK], z[:, BLK:]]
    return zs[:n_tiles]


def _col_minus_row():
    row = lax.broadcasted_iota(jnp.int32, (BLK, BLK), 0)
    col = lax.broadcasted_iota(jnp.int32, (BLK, BLK), 1)
    return col - row


def _softplus(z):
    return jnp.maximum(z, 0.0) + jnp.log(1.0 + jnp.exp(-jnp.abs(z)))


def _attn_fwd(proj, qg, kg, sends, name):
    s = proj.shape[0]
    n_blk = s // BLK
    pairs = ATTN_WIDTH // LANES
    n_send = len(sends)

    def body(q_ref, k_ref, v_ref, qg_ref, kg_ref, *rest):
        send_refs, (o_ref, t_ref), rest = rest[:n_send], rest[n_send:n_send + 2], rest[n_send + 2:]
        got_refs, (qc_s, kc_s, vd_s), sems = rest[:n_send], rest[n_send:n_send + 3], rest[n_send + 3:]
        start, relay, finish = _two_level_gather(send_refs, got_refs, sems)
        step = pl.program_id(0)
        pl.when(step == 0)(start)
        pl.when(step == pairs - 1)(relay)
        tri = _tri("after")
        diff = _col_minus_row()
        first = _first_head_lanes()
        heads = range(HEADS_PER_STEP)
        _attn_prep(q_ref, k_ref, v_ref, qg_ref, kg_ref, qc_s, kc_s, vd_s, None, n_blk)

        below = diff < 0

        def group(qc, grp, carry, acc, n_tiles, diagonal):
            blocks = [grp * KEY_GROUP + j for j in reversed(range(n_tiles))]
            zs = [_pair_scores(qc[h], kc_s.at[h], grp, n_tiles)[::-1] for h in heads]
            parts = [[None] * n_tiles for _ in heads]
            for h in heads:
                for j, z in enumerate(zs[h]):
                    sp = _softplus(z)
                    lom = -sp
                    if diagonal and j == 0:
                        lom = jnp.where(below, lom, 0.0)
                    tail, tot = _scan_rows(lom, tri)
                    parts[h][j] = (z - sp + tail, tot)
            carry = list(carry)
            for j, kb in enumerate(blocks):
                ws = []
                for h in heads:
                    lw, tot = parts[h][j]
                    w = jnp.exp(lw + carry[h])
                    if diagonal and j == 0:
                        w = jnp.where(below, w, 0.0)
                    ws.append(w.astype(BF16))
                    carry[h] = carry[h] + tot
                acc = acc + _dot(jnp.concatenate(ws, axis=1),
                                 vd_s[pl.ds(pl.multiple_of(kb * 2 * BLK, 2 * BLK), 2 * BLK), :])
            return tuple(carry), acc

        def q_blocks(top, _):
            for r in range(KEY_GROUP):
                rows = pl.ds(pl.multiple_of((top * KEY_GROUP + r) * BLK, BLK), BLK)
                qc = [qc_s[h, rows, :] for h in heads]
                zero = jnp.zeros((BLK, BLK), F32)
                carry, acc = group(qc, top, (zero,) * HEADS_PER_STEP, jnp.zeros((BLK, LANES), F32), r + 1, True)
                carry, acc = lax.fori_loop(
                    0, top, lambda t, c: group(qc, top - 1 - t, c[0], c[1], KEY_GROUP, False), (carry, acc))
                o_ref[rows, :] = acc.astype(BF16)
                t_ref[rows, :] = jnp.where(first, carry[0], carry[1])
            return 0

        lax.fori_loop(0, n_blk // KEY_GROUP, q_blocks, 0)
        pl.when(step == pairs - 1)(finish)

    col = lambda off: pl.BlockSpec((s, LANES), lambda p: (0, off + p))
    vec = pl.BlockSpec((1, LANES), lambda p: (0, 0))
    out = pl.pallas_call(
        body, name=name, grid=(pairs,),
        in_specs=[col(0), col(pairs), col(2 * pairs), vec, vec] + [ANY_SPEC] * n_send,
        out_specs=[pl.BlockSpec((s, LANES), lambda p: (0, p))] * 2 + [ANY_SPEC] * n_send,
        out_shape=[jax.ShapeDtypeStruct((s, ATTN_WIDTH), BF16), jax.ShapeDtypeStruct((s, ATTN_WIDTH), F32)]
        + _gathered_shapes(sends),
        scratch_shapes=[pltpu.VMEM((HEADS_PER_STEP, s, 4 * HEAD_DIM), BF16)] * 2
        + [pltpu.VMEM((HEADS_PER_STEP * s, LANES), BF16)] + _exchange_sems(n_send),
        compiler_params=_params(1),
    )(proj, proj, proj, qg, kg, *sends)
    return out[0], out[1], out[2:]


def _attn_bwd(proj, dcat, tsum, qg, kg, sends, name):
    s = proj.shape[0]
    n_blk = s // BLK
    pairs = ATTN_WIDTH // LANES
    scale = HEAD_DIM ** -0.5
    n_send = len(sends)
    n_scratch = 7

    def body(q_ref, k_ref, v_ref, do_ref, t_ref, qg_ref, kg_ref, *rest):
        send_refs, rest = rest[:n_send], rest[n_send:]
        (dq_ref, dk_ref, dv_ref, dqg_ref, dkg_ref), rest = rest[:5], rest[5:]
        got_refs, scratch, sems = rest[:n_send], rest[n_send:n_send + n_scratch], rest[n_send + n_scratch:]
        qc_s, kc_s, vd_s, kd_s, qd_s, dob_s, dkv_s = scratch
        copies = _scatter_copies(send_refs, got_refs, sems)

        @pl.when(pl.program_id(0) == 0)
        def _():
            for cp in copies:
                cp.start()

        tri_p = _tri("upto")
        tri_h = _tri("before")
        diff = _col_minus_row()

        @pl.when(pl.program_id(0) == 0)
        def _():
            dqg_ref[...] = jnp.zeros_like(dqg_ref)
            dkg_ref[...] = jnp.zeros_like(dkg_ref)

        first = _first_head_lanes()
        heads = range(HEADS_PER_STEP)
        _attn_prep(q_ref, k_ref, v_ref, qg_ref, kg_ref, qc_s, kc_s, vd_s, kd_s, n_blk)

        def prep(i, _):
            rows = pl.ds(pl.multiple_of(i * BLK, BLK), BLK)
            both = pl.ds(pl.multiple_of(i * 2 * BLK, 2 * BLK), 2 * BLK)
            dob = do_ref[rows, :].astype(BF16)
            dob_s[rows, :] = dob
            none = jnp.zeros((BLK, HEAD_DIM), BF16)
            for h in heads:
                qd_s[h, both, :] = jnp.concatenate(
                    [jnp.concatenate([qc_s[h, rows, 0:HEAD_DIM], none], axis=1),
                     jnp.concatenate([none, dob[:, h * HEAD_DIM:(h + 1) * HEAD_DIM]], axis=1)], axis=0)
                dkv_s[h, rows, :] = jnp.zeros((BLK, LANES), F32)
            return 0

        lax.fori_loop(0, n_blk, prep, 0)

        below = diff < 0

        def group(qc, qd, dob, tq, grp, pc, hc, dq, n_tiles, diagonal):
            blocks = [grp * KEY_GROUP + j for j in range(n_tiles)]
            cols_of = [pl.ds(pl.multiple_of(kb * BLK, BLK), BLK) for kb in blocks]
            both_of = [pl.ds(pl.multiple_of(kb * 2 * BLK, 2 * BLK), 2 * BLK) for kb in blocks]
            on_diagonal = [diagonal and j == n_tiles - 1 for j in range(n_tiles)]
            zs = [_pair_scores(qc[h], kc_s.at[h], grp, n_tiles) for h in heads]
            das = [_dot_nt(dob, vd_s[both, :]) for both in both_of]
            lbs = [[None] * n_tiles for _ in heads]
            scans = [[None] * n_tiles for _ in heads]
            for h in heads:
                for j, z in enumerate(zs[h]):
                    sp = _softplus(z)
                    lom = -sp
                    if on_diagonal[j]:
                        lom = jnp.where(below, lom, 0.0)
                    lbs[h][j] = z - sp
                    scans[h][j] = _scan_rows(lom, tri_p)
            pc, hc = list(pc), list(hc)
            avs = [[None] * n_tiles for _ in heads]
            gws = [[None] * n_tiles for _ in heads]
            hscans = [[None] * n_tiles for _ in heads]
            for h in heads:
                for j in range(n_tiles):
                    p_in, p_tot = scans[h][j]
                    a = jnp.exp(lbs[h][j] + (tq[h] - pc[h] - p_in))
                    if on_diagonal[j]:
                        a = jnp.where(below, a, 0.0)
                    pc[h] = pc[h] + p_tot
                    gw = das[j][:, h * BLK:(h + 1) * BLK] * a
                    avs[h][j] = a.astype(BF16)
                    gws[h][j] = gw
                    hscans[h][j] = _scan_rows(gw, tri_h)
            dzs = [[None] * n_tiles for _ in heads]
            for h in heads:
                for j in range(n_tiles):
                    h_in, g_tot = hscans[h][j]
                    gw = gws[h][j]
                    dz = gw - jnp.exp(lbs[h][j]) * (gw + hc[h] + h_in)
                    if on_diagonal[j]:
                        dz = jnp.where(below, dz, 0.0)
                    hc[h] = hc[h] + g_tot
                    dzs[h][j] = dz.astype(BF16)
            for j, both in enumerate(both_of):
                dq = dq + _dot(jnp.concatenate([dzs[h][j] for h in heads], axis=1), kd_s[both, :])
            for h in heads:
                for j, cols in enumerate(cols_of):
                    dkv_s[h, cols, :] += _dot_tn(jnp.concatenate([dzs[h][j], avs[h][j]], axis=0), qd[h])
            return tuple(pc), tuple(hc), dq

        def q_blocks(top, dqg):
            for r in range(KEY_GROUP):
                qi = top * KEY_GROUP + r
                rows = pl.ds(pl.multiple_of(qi * BLK, BLK), BLK)
                both = pl.ds(pl.multiple_of(qi * 2 * BLK, 2 * BLK), 2 * BLK)
                qc = [qc_s[h, rows, :] for h in heads]
                qd = [qd_s[h, both, :] for h in heads]
                dob = dob_s[rows, :]
                tboth = t_ref[rows, :]
                tq = [jnp.concatenate([tboth[:, h * HEAD_DIM:(h + 1) * HEAD_DIM]] * 2, axis=1) for h in heads]
                zero = (jnp.zeros((BLK, BLK), F32),) * HEADS_PER_STEP
                pc, hc, dq = lax.fori_loop(
                    0, top, lambda grp, c: group(qc, qd, dob, tq, grp, c[0], c[1], c[2], KEY_GROUP, False),
                    (zero, zero, jnp.zeros((BLK, LANES), F32)))
                _, _, dq = group(qc, qd, dob, tq, top, pc, hc, dq, r + 1, True)
                dq_raw, dg = _pair_norm_bwd(q_ref[rows, :], qg_ref[...], dq * scale, first)
                dq_ref[rows, :] = dq_raw.astype(BF16)
                dqg = dqg + dg
            return dqg

        dqg = lax.fori_loop(0, n_blk // KEY_GROUP, q_blocks, jnp.zeros((1, LANES), F32))

        def finish(i, dkg):
            rows = pl.ds(pl.multiple_of(i * BLK, BLK), BLK)
            dk = jnp.concatenate([dkv_s[h, rows, 0:HEAD_DIM] for h in heads], axis=1)
            dv = jnp.concatenate([dkv_s[h, rows, HEAD_DIM:2 * HEAD_DIM] for h in heads], axis=1)
            dk_raw, dg = _pair_norm_bwd(k_ref[rows, :], kg_ref[...], dk, first)
            dk_ref[rows, :] = dk_raw.astype(BF16)
            dv_ref[rows, :] = dv.astype(BF16)
            return dkg + dg

        dkg = lax.fori_loop(0, n_blk, finish, jnp.zeros((1, LANES), F32))
        dqg_ref[0:1, :] += dqg
        dkg_ref[0:1, :] += dkg

        @pl.when(pl.program_id(0) == pairs - 1)
        def _():
            for cp in copies:
                cp.wait()

    col = lambda off: pl.BlockSpec((s, LANES), lambda p: (0, off + p))
    vec = pl.BlockSpec((1, LANES), lambda p: (0, 0))
    small = pl.BlockSpec((8, LANES), lambda p: (0, 0))
    out = pl.pallas_call(
        body, name=name, grid=(pairs,),
        in_specs=[col(0), col(pairs), col(2 * pairs), col(0), col(0), vec, vec] + [ANY_SPEC] * n_send,
        out_specs=[col(0)] * 3 + [small] * 2 + [ANY_SPEC] * n_send,
        out_shape=[jax.ShapeDtypeStruct((s, ATTN_WIDTH), BF16)] * 3 + [jax.ShapeDtypeStruct((8, LANES), F32)] * 2
        + _scattered_shapes(sends),
        scratch_shapes=[pltpu.VMEM((HEADS_PER_STEP, s, 4 * HEAD_DIM), BF16)] * 2
        + [pltpu.VMEM((HEADS_PER_STEP * s, LANES), BF16)] * 2
        + [pltpu.VMEM((HEADS_PER_STEP, HEADS_PER_STEP * s, LANES), BF16), pltpu.VMEM((s, LANES), BF16),
           pltpu.VMEM((HEADS_PER_STEP, s, LANES), F32)] + _exchange_sems(n_send),
        compiler_params=_params(1),
    )(proj, proj, proj, dcat, tsum, qg, kg, *sends)
    return out[:5], out[5:]


CONV_ROWS = 128


def _shifted(window, shift, halo):
    if shift == 0:
        return window[halo:, :]
    return pltpu.roll(window, shift, 0)[halo:, :]


SUBLANES = 8


def _row_shifts(window, up):
    n = window.shape[0]
    return [window] + [pltpu.roll(window, n - b if up else b, 0) for b in range(1, SUBLANES)]


def _earlier(shifts, back, rows):
    a, b = divmod(back, SUBLANES)
    return shifts[b][CONV_HALO - SUBLANES * a:CONV_HALO - SUBLANES * a + rows, :]


def _later(shifts, ahead, rows):
    a, b = divmod(ahead, SUBLANES)
    return shifts[b][SUBLANES * a:SUBLANES * a + rows, :]


def _lane_blocks(width):
    return [slice(c, c + LANES) for c in range(0, width, LANES)]


def _conv_taps(shifts, w_ref, lanes, rows):
    y = None
    for k in range(CONV_KERNEL):
        term = _earlier(shifts, CONV_KERNEL - 1 - k, rows) * w_ref[k:k + 1, lanes]
        y = term if y is None else y + term
    return y


def _conv_fwd(proj, w, b, lg, lb, name):
    s = proj.shape[0]
    cw = w.shape[1]
    rows = CONV_ROWS
    blk_a = (proj.shape[1] - 2 * cw) // cw

    def body(a_ref, g_ref, w_ref, b_ref, lg_ref, lb_ref, o_ref, y_ref, u_s):
        u_s[0:CONV_HALO, :] = jnp.zeros((CONV_HALO, cw), F32)

        def glu(i, _):
            r0 = pl.multiple_of(i * rows, rows)
            u_s[pl.ds(CONV_HALO + r0, rows), :] = a_ref[pl.ds(r0, rows), :] * _sigmoid(g_ref[pl.ds(r0, rows), :])
            return 0

        lax.fori_loop(0, s // rows, glu, 0)

        def chunk(i, _):
            r0 = pl.multiple_of(i * rows, rows)
            for lanes in _lane_blocks(cw):
                shifts = _row_shifts(u_s[pl.ds(r0, CONV_HALO + rows), lanes], False)
                y_ref[pl.ds(r0, rows), lanes] = _conv_taps(shifts, w_ref, lanes, rows) + b_ref[:, lanes]
            y = y_ref[pl.ds(r0, rows), :]
            yc = y - jnp.mean(y, axis=-1, keepdims=True)
            n = yc * lax.rsqrt(jnp.mean(yc * yc, axis=-1, keepdims=True) + EPS)
            ln = n * lg_ref[...] + lb_ref[...]
            o_ref[pl.ds(r0, rows), :] = (ln * _sigmoid(ln)).astype(BF16)
            return 0

        lax.fori_loop(0, s // rows, chunk, 0)

    vec = pl.BlockSpec((1, cw), lambda i: (0, 0))
    return pl.pallas_call(
        body, name=name, grid=(1,),
        in_specs=[pl.BlockSpec((s, cw), lambda i: (0, blk_a)), pl.BlockSpec((s, cw), lambda i: (0, blk_a + 1)),
                  pl.BlockSpec((CONV_KERNEL, cw), lambda i: (0, 0)), vec, vec, vec],
        out_specs=[pl.BlockSpec((s, cw), lambda i: (0, 0))] * 2,
        out_shape=[jax.ShapeDtypeStruct((s, cw), BF16), jax.ShapeDtypeStruct((s, cw), F32)],
        scratch_shapes=[pltpu.VMEM((CONV_HALO + s, cw), F32)],
        compiler_params=_params(1),
    )(proj, proj, w, b, lg, lb)


def _conv_bwd(proj, y, dcat, w, lg, lb, sends, name):
    s = proj.shape[0]
    cw = w.shape[1]
    rows = CONV_ROWS
    blk_a = (proj.shape[1] - 2 * cw) // cw
    n_chunk = s // rows
    n_send = len(sends)

    def body(a_ref, g_ref, y_ref, dc_ref, w_ref, lg_ref, lb_ref, *rest):
        send_refs, (o_ref, dw_ref, db_ref, dlg_ref, dlb_ref), rest = rest[:n_send], rest[n_send:n_send + 5], rest[n_send + 5:]
        got_refs, (u_s, dy_s, dw_s), sems = rest[:n_send], rest[n_send:n_send + 3], rest[n_send + 3:]
        copies = _scatter_copies(send_refs, got_refs, sems)
        for cp in copies:
            cp.start()
        u_s[0:CONV_HALO, :] = jnp.zeros((CONV_HALO, cw), F32)
        dy_s[pl.ds(s, CONV_HALO), :] = jnp.zeros((CONV_HALO, cw), F32)
        dw_s[...] = jnp.zeros_like(dw_s)

        def glu(i, _):
            r0 = pl.multiple_of(i * rows, rows)
            u_s[pl.ds(CONV_HALO + r0, rows), :] = a_ref[pl.ds(r0, rows), :] * _sigmoid(g_ref[pl.ds(r0, rows), :])
            return 0

        lax.fori_loop(0, n_chunk, glu, 0)

        def chunk(i, carry):
            db, dlg, dlb = carry
            r0 = pl.multiple_of(i * rows, rows)
            y = y_ref[pl.ds(r0, rows), :]
            yc = y - jnp.mean(y, axis=-1, keepdims=True)
            r = lax.rsqrt(jnp.mean(yc * yc, axis=-1, keepdims=True) + EPS)
            n = yc * r
            ln = n * lg_ref[...] + lb_ref[...]
            sg = _sigmoid(ln)
            dln = dc_ref[pl.ds(r0, rows), :] * (sg * (1.0 + ln * (1.0 - sg)))
            dn = dln * lg_ref[...]
            dy = r * (dn - jnp.mean(dn, axis=-1, keepdims=True) - n * jnp.mean(dn * n, axis=-1, keepdims=True))
            dy_s[pl.ds(r0, rows), :] = dy
            for lanes in _lane_blocks(cw):
                shifts = _row_shifts(u_s[pl.ds(r0, CONV_HALO + rows), lanes], False)
                dy_part = dy[:, lanes]
                for k in range(CONV_KERNEL):
                    prod = _earlier(shifts, CONV_KERNEL - 1 - k, rows) * dy_part
                    dw_s[k, :, lanes] += jnp.sum(prod.reshape(rows // SUBLANES, SUBLANES, LANES), axis=0)
            return (db + jnp.sum(dy, axis=0, keepdims=True),
                    dlg + jnp.sum(dln * n, axis=0, keepdims=True),
                    dlb + jnp.sum(dln, axis=0, keepdims=True))

        zero = jnp.zeros((1, cw), F32)
        db, dlg, dlb = lax.fori_loop(0, n_chunk, chunk, (zero, zero, zero))
        db_ref[...] = db
        dlg_ref[...] = dlg
        dlb_ref[...] = dlb
        for k in range(CONV_KERNEL):
            dw_ref[k:k + 1, :] = jnp.sum(dw_s[k], axis=0, keepdims=True)

        def back(i, _):
            r0 = pl.multiple_of(i * rows, rows)
            for lanes in _lane_blocks(cw):
                shifts = _row_shifts(dy_s[pl.ds(r0, rows + CONV_HALO), lanes], True)
                du = None
                for k in range(CONV_KERNEL):
                    term = _later(shifts, CONV_KERNEL - 1 - k, rows) * w_ref[k:k + 1, lanes]
                    du = term if du is None else du + term
                av = a_ref[pl.ds(r0, rows), lanes]
                sg = _sigmoid(g_ref[pl.ds(r0, rows), lanes])
                o_ref[pl.ds(r0, rows), lanes] = (du * sg).astype(BF16)
                o_ref[pl.ds(r0, rows), slice(cw + lanes.start, cw + lanes.stop)] = (du * av * sg * (1.0 - sg)).astype(BF16)
            return 0

        lax.fori_loop(0, n_chunk, back, 0)
        for cp in copies:
            cp.wait()

    vec = pl.BlockSpec((1, cw), lambda i: (0, 0))
    wspec = pl.BlockSpec((CONV_KERNEL, cw), lambda i: (0, 0))
    out = pl.pallas_call(
        body, name=name, grid=(1,),
        in_specs=[pl.BlockSpec((s, cw), lambda i: (0, blk_a)), pl.BlockSpec((s, cw), lambda i: (0, blk_a + 1)),
                  pl.BlockSpec((s, cw), lambda i: (0, 0)), pl.BlockSpec((s, cw), lambda i: (0, 1)), wspec, vec, vec]
        + [ANY_SPEC] * n_send,
        out_specs=[pl.BlockSpec((s, 2 * cw), lambda i: (0, 0)), wspec, vec, vec, vec] + [ANY_SPEC] * n_send,
        out_shape=[jax.ShapeDtypeStruct((s, 2 * cw), BF16), jax.ShapeDtypeStruct((CONV_KERNEL, cw), F32)]
        + [jax.ShapeDtypeStruct((1, cw), F32)] * 3 + _scattered_shapes(sends),
        scratch_shapes=[pltpu.VMEM((CONV_HALO + s, cw), F32), pltpu.VMEM((s + CONV_HALO, cw), F32),
                        pltpu.VMEM((CONV_KERNEL, 8, cw), F32)] + _exchange_sems(n_send),
        compiler_params=_params(1),
    )(proj, proj, y, dcat, w, lg, lb, *sends)
    return out[:5], out[5:]


FFN_ROWS = 256


def _ffn_gate(g_ref, r0, rows, w_ref, b_ref):
    cur = g_ref[pl.ds(r0, rows), :].astype(F32)
    prev = g_ref[pl.ds(pl.multiple_of(jnp.maximum(r0 - FFN_HALO, 0), FFN_HALO), FFN_HALO), :].astype(F32)
    prev = jnp.where(r0 > 0, prev, 0.0)
    window = jnp.concatenate([prev, cur], axis=0)
    gc = cur * w_ref[FFN_KERNEL - 1:FFN_KERNEL, :] + b_ref[...]
    for k in range(FFN_KERNEL - 1):
        gc = gc + _shifted(window, FFN_KERNEL - 1 - k, FFN_HALO) * w_ref[k:k + 1, :]
    return gc, window


def _ffn_fwd(up, w, b, name):
    s = up.shape[0]
    f = w.shape[1]
    tc = _pick(f, (256, 128))
    nc = f // tc
    rows = _pick(s, (FFN_ROWS, 128))

    def body(g_ref, v_ref, w_ref, b_ref, o_ref):
        def chunk(i, _):
            r0 = pl.multiple_of(i * rows, rows)
            gc, _w = _ffn_gate(g_ref, r0, rows, w_ref, b_ref)
            o_ref[pl.ds(r0, rows), :] = (gc * _sigmoid(gc) * v_ref[pl.ds(r0, rows), :].astype(F32)).astype(BF16)
            return 0

        lax.fori_loop(0, s // rows, chunk, 0)

    return pl.pallas_call(
        body, name=name, grid=(nc,),
        in_specs=[pl.BlockSpec((s, tc), lambda j: (0, j)), pl.BlockSpec((s, tc), lambda j: (0, nc + j)),
                  pl.BlockSpec((FFN_KERNEL, tc), lambda j: (0, j)), pl.BlockSpec((1, tc), lambda j: (0, j))],
        out_specs=pl.BlockSpec((s, tc), lambda j: (0, j)),
        out_shape=jax.ShapeDtypeStruct((s, f), BF16),
        compiler_params=_params(1),
    )(up, up, w, b)


def _ffn_bwd(up, dact, w, b, name):
    s = up.shape[0]
    f = w.shape[1]
    tc = _pick(f, (256, 128))
    nc = f // tc
    rows = _pick(s, (FFN_ROWS, 128))
    n_chunk = s // rows

    def body(g_ref, v_ref, da_ref, w_ref, b_ref, d_ref, dw_ref, db_ref, dgc_s):
        dg_ref, dv_ref = d_ref.at[0], d_ref.at[1]
        dgc_s[pl.ds(s, FFN_HALO), :] = jnp.zeros((FFN_HALO, tc), F32)

        def chunk(i, carry):
            r0 = pl.multiple_of(i * rows, rows)
            gc, window = _ffn_gate(g_ref, r0, rows, w_ref, b_ref)
            sg = _sigmoid(gc)
            da = da_ref[pl.ds(r0, rows), :].astype(F32)
            dv_ref[pl.ds(r0, rows), :] = (da * gc * sg).astype(BF16)
            dgc = da * v_ref[pl.ds(r0, rows), :].astype(F32) * (sg * (1.0 + gc * (1.0 - sg)))
            dgc_s[pl.ds(r0, rows), :] = dgc
            out = [carry[0] + jnp.sum(dgc, axis=0, keepdims=True)]
            for k in range(FFN_KERNEL):
                out.append(carry[1 + k] + jnp.sum(_shifted(window, FFN_KERNEL - 1 - k, FFN_HALO) * dgc,
                                                  axis=0, keepdims=True))
            return tuple(out)

        zero = jnp.zeros((1, tc), F32)
        sums = lax.fori_loop(0, n_chunk, chunk, (zero,) * (1 + FFN_KERNEL))
        db_ref[...] = sums[0]
        for k in range(FFN_KERNEL):
            dw_ref[k:k + 1, :] = sums[1 + k]

        def back(i, _):
            r0 = pl.multiple_of(i * rows, rows)
            window = dgc_s[pl.ds(r0, rows + FFN_HALO), :]
            dg = window[:rows, :] * w_ref[FFN_KERNEL - 1:FFN_KERNEL, :]
            for k in range(FFN_KERNEL - 1):
                sh = FFN_KERNEL - 1 - k
                dg = dg + pltpu.roll(window, rows + FFN_HALO - sh, 0)[:rows, :] * w_ref[k:k + 1, :]
            dg_ref[pl.ds(r0, rows), :] = dg.astype(BF16)
            return 0

        lax.fori_loop(0, n_chunk, back, 0)

    blk = lambda off: pl.BlockSpec((s, tc), lambda j: (0, off + j))
    return pl.pallas_call(
        body, name=name, grid=(nc,),
        in_specs=[blk(0), blk(nc), blk(0), pl.BlockSpec((FFN_KERNEL, tc), lambda j: (0, j)),
                  pl.BlockSpec((1, tc), lambda j: (0, j))],
        out_specs=[pl.BlockSpec((2, s, tc), lambda j: (0, 0, j)), pl.BlockSpec((FFN_KERNEL, tc), lambda j: (0, j)),
                   pl.BlockSpec((1, tc), lambda j: (0, j))],
        out_shape=[jax.ShapeDtypeStruct((2, s, f), BF16),
                   jax.ShapeDtypeStruct((FFN_KERNEL, f), F32), jax.ShapeDtypeStruct((1, f), F32)],
        scratch_shapes=[pltpu.VMEM((s + FFN_HALO, tc), F32)],
        compiler_params=_params(1),
    )(up, up, dact, w, b)


def _loss_head(y, target, name):
    m, n = y.shape
    tm = _pick(m, (256, 128))

    def body(y_ref, t_ref, l_ref, d_ref, db_ref):
        e = y_ref[...] - t_ref[...]
        part = 0.5 * jnp.sum(jnp.sum(e * e, axis=-1, keepdims=True) / n, axis=0, keepdims=True)

        @pl.when(pl.program_id(0) == 0)
        def _():
            l_ref[...] = jnp.zeros_like(l_ref)

        l_ref[...] += part
        d = e / n
        d_ref[...] = d
        db_ref[...] = d.astype(BF16)

    return pl.pallas_call(
        body, name=name, grid=(m // tm,),
        in_specs=[pl.BlockSpec((tm, n), lambda i: (i, 0))] * 2,
        out_specs=[pl.BlockSpec((8, LANES), lambda i: (0, 0)), pl.BlockSpec((tm, n), lambda i: (i, 0)),
                   pl.BlockSpec((tm, n), lambda i: (i, 0))],
        out_shape=[jax.ShapeDtypeStruct((8, LANES), F32), jax.ShapeDtypeStruct((m, n), F32),
                   jax.ShapeDtypeStruct((m, n), BF16)],
        compiler_params=_params(1),
    )(y, target)


def _adamw_math(w, g, m, v):
    m = ADAM_B1 * m + (1.0 - ADAM_B1) * g
    v = ADAM_B2 * v + (1.0 - ADAM_B2) * (g * g)
    m_hat = m / (1.0 - ADAM_B1 ** ADAM_STEP)
    v_hat = v / (1.0 - ADAM_B2 ** ADAM_STEP)
    delta = -ADAM_LR * (m_hat / (jnp.sqrt(v_hat) + ADAM_EPS) + ADAM_WD * w)
    return delta, m, v


def _sum_adamw(parts, w, m, v, sends, name):
    depth, r, c = w.shape
    tr = max(t for t in range(16, min(r, 192) + 1, 16) if r % t == 0)
    steps = r // tr
    n_send = len(sends)

    def body(*refs):
        p_refs, refs = refs[:depth], refs[depth:]
        (w_ref, m_ref, v_ref), send_refs, refs = refs[:3], refs[3:3 + n_send], refs[3 + n_send:]
        (g_out, d_out, m_out, v_out), got_refs, sems = refs[:4], refs[4:4 + n_send], refs[4 + n_send:]
        is_first = (pl.program_id(0) == 0) & (pl.program_id(1) == 0)
        is_last = (pl.program_id(0) == depth - 1) & (pl.program_id(1) == steps - 1)
        if n_send:
            start, relay, finish = _two_level_gather(send_refs, got_refs, sems)
            pl.when(is_first)(start)
            pl.when(is_last)(relay)
        for layer in range(depth):
            @pl.when(pl.program_id(0) == layer)
            def _():
                g = p_refs[layer][0].astype(F32)
                for src in range(1, N_DEV):
                    g = g + p_refs[layer][src].astype(F32)
                d, mn, vn = _adamw_math(w_ref[0], g, m_ref[0], v_ref[0])
                g_out[0] = g
                d_out[0] = d
                m_out[0] = mn
                v_out[0] = vn

        if n_send:
            pl.when(is_last)(finish)

    def part_spec(layer):
        return pl.BlockSpec((N_DEV, tr, c), lambda l, i: (0, jnp.clip((l - layer) * steps + i, 0, steps - 1), 0))

    blk = pl.BlockSpec((1, tr, c), lambda l, i: (l, i, 0))
    out = pl.pallas_call(
        body, name=name, grid=(depth, steps),
        in_specs=[part_spec(layer) for layer in range(depth)] + [blk, blk, blk] + [ANY_SPEC] * n_send,
        out_specs=[blk] * 4 + [ANY_SPEC] * n_send,
        out_shape=[jax.ShapeDtypeStruct(w.shape, F32)] * 4 + _gathered_shapes(sends),
        scratch_shapes=_exchange_sems(n_send) if n_send else [],
        compiler_params=_params(2),
    )(*parts, w, m, v, *sends)
    return out[:4], out[4:]


VMEM_SPEC = pl.BlockSpec(memory_space=pltpu.VMEM)


def _sum_small(parts, name):
    n = len(parts)

    def body(*refs):
        for p_ref, o_ref in zip(refs[:n], refs[n:]):
            g = p_ref[0]
            for src in range(1, N_DEV):
                g = g + p_ref[src]
            o_ref[...] = g

    return pl.pallas_call(
        body, name=name, in_specs=[VMEM_SPEC] * n, out_specs=[VMEM_SPEC] * n,
        out_shape=[jax.ShapeDtypeStruct(p.shape[1:], F32) for p in parts],
        compiler_params=_params(),
    )(*parts)


def _adamw_small(ws, gs, ms, vs, name):
    n = len(ws)

    def body(*refs):
        ins, outs = refs[:4 * n], refs[4 * n:]
        for i in range(n):
            d, mn, vn = _adamw_math(ins[i][...], ins[n + i][...], ins[2 * n + i][...], ins[3 * n + i][...])
            outs[i][...] = d
            outs[n + i][...] = mn
            outs[2 * n + i][...] = vn

    out = pl.pallas_call(
        body, name=name, in_specs=[VMEM_SPEC] * (4 * n), out_specs=[VMEM_SPEC] * (3 * n),
        out_shape=[jax.ShapeDtypeStruct(w.shape, F32) for w in ws] * 3,
        compiler_params=_params(),
    )(*ws, *gs, *ms, *vs)
    return out[:n], out[n:2 * n], out[2 * n:]


def kernel(x, norm1_g, w_in, q_norm_g, k_norm_g, conv_dw_w, conv_dw_b, conv_ln_g, conv_ln_b, w_out, norm2_g, w_up, ffn_dw_w, ffn_dw_b, w_down, loss_target, m_norm1_g, m_w_in, m_q_norm_g, m_k_norm_g, m_conv_dw_w, m_conv_dw_b, m_conv_ln_g, m_conv_ln_b, m_w_out, m_norm2_g, m_w_up, m_ffn_dw_w, m_ffn_dw_b, m_w_down, v_norm1_g, v_w_in, v_q_norm_g, v_k_norm_g, v_conv_dw_w, v_conv_dw_b, v_conv_ln_g, v_conv_ln_b, v_w_out, v_norm2_g, v_w_up, v_ffn_dw_w, v_ffn_dw_b, v_w_down):
    depth = w_in.shape[0]
    d_ff = w_down.shape[1] * N_DEV
    conv_w = conv_dw_b.shape[1]
    cw_shard = conv_dw_w.shape[2]
    fw_shard = ffn_dw_w.shape[2]
    me = 4 * lax.axis_index("x") + 2 * lax.axis_index("y") + lax.axis_index("c")

    transposed = lambda a: a.transpose(0, 2, 1)
    b_in, b_out, b_up, b_down = (transposed(w_in).astype(BF16), w_out.astype(BF16), transposed(w_up).astype(BF16),
                                 w_down.astype(BF16))
    rows_major = lambda g: g.reshape(N_DEV * g.shape[1], g.shape[2])
    g_in0, g_cw, g_fw = _gather([b_in[0], conv_dw_w, ffn_dw_w], name="gather_first")
    wf_in, wf_out, wf_up, wf_down = [rows_major(g_in0)] + [None] * (depth - 1), [None] * depth, [None] * depth, [None] * depth
    cwf = g_cw.transpose(1, 2, 0, 3).reshape(depth, CONV_KERNEL, conv_w)
    fwf = g_fw.transpose(1, 2, 0, 3).reshape(depth, FFN_KERNEL, d_ff)

    row = lambda a, l: a[l].reshape(1, -1)
    both_heads = lambda a, l: jnp.tile(row(a, l), (1, HEADS_PER_STEP))

    xs = x[0]
    saved = []
    for l in range(depth):
        proj, h1, _ = _mm_rms(xs, row(norm1_g, l), wf_in[l], F32, [], name="fwd_in")
        sends = [b_out[l], b_up[l]] + ([b_in[l + 1]] if l + 1 < depth else [])
        attn, tsum, got = _attn_fwd(proj, both_heads(q_norm_g, l), both_heads(k_norm_g, l), sends, name="fwd_attn")
        wf_out[l], wf_up[l] = rows_major(got[0]), rows_major(got[1])
        if l + 1 < depth:
            wf_in[l + 1] = rows_major(got[2])
        conv, conv_y = _conv_fwd(proj, cwf[l], row(conv_dw_b, l), row(conv_ln_g, l), row(conv_ln_b, l),
                                 name="fwd_conv")
        cat = jnp.concatenate([attn, conv], axis=1)
        x_mid = _mm_res(cat, wf_out[l], xs, name="fwd_out")
        up, h2, got = _mm_rms(x_mid, row(norm2_g, l), wf_up[l], BF16, [b_down[l]], name="fwd_up")
        wf_down[l] = rows_major(got[0])
        act = _ffn_fwd(up, fwf[l], row(ffn_dw_b, l), name="fwd_ffn")
        x_next = _mm_res(act, wf_down[l], x_mid, name="fwd_down")
        saved.append((xs, h1, proj, tsum, cat, x_mid, h2, up, act, conv_y))
        xs = x_next

    loss_tile, dx, dxb = _loss_head(xs, loss_target[0], name="loss_head")
    loss = lax.psum(loss_tile[0, 0], ("x", "y", "c"))

    r_in, r_out, r_up, r_down = [None] * depth, [None] * depth, [None] * depth, [None] * depth
    row_blocks = lambda g: g.reshape(N_DEV, g.shape[0] // N_DEV, g.shape[1])
    small = {k: [None] * depth for k in ("norm1_g", "q_norm_g", "k_norm_g", "conv_dw_w", "conv_dw_b", "conv_ln_g",
                                         "conv_ln_b", "norm2_g", "ffn_dw_w", "ffn_dw_b")}
    gw_in = None
    for l in reversed(range(depth)):
        xs, h1, proj, tsum, cat, x_mid, h2, up, act, conv_y = saved[l]
        dact = _mm_nt(dxb, wf_down[l], BF16, name="bwd_dact")
        gw_down = _mm_tn(act, dxb, name="bwd_gw_down")
        dup, small["ffn_dw_w"][l], small["ffn_dw_b"][l] = _ffn_bwd(up, dact, fwf[l], row(ffn_dw_b, l), name="bwd_ffn")
        gw_up = _mm_tn(dup, h2, name="bwd_gw_up")
        dx, dxb, small["norm2_g"][l], _ = _mm_rmsbwd(dup, wf_up[l], x_mid, row(norm2_g, l), dx, [], name="bwd_up")
        dcat = _mm_nt(dxb, wf_out[l], F32, name="bwd_dcat")
        gw_out = _mm_tn(cat, dxb, name="bwd_gw_out")
        early = gw_in if l + 1 < depth else gw_out
        (dglu, small["conv_dw_w"][l], small["conv_dw_b"][l], small["conv_ln_g"][l], small["conv_ln_b"][l]), got = (
            _conv_bwd(proj, conv_y, dcat, cwf[l], row(conv_ln_g, l), row(conv_ln_b, l), [row_blocks(early)],
                      name="bwd_conv"))
        sends = [row_blocks(gw_down), row_blocks(gw_up)] + ([row_blocks(gw_out)] if l + 1 < depth else [])
        (dq, dk, dv, dqg, dkg), got2 = _attn_bwd(proj, dcat, tsum, both_heads(q_norm_g, l), both_heads(k_norm_g, l),
                                                 sends, name="bwd_attn")
        r_down[l], r_up[l] = got2[:2]
        if l + 1 < depth:
            r_in[l + 1], r_out[l] = got[0], got2[2]
        else:
            r_out[l] = got[0]
        small["q_norm_g"][l] = dqg[0:1, :HEAD_DIM] + dqg[0:1, HEAD_DIM:]
        small["k_norm_g"][l] = dkg[0:1, :HEAD_DIM] + dkg[0:1, HEAD_DIM:]
        dproj = jnp.concatenate([dq, dk, dv, dglu], axis=1)
        gw_in = _mm_tn(dproj, h1, name="bwd_gw_in")
        sends = [row_blocks(gw_in)] if l == 0 else []
        dx, dxb, small["norm1_g"][l], got = _mm_rmsbwd(dproj, wf_in[l], xs, row(norm1_g, l), dx, sends, name="bwd_in")
        if l == 0:
            r_in[0] = got[0]
    grad_x = dx[None]

    names = ["norm1_g", "q_norm_g", "k_norm_g", "conv_dw_w", "conv_dw_b", "conv_ln_g", "conv_ln_b", "norm2_g",
             "ffn_dw_w", "ffn_dw_b"]
    full_shapes = {"norm1_g": norm1_g.shape, "q_norm_g": q_norm_g.shape, "k_norm_g": k_norm_g.shape,
                   "conv_dw_w": (depth, CONV_KERNEL, conv_w), "conv_dw_b": conv_dw_b.shape,
                   "conv_ln_g": conv_ln_g.shape, "conv_ln_b": conv_ln_b.shape, "norm2_g": norm2_g.shape,
                   "ffn_dw_w": (depth, FFN_KERNEL, d_ff), "ffn_dw_b": ffn_dw_b.shape}
    partial = [jnp.stack(small[k]).reshape(full_shapes[k]) for k in names]
    big = {}
    big["w_out"], all_partials = _sum_adamw(r_out, w_out, m_w_out, v_w_out, partial, name="adamw_out")
    big["w_up"], _ = _sum_adamw(r_up, transposed(w_up), transposed(m_w_up), transposed(v_w_up), [], name="adamw_up")
    big["w_down"], _ = _sum_adamw(r_down, w_down, m_w_down, v_w_down, [], name="adamw_down")
    big["w_in"], _ = _sum_adamw(r_in, transposed(w_in), transposed(m_w_in), transposed(v_w_in), [], name="adamw_in")
    for k in ("w_in", "w_up"):
        big[k] = [transposed(a) for a in big[k]]

    grads = dict(zip(names, _sum_small(all_partials, name="sum_small_grads")))
    grads["conv_dw_w"] = lax.dynamic_slice_in_dim(grads["conv_dw_w"], me * cw_shard, cw_shard, axis=2)
    grads["ffn_dw_w"] = lax.dynamic_slice_in_dim(grads["ffn_dw_w"], me * fw_shard, fw_shard, axis=2)
    weights = dict(norm1_g=norm1_g, q_norm_g=q_norm_g, k_norm_g=k_norm_g, conv_dw_w=conv_dw_w, conv_dw_b=conv_dw_b,
                   conv_ln_g=conv_ln_g, conv_ln_b=conv_ln_b, norm2_g=norm2_g, ffn_dw_w=ffn_dw_w, ffn_dw_b=ffn_dw_b)
    m_in = dict(norm1_g=m_norm1_g, q_norm_g=m_q_norm_g, k_norm_g=m_k_norm_g, conv_dw_w=m_conv_dw_w,
                conv_dw_b=m_conv_dw_b, conv_ln_g=m_conv_ln_g, conv_ln_b=m_conv_ln_b, norm2_g=m_norm2_g,
                ffn_dw_w=m_ffn_dw_w, ffn_dw_b=m_ffn_dw_b)
    v_in = dict(norm1_g=v_norm1_g, q_norm_g=v_q_norm_g, k_norm_g=v_k_norm_g, conv_dw_w=v_conv_dw_w,
                conv_dw_b=v_conv_dw_b, conv_ln_g=v_conv_ln_g, conv_ln_b=v_conv_ln_b, norm2_g=v_norm2_g,
                ffn_dw_w=v_ffn_dw_w, ffn_dw_b=v_ffn_dw_b)
    d_s, m_s, v_s = _adamw_small([weights[k] for k in names], [grads[k] for k in names], [m_in[k] for k in names],
                                 [v_in[k] for k in names], name="adamw_small")
    delta, new_m, new_v = dict(zip(names, d_s)), dict(zip(names, m_s)), dict(zip(names, v_s))
    for k, (g, d, mn, vn) in big.items():
        grads[k], delta[k], new_m[k], new_v[k] = g, d, mn, vn

    order = ["norm1_g", "w_in", "q_norm_g", "k_norm_g", "conv_dw_w", "conv_dw_b", "conv_ln_g", "conv_ln_b", "w_out",
             "norm2_g", "w_up", "ffn_dw_w", "ffn_dw_b", "w_down"]
    return (loss, grad_x, *[grads[k] for k in order], *[delta[k] for k in order], *[new_m[k] for k in order],
            *[new_v[k] for k in order])
```

```python
import jax
import jax.numpy as jnp
from jax import lax
from jax.experimental import pallas as pl
from jax.experimental.pallas import tpu as pltpu

F32 = jnp.float32
BF16 = jnp.bfloat16

N_DEV = 8
HEADS = 8
HEAD_DIM = 64
ATTN_WIDTH = HEADS * HEAD_DIM
CONV_KERNEL = 31
FFN_KERNEL = 3
EPS = 1e-6
BLK = 128
KEY_GROUP = 4
LANES = 128
NORM_ROWS = 128
CONV_HALO = 32
FFN_HALO = 16

ADAM_LR = 0.001
ADAM_B1 = 0.9
ADAM_B2 = 0.999
ADAM_EPS = 1e-08
ADAM_WD = 0.01
ADAM_STEP = 10

VMEM_LIMIT = 56 * 1024 * 1024


def _params(n_axes=0):
    kw = dict(vmem_limit_bytes=VMEM_LIMIT)
    if n_axes:
        kw["dimension_semantics"] = ("arbitrary",) * n_axes
    return pltpu.CompilerParams(**kw)


def _dot(a, b):
    return jnp.dot(a, b, preferred_element_type=F32)


def _dot_nt(a, b):
    return lax.dot_general(a, b, (((1,), (1,)), ((), ())), preferred_element_type=F32)


def _dot_tn(a, b):
    return lax.dot_general(a, b, (((0,), (0,)), ((), ())), preferred_element_type=F32)


def _sigmoid(x):
    return 1.0 / (1.0 + jnp.exp(-x))


def _split_bf16(x):
    hi = x.astype(BF16)
    lo = (x - hi.astype(F32)).astype(BF16)
    return hi, lo


def _pick(n, options):
    for t in options:
        if n % t == 0:
            return t
    return n


def _tile(n, cap):
    best = None
    for t in range(LANES, min(n, cap) + 1, LANES):
        if n % t == 0:
            best = t
    return best or n


def _mm_rms(x, g, wt, out_dtype, sends, name):
    m, k = x.shape
    n = wt.shape[0]
    tm = _tile(m, 2048)
    tn = _tile(n, 512)
    n_send = len(sends)
    grid = (m // tm, n // tn)

    def body(x_ref, g_ref, w_ref, *rest):
        send_refs, (o_ref, h_ref), rest = rest[:n_send], rest[n_send:n_send + 2], rest[n_send + 2:]
        got_refs, h_s, sems = rest[:n_send], rest[n_send], rest[n_send + 1:]
        if n_send:
            start, relay, finish = _two_level_gather(send_refs, got_refs, sems)
            is_first = (pl.program_id(0) == 0) & (pl.program_id(1) == 0)
            is_last = (pl.program_id(0) == grid[0] - 1) & (pl.program_id(1) == grid[1] - 1)
            pl.when(is_first)(start)
            pl.when(is_last)(relay)

        @pl.when(pl.program_id(1) == 0)
        def _():
            def chunk(c, _):
                rows = pl.ds(pl.multiple_of(c * NORM_ROWS, NORM_ROWS), NORM_ROWS)
                xv = x_ref[rows, :]
                r = lax.rsqrt(jnp.mean(xv * xv, axis=-1, keepdims=True) + EPS)
                hv = (xv * r * g_ref[...]).astype(BF16)
                h_s[rows, :] = hv
                h_ref[rows, :] = hv
                return 0

            lax.fori_loop(0, tm // NORM_ROWS, chunk, 0)

        o_ref[...] = _dot_nt(h_s[...], w_ref[...]).astype(out_dtype)
        if n_send:
            pl.when(is_last)(finish)

    out = pl.pallas_call(
        body, name=name, grid=grid,
        in_specs=[pl.BlockSpec((tm, k), lambda i, j: (i, 0)),
                  pl.BlockSpec((1, k), lambda i, j: (0, 0)),
                  pl.BlockSpec((tn, k), lambda i, j: (j, 0))] + [ANY_SPEC] * n_send,
        out_specs=[pl.BlockSpec((tm, tn), lambda i, j: (i, j)),
                   pl.BlockSpec((tm, k), lambda i, j: (i, 0))] + [ANY_SPEC] * n_send,
        out_shape=[jax.ShapeDtypeStruct((m, n), out_dtype), jax.ShapeDtypeStruct((m, k), BF16)]
        + _gathered_shapes(sends),
        scratch_shapes=[pltpu.VMEM((tm, k), BF16)] + (_exchange_sems(n_send) if n_send else []),
        compiler_params=_params(2),
    )(x, g, wt, *sends)
    return out[0], out[1], out[2:]


def _mm_res(a, w, res, name):
    m, k = a.shape
    n = w.shape[1]
    tm = _tile(m, 1024)
    tn = _tile(n, 512)

    def body(a_ref, w_ref, r_ref, o_ref):
        o_ref[...] = r_ref[...] + _dot(a_ref[...], w_ref[...])

    return pl.pallas_call(
        body, name=name, grid=(m // tm, n // tn),
        in_specs=[pl.BlockSpec((tm, k), lambda i, j: (i, 0)),
                  pl.BlockSpec((k, tn), lambda i, j: (0, j)),
                  pl.BlockSpec((tm, tn), lambda i, j: (i, j))],
        out_specs=pl.BlockSpec((tm, tn), lambda i, j: (i, j)),
        out_shape=jax.ShapeDtypeStruct((m, n), F32),
        compiler_params=_params(2),
    )(a, w, res)


def _mm_nt(a, w, out_dtype, name):
    m, k = a.shape
    n = w.shape[0]
    tm = _tile(m, 1024)
    tn = _tile(n, 1408)

    def body(a_ref, w_ref, o_ref):
        o_ref[...] = _dot_nt(a_ref[...], w_ref[...]).astype(out_dtype)

    return pl.pallas_call(
        body, name=name, grid=(m // tm, n // tn),
        in_specs=[pl.BlockSpec((tm, k), lambda i, j: (i, 0)),
                  pl.BlockSpec((tn, k), lambda i, j: (j, 0))],
        out_specs=pl.BlockSpec((tm, tn), lambda i, j: (i, j)),
        out_shape=jax.ShapeDtypeStruct((m, n), out_dtype),
        compiler_params=_params(2),
    )(a, w)


def _column_tiles(a, cap):
    if a.ndim == 2:
        s, c = a.shape
        tc = _tile(c, cap)
        return s, c, tc, lambda rows, index: pl.BlockSpec((rows, tc), lambda *g: (index(*g)[0], index(*g)[1]))
    slabs, s, width = a.shape
    tc = _tile(width, cap)
    per = width // tc
    return s, slabs * width, tc, lambda rows, index: pl.BlockSpec(
        (None, rows, tc), lambda *g: (index(*g)[1] // per, index(*g)[0], index(*g)[1] % per))


def _mm_tn(a, b, name):
    s, m, tm, a_spec = _column_tiles(a, 1408)
    n = b.shape[1]
    tn = _tile(n, 1024)

    def body(a_ref, b_ref, o_ref):
        o_ref[...] = _dot_tn(a_ref[...], b_ref[...]).astype(BF16)

    return pl.pallas_call(
        body, name=name, grid=(m // tm, n // tn),
        in_specs=[a_spec(s, lambda i, j: (0, i)),
                  pl.BlockSpec((s, tn), lambda i, j: (0, j))],
        out_specs=pl.BlockSpec((tm, tn), lambda i, j: (i, j)),
        out_shape=jax.ShapeDtypeStruct((m, n), BF16),
        compiler_params=_params(2),
    )(a, b)


def _mm_rmsbwd(a, w, x, g, dres, sends, name):
    m, k, tk, a_spec = _column_tiles(a, 1408)
    n = w.shape[1]
    tm = _tile(m, 1024)
    nk = k // tk
    n_send = len(sends)

    def body(a_ref, w_ref, x_ref, g_ref, r_ref, *rest):
        send_refs, (dx_ref, dxb_ref, dg_ref), rest = rest[:n_send], rest[n_send:n_send + 3], rest[n_send + 3:]
        got_refs, acc, sems = rest[:n_send], rest[n_send], rest[n_send + 1:]
        i, kk = pl.program_id(0), pl.program_id(1)
        if n_send:
            copies = _scatter_copies(send_refs, got_refs, sems)

            @pl.when((i == 0) & (kk == 0))
            def _():
                for cp in copies:
                    cp.start()

        part = _dot(a_ref[...], w_ref[...])

        @pl.when(kk == 0)
        def _():
            acc[...] = part

        @pl.when(kk > 0)
        def _():
            acc[...] += part

        @pl.when(kk == nk - 1)
        def _():
            def chunk(c, dgp):
                rows = pl.ds(pl.multiple_of(c * NORM_ROWS, NORM_ROWS), NORM_ROWS)
                dh = acc[rows, :]
                xv = x_ref[rows, :]
                r = lax.rsqrt(jnp.mean(xv * xv, axis=-1, keepdims=True) + EPS)
                xh = xv * r
                dxh = dh * g_ref[...]
                dx = r_ref[rows, :] + r * (dxh - xh * jnp.mean(dxh * xh, axis=-1, keepdims=True))
                dx_ref[rows, :] = dx
                dxb_ref[rows, :] = dx.astype(BF16)
                return dgp + jnp.sum(dh * xh, axis=0, keepdims=True)

            dgp = lax.fori_loop(0, tm // NORM_ROWS, chunk, jnp.zeros((1, n), F32))

            @pl.when(i == 0)
            def _():
                dg_ref[...] = dgp

            @pl.when(i > 0)
            def _():
                dg_ref[...] += dgp

        if n_send:
            @pl.when((i == m // tm - 1) & (kk == nk - 1))
            def _():
                for cp in copies:
                    cp.wait()

    out = pl.pallas_call(
        body, name=name, grid=(m // tm, nk),
        in_specs=[a_spec(tm, lambda i, kk: (i, kk)),
                  pl.BlockSpec((tk, n), lambda i, kk: (kk, 0)),
                  pl.BlockSpec((tm, n), lambda i, kk: (i, 0)),
                  pl.BlockSpec((1, n), lambda i, kk: (0, 0)),
                  pl.BlockSpec((tm, n), lambda i, kk: (i, 0))] + [ANY_SPEC] * n_send,
        out_specs=[pl.BlockSpec((tm, n), lambda i, kk: (i, 0)),
                   pl.BlockSpec((tm, n), lambda i, kk: (i, 0)),
                   pl.BlockSpec((1, n), lambda i, kk: (0, 0))] + [ANY_SPEC] * n_send,
        out_shape=[jax.ShapeDtypeStruct((m, n), F32), jax.ShapeDtypeStruct((m, n), BF16),
                   jax.ShapeDtypeStruct((1, n), F32)] + _scattered_shapes(sends),
        scratch_shapes=[pltpu.VMEM((tm, n), F32)] + (_exchange_sems(n_send) if n_send else []),
        compiler_params=_params(2),
    )(a, w, x, g, dres, *sends)
    return out[0], out[1], out[2], out[3:]


ANY_SPEC = pl.BlockSpec(memory_space=pl.ANY)
SEMS_PER_OPERAND = N_DEV - 1


def _exchange_sems(n):
    return [pltpu.SemaphoreType.DMA((n, SEMS_PER_OPERAND)), pltpu.SemaphoreType.DMA((n, SEMS_PER_OPERAND)),
            pltpu.SemaphoreType.DMA((n,))]


def _gathered_shapes(parts):
    return [jax.ShapeDtypeStruct((N_DEV,) + a.shape, a.dtype) for a in parts]


def _scattered_shapes(parts):
    return [jax.ShapeDtypeStruct(a.shape, a.dtype) for a in parts]


def _flat(pos):
    return 4 * pos[0] + 2 * pos[1] + pos[2]


def _remote(src, dst, sems, i, k, to):
    send_sems, recv_sems, _ = sems
    return pltpu.make_async_remote_copy(src_ref=src, dst_ref=dst, send_sem=send_sems.at[i, k],
                                        recv_sem=recv_sems.at[i, k], device_id=to,
                                        device_id_type=pl.DeviceIdType.MESH)


def _scatter_copies(ins, outs, sems):
    x, y, c = lax.axis_index("x"), lax.axis_index("y"), lax.axis_index("c")
    me = _flat((x, y, c))
    copies = [pltpu.make_async_copy(ins[i].at[me], outs[i].at[me], sems[2].at[i]) for i in range(len(ins))]
    for d in range(1, N_DEV):
        peer = (1 - x if d & 4 else x, 1 - y if d & 2 else y, 1 - c if d & 1 else c)
        for i in range(len(ins)):
            copies.append(_remote(ins[i].at[_flat(peer)], outs[i].at[me], sems, i, d - 1, peer))
    return copies


def _two_level_gather(ins, outs, sems):
    x, y, c = lax.axis_index("x"), lax.axis_index("y"), lax.axis_index("c")
    me, sibling = (x, y, c), (x, y, 1 - c)
    chips = [(1 - x, y), (x, 1 - y), (1 - x, 1 - y)]
    n = len(ins)

    def block(i, pos):
        return outs[i].at[_flat(pos)]

    local = [pltpu.make_async_copy(ins[i], block(i, me), sems[2].at[i]) for i in range(n)]
    own = [_remote(ins[i], block(i, me), sems, i, 0, sibling) for i in range(n)]
    own += [_remote(ins[i], block(i, me), sems, i, 1 + j, (*chip, c)) for i in range(n) for j, chip in enumerate(chips)]
    passed = [[_remote(block(i, (*chip, c)), block(i, (*chip, c)), sems, i, 4 + j, sibling) for i in range(n)]
              for j, chip in enumerate(chips)]

    def first():
        for cp in local + own:
            cp.start()

    def relay():
        for j, chip in enumerate(chips):
            for i in range(n):
                _remote(ins[i], block(i, (*chip, c)), sems, i, 1 + j, me).wait_recv()
                passed[j][i].start()

    def finish():
        for i in range(n):
            _remote(ins[i], block(i, sibling), sems, i, 0, me).wait_recv()
            for j, chip in enumerate(chips):
                _remote(ins[i], block(i, (*chip, 1 - c)), sems, i, 4 + j, me).wait_recv()
        for cp in own + [cp for row in passed for cp in row]:
            cp.wait_send()
        for cp in local:
            cp.wait()

    return first, relay, finish


def _gather(parts, name):
    n = len(parts)

    def body(*refs):
        start, relay, finish = _two_level_gather(refs[:n], refs[n:2 * n], refs[2 * n:])
        start()
        relay()
        finish()

    return pl.pallas_call(
        body, name=name, in_specs=[ANY_SPEC] * n, out_specs=[ANY_SPEC] * n,
        out_shape=_gathered_shapes(parts), scratch_shapes=_exchange_sems(n),
    )(*parts)


def _tri(kind):
    j = lax.broadcasted_iota(jnp.int32, (BLK, BLK), 0)
    s = lax.broadcasted_iota(jnp.int32, (BLK, BLK), 1)
    m = {"after": j > s, "upto": j <= s, "before": j < s}[kind]
    return jnp.concatenate([jnp.where(m, 1.0, 0.0), jnp.ones((BLK, BLK), F32)], axis=1).astype(BF16)


def _scan_rows(v, tri):
    r = _dot(v.astype(BF16), tri)
    return r[:, :BLK], r[:, BLK:]


HEADS_PER_STEP = LANES // HEAD_DIM


def _first_head_lanes():
    return lax.broadcasted_iota(jnp.int32, (1, LANES), 1) < HEAD_DIM


def _pair_mean(v, first):
    m0 = jnp.sum(jnp.where(first, v, 0.0), axis=-1, keepdims=True)
    m1 = jnp.sum(jnp.where(first, 0.0, v), axis=-1, keepdims=True)
    return jnp.where(first, m0, m1) * (1.0 / HEAD_DIM)


def _pair_norm(v, g2, first):
    return v * lax.rsqrt(_pair_mean(v * v, first) + EPS) * g2


def _pair_norm_bwd(raw, g2, dn, first):
    r = lax.rsqrt(_pair_mean(raw * raw, first) + EPS)
    xh = raw * r
    dxh = dn * g2
    return r * (dxh - xh * _pair_mean(dxh * xh, first)), jnp.sum(dn * xh, axis=0, keepdims=True)


def _block_diag(v, first):
    zero = jnp.zeros_like(v)
    return jnp.concatenate([jnp.where(first, v, zero), jnp.where(first, zero, v)], axis=0)


def _attn_prep(q_ref, k_ref, v_ref, qg_ref, kg_ref, qc_s, kc_s, vd_s, kd_s, n_blk):
    scale = HEAD_DIM ** -0.5
    first = _first_head_lanes()

    def prep(i, _):
        rows = pl.ds(pl.multiple_of(i * BLK, BLK), BLK)
        both = pl.ds(pl.multiple_of(i * 2 * BLK, 2 * BLK), 2 * BLK)
        qh, ql = _split_bf16(_pair_norm(q_ref[rows, :], qg_ref[...], first) * scale)
        kh, kl = _split_bf16(_pair_norm(k_ref[rows, :], kg_ref[...], first))
        for h in range(HEADS_PER_STEP):
            sl = slice(h * HEAD_DIM, (h + 1) * HEAD_DIM)
            qc_s[h, rows, :] = jnp.concatenate([qh[:, sl], ql[:, sl], qh[:, sl], ql[:, sl]], axis=1)
            kc_s[h, rows, :] = jnp.concatenate([kh[:, sl], kh[:, sl], kl[:, sl], kl[:, sl]], axis=1)
        vd_s[both, :] = _block_diag(v_ref[rows, :].astype(BF16), first)
        if kd_s is not None:
            kd_s[both, :] = _block_diag(kh, first)
        return 0

    lax.fori_loop(0, n_blk, prep, 0)


def _pair_scores(qc, kc_ref, grp, n_tiles=KEY_GROUP):
    zs = []
    for j in range(0, n_tiles, 2):
        two = pl.ds(pl.multiple_of((grp * KEY_GROUP + j) * BLK, 2 * BLK), 2 * BLK)
        z = _dot_nt(qc, kc_ref[two, :])
        zs += [z[:, :BLK], z[:, BLK:]]
    return zs[:n_tiles]


def _col_minus_row():
    row = lax.broadcasted_iota(jnp.int32, (BLK, BLK), 0)
    col = lax.broadcasted_iota(jnp.int32, (BLK, BLK), 1)
    return col - row


def _softplus(z):
    return jnp.maximum(z, 0.0) + jnp.log(1.0 + jnp.exp(-jnp.abs(z)))


def _attn_fwd(proj, qg, kg, sends, name):
    s = proj.shape[0]
    n_blk = s // BLK
    pairs = ATTN_WIDTH // LANES
    n_send = len(sends)

    def body(q_ref, k_ref, v_ref, qg_ref, kg_ref, *rest):
        send_refs, (o_ref, t_ref), rest = rest[:n_send], rest[n_send:n_send + 2], rest[n_send + 2:]
        got_refs, (qc_s, kc_s, vd_s), sems = rest[:n_send], rest[n_send:n_send + 3], rest[n_send + 3:]
        start, relay, finish = _two_level_gather(send_refs, got_refs, sems)
        step = pl.program_id(0)
        pl.when(step == 0)(start)
        pl.when(step == pairs - 1)(relay)
        tri = _tri("after")
        diff = _col_minus_row()
        first = _first_head_lanes()
        heads = range(HEADS_PER_STEP)
        _attn_prep(q_ref, k_ref, v_ref, qg_ref, kg_ref, qc_s, kc_s, vd_s, None, n_blk)

        below = diff < 0

        def group(qc, grp, carry, acc, n_tiles, diagonal):
            blocks = [grp * KEY_GROUP + j for j in reversed(range(n_tiles))]
            zs = [_pair_scores(qc[h], kc_s.at[h], grp, n_tiles)[::-1] for h in heads]
            parts = [[None] * n_tiles for _ in heads]
            for h in heads:
                for j, z in enumerate(zs[h]):
                    sp = _softplus(z)
                    lom = -sp
                    if diagonal and j == 0:
                        lom = jnp.where(below, lom, 0.0)
                    tail, tot = _scan_rows(lom, tri)
                    parts[h][j] = (z - sp + tail, tot)
            carry = list(carry)
            for j, kb in enumerate(blocks):
                ws = []
                for h in heads:
                    lw, tot = parts[h][j]
                    w = jnp.exp(lw + carry[h])
                    if diagonal and j == 0:
                        w = jnp.where(below, w, 0.0)
                    ws.append(w.astype(BF16))
                    carry[h] = carry[h] + tot
                acc = acc + _dot(jnp.concatenate(ws, axis=1),
                                 vd_s[pl.ds(pl.multiple_of(kb * 2 * BLK, 2 * BLK), 2 * BLK), :])
            return tuple(carry), acc

        def q_blocks(top, _):
            for r in range(KEY_GROUP):
                rows = pl.ds(pl.multiple_of((top * KEY_GROUP + r) * BLK, BLK), BLK)
                qc = [qc_s[h, rows, :] for h in heads]
                zero = jnp.zeros((BLK, BLK), F32)
                carry, acc = group(qc, top, (zero,) * HEADS_PER_STEP, jnp.zeros((BLK, LANES), F32), r + 1, True)
                carry, acc = lax.fori_loop(
                    0, top, lambda t, c: group(qc, top - 1 - t, c[0], c[1], KEY_GROUP, False), (carry, acc))
                o_ref[rows, :] = acc.astype(BF16)
                t_ref[rows, :] = jnp.where(first, carry[0], carry[1])
            return 0

        lax.fori_loop(0, n_blk // KEY_GROUP, q_blocks, 0)
        pl.when(step == pairs - 1)(finish)

    col = lambda off: pl.BlockSpec((s, LANES), lambda p: (0, off + p))
    vec = pl.BlockSpec((1, LANES), lambda p: (0, 0))
    out = pl.pallas_call(
        body, name=name, grid=(pairs,),
        in_specs=[col(0), col(pairs), col(2 * pairs), vec, vec] + [ANY_SPEC] * n_send,
        out_specs=[pl.BlockSpec((s, LANES), lambda p: (0, p))] * 2 + [ANY_SPEC] * n_send,
        out_shape=[jax.ShapeDtypeStruct((s, ATTN_WIDTH), BF16), jax.ShapeDtypeStruct((s, ATTN_WIDTH), F32)]
        + _gathered_shapes(sends),
        scratch_shapes=[pltpu.VMEM((HEADS_PER_STEP, s, 4 * HEAD_DIM), BF16)] * 2
        + [pltpu.VMEM((HEADS_PER_STEP * s, LANES), BF16)] + _exchange_sems(n_send),
        compiler_params=_params(1),
    )(proj, proj, proj, qg, kg, *sends)
    return out[0], out[1], out[2:]


def _attn_bwd(proj, dcat, tsum, qg, kg, sends, name):
    s = proj.shape[0]
    n_blk = s // BLK
    pairs = ATTN_WIDTH // LANES
    scale = HEAD_DIM ** -0.5
    n_send = len(sends)
    n_scratch = 7

    def body(q_ref, k_ref, v_ref, do_ref, t_ref, qg_ref, kg_ref, *rest):
        send_refs, rest = rest[:n_send], rest[n_send:]
        (dq_ref, dk_ref, dv_ref, dqg_ref, dkg_ref), rest = rest[:5], rest[5:]
        got_refs, scratch, sems = rest[:n_send], rest[n_send:n_send + n_scratch], rest[n_send + n_scratch:]
        qc_s, kc_s, vd_s, kd_s, qd_s, dob_s, dkv_s = scratch
        copies = _scatter_copies(send_refs, got_refs, sems)

        @pl.when(pl.program_id(0) == 0)
        def _():
            for cp in copies:
                cp.start()

        tri_p = _tri("upto")
        tri_h = _tri("before")
        diff = _col_minus_row()

        @pl.when(pl.program_id(0) == 0)
        def _():
            dqg_ref[...] = jnp.zeros_like(dqg_ref)
            dkg_ref[...] = jnp.zeros_like(dkg_ref)

        first = _first_head_lanes()
        heads = range(HEADS_PER_STEP)
        _attn_prep(q_ref, k_ref, v_ref, qg_ref, kg_ref, qc_s, kc_s, vd_s, kd_s, n_blk)

        def prep(i, _):
            rows = pl.ds(pl.multiple_of(i * BLK, BLK), BLK)
            both = pl.ds(pl.multiple_of(i * 2 * BLK, 2 * BLK), 2 * BLK)
            dob = do_ref[rows, :].astype(BF16)
            dob_s[rows, :] = dob
            none = jnp.zeros((BLK, HEAD_DIM), BF16)
            for h in heads:
                qd_s[h, both, :] = jnp.concatenate(
                    [jnp.concatenate([qc_s[h, rows, 0:HEAD_DIM], none], axis=1),
                     jnp.concatenate([none, dob[:, h * HEAD_DIM:(h + 1) * HEAD_DIM]], axis=1)], axis=0)
                dkv_s[h, rows, :] = jnp.zeros((BLK, LANES), F32)
            return 0

        lax.fori_loop(0, n_blk, prep, 0)

        below = diff < 0

        def group(qc, qd, dob, tq, grp, pc, hc, dq, n_tiles, diagonal):
            blocks = [grp * KEY_GROUP + j for j in range(n_tiles)]
            cols_of = [pl.ds(pl.multiple_of(kb * BLK, BLK), BLK) for kb in blocks]
            both_of = [pl.ds(pl.multiple_of(kb * 2 * BLK, 2 * BLK), 2 * BLK) for kb in blocks]
            on_diagonal = [diagonal and j == n_tiles - 1 for j in range(n_tiles)]
            zs = [_pair_scores(qc[h], kc_s.at[h], grp, n_tiles) for h in heads]
            das = [_dot_nt(dob, vd_s[both, :]) for both in both_of]
            lbs = [[None] * n_tiles for _ in heads]
            scans = [[None] * n_tiles for _ in heads]
            for h in heads:
                for j, z in enumerate(zs[h]):
                    sp = _softplus(z)
                    lom = -sp
                    if on_diagonal[j]:
                        lom = jnp.where(below, lom, 0.0)
                    lbs[h][j] = z - sp
                    scans[h][j] = _scan_rows(lom, tri_p)
            pc, hc = list(pc), list(hc)
            avs = [[None] * n_tiles for _ in heads]
            gws = [[None] * n_tiles for _ in heads]
            hscans = [[None] * n_tiles for _ in heads]
            for h in heads:
                for j in range(n_tiles):
                    p_in, p_tot = scans[h][j]
                    a = jnp.exp(lbs[h][j] + (tq[h] - pc[h] - p_in))
                    if on_diagonal[j]:
                        a = jnp.where(below, a, 0.0)
                    pc[h] = pc[h] + p_tot
                    gw = das[j][:, h * BLK:(h + 1) * BLK] * a
                    avs[h][j] = a.astype(BF16)
                    gws[h][j] = gw
                    hscans[h][j] = _scan_rows(gw, tri_h)
            dzs = [[None] * n_tiles for _ in heads]
            for h in heads:
                for j in range(n_tiles):
                    h_in, g_tot = hscans[h][j]
                    gw = gws[h][j]
                    dz = gw - jnp.exp(lbs[h][j]) * (gw + hc[h] + h_in)
                    if on_diagonal[j]:
                        dz = jnp.where(below, dz, 0.0)
                    hc[h] = hc[h] + g_tot
                    dzs[h][j] = dz.astype(BF16)
            for j, both in enumerate(both_of):
                dq = dq + _dot(jnp.concatenate([dzs[h][j] for h in heads], axis=1), kd_s[both, :])
            for h in heads:
                for j, cols in enumerate(cols_of):
                    dkv_s[h, cols, :] += _dot_tn(jnp.concatenate([dzs[h][j], avs[h][j]], axis=0), qd[h])
            return tuple(pc), tuple(hc), dq

        def q_blocks(top, dqg):
            for r in range(KEY_GROUP):
                qi = top * KEY_GROUP + r
                rows = pl.ds(pl.multiple_of(qi * BLK, BLK), BLK)
                both = pl.ds(pl.multiple_of(qi * 2 * BLK, 2 * BLK), 2 * BLK)
                qc = [qc_s[h, rows, :] for h in heads]
                qd = [qd_s[h, both, :] for h in heads]
                dob = dob_s[rows, :]
                tboth = t_ref[rows, :]
                tq = [jnp.concatenate([tboth[:, h * HEAD_DIM:(h + 1) * HEAD_DIM]] * 2, axis=1) for h in heads]
                zero = (jnp.zeros((BLK, BLK), F32),) * HEADS_PER_STEP
                pc, hc, dq = lax.fori_loop(
                    0, top, lambda grp, c: group(qc, qd, dob, tq, grp, c[0], c[1], c[2], KEY_GROUP, False),
                    (zero, zero, jnp.zeros((BLK, LANES), F32)))
                _, _, dq = group(qc, qd, dob, tq, top, pc, hc, dq, r + 1, True)
                dq_raw, dg = _pair_norm_bwd(q_ref[rows, :], qg_ref[...], dq * scale, first)
                dq_ref[rows, :] = dq_raw.astype(BF16)
                dqg = dqg + dg
            return dqg

        dqg = lax.fori_loop(0, n_blk // KEY_GROUP, q_blocks, jnp.zeros((1, LANES), F32))

        def finish(i, dkg):
            rows = pl.ds(pl.multiple_of(i * BLK, BLK), BLK)
            dk = jnp.concatenate([dkv_s[h, rows, 0:HEAD_DIM] for h in heads], axis=1)
            dv = jnp.concatenate([dkv_s[h, rows, HEAD_DIM:2 * HEAD_DIM] for h in heads], axis=1)
            dk_raw, dg = _pair_norm_bwd(k_ref[rows, :], kg_ref[...], dk, first)
            dk_ref[rows, :] = dk_raw.astype(BF16)
            dv_ref[rows, :] = dv.astype(BF16)
            return dkg + dg

        dkg = lax.fori_loop(0, n_blk, finish, jnp.zeros((1, LANES), F32))
        dqg_ref[0:1, :] += dqg
        dkg_ref[0:1, :] += dkg

        @pl.when(pl.program_id(0) == pairs - 1)
        def _():
            for cp in copies:
                cp.wait()

    col = lambda off: pl.BlockSpec((s, LANES), lambda p: (0, off + p))
    vec = pl.BlockSpec((1, LANES), lambda p: (0, 0))
    small = pl.BlockSpec((8, LANES), lambda p: (0, 0))
    out = pl.pallas_call(
        body, name=name, grid=(pairs,),
        in_specs=[col(0), col(pairs), col(2 * pairs), col(0), col(0), vec, vec] + [ANY_SPEC] * n_send,
        out_specs=[col(0)] * 3 + [small] * 2 + [ANY_SPEC] * n_send,
        out_shape=[jax.ShapeDtypeStruct((s, ATTN_WIDTH), BF16)] * 3 + [jax.ShapeDtypeStruct((8, LANES), F32)] * 2
        + _scattered_shapes(sends),
        scratch_shapes=[pltpu.VMEM((HEADS_PER_STEP, s, 4 * HEAD_DIM), BF16)] * 2
        + [pltpu.VMEM((HEADS_PER_STEP * s, LANES), BF16)] * 2
        + [pltpu.VMEM((HEADS_PER_STEP, HEADS_PER_STEP * s, LANES), BF16), pltpu.VMEM((s, LANES), BF16),
           pltpu.VMEM((HEADS_PER_STEP, s, LANES), F32)] + _exchange_sems(n_send),
        compiler_params=_params(1),
    )(proj, proj, proj, dcat, tsum, qg, kg, *sends)
    return out[:5], out[5:]


CONV_ROWS = 128


def _shifted(window, shift, halo):
    if shift == 0:
        return window[halo:, :]
    return pltpu.roll(window, shift, 0)[halo:, :]


SUBLANES = 8


def _row_shifts(window, up):
    n = window.shape[0]
    return [window] + [pltpu.roll(window, n - b if up else b, 0) for b in range(1, SUBLANES)]


def _earlier(shifts, back, rows):
    a, b = divmod(back, SUBLANES)
    return shifts[b][CONV_HALO - SUBLANES * a:CONV_HALO - SUBLANES * a + rows, :]


def _later(shifts, ahead, rows):
    a, b = divmod(ahead, SUBLANES)
    return shifts[b][SUBLANES * a:SUBLANES * a + rows, :]


def _lane_blocks(width):
    return [slice(c, c + LANES) for c in range(0, width, LANES)]


def _conv_taps(shifts, w_ref, lanes, rows):
    y = None
    for k in range(CONV_KERNEL):
        term = _earlier(shifts, CONV_KERNEL - 1 - k, rows) * w_ref[k:k + 1, lanes]
        y = term if y is None else y + term
    return y


def _conv_fwd(proj, w, b, lg, lb, name):
    s = proj.shape[0]
    cw = w.shape[1]
    rows = CONV_ROWS
    blk_a = (proj.shape[1] - 2 * cw) // cw

    def body(a_ref, g_ref, w_ref, b_ref, lg_ref, lb_ref, o_ref, y_ref, u_s):
        u_s[0:CONV_HALO, :] = jnp.zeros((CONV_HALO, cw), F32)

        def glu(i, _):
            r0 = pl.multiple_of(i * rows, rows)
            u_s[pl.ds(CONV_HALO + r0, rows), :] = a_ref[pl.ds(r0, rows), :] * _sigmoid(g_ref[pl.ds(r0, rows), :])
            return 0

        lax.fori_loop(0, s // rows, glu, 0)

        def chunk(i, _):
            r0 = pl.multiple_of(i * rows, rows)
            for lanes in _lane_blocks(cw):
                shifts = _row_shifts(u_s[pl.ds(r0, CONV_HALO + rows), lanes], False)
                y_ref[pl.ds(r0, rows), lanes] = _conv_taps(shifts, w_ref, lanes, rows) + b_ref[:, lanes]
            y = y_ref[pl.ds(r0, rows), :]
            yc = y - jnp.mean(y, axis=-1, keepdims=True)
            n = yc * lax.rsqrt(jnp.mean(yc * yc, axis=-1, keepdims=True) + EPS)
            ln = n * lg_ref[...] + lb_ref[...]
            o_ref[pl.ds(r0, rows), :] = (ln * _sigmoid(ln)).astype(BF16)
            return 0

        lax.fori_loop(0, s // rows, chunk, 0)

    vec = pl.BlockSpec((1, cw), lambda i: (0, 0))
    return pl.pallas_call(
        body, name=name, grid=(1,),
        in_specs=[pl.BlockSpec((s, cw), lambda i: (0, blk_a)), pl.BlockSpec((s, cw), lambda i: (0, blk_a + 1)),
                  pl.BlockSpec((CONV_KERNEL, cw), lambda i: (0, 0)), vec, vec, vec],
        out_specs=[pl.BlockSpec((s, cw), lambda i: (0, 0))] * 2,
        out_shape=[jax.ShapeDtypeStruct((s, cw), BF16), jax.ShapeDtypeStruct((s, cw), F32)],
        scratch_shapes=[pltpu.VMEM((CONV_HALO + s, cw), F32)],
        compiler_params=_params(1),
    )(proj, proj, w, b, lg, lb)


def _conv_bwd(proj, y, dcat, w, lg, lb, sends, name):
    s = proj.shape[0]
    cw = w.shape[1]
    rows = CONV_ROWS
    blk_a = (proj.shape[1] - 2 * cw) // cw
    n_chunk = s // rows
    n_send = len(sends)

    def body(a_ref, g_ref, y_ref, dc_ref, w_ref, lg_ref, lb_ref, *rest):
        send_refs, (o_ref, dw_ref, db_ref, dlg_ref, dlb_ref), rest = rest[:n_send], rest[n_send:n_send + 5], rest[n_send + 5:]
        got_refs, (u_s, dy_s, dw_s), sems = rest[:n_send], rest[n_send:n_send + 3], rest[n_send + 3:]
        copies = _scatter_copies(send_refs, got_refs, sems)
        for cp in copies:
            cp.start()
        u_s[0:CONV_HALO, :] = jnp.zeros((CONV_HALO, cw), F32)
        dy_s[pl.ds(s, CONV_HALO), :] = jnp.zeros((CONV_HALO, cw), F32)
        dw_s[...] = jnp.zeros_like(dw_s)

        def glu(i, _):
            r0 = pl.multiple_of(i * rows, rows)
            u_s[pl.ds(CONV_HALO + r0, rows), :] = a_ref[pl.ds(r0, rows), :] * _sigmoid(g_ref[pl.ds(r0, rows), :])
            return 0

        lax.fori_loop(0, n_chunk, glu, 0)

        def chunk(i, carry):
            db, dlg, dlb = carry
            r0 = pl.multiple_of(i * rows, rows)
            y = y_ref[pl.ds(r0, rows), :]
            yc = y - jnp.mean(y, axis=-1, keepdims=True)
            r = lax.rsqrt(jnp.mean(yc * yc, axis=-1, keepdims=True) + EPS)
            n = yc * r
            ln = n * lg_ref[...] + lb_ref[...]
            sg = _sigmoid(ln)
            dln = dc_ref[pl.ds(r0, rows), :] * (sg * (1.0 + ln * (1.0 - sg)))
            dn = dln * lg_ref[...]
            dy = r * (dn - jnp.mean(dn, axis=-1, keepdims=True) - n * jnp.mean(dn * n, axis=-1, keepdims=True))
            dy_s[pl.ds(r0, rows), :] = dy
            for lanes in _lane_blocks(cw):
                shifts = _row_shifts(u_s[pl.ds(r0, CONV_HALO + rows), lanes], False)
                dy_part = dy[:, lanes]
                for k in range(CONV_KERNEL):
                    prod = _earlier(shifts, CONV_KERNEL - 1 - k, rows) * dy_part
                    dw_s[k, :, lanes] += jnp.sum(prod.reshape(rows // SUBLANES, SUBLANES, LANES), axis=0)
            return (db + jnp.sum(dy, axis=0, keepdims=True),
                    dlg + jnp.sum(dln * n, axis=0, keepdims=True),
                    dlb + jnp.sum(dln, axis=0, keepdims=True))

        zero = jnp.zeros((1, cw), F32)
        db, dlg, dlb = lax.fori_loop(0, n_chunk, chunk, (zero, zero, zero))
        db_ref[...] = db
        dlg_ref[...] = dlg
        dlb_ref[...] = dlb
        for k in range(CONV_KERNEL):
            dw_ref[k:k + 1, :] = jnp.sum(dw_s[k], axis=0, keepdims=True)

        def back(i, _):
            r0 = pl.multiple_of(i * rows, rows)
            for lanes in _lane_blocks(cw):
                shifts = _row_shifts(dy_s[pl.ds(r0, rows + CONV_HALO), lanes], True)
                du = None
                for k in range(CONV_KERNEL):
                    term = _later(shifts, CONV_KERNEL - 1 - k, rows) * w_ref[k:k + 1, lanes]
                    du = term if du is None else du + term
                av = a_ref[pl.ds(r0, rows), lanes]
                sg = _sigmoid(g_ref[pl.ds(r0, rows), lanes])
                o_ref[pl.ds(r0, rows), lanes] = (du * sg).astype(BF16)
                o_ref[pl.ds(r0, rows), slice(cw + lanes.start, cw + lanes.stop)] = (du * av * sg * (1.0 - sg)).astype(BF16)
            return 0

        lax.fori_loop(0, n_chunk, back, 0)
        for cp in copies:
            cp.wait()

    vec = pl.BlockSpec((1, cw), lambda i: (0, 0))
    wspec = pl.BlockSpec((CONV_KERNEL, cw), lambda i: (0, 0))
    out = pl.pallas_call(
        body, name=name, grid=(1,),
        in_specs=[pl.BlockSpec((s, cw), lambda i: (0, blk_a)), pl.BlockSpec((s, cw), lambda i: (0, blk_a + 1)),
                  pl.BlockSpec((s, cw), lambda i: (0, 0)), pl.BlockSpec((s, cw), lambda i: (0, 1)), wspec, vec, vec]
        + [ANY_SPEC] * n_send,
        out_specs=[pl.BlockSpec((s, 2 * cw), lambda i: (0, 0)), wspec, vec, vec, vec] + [ANY_SPEC] * n_send,
        out_shape=[jax.ShapeDtypeStruct((s, 2 * cw), BF16), jax.ShapeDtypeStruct((CONV_KERNEL, cw), F32)]
        + [jax.ShapeDtypeStruct((1, cw), F32)] * 3 + _scattered_shapes(sends),
        scratch_shapes=[pltpu.VMEM((CONV_HALO + s, cw), F32), pltpu.VMEM((s + CONV_HALO, cw), F32),
                        pltpu.VMEM((CONV_KERNEL, 8, cw), F32)] + _exchange_sems(n_send),
        compiler_params=_params(1),
    )(proj, proj, y, dcat, w, lg, lb, *sends)
    return out[:5], out[5:]


FFN_ROWS = 256


def _ffn_gate(g_ref, r0, rows, w_ref, b_ref):
    cur = g_ref[pl.ds(r0, rows), :].astype(F32)
    prev = g_ref[pl.ds(pl.multiple_of(jnp.maximum(r0 - FFN_HALO, 0), FFN_HALO), FFN_HALO), :].astype(F32)
    prev = jnp.where(r0 > 0, prev, 0.0)
    window = jnp.concatenate([prev, cur], axis=0)
    gc = cur * w_ref[FFN_KERNEL - 1:FFN_KERNEL, :] + b_ref[...]
    for k in range(FFN_KERNEL - 1):
        gc = gc + _shifted(window, FFN_KERNEL - 1 - k, FFN_HALO) * w_ref[k:k + 1, :]
    return gc, window


def _ffn_fwd(up, w, b, name):
    s = up.shape[0]
    f = w.shape[1]
    tc = _pick(f, (256, 128))
    nc = f // tc
    rows = _pick(s, (FFN_ROWS, 128))

    def body(g_ref, v_ref, w_ref, b_ref, o_ref):
        def chunk(i, _):
            r0 = pl.multiple_of(i * rows, rows)
            gc, _w = _ffn_gate(g_ref, r0, rows, w_ref, b_ref)
            o_ref[pl.ds(r0, rows), :] = (gc * _sigmoid(gc) * v_ref[pl.ds(r0, rows), :].astype(F32)).astype(BF16)
            return 0

        lax.fori_loop(0, s // rows, chunk, 0)

    return pl.pallas_call(
        body, name=name, grid=(nc,),
        in_specs=[pl.BlockSpec((s, tc), lambda j: (0, j)), pl.BlockSpec((s, tc), lambda j: (0, nc + j)),
                  pl.BlockSpec((FFN_KERNEL, tc), lambda j: (0, j)), pl.BlockSpec((1, tc), lambda j: (0, j))],
        out_specs=pl.BlockSpec((s, tc), lambda j: (0, j)),
        out_shape=jax.ShapeDtypeStruct((s, f), BF16),
        compiler_params=_params(1),
    )(up, up, w, b)


def _ffn_bwd(up, dact, w, b, name):
    s = up.shape[0]
    f = w.shape[1]
    tc = _pick(f, (256, 128))
    nc = f // tc
    rows = _pick(s, (FFN_ROWS, 128))
    n_chunk = s // rows

    def body(g_ref, v_ref, da_ref, w_ref, b_ref, d_ref, dw_ref, db_ref, dgc_s):
        dg_ref, dv_ref = d_ref.at[0], d_ref.at[1]
        dgc_s[pl.ds(s, FFN_HALO), :] = jnp.zeros((FFN_HALO, tc), F32)

        def chunk(i, carry):
            r0 = pl.multiple_of(i * rows, rows)
            gc, window = _ffn_gate(g_ref, r0, rows, w_ref, b_ref)
            sg = _sigmoid(gc)
            da = da_ref[pl.ds(r0, rows), :].astype(F32)
            dv_ref[pl.ds(r0, rows), :] = (da * gc * sg).astype(BF16)
            dgc = da * v_ref[pl.ds(r0, rows), :].astype(F32) * (sg * (1.0 + gc * (1.0 - sg)))
            dgc_s[pl.ds(r0, rows), :] = dgc
            out = [carry[0] + jnp.sum(dgc, axis=0, keepdims=True)]
            for k in range(FFN_KERNEL):
                out.append(carry[1 + k] + jnp.sum(_shifted(window, FFN_KERNEL - 1 - k, FFN_HALO) * dgc,
                                                  axis=0, keepdims=True))
            return tuple(out)

        zero = jnp.zeros((1, tc), F32)
        sums = lax.fori_loop(0, n_chunk, chunk, (zero,) * (1 + FFN_KERNEL))
        db_ref[...] = sums[0]
        for k in range(FFN_KERNEL):
            dw_ref[k:k + 1, :] = sums[1 + k]

        def back(i, _):
            r0 = pl.multiple_of(i * rows, rows)
            window = dgc_s[pl.ds(r0, rows + FFN_HALO), :]
            dg = window[:rows, :] * w_ref[FFN_KERNEL - 1:FFN_KERNEL, :]
            for k in range(FFN_KERNEL - 1):
                sh = FFN_KERNEL - 1 - k
                dg = dg + pltpu.roll(window, rows + FFN_HALO - sh, 0)[:rows, :] * w_ref[k:k + 1, :]
            dg_ref[pl.ds(r0, rows), :] = dg.astype(BF16)
            return 0

        lax.fori_loop(0, n_chunk, back, 0)

    blk = lambda off: pl.BlockSpec((s, tc), lambda j: (0, off + j))
    return pl.pallas_call(
        body, name=name, grid=(nc,),
        in_specs=[blk(0), blk(nc), blk(0), pl.BlockSpec((FFN_KERNEL, tc), lambda j: (0, j)),
                  pl.BlockSpec((1, tc), lambda j: (0, j))],
        out_specs=[pl.BlockSpec((2, s, tc), lambda j: (0, 0, j)), pl.BlockSpec((FFN_KERNEL, tc), lambda j: (0, j)),
                   pl.BlockSpec((1, tc), lambda j: (0, j))],
        out_shape=[jax.ShapeDtypeStruct((2, s, f), BF16),
                   jax.ShapeDtypeStruct((FFN_KERNEL, f), F32), jax.ShapeDtypeStruct((1, f), F32)],
        scratch_shapes=[pltpu.VMEM((s + FFN_HALO, tc), F32)],
        compiler_params=_params(1),
    )(up, up, dact, w, b)


def _loss_head(y, target, name):
    m, n = y.shape
    tm = _pick(m, (256, 128))

    def body(y_ref, t_ref, l_ref, d_ref, db_ref):
        e = y_ref[...] - t_ref[...]
        part = 0.5 * jnp.sum(jnp.sum(e * e, axis=-1, keepdims=True) / n, axis=0, keepdims=True)

        @pl.when(pl.program_id(0) == 0)
        def _():
            l_ref[...] = jnp.zeros_like(l_ref)

        l_ref[...] += part
        d = e / n
        d_ref[...] = d
        db_ref[...] = d.astype(BF16)

    return pl.pallas_call(
        body, name=name, grid=(m // tm,),
        in_specs=[pl.BlockSpec((tm, n), lambda i: (i, 0))] * 2,
        out_specs=[pl.BlockSpec((8, LANES), lambda i: (0, 0)), pl.BlockSpec((tm, n), lambda i: (i, 0)),
                   pl.BlockSpec((tm, n), lambda i: (i, 0))],
        out_shape=[jax.ShapeDtypeStruct((8, LANES), F32), jax.ShapeDtypeStruct((m, n), F32),
                   jax.ShapeDtypeStruct((m, n), BF16)],
        compiler_params=_params(1),
    )(y, target)


def _adamw_math(w, g, m, v):
    m = ADAM_B1 * m + (1.0 - ADAM_B1) * g
    v = ADAM_B2 * v + (1.0 - ADAM_B2) * (g * g)
    m_hat = m / (1.0 - ADAM_B1 ** ADAM_STEP)
    v_hat = v / (1.0 - ADAM_B2 ** ADAM_STEP)
    delta = -ADAM_LR * (m_hat / (jnp.sqrt(v_hat) + ADAM_EPS) + ADAM_WD * w)
    return delta, m, v


def _sum_adamw(parts, w, m, v, sends, name):
    depth, r, c = w.shape
    tr = max(t for t in range(16, min(r, 192) + 1, 16) if r % t == 0)
    steps = r // tr
    n_send = len(sends)

    def body(*refs):
        p_refs, refs = refs[:depth], refs[depth:]
        (w_ref, m_ref, v_ref), send_refs, refs = refs[:3], refs[3:3 + n_send], refs[3 + n_send:]
        (g_out, d_out, m_out, v_out), got_refs, sems = refs[:4], refs[4:4 + n_send], refs[4 + n_send:]
        is_first = (pl.program_id(0) == 0) & (pl.program_id(1) == 0)
        is_last = (pl.program_id(0) == depth - 1) & (pl.program_id(1) == steps - 1)
        if n_send:
            start, relay, finish = _two_level_gather(send_refs, got_refs, sems)
            pl.when(is_first)(start)
            pl.when(is_last)(relay)
        for layer in range(depth):
            @pl.when(pl.program_id(0) == layer)
            def _():
                g = p_refs[layer][0].astype(F32)
                for src in range(1, N_DEV):
                    g = g + p_refs[layer][src].astype(F32)
                d, mn, vn = _adamw_math(w_ref[0], g, m_ref[0], v_ref[0])
                g_out[0] = g
                d_out[0] = d
                m_out[0] = mn
                v_out[0] = vn

        if n_send:
            pl.when(is_last)(finish)

    def part_spec(layer):
        return pl.BlockSpec((N_DEV, tr, c), lambda l, i: (0, jnp.clip((l - layer) * steps + i, 0, steps - 1), 0))

    blk = pl.BlockSpec((1, tr, c), lambda l, i: (l, i, 0))
    out = pl.pallas_call(
        body, name=name, grid=(depth, steps),
        in_specs=[part_spec(layer) for layer in range(depth)] + [blk, blk, blk] + [ANY_SPEC] * n_send,
        out_specs=[blk] * 4 + [ANY_SPEC] * n_send,
        out_shape=[jax.ShapeDtypeStruct(w.shape, F32)] * 4 + _gathered_shapes(sends),
        scratch_shapes=_exchange_sems(n_send) if n_send else [],
        compiler_params=_params(2),
    )(*parts, w, m, v, *sends)
    return out[:4], out[4:]


VMEM_SPEC = pl.BlockSpec(memory_space=pltpu.VMEM)


def _sum_small(parts, name):
    n = len(parts)

    def body(*refs):
        for p_ref, o_ref in zip(refs[:n], refs[n:]):
            g = p_ref[0]
            for src in range(1, N_DEV):
                g = g + p_ref[src]
            o_ref[...] = g

    return pl.pallas_call(
        body, name=name, in_specs=[VMEM_SPEC] * n, out_specs=[VMEM_SPEC] * n,
        out_shape=[jax.ShapeDtypeStruct(p.shape[1:], F32) for p in parts],
        compiler_params=_params(),
    )(*parts)


def _adamw_small(ws, gs, ms, vs, name):
    n = len(ws)

    def body(*refs):
        ins, outs = refs[:4 * n], refs[4 * n:]
        for i in range(n):
            d, mn, vn = _adamw_math(ins[i][...], ins[n + i][...], ins[2 * n + i][...], ins[3 * n + i][...])
            outs[i][...] = d
            outs[n + i][...] = mn
            outs[2 * n + i][...] = vn

    out = pl.pallas_call(
        body, name=name, in_specs=[VMEM_SPEC] * (4 * n), out_specs=[VMEM_SPEC] * (3 * n),
        out_shape=[jax.ShapeDtypeStruct(w.shape, F32) for w in ws] * 3,
        compiler_params=_params(),
    )(*ws, *gs, *ms, *vs)
    return out[:n], out[n:2 * n], out[2 * n:]


def kernel(x, norm1_g, w_in, q_norm_g, k_norm_g, conv_dw_w, conv_dw_b, conv_ln_g, conv_ln_b, w_out, norm2_g, w_up, ffn_dw_w, ffn_dw_b, w_down, loss_target, m_norm1_g, m_w_in, m_q_norm_g, m_k_norm_g, m_conv_dw_w, m_conv_dw_b, m_conv_ln_g, m_conv_ln_b, m_w_out, m_norm2_g, m_w_up, m_ffn_dw_w, m_ffn_dw_b, m_w_down, v_norm1_g, v_w_in, v_q_norm_g, v_k_norm_g, v_conv_dw_w, v_conv_dw_b, v_conv_ln_g, v_conv_ln_b, v_w_out, v_norm2_g, v_w_up, v_ffn_dw_w, v_ffn_dw_b, v_w_down):
    depth = w_in.shape[0]
    d_ff = w_down.shape[1] * N_DEV
    conv_w = conv_dw_b.shape[1]
    cw_shard = conv_dw_w.shape[2]
    fw_shard = ffn_dw_w.shape[2]
    me = 4 * lax.axis_index("x") + 2 * lax.axis_index("y") + lax.axis_index("c")

    transposed = lambda a: a.transpose(0, 2, 1)
    b_in, b_out, b_up, b_down = (transposed(w_in).astype(BF16), w_out.astype(BF16), transposed(w_up).astype(BF16),
                                 w_down.astype(BF16))
    rows_major = lambda g: g.reshape(N_DEV * g.shape[1], g.shape[2])
    g_in0, g_cw, g_fw = _gather([b_in[0], conv_dw_w, ffn_dw_w], name="gather_first")
    wf_in, wf_out, wf_up, wf_down = [rows_major(g_in0)] + [None] * (depth - 1), [None] * depth, [None] * depth, [None] * depth
    cwf = g_cw.transpose(1, 2, 0, 3).reshape(depth, CONV_KERNEL, conv_w)
    fwf = g_fw.transpose(1, 2, 0, 3).reshape(depth, FFN_KERNEL, d_ff)

    row = lambda a, l: a[l].reshape(1, -1)
    both_heads = lambda a, l: jnp.tile(row(a, l), (1, HEADS_PER_STEP))

    xs = x[0]
    saved = []
    for l in range(depth):
        proj, h1, _ = _mm_rms(xs, row(norm1_g, l), wf_in[l], F32, [], name="fwd_in")
        sends = [b_out[l], b_up[l]] + ([b_in[l + 1]] if l + 1 < depth else [])
        attn, tsum, got = _attn_fwd(proj, both_heads(q_norm_g, l), both_heads(k_norm_g, l), sends, name="fwd_attn")
        wf_out[l], wf_up[l] = rows_major(got[0]), rows_major(got[1])
        if l + 1 < depth:
            wf_in[l + 1] = rows_major(got[2])
        conv, conv_y = _conv_fwd(proj, cwf[l], row(conv_dw_b, l), row(conv_ln_g, l), row(conv_ln_b, l),
                                 name="fwd_conv")
        cat = jnp.concatenate([attn, conv], axis=1)
        x_mid = _mm_res(cat, wf_out[l], xs, name="fwd_out")
        up, h2, got = _mm_rms(x_mid, row(norm2_g, l), wf_up[l], BF16, [b_down[l]], name="fwd_up")
        wf_down[l] = rows_major(got[0])
        act = _ffn_fwd(up, fwf[l], row(ffn_dw_b, l), name="fwd_ffn")
        x_next = _mm_res(act, wf_down[l], x_mid, name="fwd_down")
        saved.append((xs, h1, proj, tsum, cat, x_mid, h2, up, act, conv_y))
        xs = x_next

    loss_tile, dx, dxb = _loss_head(xs, loss_target[0], name="loss_head")
    loss = lax.psum(loss_tile[0, 0], ("x", "y", "c"))

    r_in, r_out, r_up, r_down = [None] * depth, [None] * depth, [None] * depth, [None] * depth
    row_blocks = lambda g: g.reshape(N_DEV, g.shape[0] // N_DEV, g.shape[1])
    small = {k: [None] * depth for k in ("norm1_g", "q_norm_g", "k_norm_g", "conv_dw_w", "conv_dw_b", "conv_ln_g",
                                         "conv_ln_b", "norm2_g", "ffn_dw_w", "ffn_dw_b")}
    gw_in = None
    for l in reversed(range(depth)):
        xs, h1, proj, tsum, cat, x_mid, h2, up, act, conv_y = saved[l]
        dact = _mm_nt(dxb, wf_down[l], BF16, name="bwd_dact")
        gw_down = _mm_tn(act, dxb, name="bwd_gw_down")
        dup, small["ffn_dw_w"][l], small["ffn_dw_b"][l] = _ffn_bwd(up, dact, fwf[l], row(ffn_dw_b, l), name="bwd_ffn")
        gw_up = _mm_tn(dup, h2, name="bwd_gw_up")
        sends = [row_blocks(gw_in)] if l + 1 < depth else []
        dx, dxb, small["norm2_g"][l], got = _mm_rmsbwd(dup, wf_up[l], x_mid, row(norm2_g, l), dx, sends, name="bwd_up")
        if l + 1 < depth:
            r_in[l + 1] = got[0]
        dcat = _mm_nt(dxb, wf_out[l], F32, name="bwd_dcat")
        gw_out = _mm_tn(cat, dxb, name="bwd_gw_out")
        (dglu, small["conv_dw_w"][l], small["conv_dw_b"][l], small["conv_ln_g"][l], small["conv_ln_b"][l]), got = (
            _conv_bwd(proj, conv_y, dcat, cwf[l], row(conv_ln_g, l), row(conv_ln_b, l), [row_blocks(gw_out)],
                      name="bwd_conv"))
        r_out[l] = got[0]
        (dq, dk, dv, dqg, dkg), got = _attn_bwd(proj, dcat, tsum, both_heads(q_norm_g, l), both_heads(k_norm_g, l),
                                                [row_blocks(gw_down), row_blocks(gw_up)], name="bwd_attn")
        r_down[l], r_up[l] = got
        small["q_norm_g"][l] = dqg[0:1, :HEAD_DIM] + dqg[0:1, HEAD_DIM:]
        small["k_norm_g"][l] = dkg[0:1, :HEAD_DIM] + dkg[0:1, HEAD_DIM:]
        dproj = jnp.concatenate([dq, dk, dv, dglu], axis=1)
        gw_in = _mm_tn(dproj, h1, name="bwd_gw_in")
        sends = [row_blocks(gw_in)] if l == 0 else []
        dx, dxb, small["norm1_g"][l], got = _mm_rmsbwd(dproj, wf_in[l], xs, row(norm1_g, l), dx, sends, name="bwd_in")
        if l == 0:
            r_in[0] = got[0]
    grad_x = dx[None]

    names = ["norm1_g", "q_norm_g", "k_norm_g", "conv_dw_w", "conv_dw_b", "conv_ln_g", "conv_ln_b", "norm2_g",
             "ffn_dw_w", "ffn_dw_b"]
    full_shapes = {"norm1_g": norm1_g.shape, "q_norm_g": q_norm_g.shape, "k_norm_g": k_norm_g.shape,
                   "conv_dw_w": (depth, CONV_KERNEL, conv_w), "conv_dw_b": conv_dw_b.shape,
                   "conv_ln_g": conv_ln_g.shape, "conv_ln_b": conv_ln_b.shape, "norm2_g": norm2_g.shape,
                   "ffn_dw_w": (depth, FFN_KERNEL, d_ff), "ffn_dw_b": ffn_dw_b.shape}
    partial = [jnp.stack(small[k]).reshape(full_shapes[k]) for k in names]
    big = {}
    big["w_out"], all_partials = _sum_adamw(r_out, w_out, m_w_out, v_w_out, partial, name="adamw_out")
    big["w_up"], _ = _sum_adamw(r_up, transposed(w_up), transposed(m_w_up), transposed(v_w_up), [], name="adamw_up")
    big["w_down"], _ = _sum_adamw(r_down, w_down, m_w_down, v_w_down, [], name="adamw_down")
    big["w_in"], _ = _sum_adamw(r_in, transposed(w_in), transposed(m_w_in), transposed(v_w_in), [], name="adamw_in")
    for k in ("w_in", "w_up"):
        big[k] = [transposed(a) for a in big[k]]

    grads = dict(zip(names, _sum_small(all_partials, name="sum_small_grads")))
    grads["conv_dw_w"] = lax.dynamic_slice_in_dim(grads["conv_dw_w"], me * cw_shard, cw_shard, axis=2)
    grads["ffn_dw_w"] = lax.dynamic_slice_in_dim(grads["ffn_dw_w"], me * fw_shard, fw_shard, axis=2)
    weights = dict(norm1_g=norm1_g, q_norm_g=q_norm_g, k_norm_g=k_norm_g, conv_dw_w=conv_dw_w, conv_dw_b=conv_dw_b,
                   conv_ln_g=conv_ln_g, conv_ln_b=conv_ln_b, norm2_g=norm2_g, ffn_dw_w=ffn_dw_w, ffn_dw_b=ffn_dw_b)
    m_in = dict(norm1_g=m_norm1_g, q_norm_g=m_q_norm_g, k_norm_g=m_k_norm_g, conv_dw_w=m_conv_dw_w,
                conv_dw_b=m_conv_dw_b, conv_ln_g=m_conv_ln_g, conv_ln_b=m_conv_ln_b, norm2_g=m_norm2_g,
                ffn_dw_w=m_ffn_dw_w, ffn_dw_b=m_ffn_dw_b)
    v_in = dict(norm1_g=v_norm1_g, q_norm_g=v_q_norm_g, k_norm_g=v_k_norm_g, conv_dw_w=v_conv_dw_w,
                conv_dw_b=v_conv_dw_b, conv_ln_g=v_conv_ln_g, conv_ln_b=v_conv_ln_b, norm2_g=v_norm2_g,
                ffn_dw_w=v_ffn_dw_w, ffn_dw_b=v_ffn_dw_b)
    d_s, m_s, v_s = _adamw_small([weights[k] for k in names], [grads[k] for k in names], [m_in[k] for k in names],
                                 [v_in[k] for k in names], name="adamw_small")
    delta, new_m, new_v = dict(zip(names, d_s)), dict(zip(names, m_s)), dict(zip(names, v_s))
    for k, (g, d, mn, vn) in big.items():
        grads[k], delta[k], new_m[k], new_v[k] = g, d, mn, vn

    order = ["norm1_g", "w_in", "q_norm_g", "k_norm_g", "conv_dw_w", "conv_dw_b", "conv_ln_g", "conv_ln_b", "w_out",
             "norm2_g", "w_up", "ffn_dw_w", "ffn_dw_b", "w_down"]
    return (loss, grad_x, *[grads[k] for k in order], *[delta[k] for k in order], *[new_m[k] for k in order],
            *[new_v[k] for k in order])
```

```python
import jax
import jax.numpy as jnp
from jax import lax
from jax.experimental import pallas as pl
from jax.experimental.pallas import tpu as pltpu

F32 = jnp.float32
BF16 = jnp.bfloat16

N_DEV = 8
HEADS = 8
HEAD_DIM = 64
ATTN_WIDTH = HEADS * HEAD_DIM
CONV_KERNEL = 31
FFN_KERNEL = 3
EPS = 1e-6
BLK = 128
KEY_GROUP = 4
LANES = 128
NORM_ROWS = 128
CONV_HALO = 32
FFN_HALO = 16

ADAM_LR = 0.001
ADAM_B1 = 0.9
ADAM_B2 = 0.999
ADAM_EPS = 1e-08
ADAM_WD = 0.01
ADAM_STEP = 10

VMEM_LIMIT = 56 * 1024 * 1024


def _params(n_axes=0):
    kw = dict(vmem_limit_bytes=VMEM_LIMIT)
    if n_axes:
        kw["dimension_semantics"] = ("arbitrary",) * n_axes
    return pltpu.CompilerParams(**kw)


def _dot(a, b):
    return jnp.dot(a, b, preferred_element_type=F32)


def _dot_nt(a, b):
    return lax.dot_general(a, b, (((1,), (1,)), ((), ())), preferred_element_type=F32)


def _dot_tn(a, b):
    return lax.dot_general(a, b, (((0,), (0,)), ((), ())), preferred_element_type=F32)


def _sigmoid(x):
    return 1.0 / (1.0 + jnp.exp(-x))


def _split_bf16(x):
    hi = x.astype(BF16)
    lo = (x - hi.astype(F32)).astype(BF16)
    return hi, lo


def _pick(n, options):
    for t in options:
        if n % t == 0:
            return t
    return n


def _tile(n, cap):
    best = None
    for t in range(LANES, min(n, cap) + 1, LANES):
        if n % t == 0:
            best = t
    return best or n


def _mm_rms(x, g, wt, out_dtype, sends, name):
    m, k = x.shape
    n = wt.shape[0]
    tm = _tile(m, 2048)
    tn = _tile(n, 512)
    n_send = len(sends)
    grid = (m // tm, n // tn)

    def body(x_ref, g_ref, w_ref, *rest):
        send_refs, (o_ref, h_ref), rest = rest[:n_send], rest[n_send:n_send + 2], rest[n_send + 2:]
        got_refs, h_s, sems = rest[:n_send], rest[n_send], rest[n_send + 1:]
        if n_send:
            start, relay, finish = _two_level_gather(send_refs, got_refs, sems)
            is_first = (pl.program_id(0) == 0) & (pl.program_id(1) == 0)
            is_last = (pl.program_id(0) == grid[0] - 1) & (pl.program_id(1) == grid[1] - 1)
            pl.when(is_first)(start)
            pl.when(is_last)(relay)

        @pl.when(pl.program_id(1) == 0)
        def _():
            def chunk(c, _):
                rows = pl.ds(pl.multiple_of(c * NORM_ROWS, NORM_ROWS), NORM_ROWS)
                xv = x_ref[rows, :]
                r = lax.rsqrt(jnp.mean(xv * xv, axis=-1, keepdims=True) + EPS)
                hv = (xv * r * g_ref[...]).astype(BF16)
                h_s[rows, :] = hv
                h_ref[rows, :] = hv
                return 0

            lax.fori_loop(0, tm // NORM_ROWS, chunk, 0)

        o_ref[...] = _dot_nt(h_s[...], w_ref[...]).astype(out_dtype)
        if n_send:
            pl.when(is_last)(finish)

    out = pl.pallas_call(
        body, name=name, grid=grid,
        in_specs=[pl.BlockSpec((tm, k), lambda i, j: (i, 0)),
                  pl.BlockSpec((1, k), lambda i, j: (0, 0)),
                  pl.BlockSpec((tn, k), lambda i, j: (j, 0))] + [ANY_SPEC] * n_send,
        out_specs=[pl.BlockSpec((tm, tn), lambda i, j: (i, j)),
                   pl.BlockSpec((tm, k), lambda i, j: (i, 0))] + [ANY_SPEC] * n_send,
        out_shape=[jax.ShapeDtypeStruct((m, n), out_dtype), jax.ShapeDtypeStruct((m, k), BF16)]
        + _gathered_shapes(sends),
        scratch_shapes=[pltpu.VMEM((tm, k), BF16)] + (_exchange_sems(n_send) if n_send else []),
        compiler_params=_params(2),
    )(x, g, wt, *sends)
    return out[0], out[1], out[2:]


def _mm_res(a, w, res, name):
    m, k = a.shape
    n = w.shape[1]
    tm = _tile(m, 1024)
    tn = _tile(n, 512)

    def body(a_ref, w_ref, r_ref, o_ref):
        o_ref[...] = r_ref[...] + _dot(a_ref[...], w_ref[...])

    return pl.pallas_call(
        body, name=name, grid=(m // tm, n // tn),
        in_specs=[pl.BlockSpec((tm, k), lambda i, j: (i, 0)),
                  pl.BlockSpec((k, tn), lambda i, j: (0, j)),
                  pl.BlockSpec((tm, tn), lambda i, j: (i, j))],
        out_specs=pl.BlockSpec((tm, tn), lambda i, j: (i, j)),
        out_shape=jax.ShapeDtypeStruct((m, n), F32),
        compiler_params=_params(2),
    )(a, w, res)


def _mm_nt(a, w, out_dtype, name):
    m, k = a.shape
    n = w.shape[0]
    tm = _tile(m, 1024)
    tn = _tile(n, 1408)

    def body(a_ref, w_ref, o_ref):
        o_ref[...] = _dot_nt(a_ref[...], w_ref[...]).astype(out_dtype)

    return pl.pallas_call(
        body, name=name, grid=(m // tm, n // tn),
        in_specs=[pl.BlockSpec((tm, k), lambda i, j: (i, 0)),
                  pl.BlockSpec((tn, k), lambda i, j: (j, 0))],
        out_specs=pl.BlockSpec((tm, tn), lambda i, j: (i, j)),
        out_shape=jax.ShapeDtypeStruct((m, n), out_dtype),
        compiler_params=_params(2),
    )(a, w)


def _column_tiles(a, cap):
    if a.ndim == 2:
        s, c = a.shape
        tc = _tile(c, cap)
        return s, c, tc, lambda rows, index: pl.BlockSpec((rows, tc), lambda *g: (index(*g)[0], index(*g)[1]))
    slabs, s, width = a.shape
    tc = _tile(width, cap)
    per = width // tc
    return s, slabs * width, tc, lambda rows, index: pl.BlockSpec(
        (None, rows, tc), lambda *g: (index(*g)[1] // per, index(*g)[0], index(*g)[1] % per))


def _mm_tn(a, b, name):
    s, m, tm, a_spec = _column_tiles(a, 1408)
    n = b.shape[1]
    tn = _tile(n, 1024)

    def body(a_ref, b_ref, o_ref):
        o_ref[...] = _dot_tn(a_ref[...], b_ref[...]).astype(BF16)

    return pl.pallas_call(
        body, name=name, grid=(m // tm, n // tn),
        in_specs=[a_spec(s, lambda i, j: (0, i)),
                  pl.BlockSpec((s, tn), lambda i, j: (0, j))],
        out_specs=pl.BlockSpec((tm, tn), lambda i, j: (i, j)),
        out_shape=jax.ShapeDtypeStruct((m, n), BF16),
        compiler_params=_params(2),
    )(a, b)


def _mm_rmsbwd(a, w, x, g, dres, sends, name):
    m, k, tk, a_spec = _column_tiles(a, 1408)
    n = w.shape[1]
    tm = _tile(m, 1024)
    nk = k // tk
    n_send = len(sends)

    def body(a_ref, w_ref, x_ref, g_ref, r_ref, *rest):
        send_refs, (dx_ref, dxb_ref, dg_ref), rest = rest[:n_send], rest[n_send:n_send + 3], rest[n_send + 3:]
        got_refs, acc, sems = rest[:n_send], rest[n_send], rest[n_send + 1:]
        i, kk = pl.program_id(0), pl.program_id(1)
        if n_send:
            copies = _scatter_copies(send_refs, got_refs, sems)

            @pl.when((i == 0) & (kk == 0))
            def _():
                for cp in copies:
                    cp.start()

        part = _dot(a_ref[...], w_ref[...])

        @pl.when(kk == 0)
        def _():
            acc[...] = part

        @pl.when(kk > 0)
        def _():
            acc[...] += part

        @pl.when(kk == nk - 1)
        def _():
            def chunk(c, dgp):
                rows = pl.ds(pl.multiple_of(c * NORM_ROWS, NORM_ROWS), NORM_ROWS)
                dh = acc[rows, :]
                xv = x_ref[rows, :]
                r = lax.rsqrt(jnp.mean(xv * xv, axis=-1, keepdims=True) + EPS)
                xh = xv * r
                dxh = dh * g_ref[...]
                dx = r_ref[rows, :] + r * (dxh - xh * jnp.mean(dxh * xh, axis=-1, keepdims=True))
                dx_ref[rows, :] = dx
                dxb_ref[rows, :] = dx.astype(BF16)
                return dgp + jnp.sum(dh * xh, axis=0, keepdims=True)

            dgp = lax.fori_loop(0, tm // NORM_ROWS, chunk, jnp.zeros((1, n), F32))

            @pl.when(i == 0)
            def _():
                dg_ref[...] = dgp

            @pl.when(i > 0)
            def _():
                dg_ref[...] += dgp

        if n_send:
            @pl.when((i == m // tm - 1) & (kk == nk - 1))
            def _():
                for cp in copies:
                    cp.wait()

    out = pl.pallas_call(
        body, name=name, grid=(m // tm, nk),
        in_specs=[a_spec(tm, lambda i, kk: (i, kk)),
                  pl.BlockSpec((tk, n), lambda i, kk: (kk, 0)),
                  pl.BlockSpec((tm, n), lambda i, kk: (i, 0)),
                  pl.BlockSpec((1, n), lambda i, kk: (0, 0)),
                  pl.BlockSpec((tm, n), lambda i, kk: (i, 0))] + [ANY_SPEC] * n_send,
        out_specs=[pl.BlockSpec((tm, n), lambda i, kk: (i, 0)),
                   pl.BlockSpec((tm, n), lambda i, kk: (i, 0)),
                   pl.BlockSpec((1, n), lambda i, kk: (0, 0))] + [ANY_SPEC] * n_send,
        out_shape=[jax.ShapeDtypeStruct((m, n), F32), jax.ShapeDtypeStruct((m, n), BF16),
                   jax.ShapeDtypeStruct((1, n), F32)] + _scattered_shapes(sends),
        scratch_shapes=[pltpu.VMEM((tm, n), F32)] + (_exchange_sems(n_send) if n_send else []),
        compiler_params=_params(2),
    )(a, w, x, g, dres, *sends)
    return out[0], out[1], out[2], out[3:]


ANY_SPEC = pl.BlockSpec(memory_space=pl.ANY)
SEMS_PER_OPERAND = N_DEV - 1


def _exchange_sems(n):
    return [pltpu.SemaphoreType.DMA((n, SEMS_PER_OPERAND)), pltpu.SemaphoreType.DMA((n, SEMS_PER_OPERAND)),
            pltpu.SemaphoreType.DMA((n,))]


def _gathered_shapes(parts):
    return [jax.ShapeDtypeStruct((N_DEV,) + a.shape, a.dtype) for a in parts]


def _scattered_shapes(parts):
    return [jax.ShapeDtypeStruct(a.shape, a.dtype) for a in parts]


def _flat(pos):
    return 4 * pos[0] + 2 * pos[1] + pos[2]


def _remote(src, dst, sems, i, k, to):
    send_sems, recv_sems, _ = sems
    return pltpu.make_async_remote_copy(src_ref=src, dst_ref=dst, send_sem=send_sems.at[i, k],
                                        recv_sem=recv_sems.at[i, k], device_id=to,
                                        device_id_type=pl.DeviceIdType.MESH)


def _scatter_copies(ins, outs, sems):
    x, y, c = lax.axis_index("x"), lax.axis_index("y"), lax.axis_index("c")
    me = _flat((x, y, c))
    copies = [pltpu.make_async_copy(ins[i].at[me], outs[i].at[me], sems[2].at[i]) for i in range(len(ins))]
    for d in range(1, N_DEV):
        peer = (1 - x if d & 4 else x, 1 - y if d & 2 else y, 1 - c if d & 1 else c)
        for i in range(len(ins)):
            copies.append(_remote(ins[i].at[_flat(peer)], outs[i].at[me], sems, i, d - 1, peer))
    return copies


def _two_level_gather(ins, outs, sems):
    x, y, c = lax.axis_index("x"), lax.axis_index("y"), lax.axis_index("c")
    me, sibling = (x, y, c), (x, y, 1 - c)
    chips = [(1 - x, y), (x, 1 - y), (1 - x, 1 - y)]
    n = len(ins)

    def block(i, pos):
        return outs[i].at[_flat(pos)]

    local = [pltpu.make_async_copy(ins[i], block(i, me), sems[2].at[i]) for i in range(n)]
    own = [_remote(ins[i], block(i, me), sems, i, 0, sibling) for i in range(n)]
    own += [_remote(ins[i], block(i, me), sems, i, 1 + j, (*chip, c)) for i in range(n) for j, chip in enumerate(chips)]
    passed = [[_remote(block(i, (*chip, c)), block(i, (*chip, c)), sems, i, 4 + j, sibling) for i in range(n)]
              for j, chip in enumerate(chips)]

    def first():
        for cp in local + own:
            cp.start()

    def relay():
        for j, chip in enumerate(chips):
            for i in range(n):
                _remote(ins[i], block(i, (*chip, c)), sems, i, 1 + j, me).wait_recv()
                passed[j][i].start()

    def finish():
        for i in range(n):
            _remote(ins[i], block(i, sibling), sems, i, 0, me).wait_recv()
            for j, chip in enumerate(chips):
                _remote(ins[i], block(i, (*chip, 1 - c)), sems, i, 4 + j, me).wait_recv()
        for cp in own + [cp for row in passed for cp in row]:
            cp.wait_send()
        for cp in local:
            cp.wait()

    return first, relay, finish


def _gather(parts, name):
    n = len(parts)

    def body(*refs):
        start, relay, finish = _two_level_gather(refs[:n], refs[n:2 * n], refs[2 * n:])
        start()
        relay()
        finish()

    return pl.pallas_call(
        body, name=name, in_specs=[ANY_SPEC] * n, out_specs=[ANY_SPEC] * n,
        out_shape=_gathered_shapes(parts), scratch_shapes=_exchange_sems(n),
    )(*parts)


def _tri(kind):
    j = lax.broadcasted_iota(jnp.int32, (BLK, BLK), 0)
    s = lax.broadcasted_iota(jnp.int32, (BLK, BLK), 1)
    m = {"after": j > s, "upto": j <= s, "before": j < s}[kind]
    return jnp.concatenate([jnp.where(m, 1.0, 0.0), jnp.ones((BLK, BLK), F32)], axis=1).astype(BF16)


def _scan_rows(v, tri):
    r = _dot(v.astype(BF16), tri)
    return r[:, :BLK], r[:, BLK:]


HEADS_PER_STEP = LANES // HEAD_DIM


def _first_head_lanes():
    return lax.broadcasted_iota(jnp.int32, (1, LANES), 1) < HEAD_DIM


def _pair_mean(v, first):
    m0 = jnp.sum(jnp.where(first, v, 0.0), axis=-1, keepdims=True)
    m1 = jnp.sum(jnp.where(first, 0.0, v), axis=-1, keepdims=True)
    return jnp.where(first, m0, m1) * (1.0 / HEAD_DIM)


def _pair_norm(v, g2, first):
    return v * lax.rsqrt(_pair_mean(v * v, first) + EPS) * g2


def _pair_norm_bwd(raw, g2, dn, first):
    r = lax.rsqrt(_pair_mean(raw * raw, first) + EPS)
    xh = raw * r
    dxh = dn * g2
    return r * (dxh - xh * _pair_mean(dxh * xh, first)), jnp.sum(dn * xh, axis=0, keepdims=True)


def _block_diag(v, first):
    zero = jnp.zeros_like(v)
    return jnp.concatenate([jnp.where(first, v, zero), jnp.where(first, zero, v)], axis=0)


def _attn_prep(q_ref, k_ref, v_ref, qg_ref, kg_ref, qc_s, kc_s, vd_s, kd_s, n_blk):
    scale = HEAD_DIM ** -0.5
    first = _first_head_lanes()

    def prep(i, _):
        rows = pl.ds(pl.multiple_of(i * BLK, BLK), BLK)
        both = pl.ds(pl.multiple_of(i * 2 * BLK, 2 * BLK), 2 * BLK)
        qh, ql = _split_bf16(_pair_norm(q_ref[rows, :], qg_ref[...], first) * scale)
        kh, kl = _split_bf16(_pair_norm(k_ref[rows, :], kg_ref[...], first))
        for h in range(HEADS_PER_STEP):
            sl = slice(h * HEAD_DIM, (h + 1) * HEAD_DIM)
            qc_s[h, rows, :] = jnp.concatenate([qh[:, sl], ql[:, sl], qh[:, sl], ql[:, sl]], axis=1)
            kc_s[h, rows, :] = jnp.concatenate([kh[:, sl], kh[:, sl], kl[:, sl], kl[:, sl]], axis=1)
        vd_s[both, :] = _block_diag(v_ref[rows, :].astype(BF16), first)
        if kd_s is not None:
            kd_s[both, :] = _block_diag(kh, first)
        return 0

    lax.fori_loop(0, n_blk, prep, 0)


def _pair_scores(qc, kc_ref, grp, n_tiles=KEY_GROUP):
    zs = []
    for j in range(0, n_tiles, 2):
        two = pl.ds(pl.multiple_of((grp * KEY_GROUP + j) * BLK, 2 * BLK), 2 * BLK)
        z = _dot_nt(qc, kc_ref[two, :])
        zs += [z[:, :BLK], z[:, BLK:]]
    return zs[:n_tiles]


def _col_minus_row():
    row = lax.broadcasted_iota(jnp.int32, (BLK, BLK), 0)
    col = lax.broadcasted_iota(jnp.int32, (BLK, BLK), 1)
    return col - row


def _softplus(z):
    return jnp.maximum(z, 0.0) + jnp.log(1.0 + jnp.exp(-jnp.abs(z)))


def _attn_fwd(proj, qg, kg, sends, name):
    s = proj.shape[0]
    n_blk = s // BLK
    pairs = ATTN_WIDTH // LANES
    n_send = len(sends)

    def body(q_ref, k_ref, v_ref, qg_ref, kg_ref, *rest):
        send_refs, (o_ref, t_ref), rest = rest[:n_send], rest[n_send:n_send + 2], rest[n_send + 2:]
        got_refs, (qc_s, kc_s, vd_s), sems = rest[:n_send], rest[n_send:n_send + 3], rest[n_send + 3:]
        start, relay, finish = _two_level_gather(send_refs, got_refs, sems)
        step = pl.program_id(0)
        pl.when(step == 0)(start)
        pl.when(step == pairs - 1)(relay)
        tri = _tri("after")
        diff = _col_minus_row()
        first = _first_head_lanes()
        heads = range(HEADS_PER_STEP)
        _attn_prep(q_ref, k_ref, v_ref, qg_ref, kg_ref, qc_s, kc_s, vd_s, None, n_blk)

        below = diff < 0

        def group(qc, grp, carry, acc, n_tiles, diagonal):
            blocks = [grp * KEY_GROUP + j for j in reversed(range(n_tiles))]
            zs = [_pair_scores(qc[h], kc_s.at[h], grp, n_tiles)[::-1] for h in heads]
            parts = [[None] * n_tiles for _ in heads]
            for h in heads:
                for j, z in enumerate(zs[h]):
                    sp = _softplus(z)
                    lom = -sp
                    if diagonal and j == 0:
                        lom = jnp.where(below, lom, 0.0)
                    tail, tot = _scan_rows(lom, tri)
                    parts[h][j] = (z - sp + tail, tot)
            carry = list(carry)
            for j, kb in enumerate(blocks):
                ws = []
                for h in heads:
                    lw, tot = parts[h][j]
                    w = jnp.exp(lw + carry[h])
                    if diagonal and j == 0:
                        w = jnp.where(below, w, 0.0)
                    ws.append(w.astype(BF16))
                    carry[h] = carry[h] + tot
                acc = acc + _dot(jnp.concatenate(ws, axis=1),
                                 vd_s[pl.ds(pl.multiple_of(kb * 2 * BLK, 2 * BLK), 2 * BLK), :])
            return tuple(carry), acc

        def q_blocks(top, _):
            for r in range(KEY_GROUP):
                rows = pl.ds(pl.multiple_of((top * KEY_GROUP + r) * BLK, BLK), BLK)
                qc = [qc_s[h, rows, :] for h in heads]
                zero = jnp.zeros((BLK, BLK), F32)
                carry, acc = group(qc, top, (zero,) * HEADS_PER_STEP, jnp.zeros((BLK, LANES), F32), r + 1, True)
                carry, acc = lax.fori_loop(
                    0, top, lambda t, c: group(qc, top - 1 - t, c[0], c[1], KEY_GROUP, False), (carry, acc))
                o_ref[rows, :] = acc.astype(BF16)
                t_ref[rows, :] = jnp.where(first, carry[0], carry[1])
            return 0

        lax.fori_loop(0, n_blk // KEY_GROUP, q_blocks, 0)
        pl.when(step == pairs - 1)(finish)

    col = lambda off: pl.BlockSpec((s, LANES), lambda p: (0, off + p))
    vec = pl.BlockSpec((1, LANES), lambda p: (0, 0))
    out = pl.pallas_call(
        body, name=name, grid=(pairs,),
        in_specs=[col(0), col(pairs), col(2 * pairs), vec, vec] + [ANY_SPEC] * n_send,
        out_specs=[pl.BlockSpec((s, LANES), lambda p: (0, p))] * 2 + [ANY_SPEC] * n_send,
        out_shape=[jax.ShapeDtypeStruct((s, ATTN_WIDTH), BF16), jax.ShapeDtypeStruct((s, ATTN_WIDTH), F32)]
        + _gathered_shapes(sends),
        scratch_shapes=[pltpu.VMEM((HEADS_PER_STEP, s, 4 * HEAD_DIM), BF16)] * 2
        + [pltpu.VMEM((HEADS_PER_STEP * s, LANES), BF16)] + _exchange_sems(n_send),
        compiler_params=_params(1),
    )(proj, proj, proj, qg, kg, *sends)
    return out[0], out[1], out[2:]


def _attn_bwd(proj, dcat, tsum, qg, kg, sends, name):
    s = proj.shape[0]
    n_blk = s // BLK
    pairs = ATTN_WIDTH // LANES
    scale = HEAD_DIM ** -0.5
    n_send = len(sends)
    n_scratch = 7

    def body(q_ref, k_ref, v_ref, do_ref, t_ref, qg_ref, kg_ref, *rest):
        send_refs, rest = rest[:n_send], rest[n_send:]
        (dq_ref, dk_ref, dv_ref, dqg_ref, dkg_ref), rest = rest[:5], rest[5:]
        got_refs, scratch, sems = rest[:n_send], rest[n_send:n_send + n_scratch], rest[n_send + n_scratch:]
        qc_s, kc_s, vd_s, kd_s, qd_s, dob_s, dkv_s = scratch
        copies = _scatter_copies(send_refs, got_refs, sems)

        @pl.when(pl.program_id(0) == 0)
        def _():
            for cp in copies:
                cp.start()

        tri_p = _tri("upto")
        tri_h = _tri("before")
        diff = _col_minus_row()

        @pl.when(pl.program_id(0) == 0)
        def _():
            dqg_ref[...] = jnp.zeros_like(dqg_ref)
            dkg_ref[...] = jnp.zeros_like(dkg_ref)

        first = _first_head_lanes()
        heads = range(HEADS_PER_STEP)
        _attn_prep(q_ref, k_ref, v_ref, qg_ref, kg_ref, qc_s, kc_s, vd_s, kd_s, n_blk)

        def prep(i, _):
            rows = pl.ds(pl.multiple_of(i * BLK, BLK), BLK)
            both = pl.ds(pl.multiple_of(i * 2 * BLK, 2 * BLK), 2 * BLK)
            dob = do_ref[rows, :].astype(BF16)
            dob_s[rows, :] = dob
            none = jnp.zeros((BLK, HEAD_DIM), BF16)
            for h in heads:
                qd_s[h, both, :] = jnp.concatenate(
                    [jnp.concatenate([qc_s[h, rows, 0:HEAD_DIM], none], axis=1),
                     jnp.concatenate([none, dob[:, h * HEAD_DIM:(h + 1) * HEAD_DIM]], axis=1)], axis=0)
                dkv_s[h, rows, :] = jnp.zeros((BLK, LANES), F32)
            return 0

        lax.fori_loop(0, n_blk, prep, 0)

        below = diff < 0

        def group(qc, qd, dob, tq, grp, pc, hc, dq, n_tiles, diagonal):
            blocks = [grp * KEY_GROUP + j for j in range(n_tiles)]
            cols_of = [pl.ds(pl.multiple_of(kb * BLK, BLK), BLK) for kb in blocks]
            both_of = [pl.ds(pl.multiple_of(kb * 2 * BLK, 2 * BLK), 2 * BLK) for kb in blocks]
            on_diagonal = [diagonal and j == n_tiles - 1 for j in range(n_tiles)]
            zs = [_pair_scores(qc[h], kc_s.at[h], grp, n_tiles) for h in heads]
            das = [_dot_nt(dob, vd_s[both, :]) for both in both_of]
            lbs = [[None] * n_tiles for _ in heads]
            scans = [[None] * n_tiles for _ in heads]
            for h in heads:
                for j, z in enumerate(zs[h]):
                    sp = _softplus(z)
                    lom = -sp
                    if on_diagonal[j]:
                        lom = jnp.where(below, lom, 0.0)
                    lbs[h][j] = z - sp
                    scans[h][j] = _scan_rows(lom, tri_p)
            pc, hc = list(pc), list(hc)
            avs = [[None] * n_tiles for _ in heads]
            gws = [[None] * n_tiles for _ in heads]
            hscans = [[None] * n_tiles for _ in heads]
            for h in heads:
                for j in range(n_tiles):
                    p_in, p_tot = scans[h][j]
                    a = jnp.exp(lbs[h][j] + (tq[h] - pc[h] - p_in))
                    if on_diagonal[j]:
                        a = jnp.where(below, a, 0.0)
                    pc[h] = pc[h] + p_tot
                    gw = das[j][:, h * BLK:(h + 1) * BLK] * a
                    avs[h][j] = a.astype(BF16)
                    gws[h][j] = gw
                    hscans[h][j] = _scan_rows(gw, tri_h)
            dzs = [[None] * n_tiles for _ in heads]
            for h in heads:
                for j in range(n_tiles):
                    h_in, g_tot = hscans[h][j]
                    gw = gws[h][j]
                    dz = gw - jnp.exp(lbs[h][j]) * (gw + hc[h] + h_in)
                    if on_diagonal[j]:
                        dz = jnp.where(below, dz, 0.0)
                    hc[h] = hc[h] + g_tot
                    dzs[h][j] = dz.astype(BF16)
            for j, both in enumerate(both_of):
                dq = dq + _dot(jnp.concatenate([dzs[h][j] for h in heads], axis=1), kd_s[both, :])
            for h in heads:
                for j, cols in enumerate(cols_of):
                    dkv_s[h, cols, :] += _dot_tn(jnp.concatenate([dzs[h][j], avs[h][j]], axis=0), qd[h])
            return tuple(pc), tuple(hc), dq

        def q_blocks(top, dqg):
            for r in range(KEY_GROUP):
                qi = top * KEY_GROUP + r
                rows = pl.ds(pl.multiple_of(qi * BLK, BLK), BLK)
                both = pl.ds(pl.multiple_of(qi * 2 * BLK, 2 * BLK), 2 * BLK)
                qc = [qc_s[h, rows, :] for h in heads]
                qd = [qd_s[h, both, :] for h in heads]
                dob = dob_s[rows, :]
                tboth = t_ref[rows, :]
                tq = [jnp.concatenate([tboth[:, h * HEAD_DIM:(h + 1) * HEAD_DIM]] * 2, axis=1) for h in heads]
                zero = (jnp.zeros((BLK, BLK), F32),) * HEADS_PER_STEP
                pc, hc, dq = lax.fori_loop(
                    0, top, lambda grp, c: group(qc, qd, dob, tq, grp, c[0], c[1], c[2], KEY_GROUP, False),
                    (zero, zero, jnp.zeros((BLK, LANES), F32)))
                _, _, dq = group(qc, qd, dob, tq, top, pc, hc, dq, r + 1, True)
                dq_raw, dg = _pair_norm_bwd(q_ref[rows, :], qg_ref[...], dq * scale, first)
                dq_ref[rows, :] = dq_raw.astype(BF16)
                dqg = dqg + dg
            return dqg

        dqg = lax.fori_loop(0, n_blk // KEY_GROUP, q_blocks, jnp.zeros((1, LANES), F32))

        def finish(i, dkg):
            rows = pl.ds(pl.multiple_of(i * BLK, BLK), BLK)
            dk = jnp.concatenate([dkv_s[h, rows, 0:HEAD_DIM] for h in heads], axis=1)
            dv = jnp.concatenate([dkv_s[h, rows, HEAD_DIM:2 * HEAD_DIM] for h in heads], axis=1)
            dk_raw, dg = _pair_norm_bwd(k_ref[rows, :], kg_ref[...], dk, first)
            dk_ref[rows, :] = dk_raw.astype(BF16)
            dv_ref[rows, :] = dv.astype(BF16)
            return dkg + dg

        dkg = lax.fori_loop(0, n_blk, finish, jnp.zeros((1, LANES), F32))
        dqg_ref[0:1, :] += dqg
        dkg_ref[0:1, :] += dkg

        @pl.when(pl.program_id(0) == pairs - 1)
        def _():
            for cp in copies:
                cp.wait()

    col = lambda off: pl.BlockSpec((s, LANES), lambda p: (0, off + p))
    vec = pl.BlockSpec((1, LANES), lambda p: (0, 0))
    small = pl.BlockSpec((8, LANES), lambda p: (0, 0))
    out = pl.pallas_call(
        body, name=name, grid=(pairs,),
        in_specs=[col(0), col(pairs), col(2 * pairs), col(0), col(0), vec, vec] + [ANY_SPEC] * n_send,
        out_specs=[col(0)] * 3 + [small] * 2 + [ANY_SPEC] * n_send,
        out_shape=[jax.ShapeDtypeStruct((s, ATTN_WIDTH), BF16)] * 3 + [jax.ShapeDtypeStruct((8, LANES), F32)] * 2
        + _scattered_shapes(sends),
        scratch_shapes=[pltpu.VMEM((HEADS_PER_STEP, s, 4 * HEAD_DIM), BF16)] * 2
        + [pltpu.VMEM((HEADS_PER_STEP * s, LANES), BF16)] * 2
        + [pltpu.VMEM((HEADS_PER_STEP, HEADS_PER_STEP * s, LANES), BF16), pltpu.VMEM((s, LANES), BF16),
           pltpu.VMEM((HEADS_PER_STEP, s, LANES), F32)] + _exchange_sems(n_send),
        compiler_params=_params(1),
    )(proj, proj, proj, dcat, tsum, qg, kg, *sends)
    return out[:5], out[5:]


CONV_ROWS = 128


def _shifted(window, shift, halo):
    if shift == 0:
        return window[halo:, :]
    return pltpu.roll(window, shift, 0)[halo:, :]


SUBLANES = 8


def _row_shifts(window, up):
    n = window.shape[0]
    return [window] + [pltpu.roll(window, n - b if up else b, 0) for b in range(1, SUBLANES)]


def _earlier(shifts, back, rows):
    a, b = divmod(back, SUBLANES)
    return shifts[b][CONV_HALO - SUBLANES * a:CONV_HALO - SUBLANES * a + rows, :]


def _later(shifts, ahead, rows):
    a, b = divmod(ahead, SUBLANES)
    return shifts[b][SUBLANES * a:SUBLANES * a + rows, :]


def _lane_blocks(width):
    return [slice(c, c + LANES) for c in range(0, width, LANES)]


def _conv_taps(shifts, w_ref, lanes, rows):
    y = None
    for k in range(CONV_KERNEL):
        term = _earlier(shifts, CONV_KERNEL - 1 - k, rows) * w_ref[k:k + 1, lanes]
        y = term if y is None else y + term
    return y


def _conv_fwd(proj, w, b, lg, lb, name):
    s = proj.shape[0]
    cw = w.shape[1]
    rows = CONV_ROWS
    blk_a = (proj.shape[1] - 2 * cw) // cw

    def body(a_ref, g_ref, w_ref, b_ref, lg_ref, lb_ref, o_ref, y_ref, u_s):
        u_s[0:CONV_HALO, :] = jnp.zeros((CONV_HALO, cw), F32)

        def glu(i, _):
            r0 = pl.multiple_of(i * rows, rows)
            u_s[pl.ds(CONV_HALO + r0, rows), :] = a_ref[pl.ds(r0, rows), :] * _sigmoid(g_ref[pl.ds(r0, rows), :])
            return 0

        lax.fori_loop(0, s // rows, glu, 0)

        def chunk(i, _):
            r0 = pl.multiple_of(i * rows, rows)
            for lanes in _lane_blocks(cw):
                shifts = _row_shifts(u_s[pl.ds(r0, CONV_HALO + rows), lanes], False)
                y_ref[pl.ds(r0, rows), lanes] = _conv_taps(shifts, w_ref, lanes, rows) + b_ref[:, lanes]
            y = y_ref[pl.ds(r0, rows), :]
            yc = y - jnp.mean(y, axis=-1, keepdims=True)
            n = yc * lax.rsqrt(jnp.mean(yc * yc, axis=-1, keepdims=True) + EPS)
            ln = n * lg_ref[...] + lb_ref[...]
            o_ref[pl.ds(r0, rows), :] = (ln * _sigmoid(ln)).astype(BF16)
            return 0

        lax.fori_loop(0, s // rows, chunk, 0)

    vec = pl.BlockSpec((1, cw), lambda i: (0, 0))
    return pl.pallas_call(
        body, name=name, grid=(1,),
        in_specs=[pl.BlockSpec((s, cw), lambda i: (0, blk_a)), pl.BlockSpec((s, cw), lambda i: (0, blk_a + 1)),
                  pl.BlockSpec((CONV_KERNEL, cw), lambda i: (0, 0)), vec, vec, vec],
        out_specs=[pl.BlockSpec((s, cw), lambda i: (0, 0))] * 2,
        out_shape=[jax.ShapeDtypeStruct((s, cw), BF16), jax.ShapeDtypeStruct((s, cw), F32)],
        scratch_shapes=[pltpu.VMEM((CONV_HALO + s, cw), F32)],
        compiler_params=_params(1),
    )(proj, proj, w, b, lg, lb)


def _conv_bwd(proj, y, dcat, w, lg, lb, sends, name):
    s = proj.shape[0]
    cw = w.shape[1]
    rows = CONV_ROWS
    blk_a = (proj.shape[1] - 2 * cw) // cw
    n_chunk = s // rows
    n_send = len(sends)

    def body(a_ref, g_ref, y_ref, dc_ref, w_ref, lg_ref, lb_ref, *rest):
        send_refs, (o_ref, dw_ref, db_ref, dlg_ref, dlb_ref), rest = rest[:n_send], rest[n_send:n_send + 5], rest[n_send + 5:]
        got_refs, (u_s, dy_s, dw_s), sems = rest[:n_send], rest[n_send:n_send + 3], rest[n_send + 3:]
        copies = _scatter_copies(send_refs, got_refs, sems)
        for cp in copies:
            cp.start()
        u_s[0:CONV_HALO, :] = jnp.zeros((CONV_HALO, cw), F32)
        dy_s[pl.ds(s, CONV_HALO), :] = jnp.zeros((CONV_HALO, cw), F32)
        dw_s[...] = jnp.zeros_like(dw_s)

        def glu(i, _):
            r0 = pl.multiple_of(i * rows, rows)
            u_s[pl.ds(CONV_HALO + r0, rows), :] = a_ref[pl.ds(r0, rows), :] * _sigmoid(g_ref[pl.ds(r0, rows), :])
            return 0

        lax.fori_loop(0, n_chunk, glu, 0)

        def chunk(i, carry):
            db, dlg, dlb = carry
            r0 = pl.multiple_of(i * rows, rows)
            y = y_ref[pl.ds(r0, rows), :]
            yc = y - jnp.mean(y, axis=-1, keepdims=True)
            r = lax.rsqrt(jnp.mean(yc * yc, axis=-1, keepdims=True) + EPS)
            n = yc * r
            ln = n * lg_ref[...] + lb_ref[...]
            sg = _sigmoid(ln)
            dln = dc_ref[pl.ds(r0, rows), :] * (sg * (1.0 + ln * (1.0 - sg)))
            dn = dln * lg_ref[...]
            dy = r * (dn - jnp.mean(dn, axis=-1, keepdims=True) - n * jnp.mean(dn * n, axis=-1, keepdims=True))
            dy_s[pl.ds(r0, rows), :] = dy
            for lanes in _lane_blocks(cw):
                shifts = _row_shifts(u_s[pl.ds(r0, CONV_HALO + rows), lanes], False)
                dy_part = dy[:, lanes]
                for k in range(CONV_KERNEL):
                    prod = _earlier(shifts, CONV_KERNEL - 1 - k, rows) * dy_part
                    dw_s[k, :, lanes] += jnp.sum(prod.reshape(rows // SUBLANES, SUBLANES, LANES), axis=0)
            return (db + jnp.sum(dy, axis=0, keepdims=True),
                    dlg + jnp.sum(dln * n, axis=0, keepdims=True),
                    dlb + jnp.sum(dln, axis=0, keepdims=True))

        zero = jnp.zeros((1, cw), F32)
        db, dlg, dlb = lax.fori_loop(0, n_chunk, chunk, (zero, zero, zero))
        db_ref[...] = db
        dlg_ref[...] = dlg
        dlb_ref[...] = dlb
        for k in range(CONV_KERNEL):
            dw_ref[k:k + 1, :] = jnp.sum(dw_s[k], axis=0, keepdims=True)

        def back(i, _):
            r0 = pl.multiple_of(i * rows, rows)
            for lanes in _lane_blocks(cw):
                shifts = _row_shifts(dy_s[pl.ds(r0, rows + CONV_HALO), lanes], True)
                du = None
                for k in range(CONV_KERNEL):
                    term = _later(shifts, CONV_KERNEL - 1 - k, rows) * w_ref[k:k + 1, lanes]
                    du = term if du is None else du + term
                av = a_ref[pl.ds(r0, rows), lanes]
                sg = _sigmoid(g_ref[pl.ds(r0, rows), lanes])
                o_ref[pl.ds(r0, rows), lanes] = (du * sg).astype(BF16)
                o_ref[pl.ds(r0, rows), slice(cw + lanes.start, cw + lanes.stop)] = (du * av * sg * (1.0 - sg)).astype(BF16)
            return 0

        lax.fori_loop(0, n_chunk, back, 0)
        for cp in copies:
            cp.wait()

    vec = pl.BlockSpec((1, cw), lambda i: (0, 0))
    wspec = pl.BlockSpec((CONV_KERNEL, cw), lambda i: (0, 0))
    out = pl.pallas_call(
        body, name=name, grid=(1,),
        in_specs=[pl.BlockSpec((s, cw), lambda i: (0, blk_a)), pl.BlockSpec((s, cw), lambda i: (0, blk_a + 1)),
                  pl.BlockSpec((s, cw), lambda i: (0, 0)), pl.BlockSpec((s, cw), lambda i: (0, 1)), wspec, vec, vec]
        + [ANY_SPEC] * n_send,
        out_specs=[pl.BlockSpec((s, 2 * cw), lambda i: (0, 0)), wspec, vec, vec, vec] + [ANY_SPEC] * n_send,
        out_shape=[jax.ShapeDtypeStruct((s, 2 * cw), BF16), jax.ShapeDtypeStruct((CONV_KERNEL, cw), F32)]
        + [jax.ShapeDtypeStruct((1, cw), F32)] * 3 + _scattered_shapes(sends),
        scratch_shapes=[pltpu.VMEM((CONV_HALO + s, cw), F32), pltpu.VMEM((s + CONV_HALO, cw), F32),
                        pltpu.VMEM((CONV_KERNEL, 8, cw), F32)] + _exchange_sems(n_send),
        compiler_params=_params(1),
    )(proj, proj, y, dcat, w, lg, lb, *sends)
    return out[:5], out[5:]


FFN_ROWS = 256


def _ffn_gate(g_ref, r0, rows, w_ref, b_ref):
    cur = g_ref[pl.ds(r0, rows), :].astype(F32)
    prev = g_ref[pl.ds(pl.multiple_of(jnp.maximum(r0 - FFN_HALO, 0), FFN_HALO), FFN_HALO), :].astype(F32)
    prev = jnp.where(r0 > 0, prev, 0.0)
    window = jnp.concatenate([prev, cur], axis=0)
    gc = cur * w_ref[FFN_KERNEL - 1:FFN_KERNEL, :] + b_ref[...]
    for k in range(FFN_KERNEL - 1):
        gc = gc + _shifted(window, FFN_KERNEL - 1 - k, FFN_HALO) * w_ref[k:k + 1, :]
    return gc, window


def _ffn_fwd(up, w, b, name):
    s = up.shape[0]
    f = w.shape[1]
    tc = _pick(f, (256, 128))
    nc = f // tc
    rows = _pick(s, (FFN_ROWS, 128))

    def body(g_ref, v_ref, w_ref, b_ref, o_ref):
        def chunk(i, _):
            r0 = pl.multiple_of(i * rows, rows)
            gc, _w = _ffn_gate(g_ref, r0, rows, w_ref, b_ref)
            o_ref[pl.ds(r0, rows), :] = (gc * _sigmoid(gc) * v_ref[pl.ds(r0, rows), :].astype(F32)).astype(BF16)
            return 0

        lax.fori_loop(0, s // rows, chunk, 0)

    return pl.pallas_call(
        body, name=name, grid=(nc,),
        in_specs=[pl.BlockSpec((s, tc), lambda j: (0, j)), pl.BlockSpec((s, tc), lambda j: (0, nc + j)),
                  pl.BlockSpec((FFN_KERNEL, tc), lambda j: (0, j)), pl.BlockSpec((1, tc), lambda j: (0, j))],
        out_specs=pl.BlockSpec((s, tc), lambda j: (0, j)),
        out_shape=jax.ShapeDtypeStruct((s, f), BF16),
        compiler_params=_params(1),
    )(up, up, w, b)


def _ffn_bwd(up, dact, w, b, name):
    s = up.shape[0]
    f = w.shape[1]
    tc = _pick(f, (256, 128))
    nc = f // tc
    rows = _pick(s, (FFN_ROWS, 128))
    n_chunk = s // rows

    def body(g_ref, v_ref, da_ref, w_ref, b_ref, d_ref, dw_ref, db_ref, dgc_s):
        dg_ref, dv_ref = d_ref.at[0], d_ref.at[1]
        dgc_s[pl.ds(s, FFN_HALO), :] = jnp.zeros((FFN_HALO, tc), F32)

        def chunk(i, carry):
            r0 = pl.multiple_of(i * rows, rows)
            gc, window = _ffn_gate(g_ref, r0, rows, w_ref, b_ref)
            sg = _sigmoid(gc)
            da = da_ref[pl.ds(r0, rows), :].astype(F32)
            dv_ref[pl.ds(r0, rows), :] = (da * gc * sg).astype(BF16)
            dgc = da * v_ref[pl.ds(r0, rows), :].astype(F32) * (sg * (1.0 + gc * (1.0 - sg)))
            dgc_s[pl.ds(r0, rows), :] = dgc
            out = [carry[0] + jnp.sum(dgc, axis=0, keepdims=True)]
            for k in range(FFN_KERNEL):
                out.append(carry[1 + k] + jnp.sum(_shifted(window, FFN_KERNEL - 1 - k, FFN_HALO) * dgc,
                                                  axis=0, keepdims=True))
            return tuple(out)

        zero = jnp.zeros((1, tc), F32)
        sums = lax.fori_loop(0, n_chunk, chunk, (zero,) * (1 + FFN_KERNEL))
        db_ref[...] = sums[0]
        for k in range(FFN_KERNEL):
            dw_ref[k:k + 1, :] = sums[1 + k]

        def back(i, _):
            r0 = pl.multiple_of(i * rows, rows)
            window = dgc_s[pl.ds(r0, rows + FFN_HALO), :]
            dg = window[:rows, :] * w_ref[FFN_KERNEL - 1:FFN_KERNEL, :]
            for k in range(FFN_KERNEL - 1):
                sh = FFN_KERNEL - 1 - k
                dg = dg + pltpu.roll(window, rows + FFN_HALO - sh, 0)[:rows, :] * w_ref[k:k + 1, :]
            dg_ref[pl.ds(r0, rows), :] = dg.astype(BF16)
            return 0

        lax.fori_loop(0, n_chunk, back, 0)

    blk = lambda off: pl.BlockSpec((s, tc), lambda j: (0, off + j))
    return pl.pallas_call(
        body, name=name, grid=(nc,),
        in_specs=[blk(0), blk(nc), blk(0), pl.BlockSpec((FFN_KERNEL, tc), lambda j: (0, j)),
                  pl.BlockSpec((1, tc), lambda j: (0, j))],
        out_specs=[pl.BlockSpec((2, s, tc), lambda j: (0, 0, j)), pl.BlockSpec((FFN_KERNEL, tc), lambda j: (0, j)),
                   pl.BlockSpec((1, tc), lambda j: (0, j))],
        out_shape=[jax.ShapeDtypeStruct((2, s, f), BF16),
                   jax.ShapeDtypeStruct((FFN_KERNEL, f), F32), jax.ShapeDtypeStruct((1, f), F32)],
        scratch_shapes=[pltpu.VMEM((s + FFN_HALO, tc), F32)],
        compiler_params=_params(1),
    )(up, up, dact, w, b)


def _loss_head(y, target, name):
    m, n = y.shape
    tm = _pick(m, (256, 128))

    def body(y_ref, t_ref, l_ref, d_ref, db_ref):
        e = y_ref[...] - t_ref[...]
        part = 0.5 * jnp.sum(jnp.sum(e * e, axis=-1, keepdims=True) / n, axis=0, keepdims=True)

        @pl.when(pl.program_id(0) == 0)
        def _():
            l_ref[...] = jnp.zeros_like(l_ref)

        l_ref[...] += part
        d = e / n
        d_ref[...] = d
        db_ref[...] = d.astype(BF16)

    return pl.pallas_call(
        body, name=name, grid=(m // tm,),
        in_specs=[pl.BlockSpec((tm, n), lambda i: (i, 0))] * 2,
        out_specs=[pl.BlockSpec((8, LANES), lambda i: (0, 0)), pl.BlockSpec((tm, n), lambda i: (i, 0)),
                   pl.BlockSpec((tm, n), lambda i: (i, 0))],
        out_shape=[jax.ShapeDtypeStruct((8, LANES), F32), jax.ShapeDtypeStruct((m, n), F32),
                   jax.ShapeDtypeStruct((m, n), BF16)],
        compiler_params=_params(1),
    )(y, target)


def _adamw_math(w, g, m, v):
    m = ADAM_B1 * m + (1.0 - ADAM_B1) * g
    v = ADAM_B2 * v + (1.0 - ADAM_B2) * (g * g)
    m_hat = m / (1.0 - ADAM_B1 ** ADAM_STEP)
    v_hat = v / (1.0 - ADAM_B2 ** ADAM_STEP)
    delta = -ADAM_LR * (m_hat / (jnp.sqrt(v_hat) + ADAM_EPS) + ADAM_WD * w)
    return delta, m, v


def _sum_adamw(parts, w, m, v, sends, name):
    depth, r, c = w.shape
    parts = [list(p) if isinstance(p, (list, tuple)) else [p] for p in parts]
    tr = max(t for t in range(16, min(min(a.shape[1] for p in parts for a in p), 192) + 1, 16)
             if all(a.shape[1] % t == 0 for p in parts for a in p))
    steps = r // tr
    n_send = len(sends)
    chunks = []
    for layer, p in enumerate(parts):
        s0 = 0
        for a in p:
            chunks.append((layer, s0, a.shape[1] // tr))
            s0 += a.shape[1] // tr
    n_chunk = len(chunks)

    def body(*refs):
        p_refs, refs = refs[:n_chunk], refs[n_chunk:]
        (w_ref, m_ref, v_ref), send_refs, refs = refs[:3], refs[3:3 + n_send], refs[3 + n_send:]
        (g_out, d_out, m_out, v_out), got_refs, sems = refs[:4], refs[4:4 + n_send], refs[4 + n_send:]
        is_first = (pl.program_id(0) == 0) & (pl.program_id(1) == 0)
        is_last = (pl.program_id(0) == depth - 1) & (pl.program_id(1) == steps - 1)
        if n_send:
            start, relay, finish = _two_level_gather(send_refs, got_refs, sems)
            pl.when(is_first)(start)
            pl.when(is_last)(relay)
        for p_ref, (layer, s0, n) in zip(p_refs, chunks):
            step = pl.program_id(1)

            @pl.when((pl.program_id(0) == layer) & (step >= s0) & (step < s0 + n))
            def _():
                g = p_ref[0].astype(F32)
                for src in range(1, N_DEV):
                    g = g + p_ref[src].astype(F32)
                d, mn, vn = _adamw_math(w_ref[0], g, m_ref[0], v_ref[0])
                g_out[0] = g
                d_out[0] = d
                m_out[0] = mn
                v_out[0] = vn

        if n_send:
            pl.when(is_last)(finish)

    def part_spec(layer, s0, n):
        return pl.BlockSpec((N_DEV, tr, c), lambda l, i: (0, jnp.clip((l - layer) * steps + i - s0, 0, n - 1), 0))

    blk = pl.BlockSpec((1, tr, c), lambda l, i: (l, i, 0))
    out = pl.pallas_call(
        body, name=name, grid=(depth, steps),
        in_specs=[part_spec(*ch) for ch in chunks] + [blk, blk, blk] + [ANY_SPEC] * n_send,
        out_specs=[blk] * 4 + [ANY_SPEC] * n_send,
        out_shape=[jax.ShapeDtypeStruct(w.shape, F32)] * 4 + _gathered_shapes(sends),
        scratch_shapes=_exchange_sems(n_send) if n_send else [],
        compiler_params=_params(2),
    )(*[a for p in parts for a in p], w, m, v, *sends)
    return out[:4], out[4:]


VMEM_SPEC = pl.BlockSpec(memory_space=pltpu.VMEM)


def _sum_small(parts, name):
    n = len(parts)

    def body(*refs):
        for p_ref, o_ref in zip(refs[:n], refs[n:]):
            g = p_ref[0]
            for src in range(1, N_DEV):
                g = g + p_ref[src]
            o_ref[...] = g

    return pl.pallas_call(
        body, name=name, in_specs=[VMEM_SPEC] * n, out_specs=[VMEM_SPEC] * n,
        out_shape=[jax.ShapeDtypeStruct(p.shape[1:], F32) for p in parts],
        compiler_params=_params(),
    )(*parts)


def _adamw_small(ws, gs, ms, vs, name):
    n = len(ws)

    def body(*refs):
        ins, outs = refs[:4 * n], refs[4 * n:]
        for i in range(n):
            d, mn, vn = _adamw_math(ins[i][...], ins[n + i][...], ins[2 * n + i][...], ins[3 * n + i][...])
            outs[i][...] = d
            outs[n + i][...] = mn
            outs[2 * n + i][...] = vn

    out = pl.pallas_call(
        body, name=name, in_specs=[VMEM_SPEC] * (4 * n), out_specs=[VMEM_SPEC] * (3 * n),
        out_shape=[jax.ShapeDtypeStruct(w.shape, F32) for w in ws] * 3,
        compiler_params=_params(),
    )(*ws, *gs, *ms, *vs)
    return out[:n], out[n:2 * n], out[2 * n:]


def kernel(x, norm1_g, w_in, q_norm_g, k_norm_g, conv_dw_w, conv_dw_b, conv_ln_g, conv_ln_b, w_out, norm2_g, w_up, ffn_dw_w, ffn_dw_b, w_down, loss_target, m_norm1_g, m_w_in, m_q_norm_g, m_k_norm_g, m_conv_dw_w, m_conv_dw_b, m_conv_ln_g, m_conv_ln_b, m_w_out, m_norm2_g, m_w_up, m_ffn_dw_w, m_ffn_dw_b, m_w_down, v_norm1_g, v_w_in, v_q_norm_g, v_k_norm_g, v_conv_dw_w, v_conv_dw_b, v_conv_ln_g, v_conv_ln_b, v_w_out, v_norm2_g, v_w_up, v_ffn_dw_w, v_ffn_dw_b, v_w_down):
    depth = w_in.shape[0]
    d_ff = w_down.shape[1] * N_DEV
    conv_w = conv_dw_b.shape[1]
    cw_shard = conv_dw_w.shape[2]
    fw_shard = ffn_dw_w.shape[2]
    me = 4 * lax.axis_index("x") + 2 * lax.axis_index("y") + lax.axis_index("c")

    transposed = lambda a: a.transpose(0, 2, 1)
    b_in, b_out, b_up, b_down = (transposed(w_in).astype(BF16), w_out.astype(BF16), transposed(w_up).astype(BF16),
                                 w_down.astype(BF16))
    rows_major = lambda g: g.reshape(N_DEV * g.shape[1], g.shape[2])
    g_in0, g_cw, g_fw = _gather([b_in[0], conv_dw_w, ffn_dw_w], name="gather_first")
    wf_in, wf_out, wf_up, wf_down = [rows_major(g_in0)] + [None] * (depth - 1), [None] * depth, [None] * depth, [None] * depth
    cwf = g_cw.transpose(1, 2, 0, 3).reshape(depth, CONV_KERNEL, conv_w)
    fwf = g_fw.transpose(1, 2, 0, 3).reshape(depth, FFN_KERNEL, d_ff)

    row = lambda a, l: a[l].reshape(1, -1)
    both_heads = lambda a, l: jnp.tile(row(a, l), (1, HEADS_PER_STEP))

    xs = x[0]
    saved = []
    for l in range(depth):
        proj, h1, _ = _mm_rms(xs, row(norm1_g, l), wf_in[l], F32, [], name="fwd_in")
        sends = [b_out[l], b_up[l]] + ([b_in[l + 1]] if l + 1 < depth else [])
        attn, tsum, got = _attn_fwd(proj, both_heads(q_norm_g, l), both_heads(k_norm_g, l), sends, name="fwd_attn")
        wf_out[l], wf_up[l] = rows_major(got[0]), rows_major(got[1])
        if l + 1 < depth:
            wf_in[l + 1] = rows_major(got[2])
        conv, conv_y = _conv_fwd(proj, cwf[l], row(conv_dw_b, l), row(conv_ln_g, l), row(conv_ln_b, l),
                                 name="fwd_conv")
        cat = jnp.concatenate([attn, conv], axis=1)
        x_mid = _mm_res(cat, wf_out[l], xs, name="fwd_out")
        up, h2, got = _mm_rms(x_mid, row(norm2_g, l), wf_up[l], BF16, [b_down[l]], name="fwd_up")
        wf_down[l] = rows_major(got[0])
        act = _ffn_fwd(up, fwf[l], row(ffn_dw_b, l), name="fwd_ffn")
        x_next = _mm_res(act, wf_down[l], x_mid, name="fwd_down")
        saved.append((xs, h1, proj, tsum, cat, x_mid, h2, up, act, conv_y))
        xs = x_next

    loss_tile, dx, dxb = _loss_head(xs, loss_target[0], name="loss_head")
    loss = lax.psum(loss_tile[0, 0], ("x", "y", "c"))

    r_in, r_out, r_up, r_down = [None] * depth, [None] * depth, [None] * depth, [None] * depth
    row_blocks = lambda g: g.reshape(N_DEV, g.shape[0] // N_DEV, g.shape[1])
    small = {k: [None] * depth for k in ("norm1_g", "q_norm_g", "k_norm_g", "conv_dw_w", "conv_dw_b", "conv_ln_g",
                                         "conv_ln_b", "norm2_g", "ffn_dw_w", "ffn_dw_b")}
    gw_in = None
    for l in reversed(range(depth)):
        xs, h1, proj, tsum, cat, x_mid, h2, up, act, conv_y = saved[l]
        dact = _mm_nt(dxb, wf_down[l], BF16, name="bwd_dact")
        gw_down = _mm_tn(act, dxb, name="bwd_gw_down")
        dup, small["ffn_dw_w"][l], small["ffn_dw_b"][l] = _ffn_bwd(up, dact, fwf[l], row(ffn_dw_b, l), name="bwd_ffn")
        gw_up = _mm_tn(dup, h2, name="bwd_gw_up")
        if l + 1 < depth:
            blocks = row_blocks(gw_in)
            half = blocks.shape[1] // 2
            in_halves = [blocks[:, :half], blocks[:, half:]]
        dx, dxb, small["norm2_g"][l], got_a = _mm_rmsbwd(dup, wf_up[l], x_mid, row(norm2_g, l), dx,
                                                         in_halves[:1] if l + 1 < depth else [], name="bwd_up")
        dcat = _mm_nt(dxb, wf_out[l], F32, name="bwd_dcat")
        gw_out = _mm_tn(cat, dxb, name="bwd_gw_out")
        early = in_halves[1] if l + 1 < depth else row_blocks(gw_out)
        (dglu, small["conv_dw_w"][l], small["conv_dw_b"][l], small["conv_ln_g"][l], small["conv_ln_b"][l]), got = (
            _conv_bwd(proj, conv_y, dcat, cwf[l], row(conv_ln_g, l), row(conv_ln_b, l), [early], name="bwd_conv"))
        sends = [row_blocks(gw_down), row_blocks(gw_up)] + ([row_blocks(gw_out)] if l + 1 < depth else [])
        (dq, dk, dv, dqg, dkg), got2 = _attn_bwd(proj, dcat, tsum, both_heads(q_norm_g, l), both_heads(k_norm_g, l),
                                                 sends, name="bwd_attn")
        r_down[l], r_up[l] = got2[:2]
        if l + 1 < depth:
            r_in[l + 1], r_out[l] = [got_a[0], got[0]], got2[2]
        else:
            r_out[l] = got[0]
        small["q_norm_g"][l] = dqg[0:1, :HEAD_DIM] + dqg[0:1, HEAD_DIM:]
        small["k_norm_g"][l] = dkg[0:1, :HEAD_DIM] + dkg[0:1, HEAD_DIM:]
        dproj = jnp.concatenate([dq, dk, dv, dglu], axis=1)
        gw_in = _mm_tn(dproj, h1, name="bwd_gw_in")
        sends = [row_blocks(gw_in)] if l == 0 else []
        dx, dxb, small["norm1_g"][l], got = _mm_rmsbwd(dproj, wf_in[l], xs, row(norm1_g, l), dx, sends, name="bwd_in")
        if l == 0:
            r_in[0] = got[0]
    grad_x = dx[None]

    names = ["norm1_g", "q_norm_g", "k_norm_g", "conv_dw_w", "conv_dw_b", "conv_ln_g", "conv_ln_b", "norm2_g",
             "ffn_dw_w", "ffn_dw_b"]
    full_shapes = {"norm1_g": norm1_g.shape, "q_norm_g": q_norm_g.shape, "k_norm_g": k_norm_g.shape,
                   "conv_dw_w": (depth, CONV_KERNEL, conv_w), "conv_dw_b": conv_dw_b.shape,
                   "conv_ln_g": conv_ln_g.shape, "conv_ln_b": conv_ln_b.shape, "norm2_g": norm2_g.shape,
                   "ffn_dw_w": (depth, FFN_KERNEL, d_ff), "ffn_dw_b": ffn_dw_b.shape}
    partial = [jnp.stack(small[k]).reshape(full_shapes[k]) for k in names]
    big = {}
    big["w_out"], all_partials = _sum_adamw(r_out, w_out, m_w_out, v_w_out, partial, name="adamw_out")
    big["w_up"], _ = _sum_adamw(r_up, transposed(w_up), transposed(m_w_up), transposed(v_w_up), [], name="adamw_up")
    big["w_down"], _ = _sum_adamw(r_down, w_down, m_w_down, v_w_down, [], name="adamw_down")
    big["w_in"], _ = _sum_adamw(r_in, transposed(w_in), transposed(m_w_in), transposed(v_w_in), [], name="adamw_in")
    for k in ("w_in", "w_up"):
        big[k] = [transposed(a) for a in big[k]]

    grads = dict(zip(names, _sum_small(all_partials, name="sum_small_grads")))
    grads["conv_dw_w"] = lax.dynamic_slice_in_dim(grads["conv_dw_w"], me * cw_shard, cw_shard, axis=2)
    grads["ffn_dw_w"] = lax.dynamic_slice_in_dim(grads["ffn_dw_w"], me * fw_shard, fw_shard, axis=2)
    weights = dict(norm1_g=norm1_g, q_norm_g=q_norm_g, k_norm_g=k_norm_g, conv_dw_w=conv_dw_w, conv_dw_b=conv_dw_b,
                   conv_ln_g=conv_ln_g, conv_ln_b=conv_ln_b, norm2_g=norm2_g, ffn_dw_w=ffn_dw_w, ffn_dw_b=ffn_dw_b)
    m_in = dict(norm1_g=m_norm1_g, q_norm_g=m_q_norm_g, k_norm_g=m_k_norm_g, conv_dw_w=m_conv_dw_w,
                conv_dw_b=m_conv_dw_b, conv_ln_g=m_conv_ln_g, conv_ln_b=m_conv_ln_b, norm2_g=m_norm2_g,
                ffn_dw_w=m_ffn_dw_w, ffn_dw_b=m_ffn_dw_b)
    v_in = dict(norm1_g=v_norm1_g, q_norm_g=v_q_norm_g, k_norm_g=v_k_norm_g, conv_dw_w=v_conv_dw_w,
                conv_dw_b=v_conv_dw_b, conv_ln_g=v_conv_ln_g, conv_ln_b=v_conv_ln_b, norm2_g=v_norm2_g,
                ffn_dw_w=v_ffn_dw_w, ffn_dw_b=v_ffn_dw_b)
    d_s, m_s, v_s = _adamw_small([weights[k] for k in names], [grads[k] for k in names], [m_in[k] for k in names],
                                 [v_in[k] for k in names], name="adamw_small")
    delta, new_m, new_v = dict(zip(names, d_s)), dict(zip(names, m_s)), dict(zip(names, v_s))
    for k, (g, d, mn, vn) in big.items():
        grads[k], delta[k], new_m[k], new_v[k] = g, d, mn, vn

    order = ["norm1_g", "w_in", "q_norm_g", "k_norm_g", "conv_dw_w", "conv_dw_b", "conv_ln_g", "conv_ln_b", "w_out",
             "norm2_g", "w_up", "ffn_dw_w", "ffn_dw_b", "w_down"]
    return (loss, grad_x, *[grads[k] for k in order], *[delta[k] for k in order], *[new_m[k] for k in order],
            *[new_v[k] for k in order])
```

```python
import jax
import jax.numpy as jnp
from jax import lax
from jax.experimental import pallas as pl
from jax.experimental.pallas import tpu as pltpu

F32 = jnp.float32
BF16 = jnp.bfloat16

N_DEV = 8
HEADS = 8
HEAD_DIM = 64
ATTN_WIDTH = HEADS * HEAD_DIM
CONV_KERNEL = 31
FFN_KERNEL = 3
EPS = 1e-6
BLK = 128
KEY_GROUP = 4
LANES = 128
NORM_ROWS = 128
CONV_HALO = 32
FFN_HALO = 16

ADAM_LR = 0.001
ADAM_B1 = 0.9
ADAM_B2 = 0.999
ADAM_EPS = 1e-08
ADAM_WD = 0.01
ADAM_STEP = 10

VMEM_LIMIT = 56 * 1024 * 1024


def _params(n_axes=0):
    kw = dict(vmem_limit_bytes=VMEM_LIMIT)
    if n_axes:
        kw["dimension_semantics"] = ("arbitrary",) * n_axes
    return pltpu.CompilerParams(**kw)


def _dot(a, b):
    return jnp.dot(a, b, preferred_element_type=F32)


def _dot_nt(a, b):
    return lax.dot_general(a, b, (((1,), (1,)), ((), ())), preferred_element_type=F32)


def _dot_tn(a, b):
    return lax.dot_general(a, b, (((0,), (0,)), ((), ())), preferred_element_type=F32)


def _sigmoid(x):
    return 1.0 / (1.0 + jnp.exp(-x))


def _split_bf16(x):
    hi = x.astype(BF16)
    lo = (x - hi.astype(F32)).astype(BF16)
    return hi, lo


def _pick(n, options):
    for t in options:
        if n % t == 0:
            return t
    return n


def _tile(n, cap):
    best = None
    for t in range(LANES, min(n, cap) + 1, LANES):
        if n % t == 0:
            best = t
    return best or n


def _mm_rms(x, g, wt, out_dtype, sends, name):
    m, k = x.shape
    n = wt.shape[0]
    tm = _tile(m, 2048)
    tn = _tile(n, 512)
    n_send = len(sends)
    grid = (m // tm, n // tn)

    def body(x_ref, g_ref, w_ref, *rest):
        send_refs, (o_ref, h_ref), rest = rest[:n_send], rest[n_send:n_send + 2], rest[n_send + 2:]
        got_refs, h_s, sems = rest[:n_send], rest[n_send], rest[n_send + 1:]
        if n_send:
            start, relay, finish = _two_level_gather(send_refs, got_refs, sems)
            is_first = (pl.program_id(0) == 0) & (pl.program_id(1) == 0)
            is_last = (pl.program_id(0) == grid[0] - 1) & (pl.program_id(1) == grid[1] - 1)
            pl.when(is_first)(start)
            pl.when(is_last)(relay)

        @pl.when(pl.program_id(1) == 0)
        def _():
            def chunk(c, _):
                rows = pl.ds(pl.multiple_of(c * NORM_ROWS, NORM_ROWS), NORM_ROWS)
                xv = x_ref[rows, :]
                r = lax.rsqrt(jnp.mean(xv * xv, axis=-1, keepdims=True) + EPS)
                hv = (xv * r * g_ref[...]).astype(BF16)
                h_s[rows, :] = hv
                h_ref[rows, :] = hv
                return 0

            lax.fori_loop(0, tm // NORM_ROWS, chunk, 0)

        o_ref[...] = _dot_nt(h_s[...], w_ref[...]).astype(out_dtype)
        if n_send:
            pl.when(is_last)(finish)

    out = pl.pallas_call(
        body, name=name, grid=grid,
        in_specs=[pl.BlockSpec((tm, k), lambda i, j: (i, 0)),
                  pl.BlockSpec((1, k), lambda i, j: (0, 0)),
                  pl.BlockSpec((tn, k), lambda i, j: (j, 0))] + [ANY_SPEC] * n_send,
        out_specs=[pl.BlockSpec((tm, tn), lambda i, j: (i, j)),
                   pl.BlockSpec((tm, k), lambda i, j: (i, 0))] + [ANY_SPEC] * n_send,
        out_shape=[jax.ShapeDtypeStruct((m, n), out_dtype), jax.ShapeDtypeStruct((m, k), BF16)]
        + _gathered_shapes(sends),
        scratch_shapes=[pltpu.VMEM((tm, k), BF16)] + (_exchange_sems(n_send) if n_send else []),
        compiler_params=_params(2),
    )(x, g, wt, *sends)
    return out[0], out[1], out[2:]


def _mm_res(a, w, res, name):
    m, k = a.shape
    n = w.shape[1]
    tm = _tile(m, 1024)
    tn = _tile(n, 512)

    def body(a_ref, w_ref, r_ref, o_ref):
        o_ref[...] = r_ref[...] + _dot(a_ref[...], w_ref[...])

    return pl.pallas_call(
        body, name=name, grid=(m // tm, n // tn),
        in_specs=[pl.BlockSpec((tm, k), lambda i, j: (i, 0)),
                  pl.BlockSpec((k, tn), lambda i, j: (0, j)),
                  pl.BlockSpec((tm, tn), lambda i, j: (i, j))],
        out_specs=pl.BlockSpec((tm, tn), lambda i, j: (i, j)),
        out_shape=jax.ShapeDtypeStruct((m, n), F32),
        compiler_params=_params(2),
    )(a, w, res)


def _mm_nt(a, w, out_dtype, name):
    m, k = a.shape
    n = w.shape[0]
    tm = _tile(m, 1024)
    tn = _tile(n, 1408)

    def body(a_ref, w_ref, o_ref):
        o_ref[...] = _dot_nt(a_ref[...], w_ref[...]).astype(out_dtype)

    return pl.pallas_call(
        body, name=name, grid=(m // tm, n // tn),
        in_specs=[pl.BlockSpec((tm, k), lambda i, j: (i, 0)),
                  pl.BlockSpec((tn, k), lambda i, j: (j, 0))],
        out_specs=pl.BlockSpec((tm, tn), lambda i, j: (i, j)),
        out_shape=jax.ShapeDtypeStruct((m, n), out_dtype),
        compiler_params=_params(2),
    )(a, w)


def _column_tiles(a, cap):
    if a.ndim == 2:
        s, c = a.shape
        tc = _tile(c, cap)
        return s, c, tc, lambda rows, index: pl.BlockSpec((rows, tc), lambda *g: (index(*g)[0], index(*g)[1]))
    slabs, s, width = a.shape
    tc = _tile(width, cap)
    per = width // tc
    return s, slabs * width, tc, lambda rows, index: pl.BlockSpec(
        (None, rows, tc), lambda *g: (index(*g)[1] // per, index(*g)[0], index(*g)[1] % per))


def _mm_tn(a, b, name):
    s, m, tm, a_spec = _column_tiles(a, 1408)
    n = b.shape[1]
    tn = _tile(n, 1024)

    def body(a_ref, b_ref, o_ref):
        o_ref[...] = _dot_tn(a_ref[...], b_ref[...]).astype(BF16)

    return pl.pallas_call(
        body, name=name, grid=(m // tm, n // tn),
        in_specs=[a_spec(s, lambda i, j: (0, i)),
                  pl.BlockSpec((s, tn), lambda i, j: (0, j))],
        out_specs=pl.BlockSpec((tm, tn), lambda i, j: (i, j)),
        out_shape=jax.ShapeDtypeStruct((m, n), BF16),
        compiler_params=_params(2),
    )(a, b)


def _mm_rmsbwd(a, w, x, g, dres, sends, name):
    m, k, tk, a_spec = _column_tiles(a, 1408)
    n = w.shape[1]
    tm = _tile(m, 1024)
    nk = k // tk
    n_send = len(sends)

    def body(a_ref, w_ref, x_ref, g_ref, r_ref, *rest):
        send_refs, (dx_ref, dxb_ref, dg_ref), rest = rest[:n_send], rest[n_send:n_send + 3], rest[n_send + 3:]
        got_refs, acc, sems = rest[:n_send], rest[n_send], rest[n_send + 1:]
        i, kk = pl.program_id(0), pl.program_id(1)
        if n_send:
            copies = _scatter_copies(send_refs, got_refs, sems)

            @pl.when((i == 0) & (kk == 0))
            def _():
                for cp in copies:
                    cp.start()

        part = _dot(a_ref[...], w_ref[...])

        @pl.when(kk == 0)
        def _():
            acc[...] = part

        @pl.when(kk > 0)
        def _():
            acc[...] += part

        @pl.when(kk == nk - 1)
        def _():
            def chunk(c, dgp):
                rows = pl.ds(pl.multiple_of(c * NORM_ROWS, NORM_ROWS), NORM_ROWS)
                dh = acc[rows, :]
                xv = x_ref[rows, :]
                r = lax.rsqrt(jnp.mean(xv * xv, axis=-1, keepdims=True) + EPS)
                xh = xv * r
                dxh = dh * g_ref[...]
                dx = r_ref[rows, :] + r * (dxh - xh * jnp.mean(dxh * xh, axis=-1, keepdims=True))
                dx_ref[rows, :] = dx
                dxb_ref[rows, :] = dx.astype(BF16)
                return dgp + jnp.sum(dh * xh, axis=0, keepdims=True)

            dgp = lax.fori_loop(0, tm // NORM_ROWS, chunk, jnp.zeros((1, n), F32))

            @pl.when(i == 0)
            def _():
                dg_ref[...] = dgp

            @pl.when(i > 0)
            def _():
                dg_ref[...] += dgp

        if n_send:
            @pl.when((i == m // tm - 1) & (kk == nk - 1))
            def _():
                for cp in copies:
                    cp.wait()

    out = pl.pallas_call(
        body, name=name, grid=(m // tm, nk),
        in_specs=[a_spec(tm, lambda i, kk: (i, kk)),
                  pl.BlockSpec((tk, n), lambda i, kk: (kk, 0)),
                  pl.BlockSpec((tm, n), lambda i, kk: (i, 0)),
                  pl.BlockSpec((1, n), lambda i, kk: (0, 0)),
                  pl.BlockSpec((tm, n), lambda i, kk: (i, 0))] + [ANY_SPEC] * n_send,
        out_specs=[pl.BlockSpec((tm, n), lambda i, kk: (i, 0)),
                   pl.BlockSpec((tm, n), lambda i, kk: (i, 0)),
                   pl.BlockSpec((1, n), lambda i, kk: (0, 0))] + [ANY_SPEC] * n_send,
        out_shape=[jax.ShapeDtypeStruct((m, n), F32), jax.ShapeDtypeStruct((m, n), BF16),
                   jax.ShapeDtypeStruct((1, n), F32)] + _scattered_shapes(sends),
        scratch_shapes=[pltpu.VMEM((tm, n), F32)] + (_exchange_sems(n_send) if n_send else []),
        compiler_params=_params(2),
    )(a, w, x, g, dres, *sends)
    return out[0], out[1], out[2], out[3:]


ANY_SPEC = pl.BlockSpec(memory_space=pl.ANY)
SEMS_PER_OPERAND = N_DEV - 1


def _exchange_sems(n):
    return [pltpu.SemaphoreType.DMA((n, SEMS_PER_OPERAND)), pltpu.SemaphoreType.DMA((n, SEMS_PER_OPERAND)),
            pltpu.SemaphoreType.DMA((n,))]


def _gathered_shapes(parts):
    return [jax.ShapeDtypeStruct((N_DEV,) + a.shape, a.dtype) for a in parts]


def _scattered_shapes(parts):
    return [jax.ShapeDtypeStruct(a.shape, a.dtype) for a in parts]


def _flat(pos):
    return 4 * pos[0] + 2 * pos[1] + pos[2]


def _remote(src, dst, sems, i, k, to):
    send_sems, recv_sems, _ = sems
    return pltpu.make_async_remote_copy(src_ref=src, dst_ref=dst, send_sem=send_sems.at[i, k],
                                        recv_sem=recv_sems.at[i, k], device_id=to,
                                        device_id_type=pl.DeviceIdType.MESH)


def _scatter_copies(ins, outs, sems):
    x, y, c = lax.axis_index("x"), lax.axis_index("y"), lax.axis_index("c")
    me = _flat((x, y, c))
    copies = [pltpu.make_async_copy(ins[i].at[me], outs[i].at[me], sems[2].at[i]) for i in range(len(ins))]
    for d in range(1, N_DEV):
        peer = (1 - x if d & 4 else x, 1 - y if d & 2 else y, 1 - c if d & 1 else c)
        for i in range(len(ins)):
            copies.append(_remote(ins[i].at[_flat(peer)], outs[i].at[me], sems, i, d - 1, peer))
    return copies


def _two_level_gather(ins, outs, sems):
    x, y, c = lax.axis_index("x"), lax.axis_index("y"), lax.axis_index("c")
    me, sibling = (x, y, c), (x, y, 1 - c)
    chips = [(1 - x, y), (x, 1 - y), (1 - x, 1 - y)]
    n = len(ins)

    def block(i, pos):
        return outs[i].at[_flat(pos)]

    local = [pltpu.make_async_copy(ins[i], block(i, me), sems[2].at[i]) for i in range(n)]
    own = [_remote(ins[i], block(i, me), sems, i, 0, sibling) for i in range(n)]
    own += [_remote(ins[i], block(i, me), sems, i, 1 + j, (*chip, c)) for i in range(n) for j, chip in enumerate(chips)]
    passed = [[_remote(block(i, (*chip, c)), block(i, (*chip, c)), sems, i, 4 + j, sibling) for i in range(n)]
              for j, chip in enumerate(chips)]

    def first():
        for cp in local + own:
            cp.start()

    def relay():
        for j, chip in enumerate(chips):
            for i in range(n):
                _remote(ins[i], block(i, (*chip, c)), sems, i, 1 + j, me).wait_recv()
                passed[j][i].start()

    def finish():
        for i in range(n):
            _remote(ins[i], block(i, sibling), sems, i, 0, me).wait_recv()
            for j, chip in enumerate(chips):
                _remote(ins[i], block(i, (*chip, 1 - c)), sems, i, 4 + j, me).wait_recv()
        for cp in own + [cp for row in passed for cp in row]:
            cp.wait_send()
        for cp in local:
            cp.wait()

    return first, relay, finish


def _gather(parts, name):
    n = len(parts)

    def body(*refs):
        start, relay, finish = _two_level_gather(refs[:n], refs[n:2 * n], refs[2 * n:])
        start()
        relay()
        finish()

    return pl.pallas_call(
        body, name=name, in_specs=[ANY_SPEC] * n, out_specs=[ANY_SPEC] * n,
        out_shape=_gathered_shapes(parts), scratch_shapes=_exchange_sems(n),
    )(*parts)


def _tri(kind):
    j = lax.broadcasted_iota(jnp.int32, (BLK, BLK), 0)
    s = lax.broadcasted_iota(jnp.int32, (BLK, BLK), 1)
    m = {"after": j > s, "upto": j <= s, "before": j < s}[kind]
    return jnp.concatenate([jnp.where(m, 1.0, 0.0), jnp.ones((BLK, BLK), F32)], axis=1).astype(BF16)


def _scan_rows(v, tri):
    r = _dot(v.astype(BF16), tri)
    return r[:, :BLK], r[:, BLK:]


HEADS_PER_STEP = LANES // HEAD_DIM


def _first_head_lanes():
    return lax.broadcasted_iota(jnp.int32, (1, LANES), 1) < HEAD_DIM


def _pair_mean(v, first):
    m0 = jnp.sum(jnp.where(first, v, 0.0), axis=-1, keepdims=True)
    m1 = jnp.sum(jnp.where(first, 0.0, v), axis=-1, keepdims=True)
    return jnp.where(first, m0, m1) * (1.0 / HEAD_DIM)


def _pair_norm(v, g2, first):
    return v * lax.rsqrt(_pair_mean(v * v, first) + EPS) * g2


def _pair_norm_bwd(raw, g2, dn, first):
    r = lax.rsqrt(_pair_mean(raw * raw, first) + EPS)
    xh = raw * r
    dxh = dn * g2
    return r * (dxh - xh * _pair_mean(dxh * xh, first)), jnp.sum(dn * xh, axis=0, keepdims=True)


def _block_diag(v, first):
    zero = jnp.zeros_like(v)
    return jnp.concatenate([jnp.where(first, v, zero), jnp.where(first, zero, v)], axis=0)


def _attn_prep(q_ref, k_ref, v_ref, qg_ref, kg_ref, qc_s, kc_s, vd_s, kd_s, n_blk):
    scale = HEAD_DIM ** -0.5
    first = _first_head_lanes()

    def prep(i, _):
        rows = pl.ds(pl.multiple_of(i * BLK, BLK), BLK)
        both = pl.ds(pl.multiple_of(i * 2 * BLK, 2 * BLK), 2 * BLK)
        qh, ql = _split_bf16(_pair_norm(q_ref[rows, :], qg_ref[...], first) * scale)
        kh, kl = _split_bf16(_pair_norm(k_ref[rows, :], kg_ref[...], first))
        for h in range(HEADS_PER_STEP):
            sl = slice(h * HEAD_DIM, (h + 1) * HEAD_DIM)
            qc_s[h, rows, :] = jnp.concatenate([qh[:, sl], ql[:, sl], qh[:, sl], ql[:, sl]], axis=1)
            kc_s[h, rows, :] = jnp.concatenate([kh[:, sl], kh[:, sl], kl[:, sl], kl[:, sl]], axis=1)
        vd_s[both, :] = _block_diag(v_ref[rows, :].astype(BF16), first)
        if kd_s is not None:
            kd_s[both, :] = _block_diag(kh, first)
        return 0

    lax.fori_loop(0, n_blk, prep, 0)


def _pair_scores(qc, kc_ref, grp, n_tiles=KEY_GROUP):
    zs = []
    for j in range(0, n_tiles, 2):
        two = pl.ds(pl.multiple_of((grp * KEY_GROUP + j) * BLK, 2 * BLK), 2 * BLK)
        z = _dot_nt(qc, kc_ref[two, :])
        zs += [z[:, :BLK], z[:, BLK:]]
    return zs[:n_tiles]


def _col_minus_row():
    row = lax.broadcasted_iota(jnp.int32, (BLK, BLK), 0)
    col = lax.broadcasted_iota(jnp.int32, (BLK, BLK), 1)
    return col - row


def _softplus(z):
    return jnp.maximum(z, 0.0) + jnp.log(1.0 + jnp.exp(-jnp.abs(z)))


def _attn_fwd(proj, qg, kg, sends, name):
    s = proj.shape[0]
    n_blk = s // BLK
    pairs = ATTN_WIDTH // LANES
    n_send = len(sends)

    def body(q_ref, k_ref, v_ref, qg_ref, kg_ref, *rest):
        send_refs, (o_ref, t_ref), rest = rest[:n_send], rest[n_send:n_send + 2], rest[n_send + 2:]
        got_refs, (qc_s, kc_s, vd_s), sems = rest[:n_send], rest[n_send:n_send + 3], rest[n_send + 3:]
        start, relay, finish = _two_level_gather(send_refs, got_refs, sems)
        step = pl.program_id(0)
        pl.when(step == 0)(start)
        pl.when(step == pairs - 1)(relay)
        tri = _tri("after")
        diff = _col_minus_row()
        first = _first_head_lanes()
        heads = range(HEADS_PER_STEP)
        _attn_prep(q_ref, k_ref, v_ref, qg_ref, kg_ref, qc_s, kc_s, vd_s, None, n_blk)

        below = diff < 0

        def group(qc, grp, carry, acc, n_tiles, diagonal):
            blocks = [grp * KEY_GROUP + j for j in reversed(range(n_tiles))]
            zs = [_pair_scores(qc[h], kc_s.at[h], grp, n_tiles)[::-1] for h in heads]
            parts = [[None] * n_tiles for _ in heads]
            for h in heads:
                for j, z in enumerate(zs[h]):
                    sp = _softplus(z)
                    lom = -sp
                    if diagonal and j == 0:
                        lom = jnp.where(below, lom, 0.0)
                    tail, tot = _scan_rows(lom, tri)
                    parts[h][j] = (z - sp + tail, tot)
            carry = list(carry)
            for j, kb in enumerate(blocks):
                ws = []
                for h in heads:
                    lw, tot = parts[h][j]
                    w = jnp.exp(lw + carry[h])
                    if diagonal and j == 0:
                        w = jnp.where(below, w, 0.0)
                    ws.append(w.astype(BF16))
                    carry[h] = carry[h] + tot
                acc = acc + _dot(jnp.concatenate(ws, axis=1),
                                 vd_s[pl.ds(pl.multiple_of(kb * 2 * BLK, 2 * BLK), 2 * BLK), :])
            return tuple(carry), acc

        def q_blocks(top, _):
            for r in range(KEY_GROUP):
                rows = pl.ds(pl.multiple_of((top * KEY_GROUP + r) * BLK, BLK), BLK)
                qc = [qc_s[h, rows, :] for h in heads]
                zero = jnp.zeros((BLK, BLK), F32)
                carry, acc = group(qc, top, (zero,) * HEADS_PER_STEP, jnp.zeros((BLK, LANES), F32), r + 1, True)
                carry, acc = lax.fori_loop(
                    0, top, lambda t, c: group(qc, top - 1 - t, c[0], c[1], KEY_GROUP, False), (carry, acc))
                o_ref[rows, :] = acc.astype(BF16)
                t_ref[rows, :] = jnp.where(first, carry[0], carry[1])
            return 0

        lax.fori_loop(0, n_blk // KEY_GROUP, q_blocks, 0)
        pl.when(step == pairs - 1)(finish)

    col = lambda off: pl.BlockSpec((s, LANES), lambda p: (0, off + p))
    vec = pl.BlockSpec((1, LANES), lambda p: (0, 0))
    out = pl.pallas_call(
        body, name=name, grid=(pairs,),
        in_specs=[col(0), col(pairs), col(2 * pairs), vec, vec] + [ANY_SPEC] * n_send,
        out_specs=[pl.BlockSpec((s, LANES), lambda p: (0, p))] * 2 + [ANY_SPEC] * n_send,
        out_shape=[jax.ShapeDtypeStruct((s, ATTN_WIDTH), BF16), jax.ShapeDtypeStruct((s, ATTN_WIDTH), F32)]
        + _gathered_shapes(sends),
        scratch_shapes=[pltpu.VMEM((HEADS_PER_STEP, s, 4 * HEAD_DIM), BF16)] * 2
        + [pltpu.VMEM((HEADS_PER_STEP * s, LANES), BF16)] + _exchange_sems(n_send),
        compiler_params=_params(1),
    )(proj, proj, proj, qg, kg, *sends)
    return out[0], out[1], out[2:]


def _attn_bwd(proj, dcat, tsum, qg, kg, sends, name):
    s = proj.shape[0]
    n_blk = s // BLK
    pairs = ATTN_WIDTH // LANES
    scale = HEAD_DIM ** -0.5
    n_send = len(sends)
    n_scratch = 7

    def body(q_ref, k_ref, v_ref, do_ref, t_ref, qg_ref, kg_ref, *rest):
        send_refs, rest = rest[:n_send], rest[n_send:]
        (dq_ref, dk_ref, dv_ref, dqg_ref, dkg_ref), rest = rest[:5], rest[5:]
        got_refs, scratch, sems = rest[:n_send], rest[n_send:n_send + n_scratch], rest[n_send + n_scratch:]
        qc_s, kc_s, vd_s, kd_s, qd_s, dob_s, dkv_s = scratch
        copies = _scatter_copies(send_refs, got_refs, sems)

        @pl.when(pl.program_id(0) == 0)
        def _():
            for cp in copies:
                cp.start()

        tri_p = _tri("upto")
        tri_h = _tri("before")
        diff = _col_minus_row()

        @pl.when(pl.program_id(0) == 0)
        def _():
            dqg_ref[...] = jnp.zeros_like(dqg_ref)
            dkg_ref[...] = jnp.zeros_like(dkg_ref)

        first = _first_head_lanes()
        heads = range(HEADS_PER_STEP)
        _attn_prep(q_ref, k_ref, v_ref, qg_ref, kg_ref, qc_s, kc_s, vd_s, kd_s, n_blk)

        def prep(i, _):
            rows = pl.ds(pl.multiple_of(i * BLK, BLK), BLK)
            both = pl.ds(pl.multiple_of(i * 2 * BLK, 2 * BLK), 2 * BLK)
            dob = do_ref[rows, :].astype(BF16)
            dob_s[rows, :] = dob
            none = jnp.zeros((BLK, HEAD_DIM), BF16)
            for h in heads:
                qd_s[h, both, :] = jnp.concatenate(
                    [jnp.concatenate([qc_s[h, rows, 0:HEAD_DIM], none], axis=1),
                     jnp.concatenate([none, dob[:, h * HEAD_DIM:(h + 1) * HEAD_DIM]], axis=1)], axis=0)
                dkv_s[h, rows, :] = jnp.zeros((BLK, LANES), F32)
            return 0

        lax.fori_loop(0, n_blk, prep, 0)

        below = diff < 0

        def group(qc, qd, dob, tq, grp, pc, hc, dq, n_tiles, diagonal):
            blocks = [grp * KEY_GROUP + j for j in range(n_tiles)]
            cols_of = [pl.ds(pl.multiple_of(kb * BLK, BLK), BLK) for kb in blocks]
            both_of = [pl.ds(pl.multiple_of(kb * 2 * BLK, 2 * BLK), 2 * BLK) for kb in blocks]
            on_diagonal = [diagonal and j == n_tiles - 1 for j in range(n_tiles)]
            zs = [_pair_scores(qc[h], kc_s.at[h], grp, n_tiles) for h in heads]
            das = [_dot_nt(dob, vd_s[both, :]) for both in both_of]
            lbs = [[None] * n_tiles for _ in heads]
            scans = [[None] * n_tiles for _ in heads]
            for h in heads:
                for j, z in enumerate(zs[h]):
                    sp = _softplus(z)
                    lom = -sp
                    if on_diagonal[j]:
                        lom = jnp.where(below, lom, 0.0)
                    lbs[h][j] = z - sp
                    scans[h][j] = _scan_rows(lom, tri_p)
            pc, hc = list(pc), list(hc)
            avs = [[None] * n_tiles for _ in heads]
            gws = [[None] * n_tiles for _ in heads]
            hscans = [[None] * n_tiles for _ in heads]
            for h in heads:
                for j in range(n_tiles):
                    p_in, p_tot = scans[h][j]
                    a = jnp.exp(lbs[h][j] + (tq[h] - pc[h] - p_in))
                    if on_diagonal[j]:
                        a = jnp.where(below, a, 0.0)
                    pc[h] = pc[h] + p_tot
                    gw = das[j][:, h * BLK:(h + 1) * BLK] * a
                    avs[h][j] = a.astype(BF16)
                    gws[h][j] = gw
                    hscans[h][j] = _scan_rows(gw, tri_h)
            dzs = [[None] * n_tiles for _ in heads]
            for h in heads:
                for j in range(n_tiles):
                    h_in, g_tot = hscans[h][j]
                    gw = gws[h][j]
                    dz = gw - jnp.exp(lbs[h][j]) * (gw + hc[h] + h_in)
                    if on_diagonal[j]:
                        dz = jnp.where(below, dz, 0.0)
                    hc[h] = hc[h] + g_tot
                    dzs[h][j] = dz.astype(BF16)
            for j, both in enumerate(both_of):
                dq = dq + _dot(jnp.concatenate([dzs[h][j] for h in heads], axis=1), kd_s[both, :])
            for h in heads:
                for j, cols in enumerate(cols_of):
                    dkv_s[h, cols, :] += _dot_tn(jnp.concatenate([dzs[h][j], avs[h][j]], axis=0), qd[h])
            return tuple(pc), tuple(hc), dq

        def q_blocks(top, dqg):
            for r in range(KEY_GROUP):
                qi = top * KEY_GROUP + r
                rows = pl.ds(pl.multiple_of(qi * BLK, BLK), BLK)
                both = pl.ds(pl.multiple_of(qi * 2 * BLK, 2 * BLK), 2 * BLK)
                qc = [qc_s[h, rows, :] for h in heads]
                qd = [qd_s[h, both, :] for h in heads]
                dob = dob_s[rows, :]
                tboth = t_ref[rows, :]
                tq = [jnp.concatenate([tboth[:, h * HEAD_DIM:(h + 1) * HEAD_DIM]] * 2, axis=1) for h in heads]
                zero = (jnp.zeros((BLK, BLK), F32),) * HEADS_PER_STEP
                pc, hc, dq = lax.fori_loop(
                    0, top, lambda grp, c: group(qc, qd, dob, tq, grp, c[0], c[1], c[2], KEY_GROUP, False),
                    (zero, zero, jnp.zeros((BLK, LANES), F32)))
                _, _, dq = group(qc, qd, dob, tq, top, pc, hc, dq, r + 1, True)
                dq_raw, dg = _pair_norm_bwd(q_ref[rows, :], qg_ref[...], dq * scale, first)
                dq_ref[rows, :] = dq_raw.astype(BF16)
                dqg = dqg + dg
            return dqg

        dqg = lax.fori_loop(0, n_blk // KEY_GROUP, q_blocks, jnp.zeros((1, LANES), F32))

        def finish(i, dkg):
            rows = pl.ds(pl.multiple_of(i * BLK, BLK), BLK)
            dk = jnp.concatenate([dkv_s[h, rows, 0:HEAD_DIM] for h in heads], axis=1)
            dv = jnp.concatenate([dkv_s[h, rows, HEAD_DIM:2 * HEAD_DIM] for h in heads], axis=1)
            dk_raw, dg = _pair_norm_bwd(k_ref[rows, :], kg_ref[...], dk, first)
            dk_ref[rows, :] = dk_raw.astype(BF16)
            dv_ref[rows, :] = dv.astype(BF16)
            return dkg + dg

        dkg = lax.fori_loop(0, n_blk, finish, jnp.zeros((1, LANES), F32))
        dqg_ref[0:1, :] += dqg
        dkg_ref[0:1, :] += dkg

        @pl.when(pl.program_id(0) == pairs - 1)
        def _():
            for cp in copies:
                cp.wait()

    col = lambda off: pl.BlockSpec((s, LANES), lambda p: (0, off + p))
    vec = pl.BlockSpec((1, LANES), lambda p: (0, 0))
    small = pl.BlockSpec((8, LANES), lambda p: (0, 0))
    out = pl.pallas_call(
        body, name=name, grid=(pairs,),
        in_specs=[col(0), col(pairs), col(2 * pairs), col(0), col(0), vec, vec] + [ANY_SPEC] * n_send,
        out_specs=[col(0)] * 3 + [small] * 2 + [ANY_SPEC] * n_send,
        out_shape=[jax.ShapeDtypeStruct((s, ATTN_WIDTH), BF16)] * 3 + [jax.ShapeDtypeStruct((8, LANES), F32)] * 2
        + _scattered_shapes(sends),
        scratch_shapes=[pltpu.VMEM((HEADS_PER_STEP, s, 4 * HEAD_DIM), BF16)] * 2
        + [pltpu.VMEM((HEADS_PER_STEP * s, LANES), BF16)] * 2
        + [pltpu.VMEM((HEADS_PER_STEP, HEADS_PER_STEP * s, LANES), BF16), pltpu.VMEM((s, LANES), BF16),
           pltpu.VMEM((HEADS_PER_STEP, s, LANES), F32)] + _exchange_sems(n_send),
        compiler_params=_params(1),
    )(proj, proj, proj, dcat, tsum, qg, kg, *sends)
    return out[:5], out[5:]


CONV_ROWS = 128


def _shifted(window, shift, halo):
    if shift == 0:
        return window[halo:, :]
    return pltpu.roll(window, shift, 0)[halo:, :]


SUBLANES = 8


def _row_shifts(window, up):
    n = window.shape[0]
    return [window] + [pltpu.roll(window, n - b if up else b, 0) for b in range(1, SUBLANES)]


def _earlier(shifts, back, rows):
    a, b = divmod(back, SUBLANES)
    return shifts[b][CONV_HALO - SUBLANES * a:CONV_HALO - SUBLANES * a + rows, :]


def _later(shifts, ahead, rows):
    a, b = divmod(ahead, SUBLANES)
    return shifts[b][SUBLANES * a:SUBLANES * a + rows, :]


def _lane_blocks(width):
    return [slice(c, c + LANES) for c in range(0, width, LANES)]


def _conv_taps(shifts, w_ref, lanes, rows):
    y = None
    for k in range(CONV_KERNEL):
        term = _earlier(shifts, CONV_KERNEL - 1 - k, rows) * w_ref[k:k + 1, lanes]
        y = term if y is None else y + term
    return y


def _conv_fwd(proj, w, b, lg, lb, name):
    s = proj.shape[0]
    cw = w.shape[1]
    rows = CONV_ROWS
    blk_a = (proj.shape[1] - 2 * cw) // cw

    def body(a_ref, g_ref, w_ref, b_ref, lg_ref, lb_ref, o_ref, y_ref, u_s):
        u_s[0:CONV_HALO, :] = jnp.zeros((CONV_HALO, cw), F32)

        def glu(i, _):
            r0 = pl.multiple_of(i * rows, rows)
            u_s[pl.ds(CONV_HALO + r0, rows), :] = a_ref[pl.ds(r0, rows), :] * _sigmoid(g_ref[pl.ds(r0, rows), :])
            return 0

        lax.fori_loop(0, s // rows, glu, 0)

        def chunk(i, _):
            r0 = pl.multiple_of(i * rows, rows)
            for lanes in _lane_blocks(cw):
                shifts = _row_shifts(u_s[pl.ds(r0, CONV_HALO + rows), lanes], False)
                y_ref[pl.ds(r0, rows), lanes] = _conv_taps(shifts, w_ref, lanes, rows) + b_ref[:, lanes]
            y = y_ref[pl.ds(r0, rows), :]
            yc = y - jnp.mean(y, axis=-1, keepdims=True)
            n = yc * lax.rsqrt(jnp.mean(yc * yc, axis=-1, keepdims=True) + EPS)
            ln = n * lg_ref[...] + lb_ref[...]
            o_ref[pl.ds(r0, rows), :] = (ln * _sigmoid(ln)).astype(BF16)
            return 0

        lax.fori_loop(0, s // rows, chunk, 0)

    vec = pl.BlockSpec((1, cw), lambda i: (0, 0))
    return pl.pallas_call(
        body, name=name, grid=(1,),
        in_specs=[pl.BlockSpec((s, cw), lambda i: (0, blk_a)), pl.BlockSpec((s, cw), lambda i: (0, blk_a + 1)),
                  pl.BlockSpec((CONV_KERNEL, cw), lambda i: (0, 0)), vec, vec, vec],
        out_specs=[pl.BlockSpec((s, cw), lambda i: (0, 0))] * 2,
        out_shape=[jax.ShapeDtypeStruct((s, cw), BF16), jax.ShapeDtypeStruct((s, cw), F32)],
        scratch_shapes=[pltpu.VMEM((CONV_HALO + s, cw), F32)],
        compiler_params=_params(1),
    )(proj, proj, w, b, lg, lb)


def _conv_bwd(proj, y, dcat, w, lg, lb, sends, name):
    s = proj.shape[0]
    cw = w.shape[1]
    rows = CONV_ROWS
    blk_a = (proj.shape[1] - 2 * cw) // cw
    n_chunk = s // rows
    n_send = len(sends)

    def body(a_ref, g_ref, y_ref, dc_ref, w_ref, lg_ref, lb_ref, *rest):
        send_refs, (o_ref, dw_ref, db_ref, dlg_ref, dlb_ref), rest = rest[:n_send], rest[n_send:n_send + 5], rest[n_send + 5:]
        got_refs, (u_s, dy_s, dw_s), sems = rest[:n_send], rest[n_send:n_send + 3], rest[n_send + 3:]
        copies = _scatter_copies(send_refs, got_refs, sems)
        for cp in copies:
            cp.start()
        u_s[0:CONV_HALO, :] = jnp.zeros((CONV_HALO, cw), F32)
        dy_s[pl.ds(s, CONV_HALO), :] = jnp.zeros((CONV_HALO, cw), F32)
        dw_s[...] = jnp.zeros_like(dw_s)

        def glu(i, _):
            r0 = pl.multiple_of(i * rows, rows)
            u_s[pl.ds(CONV_HALO + r0, rows), :] = a_ref[pl.ds(r0, rows), :] * _sigmoid(g_ref[pl.ds(r0, rows), :])
            return 0

        lax.fori_loop(0, n_chunk, glu, 0)

        def chunk(i, carry):
            db, dlg, dlb = carry
            r0 = pl.multiple_of(i * rows, rows)
            y = y_ref[pl.ds(r0, rows), :]
            yc = y - jnp.mean(y, axis=-1, keepdims=True)
            r = lax.rsqrt(jnp.mean(yc * yc, axis=-1, keepdims=True) + EPS)
            n = yc * r
            ln = n * lg_ref[...] + lb_ref[...]
            sg = _sigmoid(ln)
            dln = dc_ref[pl.ds(r0, rows), :] * (sg * (1.0 + ln * (1.0 - sg)))
            dn = dln * lg_ref[...]
            dy = r * (dn - jnp.mean(dn, axis=-1, keepdims=True) - n * jnp.mean(dn * n, axis=-1, keepdims=True))
            dy_s[pl.ds(r0, rows), :] = dy
            for lanes in _lane_blocks(cw):
                shifts = _row_shifts(u_s[pl.ds(r0, CONV_HALO + rows), lanes], False)
                dy_part = dy[:, lanes]
                for k in range(CONV_KERNEL):
                    prod = _earlier(shifts, CONV_KERNEL - 1 - k, rows) * dy_part
                    dw_s[k, :, lanes] += jnp.sum(prod.reshape(rows // SUBLANES, SUBLANES, LANES), axis=0)
            return (db + jnp.sum(dy, axis=0, keepdims=True),
                    dlg + jnp.sum(dln * n, axis=0, keepdims=True),
                    dlb + jnp.sum(dln, axis=0, keepdims=True))

        zero = jnp.zeros((1, cw), F32)
        db, dlg, dlb = lax.fori_loop(0, n_chunk, chunk, (zero, zero, zero))
        db_ref[...] = db
        dlg_ref[...] = dlg
        dlb_ref[...] = dlb
        for k in range(CONV_KERNEL):
            dw_ref[k:k + 1, :] = jnp.sum(dw_s[k], axis=0, keepdims=True)

        def back(i, _):
            r0 = pl.multiple_of(i * rows, rows)
            for lanes in _lane_blocks(cw):
                shifts = _row_shifts(dy_s[pl.ds(r0, rows + CONV_HALO), lanes], True)
                du = None
                for k in range(CONV_KERNEL):
                    term = _later(shifts, CONV_KERNEL - 1 - k, rows) * w_ref[k:k + 1, lanes]
                    du = term if du is None else du + term
                av = a_ref[pl.ds(r0, rows), lanes]
                sg = _sigmoid(g_ref[pl.ds(r0, rows), lanes])
                o_ref[pl.ds(r0, rows), lanes] = (du * sg).astype(BF16)
                o_ref[pl.ds(r0, rows), slice(cw + lanes.start, cw + lanes.stop)] = (du * av * sg * (1.0 - sg)).astype(BF16)
            return 0

        lax.fori_loop(0, n_chunk, back, 0)
        for cp in copies:
            cp.wait()

    vec = pl.BlockSpec((1, cw), lambda i: (0, 0))
    wspec = pl.BlockSpec((CONV_KERNEL, cw), lambda i: (0, 0))
    out = pl.pallas_call(
        body, name=name, grid=(1,),
        in_specs=[pl.BlockSpec((s, cw), lambda i: (0, blk_a)), pl.BlockSpec((s, cw), lambda i: (0, blk_a + 1)),
                  pl.BlockSpec((s, cw), lambda i: (0, 0)), pl.BlockSpec((s, cw), lambda i: (0, 1)), wspec, vec, vec]
        + [ANY_SPEC] * n_send,
        out_specs=[pl.BlockSpec((s, 2 * cw), lambda i: (0, 0)), wspec, vec, vec, vec] + [ANY_SPEC] * n_send,
        out_shape=[jax.ShapeDtypeStruct((s, 2 * cw), BF16), jax.ShapeDtypeStruct((CONV_KERNEL, cw), F32)]
        + [jax.ShapeDtypeStruct((1, cw), F32)] * 3 + _scattered_shapes(sends),
        scratch_shapes=[pltpu.VMEM((CONV_HALO + s, cw), F32), pltpu.VMEM((s + CONV_HALO, cw), F32),
                        pltpu.VMEM((CONV_KERNEL, 8, cw), F32)] + _exchange_sems(n_send),
        compiler_params=_params(1),
    )(proj, proj, y, dcat, w, lg, lb, *sends)
    return out[:5], out[5:]


FFN_ROWS = 256


def _ffn_gate(g_ref, r0, rows, w_ref, b_ref):
    cur = g_ref[pl.ds(r0, rows), :].astype(F32)
    prev = g_ref[pl.ds(pl.multiple_of(jnp.maximum(r0 - FFN_HALO, 0), FFN_HALO), FFN_HALO), :].astype(F32)
    prev = jnp.where(r0 > 0, prev, 0.0)
    window = jnp.concatenate([prev, cur], axis=0)
    gc = cur * w_ref[FFN_KERNEL - 1:FFN_KERNEL, :] + b_ref[...]
    for k in range(FFN_KERNEL - 1):
        gc = gc + _shifted(window, FFN_KERNEL - 1 - k, FFN_HALO) * w_ref[k:k + 1, :]
    return gc, window


def _ffn_fwd(up, w, b, name):
    s = up.shape[0]
    f = w.shape[1]
    tc = _pick(f, (256, 128))
    nc = f // tc
    rows = _pick(s, (FFN_ROWS, 128))

    def body(g_ref, v_ref, w_ref, b_ref, o_ref):
        def chunk(i, _):
            r0 = pl.multiple_of(i * rows, rows)
            gc, _w = _ffn_gate(g_ref, r0, rows, w_ref, b_ref)
            o_ref[pl.ds(r0, rows), :] = (gc * _sigmoid(gc) * v_ref[pl.ds(r0, rows), :].astype(F32)).astype(BF16)
            return 0

        lax.fori_loop(0, s // rows, chunk, 0)

    return pl.pallas_call(
        body, name=name, grid=(nc,),
        in_specs=[pl.BlockSpec((s, tc), lambda j: (0, j)), pl.BlockSpec((s, tc), lambda j: (0, nc + j)),
                  pl.BlockSpec((FFN_KERNEL, tc), lambda j: (0, j)), pl.BlockSpec((1, tc), lambda j: (0, j))],
        out_specs=pl.BlockSpec((s, tc), lambda j: (0, j)),
        out_shape=jax.ShapeDtypeStruct((s, f), BF16),
        compiler_params=_params(1),
    )(up, up, w, b)


def _ffn_bwd(up, dact, w, b, name):
    s = up.shape[0]
    f = w.shape[1]
    tc = _pick(f, (256, 128))
    nc = f // tc
    rows = _pick(s, (FFN_ROWS, 128))
    n_chunk = s // rows

    def body(g_ref, v_ref, da_ref, w_ref, b_ref, d_ref, dw_ref, db_ref, dgc_s):
        dg_ref, dv_ref = d_ref.at[0], d_ref.at[1]
        dgc_s[pl.ds(s, FFN_HALO), :] = jnp.zeros((FFN_HALO, tc), F32)

        def chunk(i, carry):
            r0 = pl.multiple_of(i * rows, rows)
            gc, window = _ffn_gate(g_ref, r0, rows, w_ref, b_ref)
            sg = _sigmoid(gc)
            da = da_ref[pl.ds(r0, rows), :].astype(F32)
            dv_ref[pl.ds(r0, rows), :] = (da * gc * sg).astype(BF16)
            dgc = da * v_ref[pl.ds(r0, rows), :].astype(F32) * (sg * (1.0 + gc * (1.0 - sg)))
            dgc_s[pl.ds(r0, rows), :] = dgc
            out = [carry[0] + jnp.sum(dgc, axis=0, keepdims=True)]
            for k in range(FFN_KERNEL):
                out.append(carry[1 + k] + jnp.sum(_shifted(window, FFN_KERNEL - 1 - k, FFN_HALO) * dgc,
                                                  axis=0, keepdims=True))
            return tuple(out)

        zero = jnp.zeros((1, tc), F32)
        sums = lax.fori_loop(0, n_chunk, chunk, (zero,) * (1 + FFN_KERNEL))
        db_ref[...] = sums[0]
        for k in range(FFN_KERNEL):
            dw_ref[k:k + 1, :] = sums[1 + k]

        def back(i, _):
            r0 = pl.multiple_of(i * rows, rows)
            window = dgc_s[pl.ds(r0, rows + FFN_HALO), :]
            dg = window[:rows, :] * w_ref[FFN_KERNEL - 1:FFN_KERNEL, :]
            for k in range(FFN_KERNEL - 1):
                sh = FFN_KERNEL - 1 - k
                dg = dg + pltpu.roll(window, rows + FFN_HALO - sh, 0)[:rows, :] * w_ref[k:k + 1, :]
            dg_ref[pl.ds(r0, rows), :] = dg.astype(BF16)
            return 0

        lax.fori_loop(0, n_chunk, back, 0)

    blk = lambda off: pl.BlockSpec((s, tc), lambda j: (0, off + j))
    return pl.pallas_call(
        body, name=name, grid=(nc,),
        in_specs=[blk(0), blk(nc), blk(0), pl.BlockSpec((FFN_KERNEL, tc), lambda j: (0, j)),
                  pl.BlockSpec((1, tc), lambda j: (0, j))],
        out_specs=[pl.BlockSpec((2, s, tc), lambda j: (0, 0, j)), pl.BlockSpec((FFN_KERNEL, tc), lambda j: (0, j)),
                   pl.BlockSpec((1, tc), lambda j: (0, j))],
        out_shape=[jax.ShapeDtypeStruct((2, s, f), BF16),
                   jax.ShapeDtypeStruct((FFN_KERNEL, f), F32), jax.ShapeDtypeStruct((1, f), F32)],
        scratch_shapes=[pltpu.VMEM((s + FFN_HALO, tc), F32)],
        compiler_params=_params(1),
    )(up, up, dact, w, b)


def _loss_head(y, target, name):
    m, n = y.shape
    tm = _pick(m, (256, 128))

    def body(y_ref, t_ref, l_ref, d_ref, db_ref):
        e = y_ref[...] - t_ref[...]
        part = 0.5 * jnp.sum(jnp.sum(e * e, axis=-1, keepdims=True) / n, axis=0, keepdims=True)

        @pl.when(pl.program_id(0) == 0)
        def _():
            l_ref[...] = jnp.zeros_like(l_ref)

        l_ref[...] += part
        d = e / n
        d_ref[...] = d
        db_ref[...] = d.astype(BF16)

    return pl.pallas_call(
        body, name=name, grid=(m // tm,),
        in_specs=[pl.BlockSpec((tm, n), lambda i: (i, 0))] * 2,
        out_specs=[pl.BlockSpec((8, LANES), lambda i: (0, 0)), pl.BlockSpec((tm, n), lambda i: (i, 0)),
                   pl.BlockSpec((tm, n), lambda i: (i, 0))],
        out_shape=[jax.ShapeDtypeStruct((8, LANES), F32), jax.ShapeDtypeStruct((m, n), F32),
                   jax.ShapeDtypeStruct((m, n), BF16)],
        compiler_params=_params(1),
    )(y, target)


def _adamw_math(w, g, m, v):
    m = ADAM_B1 * m + (1.0 - ADAM_B1) * g
    v = ADAM_B2 * v + (1.0 - ADAM_B2) * (g * g)
    m_hat = m / (1.0 - ADAM_B1 ** ADAM_STEP)
    v_hat = v / (1.0 - ADAM_B2 ** ADAM_STEP)
    delta = -ADAM_LR * (m_hat / (jnp.sqrt(v_hat) + ADAM_EPS) + ADAM_WD * w)
    return delta, m, v


def _sum_adamw(parts, w, m, v, scatters, sends, name):
    depth, r, c = w.shape
    n_scatter = len(scatters)
    parts = [list(p) if isinstance(p, (list, tuple)) else [p] for p in parts]
    tr = max(t for t in range(16, min(min(a.shape[1] for p in parts for a in p), 192) + 1, 16)
             if all(a.shape[1] % t == 0 for p in parts for a in p))
    steps = r // tr
    n_send = len(sends)
    chunks = []
    for layer, p in enumerate(parts):
        s0 = 0
        for a in p:
            chunks.append((layer, s0, a.shape[1] // tr))
            s0 += a.shape[1] // tr
    n_chunk = len(chunks)

    def body(*refs):
        n_both = n_scatter + n_send
        p_refs, refs = refs[:n_chunk], refs[n_chunk:]
        (w_ref, m_ref, v_ref), send_refs, refs = refs[:3], refs[3:3 + n_both], refs[3 + n_both:]
        (g_out, d_out, m_out, v_out), got_refs, sems = refs[:4], refs[4:4 + n_both], refs[4 + n_both:]
        is_first = (pl.program_id(0) == 0) & (pl.program_id(1) == 0)
        is_last = (pl.program_id(0) == depth - 1) & (pl.program_id(1) == steps - 1)
        if n_scatter:
            copies = _scatter_copies(send_refs[:n_scatter], got_refs[:n_scatter], sems[:3])

            @pl.when(is_first)
            def _():
                for cp in copies:
                    cp.start()
        if n_send:
            start, relay, finish = _two_level_gather(send_refs[n_scatter:], got_refs[n_scatter:],
                                                     sems[3 if n_scatter else 0:])
            pl.when(is_first)(start)
            pl.when(is_last)(relay)
        for p_ref, (layer, s0, n) in zip(p_refs, chunks):
            step = pl.program_id(1)

            @pl.when((pl.program_id(0) == layer) & (step >= s0) & (step < s0 + n))
            def _():
                g = p_ref[0].astype(F32)
                for src in range(1, N_DEV):
                    g = g + p_ref[src].astype(F32)
                d, mn, vn = _adamw_math(w_ref[0], g, m_ref[0], v_ref[0])
                g_out[0] = g
                d_out[0] = d
                m_out[0] = mn
                v_out[0] = vn

        if n_scatter:
            @pl.when(is_last)
            def _():
                for cp in copies:
                    cp.wait()
        if n_send:
            pl.when(is_last)(finish)

    def part_spec(layer, s0, n):
        return pl.BlockSpec((N_DEV, tr, c), lambda l, i: (0, jnp.clip((l - layer) * steps + i - s0, 0, n - 1), 0))

    blk = pl.BlockSpec((1, tr, c), lambda l, i: (l, i, 0))
    out = pl.pallas_call(
        body, name=name, grid=(depth, steps),
        in_specs=[part_spec(*ch) for ch in chunks] + [blk, blk, blk] + [ANY_SPEC] * (n_scatter + n_send),
        out_specs=[blk] * 4 + [ANY_SPEC] * (n_scatter + n_send),
        out_shape=[jax.ShapeDtypeStruct(w.shape, F32)] * 4 + _scattered_shapes(scatters) + _gathered_shapes(sends),
        scratch_shapes=(_exchange_sems(n_scatter) if n_scatter else []) + (_exchange_sems(n_send) if n_send else []),
        compiler_params=_params(2),
    )(*[a for p in parts for a in p], w, m, v, *scatters, *sends)
    return out[:4], out[4:]


VMEM_SPEC = pl.BlockSpec(memory_space=pltpu.VMEM)


def _sum_small(parts, name):
    n = len(parts)

    def body(*refs):
        for p_ref, o_ref in zip(refs[:n], refs[n:]):
            g = p_ref[0]
            for src in range(1, N_DEV):
                g = g + p_ref[src]
            o_ref[...] = g

    return pl.pallas_call(
        body, name=name, in_specs=[VMEM_SPEC] * n, out_specs=[VMEM_SPEC] * n,
        out_shape=[jax.ShapeDtypeStruct(p.shape[1:], F32) for p in parts],
        compiler_params=_params(),
    )(*parts)


def _adamw_small(ws, gs, ms, vs, name):
    n = len(ws)

    def body(*refs):
        ins, outs = refs[:4 * n], refs[4 * n:]
        for i in range(n):
            d, mn, vn = _adamw_math(ins[i][...], ins[n + i][...], ins[2 * n + i][...], ins[3 * n + i][...])
            outs[i][...] = d
            outs[n + i][...] = mn
            outs[2 * n + i][...] = vn

    out = pl.pallas_call(
        body, name=name, in_specs=[VMEM_SPEC] * (4 * n), out_specs=[VMEM_SPEC] * (3 * n),
        out_shape=[jax.ShapeDtypeStruct(w.shape, F32) for w in ws] * 3,
        compiler_params=_params(),
    )(*ws, *gs, *ms, *vs)
    return out[:n], out[n:2 * n], out[2 * n:]


def kernel(x, norm1_g, w_in, q_norm_g, k_norm_g, conv_dw_w, conv_dw_b, conv_ln_g, conv_ln_b, w_out, norm2_g, w_up, ffn_dw_w, ffn_dw_b, w_down, loss_target, m_norm1_g, m_w_in, m_q_norm_g, m_k_norm_g, m_conv_dw_w, m_conv_dw_b, m_conv_ln_g, m_conv_ln_b, m_w_out, m_norm2_g, m_w_up, m_ffn_dw_w, m_ffn_dw_b, m_w_down, v_norm1_g, v_w_in, v_q_norm_g, v_k_norm_g, v_conv_dw_w, v_conv_dw_b, v_conv_ln_g, v_conv_ln_b, v_w_out, v_norm2_g, v_w_up, v_ffn_dw_w, v_ffn_dw_b, v_w_down):
    depth = w_in.shape[0]
    d_ff = w_down.shape[1] * N_DEV
    conv_w = conv_dw_b.shape[1]
    cw_shard = conv_dw_w.shape[2]
    fw_shard = ffn_dw_w.shape[2]
    me = 4 * lax.axis_index("x") + 2 * lax.axis_index("y") + lax.axis_index("c")

    transposed = lambda a: a.transpose(0, 2, 1)
    b_in, b_out, b_up, b_down = (transposed(w_in).astype(BF16), w_out.astype(BF16), transposed(w_up).astype(BF16),
                                 w_down.astype(BF16))
    rows_major = lambda g: g.reshape(N_DEV * g.shape[1], g.shape[2])
    g_in0, g_cw, g_fw = _gather([b_in[0], conv_dw_w, ffn_dw_w], name="gather_first")
    wf_in, wf_out, wf_up, wf_down = [rows_major(g_in0)] + [None] * (depth - 1), [None] * depth, [None] * depth, [None] * depth
    cwf = g_cw.transpose(1, 2, 0, 3).reshape(depth, CONV_KERNEL, conv_w)
    fwf = g_fw.transpose(1, 2, 0, 3).reshape(depth, FFN_KERNEL, d_ff)

    row = lambda a, l: a[l].reshape(1, -1)
    both_heads = lambda a, l: jnp.tile(row(a, l), (1, HEADS_PER_STEP))

    xs = x[0]
    saved = []
    for l in range(depth):
        proj, h1, _ = _mm_rms(xs, row(norm1_g, l), wf_in[l], F32, [], name="fwd_in")
        sends = [b_out[l], b_up[l]] + ([b_in[l + 1]] if l + 1 < depth else [])
        attn, tsum, got = _attn_fwd(proj, both_heads(q_norm_g, l), both_heads(k_norm_g, l), sends, name="fwd_attn")
        wf_out[l], wf_up[l] = rows_major(got[0]), rows_major(got[1])
        if l + 1 < depth:
            wf_in[l + 1] = rows_major(got[2])
        conv, conv_y = _conv_fwd(proj, cwf[l], row(conv_dw_b, l), row(conv_ln_g, l), row(conv_ln_b, l),
                                 name="fwd_conv")
        cat = jnp.concatenate([attn, conv], axis=1)
        x_mid = _mm_res(cat, wf_out[l], xs, name="fwd_out")
        up, h2, got = _mm_rms(x_mid, row(norm2_g, l), wf_up[l], BF16, [b_down[l]], name="fwd_up")
        wf_down[l] = rows_major(got[0])
        act = _ffn_fwd(up, fwf[l], row(ffn_dw_b, l), name="fwd_ffn")
        x_next = _mm_res(act, wf_down[l], x_mid, name="fwd_down")
        saved.append((xs, h1, proj, tsum, cat, x_mid, h2, up, act, conv_y))
        xs = x_next

    loss_tile, dx, dxb = _loss_head(xs, loss_target[0], name="loss_head")
    loss = lax.psum(loss_tile[0, 0], ("x", "y", "c"))

    r_in, r_out, r_up, r_down = [None] * depth, [None] * depth, [None] * depth, [None] * depth
    row_blocks = lambda g: g.reshape(N_DEV, g.shape[0] // N_DEV, g.shape[1])
    small = {k: [None] * depth for k in ("norm1_g", "q_norm_g", "k_norm_g", "conv_dw_w", "conv_dw_b", "conv_ln_g",
                                         "conv_ln_b", "norm2_g", "ffn_dw_w", "ffn_dw_b")}
    gw_in = None
    for l in reversed(range(depth)):
        xs, h1, proj, tsum, cat, x_mid, h2, up, act, conv_y = saved[l]
        dact = _mm_nt(dxb, wf_down[l], BF16, name="bwd_dact")
        gw_down = _mm_tn(act, dxb, name="bwd_gw_down")
        dup, small["ffn_dw_w"][l], small["ffn_dw_b"][l] = _ffn_bwd(up, dact, fwf[l], row(ffn_dw_b, l), name="bwd_ffn")
        gw_up = _mm_tn(dup, h2, name="bwd_gw_up")
        if l + 1 < depth:
            blocks = row_blocks(gw_in)
            half = blocks.shape[1] // 2
            in_halves = [blocks[:, :half], blocks[:, half:]]
        dx, dxb, small["norm2_g"][l], got_a = _mm_rmsbwd(dup, wf_up[l], x_mid, row(norm2_g, l), dx,
                                                         in_halves[:1] if l + 1 < depth else [], name="bwd_up")
        dcat = _mm_nt(dxb, wf_out[l], F32, name="bwd_dcat")
        gw_out = _mm_tn(cat, dxb, name="bwd_gw_out")
        early = in_halves[1] if l + 1 < depth else row_blocks(gw_out)
        (dglu, small["conv_dw_w"][l], small["conv_dw_b"][l], small["conv_ln_g"][l], small["conv_ln_b"][l]), got = (
            _conv_bwd(proj, conv_y, dcat, cwf[l], row(conv_ln_g, l), row(conv_ln_b, l), [early], name="bwd_conv"))
        sends = [row_blocks(gw_down), row_blocks(gw_up)] + ([row_blocks(gw_out)] if l + 1 < depth else [])
        (dq, dk, dv, dqg, dkg), got2 = _attn_bwd(proj, dcat, tsum, both_heads(q_norm_g, l), both_heads(k_norm_g, l),
                                                 sends, name="bwd_attn")
        r_down[l], r_up[l] = got2[:2]
        if l + 1 < depth:
            r_in[l + 1], r_out[l] = [got_a[0], got[0]], got2[2]
        else:
            r_out[l] = got[0]
        small["q_norm_g"][l] = dqg[0:1, :HEAD_DIM] + dqg[0:1, HEAD_DIM:]
        small["k_norm_g"][l] = dkg[0:1, :HEAD_DIM] + dkg[0:1, HEAD_DIM:]
        dproj = jnp.concatenate([dq, dk, dv, dglu], axis=1)
        gw_in = _mm_tn(dproj, h1, name="bwd_gw_in")
        if l == 0:
            blocks = row_blocks(gw_in)
            half = blocks.shape[1] // 2
            last_halves = [blocks[:, :half], blocks[:, half:]]
        dx, dxb, small["norm1_g"][l], got = _mm_rmsbwd(dproj, wf_in[l], xs, row(norm1_g, l), dx,
                                                       last_halves[:1] if l == 0 else [], name="bwd_in")
    last_got = got
    grad_x = dx[None]

    names = ["norm1_g", "q_norm_g", "k_norm_g", "conv_dw_w", "conv_dw_b", "conv_ln_g", "conv_ln_b", "norm2_g",
             "ffn_dw_w", "ffn_dw_b"]
    full_shapes = {"norm1_g": norm1_g.shape, "q_norm_g": q_norm_g.shape, "k_norm_g": k_norm_g.shape,
                   "conv_dw_w": (depth, CONV_KERNEL, conv_w), "conv_dw_b": conv_dw_b.shape,
                   "conv_ln_g": conv_ln_g.shape, "conv_ln_b": conv_ln_b.shape, "norm2_g": norm2_g.shape,
                   "ffn_dw_w": (depth, FFN_KERNEL, d_ff), "ffn_dw_b": ffn_dw_b.shape}
    partial = [jnp.stack(small[k]).reshape(full_shapes[k]) for k in names]
    big = {}
    big["w_out"], got = _sum_adamw(r_out, w_out, m_w_out, v_w_out, last_halves[1:], partial, name="adamw_out")
    r_in[0], all_partials = [last_got[0], got[0]], got[1:]
    big["w_up"], _ = _sum_adamw(r_up, transposed(w_up), transposed(m_w_up), transposed(v_w_up), [], [], name="adamw_up")
    big["w_down"], _ = _sum_adamw(r_down, w_down, m_w_down, v_w_down, [], [], name="adamw_down")
    big["w_in"], _ = _sum_adamw(r_in, transposed(w_in), transposed(m_w_in), transposed(v_w_in), [], [],
                                name="adamw_in")
    for k in ("w_in", "w_up"):
        big[k] = [transposed(a) for a in big[k]]

    grads = dict(zip(names, _sum_small(all_partials, name="sum_small_grads")))
    grads["conv_dw_w"] = lax.dynamic_slice_in_dim(grads["conv_dw_w"], me * cw_shard, cw_shard, axis=2)
    grads["ffn_dw_w"] = lax.dynamic_slice_in_dim(grads["ffn_dw_w"], me * fw_shard, fw_shard, axis=2)
    weights = dict(norm1_g=norm1_g, q_norm_g=q_norm_g, k_norm_g=k_norm_g, conv_dw_w=conv_dw_w, conv_dw_b=conv_dw_b,
                   conv_ln_g=conv_ln_g, conv_ln_b=conv_ln_b, norm2_g=norm2_g, ffn_dw_w=ffn_dw_w, ffn_dw_b=ffn_dw_b)
    m_in = dict(norm1_g=m_norm1_g, q_norm_g=m_q_norm_g, k_norm_g=m_k_norm_g, conv_dw_w=m_conv_dw_w,
                conv_dw_b=m_conv_dw_b, conv_ln_g=m_conv_ln_g, conv_ln_b=m_conv_ln_b, norm2_g=m_norm2_g,
                ffn_dw_w=m_ffn_dw_w, ffn_dw_b=m_ffn_dw_b)
    v_in = dict(norm1_g=v_norm1_g, q_norm_g=v_q_norm_g, k_norm_g=v_k_norm_g, conv_dw_w=v_conv_dw_w,
                conv_dw_b=v_conv_dw_b, conv_ln_g=v_conv_ln_g, conv_ln_b=v_conv_ln_b, norm2_g=v_norm2_g,
                ffn_dw_w=v_ffn_dw_w, ffn_dw_b=v_ffn_dw_b)
    d_s, m_s, v_s = _adamw_small([weights[k] for k in names], [grads[k] for k in names], [m_in[k] for k in names],
                                 [v_in[k] for k in names], name="adamw_small")
    delta, new_m, new_v = dict(zip(names, d_s)), dict(zip(names, m_s)), dict(zip(names, v_s))
    for k, (g, d, mn, vn) in big.items():
        grads[k], delta[k], new_m[k], new_v[k] = g, d, mn, vn

    order = ["norm1_g", "w_in", "q_norm_g", "k_norm_g", "conv_dw_w", "conv_dw_b", "conv_ln_g", "conv_ln_b", "w_out",
             "norm2_g", "w_up", "ffn_dw_w", "ffn_dw_b", "w_down"]
    return (loss, grad_x, *[grads[k] for k in order], *[delta[k] for k in order], *[new_m[k] for k in order],
            *[new_v[k] for k in order])
```

```python
import jax
import jax.numpy as jnp
from jax import lax
from jax.experimental import pallas as pl
from jax.experimental.pallas import tpu as pltpu

F32 = jnp.float32
BF16 = jnp.bfloat16

N_DEV = 8
HEADS = 8
HEAD_DIM = 64
ATTN_WIDTH = HEADS * HEAD_DIM
CONV_KERNEL = 31
FFN_KERNEL = 3
EPS = 1e-6
BLK = 128
KEY_GROUP = 4
LANES = 128
NORM_ROWS = 128
CONV_HALO = 32
FFN_HALO = 16

ADAM_LR = 0.001
ADAM_B1 = 0.9
ADAM_B2 = 0.999
ADAM_EPS = 1e-08
ADAM_WD = 0.01
ADAM_STEP = 10

VMEM_LIMIT = 56 * 1024 * 1024


def _params(n_axes=0):
    kw = dict(vmem_limit_bytes=VMEM_LIMIT)
    if n_axes:
        kw["dimension_semantics"] = ("arbitrary",) * n_axes
    return pltpu.CompilerParams(**kw)


def _dot(a, b):
    return jnp.dot(a, b, preferred_element_type=F32)


def _dot_nt(a, b):
    return lax.dot_general(a, b, (((1,), (1,)), ((), ())), preferred_element_type=F32)


def _dot_tn(a, b):
    return lax.dot_general(a, b, (((0,), (0,)), ((), ())), preferred_element_type=F32)


def _sigmoid(x):
    return 1.0 / (1.0 + jnp.exp(-x))


def _split_bf16(x):
    hi = x.astype(BF16)
    lo = (x - hi.astype(F32)).astype(BF16)
    return hi, lo


def _pick(n, options):
    for t in options:
        if n % t == 0:
            return t
    return n


def _tile(n, cap):
    best = None
    for t in range(LANES, min(n, cap) + 1, LANES):
        if n % t == 0:
            best = t
    return best or n


def _mm_rms(x, g, wt, out_dtype, sends, name):
    m, k = x.shape
    n = wt.shape[0]
    tm = _tile(m, 2048)
    tn = _tile(n, 512)
    n_send = len(sends)
    grid = (m // tm, n // tn)

    def body(x_ref, g_ref, w_ref, *rest):
        send_refs, (o_ref, h_ref), rest = rest[:n_send], rest[n_send:n_send + 2], rest[n_send + 2:]
        got_refs, h_s, sems = rest[:n_send], rest[n_send], rest[n_send + 1:]
        if n_send:
            start, relay, finish = _two_level_gather(send_refs, got_refs, sems)
            is_first = (pl.program_id(0) == 0) & (pl.program_id(1) == 0)
            is_last = (pl.program_id(0) == grid[0] - 1) & (pl.program_id(1) == grid[1] - 1)
            pl.when(is_first)(start)
            pl.when(is_last)(relay)

        @pl.when(pl.program_id(1) == 0)
        def _():
            def chunk(c, _):
                rows = pl.ds(pl.multiple_of(c * NORM_ROWS, NORM_ROWS), NORM_ROWS)
                xv = x_ref[rows, :]
                r = lax.rsqrt(jnp.mean(xv * xv, axis=-1, keepdims=True) + EPS)
                hv = (xv * r * g_ref[...]).astype(BF16)
                h_s[rows, :] = hv
                h_ref[rows, :] = hv
                return 0

            lax.fori_loop(0, tm // NORM_ROWS, chunk, 0)

        o_ref[...] = _dot_nt(h_s[...], w_ref[...]).astype(out_dtype)
        if n_send:
            pl.when(is_last)(finish)

    out = pl.pallas_call(
        body, name=name, grid=grid,
        in_specs=[pl.BlockSpec((tm, k), lambda i, j: (i, 0)),
                  pl.BlockSpec((1, k), lambda i, j: (0, 0)),
                  pl.BlockSpec((tn, k), lambda i, j: (j, 0))] + [ANY_SPEC] * n_send,
        out_specs=[pl.BlockSpec((tm, tn), lambda i, j: (i, j)),
                   pl.BlockSpec((tm, k), lambda i, j: (i, 0))] + [ANY_SPEC] * n_send,
        out_shape=[jax.ShapeDtypeStruct((m, n), out_dtype), jax.ShapeDtypeStruct((m, k), BF16)]
        + _gathered_shapes(sends),
        scratch_shapes=[pltpu.VMEM((tm, k), BF16)] + (_exchange_sems(n_send) if n_send else []),
        compiler_params=_params(2),
    )(x, g, wt, *sends)
    return out[0], out[1], out[2:]


def _mm_res(a, w, res, name):
    m, k = a.shape
    n = w.shape[1]
    tm = _tile(m, 1024)
    tn = _tile(n, 512)

    def body(a_ref, w_ref, r_ref, o_ref):
        o_ref[...] = r_ref[...] + _dot(a_ref[...], w_ref[...])

    return pl.pallas_call(
        body, name=name, grid=(m // tm, n // tn),
        in_specs=[pl.BlockSpec((tm, k), lambda i, j: (i, 0)),
                  pl.BlockSpec((k, tn), lambda i, j: (0, j)),
                  pl.BlockSpec((tm, tn), lambda i, j: (i, j))],
        out_specs=pl.BlockSpec((tm, tn), lambda i, j: (i, j)),
        out_shape=jax.ShapeDtypeStruct((m, n), F32),
        compiler_params=_params(2),
    )(a, w, res)


def _mm_nt(a, w, out_dtype, name):
    m, k = a.shape
    n = w.shape[0]
    tm = _tile(m, 1024)
    tn = _tile(n, 1408)

    def body(a_ref, w_ref, o_ref):
        o_ref[...] = _dot_nt(a_ref[...], w_ref[...]).astype(out_dtype)

    return pl.pallas_call(
        body, name=name, grid=(m // tm, n // tn),
        in_specs=[pl.BlockSpec((tm, k), lambda i, j: (i, 0)),
                  pl.BlockSpec((tn, k), lambda i, j: (j, 0))],
        out_specs=pl.BlockSpec((tm, tn), lambda i, j: (i, j)),
        out_shape=jax.ShapeDtypeStruct((m, n), out_dtype),
        compiler_params=_params(2),
    )(a, w)


def _column_tiles(a, cap):
    if a.ndim == 2:
        s, c = a.shape
        tc = _tile(c, cap)
        return s, c, tc, lambda rows, index: pl.BlockSpec((rows, tc), lambda *g: (index(*g)[0], index(*g)[1]))
    slabs, s, width = a.shape
    tc = _tile(width, cap)
    per = width // tc
    return s, slabs * width, tc, lambda rows, index: pl.BlockSpec(
        (None, rows, tc), lambda *g: (index(*g)[1] // per, index(*g)[0], index(*g)[1] % per))


def _mm_tn(a, b, name):
    s, m, tm, a_spec = _column_tiles(a, 1408)
    n = b.shape[1]
    tn = _tile(n, 1024)

    def body(a_ref, b_ref, o_ref):
        o_ref[...] = _dot_tn(a_ref[...], b_ref[...]).astype(BF16)

    return pl.pallas_call(
        body, name=name, grid=(m // tm, n // tn),
        in_specs=[a_spec(s, lambda i, j: (0, i)),
                  pl.BlockSpec((s, tn), lambda i, j: (0, j))],
        out_specs=pl.BlockSpec((tm, tn), lambda i, j: (i, j)),
        out_shape=jax.ShapeDtypeStruct((m, n), BF16),
        compiler_params=_params(2),
    )(a, b)


def _mm_rmsbwd(a, w, x, g, dres, sends, name):
    m, k, tk, a_spec = _column_tiles(a, 1408)
    n = w.shape[1]
    tm = _tile(m, 1024)
    nk = k // tk
    n_send = len(sends)

    def body(a_ref, w_ref, x_ref, g_ref, r_ref, *rest):
        send_refs, (dx_ref, dxb_ref, dg_ref), rest = rest[:n_send], rest[n_send:n_send + 3], rest[n_send + 3:]
        got_refs, acc, sems = rest[:n_send], rest[n_send], rest[n_send + 1:]
        i, kk = pl.program_id(0), pl.program_id(1)
        if n_send:
            copies = _scatter_copies(send_refs, got_refs, sems)

            @pl.when((i == 0) & (kk == 0))
            def _():
                for cp in copies:
                    cp.start()

        part = _dot(a_ref[...], w_ref[...])

        @pl.when(kk == 0)
        def _():
            acc[...] = part

        @pl.when(kk > 0)
        def _():
            acc[...] += part

        @pl.when(kk == nk - 1)
        def _():
            def chunk(c, dgp):
                rows = pl.ds(pl.multiple_of(c * NORM_ROWS, NORM_ROWS), NORM_ROWS)
                dh = acc[rows, :]
                xv = x_ref[rows, :]
                r = lax.rsqrt(jnp.mean(xv * xv, axis=-1, keepdims=True) + EPS)
                xh = xv * r
                dxh = dh * g_ref[...]
                dx = r_ref[rows, :] + r * (dxh - xh * jnp.mean(dxh * xh, axis=-1, keepdims=True))
                dx_ref[rows, :] = dx
                dxb_ref[rows, :] = dx.astype(BF16)
                return dgp + jnp.sum(dh * xh, axis=0, keepdims=True)

            dgp = lax.fori_loop(0, tm // NORM_ROWS, chunk, jnp.zeros((1, n), F32))

            @pl.when(i == 0)
            def _():
                dg_ref[...] = dgp

            @pl.when(i > 0)
            def _():
                dg_ref[...] += dgp

        if n_send:
            @pl.when((i == m // tm - 1) & (kk == nk - 1))
            def _():
                for cp in copies:
                    cp.wait()

    out = pl.pallas_call(
        body, name=name, grid=(m // tm, nk),
        in_specs=[a_spec(tm, lambda i, kk: (i, kk)),
                  pl.BlockSpec((tk, n), lambda i, kk: (kk, 0)),
                  pl.BlockSpec((tm, n), lambda i, kk: (i, 0)),
                  pl.BlockSpec((1, n), lambda i, kk: (0, 0)),
                  pl.BlockSpec((tm, n), lambda i, kk: (i, 0))] + [ANY_SPEC] * n_send,
        out_specs=[pl.BlockSpec((tm, n), lambda i, kk: (i, 0)),
                   pl.BlockSpec((tm, n), lambda i, kk: (i, 0)),
                   pl.BlockSpec((1, n), lambda i, kk: (0, 0))] + [ANY_SPEC] * n_send,
        out_shape=[jax.ShapeDtypeStruct((m, n), F32), jax.ShapeDtypeStruct((m, n), BF16),
                   jax.ShapeDtypeStruct((1, n), F32)] + _scattered_shapes(sends),
        scratch_shapes=[pltpu.VMEM((tm, n), F32)] + (_exchange_sems(n_send) if n_send else []),
        compiler_params=_params(2),
    )(a, w, x, g, dres, *sends)
    return out[0], out[1], out[2], out[3:]


ANY_SPEC = pl.BlockSpec(memory_space=pl.ANY)
SEMS_PER_OPERAND = N_DEV - 1


def _exchange_sems(n):
    return [pltpu.SemaphoreType.DMA((n, SEMS_PER_OPERAND)), pltpu.SemaphoreType.DMA((n, SEMS_PER_OPERAND)),
            pltpu.SemaphoreType.DMA((n,))]


def _gathered_shapes(parts):
    return [jax.ShapeDtypeStruct((N_DEV,) + a.shape, a.dtype) for a in parts]


def _scattered_shapes(parts):
    return [jax.ShapeDtypeStruct(a.shape, a.dtype) for a in parts]


def _flat(pos):
    return 4 * pos[0] + 2 * pos[1] + pos[2]


def _remote(src, dst, sems, i, k, to):
    send_sems, recv_sems, _ = sems
    return pltpu.make_async_remote_copy(src_ref=src, dst_ref=dst, send_sem=send_sems.at[i, k],
                                        recv_sem=recv_sems.at[i, k], device_id=to,
                                        device_id_type=pl.DeviceIdType.MESH)


def _scatter_copies(ins, outs, sems):
    x, y, c = lax.axis_index("x"), lax.axis_index("y"), lax.axis_index("c")
    me = _flat((x, y, c))
    copies = [pltpu.make_async_copy(ins[i].at[me], outs[i].at[me], sems[2].at[i]) for i in range(len(ins))]
    for d in range(1, N_DEV):
        peer = (1 - x if d & 4 else x, 1 - y if d & 2 else y, 1 - c if d & 1 else c)
        for i in range(len(ins)):
            copies.append(_remote(ins[i].at[_flat(peer)], outs[i].at[me], sems, i, d - 1, peer))
    return copies


def _two_level_gather(ins, outs, sems):
    x, y, c = lax.axis_index("x"), lax.axis_index("y"), lax.axis_index("c")
    me, sibling = (x, y, c), (x, y, 1 - c)
    chips = [(1 - x, y), (x, 1 - y), (1 - x, 1 - y)]
    n = len(ins)

    def block(i, pos):
        return outs[i].at[_flat(pos)]

    local = [pltpu.make_async_copy(ins[i], block(i, me), sems[2].at[i]) for i in range(n)]
    own = [_remote(ins[i], block(i, me), sems, i, 0, sibling) for i in range(n)]
    own += [_remote(ins[i], block(i, me), sems, i, 1 + j, (*chip, c)) for i in range(n) for j, chip in enumerate(chips)]
    passed = [[_remote(block(i, (*chip, c)), block(i, (*chip, c)), sems, i, 4 + j, sibling) for i in range(n)]
              for j, chip in enumerate(chips)]

    def first():
        for cp in local + own:
            cp.start()

    def relay():
        for j, chip in enumerate(chips):
            for i in range(n):
                _remote(ins[i], block(i, (*chip, c)), sems, i, 1 + j, me).wait_recv()
                passed[j][i].start()

    def finish():
        for i in range(n):
            _remote(ins[i], block(i, sibling), sems, i, 0, me).wait_recv()
            for j, chip in enumerate(chips):
                _remote(ins[i], block(i, (*chip, 1 - c)), sems, i, 4 + j, me).wait_recv()
        for cp in own + [cp for row in passed for cp in row]:
            cp.wait_send()
        for cp in local:
            cp.wait()

    return first, relay, finish


def _gather(parts, name):
    n = len(parts)

    def body(*refs):
        start, relay, finish = _two_level_gather(refs[:n], refs[n:2 * n], refs[2 * n:])
        start()
        relay()
        finish()

    return pl.pallas_call(
        body, name=name, in_specs=[ANY_SPEC] * n, out_specs=[ANY_SPEC] * n,
        out_shape=_gathered_shapes(parts), scratch_shapes=_exchange_sems(n),
    )(*parts)


def _tri(kind):
    j = lax.broadcasted_iota(jnp.int32, (BLK, BLK), 0)
    s = lax.broadcasted_iota(jnp.int32, (BLK, BLK), 1)
    m = {"after": j > s, "upto": j <= s, "before": j < s}[kind]
    return jnp.concatenate([jnp.where(m, 1.0, 0.0), jnp.ones((BLK, BLK), F32)], axis=1).astype(BF16)


def _scan_rows(v, tri):
    r = _dot(v.astype(BF16), tri)
    return r[:, :BLK], r[:, BLK:]


HEADS_PER_STEP = LANES // HEAD_DIM


def _first_head_lanes():
    return lax.broadcasted_iota(jnp.int32, (1, LANES), 1) < HEAD_DIM


def _pair_mean(v, first):
    m0 = jnp.sum(jnp.where(first, v, 0.0), axis=-1, keepdims=True)
    m1 = jnp.sum(jnp.where(first, 0.0, v), axis=-1, keepdims=True)
    return jnp.where(first, m0, m1) * (1.0 / HEAD_DIM)


def _pair_norm(v, g2, first):
    return v * lax.rsqrt(_pair_mean(v * v, first) + EPS) * g2


def _pair_norm_bwd(raw, g2, dn, first):
    r = lax.rsqrt(_pair_mean(raw * raw, first) + EPS)
    xh = raw * r
    dxh = dn * g2
    return r * (dxh - xh * _pair_mean(dxh * xh, first)), jnp.sum(dn * xh, axis=0, keepdims=True)


def _block_diag(v, first):
    zero = jnp.zeros_like(v)
    return jnp.concatenate([jnp.where(first, v, zero), jnp.where(first, zero, v)], axis=0)


def _attn_prep(q_ref, k_ref, v_ref, qg_ref, kg_ref, qc_s, kc_s, vd_s, kd_s, n_blk):
    scale = HEAD_DIM ** -0.5
    first = _first_head_lanes()

    def prep(i, _):
        rows = pl.ds(pl.multiple_of(i * BLK, BLK), BLK)
        both = pl.ds(pl.multiple_of(i * 2 * BLK, 2 * BLK), 2 * BLK)
        qh, ql = _split_bf16(_pair_norm(q_ref[rows, :], qg_ref[...], first) * scale)
        kh, kl = _split_bf16(_pair_norm(k_ref[rows, :], kg_ref[...], first))
        for h in range(HEADS_PER_STEP):
            sl = slice(h * HEAD_DIM, (h + 1) * HEAD_DIM)
            qc_s[h, rows, :] = jnp.concatenate([qh[:, sl], ql[:, sl], qh[:, sl], ql[:, sl]], axis=1)
            kc_s[h, rows, :] = jnp.concatenate([kh[:, sl], kh[:, sl], kl[:, sl], kl[:, sl]], axis=1)
        vd_s[both, :] = _block_diag(v_ref[rows, :].astype(BF16), first)
        if kd_s is not None:
            kd_s[both, :] = _block_diag(kh, first)
        return 0

    lax.fori_loop(0, n_blk, prep, 0)


def _pair_scores(qc, kc_ref, grp, n_tiles=KEY_GROUP):
    zs = []
    for j in range(0, n_tiles, 2):
        two = pl.ds(pl.multiple_of((grp * KEY_GROUP + j) * BLK, 2 * BLK), 2 * BLK)
        z = _dot_nt(qc, kc_ref[two, :])
        zs += [z[:, :BLK], z[:, BLK:]]
    return zs[:n_tiles]


def _col_minus_row():
    row = lax.broadcasted_iota(jnp.int32, (BLK, BLK), 0)
    col = lax.broadcasted_iota(jnp.int32, (BLK, BLK), 1)
    return col - row


def _softplus(z):
    return jnp.maximum(z, 0.0) + jnp.log(1.0 + jnp.exp(-jnp.abs(z)))


def _attn_fwd(proj, qg, kg, sends, name):
    s = proj.shape[0]
    n_blk = s // BLK
    pairs = ATTN_WIDTH // LANES
    n_send = len(sends)

    def body(q_ref, k_ref, v_ref, qg_ref, kg_ref, *rest):
        send_refs, (o_ref, t_ref), rest = rest[:n_send], rest[n_send:n_send + 2], rest[n_send + 2:]
        got_refs, (qc_s, kc_s, vd_s), sems = rest[:n_send], rest[n_send:n_send + 3], rest[n_send + 3:]
        start, relay, finish = _two_level_gather(send_refs, got_refs, sems)
        step = pl.program_id(0)
        pl.when(step == 0)(start)
        pl.when(step == pairs - 1)(relay)
        tri = _tri("after")
        diff = _col_minus_row()
        first = _first_head_lanes()
        heads = range(HEADS_PER_STEP)
        _attn_prep(q_ref, k_ref, v_ref, qg_ref, kg_ref, qc_s, kc_s, vd_s, None, n_blk)

        below = diff < 0

        def group(qc, grp, carry, acc, n_tiles, diagonal):
            blocks = [grp * KEY_GROUP + j for j in reversed(range(n_tiles))]
            zs = [_pair_scores(qc[h], kc_s.at[h], grp, n_tiles)[::-1] for h in heads]
            parts = [[None] * n_tiles for _ in heads]
            for h in heads:
                for j, z in enumerate(zs[h]):
                    sp = _softplus(z)
                    lom = -sp
                    if diagonal and j == 0:
                        lom = jnp.where(below, lom, 0.0)
                    tail, tot = _scan_rows(lom, tri)
                    parts[h][j] = (z - sp + tail, tot)
            carry = list(carry)
            for j, kb in enumerate(blocks):
                ws = []
                for h in heads:
                    lw, tot = parts[h][j]
                    w = jnp.exp(lw + carry[h])
                    if diagonal and j == 0:
                        w = jnp.where(below, w, 0.0)
                    ws.append(w.astype(BF16))
                    carry[h] = carry[h] + tot
                acc = acc + _dot(jnp.concatenate(ws, axis=1),
                                 vd_s[pl.ds(pl.multiple_of(kb * 2 * BLK, 2 * BLK), 2 * BLK), :])
            return tuple(carry), acc

        def q_blocks(top, _):
            for r in range(KEY_GROUP):
                rows = pl.ds(pl.multiple_of((top * KEY_GROUP + r) * BLK, BLK), BLK)
                qc = [qc_s[h, rows, :] for h in heads]
                zero = jnp.zeros((BLK, BLK), F32)
                carry, acc = group(qc, top, (zero,) * HEADS_PER_STEP, jnp.zeros((BLK, LANES), F32), r + 1, True)
                carry, acc = lax.fori_loop(
                    0, top, lambda t, c: group(qc, top - 1 - t, c[0], c[1], KEY_GROUP, False), (carry, acc))
                o_ref[rows, :] = acc.astype(BF16)
                t_ref[rows, :] = jnp.where(first, carry[0], carry[1])
            return 0

        lax.fori_loop(0, n_blk // KEY_GROUP, q_blocks, 0)
        pl.when(step == pairs - 1)(finish)

    col = lambda off: pl.BlockSpec((s, LANES), lambda p: (0, off + p))
    vec = pl.BlockSpec((1, LANES), lambda p: (0, 0))
    out = pl.pallas_call(
        body, name=name, grid=(pairs,),
        in_specs=[col(0), col(pairs), col(2 * pairs), vec, vec] + [ANY_SPEC] * n_send,
        out_specs=[pl.BlockSpec((s, LANES), lambda p: (0, p))] * 2 + [ANY_SPEC] * n_send,
        out_shape=[jax.ShapeDtypeStruct((s, ATTN_WIDTH), BF16), jax.ShapeDtypeStruct((s, ATTN_WIDTH), F32)]
        + _gathered_shapes(sends),
        scratch_shapes=[pltpu.VMEM((HEADS_PER_STEP, s, 4 * HEAD_DIM), BF16)] * 2
        + [pltpu.VMEM((HEADS_PER_STEP * s, LANES), BF16)] + _exchange_sems(n_send),
        compiler_params=_params(1),
    )(proj, proj, proj, qg, kg, *sends)
    return out[0], out[1], out[2:]


def _attn_bwd(proj, dcat, tsum, qg, kg, sends, name):
    s = proj.shape[0]
    n_blk = s // BLK
    pairs = ATTN_WIDTH // LANES
    scale = HEAD_DIM ** -0.5
    n_send = len(sends)
    n_scratch = 7

    def body(q_ref, k_ref, v_ref, do_ref, t_ref, qg_ref, kg_ref, *rest):
        send_refs, rest = rest[:n_send], rest[n_send:]
        (dq_ref, dk_ref, dv_ref, dqg_ref, dkg_ref), rest = rest[:5], rest[5:]
        got_refs, scratch, sems = rest[:n_send], rest[n_send:n_send + n_scratch], rest[n_send + n_scratch:]
        qc_s, kc_s, vd_s, kd_s, qd_s, dob_s, dkv_s = scratch
        copies = _scatter_copies(send_refs, got_refs, sems)

        @pl.when(pl.program_id(0) == 0)
        def _():
            for cp in copies:
                cp.start()

        tri_p = _tri("upto")
        tri_h = _tri("before")
        diff = _col_minus_row()

        @pl.when(pl.program_id(0) == 0)
        def _():
            dqg_ref[...] = jnp.zeros_like(dqg_ref)
            dkg_ref[...] = jnp.zeros_like(dkg_ref)

        first = _first_head_lanes()
        heads = range(HEADS_PER_STEP)
        _attn_prep(q_ref, k_ref, v_ref, qg_ref, kg_ref, qc_s, kc_s, vd_s, kd_s, n_blk)

        def prep(i, _):
            rows = pl.ds(pl.multiple_of(i * BLK, BLK), BLK)
            both = pl.ds(pl.multiple_of(i * 2 * BLK, 2 * BLK), 2 * BLK)
            dob = do_ref[rows, :].astype(BF16)
            dob_s[rows, :] = dob
            none = jnp.zeros((BLK, HEAD_DIM), BF16)
            for h in heads:
                qd_s[h, both, :] = jnp.concatenate(
                    [jnp.concatenate([qc_s[h, rows, 0:HEAD_DIM], none], axis=1),
                     jnp.concatenate([none, dob[:, h * HEAD_DIM:(h + 1) * HEAD_DIM]], axis=1)], axis=0)
                dkv_s[h, rows, :] = jnp.zeros((BLK, LANES), F32)
            return 0

        lax.fori_loop(0, n_blk, prep, 0)

        below = diff < 0

        def group(qc, qd, dob, tq, grp, pc, hc, dq, n_tiles, diagonal):
            blocks = [grp * KEY_GROUP + j for j in range(n_tiles)]
            cols_of = [pl.ds(pl.multiple_of(kb * BLK, BLK), BLK) for kb in blocks]
            both_of = [pl.ds(pl.multiple_of(kb * 2 * BLK, 2 * BLK), 2 * BLK) for kb in blocks]
            on_diagonal = [diagonal and j == n_tiles - 1 for j in range(n_tiles)]
            zs = [_pair_scores(qc[h], kc_s.at[h], grp, n_tiles) for h in heads]
            das = [_dot_nt(dob, vd_s[both, :]) for both in both_of]
            lbs = [[None] * n_tiles for _ in heads]
            scans = [[None] * n_tiles for _ in heads]
            for h in heads:
                for j, z in enumerate(zs[h]):
                    sp = _softplus(z)
                    lom = -sp
                    if on_diagonal[j]:
                        lom = jnp.where(below, lom, 0.0)
                    lbs[h][j] = z - sp
                    scans[h][j] = _scan_rows(lom, tri_p)
            pc, hc = list(pc), list(hc)
            avs = [[None] * n_tiles for _ in heads]
            gws = [[None] * n_tiles for _ in heads]
            hscans = [[None] * n_tiles for _ in heads]
            for h in heads:
                for j in range(n_tiles):
                    p_in, p_tot = scans[h][j]
                    a = jnp.exp(lbs[h][j] + (tq[h] - pc[h] - p_in))
                    if on_diagonal[j]:
                        a = jnp.where(below, a, 0.0)
                    pc[h] = pc[h] + p_tot
                    gw = das[j][:, h * BLK:(h + 1) * BLK] * a
                    avs[h][j] = a.astype(BF16)
                    gws[h][j] = gw
                    hscans[h][j] = _scan_rows(gw, tri_h)
            dzs = [[None] * n_tiles for _ in heads]
            for h in heads:
                for j in range(n_tiles):
                    h_in, g_tot = hscans[h][j]
                    gw = gws[h][j]
                    dz = gw - jnp.exp(lbs[h][j]) * (gw + hc[h] + h_in)
                    if on_diagonal[j]:
                        dz = jnp.where(below, dz, 0.0)
                    hc[h] = hc[h] + g_tot
                    dzs[h][j] = dz.astype(BF16)
            for j, both in enumerate(both_of):
                dq = dq + _dot(jnp.concatenate([dzs[h][j] for h in heads], axis=1), kd_s[both, :])
            for h in heads:
                for j, cols in enumerate(cols_of):
                    dkv_s[h, cols, :] += _dot_tn(jnp.concatenate([dzs[h][j], avs[h][j]], axis=0), qd[h])
            return tuple(pc), tuple(hc), dq

        def q_blocks(top, dqg):
            for r in range(KEY_GROUP):
                qi = top * KEY_GROUP + r
                rows = pl.ds(pl.multiple_of(qi * BLK, BLK), BLK)
                both = pl.ds(pl.multiple_of(qi * 2 * BLK, 2 * BLK), 2 * BLK)
                qc = [qc_s[h, rows, :] for h in heads]
                qd = [qd_s[h, both, :] for h in heads]
                dob = dob_s[rows, :]
                tboth = t_ref[rows, :]
                tq = [jnp.concatenate([tboth[:, h * HEAD_DIM:(h + 1) * HEAD_DIM]] * 2, axis=1) for h in heads]
                zero = (jnp.zeros((BLK, BLK), F32),) * HEADS_PER_STEP
                pc, hc, dq = lax.fori_loop(
                    0, top, lambda grp, c: group(qc, qd, dob, tq, grp, c[0], c[1], c[2], KEY_GROUP, False),
                    (zero, zero, jnp.zeros((BLK, LANES), F32)))
                _, _, dq = group(qc, qd, dob, tq, top, pc, hc, dq, r + 1, True)
                dq_raw, dg = _pair_norm_bwd(q_ref[rows, :], qg_ref[...], dq * scale, first)
                dq_ref[rows, :] = dq_raw.astype(BF16)
                dqg = dqg + dg
            return dqg

        dqg = lax.fori_loop(0, n_blk // KEY_GROUP, q_blocks, jnp.zeros((1, LANES), F32))

        def finish(i, dkg):
            rows = pl.ds(pl.multiple_of(i * BLK, BLK), BLK)
            dk = jnp.concatenate([dkv_s[h, rows, 0:HEAD_DIM] for h in heads], axis=1)
            dv = jnp.concatenate([dkv_s[h, rows, HEAD_DIM:2 * HEAD_DIM] for h in heads], axis=1)
            dk_raw, dg = _pair_norm_bwd(k_ref[rows, :], kg_ref[...], dk, first)
            dk_ref[rows, :] = dk_raw.astype(BF16)
            dv_ref[rows, :] = dv.astype(BF16)
            return dkg + dg

        dkg = lax.fori_loop(0, n_blk, finish, jnp.zeros((1, LANES), F32))
        dqg_ref[0:1, :] += dqg
        dkg_ref[0:1, :] += dkg

        @pl.when(pl.program_id(0) == pairs - 1)
        def _():
            for cp in copies:
                cp.wait()

    col = lambda off: pl.BlockSpec((s, LANES), lambda p: (0, off + p))
    vec = pl.BlockSpec((1, LANES), lambda p: (0, 0))
    small = pl.BlockSpec((8, LANES), lambda p: (0, 0))
    out = pl.pallas_call(
        body, name=name, grid=(pairs,),
        in_specs=[col(0), col(pairs), col(2 * pairs), col(0), col(0), vec, vec] + [ANY_SPEC] * n_send,
        out_specs=[col(0)] * 3 + [small] * 2 + [ANY_SPEC] * n_send,
        out_shape=[jax.ShapeDtypeStruct((s, ATTN_WIDTH), BF16)] * 3 + [jax.ShapeDtypeStruct((8, LANES), F32)] * 2
        + _scattered_shapes(sends),
        scratch_shapes=[pltpu.VMEM((HEADS_PER_STEP, s, 4 * HEAD_DIM), BF16)] * 2
        + [pltpu.VMEM((HEADS_PER_STEP * s, LANES), BF16)] * 2
        + [pltpu.VMEM((HEADS_PER_STEP, HEADS_PER_STEP * s, LANES), BF16), pltpu.VMEM((s, LANES), BF16),
           pltpu.VMEM((HEADS_PER_STEP, s, LANES), F32)] + _exchange_sems(n_send),
        compiler_params=_params(1),
    )(proj, proj, proj, dcat, tsum, qg, kg, *sends)
    return out[:5], out[5:]


CONV_ROWS = 128


def _shifted(window, shift, halo):
    if shift == 0:
        return window[halo:, :]
    return pltpu.roll(window, shift, 0)[halo:, :]


SUBLANES = 8


def _row_shifts(window, up):
    n = window.shape[0]
    return [window] + [pltpu.roll(window, n - b if up else b, 0) for b in range(1, SUBLANES)]


def _earlier(shifts, back, rows):
    a, b = divmod(back, SUBLANES)
    return shifts[b][CONV_HALO - SUBLANES * a:CONV_HALO - SUBLANES * a + rows, :]


def _later(shifts, ahead, rows):
    a, b = divmod(ahead, SUBLANES)
    return shifts[b][SUBLANES * a:SUBLANES * a + rows, :]


def _lane_blocks(width):
    return [slice(c, c + LANES) for c in range(0, width, LANES)]


def _conv_taps(shifts, w_ref, lanes, rows):
    y = None
    for k in range(CONV_KERNEL):
        term = _earlier(shifts, CONV_KERNEL - 1 - k, rows) * w_ref[k:k + 1, lanes]
        y = term if y is None else y + term
    return y


def _conv_fwd(proj, w, b, lg, lb, name):
    s = proj.shape[0]
    cw = w.shape[1]
    rows = CONV_ROWS
    blk_a = (proj.shape[1] - 2 * cw) // cw

    def body(a_ref, g_ref, w_ref, b_ref, lg_ref, lb_ref, o_ref, y_ref, u_s):
        u_s[0:CONV_HALO, :] = jnp.zeros((CONV_HALO, cw), F32)

        def glu(i, _):
            r0 = pl.multiple_of(i * rows, rows)
            u_s[pl.ds(CONV_HALO + r0, rows), :] = a_ref[pl.ds(r0, rows), :] * _sigmoid(g_ref[pl.ds(r0, rows), :])
            return 0

        lax.fori_loop(0, s // rows, glu, 0)

        def chunk(i, _):
            r0 = pl.multiple_of(i * rows, rows)
            for lanes in _lane_blocks(cw):
                shifts = _row_shifts(u_s[pl.ds(r0, CONV_HALO + rows), lanes], False)
                y_ref[pl.ds(r0, rows), lanes] = _conv_taps(shifts, w_ref, lanes, rows) + b_ref[:, lanes]
            y = y_ref[pl.ds(r0, rows), :]
            yc = y - jnp.mean(y, axis=-1, keepdims=True)
            n = yc * lax.rsqrt(jnp.mean(yc * yc, axis=-1, keepdims=True) + EPS)
            ln = n * lg_ref[...] + lb_ref[...]
            o_ref[pl.ds(r0, rows), :] = (ln * _sigmoid(ln)).astype(BF16)
            return 0

        lax.fori_loop(0, s // rows, chunk, 0)

    vec = pl.BlockSpec((1, cw), lambda i: (0, 0))
    return pl.pallas_call(
        body, name=name, grid=(1,),
        in_specs=[pl.BlockSpec((s, cw), lambda i: (0, blk_a)), pl.BlockSpec((s, cw), lambda i: (0, blk_a + 1)),
                  pl.BlockSpec((CONV_KERNEL, cw), lambda i: (0, 0)), vec, vec, vec],
        out_specs=[pl.BlockSpec((s, cw), lambda i: (0, 0))] * 2,
        out_shape=[jax.ShapeDtypeStruct((s, cw), BF16), jax.ShapeDtypeStruct((s, cw), F32)],
        scratch_shapes=[pltpu.VMEM((CONV_HALO + s, cw), F32)],
        compiler_params=_params(1),
    )(proj, proj, w, b, lg, lb)


def _conv_bwd(proj, y, dcat, w, lg, lb, sends, name):
    s = proj.shape[0]
    cw = w.shape[1]
    rows = CONV_ROWS
    blk_a = (proj.shape[1] - 2 * cw) // cw
    n_chunk = s // rows
    n_send = len(sends)

    def body(a_ref, g_ref, y_ref, dc_ref, w_ref, lg_ref, lb_ref, *rest):
        send_refs, (o_ref, dw_ref, db_ref, dlg_ref, dlb_ref), rest = rest[:n_send], rest[n_send:n_send + 5], rest[n_send + 5:]
        got_refs, (u_s, dy_s, dw_s), sems = rest[:n_send], rest[n_send:n_send + 3], rest[n_send + 3:]
        copies = _scatter_copies(send_refs, got_refs, sems)
        for cp in copies:
            cp.start()
        u_s[0:CONV_HALO, :] = jnp.zeros((CONV_HALO, cw), F32)
        dy_s[pl.ds(s, CONV_HALO), :] = jnp.zeros((CONV_HALO, cw), F32)
        dw_s[...] = jnp.zeros_like(dw_s)

        def glu(i, _):
            r0 = pl.multiple_of(i * rows, rows)
            u_s[pl.ds(CONV_HALO + r0, rows), :] = a_ref[pl.ds(r0, rows), :] * _sigmoid(g_ref[pl.ds(r0, rows), :])
            return 0

        lax.fori_loop(0, n_chunk, glu, 0)

        def chunk(i, carry):
            db, dlg, dlb = carry
            r0 = pl.multiple_of(i * rows, rows)
            y = y_ref[pl.ds(r0, rows), :]
            yc = y - jnp.mean(y, axis=-1, keepdims=True)
            r = lax.rsqrt(jnp.mean(yc * yc, axis=-1, keepdims=True) + EPS)
            n = yc * r
            ln = n * lg_ref[...] + lb_ref[...]
            sg = _sigmoid(ln)
            dln = dc_ref[pl.ds(r0, rows), :] * (sg * (1.0 + ln * (1.0 - sg)))
            dn = dln * lg_ref[...]
            dy = r * (dn - jnp.mean(dn, axis=-1, keepdims=True) - n * jnp.mean(dn * n, axis=-1, keepdims=True))
            dy_s[pl.ds(r0, rows), :] = dy
            for lanes in _lane_blocks(cw):
                shifts = _row_shifts(u_s[pl.ds(r0, CONV_HALO + rows), lanes], False)
                dy_part = dy[:, lanes]
                for k in range(CONV_KERNEL):
                    prod = _earlier(shifts, CONV_KERNEL - 1 - k, rows) * dy_part
                    dw_s[k, :, lanes] += jnp.sum(prod.reshape(rows // SUBLANES, SUBLANES, LANES), axis=0)
            return (db + jnp.sum(dy, axis=0, keepdims=True),
                    dlg + jnp.sum(dln * n, axis=0, keepdims=True),
                    dlb + jnp.sum(dln, axis=0, keepdims=True))

        zero = jnp.zeros((1, cw), F32)
        db, dlg, dlb = lax.fori_loop(0, n_chunk, chunk, (zero, zero, zero))
        db_ref[...] = db
        dlg_ref[...] = dlg
        dlb_ref[...] = dlb
        for k in range(CONV_KERNEL):
            dw_ref[k:k + 1, :] = jnp.sum(dw_s[k], axis=0, keepdims=True)

        def back(i, _):
            r0 = pl.multiple_of(i * rows, rows)
            for lanes in _lane_blocks(cw):
                shifts = _row_shifts(dy_s[pl.ds(r0, rows + CONV_HALO), lanes], True)
                du = None
                for k in range(CONV_KERNEL):
                    term = _later(shifts, CONV_KERNEL - 1 - k, rows) * w_ref[k:k + 1, lanes]
                    du = term if du is None else du + term
                av = a_ref[pl.ds(r0, rows), lanes]
                sg = _sigmoid(g_ref[pl.ds(r0, rows), lanes])
                o_ref[pl.ds(r0, rows), lanes] = (du * sg).astype(BF16)
                o_ref[pl.ds(r0, rows), slice(cw + lanes.start, cw + lanes.stop)] = (du * av * sg * (1.0 - sg)).astype(BF16)
            return 0

        lax.fori_loop(0, n_chunk, back, 0)
        for cp in copies:
            cp.wait()

    vec = pl.BlockSpec((1, cw), lambda i: (0, 0))
    wspec = pl.BlockSpec((CONV_KERNEL, cw), lambda i: (0, 0))
    out = pl.pallas_call(
        body, name=name, grid=(1,),
        in_specs=[pl.BlockSpec((s, cw), lambda i: (0, blk_a)), pl.BlockSpec((s, cw), lambda i: (0, blk_a + 1)),
                  pl.BlockSpec((s, cw), lambda i: (0, 0)), pl.BlockSpec((s, cw), lambda i: (0, 1)), wspec, vec, vec]
        + [ANY_SPEC] * n_send,
        out_specs=[pl.BlockSpec((s, 2 * cw), lambda i: (0, 0)), wspec, vec, vec, vec] + [ANY_SPEC] * n_send,
        out_shape=[jax.ShapeDtypeStruct((s, 2 * cw), BF16), jax.ShapeDtypeStruct((CONV_KERNEL, cw), F32)]
        + [jax.ShapeDtypeStruct((1, cw), F32)] * 3 + _scattered_shapes(sends),
        scratch_shapes=[pltpu.VMEM((CONV_HALO + s, cw), F32), pltpu.VMEM((s + CONV_HALO, cw), F32),
                        pltpu.VMEM((CONV_KERNEL, 8, cw), F32)] + _exchange_sems(n_send),
        compiler_params=_params(1),
    )(proj, proj, y, dcat, w, lg, lb, *sends)
    return out[:5], out[5:]


FFN_ROWS = 256


def _ffn_gate(g_ref, r0, rows, w_ref, b_ref):
    cur = g_ref[pl.ds(r0, rows), :].astype(F32)
    prev = g_ref[pl.ds(pl.multiple_of(jnp.maximum(r0 - FFN_HALO, 0), FFN_HALO), FFN_HALO), :].astype(F32)
    prev = jnp.where(r0 > 0, prev, 0.0)
    window = jnp.concatenate([prev, cur], axis=0)
    gc = cur * w_ref[FFN_KERNEL - 1:FFN_KERNEL, :] + b_ref[...]
    for k in range(FFN_KERNEL - 1):
        gc = gc + _shifted(window, FFN_KERNEL - 1 - k, FFN_HALO) * w_ref[k:k + 1, :]
    return gc, window


def _ffn_fwd(up, w, b, name):
    s = up.shape[0]
    f = w.shape[1]
    tc = _pick(f, (256, 128))
    nc = f // tc
    rows = _pick(s, (FFN_ROWS, 128))

    def body(g_ref, v_ref, w_ref, b_ref, o_ref):
        def chunk(i, _):
            r0 = pl.multiple_of(i * rows, rows)
            gc, _w = _ffn_gate(g_ref, r0, rows, w_ref, b_ref)
            o_ref[pl.ds(r0, rows), :] = (gc * _sigmoid(gc) * v_ref[pl.ds(r0, rows), :].astype(F32)).astype(BF16)
            return 0

        lax.fori_loop(0, s // rows, chunk, 0)

    return pl.pallas_call(
        body, name=name, grid=(nc,),
        in_specs=[pl.BlockSpec((s, tc), lambda j: (0, j)), pl.BlockSpec((s, tc), lambda j: (0, nc + j)),
                  pl.BlockSpec((FFN_KERNEL, tc), lambda j: (0, j)), pl.BlockSpec((1, tc), lambda j: (0, j))],
        out_specs=pl.BlockSpec((s, tc), lambda j: (0, j)),
        out_shape=jax.ShapeDtypeStruct((s, f), BF16),
        compiler_params=_params(1),
    )(up, up, w, b)


def _ffn_bwd(up, dact, w, b, name):
    s = up.shape[0]
    f = w.shape[1]
    tc = _pick(f, (256, 128))
    nc = f // tc
    rows = _pick(s, (FFN_ROWS, 128))
    n_chunk = s // rows

    def body(g_ref, v_ref, da_ref, w_ref, b_ref, d_ref, dw_ref, db_ref, dgc_s):
        dg_ref, dv_ref = d_ref.at[0], d_ref.at[1]
        dgc_s[pl.ds(s, FFN_HALO), :] = jnp.zeros((FFN_HALO, tc), F32)

        def chunk(i, carry):
            r0 = pl.multiple_of(i * rows, rows)
            gc, window = _ffn_gate(g_ref, r0, rows, w_ref, b_ref)
            sg = _sigmoid(gc)
            da = da_ref[pl.ds(r0, rows), :].astype(F32)
            dv_ref[pl.ds(r0, rows), :] = (da * gc * sg).astype(BF16)
            dgc = da * v_ref[pl.ds(r0, rows), :].astype(F32) * (sg * (1.0 + gc * (1.0 - sg)))
            dgc_s[pl.ds(r0, rows), :] = dgc
            out = [carry[0] + jnp.sum(dgc, axis=0, keepdims=True)]
            for k in range(FFN_KERNEL):
                out.append(carry[1 + k] + jnp.sum(_shifted(window, FFN_KERNEL - 1 - k, FFN_HALO) * dgc,
                                                  axis=0, keepdims=True))
            return tuple(out)

        zero = jnp.zeros((1, tc), F32)
        sums = lax.fori_loop(0, n_chunk, chunk, (zero,) * (1 + FFN_KERNEL))
        db_ref[...] = sums[0]
        for k in range(FFN_KERNEL):
            dw_ref[k:k + 1, :] = sums[1 + k]

        def back(i, _):
            r0 = pl.multiple_of(i * rows, rows)
            window = dgc_s[pl.ds(r0, rows + FFN_HALO), :]
            dg = window[:rows, :] * w_ref[FFN_KERNEL - 1:FFN_KERNEL, :]
            for k in range(FFN_KERNEL - 1):
                sh = FFN_KERNEL - 1 - k
                dg = dg + pltpu.roll(window, rows + FFN_HALO - sh, 0)[:rows, :] * w_ref[k:k + 1, :]
            dg_ref[pl.ds(r0, rows), :] = dg.astype(BF16)
            return 0

        lax.fori_loop(0, n_chunk, back, 0)

    blk = lambda off: pl.BlockSpec((s, tc), lambda j: (0, off + j))
    return pl.pallas_call(
        body, name=name, grid=(nc,),
        in_specs=[blk(0), blk(nc), blk(0), pl.BlockSpec((FFN_KERNEL, tc), lambda j: (0, j)),
                  pl.BlockSpec((1, tc), lambda j: (0, j))],
        out_specs=[pl.BlockSpec((2, s, tc), lambda j: (0, 0, j)), pl.BlockSpec((FFN_KERNEL, tc), lambda j: (0, j)),
                   pl.BlockSpec((1, tc), lambda j: (0, j))],
        out_shape=[jax.ShapeDtypeStruct((2, s, f), BF16),
                   jax.ShapeDtypeStruct((FFN_KERNEL, f), F32), jax.ShapeDtypeStruct((1, f), F32)],
        scratch_shapes=[pltpu.VMEM((s + FFN_HALO, tc), F32)],
        compiler_params=_params(1),
    )(up, up, dact, w, b)


def _loss_head(y, target, name):
    m, n = y.shape
    tm = _pick(m, (256, 128))

    def body(y_ref, t_ref, l_ref, d_ref, db_ref):
        e = y_ref[...] - t_ref[...]
        part = 0.5 * jnp.sum(jnp.sum(e * e, axis=-1, keepdims=True) / n, axis=0, keepdims=True)

        @pl.when(pl.program_id(0) == 0)
        def _():
            l_ref[...] = jnp.zeros_like(l_ref)

        l_ref[...] += part
        d = e / n
        d_ref[...] = d
        db_ref[...] = d.astype(BF16)

    return pl.pallas_call(
        body, name=name, grid=(m // tm,),
        in_specs=[pl.BlockSpec((tm, n), lambda i: (i, 0))] * 2,
        out_specs=[pl.BlockSpec((8, LANES), lambda i: (0, 0)), pl.BlockSpec((tm, n), lambda i: (i, 0)),
                   pl.BlockSpec((tm, n), lambda i: (i, 0))],
        out_shape=[jax.ShapeDtypeStruct((8, LANES), F32), jax.ShapeDtypeStruct((m, n), F32),
                   jax.ShapeDtypeStruct((m, n), BF16)],
        compiler_params=_params(1),
    )(y, target)


def _adamw_math(w, g, m, v):
    m = ADAM_B1 * m + (1.0 - ADAM_B1) * g
    v = ADAM_B2 * v + (1.0 - ADAM_B2) * (g * g)
    m_hat = m / (1.0 - ADAM_B1 ** ADAM_STEP)
    v_hat = v / (1.0 - ADAM_B2 ** ADAM_STEP)
    delta = -ADAM_LR * (m_hat / (jnp.sqrt(v_hat) + ADAM_EPS) + ADAM_WD * w)
    return delta, m, v


def _sum_adamw(parts, w, m, v, sends, name):
    depth, r, c = w.shape
    parts = [list(p) if isinstance(p, (list, tuple)) else [p] for p in parts]
    tr = max(t for t in range(16, min(min(a.shape[1] for p in parts for a in p), 192) + 1, 16)
             if all(a.shape[1] % t == 0 for p in parts for a in p))
    steps = r // tr
    n_send = len(sends)
    chunks = []
    for layer, p in enumerate(parts):
        s0 = 0
        for a in p:
            chunks.append((layer, s0, a.shape[1] // tr))
            s0 += a.shape[1] // tr
    n_chunk = len(chunks)

    def body(*refs):
        p_refs, refs = refs[:n_chunk], refs[n_chunk:]
        (w_ref, m_ref, v_ref), send_refs, refs = refs[:3], refs[3:3 + n_send], refs[3 + n_send:]
        (g_out, d_out, m_out, v_out), got_refs, sems = refs[:4], refs[4:4 + n_send], refs[4 + n_send:]
        is_first = (pl.program_id(0) == 0) & (pl.program_id(1) == 0)
        is_last = (pl.program_id(0) == depth - 1) & (pl.program_id(1) == steps - 1)
        if n_send:
            start, relay, finish = _two_level_gather(send_refs, got_refs, sems)
            pl.when(is_first)(start)
            pl.when(is_last)(relay)
        for p_ref, (layer, s0, n) in zip(p_refs, chunks):
            step = pl.program_id(1)

            @pl.when((pl.program_id(0) == layer) & (step >= s0) & (step < s0 + n))
            def _():
                g = p_ref[0].astype(F32)
                for src in range(1, N_DEV):
                    g = g + p_ref[src].astype(F32)
                d, mn, vn = _adamw_math(w_ref[0], g, m_ref[0], v_ref[0])
                g_out[0] = g
                d_out[0] = d
                m_out[0] = mn
                v_out[0] = vn

        if n_send:
            pl.when(is_last)(finish)

    def part_spec(layer, s0, n):
        return pl.BlockSpec((N_DEV, tr, c), lambda l, i: (0, jnp.clip((l - layer) * steps + i - s0, 0, n - 1), 0))

    blk = pl.BlockSpec((1, tr, c), lambda l, i: (l, i, 0))
    out = pl.pallas_call(
        body, name=name, grid=(depth, steps),
        in_specs=[part_spec(*ch) for ch in chunks] + [blk, blk, blk] + [ANY_SPEC] * n_send,
        out_specs=[blk] * 4 + [ANY_SPEC] * n_send,
        out_shape=[jax.ShapeDtypeStruct(w.shape, F32)] * 4 + _gathered_shapes(sends),
        scratch_shapes=_exchange_sems(n_send) if n_send else [],
        compiler_params=_params(2),
    )(*[a for p in parts for a in p], w, m, v, *sends)
    return out[:4], out[4:]


VMEM_SPEC = pl.BlockSpec(memory_space=pltpu.VMEM)


def _sum_small(parts, depth, name):
    n = len(parts) // depth

    def body(*refs):
        p_refs, o_refs = refs[:n * depth], refs[n * depth:]
        for i in range(n):
            for l in range(depth):
                p_ref = p_refs[i * depth + l]
                g = p_ref[0]
                for src in range(1, N_DEV):
                    g = g + p_ref[src]
                o_refs[i][l] = g

    return pl.pallas_call(
        body, name=name, in_specs=[VMEM_SPEC] * (n * depth), out_specs=[VMEM_SPEC] * n,
        out_shape=[jax.ShapeDtypeStruct((depth,) + parts[i * depth].shape[1:], F32) for i in range(n)],
        compiler_params=_params(),
    )(*parts)


def _adamw_small(ws, gs, ms, vs, name):
    n = len(ws)

    def body(*refs):
        ins, outs = refs[:4 * n], refs[4 * n:]
        for i in range(n):
            d, mn, vn = _adamw_math(ins[i][...], ins[n + i][...], ins[2 * n + i][...], ins[3 * n + i][...])
            outs[i][...] = d
            outs[n + i][...] = mn
            outs[2 * n + i][...] = vn

    out = pl.pallas_call(
        body, name=name, in_specs=[VMEM_SPEC] * (4 * n), out_specs=[VMEM_SPEC] * (3 * n),
        out_shape=[jax.ShapeDtypeStruct(w.shape, F32) for w in ws] * 3,
        compiler_params=_params(),
    )(*ws, *gs, *ms, *vs)
    return out[:n], out[n:2 * n], out[2 * n:]


def kernel(x, norm1_g, w_in, q_norm_g, k_norm_g, conv_dw_w, conv_dw_b, conv_ln_g, conv_ln_b, w_out, norm2_g, w_up, ffn_dw_w, ffn_dw_b, w_down, loss_target, m_norm1_g, m_w_in, m_q_norm_g, m_k_norm_g, m_conv_dw_w, m_conv_dw_b, m_conv_ln_g, m_conv_ln_b, m_w_out, m_norm2_g, m_w_up, m_ffn_dw_w, m_ffn_dw_b, m_w_down, v_norm1_g, v_w_in, v_q_norm_g, v_k_norm_g, v_conv_dw_w, v_conv_dw_b, v_conv_ln_g, v_conv_ln_b, v_w_out, v_norm2_g, v_w_up, v_ffn_dw_w, v_ffn_dw_b, v_w_down):
    depth = w_in.shape[0]
    d_ff = w_down.shape[1] * N_DEV
    conv_w = conv_dw_b.shape[1]
    cw_shard = conv_dw_w.shape[2]
    fw_shard = ffn_dw_w.shape[2]
    me = 4 * lax.axis_index("x") + 2 * lax.axis_index("y") + lax.axis_index("c")

    transposed = lambda a: a.transpose(0, 2, 1)
    b_in, b_out, b_up, b_down = (transposed(w_in).astype(BF16), w_out.astype(BF16), transposed(w_up).astype(BF16),
                                 w_down.astype(BF16))
    rows_major = lambda g: g.reshape(N_DEV * g.shape[1], g.shape[2])
    g_in0, g_cw, g_fw = _gather([b_in[0], conv_dw_w, ffn_dw_w], name="gather_first")
    wf_in, wf_out, wf_up, wf_down = [rows_major(g_in0)] + [None] * (depth - 1), [None] * depth, [None] * depth, [None] * depth
    cwf = g_cw.transpose(1, 2, 0, 3).reshape(depth, CONV_KERNEL, conv_w)
    fwf = g_fw.transpose(1, 2, 0, 3).reshape(depth, FFN_KERNEL, d_ff)

    row = lambda a, l: a[l].reshape(1, -1)
    both_heads = lambda a, l: jnp.tile(row(a, l), (1, HEADS_PER_STEP))

    xs = x[0]
    saved = []
    for l in range(depth):
        proj, h1, _ = _mm_rms(xs, row(norm1_g, l), wf_in[l], F32, [], name="fwd_in")
        sends = [b_out[l], b_up[l]] + ([b_in[l + 1]] if l + 1 < depth else [])
        attn, tsum, got = _attn_fwd(proj, both_heads(q_norm_g, l), both_heads(k_norm_g, l), sends, name="fwd_attn")
        wf_out[l], wf_up[l] = rows_major(got[0]), rows_major(got[1])
        if l + 1 < depth:
            wf_in[l + 1] = rows_major(got[2])
        conv, conv_y = _conv_fwd(proj, cwf[l], row(conv_dw_b, l), row(conv_ln_g, l), row(conv_ln_b, l),
                                 name="fwd_conv")
        cat = jnp.concatenate([attn, conv], axis=1)
        x_mid = _mm_res(cat, wf_out[l], xs, name="fwd_out")
        up, h2, got = _mm_rms(x_mid, row(norm2_g, l), wf_up[l], BF16, [b_down[l]], name="fwd_up")
        wf_down[l] = rows_major(got[0])
        act = _ffn_fwd(up, fwf[l], row(ffn_dw_b, l), name="fwd_ffn")
        x_next = _mm_res(act, wf_down[l], x_mid, name="fwd_down")
        saved.append((xs, h1, proj, tsum, cat, x_mid, h2, up, act, conv_y))
        xs = x_next

    loss_tile, dx, dxb = _loss_head(xs, loss_target[0], name="loss_head")
    loss = lax.psum(loss_tile[0, 0], ("x", "y", "c"))

    r_in, r_out, r_up, r_down = [None] * depth, [None] * depth, [None] * depth, [None] * depth
    row_blocks = lambda g: g.reshape(N_DEV, g.shape[0] // N_DEV, g.shape[1])
    small = {k: [None] * depth for k in ("norm1_g", "q_norm_g", "k_norm_g", "conv_dw_w", "conv_dw_b", "conv_ln_g",
                                         "conv_ln_b", "norm2_g", "ffn_dw_w", "ffn_dw_b")}
    gw_in = None
    for l in reversed(range(depth)):
        xs, h1, proj, tsum, cat, x_mid, h2, up, act, conv_y = saved[l]
        dact = _mm_nt(dxb, wf_down[l], BF16, name="bwd_dact")
        gw_down = _mm_tn(act, dxb, name="bwd_gw_down")
        dup, small["ffn_dw_w"][l], small["ffn_dw_b"][l] = _ffn_bwd(up, dact, fwf[l], row(ffn_dw_b, l), name="bwd_ffn")
        gw_up = _mm_tn(dup, h2, name="bwd_gw_up")
        if l + 1 < depth:
            blocks = row_blocks(gw_in)
            half = blocks.shape[1] // 2
            in_halves = [blocks[:, :half], blocks[:, half:]]
        dx, dxb, small["norm2_g"][l], got_a = _mm_rmsbwd(dup, wf_up[l], x_mid, row(norm2_g, l), dx,
                                                         in_halves[:1] if l + 1 < depth else [], name="bwd_up")
        dcat = _mm_nt(dxb, wf_out[l], F32, name="bwd_dcat")
        gw_out = _mm_tn(cat, dxb, name="bwd_gw_out")
        early = in_halves[1] if l + 1 < depth else row_blocks(gw_out)
        (dglu, small["conv_dw_w"][l], small["conv_dw_b"][l], small["conv_ln_g"][l], small["conv_ln_b"][l]), got = (
            _conv_bwd(proj, conv_y, dcat, cwf[l], row(conv_ln_g, l), row(conv_ln_b, l), [early], name="bwd_conv"))
        sends = [row_blocks(gw_down), row_blocks(gw_up)] + ([row_blocks(gw_out)] if l + 1 < depth else [])
        (dq, dk, dv, dqg, dkg), got2 = _attn_bwd(proj, dcat, tsum, both_heads(q_norm_g, l), both_heads(k_norm_g, l),
                                                 sends, name="bwd_attn")
        r_down[l], r_up[l] = got2[:2]
        if l + 1 < depth:
            r_in[l + 1], r_out[l] = [got_a[0], got[0]], got2[2]
        else:
            r_out[l] = got[0]
        small["q_norm_g"][l] = dqg[0:1, :HEAD_DIM] + dqg[0:1, HEAD_DIM:]
        small["k_norm_g"][l] = dkg[0:1, :HEAD_DIM] + dkg[0:1, HEAD_DIM:]
        dproj = jnp.concatenate([dq, dk, dv, dglu], axis=1)
        gw_in = _mm_tn(dproj, h1, name="bwd_gw_in")
        sends = [row_blocks(gw_in)] if l == 0 else []
        dx, dxb, small["norm1_g"][l], got = _mm_rmsbwd(dproj, wf_in[l], xs, row(norm1_g, l), dx, sends, name="bwd_in")
        if l == 0:
            r_in[0] = got[0]
    grad_x = dx[None]

    names = ["norm1_g", "q_norm_g", "k_norm_g", "conv_dw_w", "conv_dw_b", "conv_ln_g", "conv_ln_b", "norm2_g",
             "ffn_dw_w", "ffn_dw_b"]
    full_shapes = {"norm1_g": norm1_g.shape, "q_norm_g": q_norm_g.shape, "k_norm_g": k_norm_g.shape,
                   "conv_dw_w": (depth, CONV_KERNEL, conv_w), "conv_dw_b": conv_dw_b.shape,
                   "conv_ln_g": conv_ln_g.shape, "conv_ln_b": conv_ln_b.shape, "norm2_g": norm2_g.shape,
                   "ffn_dw_w": (depth, FFN_KERNEL, d_ff), "ffn_dw_b": ffn_dw_b.shape}
    partial = [small[k][l] for k in names for l in range(depth)]
    big = {}
    big["w_out"], all_partials = _sum_adamw(r_out, w_out, m_w_out, v_w_out, partial, name="adamw_out")
    big["w_up"], _ = _sum_adamw(r_up, transposed(w_up), transposed(m_w_up), transposed(v_w_up), [], name="adamw_up")
    big["w_down"], _ = _sum_adamw(r_down, w_down, m_w_down, v_w_down, [], name="adamw_down")
    big["w_in"], _ = _sum_adamw(r_in, transposed(w_in), transposed(m_w_in), transposed(v_w_in), [], name="adamw_in")
    for k in ("w_in", "w_up"):
        big[k] = [transposed(a) for a in big[k]]

    sums = _sum_small(all_partials, depth, name="sum_small_grads")
    grads = {k: g.reshape(full_shapes[k]) for k, g in zip(names, sums)}
    grads["conv_dw_w"] = lax.dynamic_slice_in_dim(grads["conv_dw_w"], me * cw_shard, cw_shard, axis=2)
    grads["ffn_dw_w"] = lax.dynamic_slice_in_dim(grads["ffn_dw_w"], me * fw_shard, fw_shard, axis=2)
    weights = dict(norm1_g=norm1_g, q_norm_g=q_norm_g, k_norm_g=k_norm_g, conv_dw_w=conv_dw_w, conv_dw_b=conv_dw_b,
                   conv_ln_g=conv_ln_g, conv_ln_b=conv_ln_b, norm2_g=norm2_g, ffn_dw_w=ffn_dw_w, ffn_dw_b=ffn_dw_b)
    m_in = dict(norm1_g=m_norm1_g, q_norm_g=m_q_norm_g, k_norm_g=m_k_norm_g, conv_dw_w=m_conv_dw_w,
                conv_dw_b=m_conv_dw_b, conv_ln_g=m_conv_ln_g, conv_ln_b=m_conv_ln_b, norm2_g=m_norm2_g,
                ffn_dw_w=m_ffn_dw_w, ffn_dw_b=m_ffn_dw_b)
    v_in = dict(norm1_g=v_norm1_g, q_norm_g=v_q_norm_g, k_norm_g=v_k_norm_g, conv_dw_w=v_conv_dw_w,
                conv_dw_b=v_conv_dw_b, conv_ln_g=v_conv_ln_g, conv_ln_b=v_conv_ln_b, norm2_g=v_norm2_g,
                ffn_dw_w=v_ffn_dw_w, ffn_dw_b=v_ffn_dw_b)
    d_s, m_s, v_s = _adamw_small([weights[k] for k in names], [grads[k] for k in names], [m_in[k] for k in names],
                                 [v_in[k] for k in names], name="adamw_small")
    delta, new_m, new_v = dict(zip(names, d_s)), dict(zip(names, m_s)), dict(zip(names, v_s))
    for k, (g, d, mn, vn) in big.items():
        grads[k], delta[k], new_m[k], new_v[k] = g, d, mn, vn

    order = ["norm1_g", "w_in", "q_norm_g", "k_norm_g", "conv_dw_w", "conv_dw_b", "conv_ln_g", "conv_ln_b", "w_out",
             "norm2_g", "w_up", "ffn_dw_w", "ffn_dw_b", "w_down"]
    return (loss, grad_x, *[grads[k] for k in order], *[delta[k] for k in order], *[new_m[k] for k in order],
            *[new_v[k] for k in order])
```

```python
import jax
import jax.numpy as jnp
from jax import lax
from jax.experimental import pallas as pl
from jax.experimental.pallas import tpu as pltpu

F32 = jnp.float32
BF16 = jnp.bfloat16

N_DEV = 8
HEADS = 8
HEAD_DIM = 64
ATTN_WIDTH = HEADS * HEAD_DIM
CONV_KERNEL = 31
FFN_KERNEL = 3
EPS = 1e-6
BLK = 128
KEY_GROUP = 4
LANES = 128
NORM_ROWS = 128
CONV_HALO = 32
FFN_HALO = 16

ADAM_LR = 0.001
ADAM_B1 = 0.9
ADAM_B2 = 0.999
ADAM_EPS = 1e-08
ADAM_WD = 0.01
ADAM_STEP = 10

VMEM_LIMIT = 56 * 1024 * 1024


def _params(n_axes=0):
    kw = dict(vmem_limit_bytes=VMEM_LIMIT)
    if n_axes:
        kw["dimension_semantics"] = ("arbitrary",) * n_axes
    return pltpu.CompilerParams(**kw)


def _dot(a, b):
    return jnp.dot(a, b, preferred_element_type=F32)


def _dot_nt(a, b):
    return lax.dot_general(a, b, (((1,), (1,)), ((), ())), preferred_element_type=F32)


def _dot_tn(a, b):
    return lax.dot_general(a, b, (((0,), (0,)), ((), ())), preferred_element_type=F32)


def _sigmoid(x):
    return 1.0 / (1.0 + jnp.exp(-x))


def _split_bf16(x):
    hi = x.astype(BF16)
    lo = (x - hi.astype(F32)).astype(BF16)
    return hi, lo


def _pick(n, options):
    for t in options:
        if n % t == 0:
            return t
    return n


def _tile(n, cap):
    best = None
    for t in range(LANES, min(n, cap) + 1, LANES):
        if n % t == 0:
            best = t
    return best or n


def _mm_rms(x, g, wt, out_dtype, sends, name):
    m, k = x.shape
    n = wt.shape[0]
    tm = _tile(m, 2048)
    tn = _tile(n, 512 if out_dtype == F32 else 1408)
    n_send = len(sends)
    grid = (m // tm, n // tn)

    def body(x_ref, g_ref, w_ref, *rest):
        send_refs, (o_ref, h_ref), rest = rest[:n_send], rest[n_send:n_send + 2], rest[n_send + 2:]
        got_refs, h_s, sems = rest[:n_send], rest[n_send], rest[n_send + 1:]
        if n_send:
            start, relay, finish = _two_level_gather(send_refs, got_refs, sems)
            is_first = (pl.program_id(0) == 0) & (pl.program_id(1) == 0)
            is_last = (pl.program_id(0) == grid[0] - 1) & (pl.program_id(1) == grid[1] - 1)
            pl.when(is_first)(start)
            pl.when(is_last)(relay)

        @pl.when(pl.program_id(1) == 0)
        def _():
            def chunk(c, _):
                rows = pl.ds(pl.multiple_of(c * NORM_ROWS, NORM_ROWS), NORM_ROWS)
                xv = x_ref[rows, :]
                r = lax.rsqrt(jnp.mean(xv * xv, axis=-1, keepdims=True) + EPS)
                hv = (xv * r * g_ref[...]).astype(BF16)
                h_s[rows, :] = hv
                h_ref[rows, :] = hv
                return 0

            lax.fori_loop(0, tm // NORM_ROWS, chunk, 0)

        o_ref[...] = _dot_nt(h_s[...], w_ref[...]).astype(out_dtype)
        if n_send:
            pl.when(is_last)(finish)

    out = pl.pallas_call(
        body, name=name, grid=grid,
        in_specs=[pl.BlockSpec((tm, k), lambda i, j: (i, 0)),
                  pl.BlockSpec((1, k), lambda i, j: (0, 0)),
                  pl.BlockSpec((tn, k), lambda i, j: (j, 0))] + [ANY_SPEC] * n_send,
        out_specs=[pl.BlockSpec((tm, tn), lambda i, j: (i, j)),
                   pl.BlockSpec((tm, k), lambda i, j: (i, 0))] + [ANY_SPEC] * n_send,
        out_shape=[jax.ShapeDtypeStruct((m, n), out_dtype), jax.ShapeDtypeStruct((m, k), BF16)]
        + _gathered_shapes(sends),
        scratch_shapes=[pltpu.VMEM((tm, k), BF16)] + (_exchange_sems(n_send) if n_send else []),
        compiler_params=_params(2),
    )(x, g, wt, *sends)
    return out[0], out[1], out[2:]


def _mm_res(a, w, res, name):
    m, k = a.shape
    n = w.shape[1]
    tm = _tile(m, 1024)
    tn = _tile(n, 512)

    def body(a_ref, w_ref, r_ref, o_ref):
        o_ref[...] = r_ref[...] + _dot(a_ref[...], w_ref[...])

    return pl.pallas_call(
        body, name=name, grid=(m // tm, n // tn),
        in_specs=[pl.BlockSpec((tm, k), lambda i, j: (i, 0)),
                  pl.BlockSpec((k, tn), lambda i, j: (0, j)),
                  pl.BlockSpec((tm, tn), lambda i, j: (i, j))],
        out_specs=pl.BlockSpec((tm, tn), lambda i, j: (i, j)),
        out_shape=jax.ShapeDtypeStruct((m, n), F32),
        compiler_params=_params(2),
    )(a, w, res)


def _mm_nt(a, w, out_dtype, name):
    m, k = a.shape
    n = w.shape[0]
    tm = _tile(m, 1024)
    tn = _tile(n, 1408)

    def body(a_ref, w_ref, o_ref):
        o_ref[...] = _dot_nt(a_ref[...], w_ref[...]).astype(out_dtype)

    return pl.pallas_call(
        body, name=name, grid=(m // tm, n // tn),
        in_specs=[pl.BlockSpec((tm, k), lambda i, j: (i, 0)),
                  pl.BlockSpec((tn, k), lambda i, j: (j, 0))],
        out_specs=pl.BlockSpec((tm, tn), lambda i, j: (i, j)),
        out_shape=jax.ShapeDtypeStruct((m, n), out_dtype),
        compiler_params=_params(2),
    )(a, w)


def _column_tiles(a, cap):
    if a.ndim == 2:
        s, c = a.shape
        tc = _tile(c, cap)
        return s, c, tc, lambda rows, index: pl.BlockSpec((rows, tc), lambda *g: (index(*g)[0], index(*g)[1]))
    slabs, s, width = a.shape
    tc = _tile(width, cap)
    per = width // tc
    return s, slabs * width, tc, lambda rows, index: pl.BlockSpec(
        (None, rows, tc), lambda *g: (index(*g)[1] // per, index(*g)[0], index(*g)[1] % per))


def _mm_tn(a, b, name):
    s, m, tm, a_spec = _column_tiles(a, 1408)
    n = b.shape[1]
    tn = _tile(n, 1024)

    def body(a_ref, b_ref, o_ref):
        o_ref[...] = _dot_tn(a_ref[...], b_ref[...]).astype(BF16)

    return pl.pallas_call(
        body, name=name, grid=(m // tm, n // tn),
        in_specs=[a_spec(s, lambda i, j: (0, i)),
                  pl.BlockSpec((s, tn), lambda i, j: (0, j))],
        out_specs=pl.BlockSpec((tm, tn), lambda i, j: (i, j)),
        out_shape=jax.ShapeDtypeStruct((m, n), BF16),
        compiler_params=_params(2),
    )(a, b)


def _mm_rmsbwd(a, w, x, g, dres, sends, name):
    m, k, tk, a_spec = _column_tiles(a, 1408)
    n = w.shape[1]
    tm = _tile(m, 1024)
    nk = k // tk
    n_send = len(sends)

    def body(a_ref, w_ref, x_ref, g_ref, r_ref, *rest):
        send_refs, (dx_ref, dxb_ref, dg_ref), rest = rest[:n_send], rest[n_send:n_send + 3], rest[n_send + 3:]
        got_refs, acc, sems = rest[:n_send], rest[n_send], rest[n_send + 1:]
        i, kk = pl.program_id(0), pl.program_id(1)
        if n_send:
            copies = _scatter_copies(send_refs, got_refs, sems)

            @pl.when((i == 0) & (kk == 0))
            def _():
                for cp in copies:
                    cp.start()

        part = _dot(a_ref[...], w_ref[...])

        @pl.when(kk == 0)
        def _():
            acc[...] = part

        @pl.when(kk > 0)
        def _():
            acc[...] += part

        @pl.when(kk == nk - 1)
        def _():
            def chunk(c, dgp):
                rows = pl.ds(pl.multiple_of(c * NORM_ROWS, NORM_ROWS), NORM_ROWS)
                dh = acc[rows, :]
                xv = x_ref[rows, :]
                r = lax.rsqrt(jnp.mean(xv * xv, axis=-1, keepdims=True) + EPS)
                xh = xv * r
                dxh = dh * g_ref[...]
                dx = r_ref[rows, :] + r * (dxh - xh * jnp.mean(dxh * xh, axis=-1, keepdims=True))
                dx_ref[rows, :] = dx
                dxb_ref[rows, :] = dx.astype(BF16)
                return dgp + jnp.sum(dh * xh, axis=0, keepdims=True)

            dgp = lax.fori_loop(0, tm // NORM_ROWS, chunk, jnp.zeros((1, n), F32))

            @pl.when(i == 0)
            def _():
                dg_ref[...] = dgp

            @pl.when(i > 0)
            def _():
                dg_ref[...] += dgp

        if n_send:
            @pl.when((i == m // tm - 1) & (kk == nk - 1))
            def _():
                for cp in copies:
                    cp.wait()

    out = pl.pallas_call(
        body, name=name, grid=(m // tm, nk),
        in_specs=[a_spec(tm, lambda i, kk: (i, kk)),
                  pl.BlockSpec((tk, n), lambda i, kk: (kk, 0)),
                  pl.BlockSpec((tm, n), lambda i, kk: (i, 0)),
                  pl.BlockSpec((1, n), lambda i, kk: (0, 0)),
                  pl.BlockSpec((tm, n), lambda i, kk: (i, 0))] + [ANY_SPEC] * n_send,
        out_specs=[pl.BlockSpec((tm, n), lambda i, kk: (i, 0)),
                   pl.BlockSpec((tm, n), lambda i, kk: (i, 0)),
                   pl.BlockSpec((1, n), lambda i, kk: (0, 0))] + [ANY_SPEC] * n_send,
        out_shape=[jax.ShapeDtypeStruct((m, n), F32), jax.ShapeDtypeStruct((m, n), BF16),
                   jax.ShapeDtypeStruct((1, n), F32)] + _scattered_shapes(sends),
        scratch_shapes=[pltpu.VMEM((tm, n), F32)] + (_exchange_sems(n_send) if n_send else []),
        compiler_params=_params(2),
    )(a, w, x, g, dres, *sends)
    return out[0], out[1], out[2], out[3:]


ANY_SPEC = pl.BlockSpec(memory_space=pl.ANY)
SEMS_PER_OPERAND = N_DEV - 1


def _exchange_sems(n):
    return [pltpu.SemaphoreType.DMA((n, SEMS_PER_OPERAND)), pltpu.SemaphoreType.DMA((n, SEMS_PER_OPERAND)),
            pltpu.SemaphoreType.DMA((n,))]


def _gathered_shapes(parts):
    return [jax.ShapeDtypeStruct((N_DEV,) + a.shape, a.dtype) for a in parts]


def _scattered_shapes(parts):
    return [jax.ShapeDtypeStruct(a.shape, a.dtype) for a in parts]


def _flat(pos):
    return 4 * pos[0] + 2 * pos[1] + pos[2]


def _remote(src, dst, sems, i, k, to):
    send_sems, recv_sems, _ = sems
    return pltpu.make_async_remote_copy(src_ref=src, dst_ref=dst, send_sem=send_sems.at[i, k],
                                        recv_sem=recv_sems.at[i, k], device_id=to,
                                        device_id_type=pl.DeviceIdType.MESH)


def _scatter_copies(ins, outs, sems):
    x, y, c = lax.axis_index("x"), lax.axis_index("y"), lax.axis_index("c")
    me = _flat((x, y, c))
    copies = [pltpu.make_async_copy(ins[i].at[me], outs[i].at[me], sems[2].at[i]) for i in range(len(ins))]
    for d in range(1, N_DEV):
        peer = (1 - x if d & 4 else x, 1 - y if d & 2 else y, 1 - c if d & 1 else c)
        for i in range(len(ins)):
            copies.append(_remote(ins[i].at[_flat(peer)], outs[i].at[me], sems, i, d - 1, peer))
    return copies


def _two_level_gather(ins, outs, sems):
    x, y, c = lax.axis_index("x"), lax.axis_index("y"), lax.axis_index("c")
    me, sibling = (x, y, c), (x, y, 1 - c)
    chips = [(1 - x, y), (x, 1 - y), (1 - x, 1 - y)]
    n = len(ins)

    def block(i, pos):
        return outs[i].at[_flat(pos)]

    local = [pltpu.make_async_copy(ins[i], block(i, me), sems[2].at[i]) for i in range(n)]
    own = [_remote(ins[i], block(i, me), sems, i, 0, sibling) for i in range(n)]
    own += [_remote(ins[i], block(i, me), sems, i, 1 + j, (*chip, c)) for i in range(n) for j, chip in enumerate(chips)]
    passed = [[_remote(block(i, (*chip, c)), block(i, (*chip, c)), sems, i, 4 + j, sibling) for i in range(n)]
              for j, chip in enumerate(chips)]

    def first():
        for cp in local + own:
            cp.start()

    def relay():
        for j, chip in enumerate(chips):
            for i in range(n):
                _remote(ins[i], block(i, (*chip, c)), sems, i, 1 + j, me).wait_recv()
                passed[j][i].start()

    def finish():
        for i in range(n):
            _remote(ins[i], block(i, sibling), sems, i, 0, me).wait_recv()
            for j, chip in enumerate(chips):
                _remote(ins[i], block(i, (*chip, 1 - c)), sems, i, 4 + j, me).wait_recv()
        for cp in own + [cp for row in passed for cp in row]:
            cp.wait_send()
        for cp in local:
            cp.wait()

    return first, relay, finish


def _gather(parts, name):
    n = len(parts)

    def body(*refs):
        start, relay, finish = _two_level_gather(refs[:n], refs[n:2 * n], refs[2 * n:])
        start()
        relay()
        finish()

    return pl.pallas_call(
        body, name=name, in_specs=[ANY_SPEC] * n, out_specs=[ANY_SPEC] * n,
        out_shape=_gathered_shapes(parts), scratch_shapes=_exchange_sems(n),
    )(*parts)


def _tri(kind):
    j = lax.broadcasted_iota(jnp.int32, (BLK, BLK), 0)
    s = lax.broadcasted_iota(jnp.int32, (BLK, BLK), 1)
    m = {"after": j > s, "upto": j <= s, "before": j < s}[kind]
    return jnp.concatenate([jnp.where(m, 1.0, 0.0), jnp.ones((BLK, BLK), F32)], axis=1).astype(BF16)


def _scan_rows(v, tri):
    r = _dot(v.astype(BF16), tri)
    return r[:, :BLK], r[:, BLK:]


HEADS_PER_STEP = LANES // HEAD_DIM


def _first_head_lanes():
    return lax.broadcasted_iota(jnp.int32, (1, LANES), 1) < HEAD_DIM


def _pair_mean(v, first):
    m0 = jnp.sum(jnp.where(first, v, 0.0), axis=-1, keepdims=True)
    m1 = jnp.sum(jnp.where(first, 0.0, v), axis=-1, keepdims=True)
    return jnp.where(first, m0, m1) * (1.0 / HEAD_DIM)


def _pair_norm(v, g2, first):
    return v * lax.rsqrt(_pair_mean(v * v, first) + EPS) * g2


def _pair_norm_bwd(raw, g2, dn, first):
    r = lax.rsqrt(_pair_mean(raw * raw, first) + EPS)
    xh = raw * r
    dxh = dn * g2
    return r * (dxh - xh * _pair_mean(dxh * xh, first)), jnp.sum(dn * xh, axis=0, keepdims=True)


def _block_diag(v, first):
    zero = jnp.zeros_like(v)
    return jnp.concatenate([jnp.where(first, v, zero), jnp.where(first, zero, v)], axis=0)


def _attn_prep(q_ref, k_ref, v_ref, qg_ref, kg_ref, qc_s, kc_s, vd_s, kd_s, n_blk):
    scale = HEAD_DIM ** -0.5
    first = _first_head_lanes()

    def prep(i, _):
        rows = pl.ds(pl.multiple_of(i * BLK, BLK), BLK)
        both = pl.ds(pl.multiple_of(i * 2 * BLK, 2 * BLK), 2 * BLK)
        qh, ql = _split_bf16(_pair_norm(q_ref[rows, :], qg_ref[...], first) * scale)
        kh, kl = _split_bf16(_pair_norm(k_ref[rows, :], kg_ref[...], first))
        for h in range(HEADS_PER_STEP):
            sl = slice(h * HEAD_DIM, (h + 1) * HEAD_DIM)
            qc_s[h, rows, :] = jnp.concatenate([qh[:, sl], ql[:, sl], qh[:, sl], ql[:, sl]], axis=1)
            kc_s[h, rows, :] = jnp.concatenate([kh[:, sl], kh[:, sl], kl[:, sl], kl[:, sl]], axis=1)
        vd_s[both, :] = _block_diag(v_ref[rows, :].astype(BF16), first)
        if kd_s is not None:
            kd_s[both, :] = _block_diag(kh, first)
        return 0

    lax.fori_loop(0, n_blk, prep, 0)


def _pair_scores(qc, kc_ref, grp, n_tiles=KEY_GROUP):
    zs = []
    for j in range(0, n_tiles, 2):
        two = pl.ds(pl.multiple_of((grp * KEY_GROUP + j) * BLK, 2 * BLK), 2 * BLK)
        z = _dot_nt(qc, kc_ref[two, :])
        zs += [z[:, :BLK], z[:, BLK:]]
    return zs[:n_tiles]


def _col_minus_row():
    row = lax.broadcasted_iota(jnp.int32, (BLK, BLK), 0)
    col = lax.broadcasted_iota(jnp.int32, (BLK, BLK), 1)
    return col - row


def _softplus(z):
    return jnp.maximum(z, 0.0) + jnp.log(1.0 + jnp.exp(-jnp.abs(z)))


def _attn_fwd(proj, qg, kg, sends, name):
    s = proj.shape[0]
    n_blk = s // BLK
    pairs = ATTN_WIDTH // LANES
    n_send = len(sends)

    def body(q_ref, k_ref, v_ref, qg_ref, kg_ref, *rest):
        send_refs, (o_ref, t_ref), rest = rest[:n_send], rest[n_send:n_send + 2], rest[n_send + 2:]
        got_refs, (qc_s, kc_s, vd_s), sems = rest[:n_send], rest[n_send:n_send + 3], rest[n_send + 3:]
        start, relay, finish = _two_level_gather(send_refs, got_refs, sems)
        step = pl.program_id(0)
        pl.when(step == 0)(start)
        pl.when(step == pairs - 1)(relay)
        tri = _tri("after")
        diff = _col_minus_row()
        first = _first_head_lanes()
        heads = range(HEADS_PER_STEP)
        _attn_prep(q_ref, k_ref, v_ref, qg_ref, kg_ref, qc_s, kc_s, vd_s, None, n_blk)

        below = diff < 0

        def group(qc, grp, carry, acc, n_tiles, diagonal):
            blocks = [grp * KEY_GROUP + j for j in reversed(range(n_tiles))]
            zs = [_pair_scores(qc[h], kc_s.at[h], grp, n_tiles)[::-1] for h in heads]
            parts = [[None] * n_tiles for _ in heads]
            for h in heads:
                for j, z in enumerate(zs[h]):
                    sp = _softplus(z)
                    lom = -sp
                    if diagonal and j == 0:
                        lom = jnp.where(below, lom, 0.0)
                    tail, tot = _scan_rows(lom, tri)
                    parts[h][j] = (z - sp + tail, tot)
            carry = list(carry)
            for j, kb in enumerate(blocks):
                ws = []
                for h in heads:
                    lw, tot = parts[h][j]
                    w = jnp.exp(lw + carry[h])
                    if diagonal and j == 0:
                        w = jnp.where(below, w, 0.0)
                    ws.append(w.astype(BF16))
                    carry[h] = carry[h] + tot
                acc = acc + _dot(jnp.concatenate(ws, axis=1),
                                 vd_s[pl.ds(pl.multiple_of(kb * 2 * BLK, 2 * BLK), 2 * BLK), :])
            return tuple(carry), acc

        def q_blocks(top, _):
            for r in range(KEY_GROUP):
                rows = pl.ds(pl.multiple_of((top * KEY_GROUP + r) * BLK, BLK), BLK)
                qc = [qc_s[h, rows, :] for h in heads]
                zero = jnp.zeros((BLK, BLK), F32)
                carry, acc = group(qc, top, (zero,) * HEADS_PER_STEP, jnp.zeros((BLK, LANES), F32), r + 1, True)
                carry, acc = lax.fori_loop(
                    0, top, lambda t, c: group(qc, top - 1 - t, c[0], c[1], KEY_GROUP, False), (carry, acc))
                o_ref[rows, :] = acc.astype(BF16)
                t_ref[rows, :] = jnp.where(first, carry[0], carry[1])
            return 0

        lax.fori_loop(0, n_blk // KEY_GROUP, q_blocks, 0)
        pl.when(step == pairs - 1)(finish)

    col = lambda off: pl.BlockSpec((s, LANES), lambda p: (0, off + p))
    vec = pl.BlockSpec((1, LANES), lambda p: (0, 0))
    out = pl.pallas_call(
        body, name=name, grid=(pairs,),
        in_specs=[col(0), col(pairs), col(2 * pairs), vec, vec] + [ANY_SPEC] * n_send,
        out_specs=[pl.BlockSpec((s, LANES), lambda p: (0, p))] * 2 + [ANY_SPEC] * n_send,
        out_shape=[jax.ShapeDtypeStruct((s, ATTN_WIDTH), BF16), jax.ShapeDtypeStruct((s, ATTN_WIDTH), F32)]
        + _gathered_shapes(sends),
        scratch_shapes=[pltpu.VMEM((HEADS_PER_STEP, s, 4 * HEAD_DIM), BF16)] * 2
        + [pltpu.VMEM((HEADS_PER_STEP * s, LANES), BF16)] + _exchange_sems(n_send),
        compiler_params=_params(1),
    )(proj, proj, proj, qg, kg, *sends)
    return out[0], out[1], out[2:]


def _attn_bwd(proj, dcat, tsum, qg, kg, sends, name):
    s = proj.shape[0]
    n_blk = s // BLK
    pairs = ATTN_WIDTH // LANES
    scale = HEAD_DIM ** -0.5
    n_send = len(sends)
    n_scratch = 7

    def body(q_ref, k_ref, v_ref, do_ref, t_ref, qg_ref, kg_ref, *rest):
        send_refs, rest = rest[:n_send], rest[n_send:]
        (dq_ref, dk_ref, dv_ref, dqg_ref, dkg_ref), rest = rest[:5], rest[5:]
        got_refs, scratch, sems = rest[:n_send], rest[n_send:n_send + n_scratch], rest[n_send + n_scratch:]
        qc_s, kc_s, vd_s, kd_s, qd_s, dob_s, dkv_s = scratch
        copies = _scatter_copies(send_refs, got_refs, sems)

        @pl.when(pl.program_id(0) == 0)
        def _():
            for cp in copies:
                cp.start()

        tri_p = _tri("upto")
        tri_h = _tri("before")
        diff = _col_minus_row()

        @pl.when(pl.program_id(0) == 0)
        def _():
            dqg_ref[...] = jnp.zeros_like(dqg_ref)
            dkg_ref[...] = jnp.zeros_like(dkg_ref)

        first = _first_head_lanes()
        heads = range(HEADS_PER_STEP)
        _attn_prep(q_ref, k_ref, v_ref, qg_ref, kg_ref, qc_s, kc_s, vd_s, kd_s, n_blk)

        def prep(i, _):
            rows = pl.ds(pl.multiple_of(i * BLK, BLK), BLK)
            both = pl.ds(pl.multiple_of(i * 2 * BLK, 2 * BLK), 2 * BLK)
            dob = do_ref[rows, :].astype(BF16)
            dob_s[rows, :] = dob
            none = jnp.zeros((BLK, HEAD_DIM), BF16)
            for h in heads:
                qd_s[h, both, :] = jnp.concatenate(
                    [jnp.concatenate([qc_s[h, rows, 0:HEAD_DIM], none], axis=1),
                     jnp.concatenate([none, dob[:, h * HEAD_DIM:(h + 1) * HEAD_DIM]], axis=1)], axis=0)
                dkv_s[h, rows, :] = jnp.zeros((BLK, LANES), F32)
            return 0

        lax.fori_loop(0, n_blk, prep, 0)

        below = diff < 0

        def group(qc, qd, dob, tq, grp, pc, hc, dq, n_tiles, diagonal):
            blocks = [grp * KEY_GROUP + j for j in range(n_tiles)]
            cols_of = [pl.ds(pl.multiple_of(kb * BLK, BLK), BLK) for kb in blocks]
            both_of = [pl.ds(pl.multiple_of(kb * 2 * BLK, 2 * BLK), 2 * BLK) for kb in blocks]
            on_diagonal = [diagonal and j == n_tiles - 1 for j in range(n_tiles)]
            zs = [_pair_scores(qc[h], kc_s.at[h], grp, n_tiles) for h in heads]
            das = [_dot_nt(dob, vd_s[both, :]) for both in both_of]
            lbs = [[None] * n_tiles for _ in heads]
            scans = [[None] * n_tiles for _ in heads]
            for h in heads:
                for j, z in enumerate(zs[h]):
                    sp = _softplus(z)
                    lom = -sp
                    if on_diagonal[j]:
                        lom = jnp.where(below, lom, 0.0)
                    lbs[h][j] = z - sp
                    scans[h][j] = _scan_rows(lom, tri_p)
            pc, hc = list(pc), list(hc)
            avs = [[None] * n_tiles for _ in heads]
            gws = [[None] * n_tiles for _ in heads]
            hscans = [[None] * n_tiles for _ in heads]
            for h in heads:
                for j in range(n_tiles):
                    p_in, p_tot = scans[h][j]
                    a = jnp.exp(lbs[h][j] + (tq[h] - pc[h] - p_in))
                    if on_diagonal[j]:
                        a = jnp.where(below, a, 0.0)
                    pc[h] = pc[h] + p_tot
                    gw = das[j][:, h * BLK:(h + 1) * BLK] * a
                    avs[h][j] = a.astype(BF16)
                    gws[h][j] = gw
                    hscans[h][j] = _scan_rows(gw, tri_h)
            dzs = [[None] * n_tiles for _ in heads]
            for h in heads:
                for j in range(n_tiles):
                    h_in, g_tot = hscans[h][j]
                    gw = gws[h][j]
                    dz = gw - jnp.exp(lbs[h][j]) * (gw + hc[h] + h_in)
                    if on_diagonal[j]:
                        dz = jnp.where(below, dz, 0.0)
                    hc[h] = hc[h] + g_tot
                    dzs[h][j] = dz.astype(BF16)
            for j, both in enumerate(both_of):
                dq = dq + _dot(jnp.concatenate([dzs[h][j] for h in heads], axis=1), kd_s[both, :])
            for h in heads:
                for j, cols in enumerate(cols_of):
                    dkv_s[h, cols, :] += _dot_tn(jnp.concatenate([dzs[h][j], avs[h][j]], axis=0), qd[h])
            return tuple(pc), tuple(hc), dq

        def q_blocks(top, dqg):
            for r in range(KEY_GROUP):
                qi = top * KEY_GROUP + r
                rows = pl.ds(pl.multiple_of(qi * BLK, BLK), BLK)
                both = pl.ds(pl.multiple_of(qi * 2 * BLK, 2 * BLK), 2 * BLK)
                qc = [qc_s[h, rows, :] for h in heads]
                qd = [qd_s[h, both, :] for h in heads]
                dob = dob_s[rows, :]
                tboth = t_ref[rows, :]
                tq = [jnp.concatenate([tboth[:, h * HEAD_DIM:(h + 1) * HEAD_DIM]] * 2, axis=1) for h in heads]
                zero = (jnp.zeros((BLK, BLK), F32),) * HEADS_PER_STEP
                pc, hc, dq = lax.fori_loop(
                    0, top, lambda grp, c: group(qc, qd, dob, tq, grp, c[0], c[1], c[2], KEY_GROUP, False),
                    (zero, zero, jnp.zeros((BLK, LANES), F32)))
                _, _, dq = group(qc, qd, dob, tq, top, pc, hc, dq, r + 1, True)
                dq_raw, dg = _pair_norm_bwd(q_ref[rows, :], qg_ref[...], dq * scale, first)
                dq_ref[rows, :] = dq_raw.astype(BF16)
                dqg = dqg + dg
            return dqg

        dqg = lax.fori_loop(0, n_blk // KEY_GROUP, q_blocks, jnp.zeros((1, LANES), F32))

        def finish(i, dkg):
            rows = pl.ds(pl.multiple_of(i * BLK, BLK), BLK)
            dk = jnp.concatenate([dkv_s[h, rows, 0:HEAD_DIM] for h in heads], axis=1)
            dv = jnp.concatenate([dkv_s[h, rows, HEAD_DIM:2 * HEAD_DIM] for h in heads], axis=1)
            dk_raw, dg = _pair_norm_bwd(k_ref[rows, :], kg_ref[...], dk, first)
            dk_ref[rows, :] = dk_raw.astype(BF16)
            dv_ref[rows, :] = dv.astype(BF16)
            return dkg + dg

        dkg = lax.fori_loop(0, n_blk, finish, jnp.zeros((1, LANES), F32))
        dqg_ref[0:1, :] += dqg
        dkg_ref[0:1, :] += dkg

        @pl.when(pl.program_id(0) == pairs - 1)
        def _():
            for cp in copies:
                cp.wait()

    col = lambda off: pl.BlockSpec((s, LANES), lambda p: (0, off + p))
    vec = pl.BlockSpec((1, LANES), lambda p: (0, 0))
    small = pl.BlockSpec((8, LANES), lambda p: (0, 0))
    out = pl.pallas_call(
        body, name=name, grid=(pairs,),
        in_specs=[col(0), col(pairs), col(2 * pairs), col(0), col(0), vec, vec] + [ANY_SPEC] * n_send,
        out_specs=[col(0)] * 3 + [small] * 2 + [ANY_SPEC] * n_send,
        out_shape=[jax.ShapeDtypeStruct((s, ATTN_WIDTH), BF16)] * 3 + [jax.ShapeDtypeStruct((8, LANES), F32)] * 2
        + _scattered_shapes(sends),
        scratch_shapes=[pltpu.VMEM((HEADS_PER_STEP, s, 4 * HEAD_DIM), BF16)] * 2
        + [pltpu.VMEM((HEADS_PER_STEP * s, LANES), BF16)] * 2
        + [pltpu.VMEM((HEADS_PER_STEP, HEADS_PER_STEP * s, LANES), BF16), pltpu.VMEM((s, LANES), BF16),
           pltpu.VMEM((HEADS_PER_STEP, s, LANES), F32)] + _exchange_sems(n_send),
        compiler_params=_params(1),
    )(proj, proj, proj, dcat, tsum, qg, kg, *sends)
    return out[:5], out[5:]


CONV_ROWS = 128


def _shifted(window, shift, halo):
    if shift == 0:
        return window[halo:, :]
    return pltpu.roll(window, shift, 0)[halo:, :]


SUBLANES = 8


def _row_shifts(window, up):
    n = window.shape[0]
    return [window] + [pltpu.roll(window, n - b if up else b, 0) for b in range(1, SUBLANES)]


def _earlier(shifts, back, rows):
    a, b = divmod(back, SUBLANES)
    return shifts[b][CONV_HALO - SUBLANES * a:CONV_HALO - SUBLANES * a + rows, :]


def _later(shifts, ahead, rows):
    a, b = divmod(ahead, SUBLANES)
    return shifts[b][SUBLANES * a:SUBLANES * a + rows, :]


def _lane_blocks(width):
    return [slice(c, c + LANES) for c in range(0, width, LANES)]


def _conv_taps(shifts, w_ref, lanes, rows):
    y = None
    for k in range(CONV_KERNEL):
        term = _earlier(shifts, CONV_KERNEL - 1 - k, rows) * w_ref[k:k + 1, lanes]
        y = term if y is None else y + term
    return y


def _conv_fwd(proj, w, b, lg, lb, name):
    s = proj.shape[0]
    cw = w.shape[1]
    rows = CONV_ROWS
    blk_a = (proj.shape[1] - 2 * cw) // cw

    def body(a_ref, g_ref, w_ref, b_ref, lg_ref, lb_ref, o_ref, y_ref, u_s):
        u_s[0:CONV_HALO, :] = jnp.zeros((CONV_HALO, cw), F32)

        def glu(i, _):
            r0 = pl.multiple_of(i * rows, rows)
            u_s[pl.ds(CONV_HALO + r0, rows), :] = a_ref[pl.ds(r0, rows), :] * _sigmoid(g_ref[pl.ds(r0, rows), :])
            return 0

        lax.fori_loop(0, s // rows, glu, 0)

        def chunk(i, _):
            r0 = pl.multiple_of(i * rows, rows)
            for lanes in _lane_blocks(cw):
                shifts = _row_shifts(u_s[pl.ds(r0, CONV_HALO + rows), lanes], False)
                y_ref[pl.ds(r0, rows), lanes] = _conv_taps(shifts, w_ref, lanes, rows) + b_ref[:, lanes]
            y = y_ref[pl.ds(r0, rows), :]
            yc = y - jnp.mean(y, axis=-1, keepdims=True)
            n = yc * lax.rsqrt(jnp.mean(yc * yc, axis=-1, keepdims=True) + EPS)
            ln = n * lg_ref[...] + lb_ref[...]
            o_ref[pl.ds(r0, rows), :] = (ln * _sigmoid(ln)).astype(BF16)
            return 0

        lax.fori_loop(0, s // rows, chunk, 0)

    vec = pl.BlockSpec((1, cw), lambda i: (0, 0))
    return pl.pallas_call(
        body, name=name, grid=(1,),
        in_specs=[pl.BlockSpec((s, cw), lambda i: (0, blk_a)), pl.BlockSpec((s, cw), lambda i: (0, blk_a + 1)),
                  pl.BlockSpec((CONV_KERNEL, cw), lambda i: (0, 0)), vec, vec, vec],
        out_specs=[pl.BlockSpec((s, cw), lambda i: (0, 0))] * 2,
        out_shape=[jax.ShapeDtypeStruct((s, cw), BF16), jax.ShapeDtypeStruct((s, cw), F32)],
        scratch_shapes=[pltpu.VMEM((CONV_HALO + s, cw), F32)],
        compiler_params=_params(1),
    )(proj, proj, w, b, lg, lb)


def _conv_bwd(proj, y, dcat, w, lg, lb, sends, name):
    s = proj.shape[0]
    cw = w.shape[1]
    rows = CONV_ROWS
    blk_a = (proj.shape[1] - 2 * cw) // cw
    n_chunk = s // rows
    n_send = len(sends)

    def body(a_ref, g_ref, y_ref, dc_ref, w_ref, lg_ref, lb_ref, *rest):
        send_refs, (o_ref, dw_ref, db_ref, dlg_ref, dlb_ref), rest = rest[:n_send], rest[n_send:n_send + 5], rest[n_send + 5:]
        got_refs, (u_s, dy_s, dw_s), sems = rest[:n_send], rest[n_send:n_send + 3], rest[n_send + 3:]
        copies = _scatter_copies(send_refs, got_refs, sems)
        for cp in copies:
            cp.start()
        u_s[0:CONV_HALO, :] = jnp.zeros((CONV_HALO, cw), F32)
        dy_s[pl.ds(s, CONV_HALO), :] = jnp.zeros((CONV_HALO, cw), F32)
        dw_s[...] = jnp.zeros_like(dw_s)

        def glu(i, _):
            r0 = pl.multiple_of(i * rows, rows)
            u_s[pl.ds(CONV_HALO + r0, rows), :] = a_ref[pl.ds(r0, rows), :] * _sigmoid(g_ref[pl.ds(r0, rows), :])
            return 0

        lax.fori_loop(0, n_chunk, glu, 0)

        def chunk(i, carry):
            db, dlg, dlb = carry
            r0 = pl.multiple_of(i * rows, rows)
            y = y_ref[pl.ds(r0, rows), :]
            yc = y - jnp.mean(y, axis=-1, keepdims=True)
            r = lax.rsqrt(jnp.mean(yc * yc, axis=-1, keepdims=True) + EPS)
            n = yc * r
            ln = n * lg_ref[...] + lb_ref[...]
            sg = _sigmoid(ln)
            dln = dc_ref[pl.ds(r0, rows), :] * (sg * (1.0 + ln * (1.0 - sg)))
            dn = dln * lg_ref[...]
            dy = r * (dn - jnp.mean(dn, axis=-1, keepdims=True) - n * jnp.mean(dn * n, axis=-1, keepdims=True))
            dy_s[pl.ds(r0, rows), :] = dy
            for lanes in _lane_blocks(cw):
                shifts = _row_shifts(u_s[pl.ds(r0, CONV_HALO + rows), lanes], False)
                dy_part = dy[:, lanes]
                for k in range(CONV_KERNEL):
                    prod = _earlier(shifts, CONV_KERNEL - 1 - k, rows) * dy_part
                    dw_s[k, :, lanes] += jnp.sum(prod.reshape(rows // SUBLANES, SUBLANES, LANES), axis=0)
            return (db + jnp.sum(dy, axis=0, keepdims=True),
                    dlg + jnp.sum(dln * n, axis=0, keepdims=True),
                    dlb + jnp.sum(dln, axis=0, keepdims=True))

        zero = jnp.zeros((1, cw), F32)
        db, dlg, dlb = lax.fori_loop(0, n_chunk, chunk, (zero, zero, zero))
        db_ref[...] = db
        dlg_ref[...] = dlg
        dlb_ref[...] = dlb
        for k in range(CONV_KERNEL):
            dw_ref[k:k + 1, :] = jnp.sum(dw_s[k], axis=0, keepdims=True)

        def back(i, _):
            r0 = pl.multiple_of(i * rows, rows)
            for lanes in _lane_blocks(cw):
                shifts = _row_shifts(dy_s[pl.ds(r0, rows + CONV_HALO), lanes], True)
                du = None
                for k in range(CONV_KERNEL):
                    term = _later(shifts, CONV_KERNEL - 1 - k, rows) * w_ref[k:k + 1, lanes]
                    du = term if du is None else du + term
                av = a_ref[pl.ds(r0, rows), lanes]
                sg = _sigmoid(g_ref[pl.ds(r0, rows), lanes])
                o_ref[pl.ds(r0, rows), lanes] = (du * sg).astype(BF16)
                o_ref[pl.ds(r0, rows), slice(cw + lanes.start, cw + lanes.stop)] = (du * av * sg * (1.0 - sg)).astype(BF16)
            return 0

        lax.fori_loop(0, n_chunk, back, 0)
        for cp in copies:
            cp.wait()

    vec = pl.BlockSpec((1, cw), lambda i: (0, 0))
    wspec = pl.BlockSpec((CONV_KERNEL, cw), lambda i: (0, 0))
    out = pl.pallas_call(
        body, name=name, grid=(1,),
        in_specs=[pl.BlockSpec((s, cw), lambda i: (0, blk_a)), pl.BlockSpec((s, cw), lambda i: (0, blk_a + 1)),
                  pl.BlockSpec((s, cw), lambda i: (0, 0)), pl.BlockSpec((s, cw), lambda i: (0, 1)), wspec, vec, vec]
        + [ANY_SPEC] * n_send,
        out_specs=[pl.BlockSpec((s, 2 * cw), lambda i: (0, 0)), wspec, vec, vec, vec] + [ANY_SPEC] * n_send,
        out_shape=[jax.ShapeDtypeStruct((s, 2 * cw), BF16), jax.ShapeDtypeStruct((CONV_KERNEL, cw), F32)]
        + [jax.ShapeDtypeStruct((1, cw), F32)] * 3 + _scattered_shapes(sends),
        scratch_shapes=[pltpu.VMEM((CONV_HALO + s, cw), F32), pltpu.VMEM((s + CONV_HALO, cw), F32),
                        pltpu.VMEM((CONV_KERNEL, 8, cw), F32)] + _exchange_sems(n_send),
        compiler_params=_params(1),
    )(proj, proj, y, dcat, w, lg, lb, *sends)
    return out[:5], out[5:]


FFN_ROWS = 256


def _ffn_gate(g_ref, r0, rows, w_ref, b_ref):
    cur = g_ref[pl.ds(r0, rows), :].astype(F32)
    prev = g_ref[pl.ds(pl.multiple_of(jnp.maximum(r0 - FFN_HALO, 0), FFN_HALO), FFN_HALO), :].astype(F32)
    prev = jnp.where(r0 > 0, prev, 0.0)
    window = jnp.concatenate([prev, cur], axis=0)
    gc = cur * w_ref[FFN_KERNEL - 1:FFN_KERNEL, :] + b_ref[...]
    for k in range(FFN_KERNEL - 1):
        gc = gc + _shifted(window, FFN_KERNEL - 1 - k, FFN_HALO) * w_ref[k:k + 1, :]
    return gc, window


def _ffn_fwd(up, w, b, name):
    s = up.shape[0]
    f = w.shape[1]
    tc = _pick(f, (256, 128))
    nc = f // tc
    rows = _pick(s, (FFN_ROWS, 128))

    def body(g_ref, v_ref, w_ref, b_ref, o_ref):
        def chunk(i, _):
            r0 = pl.multiple_of(i * rows, rows)
            gc, _w = _ffn_gate(g_ref, r0, rows, w_ref, b_ref)
            o_ref[pl.ds(r0, rows), :] = (gc * _sigmoid(gc) * v_ref[pl.ds(r0, rows), :].astype(F32)).astype(BF16)
            return 0

        lax.fori_loop(0, s // rows, chunk, 0)

    return pl.pallas_call(
        body, name=name, grid=(nc,),
        in_specs=[pl.BlockSpec((s, tc), lambda j: (0, j)), pl.BlockSpec((s, tc), lambda j: (0, nc + j)),
                  pl.BlockSpec((FFN_KERNEL, tc), lambda j: (0, j)), pl.BlockSpec((1, tc), lambda j: (0, j))],
        out_specs=pl.BlockSpec((s, tc), lambda j: (0, j)),
        out_shape=jax.ShapeDtypeStruct((s, f), BF16),
        compiler_params=_params(1),
    )(up, up, w, b)


def _ffn_bwd(up, dact, w, b, name):
    s = up.shape[0]
    f = w.shape[1]
    tc = _pick(f, (256, 128))
    nc = f // tc
    rows = _pick(s, (FFN_ROWS, 128))
    n_chunk = s // rows

    def body(g_ref, v_ref, da_ref, w_ref, b_ref, d_ref, dw_ref, db_ref, dgc_s):
        dg_ref, dv_ref = d_ref.at[0], d_ref.at[1]
        dgc_s[pl.ds(s, FFN_HALO), :] = jnp.zeros((FFN_HALO, tc), F32)

        def chunk(i, carry):
            r0 = pl.multiple_of(i * rows, rows)
            gc, window = _ffn_gate(g_ref, r0, rows, w_ref, b_ref)
            sg = _sigmoid(gc)
            da = da_ref[pl.ds(r0, rows), :].astype(F32)
            dv_ref[pl.ds(r0, rows), :] = (da * gc * sg).astype(BF16)
            dgc = da * v_ref[pl.ds(r0, rows), :].astype(F32) * (sg * (1.0 + gc * (1.0 - sg)))
            dgc_s[pl.ds(r0, rows), :] = dgc
            out = [carry[0] + jnp.sum(dgc, axis=0, keepdims=True)]
            for k in range(FFN_KERNEL):
                out.append(carry[1 + k] + jnp.sum(_shifted(window, FFN_KERNEL - 1 - k, FFN_HALO) * dgc,
                                                  axis=0, keepdims=True))
            return tuple(out)

        zero = jnp.zeros((1, tc), F32)
        sums = lax.fori_loop(0, n_chunk, chunk, (zero,) * (1 + FFN_KERNEL))
        db_ref[...] = sums[0]
        for k in range(FFN_KERNEL):
            dw_ref[k:k + 1, :] = sums[1 + k]

        def back(i, _):
            r0 = pl.multiple_of(i * rows, rows)
            window = dgc_s[pl.ds(r0, rows + FFN_HALO), :]
            dg = window[:rows, :] * w_ref[FFN_KERNEL - 1:FFN_KERNEL, :]
            for k in range(FFN_KERNEL - 1):
                sh = FFN_KERNEL - 1 - k
                dg = dg + pltpu.roll(window, rows + FFN_HALO - sh, 0)[:rows, :] * w_ref[k:k + 1, :]
            dg_ref[pl.ds(r0, rows), :] = dg.astype(BF16)
            return 0

        lax.fori_loop(0, n_chunk, back, 0)

    blk = lambda off: pl.BlockSpec((s, tc), lambda j: (0, off + j))
    return pl.pallas_call(
        body, name=name, grid=(nc,),
        in_specs=[blk(0), blk(nc), blk(0), pl.BlockSpec((FFN_KERNEL, tc), lambda j: (0, j)),
                  pl.BlockSpec((1, tc), lambda j: (0, j))],
        out_specs=[pl.BlockSpec((2, s, tc), lambda j: (0, 0, j)), pl.BlockSpec((FFN_KERNEL, tc), lambda j: (0, j)),
                   pl.BlockSpec((1, tc), lambda j: (0, j))],
        out_shape=[jax.ShapeDtypeStruct((2, s, f), BF16),
                   jax.ShapeDtypeStruct((FFN_KERNEL, f), F32), jax.ShapeDtypeStruct((1, f), F32)],
        scratch_shapes=[pltpu.VMEM((s + FFN_HALO, tc), F32)],
        compiler_params=_params(1),
    )(up, up, dact, w, b)


def _loss_head(y, target, name):
    m, n = y.shape
    tm = _pick(m, (256, 128))

    def body(y_ref, t_ref, l_ref, d_ref, db_ref):
        e = y_ref[...] - t_ref[...]
        part = 0.5 * jnp.sum(jnp.sum(e * e, axis=-1, keepdims=True) / n, axis=0, keepdims=True)

        @pl.when(pl.program_id(0) == 0)
        def _():
            l_ref[...] = jnp.zeros_like(l_ref)

        l_ref[...] += part
        d = e / n
        d_ref[...] = d
        db_ref[...] = d.astype(BF16)

    return pl.pallas_call(
        body, name=name, grid=(m // tm,),
        in_specs=[pl.BlockSpec((tm, n), lambda i: (i, 0))] * 2,
        out_specs=[pl.BlockSpec((8, LANES), lambda i: (0, 0)), pl.BlockSpec((tm, n), lambda i: (i, 0)),
                   pl.BlockSpec((tm, n), lambda i: (i, 0))],
        out_shape=[jax.ShapeDtypeStruct((8, LANES), F32), jax.ShapeDtypeStruct((m, n), F32),
                   jax.ShapeDtypeStruct((m, n), BF16)],
        compiler_params=_params(1),
    )(y, target)


def _adamw_math(w, g, m, v):
    m = ADAM_B1 * m + (1.0 - ADAM_B1) * g
    v = ADAM_B2 * v + (1.0 - ADAM_B2) * (g * g)
    m_hat = m / (1.0 - ADAM_B1 ** ADAM_STEP)
    v_hat = v / (1.0 - ADAM_B2 ** ADAM_STEP)
    delta = -ADAM_LR * (m_hat / (jnp.sqrt(v_hat) + ADAM_EPS) + ADAM_WD * w)
    return delta, m, v


def _sum_adamw(parts, w, m, v, sends, name):
    depth, r, c = w.shape
    parts = [list(p) if isinstance(p, (list, tuple)) else [p] for p in parts]
    tr = max(t for t in range(16, min(min(a.shape[1] for p in parts for a in p), 192) + 1, 16)
             if all(a.shape[1] % t == 0 for p in parts for a in p))
    steps = r // tr
    n_send = len(sends)
    chunks = []
    for layer, p in enumerate(parts):
        s0 = 0
        for a in p:
            chunks.append((layer, s0, a.shape[1] // tr))
            s0 += a.shape[1] // tr
    n_chunk = len(chunks)

    def body(*refs):
        p_refs, refs = refs[:n_chunk], refs[n_chunk:]
        (w_ref, m_ref, v_ref), send_refs, refs = refs[:3], refs[3:3 + n_send], refs[3 + n_send:]
        (g_out, d_out, m_out, v_out), got_refs, sems = refs[:4], refs[4:4 + n_send], refs[4 + n_send:]
        is_first = (pl.program_id(0) == 0) & (pl.program_id(1) == 0)
        is_last = (pl.program_id(0) == depth - 1) & (pl.program_id(1) == steps - 1)
        if n_send:
            start, relay, finish = _two_level_gather(send_refs, got_refs, sems)
            pl.when(is_first)(start)
            pl.when(is_last)(relay)
        for p_ref, (layer, s0, n) in zip(p_refs, chunks):
            step = pl.program_id(1)

            @pl.when((pl.program_id(0) == layer) & (step >= s0) & (step < s0 + n))
            def _():
                g = p_ref[0].astype(F32)
                for src in range(1, N_DEV):
                    g = g + p_ref[src].astype(F32)
                d, mn, vn = _adamw_math(w_ref[0], g, m_ref[0], v_ref[0])
                g_out[0] = g
                d_out[0] = d
                m_out[0] = mn
                v_out[0] = vn

        if n_send:
            pl.when(is_last)(finish)

    def part_spec(layer, s0, n):
        return pl.BlockSpec((N_DEV, tr, c), lambda l, i: (0, jnp.clip((l - layer) * steps + i - s0, 0, n - 1), 0))

    blk = pl.BlockSpec((1, tr, c), lambda l, i: (l, i, 0))
    out = pl.pallas_call(
        body, name=name, grid=(depth, steps),
        in_specs=[part_spec(*ch) for ch in chunks] + [blk, blk, blk] + [ANY_SPEC] * n_send,
        out_specs=[blk] * 4 + [ANY_SPEC] * n_send,
        out_shape=[jax.ShapeDtypeStruct(w.shape, F32)] * 4 + _gathered_shapes(sends),
        scratch_shapes=_exchange_sems(n_send) if n_send else [],
        compiler_params=_params(2),
    )(*[a for p in parts for a in p], w, m, v, *sends)
    return out[:4], out[4:]


VMEM_SPEC = pl.BlockSpec(memory_space=pltpu.VMEM)


def _sum_small(parts, name):
    n = len(parts)

    def body(*refs):
        for p_ref, o_ref in zip(refs[:n], refs[n:]):
            g = p_ref[0]
            for src in range(1, N_DEV):
                g = g + p_ref[src]
            o_ref[...] = g

    return pl.pallas_call(
        body, name=name, in_specs=[VMEM_SPEC] * n, out_specs=[VMEM_SPEC] * n,
        out_shape=[jax.ShapeDtypeStruct(p.shape[1:], F32) for p in parts],
        compiler_params=_params(),
    )(*parts)


def _adamw_small(ws, gs, ms, vs, name):
    n = len(ws)

    def body(*refs):
        ins, outs = refs[:4 * n], refs[4 * n:]
        for i in range(n):
            d, mn, vn = _adamw_math(ins[i][...], ins[n + i][...], ins[2 * n + i][...], ins[3 * n + i][...])
            outs[i][...] = d
            outs[n + i][...] = mn
            outs[2 * n + i][...] = vn

    out = pl.pallas_call(
        body, name=name, in_specs=[VMEM_SPEC] * (4 * n), out_specs=[VMEM_SPEC] * (3 * n),
        out_shape=[jax.ShapeDtypeStruct(w.shape, F32) for w in ws] * 3,
        compiler_params=_params(),
    )(*ws, *gs, *ms, *vs)
    return out[:n], out[n:2 * n], out[2 * n:]


def kernel(x, norm1_g, w_in, q_norm_g, k_norm_g, conv_dw_w, conv_dw_b, conv_ln_g, conv_ln_b, w_out, norm2_g, w_up, ffn_dw_w, ffn_dw_b, w_down, loss_target, m_norm1_g, m_w_in, m_q_norm_g, m_k_norm_g, m_conv_dw_w, m_conv_dw_b, m_conv_ln_g, m_conv_ln_b, m_w_out, m_norm2_g, m_w_up, m_ffn_dw_w, m_ffn_dw_b, m_w_down, v_norm1_g, v_w_in, v_q_norm_g, v_k_norm_g, v_conv_dw_w, v_conv_dw_b, v_conv_ln_g, v_conv_ln_b, v_w_out, v_norm2_g, v_w_up, v_ffn_dw_w, v_ffn_dw_b, v_w_down):
    depth = w_in.shape[0]
    d_ff = w_down.shape[1] * N_DEV
    conv_w = conv_dw_b.shape[1]
    cw_shard = conv_dw_w.shape[2]
    fw_shard = ffn_dw_w.shape[2]
    me = 4 * lax.axis_index("x") + 2 * lax.axis_index("y") + lax.axis_index("c")

    transposed = lambda a: a.transpose(0, 2, 1)
    b_in, b_out, b_up, b_down = (transposed(w_in).astype(BF16), w_out.astype(BF16), transposed(w_up).astype(BF16),
                                 w_down.astype(BF16))
    rows_major = lambda g: g.reshape(N_DEV * g.shape[1], g.shape[2])
    g_in0, g_cw, g_fw = _gather([b_in[0], conv_dw_w, ffn_dw_w], name="gather_first")
    wf_in, wf_out, wf_up, wf_down = [rows_major(g_in0)] + [None] * (depth - 1), [None] * depth, [None] * depth, [None] * depth
    cwf = g_cw.transpose(1, 2, 0, 3).reshape(depth, CONV_KERNEL, conv_w)
    fwf = g_fw.transpose(1, 2, 0, 3).reshape(depth, FFN_KERNEL, d_ff)

    row = lambda a, l: a[l].reshape(1, -1)
    both_heads = lambda a, l: jnp.tile(row(a, l), (1, HEADS_PER_STEP))

    xs = x[0]
    saved = []
    for l in range(depth):
        proj, h1, _ = _mm_rms(xs, row(norm1_g, l), wf_in[l], F32, [], name="fwd_in")
        sends = [b_out[l], b_up[l]] + ([b_in[l + 1]] if l + 1 < depth else [])
        attn, tsum, got = _attn_fwd(proj, both_heads(q_norm_g, l), both_heads(k_norm_g, l), sends, name="fwd_attn")
        wf_out[l], wf_up[l] = rows_major(got[0]), rows_major(got[1])
        if l + 1 < depth:
            wf_in[l + 1] = rows_major(got[2])
        conv, conv_y = _conv_fwd(proj, cwf[l], row(conv_dw_b, l), row(conv_ln_g, l), row(conv_ln_b, l),
                                 name="fwd_conv")
        cat = jnp.concatenate([attn, conv], axis=1)
        x_mid = _mm_res(cat, wf_out[l], xs, name="fwd_out")
        up, h2, got = _mm_rms(x_mid, row(norm2_g, l), wf_up[l], BF16, [b_down[l]], name="fwd_up")
        wf_down[l] = rows_major(got[0])
        act = _ffn_fwd(up, fwf[l], row(ffn_dw_b, l), name="fwd_ffn")
        x_next = _mm_res(act, wf_down[l], x_mid, name="fwd_down")
        saved.append((xs, h1, proj, tsum, cat, x_mid, h2, up, act, conv_y))
        xs = x_next

    loss_tile, dx, dxb = _loss_head(xs, loss_target[0], name="loss_head")
    loss = lax.psum(loss_tile[0, 0], ("x", "y", "c"))

    r_in, r_out, r_up, r_down = [None] * depth, [None] * depth, [None] * depth, [None] * depth
    row_blocks = lambda g: g.reshape(N_DEV, g.shape[0] // N_DEV, g.shape[1])
    small = {k: [None] * depth for k in ("norm1_g", "q_norm_g", "k_norm_g", "conv_dw_w", "conv_dw_b", "conv_ln_g",
                                         "conv_ln_b", "norm2_g", "ffn_dw_w", "ffn_dw_b")}
    gw_in = None
    for l in reversed(range(depth)):
        xs, h1, proj, tsum, cat, x_mid, h2, up, act, conv_y = saved[l]
        dact = _mm_nt(dxb, wf_down[l], BF16, name="bwd_dact")
        gw_down = _mm_tn(act, dxb, name="bwd_gw_down")
        dup, small["ffn_dw_w"][l], small["ffn_dw_b"][l] = _ffn_bwd(up, dact, fwf[l], row(ffn_dw_b, l), name="bwd_ffn")
        gw_up = _mm_tn(dup, h2, name="bwd_gw_up")
        if l + 1 < depth:
            blocks = row_blocks(gw_in)
            half = blocks.shape[1] // 2
            in_halves = [blocks[:, :half], blocks[:, half:]]
        dx, dxb, small["norm2_g"][l], got_a = _mm_rmsbwd(dup, wf_up[l], x_mid, row(norm2_g, l), dx,
                                                         in_halves[:1] if l + 1 < depth else [], name="bwd_up")
        dcat = _mm_nt(dxb, wf_out[l], F32, name="bwd_dcat")
        gw_out = _mm_tn(cat, dxb, name="bwd_gw_out")
        early = in_halves[1] if l + 1 < depth else row_blocks(gw_out)
        (dglu, small["conv_dw_w"][l], small["conv_dw_b"][l], small["conv_ln_g"][l], small["conv_ln_b"][l]), got = (
            _conv_bwd(proj, conv_y, dcat, cwf[l], row(conv_ln_g, l), row(conv_ln_b, l), [early], name="bwd_conv"))
        sends = [row_blocks(gw_down), row_blocks(gw_up)] + ([row_blocks(gw_out)] if l + 1 < depth else [])
        (dq, dk, dv, dqg, dkg), got2 = _attn_bwd(proj, dcat, tsum, both_heads(q_norm_g, l), both_heads(k_norm_g, l),
                                                 sends, name="bwd_attn")
        r_down[l], r_up[l] = got2[:2]
        if l + 1 < depth:
            r_in[l + 1], r_out[l] = [got_a[0], got[0]], got2[2]
        else:
            r_out[l] = got[0]
        small["q_norm_g"][l] = dqg[0:1, :HEAD_DIM] + dqg[0:1, HEAD_DIM:]
        small["k_norm_g"][l] = dkg[0:1, :HEAD_DIM] + dkg[0:1, HEAD_DIM:]
        dproj = jnp.concatenate([dq, dk, dv, dglu], axis=1)
        gw_in = _mm_tn(dproj, h1, name="bwd_gw_in")
        sends = [row_blocks(gw_in)] if l == 0 else []
        dx, dxb, small["norm1_g"][l], got = _mm_rmsbwd(dproj, wf_in[l], xs, row(norm1_g, l), dx, sends, name="bwd_in")
        if l == 0:
            r_in[0] = got[0]
    grad_x = dx[None]

    names = ["norm1_g", "q_norm_g", "k_norm_g", "conv_dw_w", "conv_dw_b", "conv_ln_g", "conv_ln_b", "norm2_g",
             "ffn_dw_w", "ffn_dw_b"]
    full_shapes = {"norm1_g": norm1_g.shape, "q_norm_g": q_norm_g.shape, "k_norm_g": k_norm_g.shape,
                   "conv_dw_w": (depth, CONV_KERNEL, conv_w), "conv_dw_b": conv_dw_b.shape,
                   "conv_ln_g": conv_ln_g.shape, "conv_ln_b": conv_ln_b.shape, "norm2_g": norm2_g.shape,
                   "ffn_dw_w": (depth, FFN_KERNEL, d_ff), "ffn_dw_b": ffn_dw_b.shape}
    partial = [jnp.stack(small[k]).reshape(full_shapes[k]) for k in names]
    big = {}
    big["w_out"], all_partials = _sum_adamw(r_out, w_out, m_w_out, v_w_out, partial, name="adamw_out")
    big["w_up"], _ = _sum_adamw(r_up, transposed(w_up), transposed(m_w_up), transposed(v_w_up), [], name="adamw_up")
    big["w_down"], _ = _sum_adamw(r_down, w_down, m_w_down, v_w_down, [], name="adamw_down")
    big["w_in"], _ = _sum_adamw(r_in, transposed(w_in), transposed(m_w_in), transposed(v_w_in), [], name="adamw_in")
    for k in ("w_in", "w_up"):
        big[k] = [transposed(a) for a in big[k]]

    grads = dict(zip(names, _sum_small(all_partials, name="sum_small_grads")))
    grads["conv_dw_w"] = lax.dynamic_slice_in_dim(grads["conv_dw_w"], me * cw_shard, cw_shard, axis=2)
    grads["ffn_dw_w"] = lax.dynamic_slice_in_dim(grads["ffn_dw_w"], me * fw_shard, fw_shard, axis=2)
    weights = dict(norm1_g=norm1_g, q_norm_g=q_norm_g, k_norm_g=k_norm_g, conv_dw_w=conv_dw_w, conv_dw_b=conv_dw_b,
                   conv_ln_g=conv_ln_g, conv_ln_b=conv_ln_b, norm2_g=norm2_g, ffn_dw_w=ffn_dw_w, ffn_dw_b=ffn_dw_b)
    m_in = dict(norm1_g=m_norm1_g, q_norm_g=m_q_norm_g, k_norm_g=m_k_norm_g, conv_dw_w=m_conv_dw_w,
                conv_dw_b=m_conv_dw_b, conv_ln_g=m_conv_ln_g, conv_ln_b=m_conv_ln_b, norm2_g=m_norm2_g,
                ffn_dw_w=m_ffn_dw_w, ffn_dw_b=m_ffn_dw_b)
    v_in = dict(norm1_g=v_norm1_g, q_norm_g=v_q_norm_g, k_norm_g=v_k_norm_g, conv_dw_w=v_conv_dw_w,
                conv_dw_b=v_conv_dw_b, conv_ln_g=v_conv_ln_g, conv_ln_b=v_conv_ln_b, norm2_g=v_norm2_g,
                ffn_dw_w=v_ffn_dw_w, ffn_dw_b=v_ffn_dw_b)
    d_s, m_s, v_s = _adamw_small([weights[k] for k in names], [grads[k] for k in names], [m_in[k] for k in names],
                                 [v_in[k] for k in names], name="adamw_small")
    delta, new_m, new_v = dict(zip(names, d_s)), dict(zip(names, m_s)), dict(zip(names, v_s))
    for k, (g, d, mn, vn) in big.items():
        grads[k], delta[k], new_m[k], new_v[k] = g, d, mn, vn

    order = ["norm1_g", "w_in", "q_norm_g", "k_norm_g", "conv_dw_w", "conv_dw_b", "conv_ln_g", "conv_ln_b", "w_out",
             "norm2_g", "w_up", "ffn_dw_w", "ffn_dw_b", "w_down"]
    return (loss, grad_x, *[grads[k] for k in order], *[delta[k] for k in order], *[new_m[k] for k in order],
            *[new_v[k] for k in order])
```

```python
import jax
import jax.numpy as jnp
from jax import lax
from jax.experimental import pallas as pl
from jax.experimental.pallas import tpu as pltpu

F32 = jnp.float32
BF16 = jnp.bfloat16

N_DEV = 8
HEADS = 8
HEAD_DIM = 64
ATTN_WIDTH = HEADS * HEAD_DIM
CONV_KERNEL = 31
FFN_KERNEL = 3
EPS = 1e-6
BLK = 128
KEY_GROUP = 4
LANES = 128
NORM_ROWS = 128
CONV_HALO = 32
FFN_HALO = 16

ADAM_LR = 0.001
ADAM_B1 = 0.9
ADAM_B2 = 0.999
ADAM_EPS = 1e-08
ADAM_WD = 0.01
ADAM_STEP = 10

VMEM_LIMIT = 56 * 1024 * 1024


def _params(n_axes=0):
    kw = dict(vmem_limit_bytes=VMEM_LIMIT)
    if n_axes:
        kw["dimension_semantics"] = ("arbitrary",) * n_axes
    return pltpu.CompilerParams(**kw)


def _dot(a, b):
    return jnp.dot(a, b, preferred_element_type=F32)


def _dot_nt(a, b):
    return lax.dot_general(a, b, (((1,), (1,)), ((), ())), preferred_element_type=F32)


def _dot_tn(a, b):
    return lax.dot_general(a, b, (((0,), (0,)), ((), ())), preferred_element_type=F32)


def _sigmoid(x):
    return 1.0 / (1.0 + jnp.exp(-x))


def _split_bf16(x):
    hi = x.astype(BF16)
    lo = (x - hi.astype(F32)).astype(BF16)
    return hi, lo


def _pick(n, options):
    for t in options:
        if n % t == 0:
            return t
    return n


def _tile(n, cap):
    best = None
    for t in range(LANES, min(n, cap) + 1, LANES):
        if n % t == 0:
            best = t
    return best or n


def _mm_rms(x, g, wt, out_dtype, sends, name):
    m, k = x.shape
    n = wt.shape[0]
    tm = _tile(m, 2048)
    tn = _tile(n, 512)
    n_send = len(sends)
    grid = (m // tm, n // tn)

    def body(x_ref, g_ref, w_ref, *rest):
        send_refs, (o_ref, h_ref), rest = rest[:n_send], rest[n_send:n_send + 2], rest[n_send + 2:]
        got_refs, h_s, sems = rest[:n_send], rest[n_send], rest[n_send + 1:]
        if n_send:
            start, relay, finish = _two_level_gather(send_refs, got_refs, sems)
            is_first = (pl.program_id(0) == 0) & (pl.program_id(1) == 0)
            is_last = (pl.program_id(0) == grid[0] - 1) & (pl.program_id(1) == grid[1] - 1)
            pl.when(is_first)(start)
            pl.when(is_last)(relay)

        @pl.when(pl.program_id(1) == 0)
        def _():
            def chunk(c, _):
                rows = pl.ds(pl.multiple_of(c * NORM_ROWS, NORM_ROWS), NORM_ROWS)
                xv = x_ref[rows, :]
                r = lax.rsqrt(jnp.mean(xv * xv, axis=-1, keepdims=True) + EPS)
                hv = (xv * r * g_ref[...]).astype(BF16)
                h_s[rows, :] = hv
                h_ref[rows, :] = hv
                return 0

            lax.fori_loop(0, tm // NORM_ROWS, chunk, 0)

        o_ref[...] = _dot_nt(h_s[...], w_ref[...]).astype(out_dtype)
        if n_send:
            pl.when(is_last)(finish)

    out = pl.pallas_call(
        body, name=name, grid=grid,
        in_specs=[pl.BlockSpec((tm, k), lambda i, j: (i, 0)),
                  pl.BlockSpec((1, k), lambda i, j: (0, 0)),
                  pl.BlockSpec((tn, k), lambda i, j: (j, 0))] + [ANY_SPEC] * n_send,
        out_specs=[pl.BlockSpec((tm, tn), lambda i, j: (i, j)),
                   pl.BlockSpec((tm, k), lambda i, j: (i, 0))] + [ANY_SPEC] * n_send,
        out_shape=[jax.ShapeDtypeStruct((m, n), out_dtype), jax.ShapeDtypeStruct((m, k), BF16)]
        + _gathered_shapes(sends),
        scratch_shapes=[pltpu.VMEM((tm, k), BF16)] + (_exchange_sems(n_send) if n_send else []),
        compiler_params=_params(2),
    )(x, g, wt, *sends)
    return out[0], out[1], out[2:]


def _mm_res(a, w, res, name):
    m, k = a.shape
    n = w.shape[1]
    tm = _tile(m, 1024)
    tn = _tile(n, 512)

    def body(a_ref, w_ref, r_ref, o_ref):
        o_ref[...] = r_ref[...] + _dot(a_ref[...], w_ref[...])

    return pl.pallas_call(
        body, name=name, grid=(m // tm, n // tn),
        in_specs=[pl.BlockSpec((tm, k), lambda i, j: (i, 0)),
                  pl.BlockSpec((k, tn), lambda i, j: (0, j)),
                  pl.BlockSpec((tm, tn), lambda i, j: (i, j))],
        out_specs=pl.BlockSpec((tm, tn), lambda i, j: (i, j)),
        out_shape=jax.ShapeDtypeStruct((m, n), F32),
        compiler_params=_params(2),
    )(a, w, res)


def _mm_nt(a, w, out_dtype, name):
    m, k = a.shape
    n = w.shape[0]
    tm = _tile(m, 1024)
    tn = _tile(n, 1408)

    def body(a_ref, w_ref, o_ref):
        o_ref[...] = _dot_nt(a_ref[...], w_ref[...]).astype(out_dtype)

    return pl.pallas_call(
        body, name=name, grid=(m // tm, n // tn),
        in_specs=[pl.BlockSpec((tm, k), lambda i, j: (i, 0)),
                  pl.BlockSpec((tn, k), lambda i, j: (j, 0))],
        out_specs=pl.BlockSpec((tm, tn), lambda i, j: (i, j)),
        out_shape=jax.ShapeDtypeStruct((m, n), out_dtype),
        compiler_params=_params(2),
    )(a, w)


def _column_tiles(a, cap):
    if a.ndim == 2:
        s, c = a.shape
        tc = _tile(c, cap)
        return s, c, tc, lambda rows, index: pl.BlockSpec((rows, tc), lambda *g: (index(*g)[0], index(*g)[1]))
    slabs, s, width = a.shape
    tc = _tile(width, cap)
    per = width // tc
    return s, slabs * width, tc, lambda rows, index: pl.BlockSpec(
        (None, rows, tc), lambda *g: (index(*g)[1] // per, index(*g)[0], index(*g)[1] % per))


def _mm_tn(a, b, name):
    s, m, tm, a_spec = _column_tiles(a, 1408)
    n = b.shape[1]
    tn = _tile(n, 1024)

    def body(a_ref, b_ref, o_ref):
        o_ref[...] = _dot_tn(a_ref[...], b_ref[...]).astype(BF16)

    return pl.pallas_call(
        body, name=name, grid=(m // tm, n // tn),
        in_specs=[a_spec(s, lambda i, j: (0, i)),
                  pl.BlockSpec((s, tn), lambda i, j: (0, j))],
        out_specs=pl.BlockSpec((tm, tn), lambda i, j: (i, j)),
        out_shape=jax.ShapeDtypeStruct((m, n), BF16),
        compiler_params=_params(2),
    )(a, b)


def _mm_rmsbwd(a, w, x, g, dres, sends, name):
    m, k, tk, a_spec = _column_tiles(a, 1408)
    n = w.shape[1]
    tm = _tile(m, 1024)
    nk = k // tk
    n_send = len(sends)

    def body(a_ref, w_ref, x_ref, g_ref, r_ref, *rest):
        send_refs, (dx_ref, dxb_ref, dg_ref), rest = rest[:n_send], rest[n_send:n_send + 3], rest[n_send + 3:]
        got_refs, acc, sems = rest[:n_send], rest[n_send], rest[n_send + 1:]
        i, kk = pl.program_id(0), pl.program_id(1)
        if n_send:
            copies = _scatter_copies(send_refs, got_refs, sems)

            @pl.when((i == 0) & (kk == 0))
            def _():
                for cp in copies:
                    cp.start()

        part = _dot(a_ref[...], w_ref[...])

        @pl.when(kk == 0)
        def _():
            acc[...] = part

        @pl.when(kk > 0)
        def _():
            acc[...] += part

        @pl.when(kk == nk - 1)
        def _():
            def chunk(c, dgp):
                rows = pl.ds(pl.multiple_of(c * NORM_ROWS, NORM_ROWS), NORM_ROWS)
                dh = acc[rows, :]
                xv = x_ref[rows, :]
                r = lax.rsqrt(jnp.mean(xv * xv, axis=-1, keepdims=True) + EPS)
                xh = xv * r
                dxh = dh * g_ref[...]
                dx = r_ref[rows, :] + r * (dxh - xh * jnp.mean(dxh * xh, axis=-1, keepdims=True))
                dx_ref[rows, :] = dx
                dxb_ref[rows, :] = dx.astype(BF16)
                return dgp + jnp.sum(dh * xh, axis=0, keepdims=True)

            dgp = lax.fori_loop(0, tm // NORM_ROWS, chunk, jnp.zeros((1, n), F32))

            @pl.when(i == 0)
            def _():
                dg_ref[...] = dgp

            @pl.when(i > 0)
            def _():
                dg_ref[...] += dgp

        if n_send:
            @pl.when((i == m // tm - 1) & (kk == nk - 1))
            def _():
                for cp in copies:
                    cp.wait()

    out = pl.pallas_call(
        body, name=name, grid=(m // tm, nk),
        in_specs=[a_spec(tm, lambda i, kk: (i, kk)),
                  pl.BlockSpec((tk, n), lambda i, kk: (kk, 0)),
                  pl.BlockSpec((tm, n), lambda i, kk: (i, 0)),
                  pl.BlockSpec((1, n), lambda i, kk: (0, 0)),
                  pl.BlockSpec((tm, n), lambda i, kk: (i, 0))] + [ANY_SPEC] * n_send,
        out_specs=[pl.BlockSpec((tm, n), lambda i, kk: (i, 0)),
                   pl.BlockSpec((tm, n), lambda i, kk: (i, 0)),
                   pl.BlockSpec((1, n), lambda i, kk: (0, 0))] + [ANY_SPEC] * n_send,
        out_shape=[jax.ShapeDtypeStruct((m, n), F32), jax.ShapeDtypeStruct((m, n), BF16),
                   jax.ShapeDtypeStruct((1, n), F32)] + _scattered_shapes(sends),
        scratch_shapes=[pltpu.VMEM((tm, n), F32)] + (_exchange_sems(n_send) if n_send else []),
        compiler_params=_params(2),
    )(a, w, x, g, dres, *sends)
    return out[0], out[1], out[2], out[3:]


ANY_SPEC = pl.BlockSpec(memory_space=pl.ANY)
SEMS_PER_OPERAND = N_DEV - 1


def _exchange_sems(n):
    return [pltpu.SemaphoreType.DMA((n, SEMS_PER_OPERAND)), pltpu.SemaphoreType.DMA((n, SEMS_PER_OPERAND)),
            pltpu.SemaphoreType.DMA((n,))]


def _gathered_shapes(parts):
    return [jax.ShapeDtypeStruct((N_DEV,) + a.shape, a.dtype) for a in parts]


def _scattered_shapes(parts):
    return [jax.ShapeDtypeStruct(a.shape, a.dtype) for a in parts]


def _flat(pos):
    return 4 * pos[0] + 2 * pos[1] + pos[2]


def _remote(src, dst, sems, i, k, to):
    send_sems, recv_sems, _ = sems
    return pltpu.make_async_remote_copy(src_ref=src, dst_ref=dst, send_sem=send_sems.at[i, k],
                                        recv_sem=recv_sems.at[i, k], device_id=to,
                                        device_id_type=pl.DeviceIdType.MESH)


def _scatter_copies(ins, outs, sems):
    x, y, c = lax.axis_index("x"), lax.axis_index("y"), lax.axis_index("c")
    me = _flat((x, y, c))
    copies = [pltpu.make_async_copy(ins[i].at[me], outs[i].at[me], sems[2].at[i]) for i in range(len(ins))]
    for d in range(1, N_DEV):
        peer = (1 - x if d & 4 else x, 1 - y if d & 2 else y, 1 - c if d & 1 else c)
        for i in range(len(ins)):
            copies.append(_remote(ins[i].at[_flat(peer)], outs[i].at[me], sems, i, d - 1, peer))
    return copies


def _two_level_gather(ins, outs, sems):
    x, y, c = lax.axis_index("x"), lax.axis_index("y"), lax.axis_index("c")
    me, sibling = (x, y, c), (x, y, 1 - c)
    chips = [(1 - x, y), (x, 1 - y), (1 - x, 1 - y)]
    n = len(ins)

    def block(i, pos):
        return outs[i].at[_flat(pos)]

    local = [pltpu.make_async_copy(ins[i], block(i, me), sems[2].at[i]) for i in range(n)]
    own = [_remote(ins[i], block(i, me), sems, i, 0, sibling) for i in range(n)]
    own += [_remote(ins[i], block(i, me), sems, i, 1 + j, (*chip, c)) for i in range(n) for j, chip in enumerate(chips)]
    passed = [[_remote(block(i, (*chip, c)), block(i, (*chip, c)), sems, i, 4 + j, sibling) for i in range(n)]
              for j, chip in enumerate(chips)]

    def first():
        for cp in local + own:
            cp.start()

    def relay():
        for j, chip in enumerate(chips):
            for i in range(n):
                _remote(ins[i], block(i, (*chip, c)), sems, i, 1 + j, me).wait_recv()
                passed[j][i].start()

    def finish():
        for i in range(n):
            _remote(ins[i], block(i, sibling), sems, i, 0, me).wait_recv()
            for j, chip in enumerate(chips):
                _remote(ins[i], block(i, (*chip, 1 - c)), sems, i, 4 + j, me).wait_recv()
        for cp in own + [cp for row in passed for cp in row]:
            cp.wait_send()
        for cp in local:
            cp.wait()

    return first, relay, finish


def _gather(parts, name):
    n = len(parts)

    def body(*refs):
        start, relay, finish = _two_level_gather(refs[:n], refs[n:2 * n], refs[2 * n:])
        start()
        relay()
        finish()

    return pl.pallas_call(
        body, name=name, in_specs=[ANY_SPEC] * n, out_specs=[ANY_SPEC] * n,
        out_shape=_gathered_shapes(parts), scratch_shapes=_exchange_sems(n),
    )(*parts)


def _tri(kind):
    j = lax.broadcasted_iota(jnp.int32, (BLK, BLK), 0)
    s = lax.broadcasted_iota(jnp.int32, (BLK, BLK), 1)
    m = {"after": j > s, "upto": j <= s, "before": j < s}[kind]
    return jnp.concatenate([jnp.where(m, 1.0, 0.0), jnp.ones((BLK, BLK), F32)], axis=1).astype(BF16)


def _scan_rows(v, tri):
    r = _dot(v.astype(BF16), tri)
    return r[:, :BLK], r[:, BLK:]


HEADS_PER_STEP = LANES // HEAD_DIM


def _first_head_lanes():
    return lax.broadcasted_iota(jnp.int32, (1, LANES), 1) < HEAD_DIM


def _pair_mean(v, first):
    m0 = jnp.sum(jnp.where(first, v, 0.0), axis=-1, keepdims=True)
    m1 = jnp.sum(jnp.where(first, 0.0, v), axis=-1, keepdims=True)
    return jnp.where(first, m0, m1) * (1.0 / HEAD_DIM)


def _pair_norm(v, g2, first):
    return v * lax.rsqrt(_pair_mean(v * v, first) + EPS) * g2


def _pair_norm_bwd(raw, g2, dn, first):
    r = lax.rsqrt(_pair_mean(raw * raw, first) + EPS)
    xh = raw * r
    dxh = dn * g2
    return r * (dxh - xh * _pair_mean(dxh * xh, first)), jnp.sum(dn * xh, axis=0, keepdims=True)


def _block_diag(v, first):
    zero = jnp.zeros_like(v)
    return jnp.concatenate([jnp.where(first, v, zero), jnp.where(first, zero, v)], axis=0)


def _attn_prep(q_ref, k_ref, v_ref, qg_ref, kg_ref, qc_s, kc_s, vd_s, kd_s, n_blk):
    scale = HEAD_DIM ** -0.5
    first = _first_head_lanes()

    def prep(i, _):
        rows = pl.ds(pl.multiple_of(i * BLK, BLK), BLK)
        both = pl.ds(pl.multiple_of(i * 2 * BLK, 2 * BLK), 2 * BLK)
        qh, ql = _split_bf16(_pair_norm(q_ref[rows, :], qg_ref[...], first) * scale)
        kh, kl = _split_bf16(_pair_norm(k_ref[rows, :], kg_ref[...], first))
        for h in range(HEADS_PER_STEP):
            sl = slice(h * HEAD_DIM, (h + 1) * HEAD_DIM)
            qc_s[h, rows, :] = jnp.concatenate([qh[:, sl], ql[:, sl], qh[:, sl], ql[:, sl]], axis=1)
            kc_s[h, rows, :] = jnp.concatenate([kh[:, sl], kh[:, sl], kl[:, sl], kl[:, sl]], axis=1)
        vd_s[both, :] = _block_diag(v_ref[rows, :].astype(BF16), first)
        if kd_s is not None:
            kd_s[both, :] = _block_diag(kh, first)
        return 0

    lax.fori_loop(0, n_blk, prep, 0)


def _pair_scores(qc, kc_ref, grp, n_tiles=KEY_GROUP):
    zs = []
    for j in range(0, n_tiles, 2):
        two = pl.ds(pl.multiple_of((grp * KEY_GROUP + j) * BLK, 2 * BLK), 2 * BLK)
        z = _dot_nt(qc, kc_ref[two, :])
        zs += [z[:, :BLK], z[:, BLK:]]
    return zs[:n_tiles]


def _col_minus_row():
    row = lax.broadcasted_iota(jnp.int32, (BLK, BLK), 0)
    col = lax.broadcasted_iota(jnp.int32, (BLK, BLK), 1)
    return col - row


def _softplus(z):
    return jnp.maximum(z, 0.0) + jnp.log(1.0 + jnp.exp(-jnp.abs(z)))


def _attn_fwd(proj, qg, kg, sends, name):
    s = proj.shape[0]
    n_blk = s // BLK
    pairs = ATTN_WIDTH // LANES
    n_send = len(sends)

    def body(q_ref, k_ref, v_ref, qg_ref, kg_ref, *rest):
        send_refs, (o_ref, t_ref), rest = rest[:n_send], rest[n_send:n_send + 2], rest[n_send + 2:]
        got_refs, (qc_s, kc_s, vd_s), sems = rest[:n_send], rest[n_send:n_send + 3], rest[n_send + 3:]
        start, relay, finish = _two_level_gather(send_refs, got_refs, sems)
        step = pl.program_id(0)
        pl.when(step == 0)(start)
        pl.when(step == pairs - 1)(relay)
        tri = _tri("after")
        diff = _col_minus_row()
        first = _first_head_lanes()
        heads = range(HEADS_PER_STEP)
        _attn_prep(q_ref, k_ref, v_ref, qg_ref, kg_ref, qc_s, kc_s, vd_s, None, n_blk)

        below = diff < 0

        def group(qc, grp, carry, acc, n_tiles, diagonal):
            blocks = [grp * KEY_GROUP + j for j in reversed(range(n_tiles))]
            zs = [_pair_scores(qc[h], kc_s.at[h], grp, n_tiles)[::-1] for h in heads]
            parts = [[None] * n_tiles for _ in heads]
            for h in heads:
                for j, z in enumerate(zs[h]):
                    sp = _softplus(z)
                    lom = -sp
                    if diagonal and j == 0:
                        lom = jnp.where(below, lom, 0.0)
                    tail, tot = _scan_rows(lom, tri)
                    parts[h][j] = (z - sp + tail, tot)
            carry = list(carry)
            for j, kb in enumerate(blocks):
                ws = []
                for h in heads:
                    lw, tot = parts[h][j]
                    w = jnp.exp(lw + carry[h])
                    if diagonal and j == 0:
                        w = jnp.where(below, w, 0.0)
                    ws.append(w.astype(BF16))
                    carry[h] = carry[h] + tot
                acc = acc + _dot(jnp.concatenate(ws, axis=1),
                                 vd_s[pl.ds(pl.multiple_of(kb * 2 * BLK, 2 * BLK), 2 * BLK), :])
            return tuple(carry), acc

        def q_blocks(top, _):
            for r in range(KEY_GROUP):
                rows = pl.ds(pl.multiple_of((top * KEY_GROUP + r) * BLK, BLK), BLK)
                qc = [qc_s[h, rows, :] for h in heads]
                zero = jnp.zeros((BLK, BLK), F32)
                carry, acc = group(qc, top, (zero,) * HEADS_PER_STEP, jnp.zeros((BLK, LANES), F32), r + 1, True)
                carry, acc = lax.fori_loop(
                    0, top, lambda t, c: group(qc, top - 1 - t, c[0], c[1], KEY_GROUP, False), (carry, acc))
                o_ref[rows, :] = acc.astype(BF16)
                t_ref[rows, :] = jnp.where(first, carry[0], carry[1])
            return 0

        lax.fori_loop(0, n_blk // KEY_GROUP, q_blocks, 0)
        pl.when(step == pairs - 1)(finish)

    col = lambda off: pl.BlockSpec((s, LANES), lambda p: (0, off + p))
    vec = pl.BlockSpec((1, LANES), lambda p: (0, 0))
    out = pl.pallas_call(
        body, name=name, grid=(pairs,),
        in_specs=[col(0), col(pairs), col(2 * pairs), vec, vec] + [ANY_SPEC] * n_send,
        out_specs=[pl.BlockSpec((s, LANES), lambda p: (0, p))] * 2 + [ANY_SPEC] * n_send,
        out_shape=[jax.ShapeDtypeStruct((s, 2 * ATTN_WIDTH), BF16), jax.ShapeDtypeStruct((s, ATTN_WIDTH), F32)]
        + _gathered_shapes(sends),
        scratch_shapes=[pltpu.VMEM((HEADS_PER_STEP, s, 4 * HEAD_DIM), BF16)] * 2
        + [pltpu.VMEM((HEADS_PER_STEP * s, LANES), BF16)] + _exchange_sems(n_send),
        compiler_params=_params(1),
    )(proj, proj, proj, qg, kg, *sends)
    return out[0], out[1], out[2:]


def _attn_bwd(proj, dcat, tsum, qg, kg, sends, name):
    s = proj.shape[0]
    n_blk = s // BLK
    pairs = ATTN_WIDTH // LANES
    scale = HEAD_DIM ** -0.5
    n_send = len(sends)
    n_scratch = 7

    def body(q_ref, k_ref, v_ref, do_ref, t_ref, qg_ref, kg_ref, *rest):
        send_refs, rest = rest[:n_send], rest[n_send:]
        (dq_ref, dk_ref, dv_ref, dqg_ref, dkg_ref), rest = rest[:5], rest[5:]
        got_refs, scratch, sems = rest[:n_send], rest[n_send:n_send + n_scratch], rest[n_send + n_scratch:]
        qc_s, kc_s, vd_s, kd_s, qd_s, dob_s, dkv_s = scratch
        copies = _scatter_copies(send_refs, got_refs, sems)

        @pl.when(pl.program_id(0) == 0)
        def _():
            for cp in copies:
                cp.start()

        tri_p = _tri("upto")
        tri_h = _tri("before")
        diff = _col_minus_row()

        @pl.when(pl.program_id(0) == 0)
        def _():
            dqg_ref[...] = jnp.zeros_like(dqg_ref)
            dkg_ref[...] = jnp.zeros_like(dkg_ref)

        first = _first_head_lanes()
        heads = range(HEADS_PER_STEP)
        _attn_prep(q_ref, k_ref, v_ref, qg_ref, kg_ref, qc_s, kc_s, vd_s, kd_s, n_blk)

        def prep(i, _):
            rows = pl.ds(pl.multiple_of(i * BLK, BLK), BLK)
            both = pl.ds(pl.multiple_of(i * 2 * BLK, 2 * BLK), 2 * BLK)
            dob = do_ref[rows, :].astype(BF16)
            dob_s[rows, :] = dob
            none = jnp.zeros((BLK, HEAD_DIM), BF16)
            for h in heads:
                qd_s[h, both, :] = jnp.concatenate(
                    [jnp.concatenate([qc_s[h, rows, 0:HEAD_DIM], none], axis=1),
                     jnp.concatenate([none, dob[:, h * HEAD_DIM:(h + 1) * HEAD_DIM]], axis=1)], axis=0)
                dkv_s[h, rows, :] = jnp.zeros((BLK, LANES), F32)
            return 0

        lax.fori_loop(0, n_blk, prep, 0)

        below = diff < 0

        def group(qc, qd, dob, tq, grp, pc, hc, dq, n_tiles, diagonal):
            blocks = [grp * KEY_GROUP + j for j in range(n_tiles)]
            cols_of = [pl.ds(pl.multiple_of(kb * BLK, BLK), BLK) for kb in blocks]
            both_of = [pl.ds(pl.multiple_of(kb * 2 * BLK, 2 * BLK), 2 * BLK) for kb in blocks]
            on_diagonal = [diagonal and j == n_tiles - 1 for j in range(n_tiles)]
            zs = [_pair_scores(qc[h], kc_s.at[h], grp, n_tiles) for h in heads]
            das = [_dot_nt(dob, vd_s[both, :]) for both in both_of]
            lbs = [[None] * n_tiles for _ in heads]
            scans = [[None] * n_tiles for _ in heads]
            for h in heads:
                for j, z in enumerate(zs[h]):
                    sp = _softplus(z)
                    lom = -sp
                    if on_diagonal[j]:
                        lom = jnp.where(below, lom, 0.0)
                    lbs[h][j] = z - sp
                    scans[h][j] = _scan_rows(lom, tri_p)
            pc, hc = list(pc), list(hc)
            avs = [[None] * n_tiles for _ in heads]
            gws = [[None] * n_tiles for _ in heads]
            hscans = [[None] * n_tiles for _ in heads]
            for h in heads:
                for j in range(n_tiles):
                    p_in, p_tot = scans[h][j]
                    a = jnp.exp(lbs[h][j] + (tq[h] - pc[h] - p_in))
                    if on_diagonal[j]:
                        a = jnp.where(below, a, 0.0)
                    pc[h] = pc[h] + p_tot
                    gw = das[j][:, h * BLK:(h + 1) * BLK] * a
                    avs[h][j] = a.astype(BF16)
                    gws[h][j] = gw
                    hscans[h][j] = _scan_rows(gw, tri_h)
            dzs = [[None] * n_tiles for _ in heads]
            for h in heads:
                for j in range(n_tiles):
                    h_in, g_tot = hscans[h][j]
                    gw = gws[h][j]
                    dz = gw - jnp.exp(lbs[h][j]) * (gw + hc[h] + h_in)
                    if on_diagonal[j]:
                        dz = jnp.where(below, dz, 0.0)
                    hc[h] = hc[h] + g_tot
                    dzs[h][j] = dz.astype(BF16)
            for j, both in enumerate(both_of):
                dq = dq + _dot(jnp.concatenate([dzs[h][j] for h in heads], axis=1), kd_s[both, :])
            for h in heads:
                for j, cols in enumerate(cols_of):
                    dkv_s[h, cols, :] += _dot_tn(jnp.concatenate([dzs[h][j], avs[h][j]], axis=0), qd[h])
            return tuple(pc), tuple(hc), dq

        def q_blocks(top, dqg):
            for r in range(KEY_GROUP):
                qi = top * KEY_GROUP + r
                rows = pl.ds(pl.multiple_of(qi * BLK, BLK), BLK)
                both = pl.ds(pl.multiple_of(qi * 2 * BLK, 2 * BLK), 2 * BLK)
                qc = [qc_s[h, rows, :] for h in heads]
                qd = [qd_s[h, both, :] for h in heads]
                dob = dob_s[rows, :]
                tboth = t_ref[rows, :]
                tq = [jnp.concatenate([tboth[:, h * HEAD_DIM:(h + 1) * HEAD_DIM]] * 2, axis=1) for h in heads]
                zero = (jnp.zeros((BLK, BLK), F32),) * HEADS_PER_STEP
                pc, hc, dq = lax.fori_loop(
                    0, top, lambda grp, c: group(qc, qd, dob, tq, grp, c[0], c[1], c[2], KEY_GROUP, False),
                    (zero, zero, jnp.zeros((BLK, LANES), F32)))
                _, _, dq = group(qc, qd, dob, tq, top, pc, hc, dq, r + 1, True)
                dq_raw, dg = _pair_norm_bwd(q_ref[rows, :], qg_ref[...], dq * scale, first)
                dq_ref[rows, :] = dq_raw.astype(BF16)
                dqg = dqg + dg
            return dqg

        dqg = lax.fori_loop(0, n_blk // KEY_GROUP, q_blocks, jnp.zeros((1, LANES), F32))

        def finish(i, dkg):
            rows = pl.ds(pl.multiple_of(i * BLK, BLK), BLK)
            dk = jnp.concatenate([dkv_s[h, rows, 0:HEAD_DIM] for h in heads], axis=1)
            dv = jnp.concatenate([dkv_s[h, rows, HEAD_DIM:2 * HEAD_DIM] for h in heads], axis=1)
            dk_raw, dg = _pair_norm_bwd(k_ref[rows, :], kg_ref[...], dk, first)
            dk_ref[rows, :] = dk_raw.astype(BF16)
            dv_ref[rows, :] = dv.astype(BF16)
            return dkg + dg

        dkg = lax.fori_loop(0, n_blk, finish, jnp.zeros((1, LANES), F32))
        dqg_ref[0:1, :] += dqg
        dkg_ref[0:1, :] += dkg

        @pl.when(pl.program_id(0) == pairs - 1)
        def _():
            for cp in copies:
                cp.wait()

    col = lambda off: pl.BlockSpec((s, LANES), lambda p: (0, off + p))
    vec = pl.BlockSpec((1, LANES), lambda p: (0, 0))
    small = pl.BlockSpec((8, LANES), lambda p: (0, 0))
    out = pl.pallas_call(
        body, name=name, grid=(pairs,),
        in_specs=[col(0), col(pairs), col(2 * pairs), col(0), col(0), vec, vec] + [ANY_SPEC] * n_send,
        out_specs=[col(0)] * 3 + [small] * 2 + [ANY_SPEC] * n_send,
        out_shape=[jax.ShapeDtypeStruct((s, ATTN_WIDTH), BF16)] * 3 + [jax.ShapeDtypeStruct((8, LANES), F32)] * 2
        + _scattered_shapes(sends),
        scratch_shapes=[pltpu.VMEM((HEADS_PER_STEP, s, 4 * HEAD_DIM), BF16)] * 2
        + [pltpu.VMEM((HEADS_PER_STEP * s, LANES), BF16)] * 2
        + [pltpu.VMEM((HEADS_PER_STEP, HEADS_PER_STEP * s, LANES), BF16), pltpu.VMEM((s, LANES), BF16),
           pltpu.VMEM((HEADS_PER_STEP, s, LANES), F32)] + _exchange_sems(n_send),
        compiler_params=_params(1),
    )(proj, proj, proj, dcat, tsum, qg, kg, *sends)
    return out[:5], out[5:]


CONV_ROWS = 128


def _shifted(window, shift, halo):
    if shift == 0:
        return window[halo:, :]
    return pltpu.roll(window, shift, 0)[halo:, :]


SUBLANES = 8


def _row_shifts(window, up):
    n = window.shape[0]
    return [window] + [pltpu.roll(window, n - b if up else b, 0) for b in range(1, SUBLANES)]


def _earlier(shifts, back, rows):
    a, b = divmod(back, SUBLANES)
    return shifts[b][CONV_HALO - SUBLANES * a:CONV_HALO - SUBLANES * a + rows, :]


def _later(shifts, ahead, rows):
    a, b = divmod(ahead, SUBLANES)
    return shifts[b][SUBLANES * a:SUBLANES * a + rows, :]


def _lane_blocks(width):
    return [slice(c, c + LANES) for c in range(0, width, LANES)]


def _conv_taps(shifts, w_ref, lanes, rows):
    y = None
    for k in range(CONV_KERNEL):
        term = _earlier(shifts, CONV_KERNEL - 1 - k, rows) * w_ref[k:k + 1, lanes]
        y = term if y is None else y + term
    return y


def _conv_fwd(proj, cat, w, b, lg, lb, name):
    s = proj.shape[0]
    cw = w.shape[1]
    rows = CONV_ROWS
    blk_a = (proj.shape[1] - 2 * cw) // cw

    def body(a_ref, g_ref, w_ref, b_ref, lg_ref, lb_ref, _cat_ref, o_ref, y_ref, u_s):
        u_s[0:CONV_HALO, :] = jnp.zeros((CONV_HALO, cw), F32)

        def glu(i, _):
            r0 = pl.multiple_of(i * rows, rows)
            u_s[pl.ds(CONV_HALO + r0, rows), :] = a_ref[pl.ds(r0, rows), :] * _sigmoid(g_ref[pl.ds(r0, rows), :])
            return 0

        lax.fori_loop(0, s // rows, glu, 0)

        def chunk(i, _):
            r0 = pl.multiple_of(i * rows, rows)
            for lanes in _lane_blocks(cw):
                shifts = _row_shifts(u_s[pl.ds(r0, CONV_HALO + rows), lanes], False)
                y_ref[pl.ds(r0, rows), lanes] = _conv_taps(shifts, w_ref, lanes, rows) + b_ref[:, lanes]
            y = y_ref[pl.ds(r0, rows), :]
            yc = y - jnp.mean(y, axis=-1, keepdims=True)
            n = yc * lax.rsqrt(jnp.mean(yc * yc, axis=-1, keepdims=True) + EPS)
            ln = n * lg_ref[...] + lb_ref[...]
            o_ref[pl.ds(r0, rows), :] = (ln * _sigmoid(ln)).astype(BF16)
            return 0

        lax.fori_loop(0, s // rows, chunk, 0)

    vec = pl.BlockSpec((1, cw), lambda i: (0, 0))
    mine = pl.BlockSpec((s, cw), lambda i: (0, cat.shape[1] // cw - 1))
    return pl.pallas_call(
        body, name=name, grid=(1,),
        in_specs=[pl.BlockSpec((s, cw), lambda i: (0, blk_a)), pl.BlockSpec((s, cw), lambda i: (0, blk_a + 1)),
                  pl.BlockSpec((CONV_KERNEL, cw), lambda i: (0, 0)), vec, vec, vec, mine],
        out_specs=[mine, pl.BlockSpec((s, cw), lambda i: (0, 0))],
        out_shape=[jax.ShapeDtypeStruct(cat.shape, BF16), jax.ShapeDtypeStruct((s, cw), F32)],
        scratch_shapes=[pltpu.VMEM((CONV_HALO + s, cw), F32)],
        input_output_aliases={6: 0},
        compiler_params=_params(1),
    )(proj, proj, w, b, lg, lb, cat)


def _conv_bwd(proj, y, dcat, w, lg, lb, sends, name):
    s = proj.shape[0]
    cw = w.shape[1]
    rows = CONV_ROWS
    blk_a = (proj.shape[1] - 2 * cw) // cw
    n_chunk = s // rows
    n_send = len(sends)

    def body(a_ref, g_ref, y_ref, dc_ref, w_ref, lg_ref, lb_ref, *rest):
        send_refs, (o_ref, dw_ref, db_ref, dlg_ref, dlb_ref), rest = rest[:n_send], rest[n_send:n_send + 5], rest[n_send + 5:]
        got_refs, (u_s, dy_s, dw_s), sems = rest[:n_send], rest[n_send:n_send + 3], rest[n_send + 3:]
        copies = _scatter_copies(send_refs, got_refs, sems)
        for cp in copies:
            cp.start()
        u_s[0:CONV_HALO, :] = jnp.zeros((CONV_HALO, cw), F32)
        dy_s[pl.ds(s, CONV_HALO), :] = jnp.zeros((CONV_HALO, cw), F32)
        dw_s[...] = jnp.zeros_like(dw_s)

        def glu(i, _):
            r0 = pl.multiple_of(i * rows, rows)
            u_s[pl.ds(CONV_HALO + r0, rows), :] = a_ref[pl.ds(r0, rows), :] * _sigmoid(g_ref[pl.ds(r0, rows), :])
            return 0

        lax.fori_loop(0, n_chunk, glu, 0)

        def chunk(i, carry):
            db, dlg, dlb = carry
            r0 = pl.multiple_of(i * rows, rows)
            y = y_ref[pl.ds(r0, rows), :]
            yc = y - jnp.mean(y, axis=-1, keepdims=True)
            r = lax.rsqrt(jnp.mean(yc * yc, axis=-1, keepdims=True) + EPS)
            n = yc * r
            ln = n * lg_ref[...] + lb_ref[...]
            sg = _sigmoid(ln)
            dln = dc_ref[pl.ds(r0, rows), :] * (sg * (1.0 + ln * (1.0 - sg)))
            dn = dln * lg_ref[...]
            dy = r * (dn - jnp.mean(dn, axis=-1, keepdims=True) - n * jnp.mean(dn * n, axis=-1, keepdims=True))
            dy_s[pl.ds(r0, rows), :] = dy
            for lanes in _lane_blocks(cw):
                shifts = _row_shifts(u_s[pl.ds(r0, CONV_HALO + rows), lanes], False)
                dy_part = dy[:, lanes]
                for k in range(CONV_KERNEL):
                    prod = _earlier(shifts, CONV_KERNEL - 1 - k, rows) * dy_part
                    dw_s[k, :, lanes] += jnp.sum(prod.reshape(rows // SUBLANES, SUBLANES, LANES), axis=0)
            return (db + jnp.sum(dy, axis=0, keepdims=True),
                    dlg + jnp.sum(dln * n, axis=0, keepdims=True),
                    dlb + jnp.sum(dln, axis=0, keepdims=True))

        zero = jnp.zeros((1, cw), F32)
        db, dlg, dlb = lax.fori_loop(0, n_chunk, chunk, (zero, zero, zero))
        db_ref[...] = db
        dlg_ref[...] = dlg
        dlb_ref[...] = dlb
        for k in range(CONV_KERNEL):
            dw_ref[k:k + 1, :] = jnp.sum(dw_s[k], axis=0, keepdims=True)

        def back(i, _):
            r0 = pl.multiple_of(i * rows, rows)
            for lanes in _lane_blocks(cw):
                shifts = _row_shifts(dy_s[pl.ds(r0, rows + CONV_HALO), lanes], True)
                du = None
                for k in range(CONV_KERNEL):
                    term = _later(shifts, CONV_KERNEL - 1 - k, rows) * w_ref[k:k + 1, lanes]
                    du = term if du is None else du + term
                av = a_ref[pl.ds(r0, rows), lanes]
                sg = _sigmoid(g_ref[pl.ds(r0, rows), lanes])
                o_ref[pl.ds(r0, rows), lanes] = (du * sg).astype(BF16)
                o_ref[pl.ds(r0, rows), slice(cw + lanes.start, cw + lanes.stop)] = (du * av * sg * (1.0 - sg)).astype(BF16)
            return 0

        lax.fori_loop(0, n_chunk, back, 0)
        for cp in copies:
            cp.wait()

    vec = pl.BlockSpec((1, cw), lambda i: (0, 0))
    wspec = pl.BlockSpec((CONV_KERNEL, cw), lambda i: (0, 0))
    out = pl.pallas_call(
        body, name=name, grid=(1,),
        in_specs=[pl.BlockSpec((s, cw), lambda i: (0, blk_a)), pl.BlockSpec((s, cw), lambda i: (0, blk_a + 1)),
                  pl.BlockSpec((s, cw), lambda i: (0, 0)), pl.BlockSpec((s, cw), lambda i: (0, 1)), wspec, vec, vec]
        + [ANY_SPEC] * n_send,
        out_specs=[pl.BlockSpec((s, 2 * cw), lambda i: (0, 0)), wspec, vec, vec, vec] + [ANY_SPEC] * n_send,
        out_shape=[jax.ShapeDtypeStruct((s, 2 * cw), BF16), jax.ShapeDtypeStruct((CONV_KERNEL, cw), F32)]
        + [jax.ShapeDtypeStruct((1, cw), F32)] * 3 + _scattered_shapes(sends),
        scratch_shapes=[pltpu.VMEM((CONV_HALO + s, cw), F32), pltpu.VMEM((s + CONV_HALO, cw), F32),
                        pltpu.VMEM((CONV_KERNEL, 8, cw), F32)] + _exchange_sems(n_send),
        compiler_params=_params(1),
    )(proj, proj, y, dcat, w, lg, lb, *sends)
    return out[:5], out[5:]


FFN_ROWS = 256


def _ffn_gate(g_ref, r0, rows, w_ref, b_ref):
    cur = g_ref[pl.ds(r0, rows), :].astype(F32)
    prev = g_ref[pl.ds(pl.multiple_of(jnp.maximum(r0 - FFN_HALO, 0), FFN_HALO), FFN_HALO), :].astype(F32)
    prev = jnp.where(r0 > 0, prev, 0.0)
    window = jnp.concatenate([prev, cur], axis=0)
    gc = cur * w_ref[FFN_KERNEL - 1:FFN_KERNEL, :] + b_ref[...]
    for k in range(FFN_KERNEL - 1):
        gc = gc + _shifted(window, FFN_KERNEL - 1 - k, FFN_HALO) * w_ref[k:k + 1, :]
    return gc, window


def _ffn_fwd(up, w, b, name):
    s = up.shape[0]
    f = w.shape[1]
    tc = _pick(f, (256, 128))
    nc = f // tc
    rows = _pick(s, (FFN_ROWS, 128))

    def body(g_ref, v_ref, w_ref, b_ref, o_ref):
        def chunk(i, _):
            r0 = pl.multiple_of(i * rows, rows)
            gc, _w = _ffn_gate(g_ref, r0, rows, w_ref, b_ref)
            o_ref[pl.ds(r0, rows), :] = (gc * _sigmoid(gc) * v_ref[pl.ds(r0, rows), :].astype(F32)).astype(BF16)
            return 0

        lax.fori_loop(0, s // rows, chunk, 0)

    return pl.pallas_call(
        body, name=name, grid=(nc,),
        in_specs=[pl.BlockSpec((s, tc), lambda j: (0, j)), pl.BlockSpec((s, tc), lambda j: (0, nc + j)),
                  pl.BlockSpec((FFN_KERNEL, tc), lambda j: (0, j)), pl.BlockSpec((1, tc), lambda j: (0, j))],
        out_specs=pl.BlockSpec((s, tc), lambda j: (0, j)),
        out_shape=jax.ShapeDtypeStruct((s, f), BF16),
        compiler_params=_params(1),
    )(up, up, w, b)


def _ffn_bwd(up, dact, w, b, name):
    s = up.shape[0]
    f = w.shape[1]
    tc = _pick(f, (256, 128))
    nc = f // tc
    rows = _pick(s, (FFN_ROWS, 128))
    n_chunk = s // rows

    def body(g_ref, v_ref, da_ref, w_ref, b_ref, d_ref, dw_ref, db_ref, dgc_s):
        dg_ref, dv_ref = d_ref.at[0], d_ref.at[1]
        dgc_s[pl.ds(s, FFN_HALO), :] = jnp.zeros((FFN_HALO, tc), F32)

        def chunk(i, carry):
            r0 = pl.multiple_of(i * rows, rows)
            gc, window = _ffn_gate(g_ref, r0, rows, w_ref, b_ref)
            sg = _sigmoid(gc)
            da = da_ref[pl.ds(r0, rows), :].astype(F32)
            dv_ref[pl.ds(r0, rows), :] = (da * gc * sg).astype(BF16)
            dgc = da * v_ref[pl.ds(r0, rows), :].astype(F32) * (sg * (1.0 + gc * (1.0 - sg)))
            dgc_s[pl.ds(r0, rows), :] = dgc
            out = [carry[0] + jnp.sum(dgc, axis=0, keepdims=True)]
            for k in range(FFN_KERNEL):
                out.append(carry[1 + k] + jnp.sum(_shifted(window, FFN_KERNEL - 1 - k, FFN_HALO) * dgc,
                                                  axis=0, keepdims=True))
            return tuple(out)

        zero = jnp.zeros((1, tc), F32)
        sums = lax.fori_loop(0, n_chunk, chunk, (zero,) * (1 + FFN_KERNEL))
        db_ref[...] = sums[0]
        for k in range(FFN_KERNEL):
            dw_ref[k:k + 1, :] = sums[1 + k]

        def back(i, _):
            r0 = pl.multiple_of(i * rows, rows)
            window = dgc_s[pl.ds(r0, rows + FFN_HALO), :]
            dg = window[:rows, :] * w_ref[FFN_KERNEL - 1:FFN_KERNEL, :]
            for k in range(FFN_KERNEL - 1):
                sh = FFN_KERNEL - 1 - k
                dg = dg + pltpu.roll(window, rows + FFN_HALO - sh, 0)[:rows, :] * w_ref[k:k + 1, :]
            dg_ref[pl.ds(r0, rows), :] = dg.astype(BF16)
            return 0

        lax.fori_loop(0, n_chunk, back, 0)

    blk = lambda off: pl.BlockSpec((s, tc), lambda j: (0, off + j))
    return pl.pallas_call(
        body, name=name, grid=(nc,),
        in_specs=[blk(0), blk(nc), blk(0), pl.BlockSpec((FFN_KERNEL, tc), lambda j: (0, j)),
                  pl.BlockSpec((1, tc), lambda j: (0, j))],
        out_specs=[pl.BlockSpec((2, s, tc), lambda j: (0, 0, j)), pl.BlockSpec((FFN_KERNEL, tc), lambda j: (0, j)),
                   pl.BlockSpec((1, tc), lambda j: (0, j))],
        out_shape=[jax.ShapeDtypeStruct((2, s, f), BF16),
                   jax.ShapeDtypeStruct((FFN_KERNEL, f), F32), jax.ShapeDtypeStruct((1, f), F32)],
        scratch_shapes=[pltpu.VMEM((s + FFN_HALO, tc), F32)],
        compiler_params=_params(1),
    )(up, up, dact, w, b)


def _loss_head(y, target, name):
    m, n = y.shape
    tm = _pick(m, (256, 128))

    def body(y_ref, t_ref, l_ref, d_ref, db_ref):
        e = y_ref[...] - t_ref[...]
        part = 0.5 * jnp.sum(jnp.sum(e * e, axis=-1, keepdims=True) / n, axis=0, keepdims=True)

        @pl.when(pl.program_id(0) == 0)
        def _():
            l_ref[...] = jnp.zeros_like(l_ref)

        l_ref[...] += part
        d = e / n
        d_ref[...] = d
        db_ref[...] = d.astype(BF16)

    return pl.pallas_call(
        body, name=name, grid=(m // tm,),
        in_specs=[pl.BlockSpec((tm, n), lambda i: (i, 0))] * 2,
        out_specs=[pl.BlockSpec((8, LANES), lambda i: (0, 0)), pl.BlockSpec((tm, n), lambda i: (i, 0)),
                   pl.BlockSpec((tm, n), lambda i: (i, 0))],
        out_shape=[jax.ShapeDtypeStruct((8, LANES), F32), jax.ShapeDtypeStruct((m, n), F32),
                   jax.ShapeDtypeStruct((m, n), BF16)],
        compiler_params=_params(1),
    )(y, target)


def _adamw_math(w, g, m, v):
    m = ADAM_B1 * m + (1.0 - ADAM_B1) * g
    v = ADAM_B2 * v + (1.0 - ADAM_B2) * (g * g)
    m_hat = m / (1.0 - ADAM_B1 ** ADAM_STEP)
    v_hat = v / (1.0 - ADAM_B2 ** ADAM_STEP)
    delta = -ADAM_LR * (m_hat / (jnp.sqrt(v_hat) + ADAM_EPS) + ADAM_WD * w)
    return delta, m, v


def _sum_adamw(parts, w, m, v, sends, name):
    depth, r, c = w.shape
    parts = [list(p) if isinstance(p, (list, tuple)) else [p] for p in parts]
    tr = max(t for t in range(16, min(min(a.shape[1] for p in parts for a in p), 192) + 1, 16)
             if all(a.shape[1] % t == 0 for p in parts for a in p))
    steps = r // tr
    n_send = len(sends)
    chunks = []
    for layer, p in enumerate(parts):
        s0 = 0
        for a in p:
            chunks.append((layer, s0, a.shape[1] // tr))
            s0 += a.shape[1] // tr
    n_chunk = len(chunks)

    def body(*refs):
        p_refs, refs = refs[:n_chunk], refs[n_chunk:]
        (w_ref, m_ref, v_ref), send_refs, refs = refs[:3], refs[3:3 + n_send], refs[3 + n_send:]
        (g_out, d_out, m_out, v_out), got_refs, sems = refs[:4], refs[4:4 + n_send], refs[4 + n_send:]
        is_first = (pl.program_id(0) == 0) & (pl.program_id(1) == 0)
        is_last = (pl.program_id(0) == depth - 1) & (pl.program_id(1) == steps - 1)
        if n_send:
            start, relay, finish = _two_level_gather(send_refs, got_refs, sems)
            pl.when(is_first)(start)
            pl.when(is_last)(relay)
        for p_ref, (layer, s0, n) in zip(p_refs, chunks):
            step = pl.program_id(1)

            @pl.when((pl.program_id(0) == layer) & (step >= s0) & (step < s0 + n))
            def _():
                g = p_ref[0].astype(F32)
                for src in range(1, N_DEV):
                    g = g + p_ref[src].astype(F32)
                d, mn, vn = _adamw_math(w_ref[0], g, m_ref[0], v_ref[0])
                g_out[0] = g
                d_out[0] = d
                m_out[0] = mn
                v_out[0] = vn

        if n_send:
            pl.when(is_last)(finish)

    def part_spec(layer, s0, n):
        return pl.BlockSpec((N_DEV, tr, c), lambda l, i: (0, jnp.clip((l - layer) * steps + i - s0, 0, n - 1), 0))

    blk = pl.BlockSpec((1, tr, c), lambda l, i: (l, i, 0))
    out = pl.pallas_call(
        body, name=name, grid=(depth, steps),
        in_specs=[part_spec(*ch) for ch in chunks] + [blk, blk, blk] + [ANY_SPEC] * n_send,
        out_specs=[blk] * 4 + [ANY_SPEC] * n_send,
        out_shape=[jax.ShapeDtypeStruct(w.shape, F32)] * 4 + _gathered_shapes(sends),
        scratch_shapes=_exchange_sems(n_send) if n_send else [],
        compiler_params=_params(2),
    )(*[a for p in parts for a in p], w, m, v, *sends)
    return out[:4], out[4:]


VMEM_SPEC = pl.BlockSpec(memory_space=pltpu.VMEM)


def _sum_small(parts, name):
    n = len(parts)

    def body(*refs):
        for p_ref, o_ref in zip(refs[:n], refs[n:]):
            g = p_ref[0]
            for src in range(1, N_DEV):
                g = g + p_ref[src]
            o_ref[...] = g

    return pl.pallas_call(
        body, name=name, in_specs=[VMEM_SPEC] * n, out_specs=[VMEM_SPEC] * n,
        out_shape=[jax.ShapeDtypeStruct(p.shape[1:], F32) for p in parts],
        compiler_params=_params(),
    )(*parts)


def _adamw_small(ws, gs, ms, vs, name):
    n = len(ws)

    def body(*refs):
        ins, outs = refs[:4 * n], refs[4 * n:]
        for i in range(n):
            d, mn, vn = _adamw_math(ins[i][...], ins[n + i][...], ins[2 * n + i][...], ins[3 * n + i][...])
            outs[i][...] = d
            outs[n + i][...] = mn
            outs[2 * n + i][...] = vn

    out = pl.pallas_call(
        body, name=name, in_specs=[VMEM_SPEC] * (4 * n), out_specs=[VMEM_SPEC] * (3 * n),
        out_shape=[jax.ShapeDtypeStruct(w.shape, F32) for w in ws] * 3,
        compiler_params=_params(),
    )(*ws, *gs, *ms, *vs)
    return out[:n], out[n:2 * n], out[2 * n:]


def kernel(x, norm1_g, w_in, q_norm_g, k_norm_g, conv_dw_w, conv_dw_b, conv_ln_g, conv_ln_b, w_out, norm2_g, w_up, ffn_dw_w, ffn_dw_b, w_down, loss_target, m_norm1_g, m_w_in, m_q_norm_g, m_k_norm_g, m_conv_dw_w, m_conv_dw_b, m_conv_ln_g, m_conv_ln_b, m_w_out, m_norm2_g, m_w_up, m_ffn_dw_w, m_ffn_dw_b, m_w_down, v_norm1_g, v_w_in, v_q_norm_g, v_k_norm_g, v_conv_dw_w, v_conv_dw_b, v_conv_ln_g, v_conv_ln_b, v_w_out, v_norm2_g, v_w_up, v_ffn_dw_w, v_ffn_dw_b, v_w_down):
    depth = w_in.shape[0]
    d_ff = w_down.shape[1] * N_DEV
    conv_w = conv_dw_b.shape[1]
    cw_shard = conv_dw_w.shape[2]
    fw_shard = ffn_dw_w.shape[2]
    me = 4 * lax.axis_index("x") + 2 * lax.axis_index("y") + lax.axis_index("c")

    transposed = lambda a: a.transpose(0, 2, 1)
    b_in, b_out, b_up, b_down = (transposed(w_in).astype(BF16), w_out.astype(BF16), transposed(w_up).astype(BF16),
                                 w_down.astype(BF16))
    rows_major = lambda g: g.reshape(N_DEV * g.shape[1], g.shape[2])
    g_in0, g_cw, g_fw = _gather([b_in[0], conv_dw_w, ffn_dw_w], name="gather_first")
    wf_in, wf_out, wf_up, wf_down = [rows_major(g_in0)] + [None] * (depth - 1), [None] * depth, [None] * depth, [None] * depth
    cwf = g_cw.transpose(1, 2, 0, 3).reshape(depth, CONV_KERNEL, conv_w)
    fwf = g_fw.transpose(1, 2, 0, 3).reshape(depth, FFN_KERNEL, d_ff)

    row = lambda a, l: a[l].reshape(1, -1)
    both_heads = lambda a, l: jnp.tile(row(a, l), (1, HEADS_PER_STEP))

    xs = x[0]
    saved = []
    for l in range(depth):
        proj, h1, _ = _mm_rms(xs, row(norm1_g, l), wf_in[l], F32, [], name="fwd_in")
        sends = [b_out[l], b_up[l]] + ([b_in[l + 1]] if l + 1 < depth else [])
        attn, tsum, got = _attn_fwd(proj, both_heads(q_norm_g, l), both_heads(k_norm_g, l), sends, name="fwd_attn")
        wf_out[l], wf_up[l] = rows_major(got[0]), rows_major(got[1])
        if l + 1 < depth:
            wf_in[l + 1] = rows_major(got[2])
        cat, conv_y = _conv_fwd(proj, attn, cwf[l], row(conv_dw_b, l), row(conv_ln_g, l), row(conv_ln_b, l),
                                name="fwd_conv")
        x_mid = _mm_res(cat, wf_out[l], xs, name="fwd_out")
        up, h2, got = _mm_rms(x_mid, row(norm2_g, l), wf_up[l], BF16, [b_down[l]], name="fwd_up")
        wf_down[l] = rows_major(got[0])
        act = _ffn_fwd(up, fwf[l], row(ffn_dw_b, l), name="fwd_ffn")
        x_next = _mm_res(act, wf_down[l], x_mid, name="fwd_down")
        saved.append((xs, h1, proj, tsum, cat, x_mid, h2, up, act, conv_y))
        xs = x_next

    loss_tile, dx, dxb = _loss_head(xs, loss_target[0], name="loss_head")
    loss = lax.psum(loss_tile[0, 0], ("x", "y", "c"))

    r_in, r_out, r_up, r_down = [None] * depth, [None] * depth, [None] * depth, [None] * depth
    row_blocks = lambda g: g.reshape(N_DEV, g.shape[0] // N_DEV, g.shape[1])
    small = {k: [None] * depth for k in ("norm1_g", "q_norm_g", "k_norm_g", "conv_dw_w", "conv_dw_b", "conv_ln_g",
                                         "conv_ln_b", "norm2_g", "ffn_dw_w", "ffn_dw_b")}
    gw_in = None
    for l in reversed(range(depth)):
        xs, h1, proj, tsum, cat, x_mid, h2, up, act, conv_y = saved[l]
        dact = _mm_nt(dxb, wf_down[l], BF16, name="bwd_dact")
        gw_down = _mm_tn(act, dxb, name="bwd_gw_down")
        dup, small["ffn_dw_w"][l], small["ffn_dw_b"][l] = _ffn_bwd(up, dact, fwf[l], row(ffn_dw_b, l), name="bwd_ffn")
        gw_up = _mm_tn(dup, h2, name="bwd_gw_up")
        if l + 1 < depth:
            blocks = row_blocks(gw_in)
            half = blocks.shape[1] // 2
            in_halves = [blocks[:, :half], blocks[:, half:]]
        dx, dxb, small["norm2_g"][l], got_a = _mm_rmsbwd(dup, wf_up[l], x_mid, row(norm2_g, l), dx,
                                                         in_halves[:1] if l + 1 < depth else [], name="bwd_up")
        dcat = _mm_nt(dxb, wf_out[l], F32, name="bwd_dcat")
        gw_out = _mm_tn(cat, dxb, name="bwd_gw_out")
        early = in_halves[1] if l + 1 < depth else row_blocks(gw_out)
        (dglu, small["conv_dw_w"][l], small["conv_dw_b"][l], small["conv_ln_g"][l], small["conv_ln_b"][l]), got = (
            _conv_bwd(proj, conv_y, dcat, cwf[l], row(conv_ln_g, l), row(conv_ln_b, l), [early], name="bwd_conv"))
        sends = [row_blocks(gw_down), row_blocks(gw_up)] + ([row_blocks(gw_out)] if l + 1 < depth else [])
        (dq, dk, dv, dqg, dkg), got2 = _attn_bwd(proj, dcat, tsum, both_heads(q_norm_g, l), both_heads(k_norm_g, l),
                                                 sends, name="bwd_attn")
        r_down[l], r_up[l] = got2[:2]
        if l + 1 < depth:
            r_in[l + 1], r_out[l] = [got_a[0], got[0]], got2[2]
        else:
            r_out[l] = got[0]
        small["q_norm_g"][l] = dqg[0:1, :HEAD_DIM] + dqg[0:1, HEAD_DIM:]
        small["k_norm_g"][l] = dkg[0:1, :HEAD_DIM] + dkg[0:1, HEAD_DIM:]
        dproj = jnp.concatenate([dq, dk, dv, dglu], axis=1)
        gw_in = _mm_tn(dproj, h1, name="bwd_gw_in")
        sends = [row_blocks(gw_in)] if l == 0 else []
        dx, dxb, small["norm1_g"][l], got = _mm_rmsbwd(dproj, wf_in[l], xs, row(norm1_g, l), dx, sends, name="bwd_in")
        if l == 0:
            r_in[0] = got[0]
    grad_x = dx[None]

    names = ["norm1_g", "q_norm_g", "k_norm_g", "conv_dw_w", "conv_dw_b", "conv_ln_g", "conv_ln_b", "norm2_g",
             "ffn_dw_w", "ffn_dw_b"]
    full_shapes = {"norm1_g": norm1_g.shape, "q_norm_g": q_norm_g.shape, "k_norm_g": k_norm_g.shape,
                   "conv_dw_w": (depth, CONV_KERNEL, conv_w), "conv_dw_b": conv_dw_b.shape,
                   "conv_ln_g": conv_ln_g.shape, "conv_ln_b": conv_ln_b.shape, "norm2_g": norm2_g.shape,
                   "ffn_dw_w": (depth, FFN_KERNEL, d_ff), "ffn_dw_b": ffn_dw_b.shape}
    partial = [jnp.stack(small[k]).reshape(full_shapes[k]) for k in names]
    big = {}
    big["w_out"], all_partials = _sum_adamw(r_out, w_out, m_w_out, v_w_out, partial, name="adamw_out")
    big["w_up"], _ = _sum_adamw(r_up, transposed(w_up), transposed(m_w_up), transposed(v_w_up), [], name="adamw_up")
    big["w_down"], _ = _sum_adamw(r_down, w_down, m_w_down, v_w_down, [], name="adamw_down")
    big["w_in"], _ = _sum_adamw(r_in, transposed(w_in), transposed(m_w_in), transposed(v_w_in), [], name="adamw_in")
    for k in ("w_in", "w_up"):
        big[k] = [transposed(a) for a in big[k]]

    grads = dict(zip(names, _sum_small(all_partials, name="sum_small_grads")))
    grads["conv_dw_w"] = lax.dynamic_slice_in_dim(grads["conv_dw_w"], me * cw_shard, cw_shard, axis=2)
    grads["ffn_dw_w"] = lax.dynamic_slice_in_dim(grads["ffn_dw_w"], me * fw_shard, fw_shard, axis=2)
    weights = dict(norm1_g=norm1_g, q_norm_g=q_norm_g, k_norm_g=k_norm_g, conv_dw_w=conv_dw_w, conv_dw_b=conv_dw_b,
                   conv_ln_g=conv_ln_g, conv_ln_b=conv_ln_b, norm2_g=norm2_g, ffn_dw_w=ffn_dw_w, ffn_dw_b=ffn_dw_b)
    m_in = dict(norm1_g=m_norm1_g, q_norm_g=m_q_norm_g, k_norm_g=m_k_norm_g, conv_dw_w=m_conv_dw_w,
                conv_dw_b=m_conv_dw_b, conv_ln_g=m_conv_ln_g, conv_ln_b=m_conv_ln_b, norm2_g=m_norm2_g,
                ffn_dw_w=m_ffn_dw_w, ffn_dw_b=m_ffn_dw_b)
    v_in = dict(norm1_g=v_norm1_g, q_norm_g=v_q_norm_g, k_norm_g=v_k_norm_g, conv_dw_w=v_conv_dw_w,
                conv_dw_b=v_conv_dw_b, conv_ln_g=v_conv_ln_g, conv_ln_b=v_conv_ln_b, norm2_g=v_norm2_g,
                ffn_dw_w=v_ffn_dw_w, ffn_dw_b=v_ffn_dw_b)
    d_s, m_s, v_s = _adamw_small([weights[k] for k in names], [grads[k] for k in names], [m_in[k] for k in names],
                                 [v_in[k] for k in names], name="adamw_small")
    delta, new_m, new_v = dict(zip(names, d_s)), dict(zip(names, m_s)), dict(zip(names, v_s))
    for k, (g, d, mn, vn) in big.items():
        grads[k], delta[k], new_m[k], new_v[k] = g, d, mn, vn

    order = ["norm1_g", "w_in", "q_norm_g", "k_norm_g", "conv_dw_w", "conv_dw_b", "conv_ln_g", "conv_ln_b", "w_out",
             "norm2_g", "w_up", "ffn_dw_w", "ffn_dw_b", "w_down"]
    return (loss, grad_x, *[grads[k] for k in order], *[delta[k] for k in order], *[new_m[k] for k in order],
            *[new_v[k] for k in order])
```

```python
import jax
import jax.numpy as jnp
from jax import lax
from jax.experimental import pallas as pl
from jax.experimental.pallas import tpu as pltpu

F32 = jnp.float32
BF16 = jnp.bfloat16

N_DEV = 8
HEADS = 8
HEAD_DIM = 64
ATTN_WIDTH = HEADS * HEAD_DIM
CONV_KERNEL = 31
FFN_KERNEL = 3
EPS = 1e-6
BLK = 128
KEY_GROUP = 4
LANES = 128
NORM_ROWS = 128
CONV_HALO = 32
FFN_HALO = 16

ADAM_LR = 0.001
ADAM_B1 = 0.9
ADAM_B2 = 0.999
ADAM_EPS = 1e-08
ADAM_WD = 0.01
ADAM_STEP = 10

VMEM_LIMIT = 56 * 1024 * 1024


def _params(n_axes=0):
    kw = dict(vmem_limit_bytes=VMEM_LIMIT)
    if n_axes:
        kw["dimension_semantics"] = ("arbitrary",) * n_axes
    return pltpu.CompilerParams(**kw)


def _dot(a, b):
    return jnp.dot(a, b, preferred_element_type=F32)


def _dot_nt(a, b):
    return lax.dot_general(a, b, (((1,), (1,)), ((), ())), preferred_element_type=F32)


def _dot_tn(a, b):
    return lax.dot_general(a, b, (((0,), (0,)), ((), ())), preferred_element_type=F32)


def _sigmoid(x):
    return 1.0 / (1.0 + jnp.exp(-x))


def _split_bf16(x):
    hi = x.astype(BF16)
    lo = (x - hi.astype(F32)).astype(BF16)
    return hi, lo


def _pick(n, options):
    for t in options:
        if n % t == 0:
            return t
    return n


def _tile(n, cap):
    best = None
    for t in range(LANES, min(n, cap) + 1, LANES):
        if n % t == 0:
            best = t
    return best or n


def _mm_rms(x, g, wt, out_dtype, sends, name):
    m, k = x.shape
    n = wt.shape[0]
    tm = _tile(m, 2048)
    tn = _tile(n, 512)
    n_send = len(sends)
    grid = (m // tm, n // tn)

    def body(x_ref, g_ref, w_ref, *rest):
        send_refs, (o_ref, h_ref), rest = rest[:n_send], rest[n_send:n_send + 2], rest[n_send + 2:]
        got_refs, h_s, sems = rest[:n_send], rest[n_send], rest[n_send + 1:]
        if n_send:
            start, relay, finish = _two_level_gather(send_refs, got_refs, sems)
            is_first = (pl.program_id(0) == 0) & (pl.program_id(1) == 0)
            is_last = (pl.program_id(0) == grid[0] - 1) & (pl.program_id(1) == grid[1] - 1)
            pl.when(is_first)(start)
            pl.when(is_last)(relay)

        @pl.when(pl.program_id(1) == 0)
        def _():
            def chunk(c, _):
                rows = pl.ds(pl.multiple_of(c * NORM_ROWS, NORM_ROWS), NORM_ROWS)
                xv = x_ref[rows, :]
                r = lax.rsqrt(jnp.mean(xv * xv, axis=-1, keepdims=True) + EPS)
                hv = (xv * r * g_ref[...]).astype(BF16)
                h_s[rows, :] = hv
                h_ref[rows, :] = hv
                return 0

            lax.fori_loop(0, tm // NORM_ROWS, chunk, 0)

        o_ref[...] = _dot_nt(h_s[...], w_ref[...]).astype(out_dtype)
        if n_send:
            pl.when(is_last)(finish)

    out = pl.pallas_call(
        body, name=name, grid=grid,
        in_specs=[pl.BlockSpec((tm, k), lambda i, j: (i, 0)),
                  pl.BlockSpec((1, k), lambda i, j: (0, 0)),
                  pl.BlockSpec((tn, k), lambda i, j: (j, 0))] + [ANY_SPEC] * n_send,
        out_specs=[pl.BlockSpec((tm, tn), lambda i, j: (i, j)),
                   pl.BlockSpec((tm, k), lambda i, j: (i, 0))] + [ANY_SPEC] * n_send,
        out_shape=[jax.ShapeDtypeStruct((m, n), out_dtype), jax.ShapeDtypeStruct((m, k), BF16)]
        + _gathered_shapes(sends),
        scratch_shapes=[pltpu.VMEM((tm, k), BF16)] + (_exchange_sems(n_send) if n_send else []),
        compiler_params=_params(2),
    )(x, g, wt, *sends)
    return out[0], out[1], out[2:]


def _mm_res(a, w, res, name):
    m, k = a.shape
    n = w.shape[1]
    tm = _tile(m, 2048)
    tn = _tile(n, 512)

    def body(a_ref, w_ref, r_ref, o_ref):
        o_ref[...] = r_ref[...] + _dot(a_ref[...], w_ref[...])

    return pl.pallas_call(
        body, name=name, grid=(m // tm, n // tn),
        in_specs=[pl.BlockSpec((tm, k), lambda i, j: (i, 0)),
                  pl.BlockSpec((k, tn), lambda i, j: (0, j)),
                  pl.BlockSpec((tm, tn), lambda i, j: (i, j))],
        out_specs=pl.BlockSpec((tm, tn), lambda i, j: (i, j)),
        out_shape=jax.ShapeDtypeStruct((m, n), F32),
        compiler_params=_params(2),
    )(a, w, res)


def _mm_nt(a, w, out_dtype, name):
    m, k = a.shape
    n = w.shape[0]
    tm = _tile(m, 1024)
    tn = _tile(n, 1408)

    def body(a_ref, w_ref, o_ref):
        o_ref[...] = _dot_nt(a_ref[...], w_ref[...]).astype(out_dtype)

    return pl.pallas_call(
        body, name=name, grid=(m // tm, n // tn),
        in_specs=[pl.BlockSpec((tm, k), lambda i, j: (i, 0)),
                  pl.BlockSpec((tn, k), lambda i, j: (j, 0))],
        out_specs=pl.BlockSpec((tm, tn), lambda i, j: (i, j)),
        out_shape=jax.ShapeDtypeStruct((m, n), out_dtype),
        compiler_params=_params(2),
    )(a, w)


def _column_tiles(a, cap):
    if a.ndim == 2:
        s, c = a.shape
        tc = _tile(c, cap)
        return s, c, tc, lambda rows, index: pl.BlockSpec((rows, tc), lambda *g: (index(*g)[0], index(*g)[1]))
    slabs, s, width = a.shape
    tc = _tile(width, cap)
    per = width // tc
    return s, slabs * width, tc, lambda rows, index: pl.BlockSpec(
        (None, rows, tc), lambda *g: (index(*g)[1] // per, index(*g)[0], index(*g)[1] % per))


def _mm_tn(a, b, name):
    s, m, tm, a_spec = _column_tiles(a, 1408)
    n = b.shape[1]
    tn = _tile(n, 1024)

    def body(a_ref, b_ref, o_ref):
        o_ref[...] = _dot_tn(a_ref[...], b_ref[...]).astype(BF16)

    return pl.pallas_call(
        body, name=name, grid=(m // tm, n // tn),
        in_specs=[a_spec(s, lambda i, j: (0, i)),
                  pl.BlockSpec((s, tn), lambda i, j: (0, j))],
        out_specs=pl.BlockSpec((tm, tn), lambda i, j: (i, j)),
        out_shape=jax.ShapeDtypeStruct((m, n), BF16),
        compiler_params=_params(2),
    )(a, b)


def _mm_rmsbwd(a, w, x, g, dres, sends, name):
    m, k, tk, a_spec = _column_tiles(a, 1408)
    n = w.shape[1]
    tm = _tile(m, 1024)
    nk = k // tk
    n_send = len(sends)

    def body(a_ref, w_ref, x_ref, g_ref, r_ref, *rest):
        send_refs, (dx_ref, dxb_ref, dg_ref), rest = rest[:n_send], rest[n_send:n_send + 3], rest[n_send + 3:]
        got_refs, acc, sems = rest[:n_send], rest[n_send], rest[n_send + 1:]
        i, kk = pl.program_id(0), pl.program_id(1)
        if n_send:
            copies = _scatter_copies(send_refs, got_refs, sems)

            @pl.when((i == 0) & (kk == 0))
            def _():
                for cp in copies:
                    cp.start()

        part = _dot(a_ref[...], w_ref[...])

        @pl.when(kk == 0)
        def _():
            acc[...] = part

        @pl.when(kk > 0)
        def _():
            acc[...] += part

        @pl.when(kk == nk - 1)
        def _():
            def chunk(c, dgp):
                rows = pl.ds(pl.multiple_of(c * NORM_ROWS, NORM_ROWS), NORM_ROWS)
                dh = acc[rows, :]
                xv = x_ref[rows, :]
                r = lax.rsqrt(jnp.mean(xv * xv, axis=-1, keepdims=True) + EPS)
                xh = xv * r
                dxh = dh * g_ref[...]
                dx = r_ref[rows, :] + r * (dxh - xh * jnp.mean(dxh * xh, axis=-1, keepdims=True))
                dx_ref[rows, :] = dx
                dxb_ref[rows, :] = dx.astype(BF16)
                return dgp + jnp.sum(dh * xh, axis=0, keepdims=True)

            dgp = lax.fori_loop(0, tm // NORM_ROWS, chunk, jnp.zeros((1, n), F32))

            @pl.when(i == 0)
            def _():
                dg_ref[...] = dgp

            @pl.when(i > 0)
            def _():
                dg_ref[...] += dgp

        if n_send:
            @pl.when((i == m // tm - 1) & (kk == nk - 1))
            def _():
                for cp in copies:
                    cp.wait()

    out = pl.pallas_call(
        body, name=name, grid=(m // tm, nk),
        in_specs=[a_spec(tm, lambda i, kk: (i, kk)),
                  pl.BlockSpec((tk, n), lambda i, kk: (kk, 0)),
                  pl.BlockSpec((tm, n), lambda i, kk: (i, 0)),
                  pl.BlockSpec((1, n), lambda i, kk: (0, 0)),
                  pl.BlockSpec((tm, n), lambda i, kk: (i, 0))] + [ANY_SPEC] * n_send,
        out_specs=[pl.BlockSpec((tm, n), lambda i, kk: (i, 0)),
                   pl.BlockSpec((tm, n), lambda i, kk: (i, 0)),
                   pl.BlockSpec((1, n), lambda i, kk: (0, 0))] + [ANY_SPEC] * n_send,
        out_shape=[jax.ShapeDtypeStruct((m, n), F32), jax.ShapeDtypeStruct((m, n), BF16),
                   jax.ShapeDtypeStruct((1, n), F32)] + _scattered_shapes(sends),
        scratch_shapes=[pltpu.VMEM((tm, n), F32)] + (_exchange_sems(n_send) if n_send else []),
        compiler_params=_params(2),
    )(a, w, x, g, dres, *sends)
    return out[0], out[1], out[2], out[3:]


ANY_SPEC = pl.BlockSpec(memory_space=pl.ANY)
SEMS_PER_OPERAND = N_DEV - 1


def _exchange_sems(n):
    return [pltpu.SemaphoreType.DMA((n, SEMS_PER_OPERAND)), pltpu.SemaphoreType.DMA((n, SEMS_PER_OPERAND)),
            pltpu.SemaphoreType.DMA((n,))]


def _gathered_shapes(parts):
    return [jax.ShapeDtypeStruct((N_DEV,) + a.shape, a.dtype) for a in parts]


def _scattered_shapes(parts):
    return [jax.ShapeDtypeStruct(a.shape, a.dtype) for a in parts]


def _flat(pos):
    return 4 * pos[0] + 2 * pos[1] + pos[2]


def _remote(src, dst, sems, i, k, to):
    send_sems, recv_sems, _ = sems
    return pltpu.make_async_remote_copy(src_ref=src, dst_ref=dst, send_sem=send_sems.at[i, k],
                                        recv_sem=recv_sems.at[i, k], device_id=to,
                                        device_id_type=pl.DeviceIdType.MESH)


def _scatter_copies(ins, outs, sems):
    x, y, c = lax.axis_index("x"), lax.axis_index("y"), lax.axis_index("c")
    me = _flat((x, y, c))
    copies = [pltpu.make_async_copy(ins[i].at[me], outs[i].at[me], sems[2].at[i]) for i in range(len(ins))]
    for d in range(1, N_DEV):
        peer = (1 - x if d & 4 else x, 1 - y if d & 2 else y, 1 - c if d & 1 else c)
        for i in range(len(ins)):
            copies.append(_remote(ins[i].at[_flat(peer)], outs[i].at[me], sems, i, d - 1, peer))
    return copies


def _two_level_gather(ins, outs, sems):
    x, y, c = lax.axis_index("x"), lax.axis_index("y"), lax.axis_index("c")
    me, sibling = (x, y, c), (x, y, 1 - c)
    chips = [(1 - x, y), (x, 1 - y), (1 - x, 1 - y)]
    n = len(ins)

    def block(i, pos):
        return outs[i].at[_flat(pos)]

    local = [pltpu.make_async_copy(ins[i], block(i, me), sems[2].at[i]) for i in range(n)]
    own = [_remote(ins[i], block(i, me), sems, i, 0, sibling) for i in range(n)]
    own += [_remote(ins[i], block(i, me), sems, i, 1 + j, (*chip, c)) for i in range(n) for j, chip in enumerate(chips)]
    passed = [[_remote(block(i, (*chip, c)), block(i, (*chip, c)), sems, i, 4 + j, sibling) for i in range(n)]
              for j, chip in enumerate(chips)]

    def first():
        for cp in local + own:
            cp.start()

    def relay():
        for j, chip in enumerate(chips):
            for i in range(n):
                _remote(ins[i], block(i, (*chip, c)), sems, i, 1 + j, me).wait_recv()
                passed[j][i].start()

    def finish():
        for i in range(n):
            _remote(ins[i], block(i, sibling), sems, i, 0, me).wait_recv()
            for j, chip in enumerate(chips):
                _remote(ins[i], block(i, (*chip, 1 - c)), sems, i, 4 + j, me).wait_recv()
        for cp in own + [cp for row in passed for cp in row]:
            cp.wait_send()
        for cp in local:
            cp.wait()

    return first, relay, finish


def _gather(parts, name):
    n = len(parts)

    def body(*refs):
        start, relay, finish = _two_level_gather(refs[:n], refs[n:2 * n], refs[2 * n:])
        start()
        relay()
        finish()

    return pl.pallas_call(
        body, name=name, in_specs=[ANY_SPEC] * n, out_specs=[ANY_SPEC] * n,
        out_shape=_gathered_shapes(parts), scratch_shapes=_exchange_sems(n),
    )(*parts)


def _tri(kind):
    j = lax.broadcasted_iota(jnp.int32, (BLK, BLK), 0)
    s = lax.broadcasted_iota(jnp.int32, (BLK, BLK), 1)
    m = {"after": j > s, "upto": j <= s, "before": j < s}[kind]
    return jnp.concatenate([jnp.where(m, 1.0, 0.0), jnp.ones((BLK, BLK), F32)], axis=1).astype(BF16)


def _scan_rows(v, tri):
    r = _dot(v.astype(BF16), tri)
    return r[:, :BLK], r[:, BLK:]


HEADS_PER_STEP = LANES // HEAD_DIM


def _first_head_lanes():
    return lax.broadcasted_iota(jnp.int32, (1, LANES), 1) < HEAD_DIM


def _pair_mean(v, first):
    m0 = jnp.sum(jnp.where(first, v, 0.0), axis=-1, keepdims=True)
    m1 = jnp.sum(jnp.where(first, 0.0, v), axis=-1, keepdims=True)
    return jnp.where(first, m0, m1) * (1.0 / HEAD_DIM)


def _pair_norm(v, g2, first):
    return v * lax.rsqrt(_pair_mean(v * v, first) + EPS) * g2


def _pair_norm_bwd(raw, g2, dn, first):
    r = lax.rsqrt(_pair_mean(raw * raw, first) + EPS)
    xh = raw * r
    dxh = dn * g2
    return r * (dxh - xh * _pair_mean(dxh * xh, first)), jnp.sum(dn * xh, axis=0, keepdims=True)


def _block_diag(v, first):
    zero = jnp.zeros_like(v)
    return jnp.concatenate([jnp.where(first, v, zero), jnp.where(first, zero, v)], axis=0)


def _attn_prep(q_ref, k_ref, v_ref, qg_ref, kg_ref, qc_s, kc_s, vd_s, kd_s, n_blk):
    scale = HEAD_DIM ** -0.5
    first = _first_head_lanes()

    def prep(i, _):
        rows = pl.ds(pl.multiple_of(i * BLK, BLK), BLK)
        both = pl.ds(pl.multiple_of(i * 2 * BLK, 2 * BLK), 2 * BLK)
        qh, ql = _split_bf16(_pair_norm(q_ref[rows, :], qg_ref[...], first) * scale)
        kh, kl = _split_bf16(_pair_norm(k_ref[rows, :], kg_ref[...], first))
        for h in range(HEADS_PER_STEP):
            sl = slice(h * HEAD_DIM, (h + 1) * HEAD_DIM)
            qc_s[h, rows, :] = jnp.concatenate([qh[:, sl], ql[:, sl], qh[:, sl], ql[:, sl]], axis=1)
            kc_s[h, rows, :] = jnp.concatenate([kh[:, sl], kh[:, sl], kl[:, sl], kl[:, sl]], axis=1)
        vd_s[both, :] = _block_diag(v_ref[rows, :].astype(BF16), first)
        if kd_s is not None:
            kd_s[both, :] = _block_diag(kh, first)
        return 0

    lax.fori_loop(0, n_blk, prep, 0)


def _pair_scores(qc, kc_ref, grp, n_tiles=KEY_GROUP):
    zs = []
    for j in range(0, n_tiles, 2):
        two = pl.ds(pl.multiple_of((grp * KEY_GROUP + j) * BLK, 2 * BLK), 2 * BLK)
        z = _dot_nt(qc, kc_ref[two, :])
        zs += [z[:, :BLK], z[:, BLK:]]
    return zs[:n_tiles]


def _col_minus_row():
    row = lax.broadcasted_iota(jnp.int32, (BLK, BLK), 0)
    col = lax.broadcasted_iota(jnp.int32, (BLK, BLK), 1)
    return col - row


def _softplus(z):
    return jnp.maximum(z, 0.0) + jnp.log(1.0 + jnp.exp(-jnp.abs(z)))


def _attn_fwd(proj, qg, kg, sends, name):
    s = proj.shape[0]
    n_blk = s // BLK
    pairs = ATTN_WIDTH // LANES
    n_send = len(sends)

    def body(q_ref, k_ref, v_ref, qg_ref, kg_ref, *rest):
        send_refs, (o_ref, t_ref), rest = rest[:n_send], rest[n_send:n_send + 2], rest[n_send + 2:]
        got_refs, (qc_s, kc_s, vd_s), sems = rest[:n_send], rest[n_send:n_send + 3], rest[n_send + 3:]
        start, relay, finish = _two_level_gather(send_refs, got_refs, sems)
        step = pl.program_id(0)
        pl.when(step == 0)(start)
        pl.when(step == pairs - 1)(relay)
        tri = _tri("after")
        diff = _col_minus_row()
        first = _first_head_lanes()
        heads = range(HEADS_PER_STEP)
        _attn_prep(q_ref, k_ref, v_ref, qg_ref, kg_ref, qc_s, kc_s, vd_s, None, n_blk)

        below = diff < 0

        def group(qc, grp, carry, acc, n_tiles, diagonal):
            blocks = [grp * KEY_GROUP + j for j in reversed(range(n_tiles))]
            zs = [_pair_scores(qc[h], kc_s.at[h], grp, n_tiles)[::-1] for h in heads]
            parts = [[None] * n_tiles for _ in heads]
            for h in heads:
                for j, z in enumerate(zs[h]):
                    sp = _softplus(z)
                    lom = -sp
                    if diagonal and j == 0:
                        lom = jnp.where(below, lom, 0.0)
                    tail, tot = _scan_rows(lom, tri)
                    parts[h][j] = (z - sp + tail, tot)
            carry = list(carry)
            for j, kb in enumerate(blocks):
                ws = []
                for h in heads:
                    lw, tot = parts[h][j]
                    w = jnp.exp(lw + carry[h])
                    if diagonal and j == 0:
                        w = jnp.where(below, w, 0.0)
                    ws.append(w.astype(BF16))
                    carry[h] = carry[h] + tot
                acc = acc + _dot(jnp.concatenate(ws, axis=1),
                                 vd_s[pl.ds(pl.multiple_of(kb * 2 * BLK, 2 * BLK), 2 * BLK), :])
            return tuple(carry), acc

        def q_blocks(top, _):
            for r in range(KEY_GROUP):
                rows = pl.ds(pl.multiple_of((top * KEY_GROUP + r) * BLK, BLK), BLK)
                qc = [qc_s[h, rows, :] for h in heads]
                zero = jnp.zeros((BLK, BLK), F32)
                carry, acc = group(qc, top, (zero,) * HEADS_PER_STEP, jnp.zeros((BLK, LANES), F32), r + 1, True)
                carry, acc = lax.fori_loop(
                    0, top, lambda t, c: group(qc, top - 1 - t, c[0], c[1], KEY_GROUP, False), (carry, acc))
                o_ref[rows, :] = acc.astype(BF16)
                t_ref[rows, :] = jnp.where(first, carry[0], carry[1])
            return 0

        lax.fori_loop(0, n_blk // KEY_GROUP, q_blocks, 0)
        pl.when(step == pairs - 1)(finish)

    col = lambda off: pl.BlockSpec((s, LANES), lambda p: (0, off + p))
    vec = pl.BlockSpec((1, LANES), lambda p: (0, 0))
    out = pl.pallas_call(
        body, name=name, grid=(pairs,),
        in_specs=[col(0), col(pairs), col(2 * pairs), vec, vec] + [ANY_SPEC] * n_send,
        out_specs=[pl.BlockSpec((s, LANES), lambda p: (0, p))] * 2 + [ANY_SPEC] * n_send,
        out_shape=[jax.ShapeDtypeStruct((s, 2 * ATTN_WIDTH), BF16), jax.ShapeDtypeStruct((s, ATTN_WIDTH), F32)]
        + _gathered_shapes(sends),
        scratch_shapes=[pltpu.VMEM((HEADS_PER_STEP, s, 4 * HEAD_DIM), BF16)] * 2
        + [pltpu.VMEM((HEADS_PER_STEP * s, LANES), BF16)] + _exchange_sems(n_send),
        compiler_params=_params(1),
    )(proj, proj, proj, qg, kg, *sends)
    return out[0], out[1], out[2:]


def _attn_bwd(proj, dcat, tsum, qg, kg, sends, name):
    s = proj.shape[0]
    n_blk = s // BLK
    pairs = ATTN_WIDTH // LANES
    scale = HEAD_DIM ** -0.5
    n_send = len(sends)
    n_scratch = 7

    def body(q_ref, k_ref, v_ref, do_ref, t_ref, qg_ref, kg_ref, *rest):
        send_refs, rest = rest[:n_send], rest[n_send:]
        (dq_ref, dk_ref, dv_ref, dqg_ref, dkg_ref), rest = rest[:5], rest[5:]
        got_refs, scratch, sems = rest[:n_send], rest[n_send:n_send + n_scratch], rest[n_send + n_scratch:]
        qc_s, kc_s, vd_s, kd_s, qd_s, dob_s, dkv_s = scratch
        copies = _scatter_copies(send_refs, got_refs, sems)

        @pl.when(pl.program_id(0) == 0)
        def _():
            for cp in copies:
                cp.start()

        tri_p = _tri("upto")
        tri_h = _tri("before")
        diff = _col_minus_row()

        @pl.when(pl.program_id(0) == 0)
        def _():
            dqg_ref[...] = jnp.zeros_like(dqg_ref)
            dkg_ref[...] = jnp.zeros_like(dkg_ref)

        first = _first_head_lanes()
        heads = range(HEADS_PER_STEP)
        _attn_prep(q_ref, k_ref, v_ref, qg_ref, kg_ref, qc_s, kc_s, vd_s, kd_s, n_blk)

        def prep(i, _):
            rows = pl.ds(pl.multiple_of(i * BLK, BLK), BLK)
            both = pl.ds(pl.multiple_of(i * 2 * BLK, 2 * BLK), 2 * BLK)
            dob = do_ref[rows, :].astype(BF16)
            dob_s[rows, :] = dob
            none = jnp.zeros((BLK, HEAD_DIM), BF16)
            for h in heads:
                qd_s[h, both, :] = jnp.concatenate(
                    [jnp.concatenate([qc_s[h, rows, 0:HEAD_DIM], none], axis=1),
                     jnp.concatenate([none, dob[:, h * HEAD_DIM:(h + 1) * HEAD_DIM]], axis=1)], axis=0)
                dkv_s[h, rows, :] = jnp.zeros((BLK, LANES), F32)
            return 0

        lax.fori_loop(0, n_blk, prep, 0)

        below = diff < 0

        def group(qc, qd, dob, tq, grp, pc, hc, dq, n_tiles, diagonal):
            blocks = [grp * KEY_GROUP + j for j in range(n_tiles)]
            cols_of = [pl.ds(pl.multiple_of(kb * BLK, BLK), BLK) for kb in blocks]
            both_of = [pl.ds(pl.multiple_of(kb * 2 * BLK, 2 * BLK), 2 * BLK) for kb in blocks]
            on_diagonal = [diagonal and j == n_tiles - 1 for j in range(n_tiles)]
            zs = [_pair_scores(qc[h], kc_s.at[h], grp, n_tiles) for h in heads]
            das = [_dot_nt(dob, vd_s[both, :]) for both in both_of]
            lbs = [[None] * n_tiles for _ in heads]
            scans = [[None] * n_tiles for _ in heads]
            for h in heads:
                for j, z in enumerate(zs[h]):
                    sp = _softplus(z)
                    lom = -sp
                    if on_diagonal[j]:
                        lom = jnp.where(below, lom, 0.0)
                    lbs[h][j] = z - sp
                    scans[h][j] = _scan_rows(lom, tri_p)
            pc, hc = list(pc), list(hc)
            avs = [[None] * n_tiles for _ in heads]
            gws = [[None] * n_tiles for _ in heads]
            hscans = [[None] * n_tiles for _ in heads]
            for h in heads:
                for j in range(n_tiles):
                    p_in, p_tot = scans[h][j]
                    a = jnp.exp(lbs[h][j] + (tq[h] - pc[h] - p_in))
                    if on_diagonal[j]:
                        a = jnp.where(below, a, 0.0)
                    pc[h] = pc[h] + p_tot
                    gw = das[j][:, h * BLK:(h + 1) * BLK] * a
                    avs[h][j] = a.astype(BF16)
                    gws[h][j] = gw
                    hscans[h][j] = _scan_rows(gw, tri_h)
            dzs = [[None] * n_tiles for _ in heads]
            for h in heads:
                for j in range(n_tiles):
                    h_in, g_tot = hscans[h][j]
                    gw = gws[h][j]
                    dz = gw - jnp.exp(lbs[h][j]) * (gw + hc[h] + h_in)
                    if on_diagonal[j]:
                        dz = jnp.where(below, dz, 0.0)
                    hc[h] = hc[h] + g_tot
                    dzs[h][j] = dz.astype(BF16)
            for j, both in enumerate(both_of):
                dq = dq + _dot(jnp.concatenate([dzs[h][j] for h in heads], axis=1), kd_s[both, :])
            for h in heads:
                for j, cols in enumerate(cols_of):
                    dkv_s[h, cols, :] += _dot_tn(jnp.concatenate([dzs[h][j], avs[h][j]], axis=0), qd[h])
            return tuple(pc), tuple(hc), dq

        def q_blocks(top, dqg):
            for r in range(KEY_GROUP):
                qi = top * KEY_GROUP + r
                rows = pl.ds(pl.multiple_of(qi * BLK, BLK), BLK)
                both = pl.ds(pl.multiple_of(qi * 2 * BLK, 2 * BLK), 2 * BLK)
                qc = [qc_s[h, rows, :] for h in heads]
                qd = [qd_s[h, both, :] for h in heads]
                dob = dob_s[rows, :]
                tboth = t_ref[rows, :]
                tq = [jnp.concatenate([tboth[:, h * HEAD_DIM:(h + 1) * HEAD_DIM]] * 2, axis=1) for h in heads]
                zero = (jnp.zeros((BLK, BLK), F32),) * HEADS_PER_STEP
                pc, hc, dq = lax.fori_loop(
                    0, top, lambda grp, c: group(qc, qd, dob, tq, grp, c[0], c[1], c[2], KEY_GROUP, False),
                    (zero, zero, jnp.zeros((BLK, LANES), F32)))
                _, _, dq = group(qc, qd, dob, tq, top, pc, hc, dq, r + 1, True)
                dq_raw, dg = _pair_norm_bwd(q_ref[rows, :], qg_ref[...], dq * scale, first)
                dq_ref[rows, :] = dq_raw.astype(BF16)
                dqg = dqg + dg
            return dqg

        dqg = lax.fori_loop(0, n_blk // KEY_GROUP, q_blocks, jnp.zeros((1, LANES), F32))

        def finish(i, dkg):
            rows = pl.ds(pl.multiple_of(i * BLK, BLK), BLK)
            dk = jnp.concatenate([dkv_s[h, rows, 0:HEAD_DIM] for h in heads], axis=1)
            dv = jnp.concatenate([dkv_s[h, rows, HEAD_DIM:2 * HEAD_DIM] for h in heads], axis=1)
            dk_raw, dg = _pair_norm_bwd(k_ref[rows, :], kg_ref[...], dk, first)
            dk_ref[rows, :] = dk_raw.astype(BF16)
            dv_ref[rows, :] = dv.astype(BF16)
            return dkg + dg

        dkg = lax.fori_loop(0, n_blk, finish, jnp.zeros((1, LANES), F32))
        dqg_ref[0:1, :] += dqg
        dkg_ref[0:1, :] += dkg

        @pl.when(pl.program_id(0) == pairs - 1)
        def _():
            for cp in copies:
                cp.wait()

    col = lambda off: pl.BlockSpec((s, LANES), lambda p: (0, off + p))
    vec = pl.BlockSpec((1, LANES), lambda p: (0, 0))
    small = pl.BlockSpec((8, LANES), lambda p: (0, 0))
    out = pl.pallas_call(
        body, name=name, grid=(pairs,),
        in_specs=[col(0), col(pairs), col(2 * pairs), col(0), col(0), vec, vec] + [ANY_SPEC] * n_send,
        out_specs=[col(0)] * 3 + [small] * 2 + [ANY_SPEC] * n_send,
        out_shape=[jax.ShapeDtypeStruct((s, ATTN_WIDTH), BF16)] * 3 + [jax.ShapeDtypeStruct((8, LANES), F32)] * 2
        + _scattered_shapes(sends),
        scratch_shapes=[pltpu.VMEM((HEADS_PER_STEP, s, 4 * HEAD_DIM), BF16)] * 2
        + [pltpu.VMEM((HEADS_PER_STEP * s, LANES), BF16)] * 2
        + [pltpu.VMEM((HEADS_PER_STEP, HEADS_PER_STEP * s, LANES), BF16), pltpu.VMEM((s, LANES), BF16),
           pltpu.VMEM((HEADS_PER_STEP, s, LANES), F32)] + _exchange_sems(n_send),
        compiler_params=_params(1),
    )(proj, proj, proj, dcat, tsum, qg, kg, *sends)
    return out[:5], out[5:]


CONV_ROWS = 128


def _shifted(window, shift, halo):
    if shift == 0:
        return window[halo:, :]
    return pltpu.roll(window, shift, 0)[halo:, :]


SUBLANES = 8


def _row_shifts(window, up):
    n = window.shape[0]
    return [window] + [pltpu.roll(window, n - b if up else b, 0) for b in range(1, SUBLANES)]


def _earlier(shifts, back, rows):
    a, b = divmod(back, SUBLANES)
    return shifts[b][CONV_HALO - SUBLANES * a:CONV_HALO - SUBLANES * a + rows, :]


def _later(shifts, ahead, rows):
    a, b = divmod(ahead, SUBLANES)
    return shifts[b][SUBLANES * a:SUBLANES * a + rows, :]


def _lane_blocks(width):
    return [slice(c, c + LANES) for c in range(0, width, LANES)]


def _conv_taps(shifts, w_ref, lanes, rows):
    y = None
    for k in range(CONV_KERNEL):
        term = _earlier(shifts, CONV_KERNEL - 1 - k, rows) * w_ref[k:k + 1, lanes]
        y = term if y is None else y + term
    return y


def _conv_fwd(proj, cat, w, b, lg, lb, name):
    s = proj.shape[0]
    cw = w.shape[1]
    rows = CONV_ROWS
    blk_a = (proj.shape[1] - 2 * cw) // cw

    def body(a_ref, g_ref, w_ref, b_ref, lg_ref, lb_ref, _cat_ref, o_ref, y_ref, u_s):
        u_s[0:CONV_HALO, :] = jnp.zeros((CONV_HALO, cw), F32)

        def glu(i, _):
            r0 = pl.multiple_of(i * rows, rows)
            u_s[pl.ds(CONV_HALO + r0, rows), :] = a_ref[pl.ds(r0, rows), :] * _sigmoid(g_ref[pl.ds(r0, rows), :])
            return 0

        lax.fori_loop(0, s // rows, glu, 0)

        def chunk(i, _):
            r0 = pl.multiple_of(i * rows, rows)
            for lanes in _lane_blocks(cw):
                shifts = _row_shifts(u_s[pl.ds(r0, CONV_HALO + rows), lanes], False)
                y_ref[pl.ds(r0, rows), lanes] = _conv_taps(shifts, w_ref, lanes, rows) + b_ref[:, lanes]
            y = y_ref[pl.ds(r0, rows), :]
            yc = y - jnp.mean(y, axis=-1, keepdims=True)
            n = yc * lax.rsqrt(jnp.mean(yc * yc, axis=-1, keepdims=True) + EPS)
            ln = n * lg_ref[...] + lb_ref[...]
            o_ref[pl.ds(r0, rows), :] = (ln * _sigmoid(ln)).astype(BF16)
            return 0

        lax.fori_loop(0, s // rows, chunk, 0)

    vec = pl.BlockSpec((1, cw), lambda i: (0, 0))
    mine = pl.BlockSpec((s, cw), lambda i: (0, cat.shape[1] // cw - 1))
    return pl.pallas_call(
        body, name=name, grid=(1,),
        in_specs=[pl.BlockSpec((s, cw), lambda i: (0, blk_a)), pl.BlockSpec((s, cw), lambda i: (0, blk_a + 1)),
                  pl.BlockSpec((CONV_KERNEL, cw), lambda i: (0, 0)), vec, vec, vec, mine],
        out_specs=[mine, pl.BlockSpec((s, cw), lambda i: (0, 0))],
        out_shape=[jax.ShapeDtypeStruct(cat.shape, BF16), jax.ShapeDtypeStruct((s, cw), F32)],
        scratch_shapes=[pltpu.VMEM((CONV_HALO + s, cw), F32)],
        input_output_aliases={6: 0},
        compiler_params=_params(1),
    )(proj, proj, w, b, lg, lb, cat)


def _conv_bwd(proj, y, dcat, w, lg, lb, sends, name):
    s = proj.shape[0]
    cw = w.shape[1]
    rows = CONV_ROWS
    blk_a = (proj.shape[1] - 2 * cw) // cw
    n_chunk = s // rows
    n_send = len(sends)

    def body(a_ref, g_ref, y_ref, dc_ref, w_ref, lg_ref, lb_ref, *rest):
        send_refs, (o_ref, dw_ref, db_ref, dlg_ref, dlb_ref), rest = rest[:n_send], rest[n_send:n_send + 5], rest[n_send + 5:]
        got_refs, (u_s, dy_s, dw_s), sems = rest[:n_send], rest[n_send:n_send + 3], rest[n_send + 3:]
        copies = _scatter_copies(send_refs, got_refs, sems)
        for cp in copies:
            cp.start()
        u_s[0:CONV_HALO, :] = jnp.zeros((CONV_HALO, cw), F32)
        dy_s[pl.ds(s, CONV_HALO), :] = jnp.zeros((CONV_HALO, cw), F32)
        dw_s[...] = jnp.zeros_like(dw_s)

        def glu(i, _):
            r0 = pl.multiple_of(i * rows, rows)
            u_s[pl.ds(CONV_HALO + r0, rows), :] = a_ref[pl.ds(r0, rows), :] * _sigmoid(g_ref[pl.ds(r0, rows), :])
            return 0

        lax.fori_loop(0, n_chunk, glu, 0)

        def chunk(i, carry):
            db, dlg, dlb = carry
            r0 = pl.multiple_of(i * rows, rows)
            y = y_ref[pl.ds(r0, rows), :]
            yc = y - jnp.mean(y, axis=-1, keepdims=True)
            r = lax.rsqrt(jnp.mean(yc * yc, axis=-1, keepdims=True) + EPS)
            n = yc * r
            ln = n * lg_ref[...] + lb_ref[...]
            sg = _sigmoid(ln)
            dln = dc_ref[pl.ds(r0, rows), :] * (sg * (1.0 + ln * (1.0 - sg)))
            dn = dln * lg_ref[...]
            dy = r * (dn - jnp.mean(dn, axis=-1, keepdims=True) - n * jnp.mean(dn * n, axis=-1, keepdims=True))
            dy_s[pl.ds(r0, rows), :] = dy
            for lanes in _lane_blocks(cw):
                shifts = _row_shifts(u_s[pl.ds(r0, CONV_HALO + rows), lanes], False)
                dy_part = dy[:, lanes]
                for k in range(CONV_KERNEL):
                    prod = _earlier(shifts, CONV_KERNEL - 1 - k, rows) * dy_part
                    dw_s[k, :, lanes] += jnp.sum(prod.reshape(rows // SUBLANES, SUBLANES, LANES), axis=0)
            return (db + jnp.sum(dy, axis=0, keepdims=True),
                    dlg + jnp.sum(dln * n, axis=0, keepdims=True),
                    dlb + jnp.sum(dln, axis=0, keepdims=True))

        zero = jnp.zeros((1, cw), F32)
        db, dlg, dlb = lax.fori_loop(0, n_chunk, chunk, (zero, zero, zero))
        db_ref[...] = db
        dlg_ref[...] = dlg
        dlb_ref[...] = dlb
        for k in range(CONV_KERNEL):
            dw_ref[k:k + 1, :] = jnp.sum(dw_s[k], axis=0, keepdims=True)

        def back(i, _):
            r0 = pl.multiple_of(i * rows, rows)
            for lanes in _lane_blocks(cw):
                shifts = _row_shifts(dy_s[pl.ds(r0, rows + CONV_HALO), lanes], True)
                du = None
                for k in range(CONV_KERNEL):
                    term = _later(shifts, CONV_KERNEL - 1 - k, rows) * w_ref[k:k + 1, lanes]
                    du = term if du is None else du + term
                av = a_ref[pl.ds(r0, rows), lanes]
                sg = _sigmoid(g_ref[pl.ds(r0, rows), lanes])
                o_ref[pl.ds(r0, rows), lanes] = (du * sg).astype(BF16)
                o_ref[pl.ds(r0, rows), slice(cw + lanes.start, cw + lanes.stop)] = (du * av * sg * (1.0 - sg)).astype(BF16)
            return 0

        lax.fori_loop(0, n_chunk, back, 0)
        for cp in copies:
            cp.wait()

    vec = pl.BlockSpec((1, cw), lambda i: (0, 0))
    wspec = pl.BlockSpec((CONV_KERNEL, cw), lambda i: (0, 0))
    out = pl.pallas_call(
        body, name=name, grid=(1,),
        in_specs=[pl.BlockSpec((s, cw), lambda i: (0, blk_a)), pl.BlockSpec((s, cw), lambda i: (0, blk_a + 1)),
                  pl.BlockSpec((s, cw), lambda i: (0, 0)), pl.BlockSpec((s, cw), lambda i: (0, 1)), wspec, vec, vec]
        + [ANY_SPEC] * n_send,
        out_specs=[pl.BlockSpec((s, 2 * cw), lambda i: (0, 0)), wspec, vec, vec, vec] + [ANY_SPEC] * n_send,
        out_shape=[jax.ShapeDtypeStruct((s, 2 * cw), BF16), jax.ShapeDtypeStruct((CONV_KERNEL, cw), F32)]
        + [jax.ShapeDtypeStruct((1, cw), F32)] * 3 + _scattered_shapes(sends),
        scratch_shapes=[pltpu.VMEM((CONV_HALO + s, cw), F32), pltpu.VMEM((s + CONV_HALO, cw), F32),
                        pltpu.VMEM((CONV_KERNEL, 8, cw), F32)] + _exchange_sems(n_send),
        compiler_params=_params(1),
    )(proj, proj, y, dcat, w, lg, lb, *sends)
    return out[:5], out[5:]


FFN_ROWS = 256


def _ffn_gate(g_ref, r0, rows, w_ref, b_ref):
    cur = g_ref[pl.ds(r0, rows), :].astype(F32)
    prev = g_ref[pl.ds(pl.multiple_of(jnp.maximum(r0 - FFN_HALO, 0), FFN_HALO), FFN_HALO), :].astype(F32)
    prev = jnp.where(r0 > 0, prev, 0.0)
    window = jnp.concatenate([prev, cur], axis=0)
    gc = cur * w_ref[FFN_KERNEL - 1:FFN_KERNEL, :] + b_ref[...]
    for k in range(FFN_KERNEL - 1):
        gc = gc + _shifted(window, FFN_KERNEL - 1 - k, FFN_HALO) * w_ref[k:k + 1, :]
    return gc, window


def _ffn_fwd(up, w, b, name):
    s = up.shape[0]
    f = w.shape[1]
    tc = _pick(f, (256, 128))
    nc = f // tc
    rows = _pick(s, (FFN_ROWS, 128))

    def body(g_ref, v_ref, w_ref, b_ref, o_ref):
        def chunk(i, _):
            r0 = pl.multiple_of(i * rows, rows)
            gc, _w = _ffn_gate(g_ref, r0, rows, w_ref, b_ref)
            o_ref[pl.ds(r0, rows), :] = (gc * _sigmoid(gc) * v_ref[pl.ds(r0, rows), :].astype(F32)).astype(BF16)
            return 0

        lax.fori_loop(0, s // rows, chunk, 0)

    return pl.pallas_call(
        body, name=name, grid=(nc,),
        in_specs=[pl.BlockSpec((s, tc), lambda j: (0, j)), pl.BlockSpec((s, tc), lambda j: (0, nc + j)),
                  pl.BlockSpec((FFN_KERNEL, tc), lambda j: (0, j)), pl.BlockSpec((1, tc), lambda j: (0, j))],
        out_specs=pl.BlockSpec((s, tc), lambda j: (0, j)),
        out_shape=jax.ShapeDtypeStruct((s, f), BF16),
        compiler_params=_params(1),
    )(up, up, w, b)


def _ffn_bwd(up, dact, w, b, name):
    s = up.shape[0]
    f = w.shape[1]
    tc = _pick(f, (256, 128))
    nc = f // tc
    rows = _pick(s, (FFN_ROWS, 128))
    n_chunk = s // rows

    def body(g_ref, v_ref, da_ref, w_ref, b_ref, d_ref, dw_ref, db_ref, dgc_s):
        dg_ref, dv_ref = d_ref.at[0], d_ref.at[1]
        dgc_s[pl.ds(s, FFN_HALO), :] = jnp.zeros((FFN_HALO, tc), F32)

        def chunk(i, carry):
            r0 = pl.multiple_of(i * rows, rows)
            gc, window = _ffn_gate(g_ref, r0, rows, w_ref, b_ref)
            sg = _sigmoid(gc)
            da = da_ref[pl.ds(r0, rows), :].astype(F32)
            dv_ref[pl.ds(r0, rows), :] = (da * gc * sg).astype(BF16)
            dgc = da * v_ref[pl.ds(r0, rows), :].astype(F32) * (sg * (1.0 + gc * (1.0 - sg)))
            dgc_s[pl.ds(r0, rows), :] = dgc
            out = [carry[0] + jnp.sum(dgc, axis=0, keepdims=True)]
            for k in range(FFN_KERNEL):
                out.append(carry[1 + k] + jnp.sum(_shifted(window, FFN_KERNEL - 1 - k, FFN_HALO) * dgc,
                                                  axis=0, keepdims=True))
            return tuple(out)

        zero = jnp.zeros((1, tc), F32)
        sums = lax.fori_loop(0, n_chunk, chunk, (zero,) * (1 + FFN_KERNEL))
        db_ref[...] = sums[0]
        for k in range(FFN_KERNEL):
            dw_ref[k:k + 1, :] = sums[1 + k]

        def back(i, _):
            r0 = pl.multiple_of(i * rows, rows)
            window = dgc_s[pl.ds(r0, rows + FFN_HALO), :]
            dg = window[:rows, :] * w_ref[FFN_KERNEL - 1:FFN_KERNEL, :]
            for k in range(FFN_KERNEL - 1):
                sh = FFN_KERNEL - 1 - k
                dg = dg + pltpu.roll(window, rows + FFN_HALO - sh, 0)[:rows, :] * w_ref[k:k + 1, :]
            dg_ref[pl.ds(r0, rows), :] = dg.astype(BF16)
            return 0

        lax.fori_loop(0, n_chunk, back, 0)

    blk = lambda off: pl.BlockSpec((s, tc), lambda j: (0, off + j))
    return pl.pallas_call(
        body, name=name, grid=(nc,),
        in_specs=[blk(0), blk(nc), blk(0), pl.BlockSpec((FFN_KERNEL, tc), lambda j: (0, j)),
                  pl.BlockSpec((1, tc), lambda j: (0, j))],
        out_specs=[pl.BlockSpec((2, s, tc), lambda j: (0, 0, j)), pl.BlockSpec((FFN_KERNEL, tc), lambda j: (0, j)),
                   pl.BlockSpec((1, tc), lambda j: (0, j))],
        out_shape=[jax.ShapeDtypeStruct((2, s, f), BF16),
                   jax.ShapeDtypeStruct((FFN_KERNEL, f), F32), jax.ShapeDtypeStruct((1, f), F32)],
        scratch_shapes=[pltpu.VMEM((s + FFN_HALO, tc), F32)],
        compiler_params=_params(1),
    )(up, up, dact, w, b)


def _loss_head(y, target, name):
    m, n = y.shape
    tm = _pick(m, (256, 128))

    def body(y_ref, t_ref, l_ref, d_ref, db_ref):
        e = y_ref[...] - t_ref[...]
        part = 0.5 * jnp.sum(jnp.sum(e * e, axis=-1, keepdims=True) / n, axis=0, keepdims=True)

        @pl.when(pl.program_id(0) == 0)
        def _():
            l_ref[...] = jnp.zeros_like(l_ref)

        l_ref[...] += part
        d = e / n
        d_ref[...] = d
        db_ref[...] = d.astype(BF16)

    return pl.pallas_call(
        body, name=name, grid=(m // tm,),
        in_specs=[pl.BlockSpec((tm, n), lambda i: (i, 0))] * 2,
        out_specs=[pl.BlockSpec((8, LANES), lambda i: (0, 0)), pl.BlockSpec((tm, n), lambda i: (i, 0)),
                   pl.BlockSpec((tm, n), lambda i: (i, 0))],
        out_shape=[jax.ShapeDtypeStruct((8, LANES), F32), jax.ShapeDtypeStruct((m, n), F32),
                   jax.ShapeDtypeStruct((m, n), BF16)],
        compiler_params=_params(1),
    )(y, target)


def _adamw_math(w, g, m, v):
    m = ADAM_B1 * m + (1.0 - ADAM_B1) * g
    v = ADAM_B2 * v + (1.0 - ADAM_B2) * (g * g)
    m_hat = m / (1.0 - ADAM_B1 ** ADAM_STEP)
    v_hat = v / (1.0 - ADAM_B2 ** ADAM_STEP)
    delta = -ADAM_LR * (m_hat / (jnp.sqrt(v_hat) + ADAM_EPS) + ADAM_WD * w)
    return delta, m, v


def _sum_adamw(parts, w, m, v, sends, name):
    depth, r, c = w.shape
    parts = [list(p) if isinstance(p, (list, tuple)) else [p] for p in parts]
    tr = max(t for t in range(16, min(min(a.shape[1] for p in parts for a in p), 192) + 1, 16)
             if all(a.shape[1] % t == 0 for p in parts for a in p))
    steps = r // tr
    n_send = len(sends)
    chunks = []
    for layer, p in enumerate(parts):
        s0 = 0
        for a in p:
            chunks.append((layer, s0, a.shape[1] // tr))
            s0 += a.shape[1] // tr
    n_chunk = len(chunks)

    def body(*refs):
        p_refs, refs = refs[:n_chunk], refs[n_chunk:]
        (w_ref, m_ref, v_ref), send_refs, refs = refs[:3], refs[3:3 + n_send], refs[3 + n_send:]
        (g_out, d_out, m_out, v_out), got_refs, sems = refs[:4], refs[4:4 + n_send], refs[4 + n_send:]
        is_first = (pl.program_id(0) == 0) & (pl.program_id(1) == 0)
        is_last = (pl.program_id(0) == depth - 1) & (pl.program_id(1) == steps - 1)
        if n_send:
            start, relay, finish = _two_level_gather(send_refs, got_refs, sems)
            pl.when(is_first)(start)
            pl.when(is_last)(relay)
        for p_ref, (layer, s0, n) in zip(p_refs, chunks):
            step = pl.program_id(1)

            @pl.when((pl.program_id(0) == layer) & (step >= s0) & (step < s0 + n))
            def _():
                g = p_ref[0].astype(F32)
                for src in range(1, N_DEV):
                    g = g + p_ref[src].astype(F32)
                d, mn, vn = _adamw_math(w_ref[0], g, m_ref[0], v_ref[0])
                g_out[0] = g
                d_out[0] = d
                m_out[0] = mn
                v_out[0] = vn

        if n_send:
            pl.when(is_last)(finish)

    def part_spec(layer, s0, n):
        return pl.BlockSpec((N_DEV, tr, c), lambda l, i: (0, jnp.clip((l - layer) * steps + i - s0, 0, n - 1), 0))

    blk = pl.BlockSpec((1, tr, c), lambda l, i: (l, i, 0))
    out = pl.pallas_call(
        body, name=name, grid=(depth, steps),
        in_specs=[part_spec(*ch) for ch in chunks] + [blk, blk, blk] + [ANY_SPEC] * n_send,
        out_specs=[blk] * 4 + [ANY_SPEC] * n_send,
        out_shape=[jax.ShapeDtypeStruct(w.shape, F32)] * 4 + _gathered_shapes(sends),
        scratch_shapes=_exchange_sems(n_send) if n_send else [],
        compiler_params=_params(2),
    )(*[a for p in parts for a in p], w, m, v, *sends)
    return out[:4], out[4:]


VMEM_SPEC = pl.BlockSpec(memory_space=pltpu.VMEM)


def _sum_small(parts, name):
    n = len(parts)

    def body(*refs):
        for p_ref, o_ref in zip(refs[:n], refs[n:]):
            g = p_ref[0]
            for src in range(1, N_DEV):
                g = g + p_ref[src]
            o_ref[...] = g

    return pl.pallas_call(
        body, name=name, in_specs=[VMEM_SPEC] * n, out_specs=[VMEM_SPEC] * n,
        out_shape=[jax.ShapeDtypeStruct(p.shape[1:], F32) for p in parts],
        compiler_params=_params(),
    )(*parts)


def _adamw_small(ws, gs, ms, vs, name):
    n = len(ws)

    def body(*refs):
        ins, outs = refs[:4 * n], refs[4 * n:]
        for i in range(n):
            d, mn, vn = _adamw_math(ins[i][...], ins[n + i][...], ins[2 * n + i][...], ins[3 * n + i][...])
            outs[i][...] = d
            outs[n + i][...] = mn
            outs[2 * n + i][...] = vn

    out = pl.pallas_call(
        body, name=name, in_specs=[VMEM_SPEC] * (4 * n), out_specs=[VMEM_SPEC] * (3 * n),
        out_shape=[jax.ShapeDtypeStruct(w.shape, F32) for w in ws] * 3,
        compiler_params=_params(),
    )(*ws, *gs, *ms, *vs)
    return out[:n], out[n:2 * n], out[2 * n:]


def kernel(x, norm1_g, w_in, q_norm_g, k_norm_g, conv_dw_w, conv_dw_b, conv_ln_g, conv_ln_b, w_out, norm2_g, w_up, ffn_dw_w, ffn_dw_b, w_down, loss_target, m_norm1_g, m_w_in, m_q_norm_g, m_k_norm_g, m_conv_dw_w, m_conv_dw_b, m_conv_ln_g, m_conv_ln_b, m_w_out, m_norm2_g, m_w_up, m_ffn_dw_w, m_ffn_dw_b, m_w_down, v_norm1_g, v_w_in, v_q_norm_g, v_k_norm_g, v_conv_dw_w, v_conv_dw_b, v_conv_ln_g, v_conv_ln_b, v_w_out, v_norm2_g, v_w_up, v_ffn_dw_w, v_ffn_dw_b, v_w_down):
    depth = w_in.shape[0]
    d_ff = w_down.shape[1] * N_DEV
    conv_w = conv_dw_b.shape[1]
    cw_shard = conv_dw_w.shape[2]
    fw_shard = ffn_dw_w.shape[2]
    me = 4 * lax.axis_index("x") + 2 * lax.axis_index("y") + lax.axis_index("c")

    transposed = lambda a: a.transpose(0, 2, 1)
    b_in, b_out, b_up, b_down = (transposed(w_in).astype(BF16), w_out.astype(BF16), transposed(w_up).astype(BF16),
                                 w_down.astype(BF16))
    rows_major = lambda g: g.reshape(N_DEV * g.shape[1], g.shape[2])
    g_in0, g_cw, g_fw = _gather([b_in[0], conv_dw_w, ffn_dw_w], name="gather_first")
    wf_in, wf_out, wf_up, wf_down = [rows_major(g_in0)] + [None] * (depth - 1), [None] * depth, [None] * depth, [None] * depth
    cwf = g_cw.transpose(1, 2, 0, 3).reshape(depth, CONV_KERNEL, conv_w)
    fwf = g_fw.transpose(1, 2, 0, 3).reshape(depth, FFN_KERNEL, d_ff)

    row = lambda a, l: a[l].reshape(1, -1)
    both_heads = lambda a, l: jnp.tile(row(a, l), (1, HEADS_PER_STEP))

    xs = x[0]
    saved = []
    for l in range(depth):
        proj, h1, _ = _mm_rms(xs, row(norm1_g, l), wf_in[l], F32, [], name="fwd_in")
        sends = [b_out[l], b_up[l]] + ([b_in[l + 1]] if l + 1 < depth else [])
        attn, tsum, got = _attn_fwd(proj, both_heads(q_norm_g, l), both_heads(k_norm_g, l), sends, name="fwd_attn")
        wf_out[l], wf_up[l] = rows_major(got[0]), rows_major(got[1])
        if l + 1 < depth:
            wf_in[l + 1] = rows_major(got[2])
        cat, conv_y = _conv_fwd(proj, attn, cwf[l], row(conv_dw_b, l), row(conv_ln_g, l), row(conv_ln_b, l),
                                name="fwd_conv")
        x_mid = _mm_res(cat, wf_out[l], xs, name="fwd_out")
        up, h2, got = _mm_rms(x_mid, row(norm2_g, l), wf_up[l], BF16, [b_down[l]], name="fwd_up")
        wf_down[l] = rows_major(got[0])
        act = _ffn_fwd(up, fwf[l], row(ffn_dw_b, l), name="fwd_ffn")
        x_next = _mm_res(act, wf_down[l], x_mid, name="fwd_down")
        saved.append((xs, h1, proj, tsum, cat, x_mid, h2, up, act, conv_y))
        xs = x_next

    loss_tile, dx, dxb = _loss_head(xs, loss_target[0], name="loss_head")
    loss = lax.psum(loss_tile[0, 0], ("x", "y", "c"))

    r_in, r_out, r_up, r_down = [None] * depth, [None] * depth, [None] * depth, [None] * depth
    row_blocks = lambda g: g.reshape(N_DEV, g.shape[0] // N_DEV, g.shape[1])
    small = {k: [None] * depth for k in ("norm1_g", "q_norm_g", "k_norm_g", "conv_dw_w", "conv_dw_b", "conv_ln_g",
                                         "conv_ln_b", "norm2_g", "ffn_dw_w", "ffn_dw_b")}
    gw_in = None
    for l in reversed(range(depth)):
        xs, h1, proj, tsum, cat, x_mid, h2, up, act, conv_y = saved[l]
        dact = _mm_nt(dxb, wf_down[l], BF16, name="bwd_dact")
        gw_down = _mm_tn(act, dxb, name="bwd_gw_down")
        dup, small["ffn_dw_w"][l], small["ffn_dw_b"][l] = _ffn_bwd(up, dact, fwf[l], row(ffn_dw_b, l), name="bwd_ffn")
        gw_up = _mm_tn(dup, h2, name="bwd_gw_up")
        if l + 1 < depth:
            blocks = row_blocks(gw_in)
            half = blocks.shape[1] // 2
            in_halves = [blocks[:, :half], blocks[:, half:]]
        dx, dxb, small["norm2_g"][l], got_a = _mm_rmsbwd(dup, wf_up[l], x_mid, row(norm2_g, l), dx,
                                                         in_halves[:1] if l + 1 < depth else [], name="bwd_up")
        dcat = _mm_nt(dxb, wf_out[l], F32, name="bwd_dcat")
        gw_out = _mm_tn(cat, dxb, name="bwd_gw_out")
        early = in_halves[1] if l + 1 < depth else row_blocks(gw_out)
        (dglu, small["conv_dw_w"][l], small["conv_dw_b"][l], small["conv_ln_g"][l], small["conv_ln_b"][l]), got = (
            _conv_bwd(proj, conv_y, dcat, cwf[l], row(conv_ln_g, l), row(conv_ln_b, l), [early], name="bwd_conv"))
        sends = [row_blocks(gw_down), row_blocks(gw_up)] + ([row_blocks(gw_out)] if l + 1 < depth else [])
        (dq, dk, dv, dqg, dkg), got2 = _attn_bwd(proj, dcat, tsum, both_heads(q_norm_g, l), both_heads(k_norm_g, l),
                                                 sends, name="bwd_attn")
        r_down[l], r_up[l] = got2[:2]
        if l + 1 < depth:
            r_in[l + 1], r_out[l] = [got_a[0], got[0]], got2[2]
        else:
            r_out[l] = got[0]
        small["q_norm_g"][l] = dqg[0:1, :HEAD_DIM] + dqg[0:1, HEAD_DIM:]
        small["k_norm_g"][l] = dkg[0:1, :HEAD_DIM] + dkg[0:1, HEAD_DIM:]
        dproj = jnp.concatenate([dq, dk, dv, dglu], axis=1)
        gw_in = _mm_tn(dproj, h1, name="bwd_gw_in")
        sends = [row_blocks(gw_in)] if l == 0 else []
        dx, dxb, small["norm1_g"][l], got = _mm_rmsbwd(dproj, wf_in[l], xs, row(norm1_g, l), dx, sends, name="bwd_in")
        if l == 0:
            r_in[0] = got[0]
    grad_x = dx[None]

    names = ["norm1_g", "q_norm_g", "k_norm_g", "conv_dw_w", "conv_dw_b", "conv_ln_g", "conv_ln_b", "norm2_g",
             "ffn_dw_w", "ffn_dw_b"]
    full_shapes = {"norm1_g": norm1_g.shape, "q_norm_g": q_norm_g.shape, "k_norm_g": k_norm_g.shape,
                   "conv_dw_w": (depth, CONV_KERNEL, conv_w), "conv_dw_b": conv_dw_b.shape,
                   "conv_ln_g": conv_ln_g.shape, "conv_ln_b": conv_ln_b.shape, "norm2_g": norm2_g.shape,
                   "ffn_dw_w": (depth, FFN_KERNEL, d_ff), "ffn_dw_b": ffn_dw_b.shape}
    partial = [jnp.stack(small[k]).reshape(full_shapes[k]) for k in names]
    big = {}
    big["w_out"], all_partials = _sum_adamw(r_out, w_out, m_w_out, v_w_out, partial, name="adamw_out")
    big["w_up"], _ = _sum_adamw(r_up, transposed(w_up), transposed(m_w_up), transposed(v_w_up), [], name="adamw_up")
    big["w_down"], _ = _sum_adamw(r_down, w_down, m_w_down, v_w_down, [], name="adamw_down")
    big["w_in"], _ = _sum_adamw(r_in, transposed(w_in), transposed(m_w_in), transposed(v_w_in), [], name="adamw_in")
    for k in ("w_in", "w_up"):
        big[k] = [transposed(a) for a in big[k]]

    grads = dict(zip(names, _sum_small(all_partials, name="sum_small_grads")))
    grads["conv_dw_w"] = lax.dynamic_slice_in_dim(grads["conv_dw_w"], me * cw_shard, cw_shard, axis=2)
    grads["ffn_dw_w"] = lax.dynamic_slice_in_dim(grads["ffn_dw_w"], me * fw_shard, fw_shard, axis=2)
    weights = dict(norm1_g=norm1_g, q_norm_g=q_norm_g, k_norm_g=k_norm_g, conv_dw_w=conv_dw_w, conv_dw_b=conv_dw_b,
                   conv_ln_g=conv_ln_g, conv_ln_b=conv_ln_b, norm2_g=norm2_g, ffn_dw_w=ffn_dw_w, ffn_dw_b=ffn_dw_b)
    m_in = dict(norm1_g=m_norm1_g, q_norm_g=m_q_norm_g, k_norm_g=m_k_norm_g, conv_dw_w=m_conv_dw_w,
                conv_dw_b=m_conv_dw_b, conv_ln_g=m_conv_ln_g, conv_ln_b=m_conv_ln_b, norm2_g=m_norm2_g,
                ffn_dw_w=m_ffn_dw_w, ffn_dw_b=m_ffn_dw_b)
    v_in = dict(norm1_g=v_norm1_g, q_norm_g=v_q_norm_g, k_norm_g=v_k_norm_g, conv_dw_w=v_conv_dw_w,
                conv_dw_b=v_conv_dw_b, conv_ln_g=v_conv_ln_g, conv_ln_b=v_conv_ln_b, norm2_g=v_norm2_g,
                ffn_dw_w=v_ffn_dw_w, ffn_dw_b=v_ffn_dw_b)
    d_s, m_s, v_s = _adamw_small([weights[k] for k in names], [grads[k] for k in names], [m_in[k] for k in names],
                                 [v_in[k] for k in names], name="adamw_small")
    delta, new_m, new_v = dict(zip(names, d_s)), dict(zip(names, m_s)), dict(zip(names, v_s))
    for k, (g, d, mn, vn) in big.items():
        grads[k], delta[k], new_m[k], new_v[k] = g, d, mn, vn

    order = ["norm1_g", "w_in", "q_norm_g", "k_norm_g", "conv_dw_w", "conv_dw_b", "conv_ln_g", "conv_ln_b", "w_out",
             "norm2_g", "w_up", "ffn_dw_w", "ffn_dw_b", "w_down"]
    return (loss, grad_x, *[grads[k] for k in order], *[delta[k] for k in order], *[new_m[k] for k in order],
            *[new_v[k] for k in order])
```
